```python
import jax, jax.numpy as jnp
from jax import lax
import numpy as np

D_MODEL = 1024
BATCH = 8
SEQ = 4096
DEPTH = 1

CHUNK = 64
N_LEFT_CHUNKS = 8
BAND = (N_LEFT_CHUNKS + 1) * CHUNK
ATT_HEADS = 8
HEAD_DIM = 64
D_ATT = ATT_HEADS * HEAD_DIM
REL_CLIP = 256
N_REL = 2 * REL_CLIP + 1
SGU_BLOCK = 128
SGU_GROUPS = 8
SGU_GROUP_DIM = 64
D_SGU = SGU_GROUPS * SGU_GROUP_DIM
D_FF = 2816
D_IN = 3 * D_ATT + 2 * D_SGU + 2 * D_MODEL
EPS = 1e-6
NEG_INF = -1e30

kernel_name = "macaron_gated_chunkattn_gmlp_block"


def rmsnorm(x, g):
    xf = x.astype(jnp.float32)
    y = xf * lax.rsqrt(jnp.mean(xf * xf, axis=-1, keepdims=True) + EPS)
    return (y * g.astype(jnp.float32)).astype(x.dtype)


def layernorm(x, g, b):
    xf = x.astype(jnp.float32)
    mu = jnp.mean(xf, axis=-1, keepdims=True)
    var = jnp.mean(jnp.square(xf - mu), axis=-1, keepdims=True)
    y = (xf - mu) * lax.rsqrt(var + EPS)
    return (y * g.astype(jnp.float32) + b.astype(jnp.float32)).astype(x.dtype)


def swiglu(h, w_gate, w_up, w_down):
    return (jax.nn.silu(h @ w_gate) * (h @ w_up)) @ w_down


def chunked_rel_attention(q, k, v, rel_table):
    B, S = q.shape[0], q.shape[1]
    n_c = S // CHUNK
    pad = N_LEFT_CHUNKS * CHUNK
    qc = q.reshape(B, n_c, CHUNK, ATT_HEADS, HEAD_DIM)
    kp = jnp.pad(k, ((0, 0), (pad, 0), (0, 0), (0, 0))).reshape(B, n_c + N_LEFT_CHUNKS, CHUNK, ATT_HEADS, HEAD_DIM)
    vp = jnp.pad(v, ((0, 0), (pad, 0), (0, 0), (0, 0))).reshape(B, n_c + N_LEFT_CHUNKS, CHUNK, ATT_HEADS, HEAD_DIM)
    k_band = jnp.concatenate([kp[:, i:i + n_c] for i in range(N_LEFT_CHUNKS + 1)], axis=2)
    v_band = jnp.concatenate([vp[:, i:i + n_c] for i in range(N_LEFT_CHUNKS + 1)], axis=2)
    scores = jnp.einsum('bcqhd,bckhd->bchqk', qc, k_band).astype(jnp.float32) * (HEAD_DIM ** -0.5)
    qi = jnp.arange(CHUNK)[:, None]
    kj = jnp.arange(BAND)[None, :]
    rel = jnp.clip(qi + pad - kj, -REL_CLIP, REL_CLIP) + REL_CLIP
    bias = rel_table.astype(jnp.float32)[:, rel]
    key_pos = jnp.arange(n_c)[:, None] * CHUNK + jnp.arange(BAND)[None, :] - pad
    valid = key_pos >= 0
    scores = jnp.where(valid[None, :, None, None, :], scores + bias[None, None], NEG_INF)
    probs = jax.nn.softmax(scores, axis=-1).astype(v.dtype)
    out = jnp.einsum('bchqk,bckhd->bcqhd', probs, v_band)
    return out.reshape(B, S, D_ATT)


def spatial_gating(z, ln_g, ln_b, w_s, b_s):
    B, S = z.shape[0], z.shape[1]
    n_blk = S // SGU_BLOCK
    u, vs = z[..., :D_SGU], z[..., D_SGU:]
    vs = layernorm(vs, ln_g, ln_b).reshape(B, n_blk, SGU_BLOCK, SGU_GROUPS, SGU_GROUP_DIM)
    pos = jnp.arange(SGU_BLOCK)
    mask = (pos[:, None] // CHUNK) >= (pos[None, :] // CHUNK)
    w_m = jnp.where(mask[None], w_s, jnp.zeros_like(w_s))
    s = jnp.einsum('gij,bnjgd->bnigd', w_m, vs) + b_s.T[None, None, :, :, None]
    return u * s.reshape(B, S, D_SGU)


def _fwd_setup_inputs(seed: int = 0) -> dict:
    key = jax.random.key(seed)
    ks = jax.random.split(key, 24)
    L = DEPTH
    f32 = jnp.float32

    def nrm(k, shape, scale):
        return jax.random.normal(k, shape, f32) * scale

    def gain(k, shape):
        return 1.0 + 0.05 * jax.random.normal(k, shape, f32)

    return {
        "x": jax.random.normal(ks[0], (BATCH, SEQ, D_MODEL), f32),
        "norm_ffn1": gain(ks[1], (L, D_MODEL)),
        "ffn1_w_gate": nrm(ks[2], (L, D_MODEL, D_FF), D_MODEL ** -0.5),
        "ffn1_w_up": nrm(ks[3], (L, D_MODEL, D_FF), D_MODEL ** -0.5),
        "ffn1_w_down": nrm(ks[4], (L, D_FF, D_MODEL), D_FF ** -0.5),
        "norm_mix": gain(ks[5], (L, D_MODEL)),
        "w_in": nrm(ks[6], (L, D_MODEL, D_IN), D_MODEL ** -0.5),
        "b_gate": nrm(ks[7], (L, 2 * D_MODEL), 0.01),
        "rel_bias": nrm(ks[8], (L, ATT_HEADS, N_REL), 0.1),
        "sgu_ln_g": gain(ks[9], (L, D_SGU)),
        "sgu_ln_b": nrm(ks[10], (L, D_SGU), 0.01),
        "sgu_w_s": nrm(ks[11], (L, SGU_GROUPS, SGU_BLOCK, SGU_BLOCK), SGU_BLOCK ** -0.5),
        "sgu_b_s": gain(ks[12], (L, SGU_GROUPS, SGU_BLOCK)),
        "w_branch_att": nrm(ks[13], (L, D_ATT, D_MODEL), D_ATT ** -0.5),
        "w_branch_sgu": nrm(ks[14], (L, D_SGU, D_MODEL), D_SGU ** -0.5),
        "w_out": nrm(ks[15], (L, D_MODEL, D_MODEL), D_MODEL ** -0.5),
        "norm_ffn2": gain(ks[16], (L, D_MODEL)),
        "ffn2_w_gate": nrm(ks[17], (L, D_MODEL, D_FF), D_MODEL ** -0.5),
        "ffn2_w_up": nrm(ks[18], (L, D_MODEL, D_FF), D_MODEL ** -0.5),
        "ffn2_w_down": nrm(ks[19], (L, D_FF, D_MODEL), D_FF ** -0.5),
        "norm_final": gain(ks[20], (D_MODEL,)),
    }


def _fwd_reference(x, norm_ffn1, ffn1_w_gate, ffn1_w_up, ffn1_w_down, norm_mix, w_in, b_gate,
              rel_bias, sgu_ln_g, sgu_ln_b, sgu_w_s, sgu_b_s, w_branch_att, w_branch_sgu,
              w_out, norm_ffn2, ffn2_w_gate, ffn2_w_up, ffn2_w_down, norm_final):
    B, S = x.shape[0], x.shape[1]
    for l in range(DEPTH):
        x = x + 0.5 * swiglu(rmsnorm(x, norm_ffn1[l]), ffn1_w_gate[l], ffn1_w_up[l], ffn1_w_down[l])
        h = rmsnorm(x, norm_mix[l])
        z = h @ w_in[l]
        o = 0
        q = z[..., o:o + D_ATT].reshape(B, S, ATT_HEADS, HEAD_DIM); o += D_ATT
        k = z[..., o:o + D_ATT].reshape(B, S, ATT_HEADS, HEAD_DIM); o += D_ATT
        v = z[..., o:o + D_ATT].reshape(B, S, ATT_HEADS, HEAD_DIM); o += D_ATT
        z_sgu = jax.nn.gelu(z[..., o:o + 2 * D_SGU]); o += 2 * D_SGU
        g = jax.nn.sigmoid(z[..., o:o + 2 * D_MODEL] + b_gate[l])
        g_att, g_sgu = g[..., :D_MODEL], g[..., D_MODEL:]
        y_att = chunked_rel_attention(q, k, v, rel_bias[l])
        y_sgu = spatial_gating(z_sgu, sgu_ln_g[l], sgu_ln_b[l], sgu_w_s[l], sgu_b_s[l])
        merged = g_att * (y_att @ w_branch_att[l]) + g_sgu * (y_sgu @ w_branch_sgu[l])
        x = x + merged @ w_out[l]
        x = x + 0.5 * swiglu(rmsnorm(x, norm_ffn2[l]), ffn2_w_gate[l], ffn2_w_up[l], ffn2_w_down[l])
    return rmsnorm(x, norm_final)


import jax as _jax
import jax.numpy as _jnp

TWIN_FORMAT = 'train_step'
FWD_PARAMS = ['x', 'norm_ffn1', 'ffn1_w_gate', 'ffn1_w_up', 'ffn1_w_down', 'norm_mix', 'w_in', 'b_gate', 'rel_bias', 'sgu_ln_g', 'sgu_ln_b', 'sgu_w_s', 'sgu_b_s', 'w_branch_att', 'w_branch_sgu', 'w_out', 'norm_ffn2', 'ffn2_w_gate', 'ffn2_w_up', 'ffn2_w_down', 'norm_final']
TWIN_WEIGHTS = ['norm_ffn1', 'ffn1_w_gate', 'ffn1_w_up', 'ffn1_w_down', 'norm_mix', 'w_in', 'b_gate', 'rel_bias', 'sgu_ln_g', 'sgu_ln_b', 'sgu_w_s', 'sgu_b_s', 'w_branch_att', 'w_branch_sgu', 'w_out', 'norm_ffn2', 'ffn2_w_gate', 'ffn2_w_up', 'ffn2_w_down', 'norm_final']
TWIN_DIFF_INPUT = 'x'
TWIN_INPUTS = ['x', 'norm_ffn1', 'ffn1_w_gate', 'ffn1_w_up', 'ffn1_w_down', 'norm_mix', 'w_in', 'b_gate', 'rel_bias', 'sgu_ln_g', 'sgu_ln_b', 'sgu_w_s', 'sgu_b_s', 'w_branch_att', 'w_branch_sgu', 'w_out', 'norm_ffn2', 'ffn2_w_gate', 'ffn2_w_up', 'ffn2_w_down', 'norm_final', 'loss_target', 'm_norm_ffn1', 'm_ffn1_w_gate', 'm_ffn1_w_up', 'm_ffn1_w_down', 'm_norm_mix', 'm_w_in', 'm_b_gate', 'm_rel_bias', 'm_sgu_ln_g', 'm_sgu_ln_b', 'm_sgu_w_s', 'm_sgu_b_s', 'm_w_branch_att', 'm_w_branch_sgu', 'm_w_out', 'm_norm_ffn2', 'm_ffn2_w_gate', 'm_ffn2_w_up', 'm_ffn2_w_down', 'm_norm_final', 'v_norm_ffn1', 'v_ffn1_w_gate', 'v_ffn1_w_up', 'v_ffn1_w_down', 'v_norm_mix', 'v_w_in', 'v_b_gate', 'v_rel_bias', 'v_sgu_ln_g', 'v_sgu_ln_b', 'v_sgu_w_s', 'v_sgu_b_s', 'v_w_branch_att', 'v_w_branch_sgu', 'v_w_out', 'v_norm_ffn2', 'v_ffn2_w_gate', 'v_ffn2_w_up', 'v_ffn2_w_down', 'v_norm_final']
TWIN_OUTPUTS = ['loss', 'grad_x', 'grad_norm_ffn1', 'grad_ffn1_w_gate', 'grad_ffn1_w_up', 'grad_ffn1_w_down', 'grad_norm_mix', 'grad_w_in', 'grad_b_gate', 'grad_rel_bias', 'grad_sgu_ln_g', 'grad_sgu_ln_b', 'grad_sgu_w_s', 'grad_sgu_b_s', 'grad_w_branch_att', 'grad_w_branch_sgu', 'grad_w_out', 'grad_norm_ffn2', 'grad_ffn2_w_gate', 'grad_ffn2_w_up', 'grad_ffn2_w_down', 'grad_norm_final', 'delta_norm_ffn1', 'delta_ffn1_w_gate', 'delta_ffn1_w_up', 'delta_ffn1_w_down', 'delta_norm_mix', 'delta_w_in', 'delta_b_gate', 'delta_rel_bias', 'delta_sgu_ln_g', 'delta_sgu_ln_b', 'delta_sgu_w_s', 'delta_sgu_b_s', 'delta_w_branch_att', 'delta_w_branch_sgu', 'delta_w_out', 'delta_norm_ffn2', 'delta_ffn2_w_gate', 'delta_ffn2_w_up', 'delta_ffn2_w_down', 'delta_norm_final', 'new_m_norm_ffn1', 'new_m_ffn1_w_gate', 'new_m_ffn1_w_up', 'new_m_ffn1_w_down', 'new_m_norm_mix', 'new_m_w_in', 'new_m_b_gate', 'new_m_rel_bias', 'new_m_sgu_ln_g', 'new_m_sgu_ln_b', 'new_m_sgu_w_s', 'new_m_sgu_b_s', 'new_m_w_branch_att', 'new_m_w_branch_sgu', 'new_m_w_out', 'new_m_norm_ffn2', 'new_m_ffn2_w_gate', 'new_m_ffn2_w_up', 'new_m_ffn2_w_down', 'new_m_norm_final', 'new_v_norm_ffn1', 'new_v_ffn1_w_gate', 'new_v_ffn1_w_up', 'new_v_ffn1_w_down', 'new_v_norm_mix', 'new_v_w_in', 'new_v_b_gate', 'new_v_rel_bias', 'new_v_sgu_ln_g', 'new_v_sgu_ln_b', 'new_v_sgu_w_s', 'new_v_sgu_b_s', 'new_v_w_branch_att', 'new_v_w_branch_sgu', 'new_v_w_out', 'new_v_norm_ffn2', 'new_v_ffn2_w_gate', 'new_v_ffn2_w_up', 'new_v_ffn2_w_down', 'new_v_norm_final']
TWIN_LEAF_KINDS = {'loss': 'loss', 'grad_x': 'grad_x', 'grad_norm_ffn1': 'grad_w', 'grad_ffn1_w_gate': 'grad_w', 'grad_ffn1_w_up': 'grad_w', 'grad_ffn1_w_down': 'grad_w', 'grad_norm_mix': 'grad_w', 'grad_w_in': 'grad_w', 'grad_b_gate': 'grad_w', 'grad_rel_bias': 'grad_w', 'grad_sgu_ln_g': 'grad_w', 'grad_sgu_ln_b': 'grad_w', 'grad_sgu_w_s': 'grad_w', 'grad_sgu_b_s': 'grad_w', 'grad_w_branch_att': 'grad_w', 'grad_w_branch_sgu': 'grad_w', 'grad_w_out': 'grad_w', 'grad_norm_ffn2': 'grad_w', 'grad_ffn2_w_gate': 'grad_w', 'grad_ffn2_w_up': 'grad_w', 'grad_ffn2_w_down': 'grad_w', 'grad_norm_final': 'grad_w', 'delta_norm_ffn1': 'delta_w', 'delta_ffn1_w_gate': 'delta_w', 'delta_ffn1_w_up': 'delta_w', 'delta_ffn1_w_down': 'delta_w', 'delta_norm_mix': 'delta_w', 'delta_w_in': 'delta_w', 'delta_b_gate': 'delta_w', 'delta_rel_bias': 'delta_w', 'delta_sgu_ln_g': 'delta_w', 'delta_sgu_ln_b': 'delta_w', 'delta_sgu_w_s': 'delta_w', 'delta_sgu_b_s': 'delta_w', 'delta_w_branch_att': 'delta_w', 'delta_w_branch_sgu': 'delta_w', 'delta_w_out': 'delta_w', 'delta_norm_ffn2': 'delta_w', 'delta_ffn2_w_gate': 'delta_w', 'delta_ffn2_w_up': 'delta_w', 'delta_ffn2_w_down': 'delta_w', 'delta_norm_final': 'delta_w', 'new_m_norm_ffn1': 'new_m', 'new_m_ffn1_w_gate': 'new_m', 'new_m_ffn1_w_up': 'new_m', 'new_m_ffn1_w_down': 'new_m', 'new_m_norm_mix': 'new_m', 'new_m_w_in': 'new_m', 'new_m_b_gate': 'new_m', 'new_m_rel_bias': 'new_m', 'new_m_sgu_ln_g': 'new_m', 'new_m_sgu_ln_b': 'new_m', 'new_m_sgu_w_s': 'new_m', 'new_m_sgu_b_s': 'new_m', 'new_m_w_branch_att': 'new_m', 'new_m_w_branch_sgu': 'new_m', 'new_m_w_out': 'new_m', 'new_m_norm_ffn2': 'new_m', 'new_m_ffn2_w_gate': 'new_m', 'new_m_ffn2_w_up': 'new_m', 'new_m_ffn2_w_down': 'new_m', 'new_m_norm_final': 'new_m', 'new_v_norm_ffn1': 'new_v', 'new_v_ffn1_w_gate': 'new_v', 'new_v_ffn1_w_up': 'new_v', 'new_v_ffn1_w_down': 'new_v', 'new_v_norm_mix': 'new_v', 'new_v_w_in': 'new_v', 'new_v_b_gate': 'new_v', 'new_v_rel_bias': 'new_v', 'new_v_sgu_ln_g': 'new_v', 'new_v_sgu_ln_b': 'new_v', 'new_v_sgu_w_s': 'new_v', 'new_v_sgu_b_s': 'new_v', 'new_v_w_branch_att': 'new_v', 'new_v_w_branch_sgu': 'new_v', 'new_v_w_out': 'new_v', 'new_v_norm_ffn2': 'new_v', 'new_v_ffn2_w_gate': 'new_v', 'new_v_ffn2_w_up': 'new_v', 'new_v_ffn2_w_down': 'new_v', 'new_v_norm_final': 'new_v'}


def _forward(args):
    return _fwd_reference(*[args[k] for k in FWD_PARAMS])


def _output_shape():
    out = _jax.eval_shape(lambda: _forward(_fwd_setup_inputs(0)))
    return out.shape, out.dtype

N_MICROBATCH = 1
ADAM_LR = 0.001
ADAM_B1 = 0.9
ADAM_B2 = 0.999
ADAM_EPS = 1e-08
ADAM_WD = 0.01
ADAM_STEP = 10
PER_EXAMPLE_BATCH_AXIS = {'x': 0, 'loss_target': 0}
SHARED_INPUTS = []
_WEIGHT_DTYPES = {'norm_ffn1': _jnp.float32, 'ffn1_w_gate': _jnp.float32, 'ffn1_w_up': _jnp.float32, 'ffn1_w_down': _jnp.float32, 'norm_mix': _jnp.float32, 'w_in': _jnp.float32, 'b_gate': _jnp.float32, 'rel_bias': _jnp.float32, 'sgu_ln_g': _jnp.float32, 'sgu_ln_b': _jnp.float32, 'sgu_w_s': _jnp.float32, 'sgu_b_s': _jnp.float32, 'w_branch_att': _jnp.float32, 'w_branch_sgu': _jnp.float32, 'w_out': _jnp.float32, 'norm_ffn2': _jnp.float32, 'ffn2_w_gate': _jnp.float32, 'ffn2_w_up': _jnp.float32, 'ffn2_w_down': _jnp.float32, 'norm_final': _jnp.float32}
MOMENT_SCALE = {'norm_ffn1': 8.711233e-02, 'ffn1_w_gate': 3.510948e-02, 'ffn1_w_up': 3.396440e-02, 'ffn1_w_down': 5.630642e-02, 'norm_mix': 1.025267e-01, 'w_in': 4.861336e-02, 'b_gate': 2.224487e-02, 'rel_bias': 6.162061e-03, 'sgu_ln_g': 7.897734e-02, 'sgu_ln_b': 6.812879e-02, 'sgu_w_s': 4.944951e-02, 'sgu_b_s': 5.883266e-02, 'w_branch_att': 1.306163e-02, 'w_branch_sgu': 8.009378e-02, 'w_out': 7.962534e-02, 'norm_ffn2': 6.554827e-02, 'ffn2_w_gate': 2.829006e-02, 'ffn2_w_up': 2.755351e-02, 'ffn2_w_down': 4.572905e-02, 'norm_final': 3.209535e+01}


def _to_microbatches(a, axis):
    t = _jnp.moveaxis(a, axis, 0)
    t = t.reshape((N_MICROBATCH, t.shape[0] // N_MICROBATCH) + t.shape[1:])
    return _jnp.moveaxis(t, 1, axis + 1)


def setup_inputs(seed: int = 0) -> dict:
    inp = _fwd_setup_inputs(seed)
    key = _jax.random.fold_in(_jax.random.key(seed), 7919)
    shape, _ = _output_shape()
    out = dict(inp)
    out["loss_target"] = _jax.random.normal(_jax.random.fold_in(key, 0), shape, _jnp.float32)
    for i, name in enumerate(TWIN_WEIGHTS):
        w = inp[name].astype(_jnp.float32)
        if MOMENT_SCALE is None:
            s = _jnp.sqrt(_jnp.mean(_jnp.square(w)) + 1e-30)
        else:
            s = MOMENT_SCALE[name]
        km, kv = _jax.random.split(_jax.random.fold_in(key, i + 1))
        out[name] = w
        out["m_" + name] = s * _jax.random.normal(km, w.shape, _jnp.float32)
        out["v_" + name] = (s * s) * _jax.random.uniform(kv, w.shape, _jnp.float32, 0.5, 1.5)
    if N_MICROBATCH > 1:
        for name, axis in PER_EXAMPLE_BATCH_AXIS.items():
            out[name] = _to_microbatches(out[name], axis)
    return {'x': out['x'], 'norm_ffn1': out['norm_ffn1'], 'ffn1_w_gate': out['ffn1_w_gate'], 'ffn1_w_up': out['ffn1_w_up'], 'ffn1_w_down': out['ffn1_w_down'], 'norm_mix': out['norm_mix'], 'w_in': out['w_in'], 'b_gate': out['b_gate'], 'rel_bias': out['rel_bias'], 'sgu_ln_g': out['sgu_ln_g'], 'sgu_ln_b': out['sgu_ln_b'], 'sgu_w_s': out['sgu_w_s'], 'sgu_b_s': out['sgu_b_s'], 'w_branch_att': out['w_branch_att'], 'w_branch_sgu': out['w_branch_sgu'], 'w_out': out['w_out'], 'norm_ffn2': out['norm_ffn2'], 'ffn2_w_gate': out['ffn2_w_gate'], 'ffn2_w_up': out['ffn2_w_up'], 'ffn2_w_down': out['ffn2_w_down'], 'norm_final': out['norm_final'], 'loss_target': out['loss_target'], 'm_norm_ffn1': out['m_norm_ffn1'], 'm_ffn1_w_gate': out['m_ffn1_w_gate'], 'm_ffn1_w_up': out['m_ffn1_w_up'], 'm_ffn1_w_down': out['m_ffn1_w_down'], 'm_norm_mix': out['m_norm_mix'], 'm_w_in': out['m_w_in'], 'm_b_gate': out['m_b_gate'], 'm_rel_bias': out['m_rel_bias'], 'm_sgu_ln_g': out['m_sgu_ln_g'], 'm_sgu_ln_b': out['m_sgu_ln_b'], 'm_sgu_w_s': out['m_sgu_w_s'], 'm_sgu_b_s': out['m_sgu_b_s'], 'm_w_branch_att': out['m_w_branch_att'], 'm_w_branch_sgu': out['m_w_branch_sgu'], 'm_w_out': out['m_w_out'], 'm_norm_ffn2': out['m_norm_ffn2'], 'm_ffn2_w_gate': out['m_ffn2_w_gate'], 'm_ffn2_w_up': out['m_ffn2_w_up'], 'm_ffn2_w_down': out['m_ffn2_w_down'], 'm_norm_final': out['m_norm_final'], 'v_norm_ffn1': out['v_norm_ffn1'], 'v_ffn1_w_gate': out['v_ffn1_w_gate'], 'v_ffn1_w_up': out['v_ffn1_w_up'], 'v_ffn1_w_down': out['v_ffn1_w_down'], 'v_norm_mix': out['v_norm_mix'], 'v_w_in': out['v_w_in'], 'v_b_gate': out['v_b_gate'], 'v_rel_bias': out['v_rel_bias'], 'v_sgu_ln_g': out['v_sgu_ln_g'], 'v_sgu_ln_b': out['v_sgu_ln_b'], 'v_sgu_w_s': out['v_sgu_w_s'], 'v_sgu_b_s': out['v_sgu_b_s'], 'v_w_branch_att': out['v_w_branch_att'], 'v_w_branch_sgu': out['v_w_branch_sgu'], 'v_w_out': out['v_w_out'], 'v_norm_ffn2': out['v_norm_ffn2'], 'v_ffn2_w_gate': out['v_ffn2_w_gate'], 'v_ffn2_w_up': out['v_ffn2_w_up'], 'v_ffn2_w_down': out['v_ffn2_w_down'], 'v_norm_final': out['v_norm_final']}


def _loss(weights, diff, rest, loss_target):
    with _jax.named_scope("forward"):
        args = {**rest, TWIN_DIFF_INPUT: diff, **{k: w.astype(_WEIGHT_DTYPES[k]) for k, w in weights.items()}}
        y = _forward(args)
    with _jax.named_scope("loss_head"):
        err = _jnp.square(y.astype(_jnp.float32) - loss_target)
        return 0.5 * _jnp.sum(_jnp.mean(err, axis=-1)) if err.ndim else 0.5 * err


def _adamw(w, g, m, v):
    m = ADAM_B1 * m + (1.0 - ADAM_B1) * g
    v = ADAM_B2 * v + (1.0 - ADAM_B2) * _jnp.square(g)
    m_hat = m / (1.0 - ADAM_B1 ** ADAM_STEP)
    v_hat = v / (1.0 - ADAM_B2 ** ADAM_STEP)
    delta = -ADAM_LR * (m_hat / (_jnp.sqrt(v_hat) + ADAM_EPS) + ADAM_WD * w)
    return delta, m, v


def reference(x, norm_ffn1, ffn1_w_gate, ffn1_w_up, ffn1_w_down, norm_mix, w_in, b_gate, rel_bias, sgu_ln_g, sgu_ln_b, sgu_w_s, sgu_b_s, w_branch_att, w_branch_sgu, w_out, norm_ffn2, ffn2_w_gate, ffn2_w_up, ffn2_w_down, norm_final, loss_target, m_norm_ffn1, m_ffn1_w_gate, m_ffn1_w_up, m_ffn1_w_down, m_norm_mix, m_w_in, m_b_gate, m_rel_bias, m_sgu_ln_g, m_sgu_ln_b, m_sgu_w_s, m_sgu_b_s, m_w_branch_att, m_w_branch_sgu, m_w_out, m_norm_ffn2, m_ffn2_w_gate, m_ffn2_w_up, m_ffn2_w_down, m_norm_final, v_norm_ffn1, v_ffn1_w_gate, v_ffn1_w_up, v_ffn1_w_down, v_norm_mix, v_w_in, v_b_gate, v_rel_bias, v_sgu_ln_g, v_sgu_ln_b, v_sgu_w_s, v_sgu_b_s, v_w_branch_att, v_w_branch_sgu, v_w_out, v_norm_ffn2, v_ffn2_w_gate, v_ffn2_w_up, v_ffn2_w_down, v_norm_final):
    given = dict(x=x, norm_ffn1=norm_ffn1, ffn1_w_gate=ffn1_w_gate, ffn1_w_up=ffn1_w_up, ffn1_w_down=ffn1_w_down, norm_mix=norm_mix, w_in=w_in, b_gate=b_gate, rel_bias=rel_bias, sgu_ln_g=sgu_ln_g, sgu_ln_b=sgu_ln_b, sgu_w_s=sgu_w_s, sgu_b_s=sgu_b_s, w_branch_att=w_branch_att, w_branch_sgu=w_branch_sgu, w_out=w_out, norm_ffn2=norm_ffn2, ffn2_w_gate=ffn2_w_gate, ffn2_w_up=ffn2_w_up, ffn2_w_down=ffn2_w_down, norm_final=norm_final, loss_target=loss_target, m_norm_ffn1=m_norm_ffn1, m_ffn1_w_gate=m_ffn1_w_gate, m_ffn1_w_up=m_ffn1_w_up, m_ffn1_w_down=m_ffn1_w_down, m_norm_mix=m_norm_mix, m_w_in=m_w_in, m_b_gate=m_b_gate, m_rel_bias=m_rel_bias, m_sgu_ln_g=m_sgu_ln_g, m_sgu_ln_b=m_sgu_ln_b, m_sgu_w_s=m_sgu_w_s, m_sgu_b_s=m_sgu_b_s, m_w_branch_att=m_w_branch_att, m_w_branch_sgu=m_w_branch_sgu, m_w_out=m_w_out, m_norm_ffn2=m_norm_ffn2, m_ffn2_w_gate=m_ffn2_w_gate, m_ffn2_w_up=m_ffn2_w_up, m_ffn2_w_down=m_ffn2_w_down, m_norm_final=m_norm_final, v_norm_ffn1=v_norm_ffn1, v_ffn1_w_gate=v_ffn1_w_gate, v_ffn1_w_up=v_ffn1_w_up, v_ffn1_w_down=v_ffn1_w_down, v_norm_mix=v_norm_mix, v_w_in=v_w_in, v_b_gate=v_b_gate, v_rel_bias=v_rel_bias, v_sgu_ln_g=v_sgu_ln_g, v_sgu_ln_b=v_sgu_ln_b, v_sgu_w_s=v_sgu_w_s, v_sgu_b_s=v_sgu_b_s, v_w_branch_att=v_w_branch_att, v_w_branch_sgu=v_w_branch_sgu, v_w_out=v_w_out, v_norm_ffn2=v_norm_ffn2, v_ffn2_w_gate=v_ffn2_w_gate, v_ffn2_w_up=v_ffn2_w_up, v_ffn2_w_down=v_ffn2_w_down, v_norm_final=v_norm_final)
    weights = {n: given[n] for n in TWIN_WEIGHTS}
    shared = {n: given[n] for n in SHARED_INPUTS}
    per_example = {n: given[n] for n in ['x']}
    grad_fn = _jax.value_and_grad(_loss, argnums=(0, 1))

    def one_microbatch(ex, loss_target):
        ex = dict(ex)
        diff = ex.pop(TWIN_DIFF_INPUT)
        return grad_fn(weights, diff, {**shared, **ex}, loss_target)

    if N_MICROBATCH == 1:
        loss, (grad_w, grad_x) = one_microbatch(per_example, given["loss_target"])
    else:
        def body(carry, xs):
            loss_sum, grad_sum = carry
            l_k, (gw_k, gx_k) = one_microbatch(xs[0], xs[1])
            with _jax.named_scope("update"):
                return (loss_sum + l_k, _jax.tree.map(_jnp.add, grad_sum, gw_k)), gx_k

        init = (_jnp.zeros((), _jnp.float32), _jax.tree.map(_jnp.zeros_like, weights))
        (loss, grad_w), grad_x = _jax.lax.scan(body, init, (per_example, given["loss_target"]))
    with _jax.named_scope("update"):
        delta_w, new_m, new_v = {}, {}, {}
        for n in TWIN_WEIGHTS:
            delta_w[n], new_m[n], new_v[n] = _adamw(weights[n], grad_w[n], given["m_" + n], given["v_" + n])
    return (loss, grad_x, *[grad_w[n] for n in TWIN_WEIGHTS], *[delta_w[n] for n in TWIN_WEIGHTS],
            *[new_m[n] for n in TWIN_WEIGHTS], *[new_v[n] for n in TWIN_WEIGHTS])
```

```python
import functools

import jax
import jax.numpy as jnp
from jax import lax
from jax.experimental import pallas as pl
from jax.experimental.pallas import tpu as pltpu

F32 = jnp.float32
BF16 = jnp.bfloat16

D_MODEL = 1024
N_SHARD = 4
D_FF = 2816
FF_S = D_FF // N_SHARD
D_ATT = 512
D_SGU = 512
D_IN = 3 * D_ATT + 2 * D_SGU + 2 * D_MODEL
IN_S = D_IN // N_SHARD
BR_S = D_MODEL // N_SHARD
HEADS = 8
HEAD_DIM = 64
CHUNK = 64
N_LEFT = 8
BAND = (N_LEFT + 1) * CHUNK
REL_CLIP = 256
N_REL = 2 * REL_CLIP + 1
REL_PAD = 640
SGU_BLOCK = 128
SGU_GROUPS = 8
SGU_GDIM = 64
EPS = 1e-6
NEG_INF = -1e30

ATT_ROWS = 2 * CHUNK
ATT_KEYS = BAND + CHUNK
ATT_PAD = N_LEFT * CHUNK

ADAM_LR = 0.001
ADAM_B1 = 0.9
ADAM_B2 = 0.999
ADAM_EPS = 1e-08
ADAM_WD = 0.01
ADAM_STEP = 10

TM = 256
TW = 512
VMEM_LIMIT = 56 * 1024 * 1024

SMALL_ROWS = 152
MESH = pl.DeviceIdType.MESH

_NT = (((1,), (1,)), ((), ()))
_TN = (((0,), (0,)), ((), ()))


def _params(sem=None):
    return pltpu.CompilerParams(dimension_semantics=sem, vmem_limit_bytes=VMEM_LIMIT)


def _const_spec(shape):
    nd = len(shape)
    return pl.BlockSpec(shape, lambda *_: (0,) * nd, pipeline_mode=pl.Buffered(1))


def _acc_spec(shape):
    nd = len(shape)
    return pl.BlockSpec(shape, lambda *_: (0,) * nd)


def _row_spec(tm, ncols, off=0):
    return pl.BlockSpec((tm, ncols), lambda i: (i + off, 0))


def _row3_spec(tm, ncols):
    return pl.BlockSpec((N_SHARD, tm, ncols), lambda i: (0, i, 0))


def _dot(a, b):
    return jnp.dot(a, b, preferred_element_type=F32)


def _dot_nt(a, b):
    return lax.dot_general(a, b, _NT, preferred_element_type=F32)


def _dot_tn(a, b):
    return lax.dot_general(a, b, _TN, preferred_element_type=F32)


def _rms_fwd(x, g):
    r = lax.rsqrt(jnp.mean(x * x, axis=-1, keepdims=True) + EPS)
    xhat = x * r
    return xhat, r, xhat * g


def _rms_bwd(dh, xhat, r, g):
    dxhat = dh * g
    dx = r * (dxhat - xhat * jnp.mean(dxhat * xhat, axis=-1, keepdims=True))
    dg = jnp.sum(dh * xhat, axis=0, keepdims=True)
    return dx, dg


def _sigmoid(x):
    return 1.0 / (1.0 + jnp.exp(-x))


def _ffn_fwd(x, g, wg, wu, wd, name):
    T = x.shape[0]

    def body(x_ref, g_ref, wg_ref, wu_ref, wd_ref, xo_ref, h_ref, a_ref, b_ref):
        xv = x_ref[...]
        hb = _rms_fwd(xv, g_ref[...])[2].astype(BF16)
        h_ref[...] = hb
        acc = jnp.zeros((TM, D_MODEL), F32)
        for s in range(N_SHARD):
            a = _dot(hb, wg_ref[s])
            b = _dot(hb, wu_ref[s])
            a_ref[s] = a.astype(BF16)
            b_ref[s] = b.astype(BF16)
            sv = a * _sigmoid(a) * b
            acc += _dot(sv.astype(BF16), wd_ref[s])
        xo_ref[...] = xv + 0.5 * acc

    return pl.pallas_call(
        body, name=name, grid=(T // TM,),
        in_specs=[_row_spec(TM, D_MODEL), _const_spec((1, D_MODEL)), _const_spec(wg.shape), _const_spec(wu.shape),
                  _const_spec(wd.shape)],
        out_specs=[_row_spec(TM, D_MODEL), _row_spec(TM, D_MODEL), _row3_spec(TM, FF_S), _row3_spec(TM, FF_S)],
        out_shape=[jax.ShapeDtypeStruct((T, D_MODEL), F32), jax.ShapeDtypeStruct((T, D_MODEL), BF16),
                   jax.ShapeDtypeStruct((N_SHARD, T, FF_S), BF16), jax.ShapeDtypeStruct((N_SHARD, T, FF_S), BF16)],
        compiler_params=_params(("arbitrary",)),
    )(x, g, wg, wu, wd)


def _ffn_dgrad(dout, x, a, b, g, wg, wu, wd, name):
    T = x.shape[0]

    def body(do_ref, x_ref, a_ref, b_ref, g_ref, wg_ref, wu_ref, wd_ref, dx_ref, da_ref, db_ref, dg_ref):
        do = do_ref[...]
        dob = do.astype(BF16)
        dh = jnp.zeros((TM, D_MODEL), F32)
        for s in range(N_SHARD):
            ds = 0.5 * _dot_nt(dob, wd_ref[s])
            av = a_ref[s].astype(F32)
            bv = b_ref[s].astype(F32)
            sig = _sigmoid(av)
            da = (ds * bv * (sig * (1.0 + av * (1.0 - sig)))).astype(BF16)
            db = (ds * (av * sig)).astype(BF16)
            da_ref[s] = da
            db_ref[s] = db
            dh += _dot_nt(da, wg_ref[s]) + _dot_nt(db, wu_ref[s])
        gv = g_ref[...]
        xhat, r, _ = _rms_fwd(x_ref[...], gv)
        dxn, dg = _rms_bwd(dh, xhat, r, gv)
        dx_ref[...] = do + dxn

        @pl.when(pl.program_id(0) == 0)
        def _():
            dg_ref[...] = jnp.zeros_like(dg_ref)

        dg_ref[...] += dg

    return pl.pallas_call(
        body, name=name, grid=(T // TM,),
        in_specs=[_row_spec(TM, D_MODEL), _row_spec(TM, D_MODEL), _row3_spec(TM, FF_S), _row3_spec(TM, FF_S),
                  _const_spec((1, D_MODEL)), _const_spec(wg.shape), _const_spec(wu.shape), _const_spec(wd.shape)],
        out_specs=[_row_spec(TM, D_MODEL), _row3_spec(TM, FF_S), _row3_spec(TM, FF_S), _acc_spec((1, D_MODEL))],
        out_shape=[jax.ShapeDtypeStruct((T, D_MODEL), F32), jax.ShapeDtypeStruct((N_SHARD, T, FF_S), BF16),
                   jax.ShapeDtypeStruct((N_SHARD, T, FF_S), BF16), jax.ShapeDtypeStruct((1, D_MODEL), F32)],
        compiler_params=_params(("arbitrary",)),
    )(dout, x, a, b, g, wg, wu, wd)


def _ffn_wgrad(h, dout, a, b, da, db, name):
    T = h.shape[0]

    def body(h_ref, do_ref, a_ref, b_ref, da_ref, db_ref, gwg_ref, gwu_ref, gwd_ref):
        @pl.when(pl.program_id(1) == 0)
        def _():
            gwg_ref[...] = jnp.zeros_like(gwg_ref)
            gwu_ref[...] = jnp.zeros_like(gwu_ref)
            gwd_ref[...] = jnp.zeros_like(gwd_ref)

        hv = h_ref[...]
        dob = do_ref[...].astype(BF16)
        av = a_ref[0].astype(F32)
        sv = (0.5 * av * _sigmoid(av) * b_ref[0].astype(F32)).astype(BF16)
        gwg_ref[0] += _dot_tn(hv, da_ref[0])
        gwu_ref[0] += _dot_tn(hv, db_ref[0])
        gwd_ref[0] += _dot_tn(sv, dob)

    tok = pl.BlockSpec((TW, D_MODEL), lambda s, i: (i, 0))
    act = pl.BlockSpec((1, TW, FF_S), lambda s, i: (s, i, 0))
    return pl.pallas_call(
        body, name=name, grid=(N_SHARD, T // TW),
        in_specs=[tok, tok, act, act, act, act],
        out_specs=[pl.BlockSpec((1, D_MODEL, FF_S), lambda s, i: (s, 0, 0)),
                   pl.BlockSpec((1, D_MODEL, FF_S), lambda s, i: (s, 0, 0)),
                   pl.BlockSpec((1, FF_S, D_MODEL), lambda s, i: (s, 0, 0))],
        out_shape=[jax.ShapeDtypeStruct((N_SHARD, D_MODEL, FF_S), F32), jax.ShapeDtypeStruct((N_SHARD, D_MODEL, FF_S), F32),
                   jax.ShapeDtypeStruct((N_SHARD, FF_S, D_MODEL), F32)],
        compiler_params=_params(("arbitrary", "arbitrary")),
    )(h, dout, a, b, da, db)


def _in_fwd(x, g, w_in):
    T = x.shape[0]

    def body(x_ref, g_ref, w_ref, h_ref, qkv_ref, zs_ref, gl_ref):
        hb = _rms_fwd(x_ref[...], g_ref[...])[2].astype(BF16)
        h_ref[...] = hb
        z0 = _dot(hb, w_ref[0])
        qkv_ref[:, 0:IN_S] = z0.astype(BF16)
        z1 = _dot(hb, w_ref[1])
        qkv_ref[:, IN_S:3 * D_ATT] = z1[:, 0:384].astype(BF16)
        zs_ref[:, 0:768] = z1[:, 384:IN_S]
        z2 = _dot(hb, w_ref[2])
        zs_ref[:, 768:1024] = z2[:, 0:256]
        gl_ref[:, 0:896] = z2[:, 256:IN_S]
        gl_ref[:, 896:2048] = _dot(hb, w_ref[3])

    return pl.pallas_call(
        body, name="in_fwd", grid=(T // TM,),
        in_specs=[_row_spec(TM, D_MODEL), _const_spec((1, D_MODEL)), _const_spec(w_in.shape)],
        out_specs=[_row_spec(TM, D_MODEL), _row_spec(TM, 3 * D_ATT), _row_spec(TM, 2 * D_SGU), _row_spec(TM, 2 * D_MODEL)],
        out_shape=[jax.ShapeDtypeStruct((T, D_MODEL), BF16), jax.ShapeDtypeStruct((T, 3 * D_ATT), BF16),
                   jax.ShapeDtypeStruct((T, 2 * D_SGU), F32), jax.ShapeDtypeStruct((T, 2 * D_MODEL), F32)],
        compiler_params=_params(("arbitrary",)),
    )(x, g, w_in)


def _in_dgrad(dx_res, x, g, w_in, dq, dk, dv, dzs, dgl):
    T = x.shape[0]

    def body(dxr_ref, x_ref, g_ref, w_ref, dq_ref, dk_ref, dv_ref, dzs_ref, dgl_ref, dx_ref, dz_ref, dg_ref):
        dz = jnp.concatenate([dq_ref[...], dk_ref[...].astype(BF16), dv_ref[...].astype(BF16), dzs_ref[...], dgl_ref[...]],
                             axis=1)
        dz_ref[...] = dz
        dh = jnp.zeros((TM, D_MODEL), F32)
        for s in range(N_SHARD):
            dh += _dot_nt(dz[:, s * IN_S:(s + 1) * IN_S], w_ref[s])
        gv = g_ref[...]
        xhat, r, _ = _rms_fwd(x_ref[...], gv)
        dxn, dg = _rms_bwd(dh, xhat, r, gv)
        dx_ref[...] = dxr_ref[...] + dxn

        @pl.when(pl.program_id(0) == 0)
        def _():
            dg_ref[...] = jnp.zeros_like(dg_ref)

        dg_ref[...] += dg

    pad_blocks = ATT_PAD // TM
    return pl.pallas_call(
        body, name="in_dgrad", grid=(T // TM,),
        in_specs=[_row_spec(TM, D_MODEL), _row_spec(TM, D_MODEL), _const_spec((1, D_MODEL)), _const_spec(w_in.shape),
                  _row_spec(TM, D_ATT), _row_spec(TM, D_ATT, pad_blocks), _row_spec(TM, D_ATT, pad_blocks),
                  _row_spec(TM, 2 * D_SGU), _row_spec(TM, 2 * D_MODEL)],
        out_specs=[_row_spec(TM, D_MODEL), _row_spec(TM, D_IN), _acc_spec((1, D_MODEL))],
        out_shape=[jax.ShapeDtypeStruct((T, D_MODEL), F32), jax.ShapeDtypeStruct((T, D_IN), BF16),
                   jax.ShapeDtypeStruct((1, D_MODEL), F32)],
        compiler_params=_params(("arbitrary",)),
    )(dx_res, x, g, w_in, dq, dk, dv, dzs, dgl)


def _in_wgrad(h, dz):
    T = h.shape[0]

    def body(h_ref, dz_ref, gw_ref):
        @pl.when(pl.program_id(1) == 0)
        def _():
            gw_ref[...] = jnp.zeros_like(gw_ref)

        gw_ref[0] += _dot_tn(h_ref[...], dz_ref[...])

    return pl.pallas_call(
        body, name="in_wgrad", grid=(N_SHARD, T // TW),
        in_specs=[pl.BlockSpec((TW, D_MODEL), lambda s, i: (i, 0)), pl.BlockSpec((TW, IN_S), lambda s, i: (i, s))],
        out_specs=pl.BlockSpec((1, D_MODEL, IN_S), lambda s, i: (s, 0, 0)),
        out_shape=jax.ShapeDtypeStruct((N_SHARD, D_MODEL, IN_S), F32),
        compiler_params=_params(("arbitrary", "arbitrary")),
    )(h, dz)


def _rel_onehot():
    r = lax.broadcasted_iota(jnp.int32, (REL_PAD, REL_PAD), 0)
    n = lax.broadcasted_iota(jnp.int32, (REL_PAD, REL_PAD), 1)
    idx = jnp.clip(BAND - 1 - n, -REL_CLIP, REL_CLIP) + REL_CLIP
    return jnp.where(r == idx, 1.0, 0.0).astype(BF16)


def _split3(v):
    p1 = v.astype(BF16)
    r1 = v - p1.astype(F32)
    p2 = r1.astype(BF16)
    p3 = (r1 - p2.astype(F32)).astype(BF16)
    return p1, p2, p3


def _relbias_fwd(tab_pad):
    def body(t_ref, o_ref):
        oh = _rel_onehot()
        acc = jnp.zeros((HEADS, REL_PAD), F32)
        for p in _split3(t_ref[...]):
            acc += _dot(p, oh)
        o_ref[...] = acc

    return pl.pallas_call(body, name="relbias_fwd", out_shape=jax.ShapeDtypeStruct((HEADS, REL_PAD), F32))(tab_pad)


def _relbias_bwd(z):
    def body(z_ref, o_ref):
        oh = _rel_onehot()
        dt2 = jnp.sum(z_ref[...], axis=1)
        acc = jnp.zeros((HEADS, REL_PAD), F32)
        for p in _split3(dt2):
            acc += _dot_nt(p, oh)
        o_ref[...] = acc

    return pl.pallas_call(body, name="relbias_bwd", out_shape=jax.ShapeDtypeStruct((HEADS, REL_PAD), F32))(z)


def _bias_blocks(t2):
    flat = jnp.tile(t2, (1, CHUNK))
    skew = flat[:, :CHUNK * (REL_PAD - 1)].reshape(HEADS, CHUNK, REL_PAD - 1)
    bias = skew[:, :, CHUNK - 1:CHUNK - 1 + BAND]
    slabs = [jnp.pad(bias, ((0, 0), (0, 0), (CHUNK * c, ATT_KEYS - BAND - CHUNK * c)), constant_values=NEG_INF)
             for c in range(2)]
    return jnp.concatenate(slabs, axis=1)


def _unskew(db2):
    out = []
    for c in range(2):
        slab = db2[:, CHUNK * c:CHUNK * (c + 1), CHUNK * c:CHUNK * c + BAND]
        y = jnp.pad(slab, ((0, 0), (0, 0), (CHUNK - 1, REL_PAD - BAND - CHUNK + 1)))
        yf = jnp.pad(y.reshape(HEADS, CHUNK * REL_PAD), ((0, 0), (0, CHUNK)))
        out.append(yf.reshape(HEADS, CHUNK, REL_PAD + 1)[:, :, :REL_PAD])
    return jnp.concatenate(out, axis=1)


def _att_load(qkv_hbm, q_s, k_s, v_s, sem, T):
    copies = [pltpu.make_async_copy(qkv_hbm.at[:, 0:D_ATT], q_s, sem.at[0]),
              pltpu.make_async_copy(qkv_hbm.at[:, D_ATT:2 * D_ATT], k_s.at[pl.ds(ATT_PAD, T), :], sem.at[1]),
              pltpu.make_async_copy(qkv_hbm.at[:, 2 * D_ATT:3 * D_ATT], v_s.at[pl.ds(ATT_PAD, T), :], sem.at[2])]
    for cp in copies:
        cp.start()
    k_s[0:ATT_PAD, :] = jnp.zeros((ATT_PAD, D_ATT), BF16)
    v_s[0:ATT_PAD, :] = jnp.zeros((ATT_PAD, D_ATT), BF16)
    for cp in copies:
        cp.wait()


def _att_probs(qh, kh, bias, valid):
    s = _dot_nt(qh, kh) * (HEAD_DIM ** -0.5) + bias
    s = jnp.where(valid, s, NEG_INF)
    e = jnp.exp(s - jnp.max(s, axis=-1, keepdims=True))
    return e / jnp.sum(e, axis=-1, keepdims=True)


def _att_fwd(qkv, bias2):
    T = qkv.shape[0]

    def body(qkv_hbm, bias_ref, y_ref, q_s, k_s, v_s, sem):
        _att_load(qkv_hbm, q_s, k_s, v_s, sem, T)

        def block(i, carry):
            r0 = pl.multiple_of(i * ATT_ROWS, ATT_ROWS)
            q = q_s[pl.ds(r0, ATT_ROWS), :]
            kw = k_s[pl.ds(r0, ATT_KEYS), :]
            vw = v_s[pl.ds(r0, ATT_KEYS), :]
            valid = (lax.broadcasted_iota(jnp.int32, (1, ATT_KEYS), 1) + (r0 - ATT_PAD)) >= 0
            outs = []
            for h in range(HEADS):
                hs = slice(h * HEAD_DIM, (h + 1) * HEAD_DIM)
                p = _att_probs(q[:, hs], kw[:, hs], bias_ref[h], valid)
                outs.append(_dot(p.astype(BF16), vw[:, hs]))
            y_ref[pl.ds(r0, ATT_ROWS), :] = jnp.concatenate(outs, axis=1).astype(BF16)
            return carry

        lax.fori_loop(0, T // ATT_ROWS, block, 0)

    return pl.pallas_call(
        body, name="att_fwd",
        in_specs=[pl.BlockSpec(memory_space=pl.ANY), pl.BlockSpec(memory_space=pltpu.VMEM)],
        out_specs=pl.BlockSpec(memory_space=pltpu.VMEM),
        out_shape=jax.ShapeDtypeStruct((T, D_ATT), BF16),
        scratch_shapes=[pltpu.VMEM((T, D_ATT), BF16), pltpu.VMEM((T + ATT_PAD, D_ATT), BF16),
                        pltpu.VMEM((T + ATT_PAD, D_ATT), BF16), pltpu.SemaphoreType.DMA((3,))],
        compiler_params=_params(),
    )(qkv, bias2)


def _att_bwd(qkv, dy, bias2):
    T = qkv.shape[0]

    def body(qkv_hbm, dy_ref, bias_ref, dq_ref, dk_ref, dv_ref, db_ref, q_s, k_s, v_s, sem):
        _att_load(qkv_hbm, q_s, k_s, v_s, sem, T)
        dk_ref[...] = jnp.zeros_like(dk_ref)
        dv_ref[...] = jnp.zeros_like(dv_ref)
        db_ref[...] = jnp.zeros_like(db_ref)

        def block(i, carry):
            r0 = pl.multiple_of(i * ATT_ROWS, ATT_ROWS)
            q = q_s[pl.ds(r0, ATT_ROWS), :]
            kw = k_s[pl.ds(r0, ATT_KEYS), :]
            vw = v_s[pl.ds(r0, ATT_KEYS), :]
            dyb = dy_ref[pl.ds(r0, ATT_ROWS), :]
            valid = (lax.broadcasted_iota(jnp.int32, (1, ATT_KEYS), 1) + (r0 - ATT_PAD)) >= 0
            dqs, dks, dvs = [], [], []
            for h in range(HEADS):
                hs = slice(h * HEAD_DIM, (h + 1) * HEAD_DIM)
                qh, kh, vh, dyh = q[:, hs], kw[:, hs], vw[:, hs], dyb[:, hs]
                p = _att_probs(qh, kh, bias_ref[h], valid)
                dp = _dot_nt(dyh, vh)
                ds = p * (dp - jnp.sum(p * dp, axis=-1, keepdims=True))
                db_ref[h] += ds
                dsb = (ds * (HEAD_DIM ** -0.5)).astype(BF16)
                dqs.append(_dot(dsb, kh))
                dks.append(_dot_tn(dsb, qh))
                dvs.append(_dot_tn(p.astype(BF16), dyh))
            dq_ref[pl.ds(r0, ATT_ROWS), :] = jnp.concatenate(dqs, axis=1).astype(BF16)
            dk_ref[pl.ds(r0, ATT_KEYS), :] += jnp.concatenate(dks, axis=1)
            dv_ref[pl.ds(r0, ATT_KEYS), :] += jnp.concatenate(dvs, axis=1)
            return carry

        lax.fori_loop(0, T // ATT_ROWS, block, 0)

    vmem = pl.BlockSpec(memory_space=pltpu.VMEM)
    return pl.pallas_call(
        body, name="att_bwd",
        in_specs=[pl.BlockSpec(memory_space=pl.ANY), vmem, vmem],
        out_specs=[vmem, vmem, vmem, vmem],
        out_shape=[jax.ShapeDtypeStruct((T, D_ATT), BF16), jax.ShapeDtypeStruct((T + ATT_PAD, D_ATT), F32),
                   jax.ShapeDtypeStruct((T + ATT_PAD, D_ATT), F32), jax.ShapeDtypeStruct((HEADS, ATT_ROWS, ATT_KEYS), F32)],
        scratch_shapes=[pltpu.VMEM((T, D_ATT), BF16), pltpu.VMEM((T + ATT_PAD, D_ATT), BF16),
                        pltpu.VMEM((T + ATT_PAD, D_ATT), BF16), pltpu.SemaphoreType.DMA((3,))],
        compiler_params=_params(),
    )(qkv, dy, bias2)


_GELU_C = 0.7978845608028654
_GELU_A = 0.044715


def _gelu(x):
    t = jnp.tanh(_GELU_C * (x + _GELU_A * x * x * x))
    return 0.5 * x * (1.0 + t), t


def _gelu_grad(x, t):
    return 0.5 * (1.0 + t) + 0.5 * x * (1.0 - t * t) * _GELU_C * (1.0 + 3.0 * _GELU_A * x * x)


def _group_masks():
    col = lax.broadcasted_iota(jnp.int32, (SGU_GROUPS, D_SGU), 1) // SGU_GDIM
    grp = lax.broadcasted_iota(jnp.int32, (SGU_GROUPS, D_SGU), 0)
    return jnp.where(col == grp, 1.0, 0.0).astype(F32)


def _causal_mask(transposed=False):
    i = lax.broadcasted_iota(jnp.int32, (SGU_BLOCK, SGU_BLOCK), 0) // CHUNK
    j = lax.broadcasted_iota(jnp.int32, (SGU_BLOCK, SGU_BLOCK), 1) // CHUNK
    return (j >= i) if transposed else (i >= j)


def _sgu_norm(zs, lng, lnb):
    gz, t = _gelu(zs)
    u = gz[:, 0:D_SGU]
    vs = gz[:, D_SGU:2 * D_SGU]
    xc = vs - jnp.mean(vs, axis=-1, keepdims=True)
    rstd = lax.rsqrt(jnp.mean(xc * xc, axis=-1, keepdims=True) + EPS)
    xhat = xc * rstd
    return t, u, xhat, rstd, xhat * lng + lnb


def _sgu_mix(vn_blk, w_ref, bst, gm):
    mask = _causal_mask()
    s = jnp.zeros((SGU_BLOCK, D_SGU), F32)
    for g in range(SGU_GROUPS):
        wm = jnp.where(mask, w_ref[g], 0.0).astype(BF16)
        s += _dot(wm, (vn_blk * gm[g:g + 1, :]).astype(BF16))
        s += bst[:, g:g + 1] * gm[g:g + 1, :]
    return s


def _sgu_fwd(zs, lng, lnb, w_s, bst):
    T = zs.shape[0]
    nblk = TM // SGU_BLOCK

    def body(zs_ref, lng_ref, lnb_ref, w_ref, bst_ref, y_ref):
        _, u, _, _, vn = _sgu_norm(zs_ref[...], lng_ref[...], lnb_ref[...])
        gm = _group_masks()
        bst_v = bst_ref[...]
        for n in range(nblk):
            rows = slice(n * SGU_BLOCK, (n + 1) * SGU_BLOCK)
            s = _sgu_mix(vn[rows], w_ref, bst_v, gm)
            y_ref[rows, :] = (u[rows] * s).astype(BF16)

    return pl.pallas_call(
        body, name="sgu_fwd", grid=(T // TM,),
        in_specs=[_row_spec(TM, 2 * D_SGU), _const_spec((1, D_SGU)), _const_spec((1, D_SGU)),
                  _const_spec(w_s.shape), _const_spec(bst.shape)],
        out_specs=_row_spec(TM, D_SGU),
        out_shape=jax.ShapeDtypeStruct((T, D_SGU), BF16),
        compiler_params=_params(("arbitrary",)),
    )(zs, lng, lnb, w_s, bst)


def _sgu_bwd(zs, dy, lng, lnb, w_s, w_st, bst):
    T = zs.shape[0]
    nblk = TM // SGU_BLOCK

    def body(zs_ref, dy_ref, lng_ref, lnb_ref, w_ref, wt_ref, bst_ref, dzs_ref, dw_ref, dbt_ref, dlg_ref, dlb_ref):
        @pl.when(pl.program_id(0) == 0)
        def _():
            dw_ref[...] = jnp.zeros_like(dw_ref)
            dbt_ref[...] = jnp.zeros_like(dbt_ref)
            dlg_ref[...] = jnp.zeros_like(dlg_ref)
            dlb_ref[...] = jnp.zeros_like(dlb_ref)

        zs_v = zs_ref[...]
        lng_v = lng_ref[...]
        t, u, xhat, rstd, vn = _sgu_norm(zs_v, lng_v, lnb_ref[...])
        gm = _group_masks()
        bst_v = bst_ref[...]
        mask = _causal_mask()
        mask_t = _causal_mask(transposed=True)
        dyv = dy_ref[...].astype(F32)
        lane8 = lax.broadcasted_iota(jnp.int32, (1, SGU_GROUPS), 1)
        du_rows, dvn_rows = [], []
        for n in range(nblk):
            rows = slice(n * SGU_BLOCK, (n + 1) * SGU_BLOCK)
            vn_b = vn[rows]
            s = _sgu_mix(vn_b, w_ref, bst_v, gm)
            du_rows.append(dyv[rows] * s)
            dsb = dyv[rows] * u[rows]
            vnb16 = vn_b.astype(BF16)
            dvn = jnp.zeros((SGU_BLOCK, D_SGU), F32)
            dbt = jnp.zeros((SGU_BLOCK, SGU_GROUPS), F32)
            for g in range(SGU_GROUPS):
                dsg = dsb * gm[g:g + 1, :]
                dsg16 = dsg.astype(BF16)
                wmt = jnp.where(mask_t, wt_ref[g], 0.0).astype(BF16)
                dvn += _dot(wmt, dsg16)
                dw_ref[g] += jnp.where(mask, _dot_nt(dsg16, vnb16), 0.0)
                dbt += jnp.sum(dsg, axis=-1, keepdims=True) * jnp.where(lane8 == g, 1.0, 0.0)
            dbt_ref[...] += dbt
            dvn_rows.append(dvn)
        du = jnp.concatenate(du_rows, axis=0)
        dvn = jnp.concatenate(dvn_rows, axis=0)
        dlg_ref[...] += jnp.sum(dvn * xhat, axis=0, keepdims=True)
        dlb_ref[...] += jnp.sum(dvn, axis=0, keepdims=True)
        dxhat = dvn * lng_v
        dvs = rstd * (dxhat - jnp.mean(dxhat, axis=-1, keepdims=True)
                      - xhat * jnp.mean(dxhat * xhat, axis=-1, keepdims=True))
        dgz = jnp.concatenate([du, dvs], axis=1)
        dzs_ref[...] = (dgz * _gelu_grad(zs_v, t)).astype(BF16)

    return pl.pallas_call(
        body, name="sgu_bwd", grid=(T // TM,),
        in_specs=[_row_spec(TM, 2 * D_SGU), _row_spec(TM, D_SGU), _const_spec((1, D_SGU)), _const_spec((1, D_SGU)),
                  _const_spec(w_s.shape), _const_spec(w_st.shape), _const_spec(bst.shape)],
        out_specs=[_row_spec(TM, 2 * D_SGU), _acc_spec(w_s.shape), _acc_spec(bst.shape), _acc_spec((1, D_SGU)),
                   _acc_spec((1, D_SGU))],
        out_shape=[jax.ShapeDtypeStruct((T, 2 * D_SGU), BF16), jax.ShapeDtypeStruct(w_s.shape, F32),
                   jax.ShapeDtypeStruct(bst.shape, F32), jax.ShapeDtypeStruct((1, D_SGU), F32),
                   jax.ShapeDtypeStruct((1, D_SGU), F32)],
        compiler_params=_params(("arbitrary",)),
    )(zs, dy, lng, lnb, w_s, w_st, bst)


def _cols(v, s):
    return v[:, s * BR_S:(s + 1) * BR_S]


def _merge_fwd(x, y_att, y_sgu, gl, b_gate, wba, wbs, wo):
    T = x.shape[0]

    def body(x_ref, ya_ref, ys_ref, gl_ref, bg_ref, wba_ref, wbs_ref, wo_ref, xo_ref, m_ref, pa_ref, ps_ref):
        ya = ya_ref[...]
        ys = ys_ref[...]
        pa = jnp.concatenate([_dot(ya, wba_ref[s]) for s in range(N_SHARD)], axis=1)
        ps = jnp.concatenate([_dot(ys, wbs_ref[s]) for s in range(N_SHARD)], axis=1)
        g = _sigmoid(gl_ref[...] + bg_ref[...])
        mb = (g[:, 0:D_MODEL] * pa + g[:, D_MODEL:2 * D_MODEL] * ps).astype(BF16)
        m_ref[...] = mb
        pa_ref[...] = pa.astype(BF16)
        ps_ref[...] = ps.astype(BF16)
        acc = jnp.zeros((TM, D_MODEL), F32)
        for s in range(N_SHARD):
            acc += _dot(_cols(mb, s), wo_ref[s])
        xo_ref[...] = x_ref[...] + acc

    tokd = jax.ShapeDtypeStruct((T, D_MODEL), BF16)
    return pl.pallas_call(
        body, name="merge_fwd", grid=(T // TM,),
        in_specs=[_row_spec(TM, D_MODEL), _row_spec(TM, D_ATT), _row_spec(TM, D_SGU), _row_spec(TM, 2 * D_MODEL),
                  _const_spec((1, 2 * D_MODEL)), _const_spec(wba.shape), _const_spec(wbs.shape), _const_spec(wo.shape)],
        out_specs=[_row_spec(TM, D_MODEL)] * 4,
        out_shape=[jax.ShapeDtypeStruct((T, D_MODEL), F32), tokd, tokd, tokd],
        compiler_params=_params(("arbitrary",)),
    )(x, y_att, y_sgu, gl, b_gate, wba, wbs, wo)


def _merge_bwd(dx, y_att, y_sgu, gl, merged, pa, ps, b_gate, wba, wbs, wo):
    T = dx.shape[0]

    def body(dx_ref, ya_ref, ys_ref, gl_ref, m_ref, pa_ref, ps_ref, bg_ref, wba_ref, wbs_ref, wo_ref,
             dya_ref, dys_ref, dgl_ref, dbg_ref, gwba_ref, gwbs_ref, gwo_ref):
        @pl.when(pl.program_id(0) == 0)
        def _():
            dbg_ref[...] = jnp.zeros_like(dbg_ref)
            gwba_ref[...] = jnp.zeros_like(gwba_ref)
            gwbs_ref[...] = jnp.zeros_like(gwbs_ref)
            gwo_ref[...] = jnp.zeros_like(gwo_ref)

        dxb = dx_ref[...].astype(BF16)
        dm = jnp.concatenate([_dot_nt(dxb, wo_ref[s]) for s in range(N_SHARD)], axis=1)
        g = _sigmoid(gl_ref[...] + bg_ref[...])
        ga = g[:, 0:D_MODEL]
        gs = g[:, D_MODEL:2 * D_MODEL]
        dpa = (dm * ga).astype(BF16)
        dps = (dm * gs).astype(BF16)
        dgl = jnp.concatenate([dm * pa_ref[...].astype(F32) * ga * (1.0 - ga),
                               dm * ps_ref[...].astype(F32) * gs * (1.0 - gs)], axis=1)
        dgl_ref[...] = dgl.astype(BF16)
        dbg_ref[...] += jnp.sum(dgl, axis=0, keepdims=True)
        ya = ya_ref[...]
        ys = ys_ref[...]
        mb = m_ref[...]
        dya = jnp.zeros((TM, D_ATT), F32)
        dys = jnp.zeros((TM, D_SGU), F32)
        for s in range(N_SHARD):
            dya += _dot_nt(_cols(dpa, s), wba_ref[s])
            dys += _dot_nt(_cols(dps, s), wbs_ref[s])
            gwo_ref[s] += _dot_tn(_cols(mb, s), dxb)
            gwba_ref[s] += _dot_tn(ya, _cols(dpa, s))
            gwbs_ref[s] += _dot_tn(ys, _cols(dps, s))
        dya_ref[...] = dya.astype(BF16)
        dys_ref[...] = dys.astype(BF16)

    return pl.pallas_call(
        body, name="merge_bwd", grid=(T // TM,),
        in_specs=[_row_spec(TM, D_MODEL), _row_spec(TM, D_ATT), _row_spec(TM, D_SGU), _row_spec(TM, 2 * D_MODEL),
                  _row_spec(TM, D_MODEL), _row_spec(TM, D_MODEL), _row_spec(TM, D_MODEL),
                  _const_spec((1, 2 * D_MODEL)), _const_spec(wba.shape), _const_spec(wbs.shape), _const_spec(wo.shape)],
        out_specs=[_row_spec(TM, D_ATT), _row_spec(TM, D_SGU), _row_spec(TM, 2 * D_MODEL), _acc_spec((1, 2 * D_MODEL)),
                   _acc_spec(wba.shape), _acc_spec(wbs.shape), _acc_spec(wo.shape)],
        out_shape=[jax.ShapeDtypeStruct((T, D_ATT), BF16), jax.ShapeDtypeStruct((T, D_SGU), BF16),
                   jax.ShapeDtypeStruct((T, 2 * D_MODEL), BF16), jax.ShapeDtypeStruct((1, 2 * D_MODEL), F32),
                   jax.ShapeDtypeStruct(wba.shape, F32), jax.ShapeDtypeStruct(wbs.shape, F32),
                   jax.ShapeDtypeStruct(wo.shape, F32)],
        compiler_params=_params(("arbitrary",)),
    )(dx, y_att, y_sgu, gl, merged, pa, ps, b_gate, wba, wbs, wo)


def _loss_bwd(x, target, g):
    T = x.shape[0]

    def body(x_ref, t_ref, g_ref, dx_ref, loss_ref, dg_ref):
        @pl.when(pl.program_id(0) == 0)
        def _():
            loss_ref[...] = jnp.zeros_like(loss_ref)
            dg_ref[...] = jnp.zeros_like(dg_ref)

        gv = g_ref[...]
        xhat, r, y = _rms_fwd(x_ref[...], gv)
        err = y - t_ref[...]
        per_tok = jnp.mean(err * err, axis=-1, keepdims=True)
        loss_ref[...] += 0.5 * jnp.sum(per_tok, axis=0, keepdims=True)
        dxn, dg = _rms_bwd(err * (1.0 / D_MODEL), xhat, r, gv)
        dx_ref[...] = dxn
        dg_ref[...] += dg

    return pl.pallas_call(
        body, name="loss_bwd", grid=(T // TM,),
        in_specs=[_row_spec(TM, D_MODEL), _row_spec(TM, D_MODEL), _const_spec((1, D_MODEL))],
        out_specs=[_row_spec(TM, D_MODEL), _acc_spec((1, 128)), _acc_spec((1, D_MODEL))],
        out_shape=[jax.ShapeDtypeStruct((T, D_MODEL), F32), jax.ShapeDtypeStruct((1, 128), F32),
                   jax.ShapeDtypeStruct((1, D_MODEL), F32)],
        compiler_params=_params(("arbitrary",)),
    )(x, target, g)


BIG = ("ffn1_w_gate", "ffn1_w_up", "ffn1_w_down", "w_in", "w_branch_att", "w_branch_sgu", "w_out",
       "ffn2_w_gate", "ffn2_w_up", "ffn2_w_down")
SMALL = ("norm_ffn1", "norm_mix", "b_gate", "rel_bias", "sgu_ln_g", "sgu_ln_b", "sgu_w_s", "sgu_b_s", "norm_ffn2",
         "norm_final")


def _local_step(x, target, wb, ws):
    t2 = _relbias_fwd(ws["rel_bias"])
    bias2 = _bias_blocks(t2)
    bst = ws["sgu_b_s"].T
    w_st = jnp.swapaxes(ws["sgu_w_s"], 1, 2)

    x1, h1, a1, b1 = _ffn_fwd(x, ws["norm_ffn1"], wb["ffn1_w_gate"], wb["ffn1_w_up"], wb["ffn1_w_down"], "ffn1_fwd")
    h2, qkv, zs, gl = _in_fwd(x1, ws["norm_mix"], wb["w_in"])
    y_att = _att_fwd(qkv, bias2)
    y_sgu = _sgu_fwd(zs, ws["sgu_ln_g"], ws["sgu_ln_b"], ws["sgu_w_s"], bst)
    x2, merged, pa, ps = _merge_fwd(x1, y_att, y_sgu, gl, ws["b_gate"], wb["w_branch_att"], wb["w_branch_sgu"],
                                    wb["w_out"])
    x3, h3, a3, b3 = _ffn_fwd(x2, ws["norm_ffn2"], wb["ffn2_w_gate"], wb["ffn2_w_up"], wb["ffn2_w_down"], "ffn2_fwd")
    dx3, loss, g_final = _loss_bwd(x3, target, ws["norm_final"])

    gb, gs = {}, {"norm_final": g_final}
    dx2, da3, db3, gs["norm_ffn2"] = _ffn_dgrad(dx3, x2, a3, b3, ws["norm_ffn2"], wb["ffn2_w_gate"], wb["ffn2_w_up"],
                                                wb["ffn2_w_down"], "ffn2_dgrad")
    gb["ffn2_w_gate"], gb["ffn2_w_up"], gb["ffn2_w_down"] = _ffn_wgrad(h3, dx3, a3, b3, da3, db3, "ffn2_wgrad")
    dy_att, dy_sgu, dgl, gs["b_gate"], gb["w_branch_att"], gb["w_branch_sgu"], gb["w_out"] = _merge_bwd(
        dx2, y_att, y_sgu, gl, merged, pa, ps, ws["b_gate"], wb["w_branch_att"], wb["w_branch_sgu"], wb["w_out"])
    dq, dk, dv, db2 = _att_bwd(qkv, dy_att, bias2)
    gs["rel_bias"] = _relbias_bwd(_unskew(db2))
    dzs, gs["sgu_w_s"], dbt, gs["sgu_ln_g"], gs["sgu_ln_b"] = _sgu_bwd(zs, dy_sgu, ws["sgu_ln_g"], ws["sgu_ln_b"],
                                                                      ws["sgu_w_s"], w_st, bst)
    gs["sgu_b_s"] = dbt.T
    dx1, dz, gs["norm_mix"] = _in_dgrad(dx2, x1, ws["norm_mix"], wb["w_in"], dq, dk, dv, dzs, dgl)
    gb["w_in"] = _in_wgrad(h2, dz)
    gx, da1, db1, gs["norm_ffn1"] = _ffn_dgrad(dx1, x, a1, b1, ws["norm_ffn1"], wb["ffn1_w_gate"], wb["ffn1_w_up"],
                                               wb["ffn1_w_down"], "ffn1_dgrad")
    gb["ffn1_w_gate"], gb["ffn1_w_up"], gb["ffn1_w_down"] = _ffn_wgrad(h1, dx1, a1, b1, da1, db1, "ffn1_wgrad")
    return loss, gx, gb, gs


_ANY = pl.BlockSpec(memory_space=pl.ANY)
_VMEM = pl.BlockSpec(memory_space=pltpu.VMEM)


def _mesh_pos():
    return lax.axis_index("x"), lax.axis_index("y"), lax.axis_index("c")


def _allgather(shards):
    n = len(shards)

    def body(*refs):
        ins, outs = refs[:n], refs[n:2 * n]
        send_i, recv_i, send_d, recv_d, loc = refs[2 * n:]
        x, y, c = _mesh_pos()
        me = 2 * x + y

        def half(w, core):
            rh = shards[w].shape[0] // 2
            return pl.ds(core * rh, rh)

        def remote(src, dst, ssem, rsem, dev):
            return pltpu.make_async_remote_copy(src_ref=src, dst_ref=dst, send_sem=ssem, recv_sem=rsem, device_id=dev,
                                                device_id_type=MESH)

        local = [pltpu.make_async_copy(ins[w], outs[w].at[me], loc.at[w]) for w in range(n)]
        for cp in local:
            cp.start()
        sent = []
        for w in range(n):
            for j in range(3):
                t = (me + 1 + j) % N_SHARD
                cp = remote(ins[w].at[half(w, c), :], outs[w].at[me, half(w, c), :], send_i.at[3 * w + j],
                            recv_i.at[3 * w + j], (t // 2, t % 2, c))
                cp.start()
                sent.append(cp)
        for w in range(n):
            for j in range(3):
                s = (me + 3 - j) % N_SHARD
                land = outs[w].at[s, half(w, c), :]
                remote(land, land, send_i.at[3 * w + j], recv_i.at[3 * w + j], (x, y, c)).wait_recv()
                cp = remote(land, land, send_d.at[3 * w + j], recv_d.at[3 * w + j], (x, y, 1 - c))
                cp.start()
                sent.append(cp)
        for w in range(n):
            for j in range(3):
                s = (me + 3 - j) % N_SHARD
                land = outs[w].at[s, half(w, 1 - c), :]
                remote(land, land, send_d.at[3 * w + j], recv_d.at[3 * w + j], (x, y, c)).wait_recv()
        for cp in sent:
            cp.wait_send()
        for cp in local:
            cp.wait()

    return pl.pallas_call(
        body, name="allgather_weights",
        in_specs=[_ANY] * n, out_specs=[_ANY] * n,
        out_shape=[jax.ShapeDtypeStruct((N_SHARD,) + s.shape, s.dtype) for s in shards],
        scratch_shapes=[pltpu.SemaphoreType.DMA((3 * n,)), pltpu.SemaphoreType.DMA((3 * n,)),
                        pltpu.SemaphoreType.DMA((3 * n,)), pltpu.SemaphoreType.DMA((3 * n,)),
                        pltpu.SemaphoreType.DMA((n,))],
    )(*shards)


def _pair_exchange(grads, small):
    n = len(grads)

    def body(*refs):
        ins, outs = refs[:n + 1], refs[n + 1:2 * n + 2]
        send, recv = refs[2 * n + 2:]
        x, y, c = _mesh_pos()
        cps = []
        for w in range(n + 1):
            if w < n:
                rh = grads[w].shape[1] // 2
                src = ins[w].at[:, pl.ds((1 - c) * rh, rh), :]
            else:
                src = ins[w]
            cp = pltpu.make_async_remote_copy(src_ref=src, dst_ref=outs[w], send_sem=send.at[w], recv_sem=recv.at[w],
                                              device_id=(x, y, 1 - c), device_id_type=MESH)
            cp.start()
            cps.append(cp)
        for cp in cps:
            cp.wait()

    out_shape = [jax.ShapeDtypeStruct((N_SHARD, g.shape[1] // 2, g.shape[2]), F32) for g in grads]
    out_shape.append(jax.ShapeDtypeStruct(small.shape, F32))
    return pl.pallas_call(
        body, name="pair_exchange",
        in_specs=[_ANY] * (n + 1), out_specs=[_ANY] * (n + 1), out_shape=out_shape,
        scratch_shapes=[pltpu.SemaphoreType.DMA((n + 1,)), pltpu.SemaphoreType.DMA((n + 1,))],
    )(*grads, small)


def _pair_add(g, rv, core, name):
    _, r, ncol = g.shape
    rh = r // 2

    def body(c_ref, g_ref, rv_ref, pf_ref, pb_ref):
        s = g_ref[...] + rv_ref[...]
        pf_ref[...] = s
        pb_ref[...] = s.astype(BF16)

    grid_spec = pltpu.PrefetchScalarGridSpec(
        num_scalar_prefetch=1, grid=(N_SHARD,),
        in_specs=[pl.BlockSpec((1, rh, ncol), lambda s, c: (s, c[0], 0)), pl.BlockSpec((1, rh, ncol), lambda s, c: (s, 0, 0))],
        out_specs=[pl.BlockSpec((1, rh, ncol), lambda s, c: (s, 0, 0)), pl.BlockSpec((1, rh, ncol), lambda s, c: (s, 0, 0))])
    return pl.pallas_call(
        body, name=name, grid_spec=grid_spec,
        out_shape=[jax.ShapeDtypeStruct((N_SHARD, rh, ncol), F32), jax.ShapeDtypeStruct((N_SHARD, rh, ncol), BF16)],
        compiler_params=_params(("arbitrary",)),
    )(core, g, rv)


def _chip_exchange(pbs, small, small_sib):
    n = len(pbs)

    def body(*refs):
        ins, sm, sm_sib = refs[:n], refs[n], refs[n + 1]
        outs, sm_out = refs[n + 2:2 * n + 2], refs[2 * n + 2]
        sm_sum, send, recv, loc = refs[2 * n + 3:]
        x, y, c = _mesh_pos()
        me = 2 * x + y
        sm_sum[...] = sm[...] + sm_sib[...]
        local = [pltpu.make_async_copy(ins[w].at[me], outs[w].at[me], loc.at[w]) for w in range(n)]
        local.append(pltpu.make_async_copy(sm_sum, sm_out.at[me], loc.at[n]))
        for cp in local:
            cp.start()
        cps = []
        for w in range(n + 1):
            for j in range(3):
                t = (me + 1 + j) % N_SHARD
                src = ins[w].at[t] if w < n else sm_sum
                dst = outs[w].at[me] if w < n else sm_out.at[me]
                cp = pltpu.make_async_remote_copy(src_ref=src, dst_ref=dst, send_sem=send.at[3 * w + j],
                                                  recv_sem=recv.at[3 * w + j], device_id=(t // 2, t % 2, c),
                                                  device_id_type=MESH)
                cp.start()
                cps.append(cp)
        for cp in cps:
            cp.wait()
        for cp in local:
            cp.wait()

    out_shape = [jax.ShapeDtypeStruct(p.shape, BF16) for p in pbs]
    out_shape.append(jax.ShapeDtypeStruct((N_SHARD,) + small.shape, F32))
    return pl.pallas_call(
        body, name="chip_exchange",
        in_specs=[_ANY] * n + [_VMEM, _VMEM], out_specs=[_ANY] * (n + 1), out_shape=out_shape,
        scratch_shapes=[pltpu.VMEM(small.shape, F32), pltpu.SemaphoreType.DMA((3 * n + 3,)),
                        pltpu.SemaphoreType.DMA((3 * n + 3,)), pltpu.SemaphoreType.DMA((n + 1,))],
    )(*pbs, small, small_sib)


def _final_sum(pf, rin, chip, name):
    _, rh, ncol = pf.shape

    def body(me_ref, pf_ref, rin_ref, o_ref):
        me = me_ref[0]
        acc = jnp.zeros((rh, ncol), F32)
        for k in range(N_SHARD):
            acc = acc + jnp.where(me == k, pf_ref[0], rin_ref[k].astype(F32))
        o_ref[...] = acc

    grid_spec = pltpu.PrefetchScalarGridSpec(
        num_scalar_prefetch=1, grid=(1,),
        in_specs=[pl.BlockSpec((1, rh, ncol), lambda i, me: (me[0], 0, 0)),
                  pl.BlockSpec((N_SHARD, rh, ncol), lambda i, me: (0, 0, 0))],
        out_specs=pl.BlockSpec((rh, ncol), lambda i, me: (0, 0)))
    return pl.pallas_call(
        body, name=name, grid_spec=grid_spec, out_shape=jax.ShapeDtypeStruct((rh, ncol), F32),
        compiler_params=_params(("arbitrary",)),
    )(chip, pf, rin)


def _sibling_share(halves):
    n = len(halves)

    def body(*refs):
        ins, outs = refs[:n], refs[n:2 * n]
        send, recv, loc = refs[2 * n:]
        x, y, c = _mesh_pos()
        cps, local = [], []
        for w in range(n):
            rh = halves[w].shape[0]
            mine = outs[w].at[pl.ds(c * rh, rh), :]
            lc = pltpu.make_async_copy(ins[w], mine, loc.at[w])
            lc.start()
            local.append(lc)
            cp = pltpu.make_async_remote_copy(src_ref=ins[w], dst_ref=mine, send_sem=send.at[w], recv_sem=recv.at[w],
                                              device_id=(x, y, 1 - c), device_id_type=MESH)
            cp.start()
            cps.append(cp)
        for cp in cps:
            cp.wait()
        for lc in local:
            lc.wait()

    return pl.pallas_call(
        body, name="sibling_share",
        in_specs=[_ANY] * n, out_specs=[_ANY] * n,
        out_shape=[jax.ShapeDtypeStruct((2 * h.shape[0], h.shape[1]), F32) for h in halves],
        scratch_shapes=[pltpu.SemaphoreType.DMA((n,)), pltpu.SemaphoreType.DMA((n,)), pltpu.SemaphoreType.DMA((n,))],
    )(*halves)


_ROW = {"rel_bias": 128, "sgu_b_s": 136, "norm_ffn1": 144, "norm_mix": 145, "norm_ffn2": 146, "norm_final": 147,
        "b_gate": 148, "sgu_ln_g": 150, "sgu_ln_b": 151}


def _pack_small(gs):
    def body(ws, rel, bs, n1, nm, n2, nf, bg, lg, lb, o_ref):
        o_ref[...] = jnp.zeros_like(o_ref)
        for g in range(SGU_GROUPS):
            o_ref[0:SGU_BLOCK, g * SGU_BLOCK:(g + 1) * SGU_BLOCK] = ws[g]
        o_ref[128:136, 0:REL_PAD] = rel[...]
        o_ref[136:144, 0:SGU_BLOCK] = bs[...]
        o_ref[144:145, :] = n1[...]
        o_ref[145:146, :] = nm[...]
        o_ref[146:147, :] = n2[...]
        o_ref[147:148, :] = nf[...]
        o_ref[148:149, :] = bg[:, 0:D_MODEL]
        o_ref[149:150, :] = bg[:, D_MODEL:2 * D_MODEL]
        o_ref[150:151, 0:D_SGU] = lg[...]
        o_ref[151:152, 0:D_SGU] = lb[...]

    order = ("sgu_w_s", "rel_bias", "sgu_b_s", "norm_ffn1", "norm_mix", "norm_ffn2", "norm_final", "b_gate", "sgu_ln_g",
             "sgu_ln_b")
    return pl.pallas_call(body, name="pack_small", out_shape=jax.ShapeDtypeStruct((SMALL_ROWS, D_MODEL), F32))(
        *[gs[k] for k in order])


def _adam(w, g, m, v):
    m2 = ADAM_B1 * m + (1.0 - ADAM_B1) * g
    v2 = ADAM_B2 * v + (1.0 - ADAM_B2) * (g * g)
    m_hat = m2 / (1.0 - ADAM_B1 ** ADAM_STEP)
    v_hat = v2 / (1.0 - ADAM_B2 ** ADAM_STEP)
    delta = -ADAM_LR * (m_hat / (jnp.sqrt(v_hat) + ADAM_EPS) + ADAM_WD * w)
    return delta, m2, v2


def _adam_small(sin, w, m, v):
    names = SMALL
    k = len(names)

    def body(*refs):
        sin_ref = refs[0]
        w_r, m_r, v_r = refs[1:1 + k], refs[1 + k:1 + 2 * k], refs[1 + 2 * k:1 + 3 * k]
        outs = refs[1 + 3 * k:]
        tot = sin_ref[0] + sin_ref[1] + sin_ref[2] + sin_ref[3]
        for i, name in enumerate(names):
            o = outs[4 * i:4 * i + 4]
            if name == "sgu_w_s":
                for gi in range(SGU_GROUPS):
                    g = tot[0:SGU_BLOCK, gi * SGU_BLOCK:(gi + 1) * SGU_BLOCK]
                    res = (g,) + _adam(w_r[i][gi], g, m_r[i][gi], v_r[i][gi])
                    for ref, val in zip(o, res):
                        ref[gi] = val
                continue
            r0 = _ROW[name]
            if name == "rel_bias":
                g = tot[r0:r0 + HEADS, 0:REL_PAD]
            elif name == "sgu_b_s":
                g = tot[r0:r0 + SGU_GROUPS, 0:SGU_BLOCK]
            elif name == "b_gate":
                g = jnp.concatenate([tot[r0:r0 + 1, :], tot[r0 + 1:r0 + 2, :]], axis=1)
            elif name in ("sgu_ln_g", "sgu_ln_b"):
                g = tot[r0:r0 + 1, 0:D_SGU]
            else:
                g = tot[r0:r0 + 1, :]
            res = (g,) + _adam(w_r[i][...], g, m_r[i][...], v_r[i][...])
            for ref, val in zip(o, res):
                ref[...] = val

    out_shape = []
    for name in names:
        out_shape += [jax.ShapeDtypeStruct(w[name].shape, F32)] * 4
    flat = pl.pallas_call(body, name="adam_small", out_shape=out_shape, compiler_params=_params())(
        sin, *[w[n] for n in names], *[m[n] for n in names], *[v[n] for n in names])
    return {name: tuple(flat[4 * i:4 * i + 4]) for i, name in enumerate(names)}


def _adam_big(w, g, m, v, name):
    r, ncol = w.shape
    tr = 256 if r % 256 == 0 else r // 2

    def body(w_ref, g_ref, m_ref, v_ref, d_ref, m2_ref, v2_ref):
        d_ref[...], m2_ref[...], v2_ref[...] = _adam(w_ref[...], g_ref[...], m_ref[...], v_ref[...])

    spec = pl.BlockSpec((tr, ncol), lambda i: (i, 0))
    return pl.pallas_call(
        body, name=name, grid=(r // tr,), in_specs=[spec] * 4, out_specs=[spec] * 3,
        out_shape=[jax.ShapeDtypeStruct(w.shape, F32)] * 3, compiler_params=_params(("arbitrary",)),
    )(w, g, m, v)


WEIGHTS = ("norm_ffn1", "ffn1_w_gate", "ffn1_w_up", "ffn1_w_down", "norm_mix", "w_in", "b_gate", "rel_bias", "sgu_ln_g",
           "sgu_ln_b", "sgu_w_s", "sgu_b_s", "w_branch_att", "w_branch_sgu", "w_out", "norm_ffn2", "ffn2_w_gate",
           "ffn2_w_up", "ffn2_w_down", "norm_final")


def _small_form(name, a):
    if name == "norm_final":
        return a.reshape(1, D_MODEL)
    if name == "rel_bias":
        return jnp.pad(a[0], ((0, 0), (0, REL_PAD - N_REL)))
    if name in ("sgu_w_s", "sgu_b_s"):
        return a[0]
    return a


def _small_back(name, a, like):
    if name == "rel_bias":
        a = a[:, :N_REL]
    return a.reshape(like.shape)


def kernel(x, norm_ffn1, ffn1_w_gate, ffn1_w_up, ffn1_w_down, norm_mix, w_in, b_gate, rel_bias, sgu_ln_g, sgu_ln_b, sgu_w_s, sgu_b_s, w_branch_att, w_branch_sgu, w_out, norm_ffn2, ffn2_w_gate, ffn2_w_up, ffn2_w_down, norm_final, loss_target, m_norm_ffn1, m_ffn1_w_gate, m_ffn1_w_up, m_ffn1_w_down, m_norm_mix, m_w_in, m_b_gate, m_rel_bias, m_sgu_ln_g, m_sgu_ln_b, m_sgu_w_s, m_sgu_b_s, m_w_branch_att, m_w_branch_sgu, m_w_out, m_norm_ffn2, m_ffn2_w_gate, m_ffn2_w_up, m_ffn2_w_down, m_norm_final, v_norm_ffn1, v_ffn1_w_gate, v_ffn1_w_up, v_ffn1_w_down, v_norm_mix, v_w_in, v_b_gate, v_rel_bias, v_sgu_ln_g, v_sgu_ln_b, v_sgu_w_s, v_sgu_b_s, v_w_branch_att, v_w_branch_sgu, v_w_out, v_norm_ffn2, v_ffn2_w_gate, v_ffn2_w_up, v_ffn2_w_down, v_norm_final):
    w = dict(norm_ffn1=norm_ffn1, ffn1_w_gate=ffn1_w_gate, ffn1_w_up=ffn1_w_up, ffn1_w_down=ffn1_w_down, norm_mix=norm_mix,
             w_in=w_in, b_gate=b_gate, rel_bias=rel_bias, sgu_ln_g=sgu_ln_g, sgu_ln_b=sgu_ln_b, sgu_w_s=sgu_w_s,
             sgu_b_s=sgu_b_s, w_branch_att=w_branch_att, w_branch_sgu=w_branch_sgu, w_out=w_out, norm_ffn2=norm_ffn2,
             ffn2_w_gate=ffn2_w_gate, ffn2_w_up=ffn2_w_up, ffn2_w_down=ffn2_w_down, norm_final=norm_final)
    m = dict(norm_ffn1=m_norm_ffn1, ffn1_w_gate=m_ffn1_w_gate, ffn1_w_up=m_ffn1_w_up, ffn1_w_down=m_ffn1_w_down,
             norm_mix=m_norm_mix, w_in=m_w_in, b_gate=m_b_gate, rel_bias=m_rel_bias, sgu_ln_g=m_sgu_ln_g,
             sgu_ln_b=m_sgu_ln_b, sgu_w_s=m_sgu_w_s, sgu_b_s=m_sgu_b_s, w_branch_att=m_w_branch_att,
             w_branch_sgu=m_w_branch_sgu, w_out=m_w_out, norm_ffn2=m_norm_ffn2, ffn2_w_gate=m_ffn2_w_gate,
             ffn2_w_up=m_ffn2_w_up, ffn2_w_down=m_ffn2_w_down, norm_final=m_norm_final)
    v = dict(norm_ffn1=v_norm_ffn1, ffn1_w_gate=v_ffn1_w_gate, ffn1_w_up=v_ffn1_w_up, ffn1_w_down=v_ffn1_w_down,
             norm_mix=v_norm_mix, w_in=v_w_in, b_gate=v_b_gate, rel_bias=v_rel_bias, sgu_ln_g=v_sgu_ln_g,
             sgu_ln_b=v_sgu_ln_b, sgu_w_s=v_sgu_w_s, sgu_b_s=v_sgu_b_s, w_branch_att=v_w_branch_att,
             w_branch_sgu=v_w_branch_sgu, w_out=v_w_out, norm_ffn2=v_norm_ffn2, ffn2_w_gate=v_ffn2_w_gate,
             ffn2_w_up=v_ffn2_w_up, ffn2_w_down=v_ffn2_w_down, norm_final=v_norm_final)

    core = lax.axis_index("c").astype(jnp.int32).reshape(1)
    chip = (2 * lax.axis_index("x") + lax.axis_index("y")).astype(jnp.int32).reshape(1)

    gathered = _allgather([w[n][0].astype(BF16) for n in BIG])
    wb = dict(zip(BIG, gathered))
    ws = {n: _small_form(n, w[n]) for n in SMALL}
    loss, gx, gb, gs = _local_step(x[0], loss_target[0], wb, ws)

    packed = _pack_small(gs)
    exchanged = _pair_exchange([gb[n] for n in BIG], packed)
    sums = [_pair_add(gb[n], exchanged[i], core, "pair_add_" + n) for i, n in enumerate(BIG)]
    received = _chip_exchange([s[1] for s in sums], packed, exchanged[-1])
    halves = [_final_sum(sums[i][0], received[i], chip, "final_sum_" + n) for i, n in enumerate(BIG)]
    shard_grads = dict(zip(BIG, _sibling_share(halves)))

    small = _adam_small(received[-1], ws, {n: _small_form(n, m[n]) for n in SMALL},
                        {n: _small_form(n, v[n]) for n in SMALL})
    grad, delta, new_m, new_v = {}, {}, {}, {}
    for n in SMALL:
        grad[n], delta[n], new_m[n], new_v[n] = (_small_back(n, a, w[n]) for a in small[n])
    for n in BIG:
        g2 = shard_grads[n]
        d2, m2, v2 = _adam_big(w[n][0], g2, m[n][0], v[n][0], "adam_" + n)
        grad[n], delta[n], new_m[n], new_v[n] = (a.reshape(w[n].shape) for a in (g2, d2, m2, v2))

    total = lax.psum(loss[0, 0], ("x", "y", "c"))
    return (total, gx.reshape(x.shape), *[grad[n] for n in WEIGHTS], *[delta[n] for n in WEIGHTS],
            *[new_m[n] for n in WEIGHTS], *[new_v[n] for n in WEIGHTS])
```

```python
import functools

import jax
import jax.numpy as jnp
from jax import lax
from jax.experimental import pallas as pl
from jax.experimental.pallas import tpu as pltpu

F32 = jnp.float32
BF16 = jnp.bfloat16

D_MODEL = 1024
N_SHARD = 4
D_FF = 2816
FF_S = D_FF // N_SHARD
D_ATT = 512
D_SGU = 512
D_IN = 3 * D_ATT + 2 * D_SGU + 2 * D_MODEL
IN_S = D_IN // N_SHARD
BR_S = D_MODEL // N_SHARD
HEADS = 8
HEAD_DIM = 64
CHUNK = 64
N_LEFT = 8
BAND = (N_LEFT + 1) * CHUNK
REL_CLIP = 256
N_REL = 2 * REL_CLIP + 1
REL_PAD = 640
SGU_BLOCK = 128
SGU_GROUPS = 8
SGU_GDIM = 64
EPS = 1e-6
NEG_INF = -1e30

ATT_ROWS = 2 * CHUNK
ATT_KEYS = BAND + CHUNK
ATT_PAD = N_LEFT * CHUNK

ADAM_LR = 0.001
ADAM_B1 = 0.9
ADAM_B2 = 0.999
ADAM_EPS = 1e-08
ADAM_WD = 0.01
ADAM_STEP = 10

TM = 256
TW = 512
VMEM_LIMIT = 56 * 1024 * 1024

SMALL_ROWS = 152
MESH = pl.DeviceIdType.MESH

_NT = (((1,), (1,)), ((), ()))
_TN = (((0,), (0,)), ((), ()))


def _params(sem=None):
    return pltpu.CompilerParams(dimension_semantics=sem, vmem_limit_bytes=VMEM_LIMIT)


def _const_spec(shape):
    nd = len(shape)
    return pl.BlockSpec(shape, lambda *_: (0,) * nd, pipeline_mode=pl.Buffered(1))


def _acc_spec(shape):
    nd = len(shape)
    return pl.BlockSpec(shape, lambda *_: (0,) * nd)


def _row_spec(tm, ncols, off=0):
    return pl.BlockSpec((tm, ncols), lambda i: (i + off, 0))


def _row3_spec(tm, ncols):
    return pl.BlockSpec((N_SHARD, tm, ncols), lambda i: (0, i, 0))


def _dot(a, b):
    return jnp.dot(a, b, preferred_element_type=F32)


def _dot_nt(a, b):
    return lax.dot_general(a, b, _NT, preferred_element_type=F32)


def _dot_tn(a, b):
    return lax.dot_general(a, b, _TN, preferred_element_type=F32)


def _rms_fwd(x, g):
    r = lax.rsqrt(jnp.mean(x * x, axis=-1, keepdims=True) + EPS)
    xhat = x * r
    return xhat, r, xhat * g


def _rms_bwd(dh, xhat, r, g):
    dxhat = dh * g
    dx = r * (dxhat - xhat * jnp.mean(dxhat * xhat, axis=-1, keepdims=True))
    dg = jnp.sum(dh * xhat, axis=0, keepdims=True)
    return dx, dg


def _sigmoid(x):
    return 1.0 / (1.0 + jnp.exp(-x))


def _ffn_fwd(x, g, wg, wu, wd, name):
    T = x.shape[0]

    def body(x_ref, g_ref, wg_ref, wu_ref, wd_ref, xo_ref, h_ref, a_ref, b_ref):
        xv = x_ref[...]
        hb = _rms_fwd(xv, g_ref[...])[2].astype(BF16)
        h_ref[...] = hb
        acc = jnp.zeros((TM, D_MODEL), F32)
        for s in range(N_SHARD):
            a = _dot_nt(hb, wg_ref[s])
            b = _dot_nt(hb, wu_ref[s])
            a_ref[s] = a.astype(BF16)
            b_ref[s] = b.astype(BF16)
            sv = a * _sigmoid(a) * b
            acc += _dot(sv.astype(BF16), wd_ref[s])
        xo_ref[...] = xv + 0.5 * acc

    return pl.pallas_call(
        body, name=name, grid=(T // TM,),
        in_specs=[_row_spec(TM, D_MODEL), _const_spec((1, D_MODEL)), _const_spec(wg.shape), _const_spec(wu.shape),
                  _const_spec(wd.shape)],
        out_specs=[_row_spec(TM, D_MODEL), _row_spec(TM, D_MODEL), _row3_spec(TM, FF_S), _row3_spec(TM, FF_S)],
        out_shape=[jax.ShapeDtypeStruct((T, D_MODEL), F32), jax.ShapeDtypeStruct((T, D_MODEL), BF16),
                   jax.ShapeDtypeStruct((N_SHARD, T, FF_S), BF16), jax.ShapeDtypeStruct((N_SHARD, T, FF_S), BF16)],
        compiler_params=_params(("arbitrary",)),
    )(x, g, wg, wu, wd)


def _ffn_dgrad(dout, x, a, b, g, wg, wu, wd, name):
    T = x.shape[0]

    def body(do_ref, x_ref, a_ref, b_ref, g_ref, wg_ref, wu_ref, wd_ref, dx_ref, da_ref, db_ref, dg_ref):
        do = do_ref[...]
        dob = do.astype(BF16)
        dh = jnp.zeros((TM, D_MODEL), F32)
        for s in range(N_SHARD):
            ds = 0.5 * _dot_nt(dob, wd_ref[s])
            av = a_ref[s].astype(F32)
            bv = b_ref[s].astype(F32)
            sig = _sigmoid(av)
            da = (ds * bv * (sig * (1.0 + av * (1.0 - sig)))).astype(BF16)
            db = (ds * (av * sig)).astype(BF16)
            da_ref[s] = da
            db_ref[s] = db
            dh += _dot(da, wg_ref[s]) + _dot(db, wu_ref[s])
        gv = g_ref[...]
        xhat, r, _ = _rms_fwd(x_ref[...], gv)
        dxn, dg = _rms_bwd(dh, xhat, r, gv)
        dx_ref[...] = do + dxn

        @pl.when(pl.program_id(0) == 0)
        def _():
            dg_ref[...] = jnp.zeros_like(dg_ref)

        dg_ref[...] += dg

    return pl.pallas_call(
        body, name=name, grid=(T // TM,),
        in_specs=[_row_spec(TM, D_MODEL), _row_spec(TM, D_MODEL), _row3_spec(TM, FF_S), _row3_spec(TM, FF_S),
                  _const_spec((1, D_MODEL)), _const_spec(wg.shape), _const_spec(wu.shape), _const_spec(wd.shape)],
        out_specs=[_row_spec(TM, D_MODEL), _row3_spec(TM, FF_S), _row3_spec(TM, FF_S), _acc_spec((1, D_MODEL))],
        out_shape=[jax.ShapeDtypeStruct((T, D_MODEL), F32), jax.ShapeDtypeStruct((N_SHARD, T, FF_S), BF16),
                   jax.ShapeDtypeStruct((N_SHARD, T, FF_S), BF16), jax.ShapeDtypeStruct((1, D_MODEL), F32)],
        compiler_params=_params(("arbitrary",)),
    )(dout, x, a, b, g, wg, wu, wd)


def _ffn_wgrad(h, dout, a, b, da, db, name):
    T = h.shape[0]

    def body(h_ref, do_ref, a_ref, b_ref, da_ref, db_ref, gwg_ref, gwu_ref, gwd_ref):
        @pl.when(pl.program_id(1) == 0)
        def _():
            gwg_ref[...] = jnp.zeros_like(gwg_ref)
            gwu_ref[...] = jnp.zeros_like(gwu_ref)
            gwd_ref[...] = jnp.zeros_like(gwd_ref)

        hv = h_ref[...]
        dob = do_ref[...].astype(BF16)
        av = a_ref[0].astype(F32)
        sv = (0.5 * av * _sigmoid(av) * b_ref[0].astype(F32)).astype(BF16)
        gwg_ref[0] += _dot_tn(da_ref[0], hv)
        gwu_ref[0] += _dot_tn(db_ref[0], hv)
        gwd_ref[0] += _dot_tn(sv, dob)

    tok = pl.BlockSpec((TW, D_MODEL), lambda s, i: (i, 0))
    act = pl.BlockSpec((1, TW, FF_S), lambda s, i: (s, i, 0))
    return pl.pallas_call(
        body, name=name, grid=(N_SHARD, T // TW),
        in_specs=[tok, tok, act, act, act, act],
        out_specs=[pl.BlockSpec((1, FF_S, D_MODEL), lambda s, i: (s, 0, 0))] * 3,
        out_shape=[jax.ShapeDtypeStruct((N_SHARD, FF_S, D_MODEL), F32)] * 3,
        compiler_params=_params(("arbitrary", "arbitrary")),
    )(h, dout, a, b, da, db)


def _in_fwd(x, g, w_in):
    T = x.shape[0]

    def body(x_ref, g_ref, w_ref, h_ref, qkv_ref, zs_ref, gl_ref):
        hb = _rms_fwd(x_ref[...], g_ref[...])[2].astype(BF16)
        h_ref[...] = hb
        z0 = _dot(hb, w_ref[0])
        qkv_ref[:, 0:IN_S] = z0.astype(BF16)
        z1 = _dot(hb, w_ref[1])
        qkv_ref[:, IN_S:3 * D_ATT] = z1[:, 0:384].astype(BF16)
        zs_ref[:, 0:768] = z1[:, 384:IN_S]
        z2 = _dot(hb, w_ref[2])
        zs_ref[:, 768:1024] = z2[:, 0:256]
        gl_ref[:, 0:896] = z2[:, 256:IN_S]
        gl_ref[:, 896:2048] = _dot(hb, w_ref[3])

    return pl.pallas_call(
        body, name="in_fwd", grid=(T // TM,),
        in_specs=[_row_spec(TM, D_MODEL), _const_spec((1, D_MODEL)), _const_spec(w_in.shape)],
        out_specs=[_row_spec(TM, D_MODEL), _row_spec(TM, 3 * D_ATT), _row_spec(TM, 2 * D_SGU), _row_spec(TM, 2 * D_MODEL)],
        out_shape=[jax.ShapeDtypeStruct((T, D_MODEL), BF16), jax.ShapeDtypeStruct((T, 3 * D_ATT), BF16),
                   jax.ShapeDtypeStruct((T, 2 * D_SGU), F32), jax.ShapeDtypeStruct((T, 2 * D_MODEL), F32)],
        compiler_params=_params(("arbitrary",)),
    )(x, g, w_in)


def _in_dgrad(dx_res, x, g, w_in, dq, dk, dv, dzs, dgl):
    T = x.shape[0]

    def body(dxr_ref, x_ref, g_ref, w_ref, dq_ref, dk_ref, dv_ref, dzs_ref, dgl_ref, dx_ref, dz_ref, dg_ref):
        dz = jnp.concatenate([dq_ref[...], dk_ref[...].astype(BF16), dv_ref[...].astype(BF16), dzs_ref[...], dgl_ref[...]],
                             axis=1)
        dz_ref[...] = dz
        dh = jnp.zeros((TM, D_MODEL), F32)
        for s in range(N_SHARD):
            dh += _dot_nt(dz[:, s * IN_S:(s + 1) * IN_S], w_ref[s])
        gv = g_ref[...]
        xhat, r, _ = _rms_fwd(x_ref[...], gv)
        dxn, dg = _rms_bwd(dh, xhat, r, gv)
        dx_ref[...] = dxr_ref[...] + dxn

        @pl.when(pl.program_id(0) == 0)
        def _():
            dg_ref[...] = jnp.zeros_like(dg_ref)

        dg_ref[...] += dg

    pad_blocks = ATT_PAD // TM
    return pl.pallas_call(
        body, name="in_dgrad", grid=(T // TM,),
        in_specs=[_row_spec(TM, D_MODEL), _row_spec(TM, D_MODEL), _const_spec((1, D_MODEL)), _const_spec(w_in.shape),
                  _row_spec(TM, D_ATT), _row_spec(TM, D_ATT, pad_blocks), _row_spec(TM, D_ATT, pad_blocks),
                  _row_spec(TM, 2 * D_SGU), _row_spec(TM, 2 * D_MODEL)],
        out_specs=[_row_spec(TM, D_MODEL), _row_spec(TM, D_IN), _acc_spec((1, D_MODEL))],
        out_shape=[jax.ShapeDtypeStruct((T, D_MODEL), F32), jax.ShapeDtypeStruct((T, D_IN), BF16),
                   jax.ShapeDtypeStruct((1, D_MODEL), F32)],
        compiler_params=_params(("arbitrary",)),
    )(dx_res, x, g, w_in, dq, dk, dv, dzs, dgl)


def _in_wgrad(h, dz):
    T = h.shape[0]

    def body(h_ref, dz_ref, gw_ref):
        @pl.when(pl.program_id(1) == 0)
        def _():
            gw_ref[...] = jnp.zeros_like(gw_ref)

        gw_ref[0] += _dot_tn(h_ref[...], dz_ref[...])

    return pl.pallas_call(
        body, name="in_wgrad", grid=(N_SHARD, T // TW),
        in_specs=[pl.BlockSpec((TW, D_MODEL), lambda s, i: (i, 0)), pl.BlockSpec((TW, IN_S), lambda s, i: (i, s))],
        out_specs=pl.BlockSpec((1, D_MODEL, IN_S), lambda s, i: (s, 0, 0)),
        out_shape=jax.ShapeDtypeStruct((N_SHARD, D_MODEL, IN_S), F32),
        compiler_params=_params(("arbitrary", "arbitrary")),
    )(h, dz)


def _rel_onehot():
    r = lax.broadcasted_iota(jnp.int32, (REL_PAD, REL_PAD), 0)
    n = lax.broadcasted_iota(jnp.int32, (REL_PAD, REL_PAD), 1)
    idx = jnp.clip(BAND - 1 - n, -REL_CLIP, REL_CLIP) + REL_CLIP
    return jnp.where(r == idx, 1.0, 0.0).astype(BF16)


def _split3(v):
    p1 = v.astype(BF16)
    r1 = v - p1.astype(F32)
    p2 = r1.astype(BF16)
    p3 = (r1 - p2.astype(F32)).astype(BF16)
    return p1, p2, p3


def _relbias_fwd(tab_pad):
    def body(t_ref, o_ref):
        oh = _rel_onehot()
        acc = jnp.zeros((HEADS, REL_PAD), F32)
        for p in _split3(t_ref[...]):
            acc += _dot(p, oh)
        o_ref[...] = acc

    return pl.pallas_call(body, name="relbias_fwd", out_shape=jax.ShapeDtypeStruct((HEADS, REL_PAD), F32))(tab_pad)


def _relbias_bwd(z):
    def body(z_ref, o_ref):
        oh = _rel_onehot()
        dt2 = jnp.sum(z_ref[...], axis=1)
        acc = jnp.zeros((HEADS, REL_PAD), F32)
        for p in _split3(dt2):
            acc += _dot_nt(p, oh)
        o_ref[...] = acc

    return pl.pallas_call(body, name="relbias_bwd", out_shape=jax.ShapeDtypeStruct((HEADS, REL_PAD), F32))(z)


def _bias_blocks(t2):
    flat = jnp.tile(t2, (1, CHUNK))
    skew = flat[:, :CHUNK * (REL_PAD - 1)].reshape(HEADS, CHUNK, REL_PAD - 1)
    bias = skew[:, :, CHUNK - 1:CHUNK - 1 + BAND]
    slabs = [jnp.pad(bias, ((0, 0), (0, 0), (CHUNK * c, ATT_KEYS - BAND - CHUNK * c)), constant_values=NEG_INF)
             for c in range(2)]
    return jnp.concatenate(slabs, axis=1)


def _unskew(db2):
    out = []
    for c in range(2):
        slab = db2[:, CHUNK * c:CHUNK * (c + 1), CHUNK * c:CHUNK * c + BAND]
        y = jnp.pad(slab, ((0, 0), (0, 0), (CHUNK - 1, REL_PAD - BAND - CHUNK + 1)))
        yf = jnp.pad(y.reshape(HEADS, CHUNK * REL_PAD), ((0, 0), (0, CHUNK)))
        out.append(yf.reshape(HEADS, CHUNK, REL_PAD + 1)[:, :, :REL_PAD])
    return jnp.concatenate(out, axis=1)


def _att_load(qkv_hbm, q_s, k_s, v_s, sem, T):
    copies = [pltpu.make_async_copy(qkv_hbm.at[:, 0:D_ATT], q_s, sem.at[0]),
              pltpu.make_async_copy(qkv_hbm.at[:, D_ATT:2 * D_ATT], k_s.at[pl.ds(ATT_PAD, T), :], sem.at[1]),
              pltpu.make_async_copy(qkv_hbm.at[:, 2 * D_ATT:3 * D_ATT], v_s.at[pl.ds(ATT_PAD, T), :], sem.at[2])]
    for cp in copies:
        cp.start()
    k_s[0:ATT_PAD, :] = jnp.zeros((ATT_PAD, D_ATT), BF16)
    v_s[0:ATT_PAD, :] = jnp.zeros((ATT_PAD, D_ATT), BF16)
    for cp in copies:
        cp.wait()


def _att_probs(qh, kh, bias, valid):
    s = _dot_nt(qh, kh) * (HEAD_DIM ** -0.5) + bias
    s = jnp.where(valid, s, NEG_INF)
    e = jnp.exp(s - jnp.max(s, axis=-1, keepdims=True))
    return e / jnp.sum(e, axis=-1, keepdims=True)


def _att_fwd(qkv, bias2):
    T = qkv.shape[0]

    def body(qkv_hbm, bias_ref, y_ref, q_s, k_s, v_s, sem):
        _att_load(qkv_hbm, q_s, k_s, v_s, sem, T)

        def block(i, carry):
            r0 = pl.multiple_of(i * ATT_ROWS, ATT_ROWS)
            q = q_s[pl.ds(r0, ATT_ROWS), :]
            kw = k_s[pl.ds(r0, ATT_KEYS), :]
            vw = v_s[pl.ds(r0, ATT_KEYS), :]
            valid = (lax.broadcasted_iota(jnp.int32, (1, ATT_KEYS), 1) + (r0 - ATT_PAD)) >= 0
            outs = []
            for h in range(HEADS):
                hs = slice(h * HEAD_DIM, (h + 1) * HEAD_DIM)
                p = _att_probs(q[:, hs], kw[:, hs], bias_ref[h], valid)
                outs.append(_dot(p.astype(BF16), vw[:, hs]))
            y_ref[pl.ds(r0, ATT_ROWS), :] = jnp.concatenate(outs, axis=1).astype(BF16)
            return carry

        lax.fori_loop(0, T // ATT_ROWS, block, 0)

    return pl.pallas_call(
        body, name="att_fwd",
        in_specs=[pl.BlockSpec(memory_space=pl.ANY), pl.BlockSpec(memory_space=pltpu.VMEM)],
        out_specs=pl.BlockSpec(memory_space=pltpu.VMEM),
        out_shape=jax.ShapeDtypeStruct((T, D_ATT), BF16),
        scratch_shapes=[pltpu.VMEM((T, D_ATT), BF16), pltpu.VMEM((T + ATT_PAD, D_ATT), BF16),
                        pltpu.VMEM((T + ATT_PAD, D_ATT), BF16), pltpu.SemaphoreType.DMA((3,))],
        compiler_params=_params(),
    )(qkv, bias2)


def _att_bwd(qkv, dy, bias2):
    T = qkv.shape[0]

    def body(qkv_hbm, dy_ref, bias_ref, dq_ref, dk_ref, dv_ref, db_ref, q_s, k_s, v_s, sem):
        _att_load(qkv_hbm, q_s, k_s, v_s, sem, T)
        dk_ref[...] = jnp.zeros_like(dk_ref)
        dv_ref[...] = jnp.zeros_like(dv_ref)
        db_ref[...] = jnp.zeros_like(db_ref)

        def block(i, carry):
            r0 = pl.multiple_of(i * ATT_ROWS, ATT_ROWS)
            q = q_s[pl.ds(r0, ATT_ROWS), :]
            kw = k_s[pl.ds(r0, ATT_KEYS), :]
            vw = v_s[pl.ds(r0, ATT_KEYS), :]
            dyb = dy_ref[pl.ds(r0, ATT_ROWS), :]
            valid = (lax.broadcasted_iota(jnp.int32, (1, ATT_KEYS), 1) + (r0 - ATT_PAD)) >= 0
            dqs, dks, dvs = [], [], []
            for h in range(HEADS):
                hs = slice(h * HEAD_DIM, (h + 1) * HEAD_DIM)
                qh, kh, vh, dyh = q[:, hs], kw[:, hs], vw[:, hs], dyb[:, hs]
                p = _att_probs(qh, kh, bias_ref[h], valid)
                dp = _dot_nt(dyh, vh)
                ds = p * (dp - jnp.sum(p * dp, axis=-1, keepdims=True))
                db_ref[h] += ds
                dsb = (ds * (HEAD_DIM ** -0.5)).astype(BF16)
                dqs.append(_dot(dsb, kh))
                dks.append(_dot_tn(dsb, qh))
                dvs.append(_dot_tn(p.astype(BF16), dyh))
            dq_ref[pl.ds(r0, ATT_ROWS), :] = jnp.concatenate(dqs, axis=1).astype(BF16)
            dk_ref[pl.ds(r0, ATT_KEYS), :] += jnp.concatenate(dks, axis=1)
            dv_ref[pl.ds(r0, ATT_KEYS), :] += jnp.concatenate(dvs, axis=1)
            return carry

        lax.fori_loop(0, T // ATT_ROWS, block, 0)

    vmem = pl.BlockSpec(memory_space=pltpu.VMEM)
    return pl.pallas_call(
        body, name="att_bwd",
        in_specs=[pl.BlockSpec(memory_space=pl.ANY), vmem, vmem],
        out_specs=[vmem, vmem, vmem, vmem],
        out_shape=[jax.ShapeDtypeStruct((T, D_ATT), BF16), jax.ShapeDtypeStruct((T + ATT_PAD, D_ATT), F32),
                   jax.ShapeDtypeStruct((T + ATT_PAD, D_ATT), F32), jax.ShapeDtypeStruct((HEADS, ATT_ROWS, ATT_KEYS), F32)],
        scratch_shapes=[pltpu.VMEM((T, D_ATT), BF16), pltpu.VMEM((T + ATT_PAD, D_ATT), BF16),
                        pltpu.VMEM((T + ATT_PAD, D_ATT), BF16), pltpu.SemaphoreType.DMA((3,))],
        compiler_params=_params(),
    )(qkv, dy, bias2)


_GELU_C = 0.7978845608028654
_GELU_A = 0.044715


def _gelu(x):
    t = jnp.tanh(_GELU_C * (x + _GELU_A * x * x * x))
    return 0.5 * x * (1.0 + t), t


def _gelu_grad(x, t):
    return 0.5 * (1.0 + t) + 0.5 * x * (1.0 - t * t) * _GELU_C * (1.0 + 3.0 * _GELU_A * x * x)


def _group_masks():
    col = lax.broadcasted_iota(jnp.int32, (SGU_GROUPS, D_SGU), 1) // SGU_GDIM
    grp = lax.broadcasted_iota(jnp.int32, (SGU_GROUPS, D_SGU), 0)
    return jnp.where(col == grp, 1.0, 0.0).astype(F32)


def _causal_mask(transposed=False):
    i = lax.broadcasted_iota(jnp.int32, (SGU_BLOCK, SGU_BLOCK), 0) // CHUNK
    j = lax.broadcasted_iota(jnp.int32, (SGU_BLOCK, SGU_BLOCK), 1) // CHUNK
    return (j >= i) if transposed else (i >= j)


def _sgu_norm(zs, lng, lnb):
    gz, t = _gelu(zs)
    u = gz[:, 0:D_SGU]
    vs = gz[:, D_SGU:2 * D_SGU]
    xc = vs - jnp.mean(vs, axis=-1, keepdims=True)
    rstd = lax.rsqrt(jnp.mean(xc * xc, axis=-1, keepdims=True) + EPS)
    xhat = xc * rstd
    return t, u, xhat, rstd, xhat * lng + lnb


def _sgu_mix(vn_blk, w_ref, bst, gm):
    mask = _causal_mask()
    s = jnp.zeros((SGU_BLOCK, D_SGU), F32)
    for g in range(SGU_GROUPS):
        wm = jnp.where(mask, w_ref[g], 0.0).astype(BF16)
        s += _dot(wm, (vn_blk * gm[g:g + 1, :]).astype(BF16))
        s += bst[:, g:g + 1] * gm[g:g + 1, :]
    return s


def _sgu_fwd(zs, lng, lnb, w_s, bst):
    T = zs.shape[0]
    nblk = TM // SGU_BLOCK

    def body(zs_ref, lng_ref, lnb_ref, w_ref, bst_ref, y_ref):
        _, u, _, _, vn = _sgu_norm(zs_ref[...], lng_ref[...], lnb_ref[...])
        gm = _group_masks()
        bst_v = bst_ref[...]
        for n in range(nblk):
            rows = slice(n * SGU_BLOCK, (n + 1) * SGU_BLOCK)
            s = _sgu_mix(vn[rows], w_ref, bst_v, gm)
            y_ref[rows, :] = (u[rows] * s).astype(BF16)

    return pl.pallas_call(
        body, name="sgu_fwd", grid=(T // TM,),
        in_specs=[_row_spec(TM, 2 * D_SGU), _const_spec((1, D_SGU)), _const_spec((1, D_SGU)),
                  _const_spec(w_s.shape), _const_spec(bst.shape)],
        out_specs=_row_spec(TM, D_SGU),
        out_shape=jax.ShapeDtypeStruct((T, D_SGU), BF16),
        compiler_params=_params(("arbitrary",)),
    )(zs, lng, lnb, w_s, bst)


def _sgu_bwd(zs, dy, lng, lnb, w_s, w_st, bst):
    T = zs.shape[0]
    nblk = TM // SGU_BLOCK

    def body(zs_ref, dy_ref, lng_ref, lnb_ref, w_ref, wt_ref, bst_ref, dzs_ref, dw_ref, dbt_ref, dlg_ref, dlb_ref):
        @pl.when(pl.program_id(0) == 0)
        def _():
            dw_ref[...] = jnp.zeros_like(dw_ref)
            dbt_ref[...] = jnp.zeros_like(dbt_ref)
            dlg_ref[...] = jnp.zeros_like(dlg_ref)
            dlb_ref[...] = jnp.zeros_like(dlb_ref)

        zs_v = zs_ref[...]
        lng_v = lng_ref[...]
        t, u, xhat, rstd, vn = _sgu_norm(zs_v, lng_v, lnb_ref[...])
        gm = _group_masks()
        bst_v = bst_ref[...]
        mask = _causal_mask()
        mask_t = _causal_mask(transposed=True)
        dyv = dy_ref[...].astype(F32)
        lane8 = lax.broadcasted_iota(jnp.int32, (1, SGU_GROUPS), 1)
        du_rows, dvn_rows = [], []
        for n in range(nblk):
            rows = slice(n * SGU_BLOCK, (n + 1) * SGU_BLOCK)
            vn_b = vn[rows]
            s = _sgu_mix(vn_b, w_ref, bst_v, gm)
            du_rows.append(dyv[rows] * s)
            dsb = dyv[rows] * u[rows]
            vnb16 = vn_b.astype(BF16)
            dvn = jnp.zeros((SGU_BLOCK, D_SGU), F32)
            dbt = jnp.zeros((SGU_BLOCK, SGU_GROUPS), F32)
            for g in range(SGU_GROUPS):
                dsg = dsb * gm[g:g + 1, :]
                dsg16 = dsg.astype(BF16)
                wmt = jnp.where(mask_t, wt_ref[g], 0.0).astype(BF16)
                dvn += _dot(wmt, dsg16)
                dw_ref[g] += jnp.where(mask, _dot_nt(dsg16, vnb16), 0.0)
                dbt += jnp.sum(dsg, axis=-1, keepdims=True) * jnp.where(lane8 == g, 1.0, 0.0)
            dbt_ref[...] += dbt
            dvn_rows.append(dvn)
        du = jnp.concatenate(du_rows, axis=0)
        dvn = jnp.concatenate(dvn_rows, axis=0)
        dlg_ref[...] += jnp.sum(dvn * xhat, axis=0, keepdims=True)
        dlb_ref[...] += jnp.sum(dvn, axis=0, keepdims=True)
        dxhat = dvn * lng_v
        dvs = rstd * (dxhat - jnp.mean(dxhat, axis=-1, keepdims=True)
                      - xhat * jnp.mean(dxhat * xhat, axis=-1, keepdims=True))
        dgz = jnp.concatenate([du, dvs], axis=1)
        dzs_ref[...] = (dgz * _gelu_grad(zs_v, t)).astype(BF16)

    return pl.pallas_call(
        body, name="sgu_bwd", grid=(T // TM,),
        in_specs=[_row_spec(TM, 2 * D_SGU), _row_spec(TM, D_SGU), _const_spec((1, D_SGU)), _const_spec((1, D_SGU)),
                  _const_spec(w_s.shape), _const_spec(w_st.shape), _const_spec(bst.shape)],
        out_specs=[_row_spec(TM, 2 * D_SGU), _acc_spec(w_s.shape), _acc_spec(bst.shape), _acc_spec((1, D_SGU)),
                   _acc_spec((1, D_SGU))],
        out_shape=[jax.ShapeDtypeStruct((T, 2 * D_SGU), BF16), jax.ShapeDtypeStruct(w_s.shape, F32),
                   jax.ShapeDtypeStruct(bst.shape, F32), jax.ShapeDtypeStruct((1, D_SGU), F32),
                   jax.ShapeDtypeStruct((1, D_SGU), F32)],
        compiler_params=_params(("arbitrary",)),
    )(zs, dy, lng, lnb, w_s, w_st, bst)


def _cols(v, s):
    return v[:, s * BR_S:(s + 1) * BR_S]


def _merge_fwd(x, y_att, y_sgu, gl, b_gate, wba, wbs, wo):
    T = x.shape[0]

    def body(x_ref, ya_ref, ys_ref, gl_ref, bg_ref, wba_ref, wbs_ref, wo_ref, xo_ref, m_ref, pa_ref, ps_ref):
        ya = ya_ref[...]
        ys = ys_ref[...]
        pa = jnp.concatenate([_dot(ya, wba_ref[s]) for s in range(N_SHARD)], axis=1)
        ps = jnp.concatenate([_dot(ys, wbs_ref[s]) for s in range(N_SHARD)], axis=1)
        g = _sigmoid(gl_ref[...] + bg_ref[...])
        mb = (g[:, 0:D_MODEL] * pa + g[:, D_MODEL:2 * D_MODEL] * ps).astype(BF16)
        m_ref[...] = mb
        pa_ref[...] = pa.astype(BF16)
        ps_ref[...] = ps.astype(BF16)
        acc = jnp.zeros((TM, D_MODEL), F32)
        for s in range(N_SHARD):
            acc += _dot(_cols(mb, s), wo_ref[s])
        xo_ref[...] = x_ref[...] + acc

    tokd = jax.ShapeDtypeStruct((T, D_MODEL), BF16)
    return pl.pallas_call(
        body, name="merge_fwd", grid=(T // TM,),
        in_specs=[_row_spec(TM, D_MODEL), _row_spec(TM, D_ATT), _row_spec(TM, D_SGU), _row_spec(TM, 2 * D_MODEL),
                  _const_spec((1, 2 * D_MODEL)), _const_spec(wba.shape), _const_spec(wbs.shape), _const_spec(wo.shape)],
        out_specs=[_row_spec(TM, D_MODEL)] * 4,
        out_shape=[jax.ShapeDtypeStruct((T, D_MODEL), F32), tokd, tokd, tokd],
        compiler_params=_params(("arbitrary",)),
    )(x, y_att, y_sgu, gl, b_gate, wba, wbs, wo)


def _merge_bwd(dx, y_att, y_sgu, gl, merged, pa, ps, b_gate, wba, wbs, wo):
    T = dx.shape[0]

    def body(dx_ref, ya_ref, ys_ref, gl_ref, m_ref, pa_ref, ps_ref, bg_ref, wba_ref, wbs_ref, wo_ref,
             dya_ref, dys_ref, dgl_ref, dbg_ref, gwba_ref, gwbs_ref, gwo_ref):
        @pl.when(pl.program_id(0) == 0)
        def _():
            dbg_ref[...] = jnp.zeros_like(dbg_ref)
            gwba_ref[...] = jnp.zeros_like(gwba_ref)
            gwbs_ref[...] = jnp.zeros_like(gwbs_ref)
            gwo_ref[...] = jnp.zeros_like(gwo_ref)

        dxb = dx_ref[...].astype(BF16)
        dm = jnp.concatenate([_dot_nt(dxb, wo_ref[s]) for s in range(N_SHARD)], axis=1)
        g = _sigmoid(gl_ref[...] + bg_ref[...])
        ga = g[:, 0:D_MODEL]
        gs = g[:, D_MODEL:2 * D_MODEL]
        dpa = (dm * ga).astype(BF16)
        dps = (dm * gs).astype(BF16)
        dgl = jnp.concatenate([dm * pa_ref[...].astype(F32) * ga * (1.0 - ga),
                               dm * ps_ref[...].astype(F32) * gs * (1.0 - gs)], axis=1)
        dgl_ref[...] = dgl.astype(BF16)
        dbg_ref[...] += jnp.sum(dgl, axis=0, keepdims=True)
        ya = ya_ref[...]
        ys = ys_ref[...]
        mb = m_ref[...]
        dya = jnp.zeros((TM, D_ATT), F32)
        dys = jnp.zeros((TM, D_SGU), F32)
        for s in range(N_SHARD):
            dya += _dot_nt(_cols(dpa, s), wba_ref[s])
            dys += _dot_nt(_cols(dps, s), wbs_ref[s])
            gwo_ref[s] += _dot_tn(_cols(mb, s), dxb)
            gwba_ref[s] += _dot_tn(ya, _cols(dpa, s))
            gwbs_ref[s] += _dot_tn(ys, _cols(dps, s))
        dya_ref[...] = dya.astype(BF16)
        dys_ref[...] = dys.astype(BF16)

    return pl.pallas_call(
        body, name="merge_bwd", grid=(T // TM,),
        in_specs=[_row_spec(TM, D_MODEL), _row_spec(TM, D_ATT), _row_spec(TM, D_SGU), _row_spec(TM, 2 * D_MODEL),
                  _row_spec(TM, D_MODEL), _row_spec(TM, D_MODEL), _row_spec(TM, D_MODEL),
                  _const_spec((1, 2 * D_MODEL)), _const_spec(wba.shape), _const_spec(wbs.shape), _const_spec(wo.shape)],
        out_specs=[_row_spec(TM, D_ATT), _row_spec(TM, D_SGU), _row_spec(TM, 2 * D_MODEL), _acc_spec((1, 2 * D_MODEL)),
                   _acc_spec(wba.shape), _acc_spec(wbs.shape), _acc_spec(wo.shape)],
        out_shape=[jax.ShapeDtypeStruct((T, D_ATT), BF16), jax.ShapeDtypeStruct((T, D_SGU), BF16),
                   jax.ShapeDtypeStruct((T, 2 * D_MODEL), BF16), jax.ShapeDtypeStruct((1, 2 * D_MODEL), F32),
                   jax.ShapeDtypeStruct(wba.shape, F32), jax.ShapeDtypeStruct(wbs.shape, F32),
                   jax.ShapeDtypeStruct(wo.shape, F32)],
        compiler_params=_params(("arbitrary",)),
    )(dx, y_att, y_sgu, gl, merged, pa, ps, b_gate, wba, wbs, wo)


def _loss_bwd(x, target, g):
    T = x.shape[0]

    def body(x_ref, t_ref, g_ref, dx_ref, loss_ref, dg_ref):
        @pl.when(pl.program_id(0) == 0)
        def _():
            loss_ref[...] = jnp.zeros_like(loss_ref)
            dg_ref[...] = jnp.zeros_like(dg_ref)

        gv = g_ref[...]
        xhat, r, y = _rms_fwd(x_ref[...], gv)
        err = y - t_ref[...]
        per_tok = jnp.mean(err * err, axis=-1, keepdims=True)
        loss_ref[...] += 0.5 * jnp.sum(per_tok, axis=0, keepdims=True)
        dxn, dg = _rms_bwd(err * (1.0 / D_MODEL), xhat, r, gv)
        dx_ref[...] = dxn
        dg_ref[...] += dg

    return pl.pallas_call(
        body, name="loss_bwd", grid=(T // TM,),
        in_specs=[_row_spec(TM, D_MODEL), _row_spec(TM, D_MODEL), _const_spec((1, D_MODEL))],
        out_specs=[_row_spec(TM, D_MODEL), _acc_spec((1, 128)), _acc_spec((1, D_MODEL))],
        out_shape=[jax.ShapeDtypeStruct((T, D_MODEL), F32), jax.ShapeDtypeStruct((1, 128), F32),
                   jax.ShapeDtypeStruct((1, D_MODEL), F32)],
        compiler_params=_params(("arbitrary",)),
    )(x, target, g)


BIG = ("ffn1_w_gate", "ffn1_w_up", "ffn1_w_down", "w_in", "w_branch_att", "w_branch_sgu", "w_out",
       "ffn2_w_gate", "ffn2_w_up", "ffn2_w_down")
SMALL = ("norm_ffn1", "norm_mix", "b_gate", "rel_bias", "sgu_ln_g", "sgu_ln_b", "sgu_w_s", "sgu_b_s", "norm_ffn2",
         "norm_final")


def _local_step(x, target, wb, ws):
    t2 = _relbias_fwd(ws["rel_bias"])
    bias2 = _bias_blocks(t2)
    bst = ws["sgu_b_s"].T
    w_st = jnp.swapaxes(ws["sgu_w_s"], 1, 2)

    x1, h1, a1, b1 = _ffn_fwd(x, ws["norm_ffn1"], wb["ffn1_w_gate"], wb["ffn1_w_up"], wb["ffn1_w_down"], "ffn1_fwd")
    h2, qkv, zs, gl = _in_fwd(x1, ws["norm_mix"], wb["w_in"])
    y_att = _att_fwd(qkv, bias2)
    y_sgu = _sgu_fwd(zs, ws["sgu_ln_g"], ws["sgu_ln_b"], ws["sgu_w_s"], bst)
    x2, merged, pa, ps = _merge_fwd(x1, y_att, y_sgu, gl, ws["b_gate"], wb["w_branch_att"], wb["w_branch_sgu"],
                                    wb["w_out"])
    x3, h3, a3, b3 = _ffn_fwd(x2, ws["norm_ffn2"], wb["ffn2_w_gate"], wb["ffn2_w_up"], wb["ffn2_w_down"], "ffn2_fwd")
    dx3, loss, g_final = _loss_bwd(x3, target, ws["norm_final"])

    gb, gs = {}, {"norm_final": g_final}
    dx2, da3, db3, gs["norm_ffn2"] = _ffn_dgrad(dx3, x2, a3, b3, ws["norm_ffn2"], wb["ffn2_w_gate"], wb["ffn2_w_up"],
                                                wb["ffn2_w_down"], "ffn2_dgrad")
    gb["ffn2_w_gate"], gb["ffn2_w_up"], gb["ffn2_w_down"] = _ffn_wgrad(h3, dx3, a3, b3, da3, db3, "ffn2_wgrad")
    dy_att, dy_sgu, dgl, gs["b_gate"], gb["w_branch_att"], gb["w_branch_sgu"], gb["w_out"] = _merge_bwd(
        dx2, y_att, y_sgu, gl, merged, pa, ps, ws["b_gate"], wb["w_branch_att"], wb["w_branch_sgu"], wb["w_out"])
    dq, dk, dv, db2 = _att_bwd(qkv, dy_att, bias2)
    gs["rel_bias"] = _relbias_bwd(_unskew(db2))
    dzs, gs["sgu_w_s"], dbt, gs["sgu_ln_g"], gs["sgu_ln_b"] = _sgu_bwd(zs, dy_sgu, ws["sgu_ln_g"], ws["sgu_ln_b"],
                                                                      ws["sgu_w_s"], w_st, bst)
    gs["sgu_b_s"] = dbt.T
    dx1, dz, gs["norm_mix"] = _in_dgrad(dx2, x1, ws["norm_mix"], wb["w_in"], dq, dk, dv, dzs, dgl)
    gb["w_in"] = _in_wgrad(h2, dz)
    gx, da1, db1, gs["norm_ffn1"] = _ffn_dgrad(dx1, x, a1, b1, ws["norm_ffn1"], wb["ffn1_w_gate"], wb["ffn1_w_up"],
                                               wb["ffn1_w_down"], "ffn1_dgrad")
    gb["ffn1_w_gate"], gb["ffn1_w_up"], gb["ffn1_w_down"] = _ffn_wgrad(h1, dx1, a1, b1, da1, db1, "ffn1_wgrad")
    return loss, gx, gb, gs


_ANY = pl.BlockSpec(memory_space=pl.ANY)
_VMEM = pl.BlockSpec(memory_space=pltpu.VMEM)


def _mesh_pos():
    return lax.axis_index("x"), lax.axis_index("y"), lax.axis_index("c")


def _cast_slots(shards, chip, name):
    n = len(shards)
    r, ncol = shards[0].shape
    tr = r // 2

    def body(me_ref, *refs):
        for i_ref, o_ref in zip(refs[:n], refs[n:]):
            o_ref[0] = i_ref[...].astype(BF16)

    grid_spec = pltpu.PrefetchScalarGridSpec(
        num_scalar_prefetch=1, grid=(r // tr,),
        in_specs=[pl.BlockSpec((tr, ncol), lambda i, me: (i, 0))] * n,
        out_specs=[pl.BlockSpec((1, tr, ncol), lambda i, me: (me[0], i, 0))] * n)
    return pl.pallas_call(
        body, name=name, grid_spec=grid_spec,
        out_shape=[jax.ShapeDtypeStruct((N_SHARD, r, ncol), BF16)] * n,
        compiler_params=_params(("arbitrary",)),
    )(chip, *shards)


def _ag_copies(slots, refs, sems, pos):
    send_i, recv_i, send_d, recv_d = sems
    x, y, c = pos
    me = 2 * x + y
    n = len(slots)

    def half(w, core):
        rh = slots[w].shape[1] // 2
        return pl.ds(core * rh, rh)

    def remote(src, dst, ssem, rsem, dev):
        return pltpu.make_async_remote_copy(src_ref=src, dst_ref=dst, send_sem=ssem, recv_sem=rsem, device_id=dev,
                                            device_id_type=MESH)

    sent = []

    def start():
        for w in range(n):
            for j in range(3):
                t = (me + 1 + j) % N_SHARD
                mine = refs[w].at[me, half(w, c), :]
                cp = remote(mine, mine, send_i.at[3 * w + j], recv_i.at[3 * w + j], (t // 2, t % 2, c))
                cp.start()
                sent.append(cp)

    def forward():
        for w in range(n):
            for j in range(3):
                s = (me + 3 - j) % N_SHARD
                land = refs[w].at[s, half(w, c), :]
                remote(land, land, send_i.at[3 * w + j], recv_i.at[3 * w + j], (x, y, c)).wait_recv()
                cp = remote(land, land, send_d.at[3 * w + j], recv_d.at[3 * w + j], (x, y, 1 - c))
                cp.start()
                sent.append(cp)

    def finish():
        for w in range(n):
            for j in range(3):
                s = (me + 3 - j) % N_SHARD
                land = refs[w].at[s, half(w, 1 - c), :]
                remote(land, land, send_d.at[3 * w + j], recv_d.at[3 * w + j], (x, y, c)).wait_recv()
        for cp in sent:
            cp.wait_send()

    return start, forward, finish


def _ag_sems(n):
    return [pltpu.SemaphoreType.DMA((3 * n,)) for _ in range(4)]


def _allgather(slots, name):
    n = len(slots)

    def body(*refs):
        start, forward, finish = _ag_copies(slots, refs[n:2 * n], refs[2 * n:], _mesh_pos())
        start()
        forward()
        finish()

    return pl.pallas_call(
        body, name=name,
        in_specs=[_ANY] * n, out_specs=[_ANY] * n,
        out_shape=[jax.ShapeDtypeStruct(s.shape, s.dtype) for s in slots],
        input_output_aliases={i: i for i in range(n)},
        scratch_shapes=_ag_sems(n),
    )(*slots)


def _pair_exchange(grads, small):
    n = len(grads)

    def body(*refs):
        ins, outs = refs[:n + 1], refs[n + 1:2 * n + 2]
        send, recv = refs[2 * n + 2:]
        x, y, c = _mesh_pos()
        cps = []
        for w in range(n + 1):
            if w < n:
                rh = grads[w].shape[1] // 2
                src = ins[w].at[:, pl.ds((1 - c) * rh, rh), :]
            else:
                src = ins[w]
            cp = pltpu.make_async_remote_copy(src_ref=src, dst_ref=outs[w], send_sem=send.at[w], recv_sem=recv.at[w],
                                              device_id=(x, y, 1 - c), device_id_type=MESH)
            cp.start()
            cps.append(cp)
        for cp in cps:
            cp.wait()

    out_shape = [jax.ShapeDtypeStruct((N_SHARD, g.shape[1] // 2, g.shape[2]), F32) for g in grads]
    out_shape.append(jax.ShapeDtypeStruct(small.shape, F32))
    return pl.pallas_call(
        body, name="pair_exchange",
        in_specs=[_ANY] * (n + 1), out_specs=[_ANY] * (n + 1), out_shape=out_shape,
        scratch_shapes=[pltpu.SemaphoreType.DMA((n + 1,)), pltpu.SemaphoreType.DMA((n + 1,))],
    )(*grads, small)


def _pair_add(g, rv, core, chip, name):
    _, r, ncol = g.shape
    rh = r // 2

    def body(c_ref, me_ref, g_ref, rv_ref, pf_ref, pb_ref, land_ref):
        s = g_ref[0] + rv_ref[0]
        sb = s.astype(BF16)
        pb_ref[0] = sb
        land_ref[0] = sb

        @pl.when(pl.program_id(0) == me_ref[0])
        def _():
            pf_ref[...] = s

    slot = pl.BlockSpec((1, rh, ncol), lambda s, c, me: (s, 0, 0))
    grid_spec = pltpu.PrefetchScalarGridSpec(
        num_scalar_prefetch=2, grid=(N_SHARD,),
        in_specs=[pl.BlockSpec((1, rh, ncol), lambda s, c, me: (s, c[0], 0)), slot],
        out_specs=[pl.BlockSpec((rh, ncol), lambda s, c, me: (0, 0)), slot, slot])
    return pl.pallas_call(
        body, name=name, grid_spec=grid_spec,
        out_shape=[jax.ShapeDtypeStruct((rh, ncol), F32), jax.ShapeDtypeStruct((N_SHARD, rh, ncol), BF16),
                   jax.ShapeDtypeStruct((N_SHARD, rh, ncol), BF16)],
        compiler_params=_params(("arbitrary",)),
    )(core, chip, g, rv)


def _cx_copies(n, src_refs, land_refs, sems, pos):
    send, recv = sems
    x, y, c = pos
    me = 2 * x + y
    cps = []

    def start():
        for w in range(n):
            for j in range(3):
                t = (me + 1 + j) % N_SHARD
                cp = pltpu.make_async_remote_copy(src_ref=src_refs[w].at[t], dst_ref=land_refs[w].at[me],
                                                  send_sem=send.at[3 * w + j], recv_sem=recv.at[3 * w + j],
                                                  device_id=(t // 2, t % 2, c), device_id_type=MESH)
                cp.start()
                cps.append(cp)

    def finish():
        for cp in cps:
            cp.wait()

    return start, finish


def _cx_sems(n):
    return [pltpu.SemaphoreType.DMA((3 * n,)), pltpu.SemaphoreType.DMA((3 * n,))]


def _chip_exchange(pbs, lands, name):
    n = len(pbs)

    def body(*refs):
        start, finish = _cx_copies(n, refs[:n], refs[2 * n:3 * n], refs[3 * n:], _mesh_pos())
        start()
        finish()

    return pl.pallas_call(
        body, name=name,
        in_specs=[_ANY] * (2 * n), out_specs=[_ANY] * n,
        out_shape=[jax.ShapeDtypeStruct(p.shape, BF16) for p in lands],
        input_output_aliases={n + i: i for i in range(n)},
        scratch_shapes=_cx_sems(n),
    )(*pbs, *lands)


def _small_exchange(small, small_sib):
    def body(sm, sm_sib, sm_out, sm_sum, send, recv, loc):
        x, y, c = _mesh_pos()
        me = 2 * x + y
        sm_sum[...] = sm[...] + sm_sib[...]
        local = pltpu.make_async_copy(sm_sum, sm_out.at[me], loc)
        local.start()
        cps = []
        for j in range(3):
            t = (me + 1 + j) % N_SHARD
            cp = pltpu.make_async_remote_copy(src_ref=sm_sum, dst_ref=sm_out.at[me], send_sem=send.at[j],
                                              recv_sem=recv.at[j], device_id=(t // 2, t % 2, c), device_id_type=MESH)
            cp.start()
            cps.append(cp)
        for cp in cps:
            cp.wait()
        local.wait()

    return pl.pallas_call(
        body, name="small_exchange",
        in_specs=[_VMEM, _VMEM], out_specs=_ANY,
        out_shape=jax.ShapeDtypeStruct((N_SHARD,) + small.shape, F32),
        scratch_shapes=[pltpu.VMEM(small.shape, F32), pltpu.SemaphoreType.DMA((3,)), pltpu.SemaphoreType.DMA((3,)),
                        pltpu.SemaphoreType.DMA],
    )(small, small_sib)


def _final_sum(pf, land, chip, core, name):
    _, rh, ncol = land.shape

    def body(me_ref, c_ref, pf_ref, land_ref, o_ref):
        me = me_ref[0]
        acc = jnp.zeros((rh, ncol), F32)
        for k in range(N_SHARD):
            acc = acc + jnp.where(me == k, pf_ref[...], land_ref[k].astype(F32))
        o_ref[...] = acc

    grid_spec = pltpu.PrefetchScalarGridSpec(
        num_scalar_prefetch=2, grid=(1,),
        in_specs=[pl.BlockSpec((rh, ncol), lambda i, me, c: (0, 0)),
                  pl.BlockSpec((N_SHARD, rh, ncol), lambda i, me, c: (0, 0, 0))],
        out_specs=pl.BlockSpec((rh, ncol), lambda i, me, c: (c[0], 0)))
    return pl.pallas_call(
        body, name=name, grid_spec=grid_spec, out_shape=jax.ShapeDtypeStruct((2 * rh, ncol), F32),
        compiler_params=_params(("arbitrary",)),
    )(chip, core, pf, land)


def _sibling_share(fulls, name):
    n = len(fulls)

    def body(*refs):
        outs = refs[n:2 * n]
        send, recv = refs[2 * n:]
        x, y, c = _mesh_pos()
        cps = []
        for w in range(n):
            rh = fulls[w].shape[0] // 2
            mine = outs[w].at[pl.ds(c * rh, rh), :]
            cp = pltpu.make_async_remote_copy(src_ref=mine, dst_ref=mine, send_sem=send.at[w], recv_sem=recv.at[w],
                                              device_id=(x, y, 1 - c), device_id_type=MESH)
            cp.start()
            cps.append(cp)
        for cp in cps:
            cp.wait()

    return pl.pallas_call(
        body, name=name,
        in_specs=[_ANY] * n, out_specs=[_ANY] * n,
        out_shape=[jax.ShapeDtypeStruct(f.shape, F32) for f in fulls],
        input_output_aliases={i: i for i in range(n)},
        scratch_shapes=[pltpu.SemaphoreType.DMA((n,)), pltpu.SemaphoreType.DMA((n,))],
    )(*fulls)


_ROW = {"rel_bias": 128, "sgu_b_s": 136, "norm_ffn1": 144, "norm_mix": 145, "norm_ffn2": 146, "norm_final": 147,
        "b_gate": 148, "sgu_ln_g": 150, "sgu_ln_b": 151}


def _pack_small(gs):
    def body(ws, rel, bs, n1, nm, n2, nf, bg, lg, lb, o_ref):
        o_ref[...] = jnp.zeros_like(o_ref)
        for g in range(SGU_GROUPS):
            o_ref[0:SGU_BLOCK, g * SGU_BLOCK:(g + 1) * SGU_BLOCK] = ws[g]
        o_ref[128:136, 0:REL_PAD] = rel[...]
        o_ref[136:144, 0:SGU_BLOCK] = bs[...]
        o_ref[144:145, :] = n1[...]
        o_ref[145:146, :] = nm[...]
        o_ref[146:147, :] = n2[...]
        o_ref[147:148, :] = nf[...]
        o_ref[148:149, :] = bg[:, 0:D_MODEL]
        o_ref[149:150, :] = bg[:, D_MODEL:2 * D_MODEL]
        o_ref[150:151, 0:D_SGU] = lg[...]
        o_ref[151:152, 0:D_SGU] = lb[...]

    order = ("sgu_w_s", "rel_bias", "sgu_b_s", "norm_ffn1", "norm_mix", "norm_ffn2", "norm_final", "b_gate", "sgu_ln_g",
             "sgu_ln_b")
    return pl.pallas_call(body, name="pack_small", out_shape=jax.ShapeDtypeStruct((SMALL_ROWS, D_MODEL), F32))(
        *[gs[k] for k in order])


def _adam(w, g, m, v):
    m2 = ADAM_B1 * m + (1.0 - ADAM_B1) * g
    v2 = ADAM_B2 * v + (1.0 - ADAM_B2) * (g * g)
    m_hat = m2 / (1.0 - ADAM_B1 ** ADAM_STEP)
    v_hat = v2 / (1.0 - ADAM_B2 ** ADAM_STEP)
    delta = -ADAM_LR * (m_hat / (jnp.sqrt(v_hat) + ADAM_EPS) + ADAM_WD * w)
    return delta, m2, v2


def _adam_small(sin, w, m, v):
    names = SMALL
    k = len(names)

    def body(*refs):
        sin_ref = refs[0]
        w_r, m_r, v_r = refs[1:1 + k], refs[1 + k:1 + 2 * k], refs[1 + 2 * k:1 + 3 * k]
        outs = refs[1 + 3 * k:]
        tot = sin_ref[0] + sin_ref[1] + sin_ref[2] + sin_ref[3]
        for i, name in enumerate(names):
            o = outs[4 * i:4 * i + 4]
            if name == "sgu_w_s":
                for gi in range(SGU_GROUPS):
                    g = tot[0:SGU_BLOCK, gi * SGU_BLOCK:(gi + 1) * SGU_BLOCK]
                    res = (g,) + _adam(w_r[i][gi], g, m_r[i][gi], v_r[i][gi])
                    for ref, val in zip(o, res):
                        ref[gi] = val
                continue
            r0 = _ROW[name]
            if name == "rel_bias":
                g = tot[r0:r0 + HEADS, 0:REL_PAD]
            elif name == "sgu_b_s":
                g = tot[r0:r0 + SGU_GROUPS, 0:SGU_BLOCK]
            elif name == "b_gate":
                g = jnp.concatenate([tot[r0:r0 + 1, :], tot[r0 + 1:r0 + 2, :]], axis=1)
            elif name in ("sgu_ln_g", "sgu_ln_b"):
                g = tot[r0:r0 + 1, 0:D_SGU]
            else:
                g = tot[r0:r0 + 1, :]
            res = (g,) + _adam(w_r[i][...], g, m_r[i][...], v_r[i][...])
            for ref, val in zip(o, res):
                ref[...] = val

    out_shape = []
    for name in names:
        out_shape += [jax.ShapeDtypeStruct(w[name].shape, F32)] * 4
    flat = pl.pallas_call(body, name="adam_small", out_shape=out_shape, compiler_params=_params())(
        sin, *[w[n] for n in names], *[m[n] for n in names], *[v[n] for n in names])
    return {name: tuple(flat[4 * i:4 * i + 4]) for i, name in enumerate(names)}


def _adam_big(w, g, m, v, name):
    r, ncol = w.shape
    tr = 256 if r % 256 == 0 else r // 2

    def body(w_ref, g_ref, m_ref, v_ref, d_ref, m2_ref, v2_ref):
        d_ref[...], m2_ref[...], v2_ref[...] = _adam(w_ref[...], g_ref[...], m_ref[...], v_ref[...])

    spec = pl.BlockSpec((tr, ncol), lambda i: (i, 0))
    return pl.pallas_call(
        body, name=name, grid=(r // tr,), in_specs=[spec] * 4, out_specs=[spec] * 3,
        out_shape=[jax.ShapeDtypeStruct(w.shape, F32)] * 3, compiler_params=_params(("arbitrary",)),
    )(w, g, m, v)


WEIGHTS = ("norm_ffn1", "ffn1_w_gate", "ffn1_w_up", "ffn1_w_down", "norm_mix", "w_in", "b_gate", "rel_bias", "sgu_ln_g",
           "sgu_ln_b", "sgu_w_s", "sgu_b_s", "w_branch_att", "w_branch_sgu", "w_out", "norm_ffn2", "ffn2_w_gate",
           "ffn2_w_up", "ffn2_w_down", "norm_final")


GATE_UP = ("ffn1_w_gate", "ffn1_w_up", "ffn2_w_gate", "ffn2_w_up")
_FFN = ("ffn1_w_gate", "ffn1_w_up", "ffn1_w_down", "ffn2_w_gate", "ffn2_w_up", "ffn2_w_down")
_CAST_GROUPS = ((_FFN, "cast_ffn"), (("w_in",), "cast_w_in"), (("w_branch_att", "w_branch_sgu"), "cast_branch"),
                (("w_out",), "cast_w_out"))


def _big_form(name, a):
    return jnp.swapaxes(a, 1, 2)[0] if name in GATE_UP else a[0]


def _big_back(name, a):
    return jnp.swapaxes(a[None], 1, 2) if name in GATE_UP else a[None]


def _small_form(name, a):
    if name == "norm_final":
        return a.reshape(1, D_MODEL)
    if name == "rel_bias":
        return jnp.pad(a[0], ((0, 0), (0, REL_PAD - N_REL)))
    if name in ("sgu_w_s", "sgu_b_s"):
        return a[0]
    return a


def _small_back(name, a, like):
    if name == "rel_bias":
        a = a[:, :N_REL]
    return a.reshape(like.shape)


def kernel(x, norm_ffn1, ffn1_w_gate, ffn1_w_up, ffn1_w_down, norm_mix, w_in, b_gate, rel_bias, sgu_ln_g, sgu_ln_b, sgu_w_s, sgu_b_s, w_branch_att, w_branch_sgu, w_out, norm_ffn2, ffn2_w_gate, ffn2_w_up, ffn2_w_down, norm_final, loss_target, m_norm_ffn1, m_ffn1_w_gate, m_ffn1_w_up, m_ffn1_w_down, m_norm_mix, m_w_in, m_b_gate, m_rel_bias, m_sgu_ln_g, m_sgu_ln_b, m_sgu_w_s, m_sgu_b_s, m_w_branch_att, m_w_branch_sgu, m_w_out, m_norm_ffn2, m_ffn2_w_gate, m_ffn2_w_up, m_ffn2_w_down, m_norm_final, v_norm_ffn1, v_ffn1_w_gate, v_ffn1_w_up, v_ffn1_w_down, v_norm_mix, v_w_in, v_b_gate, v_rel_bias, v_sgu_ln_g, v_sgu_ln_b, v_sgu_w_s, v_sgu_b_s, v_w_branch_att, v_w_branch_sgu, v_w_out, v_norm_ffn2, v_ffn2_w_gate, v_ffn2_w_up, v_ffn2_w_down, v_norm_final):
    w = dict(norm_ffn1=norm_ffn1, ffn1_w_gate=ffn1_w_gate, ffn1_w_up=ffn1_w_up, ffn1_w_down=ffn1_w_down, norm_mix=norm_mix,
             w_in=w_in, b_gate=b_gate, rel_bias=rel_bias, sgu_ln_g=sgu_ln_g, sgu_ln_b=sgu_ln_b, sgu_w_s=sgu_w_s,
             sgu_b_s=sgu_b_s, w_branch_att=w_branch_att, w_branch_sgu=w_branch_sgu, w_out=w_out, norm_ffn2=norm_ffn2,
             ffn2_w_gate=ffn2_w_gate, ffn2_w_up=ffn2_w_up, ffn2_w_down=ffn2_w_down, norm_final=norm_final)
    m = dict(norm_ffn1=m_norm_ffn1, ffn1_w_gate=m_ffn1_w_gate, ffn1_w_up=m_ffn1_w_up, ffn1_w_down=m_ffn1_w_down,
             norm_mix=m_norm_mix, w_in=m_w_in, b_gate=m_b_gate, rel_bias=m_rel_bias, sgu_ln_g=m_sgu_ln_g,
             sgu_ln_b=m_sgu_ln_b, sgu_w_s=m_sgu_w_s, sgu_b_s=m_sgu_b_s, w_branch_att=m_w_branch_att,
             w_branch_sgu=m_w_branch_sgu, w_out=m_w_out, norm_ffn2=m_norm_ffn2, ffn2_w_gate=m_ffn2_w_gate,
             ffn2_w_up=m_ffn2_w_up, ffn2_w_down=m_ffn2_w_down, norm_final=m_norm_final)
    v = dict(norm_ffn1=v_norm_ffn1, ffn1_w_gate=v_ffn1_w_gate, ffn1_w_up=v_ffn1_w_up, ffn1_w_down=v_ffn1_w_down,
             norm_mix=v_norm_mix, w_in=v_w_in, b_gate=v_b_gate, rel_bias=v_rel_bias, sgu_ln_g=v_sgu_ln_g,
             sgu_ln_b=v_sgu_ln_b, sgu_w_s=v_sgu_w_s, sgu_b_s=v_sgu_b_s, w_branch_att=v_w_branch_att,
             w_branch_sgu=v_w_branch_sgu, w_out=v_w_out, norm_ffn2=v_norm_ffn2, ffn2_w_gate=v_ffn2_w_gate,
             ffn2_w_up=v_ffn2_w_up, ffn2_w_down=v_ffn2_w_down, norm_final=v_norm_final)

    core = lax.axis_index("c").astype(jnp.int32).reshape(1)
    chip = (2 * lax.axis_index("x") + lax.axis_index("y")).astype(jnp.int32).reshape(1)

    wk = {n: _big_form(n, w[n]) for n in BIG}
    slots = {}
    for names, call in _CAST_GROUPS:
        slots.update(zip(names, _cast_slots([wk[n] for n in names], chip, call)))
    gathered = _allgather([slots[n] for n in BIG], "allgather_weights")
    wb = dict(zip(BIG, gathered))
    ws = {n: _small_form(n, w[n]) for n in SMALL}
    loss, gx, gb, gs = _local_step(x[0], loss_target[0], wb, ws)

    packed = _pack_small(gs)
    exchanged = _pair_exchange([gb[n] for n in BIG], packed)
    sums = [_pair_add(gb[n], exchanged[i], core, chip, "pair_add_" + n) for i, n in enumerate(BIG)]
    lands = _chip_exchange([s[1] for s in sums], [s[2] for s in sums], "chip_exchange")
    small_sums = _small_exchange(packed, exchanged[-1])
    fulls = [_final_sum(sums[i][0], lands[i], chip, core, "final_sum_" + n) for i, n in enumerate(BIG)]
    shard_grads = dict(zip(BIG, _sibling_share(fulls, "sibling_share")))

    small = _adam_small(small_sums, ws, {n: _small_form(n, m[n]) for n in SMALL},
                        {n: _small_form(n, v[n]) for n in SMALL})
    grad, delta, new_m, new_v = {}, {}, {}, {}
    for n in SMALL:
        grad[n], delta[n], new_m[n], new_v[n] = (_small_back(n, a, w[n]) for a in small[n])
    for n in BIG:
        g2 = shard_grads[n]
        d2, m2, v2 = _adam_big(wk[n], g2, _big_form(n, m[n]), _big_form(n, v[n]), "adam_" + n)
        grad[n], delta[n], new_m[n], new_v[n] = (_big_back(n, a) for a in (g2, d2, m2, v2))

    total = lax.psum(loss[0, 0], ("x", "y", "c"))
    return (total, gx.reshape(x.shape), *[grad[n] for n in WEIGHTS], *[delta[n] for n in WEIGHTS],
            *[new_m[n] for n in WEIGHTS], *[new_v[n] for n in WEIGHTS])
```

```python
import functools

import jax
import jax.numpy as jnp
from jax import lax
from jax.experimental import pallas as pl
from jax.experimental.pallas import tpu as pltpu

F32 = jnp.float32
BF16 = jnp.bfloat16

D_MODEL = 1024
N_SHARD = 4
D_FF = 2816
FF_S = D_FF // N_SHARD
D_ATT = 512
D_SGU = 512
D_IN = 3 * D_ATT + 2 * D_SGU + 2 * D_MODEL
IN_S = D_IN // N_SHARD
BR_S = D_MODEL // N_SHARD
HEADS = 8
HEAD_DIM = 64
CHUNK = 64
N_LEFT = 8
BAND = (N_LEFT + 1) * CHUNK
REL_CLIP = 256
N_REL = 2 * REL_CLIP + 1
REL_PAD = 640
SGU_BLOCK = 128
SGU_GROUPS = 8
SGU_GDIM = 64
EPS = 1e-6
NEG_INF = -1e30

ATT_ROWS = 2 * CHUNK
ATT_KEYS = BAND + CHUNK
ATT_PAD = N_LEFT * CHUNK

ADAM_LR = 0.001
ADAM_B1 = 0.9
ADAM_B2 = 0.999
ADAM_EPS = 1e-08
ADAM_WD = 0.01
ADAM_STEP = 10

TM = 256
TW = 512
VMEM_LIMIT = 56 * 1024 * 1024

SMALL_ROWS = 152
MESH = pl.DeviceIdType.MESH

_NT = (((1,), (1,)), ((), ()))
_TN = (((0,), (0,)), ((), ()))


def _params(sem=None):
    return pltpu.CompilerParams(dimension_semantics=sem, vmem_limit_bytes=VMEM_LIMIT)


def _const_spec(shape):
    nd = len(shape)
    return pl.BlockSpec(shape, lambda *_: (0,) * nd, pipeline_mode=pl.Buffered(1))


def _acc_spec(shape):
    nd = len(shape)
    return pl.BlockSpec(shape, lambda *_: (0,) * nd)


def _row_spec(tm, ncols, off=0):
    return pl.BlockSpec((tm, ncols), lambda i: (i + off, 0))


def _row3_spec(tm, ncols):
    return pl.BlockSpec((N_SHARD, tm, ncols), lambda i: (0, i, 0))


def _dot(a, b):
    return jnp.dot(a, b, preferred_element_type=F32)


def _dot_nt(a, b):
    return lax.dot_general(a, b, _NT, preferred_element_type=F32)


def _dot_tn(a, b):
    return lax.dot_general(a, b, _TN, preferred_element_type=F32)


def _rms_fwd(x, g):
    r = lax.rsqrt(jnp.mean(x * x, axis=-1, keepdims=True) + EPS)
    xhat = x * r
    return xhat, r, xhat * g


def _rms_bwd(dh, xhat, r, g):
    dxhat = dh * g
    dx = r * (dxhat - xhat * jnp.mean(dxhat * xhat, axis=-1, keepdims=True))
    dg = jnp.sum(dh * xhat, axis=0, keepdims=True)
    return dx, dg


def _sigmoid(x):
    return 1.0 / (1.0 + jnp.exp(-x))


def _edges(n_steps):
    return [(0, True), (n_steps - 1, False)]


def _ffn_fwd(x, g, wg, wu, wd, name, payload=None):
    T = x.shape[0]

    def body(x_ref, g_ref, wg_ref, wu_ref, wd_ref, xo_ref, h_ref, a_ref, b_ref):
        xv = x_ref[...]
        hb = _rms_fwd(xv, g_ref[...])[2].astype(BF16)
        h_ref[...] = hb
        acc = jnp.zeros((TM, D_MODEL), F32)
        for s in range(N_SHARD):
            a = _dot_nt(hb, wg_ref[s])
            b = _dot_nt(hb, wu_ref[s])
            a_ref[s] = a.astype(BF16)
            b_ref[s] = b.astype(BF16)
            sv = a * _sigmoid(a) * b
            acc += _dot(sv.astype(BF16), wd_ref[s])
        xo_ref[...] = xv + 0.5 * acc

    return _call(
        body, payload, name=name, grid=(T // TM,), when=_edges(T // TM), sem=("arbitrary",),
        in_specs=[_row_spec(TM, D_MODEL), _const_spec((1, D_MODEL)), _const_spec(wg.shape), _const_spec(wu.shape),
                  _const_spec(wd.shape)],
        out_specs=[_row_spec(TM, D_MODEL), _row_spec(TM, D_MODEL), _row3_spec(TM, FF_S), _row3_spec(TM, FF_S)],
        out_shape=[jax.ShapeDtypeStruct((T, D_MODEL), F32), jax.ShapeDtypeStruct((T, D_MODEL), BF16),
                   jax.ShapeDtypeStruct((N_SHARD, T, FF_S), BF16), jax.ShapeDtypeStruct((N_SHARD, T, FF_S), BF16)],
        operands=(x, g, wg, wu, wd))


def _ffn_dgrad(dout, x, a, b, g, wg, wu, wd, name, payload=None):
    T = x.shape[0]

    def body(do_ref, x_ref, a_ref, b_ref, g_ref, wg_ref, wu_ref, wd_ref, dx_ref, da_ref, db_ref, dg_ref):
        do = do_ref[...]
        dob = do.astype(BF16)
        dh = jnp.zeros((TM, D_MODEL), F32)
        for s in range(N_SHARD):
            ds = 0.5 * _dot_nt(dob, wd_ref[s])
            av = a_ref[s].astype(F32)
            bv = b_ref[s].astype(F32)
            sig = _sigmoid(av)
            da = (ds * bv * (sig * (1.0 + av * (1.0 - sig)))).astype(BF16)
            db = (ds * (av * sig)).astype(BF16)
            da_ref[s] = da
            db_ref[s] = db
            dh += _dot(da, wg_ref[s]) + _dot(db, wu_ref[s])
        gv = g_ref[...]
        xhat, r, _ = _rms_fwd(x_ref[...], gv)
        dxn, dg = _rms_bwd(dh, xhat, r, gv)
        dx_ref[...] = do + dxn

        @pl.when(pl.program_id(0) == 0)
        def _():
            dg_ref[...] = jnp.zeros_like(dg_ref)

        dg_ref[...] += dg

    return _call(
        body, payload, name=name, grid=(T // TM,), when=_edges(T // TM), sem=("arbitrary",),
        in_specs=[_row_spec(TM, D_MODEL), _row_spec(TM, D_MODEL), _row3_spec(TM, FF_S), _row3_spec(TM, FF_S),
                  _const_spec((1, D_MODEL)), _const_spec(wg.shape), _const_spec(wu.shape), _const_spec(wd.shape)],
        out_specs=[_row_spec(TM, D_MODEL), _row3_spec(TM, FF_S), _row3_spec(TM, FF_S), _acc_spec((1, D_MODEL))],
        out_shape=[jax.ShapeDtypeStruct((T, D_MODEL), F32), jax.ShapeDtypeStruct((N_SHARD, T, FF_S), BF16),
                   jax.ShapeDtypeStruct((N_SHARD, T, FF_S), BF16), jax.ShapeDtypeStruct((1, D_MODEL), F32)],
        operands=(dout, x, a, b, g, wg, wu, wd))


def _ffn_wgrad(h, dout, a, b, da, db, name, payload=None):
    T = h.shape[0]

    def body(h_ref, do_ref, a_ref, b_ref, da_ref, db_ref, gwg_ref, gwu_ref, gwd_ref):
        @pl.when(pl.program_id(1) == 0)
        def _():
            gwg_ref[...] = jnp.zeros_like(gwg_ref)
            gwu_ref[...] = jnp.zeros_like(gwu_ref)
            gwd_ref[...] = jnp.zeros_like(gwd_ref)

        hv = h_ref[...]
        dob = do_ref[...].astype(BF16)
        av = a_ref[0].astype(F32)
        sv = (0.5 * av * _sigmoid(av) * b_ref[0].astype(F32)).astype(BF16)
        gwg_ref[0] += _dot_tn(da_ref[0], hv)
        gwu_ref[0] += _dot_tn(db_ref[0], hv)
        gwd_ref[0] += _dot_tn(sv, dob)

    tok = pl.BlockSpec((TW, D_MODEL), lambda s, i: (i, 0))
    act = pl.BlockSpec((1, TW, FF_S), lambda s, i: (s, i, 0))
    return _call(
        body, payload, name=name, grid=(N_SHARD, T // TW), when=_edges(N_SHARD * (T // TW)),
        sem=("arbitrary", "arbitrary"),
        in_specs=[tok, tok, act, act, act, act],
        out_specs=[pl.BlockSpec((1, FF_S, D_MODEL), lambda s, i: (s, 0, 0))] * 3,
        out_shape=[jax.ShapeDtypeStruct((N_SHARD, FF_S, D_MODEL), F32)] * 3,
        operands=(h, dout, a, b, da, db))


def _in_fwd(x, g, w_in):
    T = x.shape[0]

    def body(x_ref, g_ref, w_ref, h_ref, qkv_ref, zs_ref, gl_ref):
        hb = _rms_fwd(x_ref[...], g_ref[...])[2].astype(BF16)
        h_ref[...] = hb
        z0 = _dot(hb, w_ref[0])
        qkv_ref[:, 0:IN_S] = z0.astype(BF16)
        z1 = _dot(hb, w_ref[1])
        qkv_ref[:, IN_S:3 * D_ATT] = z1[:, 0:384].astype(BF16)
        zs_ref[:, 0:768] = z1[:, 384:IN_S]
        z2 = _dot(hb, w_ref[2])
        zs_ref[:, 768:1024] = z2[:, 0:256]
        gl_ref[:, 0:896] = z2[:, 256:IN_S]
        gl_ref[:, 896:2048] = _dot(hb, w_ref[3])

    return pl.pallas_call(
        body, name="in_fwd", grid=(T // TM,),
        in_specs=[_row_spec(TM, D_MODEL), _const_spec((1, D_MODEL)), _const_spec(w_in.shape)],
        out_specs=[_row_spec(TM, D_MODEL), _row_spec(TM, 3 * D_ATT), _row_spec(TM, 2 * D_SGU), _row_spec(TM, 2 * D_MODEL)],
        out_shape=[jax.ShapeDtypeStruct((T, D_MODEL), BF16), jax.ShapeDtypeStruct((T, 3 * D_ATT), BF16),
                   jax.ShapeDtypeStruct((T, 2 * D_SGU), F32), jax.ShapeDtypeStruct((T, 2 * D_MODEL), F32)],
        compiler_params=_params(("arbitrary",)),
    )(x, g, w_in)


def _in_dgrad(dx_res, x, g, w_in, dq, dk, dv, dzs, dgl):
    T = x.shape[0]

    def body(dxr_ref, x_ref, g_ref, w_ref, dq_ref, dk_ref, dv_ref, dzs_ref, dgl_ref, dx_ref, dz_ref, dg_ref):
        dz = jnp.concatenate([dq_ref[...], dk_ref[...].astype(BF16), dv_ref[...].astype(BF16), dzs_ref[...], dgl_ref[...]],
                             axis=1)
        dz_ref[...] = dz
        dh = jnp.zeros((TM, D_MODEL), F32)
        for s in range(N_SHARD):
            dh += _dot_nt(dz[:, s * IN_S:(s + 1) * IN_S], w_ref[s])
        gv = g_ref[...]
        xhat, r, _ = _rms_fwd(x_ref[...], gv)
        dxn, dg = _rms_bwd(dh, xhat, r, gv)
        dx_ref[...] = dxr_ref[...] + dxn

        @pl.when(pl.program_id(0) == 0)
        def _():
            dg_ref[...] = jnp.zeros_like(dg_ref)

        dg_ref[...] += dg

    pad_blocks = ATT_PAD // TM
    return pl.pallas_call(
        body, name="in_dgrad", grid=(T // TM,),
        in_specs=[_row_spec(TM, D_MODEL), _row_spec(TM, D_MODEL), _const_spec((1, D_MODEL)), _const_spec(w_in.shape),
                  _row_spec(TM, D_ATT), _row_spec(TM, D_ATT, pad_blocks), _row_spec(TM, D_ATT, pad_blocks),
                  _row_spec(TM, 2 * D_SGU), _row_spec(TM, 2 * D_MODEL)],
        out_specs=[_row_spec(TM, D_MODEL), _row_spec(TM, D_IN), _acc_spec((1, D_MODEL))],
        out_shape=[jax.ShapeDtypeStruct((T, D_MODEL), F32), jax.ShapeDtypeStruct((T, D_IN), BF16),
                   jax.ShapeDtypeStruct((1, D_MODEL), F32)],
        compiler_params=_params(("arbitrary",)),
    )(dx_res, x, g, w_in, dq, dk, dv, dzs, dgl)


def _in_wgrad(h, dz):
    T = h.shape[0]

    def body(h_ref, dz_ref, gw_ref):
        @pl.when(pl.program_id(1) == 0)
        def _():
            gw_ref[...] = jnp.zeros_like(gw_ref)

        gw_ref[0] += _dot_tn(h_ref[...], dz_ref[...])

    return pl.pallas_call(
        body, name="in_wgrad", grid=(N_SHARD, T // TW),
        in_specs=[pl.BlockSpec((TW, D_MODEL), lambda s, i: (i, 0)), pl.BlockSpec((TW, IN_S), lambda s, i: (i, s))],
        out_specs=pl.BlockSpec((1, D_MODEL, IN_S), lambda s, i: (s, 0, 0)),
        out_shape=jax.ShapeDtypeStruct((N_SHARD, D_MODEL, IN_S), F32),
        compiler_params=_params(("arbitrary", "arbitrary")),
    )(h, dz)


def _rel_onehot():
    r = lax.broadcasted_iota(jnp.int32, (REL_PAD, REL_PAD), 0)
    n = lax.broadcasted_iota(jnp.int32, (REL_PAD, REL_PAD), 1)
    idx = jnp.clip(BAND - 1 - n, -REL_CLIP, REL_CLIP) + REL_CLIP
    return jnp.where(r == idx, 1.0, 0.0).astype(BF16)


def _split3(v):
    p1 = v.astype(BF16)
    r1 = v - p1.astype(F32)
    p2 = r1.astype(BF16)
    p3 = (r1 - p2.astype(F32)).astype(BF16)
    return p1, p2, p3


def _relbias_fwd(tab_pad):
    def body(t_ref, o_ref):
        oh = _rel_onehot()
        acc = jnp.zeros((HEADS, REL_PAD), F32)
        for p in _split3(t_ref[...]):
            acc += _dot(p, oh)
        o_ref[...] = acc

    return pl.pallas_call(body, name="relbias_fwd", out_shape=jax.ShapeDtypeStruct((HEADS, REL_PAD), F32))(tab_pad)


def _relbias_bwd(z):
    def body(z_ref, o_ref):
        oh = _rel_onehot()
        dt2 = jnp.sum(z_ref[...], axis=1)
        acc = jnp.zeros((HEADS, REL_PAD), F32)
        for p in _split3(dt2):
            acc += _dot_nt(p, oh)
        o_ref[...] = acc

    return pl.pallas_call(body, name="relbias_bwd", out_shape=jax.ShapeDtypeStruct((HEADS, REL_PAD), F32))(z)


def _bias_blocks(t2):
    flat = jnp.tile(t2, (1, CHUNK))
    skew = flat[:, :CHUNK * (REL_PAD - 1)].reshape(HEADS, CHUNK, REL_PAD - 1)
    bias = skew[:, :, CHUNK - 1:CHUNK - 1 + BAND]
    slabs = [jnp.pad(bias, ((0, 0), (0, 0), (CHUNK * c, ATT_KEYS - BAND - CHUNK * c)), constant_values=NEG_INF)
             for c in range(2)]
    return jnp.concatenate(slabs, axis=1)


def _unskew(db2):
    out = []
    for c in range(2):
        slab = db2[:, CHUNK * c:CHUNK * (c + 1), CHUNK * c:CHUNK * c + BAND]
        y = jnp.pad(slab, ((0, 0), (0, 0), (CHUNK - 1, REL_PAD - BAND - CHUNK + 1)))
        yf = jnp.pad(y.reshape(HEADS, CHUNK * REL_PAD), ((0, 0), (0, CHUNK)))
        out.append(yf.reshape(HEADS, CHUNK, REL_PAD + 1)[:, :, :REL_PAD])
    return jnp.concatenate(out, axis=1)


def _att_load(qkv_hbm, q_s, k_s, v_s, sem, T):
    copies = [pltpu.make_async_copy(qkv_hbm.at[:, 0:D_ATT], q_s, sem.at[0]),
              pltpu.make_async_copy(qkv_hbm.at[:, D_ATT:2 * D_ATT], k_s.at[pl.ds(ATT_PAD, T), :], sem.at[1]),
              pltpu.make_async_copy(qkv_hbm.at[:, 2 * D_ATT:3 * D_ATT], v_s.at[pl.ds(ATT_PAD, T), :], sem.at[2])]
    for cp in copies:
        cp.start()
    k_s[0:ATT_PAD, :] = jnp.zeros((ATT_PAD, D_ATT), BF16)
    v_s[0:ATT_PAD, :] = jnp.zeros((ATT_PAD, D_ATT), BF16)
    for cp in copies:
        cp.wait()


def _att_probs(qh, kh, bias, valid):
    s = _dot_nt(qh, kh) * (HEAD_DIM ** -0.5) + bias
    s = jnp.where(valid, s, NEG_INF)
    e = jnp.exp(s - jnp.max(s, axis=-1, keepdims=True))
    return e / jnp.sum(e, axis=-1, keepdims=True)


def _att_fwd(qkv, bias2, payload=None):
    T = qkv.shape[0]

    def body(qkv_hbm, bias_ref, y_ref, q_s, k_s, v_s, sem):
        _att_load(qkv_hbm, q_s, k_s, v_s, sem, T)

        def block(i, carry):
            r0 = pl.multiple_of(i * ATT_ROWS, ATT_ROWS)
            q = q_s[pl.ds(r0, ATT_ROWS), :]
            kw = k_s[pl.ds(r0, ATT_KEYS), :]
            vw = v_s[pl.ds(r0, ATT_KEYS), :]
            valid = (lax.broadcasted_iota(jnp.int32, (1, ATT_KEYS), 1) + (r0 - ATT_PAD)) >= 0
            outs = []
            for h in range(HEADS):
                hs = slice(h * HEAD_DIM, (h + 1) * HEAD_DIM)
                p = _att_probs(q[:, hs], kw[:, hs], bias_ref[h], valid)
                outs.append(_dot(p.astype(BF16), vw[:, hs]))
            y_ref[pl.ds(r0, ATT_ROWS), :] = jnp.concatenate(outs, axis=1).astype(BF16)
            return carry

        lax.fori_loop(0, T // ATT_ROWS, block, 0)

    return _call(
        body, payload, name="att_fwd", grid=None,
        in_specs=[pl.BlockSpec(memory_space=pl.ANY), pl.BlockSpec(memory_space=pltpu.VMEM)],
        out_specs=[pl.BlockSpec(memory_space=pltpu.VMEM)],
        out_shape=[jax.ShapeDtypeStruct((T, D_ATT), BF16)],
        scratch_shapes=[pltpu.VMEM((T, D_ATT), BF16), pltpu.VMEM((T + ATT_PAD, D_ATT), BF16),
                        pltpu.VMEM((T + ATT_PAD, D_ATT), BF16), pltpu.SemaphoreType.DMA((3,))],
        operands=(qkv, bias2))


def _att_bwd(qkv, dy, bias2, payload=None):
    T = qkv.shape[0]

    def body(qkv_hbm, dy_ref, bias_ref, dq_ref, dk_ref, dv_ref, db_ref, q_s, k_s, v_s, sem):
        _att_load(qkv_hbm, q_s, k_s, v_s, sem, T)
        dk_ref[...] = jnp.zeros_like(dk_ref)
        dv_ref[...] = jnp.zeros_like(dv_ref)
        db_ref[...] = jnp.zeros_like(db_ref)

        def block(i, carry):
            r0 = pl.multiple_of(i * ATT_ROWS, ATT_ROWS)
            q = q_s[pl.ds(r0, ATT_ROWS), :]
            kw = k_s[pl.ds(r0, ATT_KEYS), :]
            vw = v_s[pl.ds(r0, ATT_KEYS), :]
            dyb = dy_ref[pl.ds(r0, ATT_ROWS), :]
            valid = (lax.broadcasted_iota(jnp.int32, (1, ATT_KEYS), 1) + (r0 - ATT_PAD)) >= 0
            dqs, dks, dvs = [], [], []
            for h in range(HEADS):
                hs = slice(h * HEAD_DIM, (h + 1) * HEAD_DIM)
                qh, kh, vh, dyh = q[:, hs], kw[:, hs], vw[:, hs], dyb[:, hs]
                p = _att_probs(qh, kh, bias_ref[h], valid)
                dp = _dot_nt(dyh, vh)
                ds = p * (dp - jnp.sum(p * dp, axis=-1, keepdims=True))
                db_ref[h] += ds
                dsb = (ds * (HEAD_DIM ** -0.5)).astype(BF16)
                dqs.append(_dot(dsb, kh))
                dks.append(_dot_tn(dsb, qh))
                dvs.append(_dot_tn(p.astype(BF16), dyh))
            dq_ref[pl.ds(r0, ATT_ROWS), :] = jnp.concatenate(dqs, axis=1).astype(BF16)
            dk_ref[pl.ds(r0, ATT_KEYS), :] += jnp.concatenate(dks, axis=1)
            dv_ref[pl.ds(r0, ATT_KEYS), :] += jnp.concatenate(dvs, axis=1)
            return carry

        lax.fori_loop(0, T // ATT_ROWS, block, 0)

    vmem = pl.BlockSpec(memory_space=pltpu.VMEM)
    return _call(
        body, payload, name="att_bwd", grid=None,
        in_specs=[pl.BlockSpec(memory_space=pl.ANY), vmem, vmem],
        out_specs=[vmem, vmem, vmem, vmem],
        out_shape=[jax.ShapeDtypeStruct((T, D_ATT), BF16), jax.ShapeDtypeStruct((T + ATT_PAD, D_ATT), F32),
                   jax.ShapeDtypeStruct((T + ATT_PAD, D_ATT), F32), jax.ShapeDtypeStruct((HEADS, ATT_ROWS, ATT_KEYS), F32)],
        scratch_shapes=[pltpu.VMEM((T, D_ATT), BF16), pltpu.VMEM((T + ATT_PAD, D_ATT), BF16),
                        pltpu.VMEM((T + ATT_PAD, D_ATT), BF16), pltpu.SemaphoreType.DMA((3,))],
        operands=(qkv, dy, bias2))


_GELU_C = 0.7978845608028654
_GELU_A = 0.044715


def _gelu(x):
    t = jnp.tanh(_GELU_C * (x + _GELU_A * x * x * x))
    return 0.5 * x * (1.0 + t), t


def _gelu_grad(x, t):
    return 0.5 * (1.0 + t) + 0.5 * x * (1.0 - t * t) * _GELU_C * (1.0 + 3.0 * _GELU_A * x * x)


def _group_masks():
    col = lax.broadcasted_iota(jnp.int32, (SGU_GROUPS, D_SGU), 1) // SGU_GDIM
    grp = lax.broadcasted_iota(jnp.int32, (SGU_GROUPS, D_SGU), 0)
    return jnp.where(col == grp, 1.0, 0.0).astype(F32)


def _causal_mask(transposed=False):
    i = lax.broadcasted_iota(jnp.int32, (SGU_BLOCK, SGU_BLOCK), 0) // CHUNK
    j = lax.broadcasted_iota(jnp.int32, (SGU_BLOCK, SGU_BLOCK), 1) // CHUNK
    return (j >= i) if transposed else (i >= j)


def _sgu_norm(zs, lng, lnb):
    gz, t = _gelu(zs)
    u = gz[:, 0:D_SGU]
    vs = gz[:, D_SGU:2 * D_SGU]
    xc = vs - jnp.mean(vs, axis=-1, keepdims=True)
    rstd = lax.rsqrt(jnp.mean(xc * xc, axis=-1, keepdims=True) + EPS)
    xhat = xc * rstd
    return t, u, xhat, rstd, xhat * lng + lnb


def _sgu_mix(vn_blk, w_ref, bst, gm):
    mask = _causal_mask()
    s = jnp.zeros((SGU_BLOCK, D_SGU), F32)
    for g in range(SGU_GROUPS):
        wm = jnp.where(mask, w_ref[g], 0.0).astype(BF16)
        s += _dot(wm, (vn_blk * gm[g:g + 1, :]).astype(BF16))
        s += bst[:, g:g + 1] * gm[g:g + 1, :]
    return s


def _sgu_fwd(zs, lng, lnb, w_s, bst):
    T = zs.shape[0]
    nblk = TM // SGU_BLOCK

    def body(zs_ref, lng_ref, lnb_ref, w_ref, bst_ref, y_ref):
        _, u, _, _, vn = _sgu_norm(zs_ref[...], lng_ref[...], lnb_ref[...])
        gm = _group_masks()
        bst_v = bst_ref[...]
        for n in range(nblk):
            rows = slice(n * SGU_BLOCK, (n + 1) * SGU_BLOCK)
            s = _sgu_mix(vn[rows], w_ref, bst_v, gm)
            y_ref[rows, :] = (u[rows] * s).astype(BF16)

    return pl.pallas_call(
        body, name="sgu_fwd", grid=(T // TM,),
        in_specs=[_row_spec(TM, 2 * D_SGU), _const_spec((1, D_SGU)), _const_spec((1, D_SGU)),
                  _const_spec(w_s.shape), _const_spec(bst.shape)],
        out_specs=_row_spec(TM, D_SGU),
        out_shape=jax.ShapeDtypeStruct((T, D_SGU), BF16),
        compiler_params=_params(("arbitrary",)),
    )(zs, lng, lnb, w_s, bst)


def _sgu_bwd(zs, dy, lng, lnb, w_s, w_st, bst):
    T = zs.shape[0]
    nblk = TM // SGU_BLOCK

    def body(zs_ref, dy_ref, lng_ref, lnb_ref, w_ref, wt_ref, bst_ref, dzs_ref, dw_ref, dbt_ref, dlg_ref, dlb_ref):
        @pl.when(pl.program_id(0) == 0)
        def _():
            dw_ref[...] = jnp.zeros_like(dw_ref)
            dbt_ref[...] = jnp.zeros_like(dbt_ref)
            dlg_ref[...] = jnp.zeros_like(dlg_ref)
            dlb_ref[...] = jnp.zeros_like(dlb_ref)

        zs_v = zs_ref[...]
        lng_v = lng_ref[...]
        t, u, xhat, rstd, vn = _sgu_norm(zs_v, lng_v, lnb_ref[...])
        gm = _group_masks()
        bst_v = bst_ref[...]
        mask = _causal_mask()
        mask_t = _causal_mask(transposed=True)
        dyv = dy_ref[...].astype(F32)
        lane8 = lax.broadcasted_iota(jnp.int32, (1, SGU_GROUPS), 1)
        du_rows, dvn_rows = [], []
        for n in range(nblk):
            rows = slice(n * SGU_BLOCK, (n + 1) * SGU_BLOCK)
            vn_b = vn[rows]
            s = _sgu_mix(vn_b, w_ref, bst_v, gm)
            du_rows.append(dyv[rows] * s)
            dsb = dyv[rows] * u[rows]
            vnb16 = vn_b.astype(BF16)
            dvn = jnp.zeros((SGU_BLOCK, D_SGU), F32)
            dbt = jnp.zeros((SGU_BLOCK, SGU_GROUPS), F32)
            for g in range(SGU_GROUPS):
                dsg = dsb * gm[g:g + 1, :]
                dsg16 = dsg.astype(BF16)
                wmt = jnp.where(mask_t, wt_ref[g], 0.0).astype(BF16)
                dvn += _dot(wmt, dsg16)
                dw_ref[g] += jnp.where(mask, _dot_nt(dsg16, vnb16), 0.0)
                dbt += jnp.sum(dsg, axis=-1, keepdims=True) * jnp.where(lane8 == g, 1.0, 0.0)
            dbt_ref[...] += dbt
            dvn_rows.append(dvn)
        du = jnp.concatenate(du_rows, axis=0)
        dvn = jnp.concatenate(dvn_rows, axis=0)
        dlg_ref[...] += jnp.sum(dvn * xhat, axis=0, keepdims=True)
        dlb_ref[...] += jnp.sum(dvn, axis=0, keepdims=True)
        dxhat = dvn * lng_v
        dvs = rstd * (dxhat - jnp.mean(dxhat, axis=-1, keepdims=True)
                      - xhat * jnp.mean(dxhat * xhat, axis=-1, keepdims=True))
        dgz = jnp.concatenate([du, dvs], axis=1)
        dzs_ref[...] = (dgz * _gelu_grad(zs_v, t)).astype(BF16)

    return pl.pallas_call(
        body, name="sgu_bwd", grid=(T // TM,),
        in_specs=[_row_spec(TM, 2 * D_SGU), _row_spec(TM, D_SGU), _const_spec((1, D_SGU)), _const_spec((1, D_SGU)),
                  _const_spec(w_s.shape), _const_spec(w_st.shape), _const_spec(bst.shape)],
        out_specs=[_row_spec(TM, 2 * D_SGU), _acc_spec(w_s.shape), _acc_spec(bst.shape), _acc_spec((1, D_SGU)),
                   _acc_spec((1, D_SGU))],
        out_shape=[jax.ShapeDtypeStruct((T, 2 * D_SGU), BF16), jax.ShapeDtypeStruct(w_s.shape, F32),
                   jax.ShapeDtypeStruct(bst.shape, F32), jax.ShapeDtypeStruct((1, D_SGU), F32),
                   jax.ShapeDtypeStruct((1, D_SGU), F32)],
        compiler_params=_params(("arbitrary",)),
    )(zs, dy, lng, lnb, w_s, w_st, bst)


def _cols(v, s):
    return v[:, s * BR_S:(s + 1) * BR_S]


def _merge_fwd(x, y_att, y_sgu, gl, b_gate, wba, wbs, wo):
    T = x.shape[0]

    def body(x_ref, ya_ref, ys_ref, gl_ref, bg_ref, wba_ref, wbs_ref, wo_ref, xo_ref, m_ref, pa_ref, ps_ref):
        ya = ya_ref[...]
        ys = ys_ref[...]
        pa = jnp.concatenate([_dot(ya, wba_ref[s]) for s in range(N_SHARD)], axis=1)
        ps = jnp.concatenate([_dot(ys, wbs_ref[s]) for s in range(N_SHARD)], axis=1)
        g = _sigmoid(gl_ref[...] + bg_ref[...])
        mb = (g[:, 0:D_MODEL] * pa + g[:, D_MODEL:2 * D_MODEL] * ps).astype(BF16)
        m_ref[...] = mb
        pa_ref[...] = pa.astype(BF16)
        ps_ref[...] = ps.astype(BF16)
        acc = jnp.zeros((TM, D_MODEL), F32)
        for s in range(N_SHARD):
            acc += _dot(_cols(mb, s), wo_ref[s])
        xo_ref[...] = x_ref[...] + acc

    tokd = jax.ShapeDtypeStruct((T, D_MODEL), BF16)
    return pl.pallas_call(
        body, name="merge_fwd", grid=(T // TM,),
        in_specs=[_row_spec(TM, D_MODEL), _row_spec(TM, D_ATT), _row_spec(TM, D_SGU), _row_spec(TM, 2 * D_MODEL),
                  _const_spec((1, 2 * D_MODEL)), _const_spec(wba.shape), _const_spec(wbs.shape), _const_spec(wo.shape)],
        out_specs=[_row_spec(TM, D_MODEL)] * 4,
        out_shape=[jax.ShapeDtypeStruct((T, D_MODEL), F32), tokd, tokd, tokd],
        compiler_params=_params(("arbitrary",)),
    )(x, y_att, y_sgu, gl, b_gate, wba, wbs, wo)


def _merge_bwd(dx, y_att, y_sgu, gl, merged, pa, ps, b_gate, wba, wbs, wo, payload=None):
    T = dx.shape[0]

    def body(dx_ref, ya_ref, ys_ref, gl_ref, m_ref, pa_ref, ps_ref, bg_ref, wba_ref, wbs_ref, wo_ref,
             dya_ref, dys_ref, dgl_ref, dbg_ref, gwba_ref, gwbs_ref, gwo_ref):
        @pl.when(pl.program_id(0) == 0)
        def _():
            dbg_ref[...] = jnp.zeros_like(dbg_ref)
            gwba_ref[...] = jnp.zeros_like(gwba_ref)
            gwbs_ref[...] = jnp.zeros_like(gwbs_ref)
            gwo_ref[...] = jnp.zeros_like(gwo_ref)

        dxb = dx_ref[...].astype(BF16)
        dm = jnp.concatenate([_dot_nt(dxb, wo_ref[s]) for s in range(N_SHARD)], axis=1)
        g = _sigmoid(gl_ref[...] + bg_ref[...])
        ga = g[:, 0:D_MODEL]
        gs = g[:, D_MODEL:2 * D_MODEL]
        dpa = (dm * ga).astype(BF16)
        dps = (dm * gs).astype(BF16)
        dgl = jnp.concatenate([dm * pa_ref[...].astype(F32) * ga * (1.0 - ga),
                               dm * ps_ref[...].astype(F32) * gs * (1.0 - gs)], axis=1)
        dgl_ref[...] = dgl.astype(BF16)
        dbg_ref[...] += jnp.sum(dgl, axis=0, keepdims=True)
        ya = ya_ref[...]
        ys = ys_ref[...]
        mb = m_ref[...]
        dya = jnp.zeros((TM, D_ATT), F32)
        dys = jnp.zeros((TM, D_SGU), F32)
        for s in range(N_SHARD):
            dya += _dot_nt(_cols(dpa, s), wba_ref[s])
            dys += _dot_nt(_cols(dps, s), wbs_ref[s])
            gwo_ref[s] += _dot_tn(_cols(mb, s), dxb)
            gwba_ref[s] += _dot_tn(ya, _cols(dpa, s))
            gwbs_ref[s] += _dot_tn(ys, _cols(dps, s))
        dya_ref[...] = dya.astype(BF16)
        dys_ref[...] = dys.astype(BF16)

    return _call(
        body, payload, name="merge_bwd", grid=(T // TM,), when=_edges(T // TM), sem=("arbitrary",),
        operands=(dx, y_att, y_sgu, gl, merged, pa, ps, b_gate, wba, wbs, wo),
        in_specs=[_row_spec(TM, D_MODEL), _row_spec(TM, D_ATT), _row_spec(TM, D_SGU), _row_spec(TM, 2 * D_MODEL),
                  _row_spec(TM, D_MODEL), _row_spec(TM, D_MODEL), _row_spec(TM, D_MODEL),
                  _const_spec((1, 2 * D_MODEL)), _const_spec(wba.shape), _const_spec(wbs.shape), _const_spec(wo.shape)],
        out_specs=[_row_spec(TM, D_ATT), _row_spec(TM, D_SGU), _row_spec(TM, 2 * D_MODEL), _acc_spec((1, 2 * D_MODEL)),
                   _acc_spec(wba.shape), _acc_spec(wbs.shape), _acc_spec(wo.shape)],
        out_shape=[jax.ShapeDtypeStruct((T, D_ATT), BF16), jax.ShapeDtypeStruct((T, D_SGU), BF16),
                   jax.ShapeDtypeStruct((T, 2 * D_MODEL), BF16), jax.ShapeDtypeStruct((1, 2 * D_MODEL), F32),
                   jax.ShapeDtypeStruct(wba.shape, F32), jax.ShapeDtypeStruct(wbs.shape, F32),
                   jax.ShapeDtypeStruct(wo.shape, F32)])


def _loss_bwd(x, target, g):
    T = x.shape[0]

    def body(x_ref, t_ref, g_ref, dx_ref, loss_ref, dg_ref):
        @pl.when(pl.program_id(0) == 0)
        def _():
            loss_ref[...] = jnp.zeros_like(loss_ref)
            dg_ref[...] = jnp.zeros_like(dg_ref)

        gv = g_ref[...]
        xhat, r, y = _rms_fwd(x_ref[...], gv)
        err = y - t_ref[...]
        per_tok = jnp.mean(err * err, axis=-1, keepdims=True)
        loss_ref[...] += 0.5 * jnp.sum(per_tok, axis=0, keepdims=True)
        dxn, dg = _rms_bwd(err * (1.0 / D_MODEL), xhat, r, gv)
        dx_ref[...] = dxn
        dg_ref[...] += dg

    return pl.pallas_call(
        body, name="loss_bwd", grid=(T // TM,),
        in_specs=[_row_spec(TM, D_MODEL), _row_spec(TM, D_MODEL), _const_spec((1, D_MODEL))],
        out_specs=[_row_spec(TM, D_MODEL), _acc_spec((1, 128)), _acc_spec((1, D_MODEL))],
        out_shape=[jax.ShapeDtypeStruct((T, D_MODEL), F32), jax.ShapeDtypeStruct((1, 128), F32),
                   jax.ShapeDtypeStruct((1, D_MODEL), F32)],
        compiler_params=_params(("arbitrary",)),
    )(x, target, g)


BIG = ("ffn1_w_gate", "ffn1_w_up", "ffn1_w_down", "w_in", "w_branch_att", "w_branch_sgu", "w_out",
       "ffn2_w_gate", "ffn2_w_up", "ffn2_w_down")
SMALL = ("norm_ffn1", "norm_mix", "b_gate", "rel_bias", "sgu_ln_g", "sgu_ln_b", "sgu_w_s", "sgu_b_s", "norm_ffn2",
         "norm_final")


G_FFN1 = ("ffn1_w_gate", "ffn1_w_up", "ffn1_w_down")
G_MIX = ("w_in", "w_branch_att", "w_branch_sgu", "w_out")
G_FFN2 = ("ffn2_w_gate", "ffn2_w_up", "ffn2_w_down")


def _local_step(x, target, wb, ws, dist=None):
    def gather_on(names):
        return _ag_payload([wb[n] for n in names]) if dist else None

    t2 = _relbias_fwd(ws["rel_bias"])
    bias2 = _bias_blocks(t2)
    bst = ws["sgu_b_s"].T
    w_st = jnp.swapaxes(ws["sgu_w_s"], 1, 2)

    x1, h1, a1, b1, *got = _ffn_fwd(x, ws["norm_ffn1"], wb["ffn1_w_gate"], wb["ffn1_w_up"], wb["ffn1_w_down"],
                                    "ffn1_fwd", gather_on(G_MIX))
    wb.update(zip(G_MIX, got))
    h2, qkv, zs, gl = _in_fwd(x1, ws["norm_mix"], wb["w_in"])
    y_att, *got = _att_fwd(qkv, bias2, gather_on(G_FFN2))
    wb.update(zip(G_FFN2, got))
    y_sgu = _sgu_fwd(zs, ws["sgu_ln_g"], ws["sgu_ln_b"], ws["sgu_w_s"], bst)
    x2, merged, pa, ps = _merge_fwd(x1, y_att, y_sgu, gl, ws["b_gate"], wb["w_branch_att"], wb["w_branch_sgu"],
                                    wb["w_out"])
    x3, h3, a3, b3 = _ffn_fwd(x2, ws["norm_ffn2"], wb["ffn2_w_gate"], wb["ffn2_w_up"], wb["ffn2_w_down"], "ffn2_fwd")
    dx3, loss, g_final = _loss_bwd(x3, target, ws["norm_final"])

    gb, gs, sums = {}, {"norm_final": g_final}, {}

    def pair_on(names, small=None):
        return _px_payload([gb[n] for n in names], small) if dist else None

    def pair_add(names, halves):
        for n, rv in zip(names, halves):
            sums[n] = _pair_add(gb[n], rv, dist[0], dist[1], "pair_add_" + n)

    def chips_on(names):
        return _cx_payload([sums[n][1] for n in names], [sums[n][2] for n in names]) if dist else None

    dx2, da3, db3, gs["norm_ffn2"] = _ffn_dgrad(dx3, x2, a3, b3, ws["norm_ffn2"], wb["ffn2_w_gate"], wb["ffn2_w_up"],
                                                wb["ffn2_w_down"], "ffn2_dgrad")
    gb["ffn2_w_gate"], gb["ffn2_w_up"], gb["ffn2_w_down"] = _ffn_wgrad(h3, dx3, a3, b3, da3, db3, "ffn2_wgrad")
    dy_att, dy_sgu, dgl, gs["b_gate"], gb["w_branch_att"], gb["w_branch_sgu"], gb["w_out"], *got = _merge_bwd(
        dx2, y_att, y_sgu, gl, merged, pa, ps, ws["b_gate"], wb["w_branch_att"], wb["w_branch_sgu"], wb["w_out"],
        pair_on(G_FFN2))
    pair_add(G_FFN2, got)
    dq, dk, dv, db2, *lands2 = _att_bwd(qkv, dy_att, bias2, chips_on(G_FFN2))
    gs["rel_bias"] = _relbias_bwd(_unskew(db2))
    dzs, gs["sgu_w_s"], dbt, gs["sgu_ln_g"], gs["sgu_ln_b"] = _sgu_bwd(zs, dy_sgu, ws["sgu_ln_g"], ws["sgu_ln_b"],
                                                                      ws["sgu_w_s"], w_st, bst)
    gs["sgu_b_s"] = dbt.T
    dx1, dz, gs["norm_mix"] = _in_dgrad(dx2, x1, ws["norm_mix"], wb["w_in"], dq, dk, dv, dzs, dgl)
    gb["w_in"] = _in_wgrad(h2, dz)
    gx, da1, db1, gs["norm_ffn1"], *got = _ffn_dgrad(dx1, x, a1, b1, ws["norm_ffn1"], wb["ffn1_w_gate"],
                                                    wb["ffn1_w_up"], wb["ffn1_w_down"], "ffn1_dgrad", pair_on(G_MIX))
    pair_add(G_MIX, got)
    gb["ffn1_w_gate"], gb["ffn1_w_up"], gb["ffn1_w_down"], *lands_mix = _ffn_wgrad(h1, dx1, a1, b1, da1, db1,
                                                                                   "ffn1_wgrad", chips_on(G_MIX))
    if not dist:
        return loss, gx, gb, gs

    packed = _pack_small(gs)
    *got, packed_sib = _exchange(pair_on(G_FFN1, packed), "pair_exchange_ffn1")
    pair_add(G_FFN1, got)
    lands1 = _exchange(chips_on(G_FFN1), "chip_exchange_ffn1")
    small_sums = _small_exchange(packed, packed_sib)
    lands = dict(zip(G_FFN2 + G_MIX + G_FFN1, list(lands2) + list(lands_mix) + list(lands1)))
    fulls = [_final_sum(sums[n][0], lands[n], dist[1], dist[0], "final_sum_" + n) for n in BIG]
    return loss, gx, dict(zip(BIG, _sibling_share(fulls, "sibling_share"))), small_sums


_ANY = pl.BlockSpec(memory_space=pl.ANY)
_VMEM = pl.BlockSpec(memory_space=pltpu.VMEM)


def _mesh_pos():
    return lax.axis_index("x"), lax.axis_index("y"), lax.axis_index("c")


def _cast_slots(shards, chip, name):
    n = len(shards)
    r, ncol = shards[0].shape
    tr = r // 2

    def body(me_ref, *refs):
        for i_ref, o_ref in zip(refs[:n], refs[n:]):
            o_ref[0] = i_ref[...].astype(BF16)

    grid_spec = pltpu.PrefetchScalarGridSpec(
        num_scalar_prefetch=1, grid=(r // tr,),
        in_specs=[pl.BlockSpec((tr, ncol), lambda i, me: (i, 0))] * n,
        out_specs=[pl.BlockSpec((1, tr, ncol), lambda i, me: (me[0], i, 0))] * n)
    return pl.pallas_call(
        body, name=name, grid_spec=grid_spec,
        out_shape=[jax.ShapeDtypeStruct((N_SHARD, r, ncol), BF16)] * n,
        compiler_params=_params(("arbitrary",)),
    )(chip, *shards)


class _Payload:
    def __init__(self, arrays, out_shapes, aliases, scratch, phases):
        self.arrays = list(arrays)
        self.out_shapes = list(out_shapes)
        self.aliases = dict(aliases)
        self.scratch = list(scratch)
        self.phases = phases


def _remote(src, dst, ssem, rsem, dev):
    return pltpu.make_async_remote_copy(src_ref=src, dst_ref=dst, send_sem=ssem, recv_sem=rsem, device_id=dev,
                                        device_id_type=MESH)


def _call(body, payload, *, name, grid, in_specs, out_specs, out_shape, scratch_shapes=(), sem=None, when=None,
          operands=()):
    in_specs, out_specs, out_shape = list(in_specs), list(out_specs), list(out_shape)
    scratch_shapes = list(scratch_shapes)
    n_in, n_out, n_scr = len(in_specs), len(out_specs), len(scratch_shapes)
    kwargs = {}
    kernel = body
    if payload is not None:
        k_in, k_out = len(payload.arrays), len(payload.out_shapes)
        rank = len(grid) if grid else 0

        def kernel(*refs):
            a, b = n_in, n_in + k_in
            c, d = b + n_out, b + n_out + k_out
            e = d + n_scr
            phases = payload.phases(refs[a:b], refs[c:d], refs[e:])

            def run():
                body(*refs[:a], *refs[b:c], *refs[d:e])

            if not grid:
                phases[0]()
                run()
                for ph in phases[1:]:
                    ph()
                return
            step = pl.program_id(0)
            if rank == 2:
                step = step * grid[1] + pl.program_id(1)
            for ph, (at, before) in zip(phases, when):
                if before:
                    pl.when(step == at)(ph)
            run()
            for ph, (at, before) in zip(phases, when):
                if not before:
                    pl.when(step == at)(ph)

        in_specs += [_ANY] * k_in
        out_specs += [_ANY] * k_out
        out_shape += payload.out_shapes
        scratch_shapes += payload.scratch
        kwargs["input_output_aliases"] = {n_in + i: n_out + j for i, j in payload.aliases.items()}
        operands = tuple(operands) + tuple(payload.arrays)
    if grid:
        kwargs["grid"] = grid
    return pl.pallas_call(kernel, name=name, in_specs=in_specs, out_specs=out_specs, out_shape=out_shape,
                          scratch_shapes=scratch_shapes, compiler_params=_params(sem), **kwargs)(*operands)


def _exchange(payload, name):
    return _call(lambda: None, payload, name=name, grid=None, in_specs=[], out_specs=[], out_shape=[])


def _ag_payload(slots):
    n = len(slots)

    def phases(_, refs, sems):
        send_i, recv_i, send_d, recv_d = sems
        x, y, c = _mesh_pos()
        me = 2 * x + y

        def half(w, core):
            rh = slots[w].shape[1] // 2
            return pl.ds(core * rh, rh)

        def ici(w, j):
            t = (me + 1 + j) % N_SHARD
            mine = refs[w].at[me, half(w, c), :]
            return _remote(mine, mine, send_i.at[3 * w + j], recv_i.at[3 * w + j], (t // 2, t % 2, c))

        def d2d(w, j, core):
            s = (me + 3 - j) % N_SHARD
            land = refs[w].at[s, half(w, core), :]
            return _remote(land, land, send_d.at[3 * w + j], recv_d.at[3 * w + j], (x, y, 1 - c))

        def start():
            for w in range(n):
                for j in range(3):
                    ici(w, j).start()

        def finish():
            for w in range(n):
                for j in range(3):
                    s = (me + 3 - j) % N_SHARD
                    land = refs[w].at[s, half(w, c), :]
                    _remote(land, land, send_i.at[3 * w + j], recv_i.at[3 * w + j], (x, y, c)).wait_recv()
                    d2d(w, j, c).start()
            for w in range(n):
                for j in range(3):
                    d2d(w, j, 1 - c).wait_recv()
            for w in range(n):
                for j in range(3):
                    ici(w, j).wait_send()
                    d2d(w, j, c).wait_send()

        return [start, finish]

    return _Payload(slots, [jax.ShapeDtypeStruct(s.shape, s.dtype) for s in slots], {i: i for i in range(n)},
                    [pltpu.SemaphoreType.DMA((3 * n,)) for _ in range(4)], phases)


def _px_payload(grads, small=None):
    arrays = list(grads) + ([small] if small is not None else [])
    n = len(arrays)

    def phases(ins, outs, sems):
        send, recv = sems
        x, y, c = _mesh_pos()

        def copy(w):
            if w < len(grads):
                rh = grads[w].shape[1] // 2
                src = ins[w].at[:, pl.ds((1 - c) * rh, rh), :]
            else:
                src = ins[w]
            return _remote(src, outs[w], send.at[w], recv.at[w], (x, y, 1 - c))

        def start():
            for w in range(n):
                copy(w).start()

        def finish():
            for w in range(n):
                copy(w).wait()

        return [start, finish]

    out_shapes = [jax.ShapeDtypeStruct((N_SHARD, g.shape[1] // 2, g.shape[2]), F32) for g in grads]
    if small is not None:
        out_shapes.append(jax.ShapeDtypeStruct(small.shape, F32))
    return _Payload(arrays, out_shapes, {}, [pltpu.SemaphoreType.DMA((n,)), pltpu.SemaphoreType.DMA((n,))], phases)


def _cx_payload(pbs, lands):
    n = len(pbs)

    def phases(ins, outs, sems):
        send, recv = sems
        x, y, c = _mesh_pos()
        me = 2 * x + y

        def copy(w, j):
            t = (me + 1 + j) % N_SHARD
            return _remote(ins[w].at[t], outs[w].at[me], send.at[3 * w + j], recv.at[3 * w + j], (t // 2, t % 2, c))

        def start():
            for w in range(n):
                for j in range(3):
                    copy(w, j).start()

        def finish():
            for w in range(n):
                for j in range(3):
                    copy(w, j).wait()

        return [start, finish]

    return _Payload(list(pbs) + list(lands), [jax.ShapeDtypeStruct(p.shape, BF16) for p in lands],
                    {n + i: i for i in range(n)},
                    [pltpu.SemaphoreType.DMA((3 * n,)), pltpu.SemaphoreType.DMA((3 * n,))], phases)


def _pair_add(g, rv, core, chip, name):
    _, r, ncol = g.shape
    rh = r // 2

    def body(c_ref, me_ref, g_ref, rv_ref, pf_ref, pb_ref, land_ref):
        s = g_ref[0] + rv_ref[0]
        sb = s.astype(BF16)
        pb_ref[0] = sb
        land_ref[0] = sb

        @pl.when(pl.program_id(0) == me_ref[0])
        def _():
            pf_ref[...] = s

    slot = pl.BlockSpec((1, rh, ncol), lambda s, c, me: (s, 0, 0))
    grid_spec = pltpu.PrefetchScalarGridSpec(
        num_scalar_prefetch=2, grid=(N_SHARD,),
        in_specs=[pl.BlockSpec((1, rh, ncol), lambda s, c, me: (s, c[0], 0)), slot],
        out_specs=[pl.BlockSpec((rh, ncol), lambda s, c, me: (0, 0)), slot, slot])
    return pl.pallas_call(
        body, name=name, grid_spec=grid_spec,
        out_shape=[jax.ShapeDtypeStruct((rh, ncol), F32), jax.ShapeDtypeStruct((N_SHARD, rh, ncol), BF16),
                   jax.ShapeDtypeStruct((N_SHARD, rh, ncol), BF16)],
        compiler_params=_params(("arbitrary",)),
    )(core, chip, g, rv)


def _small_exchange(small, small_sib):
    def body(sm, sm_sib, sm_out, sm_sum, send, recv, loc):
        x, y, c = _mesh_pos()
        me = 2 * x + y
        sm_sum[...] = sm[...] + sm_sib[...]
        local = pltpu.make_async_copy(sm_sum, sm_out.at[me], loc)
        local.start()
        cps = []
        for j in range(3):
            t = (me + 1 + j) % N_SHARD
            cp = pltpu.make_async_remote_copy(src_ref=sm_sum, dst_ref=sm_out.at[me], send_sem=send.at[j],
                                              recv_sem=recv.at[j], device_id=(t // 2, t % 2, c), device_id_type=MESH)
            cp.start()
            cps.append(cp)
        for cp in cps:
            cp.wait()
        local.wait()

    return pl.pallas_call(
        body, name="small_exchange",
        in_specs=[_VMEM, _VMEM], out_specs=_ANY,
        out_shape=jax.ShapeDtypeStruct((N_SHARD,) + small.shape, F32),
        scratch_shapes=[pltpu.VMEM(small.shape, F32), pltpu.SemaphoreType.DMA((3,)), pltpu.SemaphoreType.DMA((3,)),
                        pltpu.SemaphoreType.DMA],
    )(small, small_sib)


def _final_sum(pf, land, chip, core, name):
    _, rh, ncol = land.shape

    def body(me_ref, c_ref, pf_ref, land_ref, o_ref):
        me = me_ref[0]
        acc = jnp.zeros((rh, ncol), F32)
        for k in range(N_SHARD):
            acc = acc + jnp.where(me == k, pf_ref[...], land_ref[k].astype(F32))
        o_ref[...] = acc

    grid_spec = pltpu.PrefetchScalarGridSpec(
        num_scalar_prefetch=2, grid=(1,),
        in_specs=[pl.BlockSpec((rh, ncol), lambda i, me, c: (0, 0)),
                  pl.BlockSpec((N_SHARD, rh, ncol), lambda i, me, c: (0, 0, 0))],
        out_specs=pl.BlockSpec((rh, ncol), lambda i, me, c: (c[0], 0)))
    return pl.pallas_call(
        body, name=name, grid_spec=grid_spec, out_shape=jax.ShapeDtypeStruct((2 * rh, ncol), F32),
        compiler_params=_params(("arbitrary",)),
    )(chip, core, pf, land)


def _sibling_share(fulls, name):
    n = len(fulls)

    def body(*refs):
        outs = refs[n:2 * n]
        send, recv = refs[2 * n:]
        x, y, c = _mesh_pos()
        cps = []
        for w in range(n):
            rh = fulls[w].shape[0] // 2
            mine = outs[w].at[pl.ds(c * rh, rh), :]
            cp = pltpu.make_async_remote_copy(src_ref=mine, dst_ref=mine, send_sem=send.at[w], recv_sem=recv.at[w],
                                              device_id=(x, y, 1 - c), device_id_type=MESH)
            cp.start()
            cps.append(cp)
        for cp in cps:
            cp.wait()

    return pl.pallas_call(
        body, name=name,
        in_specs=[_ANY] * n, out_specs=[_ANY] * n,
        out_shape=[jax.ShapeDtypeStruct(f.shape, F32) for f in fulls],
        input_output_aliases={i: i for i in range(n)},
        scratch_shapes=[pltpu.SemaphoreType.DMA((n,)), pltpu.SemaphoreType.DMA((n,))],
    )(*fulls)


_ROW = {"rel_bias": 128, "sgu_b_s": 136, "norm_ffn1": 144, "norm_mix": 145, "norm_ffn2": 146, "norm_final": 147,
        "b_gate": 148, "sgu_ln_g": 150, "sgu_ln_b": 151}


def _pack_small(gs):
    def body(ws, rel, bs, n1, nm, n2, nf, bg, lg, lb, o_ref):
        o_ref[...] = jnp.zeros_like(o_ref)
        for g in range(SGU_GROUPS):
            o_ref[0:SGU_BLOCK, g * SGU_BLOCK:(g + 1) * SGU_BLOCK] = ws[g]
        o_ref[128:136, 0:REL_PAD] = rel[...]
        o_ref[136:144, 0:SGU_BLOCK] = bs[...]
        o_ref[144:145, :] = n1[...]
        o_ref[145:146, :] = nm[...]
        o_ref[146:147, :] = n2[...]
        o_ref[147:148, :] = nf[...]
        o_ref[148:149, :] = bg[:, 0:D_MODEL]
        o_ref[149:150, :] = bg[:, D_MODEL:2 * D_MODEL]
        o_ref[150:151, 0:D_SGU] = lg[...]
        o_ref[151:152, 0:D_SGU] = lb[...]

    order = ("sgu_w_s", "rel_bias", "sgu_b_s", "norm_ffn1", "norm_mix", "norm_ffn2", "norm_final", "b_gate", "sgu_ln_g",
             "sgu_ln_b")
    return pl.pallas_call(body, name="pack_small", out_shape=jax.ShapeDtypeStruct((SMALL_ROWS, D_MODEL), F32))(
        *[gs[k] for k in order])


def _adam(w, g, m, v):
    m2 = ADAM_B1 * m + (1.0 - ADAM_B1) * g
    v2 = ADAM_B2 * v + (1.0 - ADAM_B2) * (g * g)
    m_hat = m2 / (1.0 - ADAM_B1 ** ADAM_STEP)
    v_hat = v2 / (1.0 - ADAM_B2 ** ADAM_STEP)
    delta = -ADAM_LR * (m_hat / (jnp.sqrt(v_hat) + ADAM_EPS) + ADAM_WD * w)
    return delta, m2, v2


def _adam_small(sin, w, m, v):
    names = SMALL
    k = len(names)

    def body(*refs):
        sin_ref = refs[0]
        w_r, m_r, v_r = refs[1:1 + k], refs[1 + k:1 + 2 * k], refs[1 + 2 * k:1 + 3 * k]
        outs = refs[1 + 3 * k:]
        tot = sin_ref[0] + sin_ref[1] + sin_ref[2] + sin_ref[3]
        for i, name in enumerate(names):
            o = outs[4 * i:4 * i + 4]
            if name == "sgu_w_s":
                for gi in range(SGU_GROUPS):
                    g = tot[0:SGU_BLOCK, gi * SGU_BLOCK:(gi + 1) * SGU_BLOCK]
                    res = (g,) + _adam(w_r[i][gi], g, m_r[i][gi], v_r[i][gi])
                    for ref, val in zip(o, res):
                        ref[gi] = val
                continue
            r0 = _ROW[name]
            if name == "rel_bias":
                g = tot[r0:r0 + HEADS, 0:REL_PAD]
            elif name == "sgu_b_s":
                g = tot[r0:r0 + SGU_GROUPS, 0:SGU_BLOCK]
            elif name == "b_gate":
                g = jnp.concatenate([tot[r0:r0 + 1, :], tot[r0 + 1:r0 + 2, :]], axis=1)
            elif name in ("sgu_ln_g", "sgu_ln_b"):
                g = tot[r0:r0 + 1, 0:D_SGU]
            else:
                g = tot[r0:r0 + 1, :]
            res = (g,) + _adam(w_r[i][...], g, m_r[i][...], v_r[i][...])
            for ref, val in zip(o, res):
                ref[...] = val

    out_shape = []
    for name in names:
        out_shape += [jax.ShapeDtypeStruct(w[name].shape, F32)] * 4
    flat = pl.pallas_call(body, name="adam_small", out_shape=out_shape, compiler_params=_params())(
        sin, *[w[n] for n in names], *[m[n] for n in names], *[v[n] for n in names])
    return {name: tuple(flat[4 * i:4 * i + 4]) for i, name in enumerate(names)}


def _adam_big(w, g, m, v, name):
    r, ncol = w.shape
    tr = 256 if r % 256 == 0 else r // 2

    def body(w_ref, g_ref, m_ref, v_ref, d_ref, m2_ref, v2_ref):
        d_ref[...], m2_ref[...], v2_ref[...] = _adam(w_ref[...], g_ref[...], m_ref[...], v_ref[...])

    spec = pl.BlockSpec((tr, ncol), lambda i: (i, 0))
    return pl.pallas_call(
        body, name=name, grid=(r // tr,), in_specs=[spec] * 4, out_specs=[spec] * 3,
        out_shape=[jax.ShapeDtypeStruct(w.shape, F32)] * 3, compiler_params=_params(("arbitrary",)),
    )(w, g, m, v)


WEIGHTS = ("norm_ffn1", "ffn1_w_gate", "ffn1_w_up", "ffn1_w_down", "norm_mix", "w_in", "b_gate", "rel_bias", "sgu_ln_g",
           "sgu_ln_b", "sgu_w_s", "sgu_b_s", "w_branch_att", "w_branch_sgu", "w_out", "norm_ffn2", "ffn2_w_gate",
           "ffn2_w_up", "ffn2_w_down", "norm_final")


GATE_UP = ("ffn1_w_gate", "ffn1_w_up", "ffn2_w_gate", "ffn2_w_up")
_FFN = ("ffn1_w_gate", "ffn1_w_up", "ffn1_w_down", "ffn2_w_gate", "ffn2_w_up", "ffn2_w_down")
_CAST_GROUPS = ((_FFN, "cast_ffn"), (("w_in",), "cast_w_in"), (("w_branch_att", "w_branch_sgu"), "cast_branch"),
                (("w_out",), "cast_w_out"))


def _big_form(name, a):
    return jnp.swapaxes(a, 1, 2)[0] if name in GATE_UP else a[0]


def _big_back(name, a):
    return jnp.swapaxes(a[None], 1, 2) if name in GATE_UP else a[None]


def _small_form(name, a):
    if name == "norm_final":
        return a.reshape(1, D_MODEL)
    if name == "rel_bias":
        return jnp.pad(a[0], ((0, 0), (0, REL_PAD - N_REL)))
    if name in ("sgu_w_s", "sgu_b_s"):
        return a[0]
    return a


def _small_back(name, a, like):
    if name == "rel_bias":
        a = a[:, :N_REL]
    return a.reshape(like.shape)


def kernel(x, norm_ffn1, ffn1_w_gate, ffn1_w_up, ffn1_w_down, norm_mix, w_in, b_gate, rel_bias, sgu_ln_g, sgu_ln_b, sgu_w_s, sgu_b_s, w_branch_att, w_branch_sgu, w_out, norm_ffn2, ffn2_w_gate, ffn2_w_up, ffn2_w_down, norm_final, loss_target, m_norm_ffn1, m_ffn1_w_gate, m_ffn1_w_up, m_ffn1_w_down, m_norm_mix, m_w_in, m_b_gate, m_rel_bias, m_sgu_ln_g, m_sgu_ln_b, m_sgu_w_s, m_sgu_b_s, m_w_branch_att, m_w_branch_sgu, m_w_out, m_norm_ffn2, m_ffn2_w_gate, m_ffn2_w_up, m_ffn2_w_down, m_norm_final, v_norm_ffn1, v_ffn1_w_gate, v_ffn1_w_up, v_ffn1_w_down, v_norm_mix, v_w_in, v_b_gate, v_rel_bias, v_sgu_ln_g, v_sgu_ln_b, v_sgu_w_s, v_sgu_b_s, v_w_branch_att, v_w_branch_sgu, v_w_out, v_norm_ffn2, v_ffn2_w_gate, v_ffn2_w_up, v_ffn2_w_down, v_norm_final):
    w = dict(norm_ffn1=norm_ffn1, ffn1_w_gate=ffn1_w_gate, ffn1_w_up=ffn1_w_up, ffn1_w_down=ffn1_w_down, norm_mix=norm_mix,
             w_in=w_in, b_gate=b_gate, rel_bias=rel_bias, sgu_ln_g=sgu_ln_g, sgu_ln_b=sgu_ln_b, sgu_w_s=sgu_w_s,
             sgu_b_s=sgu_b_s, w_branch_att=w_branch_att, w_branch_sgu=w_branch_sgu, w_out=w_out, norm_ffn2=norm_ffn2,
             ffn2_w_gate=ffn2_w_gate, ffn2_w_up=ffn2_w_up, ffn2_w_down=ffn2_w_down, norm_final=norm_final)
    m = dict(norm_ffn1=m_norm_ffn1, ffn1_w_gate=m_ffn1_w_gate, ffn1_w_up=m_ffn1_w_up, ffn1_w_down=m_ffn1_w_down,
             norm_mix=m_norm_mix, w_in=m_w_in, b_gate=m_b_gate, rel_bias=m_rel_bias, sgu_ln_g=m_sgu_ln_g,
             sgu_ln_b=m_sgu_ln_b, sgu_w_s=m_sgu_w_s, sgu_b_s=m_sgu_b_s, w_branch_att=m_w_branch_att,
             w_branch_sgu=m_w_branch_sgu, w_out=m_w_out, norm_ffn2=m_norm_ffn2, ffn2_w_gate=m_ffn2_w_gate,
             ffn2_w_up=m_ffn2_w_up, ffn2_w_down=m_ffn2_w_down, norm_final=m_norm_final)
    v = dict(norm_ffn1=v_norm_ffn1, ffn1_w_gate=v_ffn1_w_gate, ffn1_w_up=v_ffn1_w_up, ffn1_w_down=v_ffn1_w_down,
             norm_mix=v_norm_mix, w_in=v_w_in, b_gate=v_b_gate, rel_bias=v_rel_bias, sgu_ln_g=v_sgu_ln_g,
             sgu_ln_b=v_sgu_ln_b, sgu_w_s=v_sgu_w_s, sgu_b_s=v_sgu_b_s, w_branch_att=v_w_branch_att,
             w_branch_sgu=v_w_branch_sgu, w_out=v_w_out, norm_ffn2=v_norm_ffn2, ffn2_w_gate=v_ffn2_w_gate,
             ffn2_w_up=v_ffn2_w_up, ffn2_w_down=v_ffn2_w_down, norm_final=v_norm_final)

    core = lax.axis_index("c").astype(jnp.int32).reshape(1)
    chip = (2 * lax.axis_index("x") + lax.axis_index("y")).astype(jnp.int32).reshape(1)

    wk = {n: _big_form(n, w[n]) for n in BIG}
    slots = {}
    for names, call in _CAST_GROUPS:
        slots.update(zip(names, _cast_slots([wk[n] for n in names], chip, call)))
    slots.update(zip(G_FFN1, _exchange(_ag_payload([slots[n] for n in G_FFN1]), "allgather_ffn1")))
    ws = {n: _small_form(n, w[n]) for n in SMALL}
    loss, gx, shard_grads, small_sums = _local_step(x[0], loss_target[0], slots, ws, (core, chip))

    small = _adam_small(small_sums, ws, {n: _small_form(n, m[n]) for n in SMALL},
                        {n: _small_form(n, v[n]) for n in SMALL})
    grad, delta, new_m, new_v = {}, {}, {}, {}
    for n in SMALL:
        grad[n], delta[n], new_m[n], new_v[n] = (_small_back(n, a, w[n]) for a in small[n])
    for n in BIG:
        g2 = shard_grads[n]
        d2, m2, v2 = _adam_big(wk[n], g2, _big_form(n, m[n]), _big_form(n, v[n]), "adam_" + n)
        grad[n], delta[n], new_m[n], new_v[n] = (_big_back(n, a) for a in (g2, d2, m2, v2))

    total = lax.psum(loss[0, 0], ("x", "y", "c"))
    return (total, gx.reshape(x.shape), *[grad[n] for n in WEIGHTS], *[delta[n] for n in WEIGHTS],
            *[new_m[n] for n in WEIGHTS], *[new_v[n] for n in WEIGHTS])
```

```python
import functools

import jax
import jax.numpy as jnp
from jax import lax
from jax.experimental import pallas as pl
from jax.experimental.pallas import tpu as pltpu

F32 = jnp.float32
BF16 = jnp.bfloat16

D_MODEL = 1024
N_SHARD = 4
D_FF = 2816
FF_S = D_FF // N_SHARD
D_ATT = 512
D_SGU = 512
D_IN = 3 * D_ATT + 2 * D_SGU + 2 * D_MODEL
IN_S = D_IN // N_SHARD
BR_S = D_MODEL // N_SHARD
HEADS = 8
HEAD_DIM = 64
CHUNK = 64
N_LEFT = 8
BAND = (N_LEFT + 1) * CHUNK
REL_CLIP = 256
N_REL = 2 * REL_CLIP + 1
REL_PAD = 640
SGU_BLOCK = 128
SGU_GROUPS = 8
SGU_GDIM = 64
EPS = 1e-6
NEG_INF = -1e30

ATT_ROWS = 2 * CHUNK
ATT_KEYS = BAND + CHUNK
ATT_PAD = N_LEFT * CHUNK

ADAM_LR = 0.001
ADAM_B1 = 0.9
ADAM_B2 = 0.999
ADAM_EPS = 1e-08
ADAM_WD = 0.01
ADAM_STEP = 10

TM = 256
TW = 512
VMEM_LIMIT = 56 * 1024 * 1024

SMALL_ROWS = 160
LOSS_ROW = 152
MESH = pl.DeviceIdType.MESH

_NT = (((1,), (1,)), ((), ()))
_TN = (((0,), (0,)), ((), ()))


def _params(sem=None):
    return pltpu.CompilerParams(dimension_semantics=sem, vmem_limit_bytes=VMEM_LIMIT)


def _const_spec(shape):
    nd = len(shape)
    return pl.BlockSpec(shape, lambda *_: (0,) * nd, pipeline_mode=pl.Buffered(1))


def _acc_spec(shape):
    nd = len(shape)
    return pl.BlockSpec(shape, lambda *_: (0,) * nd)


def _row_spec(tm, ncols, off=0):
    return pl.BlockSpec((tm, ncols), lambda i: (i + off, 0))


def _row3_spec(tm, ncols):
    return pl.BlockSpec((N_SHARD, tm, ncols), lambda i: (0, i, 0))


def _dot(a, b):
    return jnp.dot(a, b, preferred_element_type=F32)


def _dot_nt(a, b):
    return lax.dot_general(a, b, _NT, preferred_element_type=F32)


def _dot_tn(a, b):
    return lax.dot_general(a, b, _TN, preferred_element_type=F32)


def _rms_fwd(x, g):
    r = lax.rsqrt(jnp.mean(x * x, axis=-1, keepdims=True) + EPS)
    xhat = x * r
    return xhat, r, xhat * g


def _rms_bwd(dh, xhat, r, g):
    dxhat = dh * g
    dx = r * (dxhat - xhat * jnp.mean(dxhat * xhat, axis=-1, keepdims=True))
    dg = jnp.sum(dh * xhat, axis=0, keepdims=True)
    return dx, dg


def _sigmoid(x):
    return 1.0 / (1.0 + jnp.exp(-x))


def _edges(n_steps):
    return [(0, True), (n_steps - 1, False)]


def _ffn_fwd(x, g, wg, wu, wd, name, payload=None):
    T = x.shape[0]

    def body(x_ref, g_ref, wg_ref, wu_ref, wd_ref, xo_ref, h_ref, a_ref, b_ref):
        xv = x_ref[...]
        hb = _rms_fwd(xv, g_ref[...])[2].astype(BF16)
        h_ref[...] = hb
        acc = jnp.zeros((TM, D_MODEL), F32)
        for s in range(N_SHARD):
            a = _dot_nt(hb, wg_ref[s])
            b = _dot_nt(hb, wu_ref[s])
            a_ref[s] = a.astype(BF16)
            b_ref[s] = b.astype(BF16)
            sv = a * _sigmoid(a) * b
            acc += _dot(sv.astype(BF16), wd_ref[s])
        xo_ref[...] = xv + 0.5 * acc

    return _call(
        body, payload, name=name, grid=(T // TM,), when=_edges(T // TM), sem=("arbitrary",),
        in_specs=[_row_spec(TM, D_MODEL), _const_spec((1, D_MODEL)), _const_spec(wg.shape), _const_spec(wu.shape),
                  _const_spec(wd.shape)],
        out_specs=[_row_spec(TM, D_MODEL), _row_spec(TM, D_MODEL), _row3_spec(TM, FF_S), _row3_spec(TM, FF_S)],
        out_shape=[jax.ShapeDtypeStruct((T, D_MODEL), F32), jax.ShapeDtypeStruct((T, D_MODEL), BF16),
                   jax.ShapeDtypeStruct((N_SHARD, T, FF_S), BF16), jax.ShapeDtypeStruct((N_SHARD, T, FF_S), BF16)],
        operands=(x, g, wg, wu, wd))


def _ffn_dgrad(dout, x, a, b, g, wg, wu, wd, name, payload=None):
    T = x.shape[0]

    def body(do_ref, x_ref, a_ref, b_ref, g_ref, wg_ref, wu_ref, wd_ref, dx_ref, da_ref, db_ref, dg_ref):
        do = do_ref[...]
        dob = do.astype(BF16)
        dh = jnp.zeros((TM, D_MODEL), F32)
        for s in range(N_SHARD):
            ds = 0.5 * _dot_nt(dob, wd_ref[s])
            av = a_ref[s].astype(F32)
            bv = b_ref[s].astype(F32)
            sig = _sigmoid(av)
            da = (ds * bv * (sig * (1.0 + av * (1.0 - sig)))).astype(BF16)
            db = (ds * (av * sig)).astype(BF16)
            da_ref[s] = da
            db_ref[s] = db
            dh += _dot(da, wg_ref[s]) + _dot(db, wu_ref[s])
        gv = g_ref[...]
        xhat, r, _ = _rms_fwd(x_ref[...], gv)
        dxn, dg = _rms_bwd(dh, xhat, r, gv)
        dx_ref[...] = do + dxn

        @pl.when(pl.program_id(0) == 0)
        def _():
            dg_ref[...] = jnp.zeros_like(dg_ref)

        dg_ref[...] += dg

    return _call(
        body, payload, name=name, grid=(T // TM,), when=_edges(T // TM), sem=("arbitrary",),
        in_specs=[_row_spec(TM, D_MODEL), _row_spec(TM, D_MODEL), _row3_spec(TM, FF_S), _row3_spec(TM, FF_S),
                  _const_spec((1, D_MODEL)), _const_spec(wg.shape), _const_spec(wu.shape), _const_spec(wd.shape)],
        out_specs=[_row_spec(TM, D_MODEL), _row3_spec(TM, FF_S), _row3_spec(TM, FF_S), _acc_spec((1, D_MODEL))],
        out_shape=[jax.ShapeDtypeStruct((T, D_MODEL), F32), jax.ShapeDtypeStruct((N_SHARD, T, FF_S), BF16),
                   jax.ShapeDtypeStruct((N_SHARD, T, FF_S), BF16), jax.ShapeDtypeStruct((1, D_MODEL), F32)],
        operands=(dout, x, a, b, g, wg, wu, wd))


def _ffn_wgrad(h, dout, a, b, da, db, name, payload=None):
    T = h.shape[0]

    def body(h_ref, do_ref, a_ref, b_ref, da_ref, db_ref, gwg_ref, gwu_ref, gwd_ref):
        @pl.when(pl.program_id(1) == 0)
        def _():
            gwg_ref[...] = jnp.zeros_like(gwg_ref)
            gwu_ref[...] = jnp.zeros_like(gwu_ref)
            gwd_ref[...] = jnp.zeros_like(gwd_ref)

        hv = h_ref[...]
        dob = do_ref[...].astype(BF16)
        av = a_ref[0].astype(F32)
        sv = (0.5 * av * _sigmoid(av) * b_ref[0].astype(F32)).astype(BF16)
        gwg_ref[0] += _dot_tn(da_ref[0], hv)
        gwu_ref[0] += _dot_tn(db_ref[0], hv)
        gwd_ref[0] += _dot_tn(sv, dob)

    tok = pl.BlockSpec((TW, D_MODEL), lambda s, i: (i, 0))
    act = pl.BlockSpec((1, TW, FF_S), lambda s, i: (s, i, 0))
    return _call(
        body, payload, name=name, grid=(N_SHARD, T // TW), when=_edges(N_SHARD * (T // TW)),
        sem=("arbitrary", "arbitrary"),
        in_specs=[tok, tok, act, act, act, act],
        out_specs=[pl.BlockSpec((1, FF_S, D_MODEL), lambda s, i: (s, 0, 0))] * 3,
        out_shape=[jax.ShapeDtypeStruct((N_SHARD, FF_S, D_MODEL), F32)] * 3,
        operands=(h, dout, a, b, da, db))


def _in_fwd(x, g, w_in):
    T = x.shape[0]

    def body(x_ref, g_ref, w_ref, h_ref, qkv_ref, zs_ref, gl_ref):
        hb = _rms_fwd(x_ref[...], g_ref[...])[2].astype(BF16)
        h_ref[...] = hb
        z0 = _dot(hb, w_ref[0])
        qkv_ref[:, 0:IN_S] = z0.astype(BF16)
        z1 = _dot(hb, w_ref[1])
        qkv_ref[:, IN_S:3 * D_ATT] = z1[:, 0:384].astype(BF16)
        zs_ref[:, 0:768] = z1[:, 384:IN_S]
        z2 = _dot(hb, w_ref[2])
        zs_ref[:, 768:1024] = z2[:, 0:256]
        gl_ref[:, 0:896] = z2[:, 256:IN_S]
        gl_ref[:, 896:2048] = _dot(hb, w_ref[3])

    return pl.pallas_call(
        body, name="in_fwd", grid=(T // TM,),
        in_specs=[_row_spec(TM, D_MODEL), _const_spec((1, D_MODEL)), _const_spec(w_in.shape)],
        out_specs=[_row_spec(TM, D_MODEL), _row_spec(TM, 3 * D_ATT), _row_spec(TM, 2 * D_SGU), _row_spec(TM, 2 * D_MODEL)],
        out_shape=[jax.ShapeDtypeStruct((T, D_MODEL), BF16), jax.ShapeDtypeStruct((T, 3 * D_ATT), BF16),
                   jax.ShapeDtypeStruct((T, 2 * D_SGU), F32), jax.ShapeDtypeStruct((T, 2 * D_MODEL), F32)],
        compiler_params=_params(("arbitrary",)),
    )(x, g, w_in)


def _in_dgrad(dx_res, x, g, w_in, dq, dk, dv, dzs, dgl):
    T = x.shape[0]

    def body(dxr_ref, x_ref, g_ref, w_ref, dq_ref, dk_ref, dv_ref, dzs_ref, dgl_ref, dx_ref, dz_ref, dg_ref):
        dz = jnp.concatenate([dq_ref[...], dk_ref[...].astype(BF16), dv_ref[...].astype(BF16), dzs_ref[...], dgl_ref[...]],
                             axis=1)
        dz_ref[...] = dz
        dh = jnp.zeros((TM, D_MODEL), F32)
        for s in range(N_SHARD):
            dh += _dot_nt(dz[:, s * IN_S:(s + 1) * IN_S], w_ref[s])
        gv = g_ref[...]
        xhat, r, _ = _rms_fwd(x_ref[...], gv)
        dxn, dg = _rms_bwd(dh, xhat, r, gv)
        dx_ref[...] = dxr_ref[...] + dxn

        @pl.when(pl.program_id(0) == 0)
        def _():
            dg_ref[...] = jnp.zeros_like(dg_ref)

        dg_ref[...] += dg

    pad_blocks = ATT_PAD // TM
    return pl.pallas_call(
        body, name="in_dgrad", grid=(T // TM,),
        in_specs=[_row_spec(TM, D_MODEL), _row_spec(TM, D_MODEL), _const_spec((1, D_MODEL)), _const_spec(w_in.shape),
                  _row_spec(TM, D_ATT), _row_spec(TM, D_ATT, pad_blocks), _row_spec(TM, D_ATT, pad_blocks),
                  _row_spec(TM, 2 * D_SGU), _row_spec(TM, 2 * D_MODEL)],
        out_specs=[_row_spec(TM, D_MODEL), _row_spec(TM, D_IN), _acc_spec((1, D_MODEL))],
        out_shape=[jax.ShapeDtypeStruct((T, D_MODEL), F32), jax.ShapeDtypeStruct((T, D_IN), BF16),
                   jax.ShapeDtypeStruct((1, D_MODEL), F32)],
        compiler_params=_params(("arbitrary",)),
    )(dx_res, x, g, w_in, dq, dk, dv, dzs, dgl)


def _in_wgrad(h, dz):
    T = h.shape[0]

    def body(h_ref, dz_ref, gw_ref):
        @pl.when(pl.program_id(1) == 0)
        def _():
            gw_ref[...] = jnp.zeros_like(gw_ref)

        gw_ref[0] += _dot_tn(h_ref[...], dz_ref[...])

    return pl.pallas_call(
        body, name="in_wgrad", grid=(N_SHARD, T // TW),
        in_specs=[pl.BlockSpec((TW, D_MODEL), lambda s, i: (i, 0)), pl.BlockSpec((TW, IN_S), lambda s, i: (i, s))],
        out_specs=pl.BlockSpec((1, D_MODEL, IN_S), lambda s, i: (s, 0, 0)),
        out_shape=jax.ShapeDtypeStruct((N_SHARD, D_MODEL, IN_S), F32),
        compiler_params=_params(("arbitrary", "arbitrary")),
    )(h, dz)


def _rel_onehot():
    r = lax.broadcasted_iota(jnp.int32, (REL_PAD, REL_PAD), 0)
    n = lax.broadcasted_iota(jnp.int32, (REL_PAD, REL_PAD), 1)
    idx = jnp.clip(BAND - 1 - n, -REL_CLIP, REL_CLIP) + REL_CLIP
    return jnp.where(r == idx, 1.0, 0.0).astype(BF16)


def _split3(v):
    p1 = v.astype(BF16)
    r1 = v - p1.astype(F32)
    p2 = r1.astype(BF16)
    p3 = (r1 - p2.astype(F32)).astype(BF16)
    return p1, p2, p3


def _relbias_fwd(tab_pad):
    def body(t_ref, o_ref):
        oh = _rel_onehot()
        acc = jnp.zeros((HEADS, REL_PAD), F32)
        for p in _split3(t_ref[...]):
            acc += _dot(p, oh)
        o_ref[...] = acc

    return pl.pallas_call(body, name="relbias_fwd", out_shape=jax.ShapeDtypeStruct((HEADS, REL_PAD), F32))(tab_pad)


def _relbias_bwd(z):
    def body(z_ref, o_ref):
        oh = _rel_onehot()
        dt2 = jnp.sum(z_ref[...], axis=1)
        acc = jnp.zeros((HEADS, REL_PAD), F32)
        for p in _split3(dt2):
            acc += _dot_nt(p, oh)
        o_ref[...] = acc

    return pl.pallas_call(body, name="relbias_bwd", out_shape=jax.ShapeDtypeStruct((HEADS, REL_PAD), F32))(z)


def _bias_blocks(t2):
    flat = jnp.tile(t2, (1, CHUNK))
    skew = flat[:, :CHUNK * (REL_PAD - 1)].reshape(HEADS, CHUNK, REL_PAD - 1)
    bias = skew[:, :, CHUNK - 1:CHUNK - 1 + BAND]
    slabs = [jnp.pad(bias, ((0, 0), (0, 0), (CHUNK * c, ATT_KEYS - BAND - CHUNK * c)), constant_values=NEG_INF)
             for c in range(2)]
    return jnp.concatenate(slabs, axis=1)


def _unskew(db2):
    out = []
    for c in range(2):
        slab = db2[:, CHUNK * c:CHUNK * (c + 1), CHUNK * c:CHUNK * c + BAND]
        y = jnp.pad(slab, ((0, 0), (0, 0), (CHUNK - 1, REL_PAD - BAND - CHUNK + 1)))
        yf = jnp.pad(y.reshape(HEADS, CHUNK * REL_PAD), ((0, 0), (0, CHUNK)))
        out.append(yf.reshape(HEADS, CHUNK, REL_PAD + 1)[:, :, :REL_PAD])
    return jnp.concatenate(out, axis=1)


def _att_load(qkv_hbm, q_s, k_s, v_s, sem, T):
    copies = [pltpu.make_async_copy(qkv_hbm.at[:, 0:D_ATT], q_s, sem.at[0]),
              pltpu.make_async_copy(qkv_hbm.at[:, D_ATT:2 * D_ATT], k_s.at[pl.ds(ATT_PAD, T), :], sem.at[1]),
              pltpu.make_async_copy(qkv_hbm.at[:, 2 * D_ATT:3 * D_ATT], v_s.at[pl.ds(ATT_PAD, T), :], sem.at[2])]
    for cp in copies:
        cp.start()
    k_s[0:ATT_PAD, :] = jnp.zeros((ATT_PAD, D_ATT), BF16)
    v_s[0:ATT_PAD, :] = jnp.zeros((ATT_PAD, D_ATT), BF16)
    for cp in copies:
        cp.wait()


def _head(v, h):
    return v[:, h * HEAD_DIM:(h + 1) * HEAD_DIM]


def _rows(v, h):
    return v[h * ATT_ROWS:(h + 1) * ATT_ROWS]


def _att_exp(qs, kw, bias_ref, valid):
    s = jnp.concatenate([_dot_nt(_head(qs, h), _head(kw, h)) + bias_ref[h] for h in range(HEADS)], axis=0)
    if valid is not None:
        s = jnp.where(valid, s, NEG_INF)
    e = jnp.exp(s - jnp.max(s, axis=-1, keepdims=True))
    return e, 1.0 / jnp.sum(e, axis=-1, keepdims=True)


def _att_blocks(T, block):
    n_edge = min(ATT_PAD // ATT_ROWS, T // ATT_ROWS)

    def edge(i, carry):
        r0 = i * ATT_ROWS
        block(i, (lax.broadcasted_iota(jnp.int32, (1, ATT_KEYS), 1) + (r0 - ATT_PAD)) >= 0)
        return carry

    def inner(i, carry):
        block(i, None)
        return carry

    lax.fori_loop(0, n_edge, edge, 0)
    lax.fori_loop(n_edge, T // ATT_ROWS, inner, 0)


def _att_fwd(qkv, bias2, payload=None):
    T = qkv.shape[0]

    def body(qkv_hbm, bias_ref, y_ref, q_s, k_s, v_s, sem):
        _att_load(qkv_hbm, q_s, k_s, v_s, sem, T)

        def block(i, valid):
            r0 = pl.multiple_of(i * ATT_ROWS, ATT_ROWS)
            qs = q_s[pl.ds(r0, ATT_ROWS), :] * (HEAD_DIM ** -0.5)
            kw = k_s[pl.ds(r0, ATT_KEYS), :]
            vw = v_s[pl.ds(r0, ATT_KEYS), :]
            e, rinv = _att_exp(qs, kw, bias_ref, valid)
            eb = e.astype(BF16)
            outs = [_dot(_rows(eb, h), _head(vw, h)) * _rows(rinv, h) for h in range(HEADS)]
            y_ref[pl.ds(r0, ATT_ROWS), :] = jnp.concatenate(outs, axis=1).astype(BF16)

        _att_blocks(T, block)

    return _call(
        body, payload, name="att_fwd", grid=None,
        in_specs=[pl.BlockSpec(memory_space=pl.ANY), pl.BlockSpec(memory_space=pltpu.VMEM)],
        out_specs=[pl.BlockSpec(memory_space=pltpu.VMEM)],
        out_shape=[jax.ShapeDtypeStruct((T, D_ATT), BF16)],
        scratch_shapes=[pltpu.VMEM((T, D_ATT), BF16), pltpu.VMEM((T + ATT_PAD, D_ATT), BF16),
                        pltpu.VMEM((T + ATT_PAD, D_ATT), BF16), pltpu.SemaphoreType.DMA((3,))],
        operands=(qkv, bias2))


def _att_bwd(qkv, dy, bias2, payload=None):
    T = qkv.shape[0]

    def body(qkv_hbm, dy_ref, bias_ref, dq_ref, dk_ref, dv_ref, db_ref, q_s, k_s, v_s, sem):
        _att_load(qkv_hbm, q_s, k_s, v_s, sem, T)
        dk_ref[...] = jnp.zeros_like(dk_ref)
        dv_ref[...] = jnp.zeros_like(dv_ref)
        db_ref[...] = jnp.zeros_like(db_ref)


        def block(i, valid):
            r0 = pl.multiple_of(i * ATT_ROWS, ATT_ROWS)
            qs = q_s[pl.ds(r0, ATT_ROWS), :] * (HEAD_DIM ** -0.5)
            kw = k_s[pl.ds(r0, ATT_KEYS), :]
            vw = v_s[pl.ds(r0, ATT_KEYS), :]
            dyb = dy_ref[pl.ds(r0, ATT_ROWS), :]
            e, rinv = _att_exp(qs, kw, bias_ref, valid)
            p = e * rinv
            dp = jnp.concatenate([_dot_nt(_head(dyb, h), _head(vw, h)) for h in range(HEADS)], axis=0)
            ds = p * (dp - jnp.sum(p * dp, axis=-1, keepdims=True))
            db_ref[...] += ds.reshape(HEADS, ATT_ROWS, ATT_KEYS)
            dsb = ds.astype(BF16)
            pb = p.astype(BF16)
            dq = [_dot(_rows(dsb, h), _head(kw, h)) for h in range(HEADS)]
            dk = [_dot_tn(_rows(dsb, h), _head(qs, h)) for h in range(HEADS)]
            dv = [_dot_tn(_rows(pb, h), _head(dyb, h)) for h in range(HEADS)]
            dq_ref[pl.ds(r0, ATT_ROWS), :] = (jnp.concatenate(dq, axis=1) * (HEAD_DIM ** -0.5)).astype(BF16)
            dk_ref[pl.ds(r0, ATT_KEYS), :] += jnp.concatenate(dk, axis=1)
            dv_ref[pl.ds(r0, ATT_KEYS), :] += jnp.concatenate(dv, axis=1)

        _att_blocks(T, block)

    vmem = pl.BlockSpec(memory_space=pltpu.VMEM)
    return _call(
        body, payload, name="att_bwd", grid=None,
        in_specs=[pl.BlockSpec(memory_space=pl.ANY), vmem, vmem],
        out_specs=[vmem, vmem, vmem, vmem],
        out_shape=[jax.ShapeDtypeStruct((T, D_ATT), BF16), jax.ShapeDtypeStruct((T + ATT_PAD, D_ATT), F32),
                   jax.ShapeDtypeStruct((T + ATT_PAD, D_ATT), F32), jax.ShapeDtypeStruct((HEADS, ATT_ROWS, ATT_KEYS), F32)],
        scratch_shapes=[pltpu.VMEM((T, D_ATT), BF16), pltpu.VMEM((T + ATT_PAD, D_ATT), BF16),
                        pltpu.VMEM((T + ATT_PAD, D_ATT), BF16), pltpu.SemaphoreType.DMA((3,))],
        operands=(qkv, dy, bias2))


_GELU_C = 0.7978845608028654
_GELU_A = 0.044715


def _gelu(x):
    t = jnp.tanh(_GELU_C * (x + _GELU_A * x * x * x))
    return 0.5 * x * (1.0 + t), t


def _gelu_grad(x, t):
    return 0.5 * (1.0 + t) + 0.5 * x * (1.0 - t * t) * _GELU_C * (1.0 + 3.0 * _GELU_A * x * x)


def _group_masks():
    col = lax.broadcasted_iota(jnp.int32, (SGU_GROUPS, D_SGU), 1) // SGU_GDIM
    grp = lax.broadcasted_iota(jnp.int32, (SGU_GROUPS, D_SGU), 0)
    return jnp.where(col == grp, 1.0, 0.0).astype(F32)


def _causal_mask(transposed=False):
    i = lax.broadcasted_iota(jnp.int32, (SGU_BLOCK, SGU_BLOCK), 0) // CHUNK
    j = lax.broadcasted_iota(jnp.int32, (SGU_BLOCK, SGU_BLOCK), 1) // CHUNK
    return (j >= i) if transposed else (i >= j)


def _sgu_norm(zs, lng, lnb):
    gz, t = _gelu(zs)
    u = gz[:, 0:D_SGU]
    vs = gz[:, D_SGU:2 * D_SGU]
    xc = vs - jnp.mean(vs, axis=-1, keepdims=True)
    rstd = lax.rsqrt(jnp.mean(xc * xc, axis=-1, keepdims=True) + EPS)
    xhat = xc * rstd
    return t, u, xhat, rstd, xhat * lng + lnb


def _sgu_mix(vn_blk, w_ref, bst, gm):
    mask = _causal_mask()
    s = jnp.zeros((SGU_BLOCK, D_SGU), F32)
    for g in range(SGU_GROUPS):
        wm = jnp.where(mask, w_ref[g], 0.0).astype(BF16)
        s += _dot(wm, (vn_blk * gm[g:g + 1, :]).astype(BF16))
        s += bst[:, g:g + 1] * gm[g:g + 1, :]
    return s


def _sgu_fwd(zs, lng, lnb, w_s, bst):
    T = zs.shape[0]
    nblk = TM // SGU_BLOCK

    def body(zs_ref, lng_ref, lnb_ref, w_ref, bst_ref, y_ref):
        _, u, _, _, vn = _sgu_norm(zs_ref[...], lng_ref[...], lnb_ref[...])
        gm = _group_masks()
        bst_v = bst_ref[...]
        for n in range(nblk):
            rows = slice(n * SGU_BLOCK, (n + 1) * SGU_BLOCK)
            s = _sgu_mix(vn[rows], w_ref, bst_v, gm)
            y_ref[rows, :] = (u[rows] * s).astype(BF16)

    return pl.pallas_call(
        body, name="sgu_fwd", grid=(T // TM,),
        in_specs=[_row_spec(TM, 2 * D_SGU), _const_spec((1, D_SGU)), _const_spec((1, D_SGU)),
                  _const_spec(w_s.shape), _const_spec(bst.shape)],
        out_specs=_row_spec(TM, D_SGU),
        out_shape=jax.ShapeDtypeStruct((T, D_SGU), BF16),
        compiler_params=_params(("arbitrary",)),
    )(zs, lng, lnb, w_s, bst)


def _sgu_bwd(zs, dy, lng, lnb, w_s, w_st, bst):
    T = zs.shape[0]
    nblk = TM // SGU_BLOCK

    def body(zs_ref, dy_ref, lng_ref, lnb_ref, w_ref, wt_ref, bst_ref, dzs_ref, dw_ref, dbt_ref, dlg_ref, dlb_ref):
        @pl.when(pl.program_id(0) == 0)
        def _():
            dw_ref[...] = jnp.zeros_like(dw_ref)
            dbt_ref[...] = jnp.zeros_like(dbt_ref)
            dlg_ref[...] = jnp.zeros_like(dlg_ref)
            dlb_ref[...] = jnp.zeros_like(dlb_ref)

        zs_v = zs_ref[...]
        lng_v = lng_ref[...]
        t, u, xhat, rstd, vn = _sgu_norm(zs_v, lng_v, lnb_ref[...])
        gm = _group_masks()
        bst_v = bst_ref[...]
        mask = _causal_mask()
        mask_t = _causal_mask(transposed=True)
        dyv = dy_ref[...].astype(F32)
        lane8 = lax.broadcasted_iota(jnp.int32, (1, SGU_GROUPS), 1)
        du_rows, dvn_rows = [], []
        for n in range(nblk):
            rows = slice(n * SGU_BLOCK, (n + 1) * SGU_BLOCK)
            vn_b = vn[rows]
            s = _sgu_mix(vn_b, w_ref, bst_v, gm)
            du_rows.append(dyv[rows] * s)
            dsb = dyv[rows] * u[rows]
            vnb16 = vn_b.astype(BF16)
            dvn = jnp.zeros((SGU_BLOCK, D_SGU), F32)
            dbt = jnp.zeros((SGU_BLOCK, SGU_GROUPS), F32)
            for g in range(SGU_GROUPS):
                dsg = dsb * gm[g:g + 1, :]
                dsg16 = dsg.astype(BF16)
                wmt = jnp.where(mask_t, wt_ref[g], 0.0).astype(BF16)
                dvn += _dot(wmt, dsg16)
                dw_ref[g] += jnp.where(mask, _dot_nt(dsg16, vnb16), 0.0)
                dbt += jnp.sum(dsg, axis=-1, keepdims=True) * jnp.where(lane8 == g, 1.0, 0.0)
            dbt_ref[...] += dbt
            dvn_rows.append(dvn)
        du = jnp.concatenate(du_rows, axis=0)
        dvn = jnp.concatenate(dvn_rows, axis=0)
        dlg_ref[...] += jnp.sum(dvn * xhat, axis=0, keepdims=True)
        dlb_ref[...] += jnp.sum(dvn, axis=0, keepdims=True)
        dxhat = dvn * lng_v
        dvs = rstd * (dxhat - jnp.mean(dxhat, axis=-1, keepdims=True)
                      - xhat * jnp.mean(dxhat * xhat, axis=-1, keepdims=True))
        dgz = jnp.concatenate([du, dvs], axis=1)
        dzs_ref[...] = (dgz * _gelu_grad(zs_v, t)).astype(BF16)

    return pl.pallas_call(
        body, name="sgu_bwd", grid=(T // TM,),
        in_specs=[_row_spec(TM, 2 * D_SGU), _row_spec(TM, D_SGU), _const_spec((1, D_SGU)), _const_spec((1, D_SGU)),
                  _const_spec(w_s.shape), _const_spec(w_st.shape), _const_spec(bst.shape)],
        out_specs=[_row_spec(TM, 2 * D_SGU), _acc_spec(w_s.shape), _acc_spec(bst.shape), _acc_spec((1, D_SGU)),
                   _acc_spec((1, D_SGU))],
        out_shape=[jax.ShapeDtypeStruct((T, 2 * D_SGU), BF16), jax.ShapeDtypeStruct(w_s.shape, F32),
                   jax.ShapeDtypeStruct(bst.shape, F32), jax.ShapeDtypeStruct((1, D_SGU), F32),
                   jax.ShapeDtypeStruct((1, D_SGU), F32)],
        compiler_params=_params(("arbitrary",)),
    )(zs, dy, lng, lnb, w_s, w_st, bst)


def _cols(v, s):
    return v[:, s * BR_S:(s + 1) * BR_S]


def _merge_fwd(x, y_att, y_sgu, gl, b_gate, wba, wbs, wo):
    T = x.shape[0]

    def body(x_ref, ya_ref, ys_ref, gl_ref, bg_ref, wba_ref, wbs_ref, wo_ref, xo_ref, m_ref, pa_ref, ps_ref):
        ya = ya_ref[...]
        ys = ys_ref[...]
        pa = jnp.concatenate([_dot(ya, wba_ref[s]) for s in range(N_SHARD)], axis=1)
        ps = jnp.concatenate([_dot(ys, wbs_ref[s]) for s in range(N_SHARD)], axis=1)
        g = _sigmoid(gl_ref[...] + bg_ref[...])
        mb = (g[:, 0:D_MODEL] * pa + g[:, D_MODEL:2 * D_MODEL] * ps).astype(BF16)
        m_ref[...] = mb
        pa_ref[...] = pa.astype(BF16)
        ps_ref[...] = ps.astype(BF16)
        acc = jnp.zeros((TM, D_MODEL), F32)
        for s in range(N_SHARD):
            acc += _dot(_cols(mb, s), wo_ref[s])
        xo_ref[...] = x_ref[...] + acc

    tokd = jax.ShapeDtypeStruct((T, D_MODEL), BF16)
    return pl.pallas_call(
        body, name="merge_fwd", grid=(T // TM,),
        in_specs=[_row_spec(TM, D_MODEL), _row_spec(TM, D_ATT), _row_spec(TM, D_SGU), _row_spec(TM, 2 * D_MODEL),
                  _const_spec((1, 2 * D_MODEL)), _const_spec(wba.shape), _const_spec(wbs.shape), _const_spec(wo.shape)],
        out_specs=[_row_spec(TM, D_MODEL)] * 4,
        out_shape=[jax.ShapeDtypeStruct((T, D_MODEL), F32), tokd, tokd, tokd],
        compiler_params=_params(("arbitrary",)),
    )(x, y_att, y_sgu, gl, b_gate, wba, wbs, wo)


def _merge_bwd(dx, y_att, y_sgu, gl, merged, pa, ps, b_gate, wba, wbs, wo, payload=None):
    T = dx.shape[0]

    def body(dx_ref, ya_ref, ys_ref, gl_ref, m_ref, pa_ref, ps_ref, bg_ref, wba_ref, wbs_ref, wo_ref,
             dya_ref, dys_ref, dgl_ref, dbg_ref, gwba_ref, gwbs_ref, gwo_ref):
        @pl.when(pl.program_id(0) == 0)
        def _():
            dbg_ref[...] = jnp.zeros_like(dbg_ref)
            gwba_ref[...] = jnp.zeros_like(gwba_ref)
            gwbs_ref[...] = jnp.zeros_like(gwbs_ref)
            gwo_ref[...] = jnp.zeros_like(gwo_ref)

        dxb = dx_ref[...].astype(BF16)
        dm = jnp.concatenate([_dot_nt(dxb, wo_ref[s]) for s in range(N_SHARD)], axis=1)
        g = _sigmoid(gl_ref[...] + bg_ref[...])
        ga = g[:, 0:D_MODEL]
        gs = g[:, D_MODEL:2 * D_MODEL]
        dpa = (dm * ga).astype(BF16)
        dps = (dm * gs).astype(BF16)
        dgl = jnp.concatenate([dm * pa_ref[...].astype(F32) * ga * (1.0 - ga),
                               dm * ps_ref[...].astype(F32) * gs * (1.0 - gs)], axis=1)
        dgl_ref[...] = dgl.astype(BF16)
        dbg_ref[...] += jnp.sum(dgl, axis=0, keepdims=True)
        ya = ya_ref[...]
        ys = ys_ref[...]
        mb = m_ref[...]
        dya = jnp.zeros((TM, D_ATT), F32)
        dys = jnp.zeros((TM, D_SGU), F32)
        for s in range(N_SHARD):
            dya += _dot_nt(_cols(dpa, s), wba_ref[s])
            dys += _dot_nt(_cols(dps, s), wbs_ref[s])
            gwo_ref[s] += _dot_tn(_cols(mb, s), dxb)
            gwba_ref[s] += _dot_tn(ya, _cols(dpa, s))
            gwbs_ref[s] += _dot_tn(ys, _cols(dps, s))
        dya_ref[...] = dya.astype(BF16)
        dys_ref[...] = dys.astype(BF16)

    return _call(
        body, payload, name="merge_bwd", grid=(T // TM,), when=_edges(T // TM), sem=("arbitrary",),
        operands=(dx, y_att, y_sgu, gl, merged, pa, ps, b_gate, wba, wbs, wo),
        in_specs=[_row_spec(TM, D_MODEL), _row_spec(TM, D_ATT), _row_spec(TM, D_SGU), _row_spec(TM, 2 * D_MODEL),
                  _row_spec(TM, D_MODEL), _row_spec(TM, D_MODEL), _row_spec(TM, D_MODEL),
                  _const_spec((1, 2 * D_MODEL)), _const_spec(wba.shape), _const_spec(wbs.shape), _const_spec(wo.shape)],
        out_specs=[_row_spec(TM, D_ATT), _row_spec(TM, D_SGU), _row_spec(TM, 2 * D_MODEL), _acc_spec((1, 2 * D_MODEL)),
                   _acc_spec(wba.shape), _acc_spec(wbs.shape), _acc_spec(wo.shape)],
        out_shape=[jax.ShapeDtypeStruct((T, D_ATT), BF16), jax.ShapeDtypeStruct((T, D_SGU), BF16),
                   jax.ShapeDtypeStruct((T, 2 * D_MODEL), BF16), jax.ShapeDtypeStruct((1, 2 * D_MODEL), F32),
                   jax.ShapeDtypeStruct(wba.shape, F32), jax.ShapeDtypeStruct(wbs.shape, F32),
                   jax.ShapeDtypeStruct(wo.shape, F32)])


def _loss_bwd(x, target, g):
    T = x.shape[0]

    def body(x_ref, t_ref, g_ref, dx_ref, loss_ref, dg_ref):
        @pl.when(pl.program_id(0) == 0)
        def _():
            loss_ref[...] = jnp.zeros_like(loss_ref)
            dg_ref[...] = jnp.zeros_like(dg_ref)

        gv = g_ref[...]
        xhat, r, y = _rms_fwd(x_ref[...], gv)
        err = y - t_ref[...]
        per_tok = jnp.mean(err * err, axis=-1, keepdims=True)
        loss_ref[...] += 0.5 * jnp.sum(per_tok, axis=0, keepdims=True)
        dxn, dg = _rms_bwd(err * (1.0 / D_MODEL), xhat, r, gv)
        dx_ref[...] = dxn
        dg_ref[...] += dg

    return pl.pallas_call(
        body, name="loss_bwd", grid=(T // TM,),
        in_specs=[_row_spec(TM, D_MODEL), _row_spec(TM, D_MODEL), _const_spec((1, D_MODEL))],
        out_specs=[_row_spec(TM, D_MODEL), _acc_spec((1, 128)), _acc_spec((1, D_MODEL))],
        out_shape=[jax.ShapeDtypeStruct((T, D_MODEL), F32), jax.ShapeDtypeStruct((1, 128), F32),
                   jax.ShapeDtypeStruct((1, D_MODEL), F32)],
        compiler_params=_params(("arbitrary",)),
    )(x, target, g)


BIG = ("ffn1_w_gate", "ffn1_w_up", "ffn1_w_down", "w_in", "w_branch_att", "w_branch_sgu", "w_out",
       "ffn2_w_gate", "ffn2_w_up", "ffn2_w_down")
SMALL = ("norm_ffn1", "norm_mix", "b_gate", "rel_bias", "sgu_ln_g", "sgu_ln_b", "sgu_w_s", "sgu_b_s", "norm_ffn2",
         "norm_final")


G_FFN1 = ("ffn1_w_gate", "ffn1_w_up", "ffn1_w_down")
G_MIX = ("w_in", "w_branch_att", "w_branch_sgu", "w_out")
G_FFN2 = ("ffn2_w_gate", "ffn2_w_up", "ffn2_w_down")


def _local_step(x, target, wb, ws, dist=None):
    def gather_on(names):
        return _ag_payload([wb[n] for n in names]) if dist else None

    t2 = _relbias_fwd(ws["rel_bias"])
    bias2 = _bias_blocks(t2)
    bst = ws["sgu_b_s"].T
    w_st = jnp.swapaxes(ws["sgu_w_s"], 1, 2)

    x1, h1, a1, b1, *got = _ffn_fwd(x, ws["norm_ffn1"], wb["ffn1_w_gate"], wb["ffn1_w_up"], wb["ffn1_w_down"],
                                    "ffn1_fwd", gather_on(G_MIX))
    wb.update(zip(G_MIX, got))
    h2, qkv, zs, gl = _in_fwd(x1, ws["norm_mix"], wb["w_in"])
    y_att, *got = _att_fwd(qkv, bias2, gather_on(G_FFN2))
    wb.update(zip(G_FFN2, got))
    y_sgu = _sgu_fwd(zs, ws["sgu_ln_g"], ws["sgu_ln_b"], ws["sgu_w_s"], bst)
    x2, merged, pa, ps = _merge_fwd(x1, y_att, y_sgu, gl, ws["b_gate"], wb["w_branch_att"], wb["w_branch_sgu"],
                                    wb["w_out"])
    x3, h3, a3, b3 = _ffn_fwd(x2, ws["norm_ffn2"], wb["ffn2_w_gate"], wb["ffn2_w_up"], wb["ffn2_w_down"], "ffn2_fwd")
    dx3, loss, g_final = _loss_bwd(x3, target, ws["norm_final"])

    gb, gs, sums = {}, {"norm_final": g_final}, {}

    def pair_on(names, small=None):
        return _px_payload([gb[n] for n in names], small) if dist else None

    def pair_add(names, halves):
        for n, rv in zip(names, halves):
            sums[n] = _pair_add(gb[n], rv, dist[0], dist[1], "pair_add_" + n)

    def chips_on(names):
        return _cx_payload([sums[n][1] for n in names], [sums[n][2] for n in names]) if dist else None

    dx2, da3, db3, gs["norm_ffn2"] = _ffn_dgrad(dx3, x2, a3, b3, ws["norm_ffn2"], wb["ffn2_w_gate"], wb["ffn2_w_up"],
                                                wb["ffn2_w_down"], "ffn2_dgrad")
    gb["ffn2_w_gate"], gb["ffn2_w_up"], gb["ffn2_w_down"] = _ffn_wgrad(h3, dx3, a3, b3, da3, db3, "ffn2_wgrad")
    dy_att, dy_sgu, dgl, gs["b_gate"], gb["w_branch_att"], gb["w_branch_sgu"], gb["w_out"], *got = _merge_bwd(
        dx2, y_att, y_sgu, gl, merged, pa, ps, ws["b_gate"], wb["w_branch_att"], wb["w_branch_sgu"], wb["w_out"],
        pair_on(G_FFN2))
    pair_add(G_FFN2, got)
    dq, dk, dv, db2, *lands2 = _att_bwd(qkv, dy_att, bias2, chips_on(G_FFN2))
    gs["rel_bias"] = _relbias_bwd(_unskew(db2))
    dzs, gs["sgu_w_s"], dbt, gs["sgu_ln_g"], gs["sgu_ln_b"] = _sgu_bwd(zs, dy_sgu, ws["sgu_ln_g"], ws["sgu_ln_b"],
                                                                      ws["sgu_w_s"], w_st, bst)
    gs["sgu_b_s"] = dbt.T
    dx1, dz, gs["norm_mix"] = _in_dgrad(dx2, x1, ws["norm_mix"], wb["w_in"], dq, dk, dv, dzs, dgl)
    gb["w_in"] = _in_wgrad(h2, dz)
    gx, da1, db1, gs["norm_ffn1"], *got = _ffn_dgrad(dx1, x, a1, b1, ws["norm_ffn1"], wb["ffn1_w_gate"],
                                                    wb["ffn1_w_up"], wb["ffn1_w_down"], "ffn1_dgrad", pair_on(G_MIX))
    pair_add(G_MIX, got)
    gb["ffn1_w_gate"], gb["ffn1_w_up"], gb["ffn1_w_down"], *lands_mix = _ffn_wgrad(h1, dx1, a1, b1, da1, db1,
                                                                                   "ffn1_wgrad", chips_on(G_MIX))
    if not dist:
        return loss, gx, gb, gs

    packed = _pack_small(gs, loss)
    *got, packed_sib = _exchange(pair_on(G_FFN1, packed), "pair_exchange_ffn1")
    pair_add(G_FFN1, got)
    small_sums, *lands1 = _small_exchange(packed, packed_sib, chips_on(G_FFN1))
    lands = dict(zip(G_FFN2 + G_MIX + G_FFN1, list(lands2) + list(lands_mix) + list(lands1)))
    fulls = [_final_sum(sums[n][0], lands[n], dist[1], dist[0], "final_sum_" + n) for n in BIG]
    return loss, gx, dict(zip(BIG, _sibling_share(fulls, "sibling_share"))), small_sums


_ANY = pl.BlockSpec(memory_space=pl.ANY)
_VMEM = pl.BlockSpec(memory_space=pltpu.VMEM)


def _mesh_pos():
    return lax.axis_index("x"), lax.axis_index("y"), lax.axis_index("c")


def _cast_slots(shards, chip, name):
    n = len(shards)
    r, ncol = shards[0].shape
    tr = r // 2

    def body(me_ref, *refs):
        for i_ref, o_ref in zip(refs[:n], refs[n:]):
            o_ref[0] = i_ref[...].astype(BF16)

    grid_spec = pltpu.PrefetchScalarGridSpec(
        num_scalar_prefetch=1, grid=(r // tr,),
        in_specs=[pl.BlockSpec((tr, ncol), lambda i, me: (i, 0))] * n,
        out_specs=[pl.BlockSpec((1, tr, ncol), lambda i, me: (me[0], i, 0))] * n)
    return pl.pallas_call(
        body, name=name, grid_spec=grid_spec,
        out_shape=[jax.ShapeDtypeStruct((N_SHARD, r, ncol), BF16)] * n,
        compiler_params=_params(("arbitrary",)),
    )(chip, *shards)


class _Payload:
    def __init__(self, arrays, out_shapes, aliases, scratch, phases):
        self.arrays = list(arrays)
        self.out_shapes = list(out_shapes)
        self.aliases = dict(aliases)
        self.scratch = list(scratch)
        self.phases = phases


def _remote(src, dst, ssem, rsem, dev):
    return pltpu.make_async_remote_copy(src_ref=src, dst_ref=dst, send_sem=ssem, recv_sem=rsem, device_id=dev,
                                        device_id_type=MESH)


def _call(body, payload, *, name, grid, in_specs, out_specs, out_shape, scratch_shapes=(), sem=None, when=None,
          operands=()):
    in_specs, out_specs, out_shape = list(in_specs), list(out_specs), list(out_shape)
    scratch_shapes = list(scratch_shapes)
    n_in, n_out, n_scr = len(in_specs), len(out_specs), len(scratch_shapes)
    kwargs = {}
    kernel = body
    if payload is not None:
        k_in, k_out = len(payload.arrays), len(payload.out_shapes)
        rank = len(grid) if grid else 0

        def kernel(*refs):
            a, b = n_in, n_in + k_in
            c, d = b + n_out, b + n_out + k_out
            e = d + n_scr
            phases = payload.phases(refs[a:b], refs[c:d], refs[e:])

            def run():
                body(*refs[:a], *refs[b:c], *refs[d:e])

            if not grid:
                phases[0]()
                run()
                for ph in phases[1:]:
                    ph()
                return
            step = pl.program_id(0)
            if rank == 2:
                step = step * grid[1] + pl.program_id(1)
            for ph, (at, before) in zip(phases, when):
                if before:
                    pl.when(step == at)(ph)
            run()
            for ph, (at, before) in zip(phases, when):
                if not before:
                    pl.when(step == at)(ph)

        in_specs += [_ANY] * k_in
        out_specs += [_ANY] * k_out
        out_shape += payload.out_shapes
        scratch_shapes += payload.scratch
        kwargs["input_output_aliases"] = {n_in + i: n_out + j for i, j in payload.aliases.items()}
        operands = tuple(operands) + tuple(payload.arrays)
    if grid:
        kwargs["grid"] = grid
    return pl.pallas_call(kernel, name=name, in_specs=in_specs, out_specs=out_specs, out_shape=out_shape,
                          scratch_shapes=scratch_shapes, compiler_params=_params(sem), **kwargs)(*operands)


def _exchange(payload, name):
    return _call(lambda: None, payload, name=name, grid=None, in_specs=[], out_specs=[], out_shape=[])


def _ag_payload(slots):
    n = len(slots)

    def phases(_, refs, sems):
        send_i, recv_i, send_d, recv_d = sems
        x, y, c = _mesh_pos()
        me = 2 * x + y

        def half(w, core):
            rh = slots[w].shape[1] // 2
            return pl.ds(core * rh, rh)

        def ici(w, j):
            t = (me + 1 + j) % N_SHARD
            mine = refs[w].at[me, half(w, c), :]
            return _remote(mine, mine, send_i.at[3 * w + j], recv_i.at[3 * w + j], (t // 2, t % 2, c))

        def d2d(w, j, core):
            s = (me + 3 - j) % N_SHARD
            land = refs[w].at[s, half(w, core), :]
            return _remote(land, land, send_d.at[3 * w + j], recv_d.at[3 * w + j], (x, y, 1 - c))

        def start():
            for w in range(n):
                for j in range(3):
                    ici(w, j).start()

        def finish():
            for w in range(n):
                for j in range(3):
                    s = (me + 3 - j) % N_SHARD
                    land = refs[w].at[s, half(w, c), :]
                    _remote(land, land, send_i.at[3 * w + j], recv_i.at[3 * w + j], (x, y, c)).wait_recv()
                    d2d(w, j, c).start()
            for w in range(n):
                for j in range(3):
                    d2d(w, j, 1 - c).wait_recv()
            for w in range(n):
                for j in range(3):
                    ici(w, j).wait_send()
                    d2d(w, j, c).wait_send()

        return [start, finish]

    return _Payload(slots, [jax.ShapeDtypeStruct(s.shape, s.dtype) for s in slots], {i: i for i in range(n)},
                    [pltpu.SemaphoreType.DMA((3 * n,)) for _ in range(4)], phases)


def _px_payload(grads, small=None):
    arrays = list(grads) + ([small] if small is not None else [])
    n = len(arrays)

    def phases(ins, outs, sems):
        send, recv = sems
        x, y, c = _mesh_pos()

        def copy(w):
            if w < len(grads):
                rh = grads[w].shape[1] // 2
                src = ins[w].at[:, pl.ds((1 - c) * rh, rh), :]
            else:
                src = ins[w]
            return _remote(src, outs[w], send.at[w], recv.at[w], (x, y, 1 - c))

        def start():
            for w in range(n):
                copy(w).start()

        def finish():
            for w in range(n):
                copy(w).wait()

        return [start, finish]

    out_shapes = [jax.ShapeDtypeStruct((N_SHARD, g.shape[1] // 2, g.shape[2]), F32) for g in grads]
    if small is not None:
        out_shapes.append(jax.ShapeDtypeStruct(small.shape, F32))
    return _Payload(arrays, out_shapes, {}, [pltpu.SemaphoreType.DMA((n,)), pltpu.SemaphoreType.DMA((n,))], phases)


def _cx_payload(pbs, lands):
    n = len(pbs)

    def phases(ins, outs, sems):
        send, recv = sems
        x, y, c = _mesh_pos()
        me = 2 * x + y

        def copy(w, j):
            t = (me + 1 + j) % N_SHARD
            return _remote(ins[w].at[t], outs[w].at[me], send.at[3 * w + j], recv.at[3 * w + j], (t // 2, t % 2, c))

        def start():
            for w in range(n):
                for j in range(3):
                    copy(w, j).start()

        def finish():
            for w in range(n):
                for j in range(3):
                    copy(w, j).wait()

        return [start, finish]

    return _Payload(list(pbs) + list(lands), [jax.ShapeDtypeStruct(p.shape, BF16) for p in lands],
                    {n + i: i for i in range(n)},
                    [pltpu.SemaphoreType.DMA((3 * n,)), pltpu.SemaphoreType.DMA((3 * n,))], phases)


def _pair_add(g, rv, core, chip, name):
    _, r, ncol = g.shape
    rh = r // 2

    def body(c_ref, me_ref, g_ref, rv_ref, pf_ref, pb_ref, land_ref):
        s = g_ref[0] + rv_ref[0]
        sb = s.astype(BF16)
        pb_ref[0] = sb
        land_ref[0] = sb

        @pl.when(pl.program_id(0) == me_ref[0])
        def _():
            pf_ref[...] = s

    slot = pl.BlockSpec((1, rh, ncol), lambda s, c, me: (s, 0, 0))
    grid_spec = pltpu.PrefetchScalarGridSpec(
        num_scalar_prefetch=2, grid=(N_SHARD,),
        in_specs=[pl.BlockSpec((1, rh, ncol), lambda s, c, me: (s, c[0], 0)), slot],
        out_specs=[pl.BlockSpec((rh, ncol), lambda s, c, me: (0, 0)), slot, slot])
    return pl.pallas_call(
        body, name=name, grid_spec=grid_spec,
        out_shape=[jax.ShapeDtypeStruct((rh, ncol), F32), jax.ShapeDtypeStruct((N_SHARD, rh, ncol), BF16),
                   jax.ShapeDtypeStruct((N_SHARD, rh, ncol), BF16)],
        compiler_params=_params(("arbitrary",)),
    )(core, chip, g, rv)


def _small_exchange(small, small_sib, payload=None):
    def body(sm, sm_sib, sm_out, sm_sum, send, recv, loc):
        x, y, c = _mesh_pos()
        me = 2 * x + y
        sm_sum[...] = sm[...] + sm_sib[...]
        local = pltpu.make_async_copy(sm_sum, sm_out.at[me], loc)
        local.start()
        cps = []
        for j in range(3):
            t = (me + 1 + j) % N_SHARD
            cp = pltpu.make_async_remote_copy(src_ref=sm_sum, dst_ref=sm_out.at[me], send_sem=send.at[j],
                                              recv_sem=recv.at[j], device_id=(t // 2, t % 2, c), device_id_type=MESH)
            cp.start()
            cps.append(cp)
        for cp in cps:
            cp.wait()
        local.wait()

    return _call(
        body, payload, name="small_exchange", grid=None,
        in_specs=[_VMEM, _VMEM], out_specs=[_ANY],
        out_shape=[jax.ShapeDtypeStruct((N_SHARD,) + small.shape, F32)],
        scratch_shapes=[pltpu.VMEM(small.shape, F32), pltpu.SemaphoreType.DMA((3,)), pltpu.SemaphoreType.DMA((3,)),
                        pltpu.SemaphoreType.DMA],
        operands=(small, small_sib))


def _final_sum(pf, land, chip, core, name):
    _, rh, ncol = land.shape

    def body(me_ref, c_ref, pf_ref, land_ref, o_ref):
        me = me_ref[0]
        acc = jnp.zeros((rh, ncol), F32)
        for k in range(N_SHARD):
            acc = acc + jnp.where(me == k, pf_ref[...], land_ref[k].astype(F32))
        o_ref[...] = acc

    grid_spec = pltpu.PrefetchScalarGridSpec(
        num_scalar_prefetch=2, grid=(1,),
        in_specs=[pl.BlockSpec((rh, ncol), lambda i, me, c: (0, 0)),
                  pl.BlockSpec((N_SHARD, rh, ncol), lambda i, me, c: (0, 0, 0))],
        out_specs=pl.BlockSpec((rh, ncol), lambda i, me, c: (c[0], 0)))
    return pl.pallas_call(
        body, name=name, grid_spec=grid_spec, out_shape=jax.ShapeDtypeStruct((2 * rh, ncol), F32),
        compiler_params=_params(("arbitrary",)),
    )(chip, core, pf, land)


def _sibling_share(fulls, name):
    n = len(fulls)

    def body(*refs):
        outs = refs[n:2 * n]
        send, recv = refs[2 * n:]
        x, y, c = _mesh_pos()
        cps = []
        for w in range(n):
            rh = fulls[w].shape[0] // 2
            mine = outs[w].at[pl.ds(c * rh, rh), :]
            cp = pltpu.make_async_remote_copy(src_ref=mine, dst_ref=mine, send_sem=send.at[w], recv_sem=recv.at[w],
                                              device_id=(x, y, 1 - c), device_id_type=MESH)
            cp.start()
            cps.append(cp)
        for cp in cps:
            cp.wait()

    return pl.pallas_call(
        body, name=name,
        in_specs=[_ANY] * n, out_specs=[_ANY] * n,
        out_shape=[jax.ShapeDtypeStruct(f.shape, F32) for f in fulls],
        input_output_aliases={i: i for i in range(n)},
        scratch_shapes=[pltpu.SemaphoreType.DMA((n,)), pltpu.SemaphoreType.DMA((n,))],
    )(*fulls)


_ROW = {"rel_bias": 128, "sgu_b_s": 136, "norm_ffn1": 144, "norm_mix": 145, "norm_ffn2": 146, "norm_final": 147,
        "b_gate": 148, "sgu_ln_g": 150, "sgu_ln_b": 151}


def _pack_small(gs, loss):
    def body(ws, rel, bs, n1, nm, n2, nf, bg, lg, lb, loss_ref, o_ref):
        o_ref[...] = jnp.zeros_like(o_ref)
        o_ref[LOSS_ROW:LOSS_ROW + 1, 0:128] = loss_ref[...]
        for g in range(SGU_GROUPS):
            o_ref[0:SGU_BLOCK, g * SGU_BLOCK:(g + 1) * SGU_BLOCK] = ws[g]
        o_ref[128:136, 0:REL_PAD] = rel[...]
        o_ref[136:144, 0:SGU_BLOCK] = bs[...]
        o_ref[144:145, :] = n1[...]
        o_ref[145:146, :] = nm[...]
        o_ref[146:147, :] = n2[...]
        o_ref[147:148, :] = nf[...]
        o_ref[148:149, :] = bg[:, 0:D_MODEL]
        o_ref[149:150, :] = bg[:, D_MODEL:2 * D_MODEL]
        o_ref[150:151, 0:D_SGU] = lg[...]
        o_ref[151:152, 0:D_SGU] = lb[...]

    order = ("sgu_w_s", "rel_bias", "sgu_b_s", "norm_ffn1", "norm_mix", "norm_ffn2", "norm_final", "b_gate", "sgu_ln_g",
             "sgu_ln_b")
    return pl.pallas_call(body, name="pack_small", out_shape=jax.ShapeDtypeStruct((SMALL_ROWS, D_MODEL), F32))(
        *[gs[k] for k in order], loss)


def _adam(w, g, m, v):
    m2 = ADAM_B1 * m + (1.0 - ADAM_B1) * g
    v2 = ADAM_B2 * v + (1.0 - ADAM_B2) * (g * g)
    m_hat = m2 / (1.0 - ADAM_B1 ** ADAM_STEP)
    v_hat = v2 / (1.0 - ADAM_B2 ** ADAM_STEP)
    delta = -ADAM_LR * (m_hat / (jnp.sqrt(v_hat) + ADAM_EPS) + ADAM_WD * w)
    return delta, m2, v2


def _adam_small(sin, w, m, v):
    names = SMALL
    k = len(names)

    def body(*refs):
        sin_ref = refs[0]
        w_r, m_r, v_r = refs[1:1 + k], refs[1 + k:1 + 2 * k], refs[1 + 2 * k:1 + 3 * k]
        outs = refs[1 + 3 * k:]
        tot = sin_ref[0] + sin_ref[1] + sin_ref[2] + sin_ref[3]
        outs[4 * k][...] = tot[LOSS_ROW:LOSS_ROW + 1, 0:128]
        for i, name in enumerate(names):
            o = outs[4 * i:4 * i + 4]
            if name == "sgu_w_s":
                for gi in range(SGU_GROUPS):
                    g = tot[0:SGU_BLOCK, gi * SGU_BLOCK:(gi + 1) * SGU_BLOCK]
                    res = (g,) + _adam(w_r[i][gi], g, m_r[i][gi], v_r[i][gi])
                    for ref, val in zip(o, res):
                        ref[gi] = val
                continue
            r0 = _ROW[name]
            if name == "rel_bias":
                g = tot[r0:r0 + HEADS, 0:REL_PAD]
            elif name == "sgu_b_s":
                g = tot[r0:r0 + SGU_GROUPS, 0:SGU_BLOCK]
            elif name == "b_gate":
                g = jnp.concatenate([tot[r0:r0 + 1, :], tot[r0 + 1:r0 + 2, :]], axis=1)
            elif name in ("sgu_ln_g", "sgu_ln_b"):
                g = tot[r0:r0 + 1, 0:D_SGU]
            else:
                g = tot[r0:r0 + 1, :]
            res = (g,) + _adam(w_r[i][...], g, m_r[i][...], v_r[i][...])
            for ref, val in zip(o, res):
                ref[...] = val

    out_shape = []
    for name in names:
        out_shape += [jax.ShapeDtypeStruct(w[name].shape, F32)] * 4
    out_shape.append(jax.ShapeDtypeStruct((1, 128), F32))
    flat = pl.pallas_call(body, name="adam_small", out_shape=out_shape, compiler_params=_params())(
        sin, *[w[n] for n in names], *[m[n] for n in names], *[v[n] for n in names])
    return {name: tuple(flat[4 * i:4 * i + 4]) for i, name in enumerate(names)}, flat[4 * k]


def _adam_big(w, g, m, v, name):
    r, ncol = w.shape
    tr = 256 if r % 256 == 0 else r // 2

    def body(w_ref, g_ref, m_ref, v_ref, d_ref, m2_ref, v2_ref):
        d_ref[...], m2_ref[...], v2_ref[...] = _adam(w_ref[...], g_ref[...], m_ref[...], v_ref[...])

    spec = pl.BlockSpec((tr, ncol), lambda i: (i, 0))
    return pl.pallas_call(
        body, name=name, grid=(r // tr,), in_specs=[spec] * 4, out_specs=[spec] * 3,
        out_shape=[jax.ShapeDtypeStruct(w.shape, F32)] * 3, compiler_params=_params(("arbitrary",)),
    )(w, g, m, v)


WEIGHTS = ("norm_ffn1", "ffn1_w_gate", "ffn1_w_up", "ffn1_w_down", "norm_mix", "w_in", "b_gate", "rel_bias", "sgu_ln_g",
           "sgu_ln_b", "sgu_w_s", "sgu_b_s", "w_branch_att", "w_branch_sgu", "w_out", "norm_ffn2", "ffn2_w_gate",
           "ffn2_w_up", "ffn2_w_down", "norm_final")


GATE_UP = ("ffn1_w_gate", "ffn1_w_up", "ffn2_w_gate", "ffn2_w_up")
_FFN = ("ffn1_w_gate", "ffn1_w_up", "ffn1_w_down", "ffn2_w_gate", "ffn2_w_up", "ffn2_w_down")
_CAST_GROUPS = ((_FFN, "cast_ffn"), (("w_in",), "cast_w_in"), (("w_branch_att", "w_branch_sgu"), "cast_branch"),
                (("w_out",), "cast_w_out"))


def _big_form(name, a):
    return jnp.swapaxes(a, 1, 2)[0] if name in GATE_UP else a[0]


def _big_back(name, a):
    return jnp.swapaxes(a[None], 1, 2) if name in GATE_UP else a[None]


def _small_form(name, a):
    if name == "norm_final":
        return a.reshape(1, D_MODEL)
    if name == "rel_bias":
        return jnp.pad(a[0], ((0, 0), (0, REL_PAD - N_REL)))
    if name in ("sgu_w_s", "sgu_b_s"):
        return a[0]
    return a


def _small_back(name, a, like):
    if name == "rel_bias":
        a = a[:, :N_REL]
    return a.reshape(like.shape)


def kernel(x, norm_ffn1, ffn1_w_gate, ffn1_w_up, ffn1_w_down, norm_mix, w_in, b_gate, rel_bias, sgu_ln_g, sgu_ln_b, sgu_w_s, sgu_b_s, w_branch_att, w_branch_sgu, w_out, norm_ffn2, ffn2_w_gate, ffn2_w_up, ffn2_w_down, norm_final, loss_target, m_norm_ffn1, m_ffn1_w_gate, m_ffn1_w_up, m_ffn1_w_down, m_norm_mix, m_w_in, m_b_gate, m_rel_bias, m_sgu_ln_g, m_sgu_ln_b, m_sgu_w_s, m_sgu_b_s, m_w_branch_att, m_w_branch_sgu, m_w_out, m_norm_ffn2, m_ffn2_w_gate, m_ffn2_w_up, m_ffn2_w_down, m_norm_final, v_norm_ffn1, v_ffn1_w_gate, v_ffn1_w_up, v_ffn1_w_down, v_norm_mix, v_w_in, v_b_gate, v_rel_bias, v_sgu_ln_g, v_sgu_ln_b, v_sgu_w_s, v_sgu_b_s, v_w_branch_att, v_w_branch_sgu, v_w_out, v_norm_ffn2, v_ffn2_w_gate, v_ffn2_w_up, v_ffn2_w_down, v_norm_final):
    w = dict(norm_ffn1=norm_ffn1, ffn1_w_gate=ffn1_w_gate, ffn1_w_up=ffn1_w_up, ffn1_w_down=ffn1_w_down, norm_mix=norm_mix,
             w_in=w_in, b_gate=b_gate, rel_bias=rel_bias, sgu_ln_g=sgu_ln_g, sgu_ln_b=sgu_ln_b, sgu_w_s=sgu_w_s,
             sgu_b_s=sgu_b_s, w_branch_att=w_branch_att, w_branch_sgu=w_branch_sgu, w_out=w_out, norm_ffn2=norm_ffn2,
             ffn2_w_gate=ffn2_w_gate, ffn2_w_up=ffn2_w_up, ffn2_w_down=ffn2_w_down, norm_final=norm_final)
    m = dict(norm_ffn1=m_norm_ffn1, ffn1_w_gate=m_ffn1_w_gate, ffn1_w_up=m_ffn1_w_up, ffn1_w_down=m_ffn1_w_down,
             norm_mix=m_norm_mix, w_in=m_w_in, b_gate=m_b_gate, rel_bias=m_rel_bias, sgu_ln_g=m_sgu_ln_g,
             sgu_ln_b=m_sgu_ln_b, sgu_w_s=m_sgu_w_s, sgu_b_s=m_sgu_b_s, w_branch_att=m_w_branch_att,
             w_branch_sgu=m_w_branch_sgu, w_out=m_w_out, norm_ffn2=m_norm_ffn2, ffn2_w_gate=m_ffn2_w_gate,
             ffn2_w_up=m_ffn2_w_up, ffn2_w_down=m_ffn2_w_down, norm_final=m_norm_final)
    v = dict(norm_ffn1=v_norm_ffn1, ffn1_w_gate=v_ffn1_w_gate, ffn1_w_up=v_ffn1_w_up, ffn1_w_down=v_ffn1_w_down,
             norm_mix=v_norm_mix, w_in=v_w_in, b_gate=v_b_gate, rel_bias=v_rel_bias, sgu_ln_g=v_sgu_ln_g,
             sgu_ln_b=v_sgu_ln_b, sgu_w_s=v_sgu_w_s, sgu_b_s=v_sgu_b_s, w_branch_att=v_w_branch_att,
             w_branch_sgu=v_w_branch_sgu, w_out=v_w_out, norm_ffn2=v_norm_ffn2, ffn2_w_gate=v_ffn2_w_gate,
             ffn2_w_up=v_ffn2_w_up, ffn2_w_down=v_ffn2_w_down, norm_final=v_norm_final)

    core = lax.axis_index("c").astype(jnp.int32).reshape(1)
    chip = (2 * lax.axis_index("x") + lax.axis_index("y")).astype(jnp.int32).reshape(1)

    wk = {n: _big_form(n, w[n]) for n in BIG}
    slots = {}
    for names, call in _CAST_GROUPS:
        slots.update(zip(names, _cast_slots([wk[n] for n in names], chip, call)))
    slots.update(zip(G_FFN1, _exchange(_ag_payload([slots[n] for n in G_FFN1]), "allgather_ffn1")))
    ws = {n: _small_form(n, w[n]) for n in SMALL}
    _, gx, shard_grads, small_sums = _local_step(x[0], loss_target[0], slots, ws, (core, chip))

    small, loss = _adam_small(small_sums, ws, {n: _small_form(n, m[n]) for n in SMALL},
                              {n: _small_form(n, v[n]) for n in SMALL})
    grad, delta, new_m, new_v = {}, {}, {}, {}
    for n in SMALL:
        grad[n], delta[n], new_m[n], new_v[n] = (_small_back(n, a, w[n]) for a in small[n])
    for n in BIG:
        g2 = shard_grads[n]
        d2, m2, v2 = _adam_big(wk[n], g2, _big_form(n, m[n]), _big_form(n, v[n]), "adam_" + n)
        grad[n], delta[n], new_m[n], new_v[n] = (_big_back(n, a) for a in (g2, d2, m2, v2))

    return (loss[0, 0], gx.reshape(x.shape), *[grad[n] for n in WEIGHTS], *[delta[n] for n in WEIGHTS],
            *[new_m[n] for n in WEIGHTS], *[new_v[n] for n in WEIGHTS])
```

```python
import functools

import jax
import jax.numpy as jnp
from jax import lax
from jax.experimental import pallas as pl
from jax.experimental.pallas import tpu as pltpu

F32 = jnp.float32
BF16 = jnp.bfloat16

D_MODEL = 1024
N_SHARD = 4
D_FF = 2816
FF_S = D_FF // N_SHARD
D_ATT = 512
D_SGU = 512
D_IN = 3 * D_ATT + 2 * D_SGU + 2 * D_MODEL
IN_S = D_IN // N_SHARD
BR_S = D_MODEL // N_SHARD
HEADS = 8
HEAD_DIM = 64
CHUNK = 64
N_LEFT = 8
BAND = (N_LEFT + 1) * CHUNK
REL_CLIP = 256
N_REL = 2 * REL_CLIP + 1
REL_PAD = 640
SGU_BLOCK = 128
SGU_GROUPS = 8
SGU_GDIM = 64
EPS = 1e-6
NEG_INF = -1e30

ATT_ROWS = 2 * CHUNK
ATT_KEYS = BAND + CHUNK
ATT_PAD = N_LEFT * CHUNK

ADAM_LR = 0.001
ADAM_B1 = 0.9
ADAM_B2 = 0.999
ADAM_EPS = 1e-08
ADAM_WD = 0.01
ADAM_STEP = 10

TM = 256
TW = 512
VMEM_LIMIT = 56 * 1024 * 1024

SMALL_ROWS = 160
LOSS_ROW = 152
MESH = pl.DeviceIdType.MESH

_NT = (((1,), (1,)), ((), ()))
_TN = (((0,), (0,)), ((), ()))


def _params(sem=None):
    return pltpu.CompilerParams(dimension_semantics=sem, vmem_limit_bytes=VMEM_LIMIT)


def _const_spec(shape):
    nd = len(shape)
    return pl.BlockSpec(shape, lambda *_: (0,) * nd, pipeline_mode=pl.Buffered(1))


def _acc_spec(shape):
    nd = len(shape)
    return pl.BlockSpec(shape, lambda *_: (0,) * nd)


def _row_spec(tm, ncols, off=0):
    return pl.BlockSpec((tm, ncols), lambda i: (i + off, 0))


def _row3_spec(tm, ncols):
    return pl.BlockSpec((N_SHARD, tm, ncols), lambda i: (0, i, 0))


def _dot(a, b):
    return jnp.dot(a, b, preferred_element_type=F32)


def _dot_nt(a, b):
    return lax.dot_general(a, b, _NT, preferred_element_type=F32)


def _dot_tn(a, b):
    return lax.dot_general(a, b, _TN, preferred_element_type=F32)


def _rms_fwd(x, g):
    r = lax.rsqrt(jnp.mean(x * x, axis=-1, keepdims=True) + EPS)
    xhat = x * r
    return xhat, r, xhat * g


def _rms_bwd(dh, xhat, r, g):
    dxhat = dh * g
    dx = r * (dxhat - xhat * jnp.mean(dxhat * xhat, axis=-1, keepdims=True))
    dg = jnp.sum(dh * xhat, axis=0, keepdims=True)
    return dx, dg


def _sigmoid(x):
    return 1.0 / (1.0 + jnp.exp(-x))


def _edges(n_steps):
    return [(0, True), (n_steps - 1, False)]


def _ffn_fwd(x, g, wg, wu, wd, name, payload=None):
    T = x.shape[0]

    def body(x_ref, g_ref, wg_ref, wu_ref, wd_ref, xo_ref, h_ref, a_ref, b_ref):
        xv = x_ref[...]
        hb = _rms_fwd(xv, g_ref[...])[2].astype(BF16)
        h_ref[...] = hb
        acc = jnp.zeros((TM, D_MODEL), F32)
        for s in range(N_SHARD):
            a = _dot_nt(hb, wg_ref[s])
            b = _dot_nt(hb, wu_ref[s])
            a_ref[s] = a.astype(BF16)
            b_ref[s] = b.astype(BF16)
            sv = a * _sigmoid(a) * b
            acc += _dot(sv.astype(BF16), wd_ref[s])
        xo_ref[...] = xv + 0.5 * acc

    return _call(
        body, payload, name=name, grid=(T // TM,), when=_edges(T // TM), sem=("arbitrary",),
        in_specs=[_row_spec(TM, D_MODEL), _const_spec((1, D_MODEL)), _const_spec(wg.shape), _const_spec(wu.shape),
                  _const_spec(wd.shape)],
        out_specs=[_row_spec(TM, D_MODEL), _row_spec(TM, D_MODEL), _row3_spec(TM, FF_S), _row3_spec(TM, FF_S)],
        out_shape=[jax.ShapeDtypeStruct((T, D_MODEL), F32), jax.ShapeDtypeStruct((T, D_MODEL), BF16),
                   jax.ShapeDtypeStruct((N_SHARD, T, FF_S), BF16), jax.ShapeDtypeStruct((N_SHARD, T, FF_S), BF16)],
        operands=(x, g, wg, wu, wd))


def _ffn_dgrad(dout, x, a, b, g, wg, wu, wd, name, payload=None):
    T = x.shape[0]

    def body(do_ref, x_ref, a_ref, b_ref, g_ref, wg_ref, wu_ref, wd_ref, dx_ref, da_ref, db_ref, dg_ref):
        do = do_ref[...]
        dob = do.astype(BF16)
        dh = jnp.zeros((TM, D_MODEL), F32)
        for s in range(N_SHARD):
            ds = 0.5 * _dot_nt(dob, wd_ref[s])
            av = a_ref[s].astype(F32)
            bv = b_ref[s].astype(F32)
            sig = _sigmoid(av)
            da = (ds * bv * (sig * (1.0 + av * (1.0 - sig)))).astype(BF16)
            db = (ds * (av * sig)).astype(BF16)
            da_ref[s] = da
            db_ref[s] = db
            dh += _dot(da, wg_ref[s]) + _dot(db, wu_ref[s])
        gv = g_ref[...]
        xhat, r, _ = _rms_fwd(x_ref[...], gv)
        dxn, dg = _rms_bwd(dh, xhat, r, gv)
        dx_ref[...] = do + dxn

        @pl.when(pl.program_id(0) == 0)
        def _():
            dg_ref[...] = jnp.zeros_like(dg_ref)

        dg_ref[...] += dg

    return _call(
        body, payload, name=name, grid=(T // TM,), when=_edges(T // TM), sem=("arbitrary",),
        in_specs=[_row_spec(TM, D_MODEL), _row_spec(TM, D_MODEL), _row3_spec(TM, FF_S), _row3_spec(TM, FF_S),
                  _const_spec((1, D_MODEL)), _const_spec(wg.shape), _const_spec(wu.shape), _const_spec(wd.shape)],
        out_specs=[_row_spec(TM, D_MODEL), _row3_spec(TM, FF_S), _row3_spec(TM, FF_S), _acc_spec((1, D_MODEL))],
        out_shape=[jax.ShapeDtypeStruct((T, D_MODEL), F32), jax.ShapeDtypeStruct((N_SHARD, T, FF_S), BF16),
                   jax.ShapeDtypeStruct((N_SHARD, T, FF_S), BF16), jax.ShapeDtypeStruct((1, D_MODEL), F32)],
        operands=(dout, x, a, b, g, wg, wu, wd))


def _ffn_wgrad(h, dout, a, b, da, db, name, payload=None):
    T = h.shape[0]

    def body(h_ref, do_ref, a_ref, b_ref, da_ref, db_ref, gwg_ref, gwu_ref, gwd_ref):
        @pl.when(pl.program_id(1) == 0)
        def _():
            gwg_ref[...] = jnp.zeros_like(gwg_ref)
            gwu_ref[...] = jnp.zeros_like(gwu_ref)
            gwd_ref[...] = jnp.zeros_like(gwd_ref)

        hv = h_ref[...]
        dob = do_ref[...].astype(BF16)
        av = a_ref[0].astype(F32)
        sv = (0.5 * av * _sigmoid(av) * b_ref[0].astype(F32)).astype(BF16)
        gwg_ref[0] += _dot_tn(da_ref[0], hv)
        gwu_ref[0] += _dot_tn(db_ref[0], hv)
        gwd_ref[0] += _dot_tn(sv, dob)

    tok = pl.BlockSpec((TW, D_MODEL), lambda s, i: (i, 0))
    act = pl.BlockSpec((1, TW, FF_S), lambda s, i: (s, i, 0))
    return _call(
        body, payload, name=name, grid=(N_SHARD, T // TW), when=_edges(N_SHARD * (T // TW)),
        sem=("arbitrary", "arbitrary"),
        in_specs=[tok, tok, act, act, act, act],
        out_specs=[pl.BlockSpec((1, FF_S, D_MODEL), lambda s, i: (s, 0, 0))] * 3,
        out_shape=[jax.ShapeDtypeStruct((N_SHARD, FF_S, D_MODEL), F32)] * 3,
        operands=(h, dout, a, b, da, db))


def _in_fwd(x, g, w_in, payload=None):
    T = x.shape[0]

    def body(x_ref, g_ref, w_ref, h_ref, qkv_ref, zs_ref, gl_ref):
        hb = _rms_fwd(x_ref[...], g_ref[...])[2].astype(BF16)
        h_ref[...] = hb
        z0 = _dot(hb, w_ref[0])
        qkv_ref[:, 0:IN_S] = z0.astype(BF16)
        z1 = _dot(hb, w_ref[1])
        qkv_ref[:, IN_S:3 * D_ATT] = z1[:, 0:384].astype(BF16)
        zs_ref[:, 0:768] = z1[:, 384:IN_S]
        z2 = _dot(hb, w_ref[2])
        zs_ref[:, 768:1024] = z2[:, 0:256]
        gl_ref[:, 0:896] = z2[:, 256:IN_S]
        gl_ref[:, 896:2048] = _dot(hb, w_ref[3])

    return _call(
        body, payload, name="in_fwd", grid=(T // TM,), when=_edges(T // TM), sem=("arbitrary",),
        in_specs=[_row_spec(TM, D_MODEL), _const_spec((1, D_MODEL)), _const_spec(w_in.shape)],
        out_specs=[_row_spec(TM, D_MODEL), _row_spec(TM, 3 * D_ATT), _row_spec(TM, 2 * D_SGU), _row_spec(TM, 2 * D_MODEL)],
        out_shape=[jax.ShapeDtypeStruct((T, D_MODEL), BF16), jax.ShapeDtypeStruct((T, 3 * D_ATT), BF16),
                   jax.ShapeDtypeStruct((T, 2 * D_SGU), F32), jax.ShapeDtypeStruct((T, 2 * D_MODEL), F32)],
        operands=(x, g, w_in))


def _in_dgrad(dx_res, x, g, w_in, dq, dk, dv, dzs, dgl):
    T = x.shape[0]

    def body(dxr_ref, x_ref, g_ref, w_ref, dq_ref, dk_ref, dv_ref, dzs_ref, dgl_ref, dx_ref, dz_ref, dg_ref):
        dz = jnp.concatenate([dq_ref[...], dk_ref[...].astype(BF16), dv_ref[...].astype(BF16), dzs_ref[...], dgl_ref[...]],
                             axis=1)
        dz_ref[...] = dz
        dh = jnp.zeros((TM, D_MODEL), F32)
        for s in range(N_SHARD):
            dh += _dot_nt(dz[:, s * IN_S:(s + 1) * IN_S], w_ref[s])
        gv = g_ref[...]
        xhat, r, _ = _rms_fwd(x_ref[...], gv)
        dxn, dg = _rms_bwd(dh, xhat, r, gv)
        dx_ref[...] = dxr_ref[...] + dxn

        @pl.when(pl.program_id(0) == 0)
        def _():
            dg_ref[...] = jnp.zeros_like(dg_ref)

        dg_ref[...] += dg

    pad_blocks = ATT_PAD // TM
    return pl.pallas_call(
        body, name="in_dgrad", grid=(T // TM,),
        in_specs=[_row_spec(TM, D_MODEL), _row_spec(TM, D_MODEL), _const_spec((1, D_MODEL)), _const_spec(w_in.shape),
                  _row_spec(TM, D_ATT), _row_spec(TM, D_ATT, pad_blocks), _row_spec(TM, D_ATT, pad_blocks),
                  _row_spec(TM, 2 * D_SGU), _row_spec(TM, 2 * D_MODEL)],
        out_specs=[_row_spec(TM, D_MODEL), _row_spec(TM, D_IN), _acc_spec((1, D_MODEL))],
        out_shape=[jax.ShapeDtypeStruct((T, D_MODEL), F32), jax.ShapeDtypeStruct((T, D_IN), BF16),
                   jax.ShapeDtypeStruct((1, D_MODEL), F32)],
        compiler_params=_params(("arbitrary",)),
    )(dx_res, x, g, w_in, dq, dk, dv, dzs, dgl)


def _in_wgrad(h, dz):
    T = h.shape[0]

    def body(h_ref, dz_ref, gw_ref):
        @pl.when(pl.program_id(1) == 0)
        def _():
            gw_ref[...] = jnp.zeros_like(gw_ref)

        gw_ref[0] += _dot_tn(h_ref[...], dz_ref[...])

    return pl.pallas_call(
        body, name="in_wgrad", grid=(N_SHARD, T // TW),
        in_specs=[pl.BlockSpec((TW, D_MODEL), lambda s, i: (i, 0)), pl.BlockSpec((TW, IN_S), lambda s, i: (i, s))],
        out_specs=pl.BlockSpec((1, D_MODEL, IN_S), lambda s, i: (s, 0, 0)),
        out_shape=jax.ShapeDtypeStruct((N_SHARD, D_MODEL, IN_S), F32),
        compiler_params=_params(("arbitrary", "arbitrary")),
    )(h, dz)


def _rel_onehot():
    r = lax.broadcasted_iota(jnp.int32, (REL_PAD, REL_PAD), 0)
    n = lax.broadcasted_iota(jnp.int32, (REL_PAD, REL_PAD), 1)
    idx = jnp.clip(BAND - 1 - n, -REL_CLIP, REL_CLIP) + REL_CLIP
    return jnp.where(r == idx, 1.0, 0.0).astype(BF16)


def _split3(v):
    p1 = v.astype(BF16)
    r1 = v - p1.astype(F32)
    p2 = r1.astype(BF16)
    p3 = (r1 - p2.astype(F32)).astype(BF16)
    return p1, p2, p3


def _relbias_fwd(tab_pad):
    def body(t_ref, o_ref):
        oh = _rel_onehot()
        acc = jnp.zeros((HEADS, REL_PAD), F32)
        for p in _split3(t_ref[...]):
            acc += _dot(p, oh)
        o_ref[...] = acc

    return pl.pallas_call(body, name="relbias_fwd", out_shape=jax.ShapeDtypeStruct((HEADS, REL_PAD), F32))(tab_pad)


def _relbias_bwd(z):
    def body(z_ref, o_ref):
        oh = _rel_onehot()
        dt2 = jnp.sum(z_ref[...], axis=1)
        acc = jnp.zeros((HEADS, REL_PAD), F32)
        for p in _split3(dt2):
            acc += _dot_nt(p, oh)
        o_ref[...] = acc

    return pl.pallas_call(body, name="relbias_bwd", out_shape=jax.ShapeDtypeStruct((HEADS, REL_PAD), F32))(z)


def _bias_blocks(t2):
    flat = jnp.tile(t2, (1, CHUNK))
    skew = flat[:, :CHUNK * (REL_PAD - 1)].reshape(HEADS, CHUNK, REL_PAD - 1)
    bias = skew[:, :, CHUNK - 1:CHUNK - 1 + BAND]
    slabs = [jnp.pad(bias, ((0, 0), (0, 0), (CHUNK * c, ATT_KEYS - BAND - CHUNK * c)), constant_values=NEG_INF)
             for c in range(2)]
    return jnp.concatenate(slabs, axis=1)


def _unskew(db2):
    out = []
    for c in range(2):
        slab = db2[:, CHUNK * c:CHUNK * (c + 1), CHUNK * c:CHUNK * c + BAND]
        y = jnp.pad(slab, ((0, 0), (0, 0), (CHUNK - 1, REL_PAD - BAND - CHUNK + 1)))
        yf = jnp.pad(y.reshape(HEADS, CHUNK * REL_PAD), ((0, 0), (0, CHUNK)))
        out.append(yf.reshape(HEADS, CHUNK, REL_PAD + 1)[:, :, :REL_PAD])
    return jnp.concatenate(out, axis=1)


def _att_load(qkv_hbm, q_s, k_s, v_s, sem, T):
    copies = [pltpu.make_async_copy(qkv_hbm.at[:, 0:D_ATT], q_s, sem.at[0]),
              pltpu.make_async_copy(qkv_hbm.at[:, D_ATT:2 * D_ATT], k_s.at[pl.ds(ATT_PAD, T), :], sem.at[1]),
              pltpu.make_async_copy(qkv_hbm.at[:, 2 * D_ATT:3 * D_ATT], v_s.at[pl.ds(ATT_PAD, T), :], sem.at[2])]
    for cp in copies:
        cp.start()
    k_s[0:ATT_PAD, :] = jnp.zeros((ATT_PAD, D_ATT), BF16)
    v_s[0:ATT_PAD, :] = jnp.zeros((ATT_PAD, D_ATT), BF16)
    for cp in copies:
        cp.wait()


def _head(v, h):
    return v[:, h * HEAD_DIM:(h + 1) * HEAD_DIM]


def _rows(v, h):
    return v[h * ATT_ROWS:(h + 1) * ATT_ROWS]


def _att_exp(qs, kw, bias_ref, valid):
    s = jnp.concatenate([_dot_nt(_head(qs, h), _head(kw, h)) + bias_ref[h] for h in range(HEADS)], axis=0)
    if valid is not None:
        s = jnp.where(valid, s, NEG_INF)
    e = jnp.exp(s - jnp.max(s, axis=-1, keepdims=True))
    return e, 1.0 / jnp.sum(e, axis=-1, keepdims=True)


def _att_blocks(T, block):
    n_edge = min(ATT_PAD // ATT_ROWS, T // ATT_ROWS)

    def edge(i, carry):
        r0 = i * ATT_ROWS
        block(i, (lax.broadcasted_iota(jnp.int32, (1, ATT_KEYS), 1) + (r0 - ATT_PAD)) >= 0)
        return carry

    def inner(i, carry):
        block(i, None)
        return carry

    lax.fori_loop(0, n_edge, edge, 0)
    lax.fori_loop(n_edge, T // ATT_ROWS, inner, 0)


def _att_fwd(qkv, bias2, payload=None):
    T = qkv.shape[0]

    def body(qkv_hbm, bias_ref, y_ref, q_s, k_s, v_s, sem):
        _att_load(qkv_hbm, q_s, k_s, v_s, sem, T)

        def block(i, valid):
            r0 = pl.multiple_of(i * ATT_ROWS, ATT_ROWS)
            qs = q_s[pl.ds(r0, ATT_ROWS), :] * (HEAD_DIM ** -0.5)
            kw = k_s[pl.ds(r0, ATT_KEYS), :]
            vw = v_s[pl.ds(r0, ATT_KEYS), :]
            e, rinv = _att_exp(qs, kw, bias_ref, valid)
            eb = e.astype(BF16)
            outs = [_dot(_rows(eb, h), _head(vw, h)) * _rows(rinv, h) for h in range(HEADS)]
            y_ref[pl.ds(r0, ATT_ROWS), :] = jnp.concatenate(outs, axis=1).astype(BF16)

        _att_blocks(T, block)

    return _call(
        body, payload, name="att_fwd", grid=None,
        in_specs=[pl.BlockSpec(memory_space=pl.ANY), pl.BlockSpec(memory_space=pltpu.VMEM)],
        out_specs=[pl.BlockSpec(memory_space=pltpu.VMEM)],
        out_shape=[jax.ShapeDtypeStruct((T, D_ATT), BF16)],
        scratch_shapes=[pltpu.VMEM((T, D_ATT), BF16), pltpu.VMEM((T + ATT_PAD, D_ATT), BF16),
                        pltpu.VMEM((T + ATT_PAD, D_ATT), BF16), pltpu.SemaphoreType.DMA((3,))],
        operands=(qkv, bias2))


def _att_bwd(qkv, dy, bias2, payload=None):
    T = qkv.shape[0]

    def body(qkv_hbm, dy_ref, bias_ref, dq_ref, dk_ref, dv_ref, db_ref, q_s, k_s, v_s, sem):
        _att_load(qkv_hbm, q_s, k_s, v_s, sem, T)
        dk_ref[...] = jnp.zeros_like(dk_ref)
        dv_ref[...] = jnp.zeros_like(dv_ref)
        db_ref[...] = jnp.zeros_like(db_ref)


        def block(i, valid):
            r0 = pl.multiple_of(i * ATT_ROWS, ATT_ROWS)
            qs = q_s[pl.ds(r0, ATT_ROWS), :] * (HEAD_DIM ** -0.5)
            kw = k_s[pl.ds(r0, ATT_KEYS), :]
            vw = v_s[pl.ds(r0, ATT_KEYS), :]
            dyb = dy_ref[pl.ds(r0, ATT_ROWS), :]
            e, rinv = _att_exp(qs, kw, bias_ref, valid)
            p = e * rinv
            dp = jnp.concatenate([_dot_nt(_head(dyb, h), _head(vw, h)) for h in range(HEADS)], axis=0)
            ds = p * (dp - jnp.sum(p * dp, axis=-1, keepdims=True))
            db_ref[...] += ds.reshape(HEADS, ATT_ROWS, ATT_KEYS)
            dsb = ds.astype(BF16)
            pb = p.astype(BF16)
            dq = [_dot(_rows(dsb, h), _head(kw, h)) for h in range(HEADS)]
            dk = [_dot_tn(_rows(dsb, h), _head(qs, h)) for h in range(HEADS)]
            dv = [_dot_tn(_rows(pb, h), _head(dyb, h)) for h in range(HEADS)]
            dq_ref[pl.ds(r0, ATT_ROWS), :] = (jnp.concatenate(dq, axis=1) * (HEAD_DIM ** -0.5)).astype(BF16)
            dk_ref[pl.ds(r0, ATT_KEYS), :] += jnp.concatenate(dk, axis=1)
            dv_ref[pl.ds(r0, ATT_KEYS), :] += jnp.concatenate(dv, axis=1)

        _att_blocks(T, block)

    vmem = pl.BlockSpec(memory_space=pltpu.VMEM)
    return _call(
        body, payload, name="att_bwd", grid=None,
        in_specs=[pl.BlockSpec(memory_space=pl.ANY), vmem, vmem],
        out_specs=[vmem, vmem, vmem, vmem],
        out_shape=[jax.ShapeDtypeStruct((T, D_ATT), BF16), jax.ShapeDtypeStruct((T + ATT_PAD, D_ATT), F32),
                   jax.ShapeDtypeStruct((T + ATT_PAD, D_ATT), F32), jax.ShapeDtypeStruct((HEADS, ATT_ROWS, ATT_KEYS), F32)],
        scratch_shapes=[pltpu.VMEM((T, D_ATT), BF16), pltpu.VMEM((T + ATT_PAD, D_ATT), BF16),
                        pltpu.VMEM((T + ATT_PAD, D_ATT), BF16), pltpu.SemaphoreType.DMA((3,))],
        operands=(qkv, dy, bias2))


_GELU_C = 0.7978845608028654
_GELU_A = 0.044715


def _gelu(x):
    t = jnp.tanh(_GELU_C * (x + _GELU_A * x * x * x))
    return 0.5 * x * (1.0 + t), t


def _gelu_grad(x, t):
    return 0.5 * (1.0 + t) + 0.5 * x * (1.0 - t * t) * _GELU_C * (1.0 + 3.0 * _GELU_A * x * x)


def _group_masks():
    col = lax.broadcasted_iota(jnp.int32, (SGU_GROUPS, D_SGU), 1) // SGU_GDIM
    grp = lax.broadcasted_iota(jnp.int32, (SGU_GROUPS, D_SGU), 0)
    return jnp.where(col == grp, 1.0, 0.0).astype(F32)


def _causal_mask(transposed=False):
    i = lax.broadcasted_iota(jnp.int32, (SGU_BLOCK, SGU_BLOCK), 0) // CHUNK
    j = lax.broadcasted_iota(jnp.int32, (SGU_BLOCK, SGU_BLOCK), 1) // CHUNK
    return (j >= i) if transposed else (i >= j)


def _sgu_norm(zs, lng, lnb):
    gz, t = _gelu(zs)
    u = gz[:, 0:D_SGU]
    vs = gz[:, D_SGU:2 * D_SGU]
    xc = vs - jnp.mean(vs, axis=-1, keepdims=True)
    rstd = lax.rsqrt(jnp.mean(xc * xc, axis=-1, keepdims=True) + EPS)
    xhat = xc * rstd
    return t, u, xhat, rstd, xhat * lng + lnb


def _sgu_mix(vn_blk, w_ref, bst, gm):
    mask = _causal_mask()
    s = jnp.zeros((SGU_BLOCK, D_SGU), F32)
    for g in range(SGU_GROUPS):
        wm = jnp.where(mask, w_ref[g], 0.0).astype(BF16)
        s += _dot(wm, (vn_blk * gm[g:g + 1, :]).astype(BF16))
        s += bst[:, g:g + 1] * gm[g:g + 1, :]
    return s


def _sgu_fwd(zs, lng, lnb, w_s, bst):
    T = zs.shape[0]
    nblk = TM // SGU_BLOCK

    def body(zs_ref, lng_ref, lnb_ref, w_ref, bst_ref, y_ref):
        _, u, _, _, vn = _sgu_norm(zs_ref[...], lng_ref[...], lnb_ref[...])
        gm = _group_masks()
        bst_v = bst_ref[...]
        for n in range(nblk):
            rows = slice(n * SGU_BLOCK, (n + 1) * SGU_BLOCK)
            s = _sgu_mix(vn[rows], w_ref, bst_v, gm)
            y_ref[rows, :] = (u[rows] * s).astype(BF16)

    return pl.pallas_call(
        body, name="sgu_fwd", grid=(T // TM,),
        in_specs=[_row_spec(TM, 2 * D_SGU), _const_spec((1, D_SGU)), _const_spec((1, D_SGU)),
                  _const_spec(w_s.shape), _const_spec(bst.shape)],
        out_specs=_row_spec(TM, D_SGU),
        out_shape=jax.ShapeDtypeStruct((T, D_SGU), BF16),
        compiler_params=_params(("arbitrary",)),
    )(zs, lng, lnb, w_s, bst)


def _sgu_bwd(zs, dy, lng, lnb, w_s, w_st, bst):
    T = zs.shape[0]
    nblk = TM // SGU_BLOCK

    def body(zs_ref, dy_ref, lng_ref, lnb_ref, w_ref, wt_ref, bst_ref, dzs_ref, dw_ref, dbt_ref, dlg_ref, dlb_ref):
        @pl.when(pl.program_id(0) == 0)
        def _():
            dw_ref[...] = jnp.zeros_like(dw_ref)
            dbt_ref[...] = jnp.zeros_like(dbt_ref)
            dlg_ref[...] = jnp.zeros_like(dlg_ref)
            dlb_ref[...] = jnp.zeros_like(dlb_ref)

        zs_v = zs_ref[...]
        lng_v = lng_ref[...]
        t, u, xhat, rstd, vn = _sgu_norm(zs_v, lng_v, lnb_ref[...])
        gm = _group_masks()
        bst_v = bst_ref[...]
        mask = _causal_mask()
        mask_t = _causal_mask(transposed=True)
        dyv = dy_ref[...].astype(F32)
        lane8 = lax.broadcasted_iota(jnp.int32, (1, SGU_GROUPS), 1)
        du_rows, dvn_rows = [], []
        for n in range(nblk):
            rows = slice(n * SGU_BLOCK, (n + 1) * SGU_BLOCK)
            vn_b = vn[rows]
            s = _sgu_mix(vn_b, w_ref, bst_v, gm)
            du_rows.append(dyv[rows] * s)
            dsb = dyv[rows] * u[rows]
            vnb16 = vn_b.astype(BF16)
            dvn = jnp.zeros((SGU_BLOCK, D_SGU), F32)
            dbt = jnp.zeros((SGU_BLOCK, SGU_GROUPS), F32)
            for g in range(SGU_GROUPS):
                dsg = dsb * gm[g:g + 1, :]
                dsg16 = dsg.astype(BF16)
                wmt = jnp.where(mask_t, wt_ref[g], 0.0).astype(BF16)
                dvn += _dot(wmt, dsg16)
                dw_ref[g] += jnp.where(mask, _dot_nt(dsg16, vnb16), 0.0)
                dbt += jnp.sum(dsg, axis=-1, keepdims=True) * jnp.where(lane8 == g, 1.0, 0.0)
            dbt_ref[...] += dbt
            dvn_rows.append(dvn)
        du = jnp.concatenate(du_rows, axis=0)
        dvn = jnp.concatenate(dvn_rows, axis=0)
        dlg_ref[...] += jnp.sum(dvn * xhat, axis=0, keepdims=True)
        dlb_ref[...] += jnp.sum(dvn, axis=0, keepdims=True)
        dxhat = dvn * lng_v
        dvs = rstd * (dxhat - jnp.mean(dxhat, axis=-1, keepdims=True)
                      - xhat * jnp.mean(dxhat * xhat, axis=-1, keepdims=True))
        dgz = jnp.concatenate([du, dvs], axis=1)
        dzs_ref[...] = (dgz * _gelu_grad(zs_v, t)).astype(BF16)

    return pl.pallas_call(
        body, name="sgu_bwd", grid=(T // TM,),
        in_specs=[_row_spec(TM, 2 * D_SGU), _row_spec(TM, D_SGU), _const_spec((1, D_SGU)), _const_spec((1, D_SGU)),
                  _const_spec(w_s.shape), _const_spec(w_st.shape), _const_spec(bst.shape)],
        out_specs=[_row_spec(TM, 2 * D_SGU), _acc_spec(w_s.shape), _acc_spec(bst.shape), _acc_spec((1, D_SGU)),
                   _acc_spec((1, D_SGU))],
        out_shape=[jax.ShapeDtypeStruct((T, 2 * D_SGU), BF16), jax.ShapeDtypeStruct(w_s.shape, F32),
                   jax.ShapeDtypeStruct(bst.shape, F32), jax.ShapeDtypeStruct((1, D_SGU), F32),
                   jax.ShapeDtypeStruct((1, D_SGU), F32)],
        compiler_params=_params(("arbitrary",)),
    )(zs, dy, lng, lnb, w_s, w_st, bst)


def _cols(v, s):
    return v[:, s * BR_S:(s + 1) * BR_S]


def _merge_fwd(x, y_att, y_sgu, gl, b_gate, wba, wbs, wo, payload=None):
    T = x.shape[0]

    def body(x_ref, ya_ref, ys_ref, gl_ref, bg_ref, wba_ref, wbs_ref, wo_ref, xo_ref, m_ref, pa_ref, ps_ref):
        ya = ya_ref[...]
        ys = ys_ref[...]
        pa = jnp.concatenate([_dot(ya, wba_ref[s]) for s in range(N_SHARD)], axis=1)
        ps = jnp.concatenate([_dot(ys, wbs_ref[s]) for s in range(N_SHARD)], axis=1)
        g = _sigmoid(gl_ref[...] + bg_ref[...])
        mb = (g[:, 0:D_MODEL] * pa + g[:, D_MODEL:2 * D_MODEL] * ps).astype(BF16)
        m_ref[...] = mb
        pa_ref[...] = pa.astype(BF16)
        ps_ref[...] = ps.astype(BF16)
        acc = jnp.zeros((TM, D_MODEL), F32)
        for s in range(N_SHARD):
            acc += _dot(_cols(mb, s), wo_ref[s])
        xo_ref[...] = x_ref[...] + acc

    tokd = jax.ShapeDtypeStruct((T, D_MODEL), BF16)
    return _call(
        body, payload, name="merge_fwd", grid=(T // TM,), when=_edges(T // TM), sem=("arbitrary",),
        in_specs=[_row_spec(TM, D_MODEL), _row_spec(TM, D_ATT), _row_spec(TM, D_SGU), _row_spec(TM, 2 * D_MODEL),
                  _const_spec((1, 2 * D_MODEL)), _const_spec(wba.shape), _const_spec(wbs.shape), _const_spec(wo.shape)],
        out_specs=[_row_spec(TM, D_MODEL)] * 4,
        out_shape=[jax.ShapeDtypeStruct((T, D_MODEL), F32), tokd, tokd, tokd],
        operands=(x, y_att, y_sgu, gl, b_gate, wba, wbs, wo))


def _merge_bwd(dx, y_att, y_sgu, gl, merged, pa, ps, b_gate, wba, wbs, wo, payload=None):
    T = dx.shape[0]

    def body(dx_ref, ya_ref, ys_ref, gl_ref, m_ref, pa_ref, ps_ref, bg_ref, wba_ref, wbs_ref, wo_ref,
             dya_ref, dys_ref, dgl_ref, dbg_ref, gwba_ref, gwbs_ref, gwo_ref):
        @pl.when(pl.program_id(0) == 0)
        def _():
            dbg_ref[...] = jnp.zeros_like(dbg_ref)
            gwba_ref[...] = jnp.zeros_like(gwba_ref)
            gwbs_ref[...] = jnp.zeros_like(gwbs_ref)
            gwo_ref[...] = jnp.zeros_like(gwo_ref)

        dxb = dx_ref[...].astype(BF16)
        dm = jnp.concatenate([_dot_nt(dxb, wo_ref[s]) for s in range(N_SHARD)], axis=1)
        g = _sigmoid(gl_ref[...] + bg_ref[...])
        ga = g[:, 0:D_MODEL]
        gs = g[:, D_MODEL:2 * D_MODEL]
        dpa = (dm * ga).astype(BF16)
        dps = (dm * gs).astype(BF16)
        dgl = jnp.concatenate([dm * pa_ref[...].astype(F32) * ga * (1.0 - ga),
                               dm * ps_ref[...].astype(F32) * gs * (1.0 - gs)], axis=1)
        dgl_ref[...] = dgl.astype(BF16)
        dbg_ref[...] += jnp.sum(dgl, axis=0, keepdims=True)
        ya = ya_ref[...]
        ys = ys_ref[...]
        mb = m_ref[...]
        dya = jnp.zeros((TM, D_ATT), F32)
        dys = jnp.zeros((TM, D_SGU), F32)
        for s in range(N_SHARD):
            dya += _dot_nt(_cols(dpa, s), wba_ref[s])
            dys += _dot_nt(_cols(dps, s), wbs_ref[s])
            gwo_ref[s] += _dot_tn(_cols(mb, s), dxb)
            gwba_ref[s] += _dot_tn(ya, _cols(dpa, s))
            gwbs_ref[s] += _dot_tn(ys, _cols(dps, s))
        dya_ref[...] = dya.astype(BF16)
        dys_ref[...] = dys.astype(BF16)

    return _call(
        body, payload, name="merge_bwd", grid=(T // TM,), when=_edges(T // TM), sem=("arbitrary",),
        operands=(dx, y_att, y_sgu, gl, merged, pa, ps, b_gate, wba, wbs, wo),
        in_specs=[_row_spec(TM, D_MODEL), _row_spec(TM, D_ATT), _row_spec(TM, D_SGU), _row_spec(TM, 2 * D_MODEL),
                  _row_spec(TM, D_MODEL), _row_spec(TM, D_MODEL), _row_spec(TM, D_MODEL),
                  _const_spec((1, 2 * D_MODEL)), _const_spec(wba.shape), _const_spec(wbs.shape), _const_spec(wo.shape)],
        out_specs=[_row_spec(TM, D_ATT), _row_spec(TM, D_SGU), _row_spec(TM, 2 * D_MODEL), _acc_spec((1, 2 * D_MODEL)),
                   _acc_spec(wba.shape), _acc_spec(wbs.shape), _acc_spec(wo.shape)],
        out_shape=[jax.ShapeDtypeStruct((T, D_ATT), BF16), jax.ShapeDtypeStruct((T, D_SGU), BF16),
                   jax.ShapeDtypeStruct((T, 2 * D_MODEL), BF16), jax.ShapeDtypeStruct((1, 2 * D_MODEL), F32),
                   jax.ShapeDtypeStruct(wba.shape, F32), jax.ShapeDtypeStruct(wbs.shape, F32),
                   jax.ShapeDtypeStruct(wo.shape, F32)])


def _loss_bwd(x, target, g):
    T = x.shape[0]

    def body(x_ref, t_ref, g_ref, dx_ref, loss_ref, dg_ref):
        @pl.when(pl.program_id(0) == 0)
        def _():
            loss_ref[...] = jnp.zeros_like(loss_ref)
            dg_ref[...] = jnp.zeros_like(dg_ref)

        gv = g_ref[...]
        xhat, r, y = _rms_fwd(x_ref[...], gv)
        err = y - t_ref[...]
        per_tok = jnp.mean(err * err, axis=-1, keepdims=True)
        loss_ref[...] += 0.5 * jnp.sum(per_tok, axis=0, keepdims=True)
        dxn, dg = _rms_bwd(err * (1.0 / D_MODEL), xhat, r, gv)
        dx_ref[...] = dxn
        dg_ref[...] += dg

    return pl.pallas_call(
        body, name="loss_bwd", grid=(T // TM,),
        in_specs=[_row_spec(TM, D_MODEL), _row_spec(TM, D_MODEL), _const_spec((1, D_MODEL))],
        out_specs=[_row_spec(TM, D_MODEL), _acc_spec((1, 128)), _acc_spec((1, D_MODEL))],
        out_shape=[jax.ShapeDtypeStruct((T, D_MODEL), F32), jax.ShapeDtypeStruct((1, 128), F32),
                   jax.ShapeDtypeStruct((1, D_MODEL), F32)],
        compiler_params=_params(("arbitrary",)),
    )(x, target, g)


BIG = ("ffn1_w_gate", "ffn1_w_up", "ffn1_w_down", "w_in", "w_branch_att", "w_branch_sgu", "w_out",
       "ffn2_w_gate", "ffn2_w_up", "ffn2_w_down")
SMALL = ("norm_ffn1", "norm_mix", "b_gate", "rel_bias", "sgu_ln_g", "sgu_ln_b", "sgu_w_s", "sgu_b_s", "norm_ffn2",
         "norm_final")


G_FFN1 = ("ffn1_w_gate", "ffn1_w_up", "ffn1_w_down")
G_MIX = ("w_in", "w_branch_att", "w_branch_sgu", "w_out")
G_FFN2 = ("ffn2_w_gate", "ffn2_w_up", "ffn2_w_down")


def _local_step(x, target, wb, ws, dist=None):
    def gather_on(names):
        return _ag_payload([wb[n] for n in names]) if dist else None

    t2 = _relbias_fwd(ws["rel_bias"])
    bias2 = _bias_blocks(t2)
    bst = ws["sgu_b_s"].T
    w_st = jnp.swapaxes(ws["sgu_w_s"], 1, 2)

    x1, h1, a1, b1, *got = _ffn_fwd(x, ws["norm_ffn1"], wb["ffn1_w_gate"], wb["ffn1_w_up"], wb["ffn1_w_down"],
                                    "ffn1_fwd", gather_on(G_MIX))
    wb.update(zip(G_MIX, got))
    h2, qkv, zs, gl, *got = _in_fwd(x1, ws["norm_mix"], wb["w_in"], gather_on(G_FFN2[0:1]))
    wb.update(zip(G_FFN2[0:1], got))
    y_att, *got = _att_fwd(qkv, bias2, gather_on(G_FFN2[1:2]))
    wb.update(zip(G_FFN2[1:2], got))
    y_sgu = _sgu_fwd(zs, ws["sgu_ln_g"], ws["sgu_ln_b"], ws["sgu_w_s"], bst)
    x2, merged, pa, ps, *got = _merge_fwd(x1, y_att, y_sgu, gl, ws["b_gate"], wb["w_branch_att"], wb["w_branch_sgu"],
                                          wb["w_out"], gather_on(G_FFN2[2:3]))
    wb.update(zip(G_FFN2[2:3], got))
    x3, h3, a3, b3 = _ffn_fwd(x2, ws["norm_ffn2"], wb["ffn2_w_gate"], wb["ffn2_w_up"], wb["ffn2_w_down"], "ffn2_fwd")
    dx3, loss, g_final = _loss_bwd(x3, target, ws["norm_final"])

    gb, gs, sums = {}, {"norm_final": g_final}, {}

    def pair_on(names, small=None):
        return _px_payload([gb[n] for n in names], small) if dist else None

    def pair_add(names, halves):
        for n, rv in zip(names, halves):
            sums[n] = _pair_add(gb[n], rv, dist[0], dist[1], "pair_add_" + n)

    def chips_on(names):
        return _cx_payload([sums[n][1] for n in names], [sums[n][2] for n in names]) if dist else None

    dx2, da3, db3, gs["norm_ffn2"] = _ffn_dgrad(dx3, x2, a3, b3, ws["norm_ffn2"], wb["ffn2_w_gate"], wb["ffn2_w_up"],
                                                wb["ffn2_w_down"], "ffn2_dgrad")
    gb["ffn2_w_gate"], gb["ffn2_w_up"], gb["ffn2_w_down"] = _ffn_wgrad(h3, dx3, a3, b3, da3, db3, "ffn2_wgrad")
    dy_att, dy_sgu, dgl, gs["b_gate"], gb["w_branch_att"], gb["w_branch_sgu"], gb["w_out"], *got = _merge_bwd(
        dx2, y_att, y_sgu, gl, merged, pa, ps, ws["b_gate"], wb["w_branch_att"], wb["w_branch_sgu"], wb["w_out"],
        pair_on(G_FFN2))
    pair_add(G_FFN2, got)
    dq, dk, dv, db2, *lands2 = _att_bwd(qkv, dy_att, bias2, chips_on(G_FFN2))
    gs["rel_bias"] = _relbias_bwd(_unskew(db2))
    dzs, gs["sgu_w_s"], dbt, gs["sgu_ln_g"], gs["sgu_ln_b"] = _sgu_bwd(zs, dy_sgu, ws["sgu_ln_g"], ws["sgu_ln_b"],
                                                                      ws["sgu_w_s"], w_st, bst)
    gs["sgu_b_s"] = dbt.T
    dx1, dz, gs["norm_mix"] = _in_dgrad(dx2, x1, ws["norm_mix"], wb["w_in"], dq, dk, dv, dzs, dgl)
    gb["w_in"] = _in_wgrad(h2, dz)
    gx, da1, db1, gs["norm_ffn1"], *got = _ffn_dgrad(dx1, x, a1, b1, ws["norm_ffn1"], wb["ffn1_w_gate"],
                                                    wb["ffn1_w_up"], wb["ffn1_w_down"], "ffn1_dgrad", pair_on(G_MIX))
    pair_add(G_MIX, got)
    gb["ffn1_w_gate"], gb["ffn1_w_up"], gb["ffn1_w_down"], *lands_mix = _ffn_wgrad(h1, dx1, a1, b1, da1, db1,
                                                                                   "ffn1_wgrad", chips_on(G_MIX))
    if not dist:
        return loss, gx, gb, gs

    packed = _pack_small(gs, loss)
    *got, packed_sib = _exchange(pair_on(G_FFN1, packed), "pair_exchange_ffn1")
    pair_add(G_FFN1, got)
    small_sums, *lands1 = _small_exchange(packed, packed_sib, chips_on(G_FFN1))
    lands = dict(zip(G_FFN2 + G_MIX + G_FFN1, list(lands2) + list(lands_mix) + list(lands1)))
    fulls = [_final_sum(sums[n][0], lands[n], dist[1], dist[0], "final_sum_" + n) for n in BIG]
    return loss, gx, dict(zip(BIG, _sibling_share(fulls, "sibling_share"))), small_sums


_ANY = pl.BlockSpec(memory_space=pl.ANY)
_VMEM = pl.BlockSpec(memory_space=pltpu.VMEM)


def _mesh_pos():
    return lax.axis_index("x"), lax.axis_index("y"), lax.axis_index("c")


def _cast_slots(shards, chip, name):
    n = len(shards)
    r, ncol = shards[0].shape
    tr = r // 2

    def body(me_ref, *refs):
        for i_ref, o_ref in zip(refs[:n], refs[n:]):
            o_ref[0] = i_ref[...].astype(BF16)

    grid_spec = pltpu.PrefetchScalarGridSpec(
        num_scalar_prefetch=1, grid=(r // tr,),
        in_specs=[pl.BlockSpec((tr, ncol), lambda i, me: (i, 0))] * n,
        out_specs=[pl.BlockSpec((1, tr, ncol), lambda i, me: (me[0], i, 0))] * n)
    return pl.pallas_call(
        body, name=name, grid_spec=grid_spec,
        out_shape=[jax.ShapeDtypeStruct((N_SHARD, r, ncol), BF16)] * n,
        compiler_params=_params(("arbitrary",)),
    )(chip, *shards)


class _Payload:
    def __init__(self, arrays, out_shapes, aliases, scratch, phases):
        self.arrays = list(arrays)
        self.out_shapes = list(out_shapes)
        self.aliases = dict(aliases)
        self.scratch = list(scratch)
        self.phases = phases


def _remote(src, dst, ssem, rsem, dev):
    return pltpu.make_async_remote_copy(src_ref=src, dst_ref=dst, send_sem=ssem, recv_sem=rsem, device_id=dev,
                                        device_id_type=MESH)


def _call(body, payload, *, name, grid, in_specs, out_specs, out_shape, scratch_shapes=(), sem=None, when=None,
          operands=()):
    in_specs, out_specs, out_shape = list(in_specs), list(out_specs), list(out_shape)
    scratch_shapes = list(scratch_shapes)
    n_in, n_out, n_scr = len(in_specs), len(out_specs), len(scratch_shapes)
    kwargs = {}
    kernel = body
    if payload is not None:
        k_in, k_out = len(payload.arrays), len(payload.out_shapes)
        rank = len(grid) if grid else 0

        def kernel(*refs):
            a, b = n_in, n_in + k_in
            c, d = b + n_out, b + n_out + k_out
            e = d + n_scr
            phases = payload.phases(refs[a:b], refs[c:d], refs[e:])

            def run():
                body(*refs[:a], *refs[b:c], *refs[d:e])

            if not grid:
                phases[0]()
                run()
                for ph in phases[1:]:
                    ph()
                return
            step = pl.program_id(0)
            if rank == 2:
                step = step * grid[1] + pl.program_id(1)
            for ph, (at, before) in zip(phases, when):
                if before:
                    pl.when(step == at)(ph)
            run()
            for ph, (at, before) in zip(phases, when):
                if not before:
                    pl.when(step == at)(ph)

        in_specs += [_ANY] * k_in
        out_specs += [_ANY] * k_out
        out_shape += payload.out_shapes
        scratch_shapes += payload.scratch
        kwargs["input_output_aliases"] = {n_in + i: n_out + j for i, j in payload.aliases.items()}
        operands = tuple(operands) + tuple(payload.arrays)
    if grid:
        kwargs["grid"] = grid
    return pl.pallas_call(kernel, name=name, in_specs=in_specs, out_specs=out_specs, out_shape=out_shape,
                          scratch_shapes=scratch_shapes, compiler_params=_params(sem), **kwargs)(*operands)


def _exchange(payload, name):
    return _call(lambda: None, payload, name=name, grid=None, in_specs=[], out_specs=[], out_shape=[])


def _ag_payload(slots):
    n = len(slots)

    def phases(_, refs, sems):
        send_i, recv_i, send_d, recv_d = sems
        x, y, c = _mesh_pos()
        me = 2 * x + y

        def half(w, core):
            rh = slots[w].shape[1] // 2
            return pl.ds(core * rh, rh)

        def ici(w, j):
            t = (me + 1 + j) % N_SHARD
            mine = refs[w].at[me, half(w, c), :]
            return _remote(mine, mine, send_i.at[3 * w + j], recv_i.at[3 * w + j], (t // 2, t % 2, c))

        def d2d(w, j, core):
            s = (me + 3 - j) % N_SHARD
            land = refs[w].at[s, half(w, core), :]
            return _remote(land, land, send_d.at[3 * w + j], recv_d.at[3 * w + j], (x, y, 1 - c))

        def start():
            for w in range(n):
                for j in range(3):
                    ici(w, j).start()

        def finish():
            for w in range(n):
                for j in range(3):
                    s = (me + 3 - j) % N_SHARD
                    land = refs[w].at[s, half(w, c), :]
                    _remote(land, land, send_i.at[3 * w + j], recv_i.at[3 * w + j], (x, y, c)).wait_recv()
                    d2d(w, j, c).start()
            for w in range(n):
                for j in range(3):
                    d2d(w, j, 1 - c).wait_recv()
            for w in range(n):
                for j in range(3):
                    ici(w, j).wait_send()
                    d2d(w, j, c).wait_send()

        return [start, finish]

    return _Payload(slots, [jax.ShapeDtypeStruct(s.shape, s.dtype) for s in slots], {i: i for i in range(n)},
                    [pltpu.SemaphoreType.DMA((3 * n,)) for _ in range(4)], phases)


def _px_payload(grads, small=None):
    arrays = list(grads) + ([small] if small is not None else [])
    n = len(arrays)

    def phases(ins, outs, sems):
        send, recv = sems
        x, y, c = _mesh_pos()

        def copy(w):
            if w < len(grads):
                rh = grads[w].shape[1] // 2
                src = ins[w].at[:, pl.ds((1 - c) * rh, rh), :]
            else:
                src = ins[w]
            return _remote(src, outs[w], send.at[w], recv.at[w], (x, y, 1 - c))

        def start():
            for w in range(n):
                copy(w).start()

        def finish():
            for w in range(n):
                copy(w).wait()

        return [start, finish]

    out_shapes = [jax.ShapeDtypeStruct((N_SHARD, g.shape[1] // 2, g.shape[2]), F32) for g in grads]
    if small is not None:
        out_shapes.append(jax.ShapeDtypeStruct(small.shape, F32))
    return _Payload(arrays, out_shapes, {}, [pltpu.SemaphoreType.DMA((n,)), pltpu.SemaphoreType.DMA((n,))], phases)


def _cx_payload(pbs, lands):
    n = len(pbs)

    def phases(ins, outs, sems):
        send, recv = sems
        x, y, c = _mesh_pos()
        me = 2 * x + y

        def copy(w, j):
            t = (me + 1 + j) % N_SHARD
            return _remote(ins[w].at[t], outs[w].at[me], send.at[3 * w + j], recv.at[3 * w + j], (t // 2, t % 2, c))

        def start():
            for w in range(n):
                for j in range(3):
                    copy(w, j).start()

        def finish():
            for w in range(n):
                for j in range(3):
                    copy(w, j).wait()

        return [start, finish]

    return _Payload(list(pbs) + list(lands), [jax.ShapeDtypeStruct(p.shape, BF16) for p in lands],
                    {n + i: i for i in range(n)},
                    [pltpu.SemaphoreType.DMA((3 * n,)), pltpu.SemaphoreType.DMA((3 * n,))], phases)


def _pair_add(g, rv, core, chip, name):
    _, r, ncol = g.shape
    rh = r // 2

    def body(c_ref, me_ref, g_ref, rv_ref, pf_ref, pb_ref, land_ref):
        s = g_ref[0] + rv_ref[0]
        sb = s.astype(BF16)
        pb_ref[0] = sb
        land_ref[0] = sb

        @pl.when(pl.program_id(0) == me_ref[0])
        def _():
            pf_ref[...] = s

    slot = pl.BlockSpec((1, rh, ncol), lambda s, c, me: (s, 0, 0))
    grid_spec = pltpu.PrefetchScalarGridSpec(
        num_scalar_prefetch=2, grid=(N_SHARD,),
        in_specs=[pl.BlockSpec((1, rh, ncol), lambda s, c, me: (s, c[0], 0)), slot],
        out_specs=[pl.BlockSpec((rh, ncol), lambda s, c, me: (0, 0)), slot, slot])
    return pl.pallas_call(
        body, name=name, grid_spec=grid_spec,
        out_shape=[jax.ShapeDtypeStruct((rh, ncol), F32), jax.ShapeDtypeStruct((N_SHARD, rh, ncol), BF16),
                   jax.ShapeDtypeStruct((N_SHARD, rh, ncol), BF16)],
        compiler_params=_params(("arbitrary",)),
    )(core, chip, g, rv)


def _small_exchange(small, small_sib, payload=None):
    def body(sm, sm_sib, sm_out, sm_sum, send, recv, loc):
        x, y, c = _mesh_pos()
        me = 2 * x + y
        sm_sum[...] = sm[...] + sm_sib[...]
        local = pltpu.make_async_copy(sm_sum, sm_out.at[me], loc)
        local.start()
        cps = []
        for j in range(3):
            t = (me + 1 + j) % N_SHARD
            cp = pltpu.make_async_remote_copy(src_ref=sm_sum, dst_ref=sm_out.at[me], send_sem=send.at[j],
                                              recv_sem=recv.at[j], device_id=(t // 2, t % 2, c), device_id_type=MESH)
            cp.start()
            cps.append(cp)
        for cp in cps:
            cp.wait()
        local.wait()

    return _call(
        body, payload, name="small_exchange", grid=None,
        in_specs=[_VMEM, _VMEM], out_specs=[_ANY],
        out_shape=[jax.ShapeDtypeStruct((N_SHARD,) + small.shape, F32)],
        scratch_shapes=[pltpu.VMEM(small.shape, F32), pltpu.SemaphoreType.DMA((3,)), pltpu.SemaphoreType.DMA((3,)),
                        pltpu.SemaphoreType.DMA],
        operands=(small, small_sib))


def _final_sum(pf, land, chip, core, name):
    _, rh, ncol = land.shape

    def body(me_ref, c_ref, pf_ref, land_ref, o_ref):
        me = me_ref[0]
        acc = jnp.zeros((rh, ncol), F32)
        for k in range(N_SHARD):
            acc = acc + jnp.where(me == k, pf_ref[...], land_ref[k].astype(F32))
        o_ref[...] = acc

    grid_spec = pltpu.PrefetchScalarGridSpec(
        num_scalar_prefetch=2, grid=(1,),
        in_specs=[pl.BlockSpec((rh, ncol), lambda i, me, c: (0, 0)),
                  pl.BlockSpec((N_SHARD, rh, ncol), lambda i, me, c: (0, 0, 0))],
        out_specs=pl.BlockSpec((rh, ncol), lambda i, me, c: (c[0], 0)))
    return pl.pallas_call(
        body, name=name, grid_spec=grid_spec, out_shape=jax.ShapeDtypeStruct((2 * rh, ncol), F32),
        compiler_params=_params(("arbitrary",)),
    )(chip, core, pf, land)


def _sibling_share(fulls, name):
    n = len(fulls)

    def body(*refs):
        outs = refs[n:2 * n]
        send, recv = refs[2 * n:]
        x, y, c = _mesh_pos()
        cps = []
        for w in range(n):
            rh = fulls[w].shape[0] // 2
            mine = outs[w].at[pl.ds(c * rh, rh), :]
            cp = pltpu.make_async_remote_copy(src_ref=mine, dst_ref=mine, send_sem=send.at[w], recv_sem=recv.at[w],
                                              device_id=(x, y, 1 - c), device_id_type=MESH)
            cp.start()
            cps.append(cp)
        for cp in cps:
            cp.wait()

    return pl.pallas_call(
        body, name=name,
        in_specs=[_ANY] * n, out_specs=[_ANY] * n,
        out_shape=[jax.ShapeDtypeStruct(f.shape, F32) for f in fulls],
        input_output_aliases={i: i for i in range(n)},
        scratch_shapes=[pltpu.SemaphoreType.DMA((n,)), pltpu.SemaphoreType.DMA((n,))],
    )(*fulls)


_ROW = {"rel_bias": 128, "sgu_b_s": 136, "norm_ffn1": 144, "norm_mix": 145, "norm_ffn2": 146, "norm_final": 147,
        "b_gate": 148, "sgu_ln_g": 150, "sgu_ln_b": 151}


def _pack_small(gs, loss):
    def body(ws, rel, bs, n1, nm, n2, nf, bg, lg, lb, loss_ref, o_ref):
        o_ref[...] = jnp.zeros_like(o_ref)
        o_ref[LOSS_ROW:LOSS_ROW + 1, 0:128] = loss_ref[...]
        for g in range(SGU_GROUPS):
            o_ref[0:SGU_BLOCK, g * SGU_BLOCK:(g + 1) * SGU_BLOCK] = ws[g]
        o_ref[128:136, 0:REL_PAD] = rel[...]
        o_ref[136:144, 0:SGU_BLOCK] = bs[...]
        o_ref[144:145, :] = n1[...]
        o_ref[145:146, :] = nm[...]
        o_ref[146:147, :] = n2[...]
        o_ref[147:148, :] = nf[...]
        o_ref[148:149, :] = bg[:, 0:D_MODEL]
        o_ref[149:150, :] = bg[:, D_MODEL:2 * D_MODEL]
        o_ref[150:151, 0:D_SGU] = lg[...]
        o_ref[151:152, 0:D_SGU] = lb[...]

    order = ("sgu_w_s", "rel_bias", "sgu_b_s", "norm_ffn1", "norm_mix", "norm_ffn2", "norm_final", "b_gate", "sgu_ln_g",
             "sgu_ln_b")
    return pl.pallas_call(body, name="pack_small", out_shape=jax.ShapeDtypeStruct((SMALL_ROWS, D_MODEL), F32))(
        *[gs[k] for k in order], loss)


def _adam(w, g, m, v):
    m2 = ADAM_B1 * m + (1.0 - ADAM_B1) * g
    v2 = ADAM_B2 * v + (1.0 - ADAM_B2) * (g * g)
    m_hat = m2 / (1.0 - ADAM_B1 ** ADAM_STEP)
    v_hat = v2 / (1.0 - ADAM_B2 ** ADAM_STEP)
    delta = -ADAM_LR * (m_hat / (jnp.sqrt(v_hat) + ADAM_EPS) + ADAM_WD * w)
    return delta, m2, v2


def _adam_small(sin, w, m, v):
    names = SMALL
    k = len(names)

    def body(*refs):
        sin_ref = refs[0]
        w_r, m_r, v_r = refs[1:1 + k], refs[1 + k:1 + 2 * k], refs[1 + 2 * k:1 + 3 * k]
        outs = refs[1 + 3 * k:]
        tot = sin_ref[0] + sin_ref[1] + sin_ref[2] + sin_ref[3]
        outs[4 * k][...] = tot[LOSS_ROW:LOSS_ROW + 1, 0:128]
        for i, name in enumerate(names):
            o = outs[4 * i:4 * i + 4]
            if name == "sgu_w_s":
                for gi in range(SGU_GROUPS):
                    g = tot[0:SGU_BLOCK, gi * SGU_BLOCK:(gi + 1) * SGU_BLOCK]
                    res = (g,) + _adam(w_r[i][gi], g, m_r[i][gi], v_r[i][gi])
                    for ref, val in zip(o, res):
                        ref[gi] = val
                continue
            r0 = _ROW[name]
            if name == "rel_bias":
                g = tot[r0:r0 + HEADS, 0:REL_PAD]
            elif name == "sgu_b_s":
                g = tot[r0:r0 + SGU_GROUPS, 0:SGU_BLOCK]
            elif name == "b_gate":
                g = jnp.concatenate([tot[r0:r0 + 1, :], tot[r0 + 1:r0 + 2, :]], axis=1)
            elif name in ("sgu_ln_g", "sgu_ln_b"):
                g = tot[r0:r0 + 1, 0:D_SGU]
            else:
                g = tot[r0:r0 + 1, :]
            res = (g,) + _adam(w_r[i][...], g, m_r[i][...], v_r[i][...])
            for ref, val in zip(o, res):
                ref[...] = val

    out_shape = []
    for name in names:
        out_shape += [jax.ShapeDtypeStruct(w[name].shape, F32)] * 4
    out_shape.append(jax.ShapeDtypeStruct((1, 128), F32))
    flat = pl.pallas_call(body, name="adam_small", out_shape=out_shape, compiler_params=_params())(
        sin, *[w[n] for n in names], *[m[n] for n in names], *[v[n] for n in names])
    return {name: tuple(flat[4 * i:4 * i + 4]) for i, name in enumerate(names)}, flat[4 * k]


def _adam_big(w, g, m, v, name):
    r, ncol = w.shape
    tr = 256 if r % 256 == 0 else r // 2

    def body(w_ref, g_ref, m_ref, v_ref, d_ref, m2_ref, v2_ref):
        d_ref[...], m2_ref[...], v2_ref[...] = _adam(w_ref[...], g_ref[...], m_ref[...], v_ref[...])

    spec = pl.BlockSpec((tr, ncol), lambda i: (i, 0))
    return pl.pallas_call(
        body, name=name, grid=(r // tr,), in_specs=[spec] * 4, out_specs=[spec] * 3,
        out_shape=[jax.ShapeDtypeStruct(w.shape, F32)] * 3, compiler_params=_params(("arbitrary",)),
    )(w, g, m, v)


WEIGHTS = ("norm_ffn1", "ffn1_w_gate", "ffn1_w_up", "ffn1_w_down", "norm_mix", "w_in", "b_gate", "rel_bias", "sgu_ln_g",
           "sgu_ln_b", "sgu_w_s", "sgu_b_s", "w_branch_att", "w_branch_sgu", "w_out", "norm_ffn2", "ffn2_w_gate",
           "ffn2_w_up", "ffn2_w_down", "norm_final")


GATE_UP = ("ffn1_w_gate", "ffn1_w_up", "ffn2_w_gate", "ffn2_w_up")
_FFN = ("ffn1_w_gate", "ffn1_w_up", "ffn1_w_down", "ffn2_w_gate", "ffn2_w_up", "ffn2_w_down")
_CAST_GROUPS = ((_FFN, "cast_ffn"), (("w_in",), "cast_w_in"), (("w_branch_att", "w_branch_sgu"), "cast_branch"),
                (("w_out",), "cast_w_out"))


def _big_form(name, a):
    return jnp.swapaxes(a, 1, 2)[0] if name in GATE_UP else a[0]


def _big_back(name, a):
    return jnp.swapaxes(a[None], 1, 2) if name in GATE_UP else a[None]


def _small_form(name, a):
    if name == "norm_final":
        return a.reshape(1, D_MODEL)
    if name == "rel_bias":
        return jnp.pad(a[0], ((0, 0), (0, REL_PAD - N_REL)))
    if name in ("sgu_w_s", "sgu_b_s"):
        return a[0]
    return a


def _small_back(name, a, like):
    if name == "rel_bias":
        a = a[:, :N_REL]
    return a.reshape(like.shape)


def kernel(x, norm_ffn1, ffn1_w_gate, ffn1_w_up, ffn1_w_down, norm_mix, w_in, b_gate, rel_bias, sgu_ln_g, sgu_ln_b, sgu_w_s, sgu_b_s, w_branch_att, w_branch_sgu, w_out, norm_ffn2, ffn2_w_gate, ffn2_w_up, ffn2_w_down, norm_final, loss_target, m_norm_ffn1, m_ffn1_w_gate, m_ffn1_w_up, m_ffn1_w_down, m_norm_mix, m_w_in, m_b_gate, m_rel_bias, m_sgu_ln_g, m_sgu_ln_b, m_sgu_w_s, m_sgu_b_s, m_w_branch_att, m_w_branch_sgu, m_w_out, m_norm_ffn2, m_ffn2_w_gate, m_ffn2_w_up, m_ffn2_w_down, m_norm_final, v_norm_ffn1, v_ffn1_w_gate, v_ffn1_w_up, v_ffn1_w_down, v_norm_mix, v_w_in, v_b_gate, v_rel_bias, v_sgu_ln_g, v_sgu_ln_b, v_sgu_w_s, v_sgu_b_s, v_w_branch_att, v_w_branch_sgu, v_w_out, v_norm_ffn2, v_ffn2_w_gate, v_ffn2_w_up, v_ffn2_w_down, v_norm_final):
    w = dict(norm_ffn1=norm_ffn1, ffn1_w_gate=ffn1_w_gate, ffn1_w_up=ffn1_w_up, ffn1_w_down=ffn1_w_down, norm_mix=norm_mix,
             w_in=w_in, b_gate=b_gate, rel_bias=rel_bias, sgu_ln_g=sgu_ln_g, sgu_ln_b=sgu_ln_b, sgu_w_s=sgu_w_s,
             sgu_b_s=sgu_b_s, w_branch_att=w_branch_att, w_branch_sgu=w_branch_sgu, w_out=w_out, norm_ffn2=norm_ffn2,
             ffn2_w_gate=ffn2_w_gate, ffn2_w_up=ffn2_w_up, ffn2_w_down=ffn2_w_down, norm_final=norm_final)
    m = dict(norm_ffn1=m_norm_ffn1, ffn1_w_gate=m_ffn1_w_gate, ffn1_w_up=m_ffn1_w_up, ffn1_w_down=m_ffn1_w_down,
             norm_mix=m_norm_mix, w_in=m_w_in, b_gate=m_b_gate, rel_bias=m_rel_bias, sgu_ln_g=m_sgu_ln_g,
             sgu_ln_b=m_sgu_ln_b, sgu_w_s=m_sgu_w_s, sgu_b_s=m_sgu_b_s, w_branch_att=m_w_branch_att,
             w_branch_sgu=m_w_branch_sgu, w_out=m_w_out, norm_ffn2=m_norm_ffn2, ffn2_w_gate=m_ffn2_w_gate,
             ffn2_w_up=m_ffn2_w_up, ffn2_w_down=m_ffn2_w_down, norm_final=m_norm_final)
    v = dict(norm_ffn1=v_norm_ffn1, ffn1_w_gate=v_ffn1_w_gate, ffn1_w_up=v_ffn1_w_up, ffn1_w_down=v_ffn1_w_down,
             norm_mix=v_norm_mix, w_in=v_w_in, b_gate=v_b_gate, rel_bias=v_rel_bias, sgu_ln_g=v_sgu_ln_g,
             sgu_ln_b=v_sgu_ln_b, sgu_w_s=v_sgu_w_s, sgu_b_s=v_sgu_b_s, w_branch_att=v_w_branch_att,
             w_branch_sgu=v_w_branch_sgu, w_out=v_w_out, norm_ffn2=v_norm_ffn2, ffn2_w_gate=v_ffn2_w_gate,
             ffn2_w_up=v_ffn2_w_up, ffn2_w_down=v_ffn2_w_down, norm_final=v_norm_final)

    core = lax.axis_index("c").astype(jnp.int32).reshape(1)
    chip = (2 * lax.axis_index("x") + lax.axis_index("y")).astype(jnp.int32).reshape(1)

    wk = {n: _big_form(n, w[n]) for n in BIG}
    slots = {}
    for names, call in _CAST_GROUPS:
        slots.update(zip(names, _cast_slots([wk[n] for n in names], chip, call)))
    slots.update(zip(G_FFN1, _exchange(_ag_payload([slots[n] for n in G_FFN1]), "allgather_ffn1")))
    ws = {n: _small_form(n, w[n]) for n in SMALL}
    _, gx, shard_grads, small_sums = _local_step(x[0], loss_target[0], slots, ws, (core, chip))

    small, loss = _adam_small(small_sums, ws, {n: _small_form(n, m[n]) for n in SMALL},
                              {n: _small_form(n, v[n]) for n in SMALL})
    grad, delta, new_m, new_v = {}, {}, {}, {}
    for n in SMALL:
        grad[n], delta[n], new_m[n], new_v[n] = (_small_back(n, a, w[n]) for a in small[n])
    for n in BIG:
        g2 = shard_grads[n]
        d2, m2, v2 = _adam_big(wk[n], g2, _big_form(n, m[n]), _big_form(n, v[n]), "adam_" + n)
        grad[n], delta[n], new_m[n], new_v[n] = (_big_back(n, a) for a in (g2, d2, m2, v2))

    return (loss[0, 0], gx.reshape(x.shape), *[grad[n] for n in WEIGHTS], *[delta[n] for n in WEIGHTS],
            *[new_m[n] for n in WEIGHTS], *[new_v[n] for n in WEIGHTS])
```

```python
import functools

import jax
import jax.numpy as jnp
from jax import lax
from jax.experimental import pallas as pl
from jax.experimental.pallas import tpu as pltpu

F32 = jnp.float32
BF16 = jnp.bfloat16

D_MODEL = 1024
N_SHARD = 4
D_FF = 2816
FF_S = D_FF // N_SHARD
D_ATT = 512
D_SGU = 512
D_IN = 3 * D_ATT + 2 * D_SGU + 2 * D_MODEL
IN_S = D_IN // N_SHARD
BR_S = D_MODEL // N_SHARD
HEADS = 8
HEAD_DIM = 64
CHUNK = 64
N_LEFT = 8
BAND = (N_LEFT + 1) * CHUNK
REL_CLIP = 256
N_REL = 2 * REL_CLIP + 1
REL_PAD = 640
SGU_BLOCK = 128
SGU_GROUPS = 8
SGU_GDIM = 64
EPS = 1e-6
NEG_INF = -1e30

ATT_ROWS = 2 * CHUNK
ATT_KEYS = BAND + CHUNK
ATT_PAD = N_LEFT * CHUNK

ADAM_LR = 0.001
ADAM_B1 = 0.9
ADAM_B2 = 0.999
ADAM_EPS = 1e-08
ADAM_WD = 0.01
ADAM_STEP = 10

TM = 256
TW = 512
VMEM_LIMIT = 56 * 1024 * 1024

SMALL_ROWS = 160
LOSS_ROW = 152
MESH = pl.DeviceIdType.MESH

_NT = (((1,), (1,)), ((), ()))
_TN = (((0,), (0,)), ((), ()))


def _params(sem=None):
    return pltpu.CompilerParams(dimension_semantics=sem, vmem_limit_bytes=VMEM_LIMIT)


def _const_spec(shape):
    nd = len(shape)
    return pl.BlockSpec(shape, lambda *_: (0,) * nd, pipeline_mode=pl.Buffered(1))


def _acc_spec(shape):
    nd = len(shape)
    return pl.BlockSpec(shape, lambda *_: (0,) * nd)


def _row_spec(tm, ncols, off=0):
    return pl.BlockSpec((tm, ncols), lambda i: (i + off, 0))


def _row3_spec(tm, ncols):
    return pl.BlockSpec((N_SHARD, tm, ncols), lambda i: (0, i, 0))


def _dot(a, b):
    return jnp.dot(a, b, preferred_element_type=F32)


def _dot_nt(a, b):
    return lax.dot_general(a, b, _NT, preferred_element_type=F32)


def _dot_tn(a, b):
    return lax.dot_general(a, b, _TN, preferred_element_type=F32)


def _rms_fwd(x, g):
    r = lax.rsqrt(jnp.mean(x * x, axis=-1, keepdims=True) + EPS)
    xhat = x * r
    return xhat, r, xhat * g


def _rms_bwd(dh, xhat, r, g):
    dxhat = dh * g
    dx = r * (dxhat - xhat * jnp.mean(dxhat * xhat, axis=-1, keepdims=True))
    dg = jnp.sum(dh * xhat, axis=0, keepdims=True)
    return dx, dg


def _sigmoid(x):
    return 1.0 / (1.0 + jnp.exp(-x))


def _edges(n_steps):
    return [(0, True), (n_steps - 1, False)]


def _ffn_fwd(x, g, wg, wu, wd, name, payload=None):
    T = x.shape[0]

    def body(x_ref, g_ref, wg_ref, wu_ref, wd_ref, xo_ref, h_ref, a_ref, b_ref):
        xv = x_ref[...]
        hb = _rms_fwd(xv, g_ref[...])[2].astype(BF16)
        h_ref[...] = hb
        acc = jnp.zeros((TM, D_MODEL), F32)
        for s in range(N_SHARD):
            a = _dot_nt(hb, wg_ref[s])
            b = _dot_nt(hb, wu_ref[s])
            a_ref[s] = a.astype(BF16)
            b_ref[s] = b.astype(BF16)
            sv = a * _sigmoid(a) * b
            acc += _dot(sv.astype(BF16), wd_ref[s])
        xo_ref[...] = xv + 0.5 * acc

    return _call(
        body, payload, name=name, grid=(T // TM,), when=_edges(T // TM), sem=("arbitrary",),
        in_specs=[_row_spec(TM, D_MODEL), _const_spec((1, D_MODEL)), _const_spec(wg.shape), _const_spec(wu.shape),
                  _const_spec(wd.shape)],
        out_specs=[_row_spec(TM, D_MODEL), _row_spec(TM, D_MODEL), _row3_spec(TM, FF_S), _row3_spec(TM, FF_S)],
        out_shape=[jax.ShapeDtypeStruct((T, D_MODEL), F32), jax.ShapeDtypeStruct((T, D_MODEL), BF16),
                   jax.ShapeDtypeStruct((N_SHARD, T, FF_S), BF16), jax.ShapeDtypeStruct((N_SHARD, T, FF_S), BF16)],
        operands=(x, g, wg, wu, wd))


def _ffn_dgrad(dout, x, a, b, g, wg, wu, wd, name, payload=None):
    T = x.shape[0]

    def body(do_ref, x_ref, a_ref, b_ref, g_ref, wg_ref, wu_ref, wd_ref, dx_ref, da_ref, db_ref, dg_ref):
        do = do_ref[...]
        dob = do.astype(BF16)
        dh = jnp.zeros((TM, D_MODEL), F32)
        for s in range(N_SHARD):
            ds = 0.5 * _dot_nt(dob, wd_ref[s])
            av = a_ref[s].astype(F32)
            bv = b_ref[s].astype(F32)
            sig = _sigmoid(av)
            da = (ds * bv * (sig * (1.0 + av * (1.0 - sig)))).astype(BF16)
            db = (ds * (av * sig)).astype(BF16)
            da_ref[s] = da
            db_ref[s] = db
            dh += _dot(da, wg_ref[s]) + _dot(db, wu_ref[s])
        gv = g_ref[...]
        xhat, r, _ = _rms_fwd(x_ref[...], gv)
        dxn, dg = _rms_bwd(dh, xhat, r, gv)
        dx_ref[...] = do + dxn

        @pl.when(pl.program_id(0) == 0)
        def _():
            dg_ref[...] = jnp.zeros_like(dg_ref)

        dg_ref[...] += dg

    return _call(
        body, payload, name=name, grid=(T // TM,), when=_edges(T // TM), sem=("arbitrary",),
        in_specs=[_row_spec(TM, D_MODEL), _row_spec(TM, D_MODEL), _row3_spec(TM, FF_S), _row3_spec(TM, FF_S),
                  _const_spec((1, D_MODEL)), _const_spec(wg.shape), _const_spec(wu.shape), _const_spec(wd.shape)],
        out_specs=[_row_spec(TM, D_MODEL), _row3_spec(TM, FF_S), _row3_spec(TM, FF_S), _acc_spec((1, D_MODEL))],
        out_shape=[jax.ShapeDtypeStruct((T, D_MODEL), F32), jax.ShapeDtypeStruct((N_SHARD, T, FF_S), BF16),
                   jax.ShapeDtypeStruct((N_SHARD, T, FF_S), BF16), jax.ShapeDtypeStruct((1, D_MODEL), F32)],
        operands=(dout, x, a, b, g, wg, wu, wd))


def _ffn_wgrad(h, dout, a, b, da, db, name, payload=None):
    T = h.shape[0]

    def body(h_ref, do_ref, a_ref, b_ref, da_ref, db_ref, gwg_ref, gwu_ref, gwd_ref):
        @pl.when(pl.program_id(1) == 0)
        def _():
            gwg_ref[...] = jnp.zeros_like(gwg_ref)
            gwu_ref[...] = jnp.zeros_like(gwu_ref)
            gwd_ref[...] = jnp.zeros_like(gwd_ref)

        hv = h_ref[...]
        dob = do_ref[...].astype(BF16)
        av = a_ref[0].astype(F32)
        sv = (0.5 * av * _sigmoid(av) * b_ref[0].astype(F32)).astype(BF16)
        gwg_ref[0] += _dot_tn(da_ref[0], hv)
        gwu_ref[0] += _dot_tn(db_ref[0], hv)
        gwd_ref[0] += _dot_tn(sv, dob)

    tok = pl.BlockSpec((TW, D_MODEL), lambda s, i: (i, 0))
    act = pl.BlockSpec((1, TW, FF_S), lambda s, i: (s, i, 0))
    return _call(
        body, payload, name=name, grid=(N_SHARD, T // TW), when=_edges(N_SHARD * (T // TW)),
        sem=("arbitrary", "arbitrary"),
        in_specs=[tok, tok, act, act, act, act],
        out_specs=[pl.BlockSpec((1, FF_S, D_MODEL), lambda s, i: (s, 0, 0))] * 3,
        out_shape=[jax.ShapeDtypeStruct((N_SHARD, FF_S, D_MODEL), F32)] * 3,
        operands=(h, dout, a, b, da, db))


def _in_fwd(x, g, w_in, payload=None):
    T = x.shape[0]

    def body(x_ref, g_ref, w_ref, h_ref, qkv_ref, zs_ref, gl_ref):
        hb = _rms_fwd(x_ref[...], g_ref[...])[2].astype(BF16)
        h_ref[...] = hb
        z0 = _dot(hb, w_ref[0])
        qkv_ref[:, 0:IN_S] = z0.astype(BF16)
        z1 = _dot(hb, w_ref[1])
        qkv_ref[:, IN_S:3 * D_ATT] = z1[:, 0:384].astype(BF16)
        zs_ref[:, 0:768] = z1[:, 384:IN_S]
        z2 = _dot(hb, w_ref[2])
        zs_ref[:, 768:1024] = z2[:, 0:256]
        gl_ref[:, 0:896] = z2[:, 256:IN_S]
        gl_ref[:, 896:2048] = _dot(hb, w_ref[3])

    return _call(
        body, payload, name="in_fwd", grid=(T // TM,), when=_edges(T // TM), sem=("arbitrary",),
        in_specs=[_row_spec(TM, D_MODEL), _const_spec((1, D_MODEL)), _const_spec(w_in.shape)],
        out_specs=[_row_spec(TM, D_MODEL), _row_spec(TM, 3 * D_ATT), _row_spec(TM, 2 * D_SGU), _row_spec(TM, 2 * D_MODEL)],
        out_shape=[jax.ShapeDtypeStruct((T, D_MODEL), BF16), jax.ShapeDtypeStruct((T, 3 * D_ATT), BF16),
                   jax.ShapeDtypeStruct((T, 2 * D_SGU), F32), jax.ShapeDtypeStruct((T, 2 * D_MODEL), F32)],
        operands=(x, g, w_in))


def _in_dgrad(dx_res, x, g, w_in, dq, dk, dv, dzs, dgl):
    T = x.shape[0]

    def body(dxr_ref, x_ref, g_ref, w_ref, dq_ref, dk_ref, dv_ref, dzs_ref, dgl_ref, dx_ref, dz_ref, dg_ref):
        dz = jnp.concatenate([dq_ref[...], dk_ref[...].astype(BF16), dv_ref[...].astype(BF16), dzs_ref[...], dgl_ref[...]],
                             axis=1)
        dz_ref[...] = dz
        dh = jnp.zeros((TM, D_MODEL), F32)
        for s in range(N_SHARD):
            dh += _dot_nt(dz[:, s * IN_S:(s + 1) * IN_S], w_ref[s])
        gv = g_ref[...]
        xhat, r, _ = _rms_fwd(x_ref[...], gv)
        dxn, dg = _rms_bwd(dh, xhat, r, gv)
        dx_ref[...] = dxr_ref[...] + dxn

        @pl.when(pl.program_id(0) == 0)
        def _():
            dg_ref[...] = jnp.zeros_like(dg_ref)

        dg_ref[...] += dg

    pad_blocks = ATT_PAD // TM
    return pl.pallas_call(
        body, name="in_dgrad", grid=(T // TM,),
        in_specs=[_row_spec(TM, D_MODEL), _row_spec(TM, D_MODEL), _const_spec((1, D_MODEL)), _const_spec(w_in.shape),
                  _row_spec(TM, D_ATT), _row_spec(TM, D_ATT, pad_blocks), _row_spec(TM, D_ATT, pad_blocks),
                  _row_spec(TM, 2 * D_SGU), _row_spec(TM, 2 * D_MODEL)],
        out_specs=[_row_spec(TM, D_MODEL), _row_spec(TM, D_IN), _acc_spec((1, D_MODEL))],
        out_shape=[jax.ShapeDtypeStruct((T, D_MODEL), F32), jax.ShapeDtypeStruct((T, D_IN), BF16),
                   jax.ShapeDtypeStruct((1, D_MODEL), F32)],
        compiler_params=_params(("arbitrary",)),
    )(dx_res, x, g, w_in, dq, dk, dv, dzs, dgl)


def _in_wgrad(h, dz):
    T = h.shape[0]

    def body(h_ref, dz_ref, gw_ref):
        @pl.when(pl.program_id(1) == 0)
        def _():
            gw_ref[...] = jnp.zeros_like(gw_ref)

        gw_ref[0] += _dot_tn(h_ref[...], dz_ref[...])

    return pl.pallas_call(
        body, name="in_wgrad", grid=(N_SHARD, T // TW),
        in_specs=[pl.BlockSpec((TW, D_MODEL), lambda s, i: (i, 0)), pl.BlockSpec((TW, IN_S), lambda s, i: (i, s))],
        out_specs=pl.BlockSpec((1, D_MODEL, IN_S), lambda s, i: (s, 0, 0)),
        out_shape=jax.ShapeDtypeStruct((N_SHARD, D_MODEL, IN_S), F32),
        compiler_params=_params(("arbitrary", "arbitrary")),
    )(h, dz)


def _rel_onehot():
    r = lax.broadcasted_iota(jnp.int32, (REL_PAD, REL_PAD), 0)
    n = lax.broadcasted_iota(jnp.int32, (REL_PAD, REL_PAD), 1)
    idx = jnp.clip(BAND - 1 - n, -REL_CLIP, REL_CLIP) + REL_CLIP
    return jnp.where(r == idx, 1.0, 0.0).astype(BF16)


def _split3(v):
    p1 = v.astype(BF16)
    r1 = v - p1.astype(F32)
    p2 = r1.astype(BF16)
    p3 = (r1 - p2.astype(F32)).astype(BF16)
    return p1, p2, p3


def _relbias_fwd(tab_pad):
    def body(t_ref, o_ref):
        oh = _rel_onehot()
        acc = jnp.zeros((HEADS, REL_PAD), F32)
        for p in _split3(t_ref[...]):
            acc += _dot(p, oh)
        o_ref[...] = acc

    return pl.pallas_call(body, name="relbias_fwd", out_shape=jax.ShapeDtypeStruct((HEADS, REL_PAD), F32))(tab_pad)


def _relbias_bwd(z):
    def body(z_ref, o_ref):
        oh = _rel_onehot()
        dt2 = jnp.sum(z_ref[...], axis=1)
        acc = jnp.zeros((HEADS, REL_PAD), F32)
        for p in _split3(dt2):
            acc += _dot_nt(p, oh)
        o_ref[...] = acc

    return pl.pallas_call(body, name="relbias_bwd", out_shape=jax.ShapeDtypeStruct((HEADS, REL_PAD), F32))(z)


def _bias_blocks(t2):
    flat = jnp.tile(t2, (1, CHUNK))
    skew = flat[:, :CHUNK * (REL_PAD - 1)].reshape(HEADS, CHUNK, REL_PAD - 1)
    bias = skew[:, :, CHUNK - 1:CHUNK - 1 + BAND]
    slabs = [jnp.pad(bias, ((0, 0), (0, 0), (CHUNK * c, ATT_KEYS - BAND - CHUNK * c)), constant_values=NEG_INF)
             for c in range(2)]
    return jnp.concatenate(slabs, axis=1)


def _unskew(db2):
    out = []
    for c in range(2):
        slab = db2[:, CHUNK * c:CHUNK * (c + 1), CHUNK * c:CHUNK * c + BAND]
        y = jnp.pad(slab, ((0, 0), (0, 0), (CHUNK - 1, REL_PAD - BAND - CHUNK + 1)))
        yf = jnp.pad(y.reshape(HEADS, CHUNK * REL_PAD), ((0, 0), (0, CHUNK)))
        out.append(yf.reshape(HEADS, CHUNK, REL_PAD + 1)[:, :, :REL_PAD])
    return jnp.concatenate(out, axis=1)


def _att_load(qkv_hbm, q_s, k_s, v_s, sem, T):
    copies = [pltpu.make_async_copy(qkv_hbm.at[:, 0:D_ATT], q_s, sem.at[0]),
              pltpu.make_async_copy(qkv_hbm.at[:, D_ATT:2 * D_ATT], k_s.at[pl.ds(ATT_PAD, T), :], sem.at[1]),
              pltpu.make_async_copy(qkv_hbm.at[:, 2 * D_ATT:3 * D_ATT], v_s.at[pl.ds(ATT_PAD, T), :], sem.at[2])]
    for cp in copies:
        cp.start()
    k_s[0:ATT_PAD, :] = jnp.zeros((ATT_PAD, D_ATT), BF16)
    v_s[0:ATT_PAD, :] = jnp.zeros((ATT_PAD, D_ATT), BF16)
    for cp in copies:
        cp.wait()


def _head(v, h):
    return v[:, h * HEAD_DIM:(h + 1) * HEAD_DIM]


def _rows(v, h):
    return v[h * ATT_ROWS:(h + 1) * ATT_ROWS]


def _att_exp(qs, kw, bias_ref, valid):
    s = jnp.concatenate([_dot_nt(_head(qs, h), _head(kw, h)) + bias_ref[h] for h in range(HEADS)], axis=0)
    if valid is not None:
        s = jnp.where(valid, s, NEG_INF)
    e = jnp.exp(s - jnp.max(s, axis=-1, keepdims=True))
    return e, 1.0 / jnp.sum(e, axis=-1, keepdims=True)


def _att_blocks(T, block):
    n_edge = min(ATT_PAD // ATT_ROWS, T // ATT_ROWS)

    def edge(i, carry):
        r0 = i * ATT_ROWS
        block(i, (lax.broadcasted_iota(jnp.int32, (1, ATT_KEYS), 1) + (r0 - ATT_PAD)) >= 0)
        return carry

    def inner(i, carry):
        block(i, None)
        return carry

    lax.fori_loop(0, n_edge, edge, 0)
    lax.fori_loop(n_edge, T // ATT_ROWS, inner, 0)


def _att_fwd(qkv, bias2, payload=None):
    T = qkv.shape[0]

    def body(qkv_hbm, bias_ref, y_ref, q_s, k_s, v_s, sem):
        _att_load(qkv_hbm, q_s, k_s, v_s, sem, T)

        def block(i, valid):
            r0 = pl.multiple_of(i * ATT_ROWS, ATT_ROWS)
            qs = q_s[pl.ds(r0, ATT_ROWS), :] * (HEAD_DIM ** -0.5)
            kw = k_s[pl.ds(r0, ATT_KEYS), :]
            vw = v_s[pl.ds(r0, ATT_KEYS), :]
            e, rinv = _att_exp(qs, kw, bias_ref, valid)
            eb = e.astype(BF16)
            outs = [_dot(_rows(eb, h), _head(vw, h)) * _rows(rinv, h) for h in range(HEADS)]
            y_ref[pl.ds(r0, ATT_ROWS), :] = jnp.concatenate(outs, axis=1).astype(BF16)

        _att_blocks(T, block)

    return _call(
        body, payload, name="att_fwd", grid=None,
        in_specs=[pl.BlockSpec(memory_space=pl.ANY), pl.BlockSpec(memory_space=pltpu.VMEM)],
        out_specs=[pl.BlockSpec(memory_space=pltpu.VMEM)],
        out_shape=[jax.ShapeDtypeStruct((T, D_ATT), BF16)],
        scratch_shapes=[pltpu.VMEM((T, D_ATT), BF16), pltpu.VMEM((T + ATT_PAD, D_ATT), BF16),
                        pltpu.VMEM((T + ATT_PAD, D_ATT), BF16), pltpu.SemaphoreType.DMA((3,))],
        operands=(qkv, bias2))


def _att_bwd(qkv, dy, bias2, payload=None):
    T = qkv.shape[0]

    def body(qkv_hbm, dy_ref, bias_ref, dq_ref, dk_ref, dv_ref, db_ref, q_s, k_s, v_s, sem):
        _att_load(qkv_hbm, q_s, k_s, v_s, sem, T)
        dk_ref[...] = jnp.zeros_like(dk_ref)
        dv_ref[...] = jnp.zeros_like(dv_ref)
        db_ref[...] = jnp.zeros_like(db_ref)


        def block(i, valid):
            r0 = pl.multiple_of(i * ATT_ROWS, ATT_ROWS)
            qs = q_s[pl.ds(r0, ATT_ROWS), :] * (HEAD_DIM ** -0.5)
            kw = k_s[pl.ds(r0, ATT_KEYS), :]
            vw = v_s[pl.ds(r0, ATT_KEYS), :]
            dyb = dy_ref[pl.ds(r0, ATT_ROWS), :]
            e, rinv = _att_exp(qs, kw, bias_ref, valid)
            p = e * rinv
            dp = jnp.concatenate([_dot_nt(_head(dyb, h), _head(vw, h)) for h in range(HEADS)], axis=0)
            ds = p * (dp - jnp.sum(p * dp, axis=-1, keepdims=True))
            db_ref[...] += ds.reshape(HEADS, ATT_ROWS, ATT_KEYS)
            dsb = ds.astype(BF16)
            pb = p.astype(BF16)
            dq = [_dot(_rows(dsb, h), _head(kw, h)) for h in range(HEADS)]
            dk = [_dot_tn(_rows(dsb, h), _head(qs, h)) for h in range(HEADS)]
            dv = [_dot_tn(_rows(pb, h), _head(dyb, h)) for h in range(HEADS)]
            dq_ref[pl.ds(r0, ATT_ROWS), :] = (jnp.concatenate(dq, axis=1) * (HEAD_DIM ** -0.5)).astype(BF16)
            dk_ref[pl.ds(r0, ATT_KEYS), :] += jnp.concatenate(dk, axis=1)
            dv_ref[pl.ds(r0, ATT_KEYS), :] += jnp.concatenate(dv, axis=1)

        _att_blocks(T, block)

    vmem = pl.BlockSpec(memory_space=pltpu.VMEM)
    return _call(
        body, payload, name="att_bwd", grid=None,
        in_specs=[pl.BlockSpec(memory_space=pl.ANY), vmem, vmem],
        out_specs=[vmem, vmem, vmem, vmem],
        out_shape=[jax.ShapeDtypeStruct((T, D_ATT), BF16), jax.ShapeDtypeStruct((T + ATT_PAD, D_ATT), F32),
                   jax.ShapeDtypeStruct((T + ATT_PAD, D_ATT), F32), jax.ShapeDtypeStruct((HEADS, ATT_ROWS, ATT_KEYS), F32)],
        scratch_shapes=[pltpu.VMEM((T, D_ATT), BF16), pltpu.VMEM((T + ATT_PAD, D_ATT), BF16),
                        pltpu.VMEM((T + ATT_PAD, D_ATT), BF16), pltpu.SemaphoreType.DMA((3,))],
        operands=(qkv, dy, bias2))


_GELU_C = 0.7978845608028654
_GELU_A = 0.044715


def _gelu(x):
    t = jnp.tanh(_GELU_C * (x + _GELU_A * x * x * x))
    return 0.5 * x * (1.0 + t), t


def _gelu_grad(x, t):
    return 0.5 * (1.0 + t) + 0.5 * x * (1.0 - t * t) * _GELU_C * (1.0 + 3.0 * _GELU_A * x * x)


def _group_masks():
    col = lax.broadcasted_iota(jnp.int32, (SGU_GROUPS, D_SGU), 1) // SGU_GDIM
    grp = lax.broadcasted_iota(jnp.int32, (SGU_GROUPS, D_SGU), 0)
    return jnp.where(col == grp, 1.0, 0.0).astype(F32)


def _causal_mask(transposed=False):
    i = lax.broadcasted_iota(jnp.int32, (SGU_BLOCK, SGU_BLOCK), 0) // CHUNK
    j = lax.broadcasted_iota(jnp.int32, (SGU_BLOCK, SGU_BLOCK), 1) // CHUNK
    return (j >= i) if transposed else (i >= j)


def _sgu_norm(zs, lng, lnb):
    gz, t = _gelu(zs)
    u = gz[:, 0:D_SGU]
    vs = gz[:, D_SGU:2 * D_SGU]
    xc = vs - jnp.mean(vs, axis=-1, keepdims=True)
    rstd = lax.rsqrt(jnp.mean(xc * xc, axis=-1, keepdims=True) + EPS)
    xhat = xc * rstd
    return t, u, xhat, rstd, xhat * lng + lnb


def _sgu_mix(vn_blk, w_ref, bst, gm):
    mask = _causal_mask()
    s = jnp.zeros((SGU_BLOCK, D_SGU), F32)
    for g in range(SGU_GROUPS):
        wm = jnp.where(mask, w_ref[g], 0.0).astype(BF16)
        s += _dot(wm, (vn_blk * gm[g:g + 1, :]).astype(BF16))
        s += bst[:, g:g + 1] * gm[g:g + 1, :]
    return s


def _sgu_fwd(zs, lng, lnb, w_s, bst):
    T = zs.shape[0]
    nblk = TM // SGU_BLOCK

    def body(zs_ref, lng_ref, lnb_ref, w_ref, bst_ref, y_ref):
        _, u, _, _, vn = _sgu_norm(zs_ref[...], lng_ref[...], lnb_ref[...])
        gm = _group_masks()
        bst_v = bst_ref[...]
        for n in range(nblk):
            rows = slice(n * SGU_BLOCK, (n + 1) * SGU_BLOCK)
            s = _sgu_mix(vn[rows], w_ref, bst_v, gm)
            y_ref[rows, :] = (u[rows] * s).astype(BF16)

    return pl.pallas_call(
        body, name="sgu_fwd", grid=(T // TM,),
        in_specs=[_row_spec(TM, 2 * D_SGU), _const_spec((1, D_SGU)), _const_spec((1, D_SGU)),
                  _const_spec(w_s.shape), _const_spec(bst.shape)],
        out_specs=_row_spec(TM, D_SGU),
        out_shape=jax.ShapeDtypeStruct((T, D_SGU), BF16),
        compiler_params=_params(("arbitrary",)),
    )(zs, lng, lnb, w_s, bst)


def _sgu_bwd(zs, dy, lng, lnb, w_s, w_st, bst):
    T = zs.shape[0]
    nblk = TM // SGU_BLOCK

    def body(zs_ref, dy_ref, lng_ref, lnb_ref, w_ref, wt_ref, bst_ref, dzs_ref, dw_ref, dbt_ref, dlg_ref, dlb_ref):
        @pl.when(pl.program_id(0) == 0)
        def _():
            dw_ref[...] = jnp.zeros_like(dw_ref)
            dbt_ref[...] = jnp.zeros_like(dbt_ref)
            dlg_ref[...] = jnp.zeros_like(dlg_ref)
            dlb_ref[...] = jnp.zeros_like(dlb_ref)

        zs_v = zs_ref[...]
        lng_v = lng_ref[...]
        t, u, xhat, rstd, vn = _sgu_norm(zs_v, lng_v, lnb_ref[...])
        gm = _group_masks()
        bst_v = bst_ref[...]
        mask = _causal_mask()
        mask_t = _causal_mask(transposed=True)
        dyv = dy_ref[...].astype(F32)
        lane8 = lax.broadcasted_iota(jnp.int32, (1, SGU_GROUPS), 1)
        du_rows, dvn_rows = [], []
        for n in range(nblk):
            rows = slice(n * SGU_BLOCK, (n + 1) * SGU_BLOCK)
            vn_b = vn[rows]
            s = _sgu_mix(vn_b, w_ref, bst_v, gm)
            du_rows.append(dyv[rows] * s)
            dsb = dyv[rows] * u[rows]
            vnb16 = vn_b.astype(BF16)
            dvn = jnp.zeros((SGU_BLOCK, D_SGU), F32)
            dbt = jnp.zeros((SGU_BLOCK, SGU_GROUPS), F32)
            for g in range(SGU_GROUPS):
                dsg = dsb * gm[g:g + 1, :]
                dsg16 = dsg.astype(BF16)
                wmt = jnp.where(mask_t, wt_ref[g], 0.0).astype(BF16)
                dvn += _dot(wmt, dsg16)
                dw_ref[g] += jnp.where(mask, _dot_nt(dsg16, vnb16), 0.0)
                dbt += jnp.sum(dsg, axis=-1, keepdims=True) * jnp.where(lane8 == g, 1.0, 0.0)
            dbt_ref[...] += dbt
            dvn_rows.append(dvn)
        du = jnp.concatenate(du_rows, axis=0)
        dvn = jnp.concatenate(dvn_rows, axis=0)
        dlg_ref[...] += jnp.sum(dvn * xhat, axis=0, keepdims=True)
        dlb_ref[...] += jnp.sum(dvn, axis=0, keepdims=True)
        dxhat = dvn * lng_v
        dvs = rstd * (dxhat - jnp.mean(dxhat, axis=-1, keepdims=True)
                      - xhat * jnp.mean(dxhat * xhat, axis=-1, keepdims=True))
        dgz = jnp.concatenate([du, dvs], axis=1)
        dzs_ref[...] = (dgz * _gelu_grad(zs_v, t)).astype(BF16)

    return pl.pallas_call(
        body, name="sgu_bwd", grid=(T // TM,),
        in_specs=[_row_spec(TM, 2 * D_SGU), _row_spec(TM, D_SGU), _const_spec((1, D_SGU)), _const_spec((1, D_SGU)),
                  _const_spec(w_s.shape), _const_spec(w_st.shape), _const_spec(bst.shape)],
        out_specs=[_row_spec(TM, 2 * D_SGU), _acc_spec(w_s.shape), _acc_spec(bst.shape), _acc_spec((1, D_SGU)),
                   _acc_spec((1, D_SGU))],
        out_shape=[jax.ShapeDtypeStruct((T, 2 * D_SGU), BF16), jax.ShapeDtypeStruct(w_s.shape, F32),
                   jax.ShapeDtypeStruct(bst.shape, F32), jax.ShapeDtypeStruct((1, D_SGU), F32),
                   jax.ShapeDtypeStruct((1, D_SGU), F32)],
        compiler_params=_params(("arbitrary",)),
    )(zs, dy, lng, lnb, w_s, w_st, bst)


def _cols(v, s):
    return v[:, s * BR_S:(s + 1) * BR_S]


def _merge_fwd(x, y_att, y_sgu, gl, b_gate, wba, wbs, wo, payload=None):
    T = x.shape[0]

    def body(x_ref, ya_ref, ys_ref, gl_ref, bg_ref, wba_ref, wbs_ref, wo_ref, xo_ref, m_ref, pa_ref, ps_ref):
        ya = ya_ref[...]
        ys = ys_ref[...]
        pa = jnp.concatenate([_dot(ya, wba_ref[s]) for s in range(N_SHARD)], axis=1)
        ps = jnp.concatenate([_dot(ys, wbs_ref[s]) for s in range(N_SHARD)], axis=1)
        g = _sigmoid(gl_ref[...] + bg_ref[...])
        mb = (g[:, 0:D_MODEL] * pa + g[:, D_MODEL:2 * D_MODEL] * ps).astype(BF16)
        m_ref[...] = mb
        pa_ref[...] = pa.astype(BF16)
        ps_ref[...] = ps.astype(BF16)
        acc = jnp.zeros((TM, D_MODEL), F32)
        for s in range(N_SHARD):
            acc += _dot(_cols(mb, s), wo_ref[s])
        xo_ref[...] = x_ref[...] + acc

    tokd = jax.ShapeDtypeStruct((T, D_MODEL), BF16)
    return _call(
        body, payload, name="merge_fwd", grid=(T // TM,), when=_edges(T // TM), sem=("arbitrary",),
        in_specs=[_row_spec(TM, D_MODEL), _row_spec(TM, D_ATT), _row_spec(TM, D_SGU), _row_spec(TM, 2 * D_MODEL),
                  _const_spec((1, 2 * D_MODEL)), _const_spec(wba.shape), _const_spec(wbs.shape), _const_spec(wo.shape)],
        out_specs=[_row_spec(TM, D_MODEL)] * 4,
        out_shape=[jax.ShapeDtypeStruct((T, D_MODEL), F32), tokd, tokd, tokd],
        operands=(x, y_att, y_sgu, gl, b_gate, wba, wbs, wo))


def _merge_bwd(dx, y_att, y_sgu, gl, merged, pa, ps, b_gate, wba, wbs, wo, payload=None):
    T = dx.shape[0]

    def body(dx_ref, ya_ref, ys_ref, gl_ref, m_ref, pa_ref, ps_ref, bg_ref, wba_ref, wbs_ref, wo_ref,
             dya_ref, dys_ref, dgl_ref, dbg_ref, gwba_ref, gwbs_ref, gwo_ref):
        @pl.when(pl.program_id(0) == 0)
        def _():
            dbg_ref[...] = jnp.zeros_like(dbg_ref)
            gwba_ref[...] = jnp.zeros_like(gwba_ref)
            gwbs_ref[...] = jnp.zeros_like(gwbs_ref)
            gwo_ref[...] = jnp.zeros_like(gwo_ref)

        dxb = dx_ref[...].astype(BF16)
        dm = jnp.concatenate([_dot_nt(dxb, wo_ref[s]) for s in range(N_SHARD)], axis=1)
        g = _sigmoid(gl_ref[...] + bg_ref[...])
        ga = g[:, 0:D_MODEL]
        gs = g[:, D_MODEL:2 * D_MODEL]
        dpa = (dm * ga).astype(BF16)
        dps = (dm * gs).astype(BF16)
        dgl = jnp.concatenate([dm * pa_ref[...].astype(F32) * ga * (1.0 - ga),
                               dm * ps_ref[...].astype(F32) * gs * (1.0 - gs)], axis=1)
        dgl_ref[...] = dgl.astype(BF16)
        dbg_ref[...] += jnp.sum(dgl, axis=0, keepdims=True)
        ya = ya_ref[...]
        ys = ys_ref[...]
        mb = m_ref[...]
        dya = jnp.zeros((TM, D_ATT), F32)
        dys = jnp.zeros((TM, D_SGU), F32)
        for s in range(N_SHARD):
            dya += _dot_nt(_cols(dpa, s), wba_ref[s])
            dys += _dot_nt(_cols(dps, s), wbs_ref[s])
            gwo_ref[s] += _dot_tn(_cols(mb, s), dxb)
            gwba_ref[s] += _dot_tn(ya, _cols(dpa, s))
            gwbs_ref[s] += _dot_tn(ys, _cols(dps, s))
        dya_ref[...] = dya.astype(BF16)
        dys_ref[...] = dys.astype(BF16)

    return _call(
        body, payload, name="merge_bwd", grid=(T // TM,), when=_edges(T // TM), sem=("arbitrary",),
        operands=(dx, y_att, y_sgu, gl, merged, pa, ps, b_gate, wba, wbs, wo),
        in_specs=[_row_spec(TM, D_MODEL), _row_spec(TM, D_ATT), _row_spec(TM, D_SGU), _row_spec(TM, 2 * D_MODEL),
                  _row_spec(TM, D_MODEL), _row_spec(TM, D_MODEL), _row_spec(TM, D_MODEL),
                  _const_spec((1, 2 * D_MODEL)), _const_spec(wba.shape), _const_spec(wbs.shape), _const_spec(wo.shape)],
        out_specs=[_row_spec(TM, D_ATT), _row_spec(TM, D_SGU), _row_spec(TM, 2 * D_MODEL), _acc_spec((1, 2 * D_MODEL)),
                   _acc_spec(wba.shape), _acc_spec(wbs.shape), _acc_spec(wo.shape)],
        out_shape=[jax.ShapeDtypeStruct((T, D_ATT), BF16), jax.ShapeDtypeStruct((T, D_SGU), BF16),
                   jax.ShapeDtypeStruct((T, 2 * D_MODEL), BF16), jax.ShapeDtypeStruct((1, 2 * D_MODEL), F32),
                   jax.ShapeDtypeStruct(wba.shape, F32), jax.ShapeDtypeStruct(wbs.shape, F32),
                   jax.ShapeDtypeStruct(wo.shape, F32)])


def _loss_bwd(x, target, g):
    T = x.shape[0]

    def body(x_ref, t_ref, g_ref, dx_ref, loss_ref, dg_ref):
        @pl.when(pl.program_id(0) == 0)
        def _():
            loss_ref[...] = jnp.zeros_like(loss_ref)
            dg_ref[...] = jnp.zeros_like(dg_ref)

        gv = g_ref[...]
        xhat, r, y = _rms_fwd(x_ref[...], gv)
        err = y - t_ref[...]
        per_tok = jnp.mean(err * err, axis=-1, keepdims=True)
        loss_ref[...] += 0.5 * jnp.sum(per_tok, axis=0, keepdims=True)
        dxn, dg = _rms_bwd(err * (1.0 / D_MODEL), xhat, r, gv)
        dx_ref[...] = dxn
        dg_ref[...] += dg

    return pl.pallas_call(
        body, name="loss_bwd", grid=(T // TM,),
        in_specs=[_row_spec(TM, D_MODEL), _row_spec(TM, D_MODEL), _const_spec((1, D_MODEL))],
        out_specs=[_row_spec(TM, D_MODEL), _acc_spec((1, 128)), _acc_spec((1, D_MODEL))],
        out_shape=[jax.ShapeDtypeStruct((T, D_MODEL), F32), jax.ShapeDtypeStruct((1, 128), F32),
                   jax.ShapeDtypeStruct((1, D_MODEL), F32)],
        compiler_params=_params(("arbitrary",)),
    )(x, target, g)


BIG = ("ffn1_w_gate", "ffn1_w_up", "ffn1_w_down", "w_in", "w_branch_att", "w_branch_sgu", "w_out",
       "ffn2_w_gate", "ffn2_w_up", "ffn2_w_down")
SMALL = ("norm_ffn1", "norm_mix", "b_gate", "rel_bias", "sgu_ln_g", "sgu_ln_b", "sgu_w_s", "sgu_b_s", "norm_ffn2",
         "norm_final")


G_FFN1 = ("ffn1_w_gate", "ffn1_w_up", "ffn1_w_down")
G_MIX = ("w_in", "w_branch_att", "w_branch_sgu", "w_out")
G_FFN2 = ("ffn2_w_gate", "ffn2_w_up", "ffn2_w_down")


def _local_step(x, target, wb, ws, dist=None):
    def gather_on(names):
        return _ag_payload([wb[n] for n in names]) if dist else None

    t2 = _relbias_fwd(ws["rel_bias"])
    bias2 = _bias_blocks(t2)
    bst = ws["sgu_b_s"].T
    w_st = jnp.swapaxes(ws["sgu_w_s"], 1, 2)

    x1, h1, a1, b1, *got = _ffn_fwd(x, ws["norm_ffn1"], wb["ffn1_w_gate"], wb["ffn1_w_up"], wb["ffn1_w_down"],
                                    "ffn1_fwd", gather_on(G_MIX))
    wb.update(zip(G_MIX, got))
    h2, qkv, zs, gl, *got = _in_fwd(x1, ws["norm_mix"], wb["w_in"], gather_on(G_FFN2[0:1]))
    wb.update(zip(G_FFN2[0:1], got))
    y_att, *got = _att_fwd(qkv, bias2, gather_on(G_FFN2[1:2]))
    wb.update(zip(G_FFN2[1:2], got))
    y_sgu = _sgu_fwd(zs, ws["sgu_ln_g"], ws["sgu_ln_b"], ws["sgu_w_s"], bst)
    x2, merged, pa, ps, *got = _merge_fwd(x1, y_att, y_sgu, gl, ws["b_gate"], wb["w_branch_att"], wb["w_branch_sgu"],
                                          wb["w_out"], gather_on(G_FFN2[2:3]))
    wb.update(zip(G_FFN2[2:3], got))
    x3, h3, a3, b3 = _ffn_fwd(x2, ws["norm_ffn2"], wb["ffn2_w_gate"], wb["ffn2_w_up"], wb["ffn2_w_down"], "ffn2_fwd")
    dx3, loss, g_final = _loss_bwd(x3, target, ws["norm_final"])

    gb, gs, sums = {}, {"norm_final": g_final}, {}

    def pair_on(names, small=None):
        return _px_payload([gb[n] for n in names], small) if dist else None

    def pair_add(names, halves):
        for n, rv in zip(names, halves):
            sums[n] = _pair_add(gb[n], rv, dist[0], dist[1], "pair_add_" + n)

    def chips_on(names):
        return _cx_payload([sums[n][1] for n in names], [sums[n][2] for n in names]) if dist else None

    dx2, da3, db3, gs["norm_ffn2"] = _ffn_dgrad(dx3, x2, a3, b3, ws["norm_ffn2"], wb["ffn2_w_gate"], wb["ffn2_w_up"],
                                                wb["ffn2_w_down"], "ffn2_dgrad")
    gb["ffn2_w_gate"], gb["ffn2_w_up"], gb["ffn2_w_down"] = _ffn_wgrad(h3, dx3, a3, b3, da3, db3, "ffn2_wgrad")
    dy_att, dy_sgu, dgl, gs["b_gate"], gb["w_branch_att"], gb["w_branch_sgu"], gb["w_out"], *got = _merge_bwd(
        dx2, y_att, y_sgu, gl, merged, pa, ps, ws["b_gate"], wb["w_branch_att"], wb["w_branch_sgu"], wb["w_out"],
        pair_on(G_FFN2))
    pair_add(G_FFN2, got)
    dq, dk, dv, db2, *lands2 = _att_bwd(qkv, dy_att, bias2, chips_on(G_FFN2))
    gs["rel_bias"] = _relbias_bwd(_unskew(db2))
    dzs, gs["sgu_w_s"], dbt, gs["sgu_ln_g"], gs["sgu_ln_b"] = _sgu_bwd(zs, dy_sgu, ws["sgu_ln_g"], ws["sgu_ln_b"],
                                                                      ws["sgu_w_s"], w_st, bst)
    gs["sgu_b_s"] = dbt.T
    dx1, dz, gs["norm_mix"] = _in_dgrad(dx2, x1, ws["norm_mix"], wb["w_in"], dq, dk, dv, dzs, dgl)
    gb["w_in"] = _in_wgrad(h2, dz)
    gx, da1, db1, gs["norm_ffn1"], *got = _ffn_dgrad(dx1, x, a1, b1, ws["norm_ffn1"], wb["ffn1_w_gate"],
                                                    wb["ffn1_w_up"], wb["ffn1_w_down"], "ffn1_dgrad", pair_on(G_MIX))
    pair_add(G_MIX, got)
    gb["ffn1_w_gate"], gb["ffn1_w_up"], gb["ffn1_w_down"], *lands_mix = _ffn_wgrad(h1, dx1, a1, b1, da1, db1,
                                                                                   "ffn1_wgrad", chips_on(G_MIX))
    if not dist:
        return loss, gx, gb, gs

    tail = _tail_reduce([gb[n] for n in G_FFN1], _pack_small(gs, loss))
    for i, n in enumerate(G_FFN1):
        sums[n] = (tail[i],)
    lands1, small_sums = tail[len(G_FFN1):2 * len(G_FFN1)], tail[-1]
    lands = dict(zip(G_FFN2 + G_MIX + G_FFN1, list(lands2) + list(lands_mix) + list(lands1)))
    fulls = [_final_sum(sums[n][0], lands[n], dist[1], dist[0], "final_sum_" + n) for n in BIG]
    return loss, gx, dict(zip(BIG, _sibling_share(fulls, "sibling_share"))), small_sums


_ANY = pl.BlockSpec(memory_space=pl.ANY)
_VMEM = pl.BlockSpec(memory_space=pltpu.VMEM)


def _mesh_pos():
    return lax.axis_index("x"), lax.axis_index("y"), lax.axis_index("c")


def _cast_slots(shards, chip, name):
    n = len(shards)
    r, ncol = shards[0].shape
    tr = r // 2

    def body(me_ref, *refs):
        for i_ref, o_ref in zip(refs[:n], refs[n:]):
            o_ref[0] = i_ref[...].astype(BF16)

    grid_spec = pltpu.PrefetchScalarGridSpec(
        num_scalar_prefetch=1, grid=(r // tr,),
        in_specs=[pl.BlockSpec((tr, ncol), lambda i, me: (i, 0))] * n,
        out_specs=[pl.BlockSpec((1, tr, ncol), lambda i, me: (me[0], i, 0))] * n)
    return pl.pallas_call(
        body, name=name, grid_spec=grid_spec,
        out_shape=[jax.ShapeDtypeStruct((N_SHARD, r, ncol), BF16)] * n,
        compiler_params=_params(("arbitrary",)),
    )(chip, *shards)


class _Payload:
    def __init__(self, arrays, out_shapes, aliases, scratch, phases):
        self.arrays = list(arrays)
        self.out_shapes = list(out_shapes)
        self.aliases = dict(aliases)
        self.scratch = list(scratch)
        self.phases = phases


def _remote(src, dst, ssem, rsem, dev):
    return pltpu.make_async_remote_copy(src_ref=src, dst_ref=dst, send_sem=ssem, recv_sem=rsem, device_id=dev,
                                        device_id_type=MESH)


def _call(body, payload, *, name, grid, in_specs, out_specs, out_shape, scratch_shapes=(), sem=None, when=None,
          operands=()):
    in_specs, out_specs, out_shape = list(in_specs), list(out_specs), list(out_shape)
    scratch_shapes = list(scratch_shapes)
    n_in, n_out, n_scr = len(in_specs), len(out_specs), len(scratch_shapes)
    kwargs = {}
    kernel = body
    if payload is not None:
        k_in, k_out = len(payload.arrays), len(payload.out_shapes)
        rank = len(grid) if grid else 0

        def kernel(*refs):
            a, b = n_in, n_in + k_in
            c, d = b + n_out, b + n_out + k_out
            e = d + n_scr
            phases = payload.phases(refs[a:b], refs[c:d], refs[e:])

            def run():
                body(*refs[:a], *refs[b:c], *refs[d:e])

            if not grid:
                phases[0]()
                run()
                for ph in phases[1:]:
                    ph()
                return
            step = pl.program_id(0)
            if rank == 2:
                step = step * grid[1] + pl.program_id(1)
            for ph, (at, before) in zip(phases, when):
                if before:
                    pl.when(step == at)(ph)
            run()
            for ph, (at, before) in zip(phases, when):
                if not before:
                    pl.when(step == at)(ph)

        in_specs += [_ANY] * k_in
        out_specs += [_ANY] * k_out
        out_shape += payload.out_shapes
        scratch_shapes += payload.scratch
        kwargs["input_output_aliases"] = {n_in + i: n_out + j for i, j in payload.aliases.items()}
        operands = tuple(operands) + tuple(payload.arrays)
    if grid:
        kwargs["grid"] = grid
    return pl.pallas_call(kernel, name=name, in_specs=in_specs, out_specs=out_specs, out_shape=out_shape,
                          scratch_shapes=scratch_shapes, compiler_params=_params(sem), **kwargs)(*operands)


def _exchange(payload, name):
    return _call(lambda: None, payload, name=name, grid=None, in_specs=[], out_specs=[], out_shape=[])


def _ag_payload(slots):
    n = len(slots)

    def phases(_, refs, sems):
        send_i, recv_i, send_d, recv_d = sems
        x, y, c = _mesh_pos()
        me = 2 * x + y

        def half(w, core):
            rh = slots[w].shape[1] // 2
            return pl.ds(core * rh, rh)

        def ici(w, j):
            t = (me + 1 + j) % N_SHARD
            mine = refs[w].at[me, half(w, c), :]
            return _remote(mine, mine, send_i.at[3 * w + j], recv_i.at[3 * w + j], (t // 2, t % 2, c))

        def d2d(w, j, core):
            s = (me + 3 - j) % N_SHARD
            land = refs[w].at[s, half(w, core), :]
            return _remote(land, land, send_d.at[3 * w + j], recv_d.at[3 * w + j], (x, y, 1 - c))

        def start():
            for w in range(n):
                for j in range(3):
                    ici(w, j).start()

        def finish():
            for w in range(n):
                for j in range(3):
                    s = (me + 3 - j) % N_SHARD
                    land = refs[w].at[s, half(w, c), :]
                    _remote(land, land, send_i.at[3 * w + j], recv_i.at[3 * w + j], (x, y, c)).wait_recv()
                    d2d(w, j, c).start()
            for w in range(n):
                for j in range(3):
                    d2d(w, j, 1 - c).wait_recv()
            for w in range(n):
                for j in range(3):
                    ici(w, j).wait_send()
                    d2d(w, j, c).wait_send()

        return [start, finish]

    return _Payload(slots, [jax.ShapeDtypeStruct(s.shape, s.dtype) for s in slots], {i: i for i in range(n)},
                    [pltpu.SemaphoreType.DMA((3 * n,)) for _ in range(4)], phases)


def _px_payload(grads, small=None):
    arrays = list(grads) + ([small] if small is not None else [])
    n = len(arrays)

    def phases(ins, outs, sems):
        send, recv = sems
        x, y, c = _mesh_pos()

        def copy(w):
            if w < len(grads):
                rh = grads[w].shape[1] // 2
                src = ins[w].at[:, pl.ds((1 - c) * rh, rh), :]
            else:
                src = ins[w]
            return _remote(src, outs[w], send.at[w], recv.at[w], (x, y, 1 - c))

        def start():
            for w in range(n):
                copy(w).start()

        def finish():
            for w in range(n):
                copy(w).wait()

        return [start, finish]

    out_shapes = [jax.ShapeDtypeStruct((N_SHARD, g.shape[1] // 2, g.shape[2]), F32) for g in grads]
    if small is not None:
        out_shapes.append(jax.ShapeDtypeStruct(small.shape, F32))
    return _Payload(arrays, out_shapes, {}, [pltpu.SemaphoreType.DMA((n,)), pltpu.SemaphoreType.DMA((n,))], phases)


def _cx_payload(pbs, lands):
    n = len(pbs)

    def phases(ins, outs, sems):
        send, recv = sems
        x, y, c = _mesh_pos()
        me = 2 * x + y

        def copy(w, j):
            t = (me + 1 + j) % N_SHARD
            return _remote(ins[w].at[t], outs[w].at[me], send.at[3 * w + j], recv.at[3 * w + j], (t // 2, t % 2, c))

        def start():
            for w in range(n):
                for j in range(3):
                    copy(w, j).start()

        def finish():
            for w in range(n):
                for j in range(3):
                    copy(w, j).wait()

        return [start, finish]

    return _Payload(list(pbs) + list(lands), [jax.ShapeDtypeStruct(p.shape, BF16) for p in lands],
                    {n + i: i for i in range(n)},
                    [pltpu.SemaphoreType.DMA((3 * n,)), pltpu.SemaphoreType.DMA((3 * n,))], phases)


def _pair_add(g, rv, core, chip, name):
    _, r, ncol = g.shape
    rh = r // 2

    def body(c_ref, me_ref, g_ref, rv_ref, pf_ref, pb_ref, land_ref):
        s = g_ref[0] + rv_ref[0]
        sb = s.astype(BF16)
        pb_ref[0] = sb
        land_ref[0] = sb

        @pl.when(pl.program_id(0) == me_ref[0])
        def _():
            pf_ref[...] = s

    slot = pl.BlockSpec((1, rh, ncol), lambda s, c, me: (s, 0, 0))
    grid_spec = pltpu.PrefetchScalarGridSpec(
        num_scalar_prefetch=2, grid=(N_SHARD,),
        in_specs=[pl.BlockSpec((1, rh, ncol), lambda s, c, me: (s, c[0], 0)), slot],
        out_specs=[pl.BlockSpec((rh, ncol), lambda s, c, me: (0, 0)), slot, slot])
    return pl.pallas_call(
        body, name=name, grid_spec=grid_spec,
        out_shape=[jax.ShapeDtypeStruct((rh, ncol), F32), jax.ShapeDtypeStruct((N_SHARD, rh, ncol), BF16),
                   jax.ShapeDtypeStruct((N_SHARD, rh, ncol), BF16)],
        compiler_params=_params(("arbitrary",)),
    )(core, chip, g, rv)


def _small_exchange(small, small_sib, payload=None):
    def body(sm, sm_sib, sm_out, sm_sum, send, recv, loc):
        x, y, c = _mesh_pos()
        me = 2 * x + y
        sm_sum[...] = sm[...] + sm_sib[...]
        local = pltpu.make_async_copy(sm_sum, sm_out.at[me], loc)
        local.start()
        cps = []
        for j in range(3):
            t = (me + 1 + j) % N_SHARD
            cp = pltpu.make_async_remote_copy(src_ref=sm_sum, dst_ref=sm_out.at[me], send_sem=send.at[j],
                                              recv_sem=recv.at[j], device_id=(t // 2, t % 2, c), device_id_type=MESH)
            cp.start()
            cps.append(cp)
        for cp in cps:
            cp.wait()
        local.wait()

    return _call(
        body, payload, name="small_exchange", grid=None,
        in_specs=[_VMEM, _VMEM], out_specs=[_ANY],
        out_shape=[jax.ShapeDtypeStruct((N_SHARD,) + small.shape, F32)],
        scratch_shapes=[pltpu.VMEM(small.shape, F32), pltpu.SemaphoreType.DMA((3,)), pltpu.SemaphoreType.DMA((3,)),
                        pltpu.SemaphoreType.DMA],
        operands=(small, small_sib))


def _tail_reduce(grads, small):
    n = len(grads)
    _, r, ncol = grads[0].shape
    rh = r // 2

    def body(*refs):
        g_hbm, sm = refs[:n], refs[n]
        pf, land, sm_out = refs[n + 1:2 * n + 1], refs[2 * n + 1:3 * n + 1], refs[3 * n + 1]
        scr = refs[3 * n + 2:]
        rv, mine, sendb = scr[:n], scr[n:2 * n], scr[2 * n:3 * n]
        sm_rv, sm_sum, d_send, d_recv, load, i_send, i_recv, store = scr[3 * n:]
        x, y, c = _mesh_pos()
        me = 2 * x + y
        sib = (x, y, 1 - c)

        def pair(w):
            src = g_hbm[w].at[:, pl.ds((1 - c) * rh, rh), :] if w < n else sm
            return _remote(src, rv[w] if w < n else sm_rv, d_send.at[w], d_recv.at[w], sib)

        def chips(w, j):
            t = (me + 1 + j) % N_SHARD
            src = sendb[w].at[t] if w < n else sm_sum
            dst = land[w].at[me] if w < n else sm_out.at[me]
            return _remote(src, dst, i_send.at[3 * w + j], i_recv.at[3 * w + j], (t // 2, t % 2, c))

        loads = [pltpu.make_async_copy(g_hbm[w].at[:, pl.ds(c * rh, rh), :], mine[w], load.at[w]) for w in range(n)]
        for w in range(n + 1):
            pair(w).start()
        for cp in loads:
            cp.start()
        stores = []
        for w in range(n):
            loads[w].wait()
            pair(w).wait_recv()
            for k in range(N_SHARD):
                s = mine[w][k] + rv[w][k]
                mine[w][k] = s
                sendb[w][k] = s.astype(BF16)
            stores += [pltpu.make_async_copy(mine[w].at[me], pf[w], store.at[2 * w]),
                       pltpu.make_async_copy(sendb[w].at[me], land[w].at[me], store.at[2 * w + 1])]
            for cp in stores[-2:]:
                cp.start()
            for j in range(3):
                chips(w, j).start()
        pair(n).wait_recv()
        sm_sum[...] = sm[...] + sm_rv[...]
        stores.append(pltpu.make_async_copy(sm_sum, sm_out.at[me], store.at[2 * n]))
        stores[-1].start()
        for j in range(3):
            chips(n, j).start()
        for w in range(n + 1):
            pair(w).wait_send()
            for j in range(3):
                chips(w, j).wait()
        for cp in stores:
            cp.wait()

    half = (N_SHARD, rh, ncol)
    return _call(
        body, None, name="tail_reduce", grid=None,
        in_specs=[_ANY] * n + [_VMEM], out_specs=[_ANY] * (2 * n + 1),
        out_shape=([jax.ShapeDtypeStruct((rh, ncol), F32)] * n + [jax.ShapeDtypeStruct(half, BF16)] * n
                   + [jax.ShapeDtypeStruct((N_SHARD,) + small.shape, F32)]),
        scratch_shapes=([pltpu.VMEM(half, F32)] * (2 * n) + [pltpu.VMEM(half, BF16)] * n
                        + [pltpu.VMEM(small.shape, F32), pltpu.VMEM(small.shape, F32),
                           pltpu.SemaphoreType.DMA((n + 1,)), pltpu.SemaphoreType.DMA((n + 1,)),
                           pltpu.SemaphoreType.DMA((n,)), pltpu.SemaphoreType.DMA((3 * n + 3,)),
                           pltpu.SemaphoreType.DMA((3 * n + 3,)), pltpu.SemaphoreType.DMA((2 * n + 1,))]),
        operands=(*grads, small))


def _final_sum(pf, land, chip, core, name):
    _, rh, ncol = land.shape

    def body(me_ref, c_ref, pf_ref, land_ref, o_ref):
        me = me_ref[0]
        acc = jnp.zeros((rh, ncol), F32)
        for k in range(N_SHARD):
            acc = acc + jnp.where(me == k, pf_ref[...], land_ref[k].astype(F32))
        o_ref[...] = acc

    grid_spec = pltpu.PrefetchScalarGridSpec(
        num_scalar_prefetch=2, grid=(1,),
        in_specs=[pl.BlockSpec((rh, ncol), lambda i, me, c: (0, 0)),
                  pl.BlockSpec((N_SHARD, rh, ncol), lambda i, me, c: (0, 0, 0))],
        out_specs=pl.BlockSpec((rh, ncol), lambda i, me, c: (c[0], 0)))
    return pl.pallas_call(
        body, name=name, grid_spec=grid_spec, out_shape=jax.ShapeDtypeStruct((2 * rh, ncol), F32),
        compiler_params=_params(("arbitrary",)),
    )(chip, core, pf, land)


def _sibling_share(fulls, name):
    n = len(fulls)

    def body(*refs):
        outs = refs[n:2 * n]
        send, recv = refs[2 * n:]
        x, y, c = _mesh_pos()
        cps = []
        for w in range(n):
            rh = fulls[w].shape[0] // 2
            mine = outs[w].at[pl.ds(c * rh, rh), :]
            cp = pltpu.make_async_remote_copy(src_ref=mine, dst_ref=mine, send_sem=send.at[w], recv_sem=recv.at[w],
                                              device_id=(x, y, 1 - c), device_id_type=MESH)
            cp.start()
            cps.append(cp)
        for cp in cps:
            cp.wait()

    return pl.pallas_call(
        body, name=name,
        in_specs=[_ANY] * n, out_specs=[_ANY] * n,
        out_shape=[jax.ShapeDtypeStruct(f.shape, F32) for f in fulls],
        input_output_aliases={i: i for i in range(n)},
        scratch_shapes=[pltpu.SemaphoreType.DMA((n,)), pltpu.SemaphoreType.DMA((n,))],
    )(*fulls)


_ROW = {"rel_bias": 128, "sgu_b_s": 136, "norm_ffn1": 144, "norm_mix": 145, "norm_ffn2": 146, "norm_final": 147,
        "b_gate": 148, "sgu_ln_g": 150, "sgu_ln_b": 151}


def _pack_small(gs, loss):
    def body(ws, rel, bs, n1, nm, n2, nf, bg, lg, lb, loss_ref, o_ref):
        o_ref[...] = jnp.zeros_like(o_ref)
        o_ref[LOSS_ROW:LOSS_ROW + 1, 0:128] = loss_ref[...]
        for g in range(SGU_GROUPS):
            o_ref[0:SGU_BLOCK, g * SGU_BLOCK:(g + 1) * SGU_BLOCK] = ws[g]
        o_ref[128:136, 0:REL_PAD] = rel[...]
        o_ref[136:144, 0:SGU_BLOCK] = bs[...]
        o_ref[144:145, :] = n1[...]
        o_ref[145:146, :] = nm[...]
        o_ref[146:147, :] = n2[...]
        o_ref[147:148, :] = nf[...]
        o_ref[148:149, :] = bg[:, 0:D_MODEL]
        o_ref[149:150, :] = bg[:, D_MODEL:2 * D_MODEL]
        o_ref[150:151, 0:D_SGU] = lg[...]
        o_ref[151:152, 0:D_SGU] = lb[...]

    order = ("sgu_w_s", "rel_bias", "sgu_b_s", "norm_ffn1", "norm_mix", "norm_ffn2", "norm_final", "b_gate", "sgu_ln_g",
             "sgu_ln_b")
    return pl.pallas_call(body, name="pack_small", out_shape=jax.ShapeDtypeStruct((SMALL_ROWS, D_MODEL), F32))(
        *[gs[k] for k in order], loss)


def _adam(w, g, m, v):
    m2 = ADAM_B1 * m + (1.0 - ADAM_B1) * g
    v2 = ADAM_B2 * v + (1.0 - ADAM_B2) * (g * g)
    m_hat = m2 / (1.0 - ADAM_B1 ** ADAM_STEP)
    v_hat = v2 / (1.0 - ADAM_B2 ** ADAM_STEP)
    delta = -ADAM_LR * (m_hat / (jnp.sqrt(v_hat) + ADAM_EPS) + ADAM_WD * w)
    return delta, m2, v2


def _adam_small(sin, w, m, v):
    names = SMALL
    k = len(names)

    def body(*refs):
        sin_ref = refs[0]
        w_r, m_r, v_r = refs[1:1 + k], refs[1 + k:1 + 2 * k], refs[1 + 2 * k:1 + 3 * k]
        outs = refs[1 + 3 * k:]
        tot = sin_ref[0] + sin_ref[1] + sin_ref[2] + sin_ref[3]
        outs[4 * k][...] = tot[LOSS_ROW:LOSS_ROW + 1, 0:128]
        for i, name in enumerate(names):
            o = outs[4 * i:4 * i + 4]
            if name == "sgu_w_s":
                for gi in range(SGU_GROUPS):
                    g = tot[0:SGU_BLOCK, gi * SGU_BLOCK:(gi + 1) * SGU_BLOCK]
                    res = (g,) + _adam(w_r[i][gi], g, m_r[i][gi], v_r[i][gi])
                    for ref, val in zip(o, res):
                        ref[gi] = val
                continue
            r0 = _ROW[name]
            if name == "rel_bias":
                g = tot[r0:r0 + HEADS, 0:REL_PAD]
            elif name == "sgu_b_s":
                g = tot[r0:r0 + SGU_GROUPS, 0:SGU_BLOCK]
            elif name == "b_gate":
                g = jnp.concatenate([tot[r0:r0 + 1, :], tot[r0 + 1:r0 + 2, :]], axis=1)
            elif name in ("sgu_ln_g", "sgu_ln_b"):
                g = tot[r0:r0 + 1, 0:D_SGU]
            else:
                g = tot[r0:r0 + 1, :]
            res = (g,) + _adam(w_r[i][...], g, m_r[i][...], v_r[i][...])
            for ref, val in zip(o, res):
                ref[...] = val

    out_shape = []
    for name in names:
        out_shape += [jax.ShapeDtypeStruct(w[name].shape, F32)] * 4
    out_shape.append(jax.ShapeDtypeStruct((1, 128), F32))
    flat = pl.pallas_call(body, name="adam_small", out_shape=out_shape, compiler_params=_params())(
        sin, *[w[n] for n in names], *[m[n] for n in names], *[v[n] for n in names])
    return {name: tuple(flat[4 * i:4 * i + 4]) for i, name in enumerate(names)}, flat[4 * k]


def _adam_big(w, g, m, v, name):
    r, ncol = w.shape
    tr = 256 if r % 256 == 0 else r // 2

    def body(w_ref, g_ref, m_ref, v_ref, d_ref, m2_ref, v2_ref):
        d_ref[...], m2_ref[...], v2_ref[...] = _adam(w_ref[...], g_ref[...], m_ref[...], v_ref[...])

    spec = pl.BlockSpec((tr, ncol), lambda i: (i, 0))
    return pl.pallas_call(
        body, name=name, grid=(r // tr,), in_specs=[spec] * 4, out_specs=[spec] * 3,
        out_shape=[jax.ShapeDtypeStruct(w.shape, F32)] * 3, compiler_params=_params(("arbitrary",)),
    )(w, g, m, v)


WEIGHTS = ("norm_ffn1", "ffn1_w_gate", "ffn1_w_up", "ffn1_w_down", "norm_mix", "w_in", "b_gate", "rel_bias", "sgu_ln_g",
           "sgu_ln_b", "sgu_w_s", "sgu_b_s", "w_branch_att", "w_branch_sgu", "w_out", "norm_ffn2", "ffn2_w_gate",
           "ffn2_w_up", "ffn2_w_down", "norm_final")


GATE_UP = ("ffn1_w_gate", "ffn1_w_up", "ffn2_w_gate", "ffn2_w_up")
_FFN = ("ffn1_w_gate", "ffn1_w_up", "ffn1_w_down", "ffn2_w_gate", "ffn2_w_up", "ffn2_w_down")
_CAST_GROUPS = ((_FFN, "cast_ffn"), (("w_in",), "cast_w_in"), (("w_branch_att", "w_branch_sgu"), "cast_branch"),
                (("w_out",), "cast_w_out"))


def _big_form(name, a):
    return jnp.swapaxes(a, 1, 2)[0] if name in GATE_UP else a[0]


def _big_back(name, a):
    return jnp.swapaxes(a[None], 1, 2) if name in GATE_UP else a[None]


def _small_form(name, a):
    if name == "norm_final":
        return a.reshape(1, D_MODEL)
    if name == "rel_bias":
        return jnp.pad(a[0], ((0, 0), (0, REL_PAD - N_REL)))
    if name in ("sgu_w_s", "sgu_b_s"):
        return a[0]
    return a


def _small_back(name, a, like):
    if name == "rel_bias":
        a = a[:, :N_REL]
    return a.reshape(like.shape)


def kernel(x, norm_ffn1, ffn1_w_gate, ffn1_w_up, ffn1_w_down, norm_mix, w_in, b_gate, rel_bias, sgu_ln_g, sgu_ln_b, sgu_w_s, sgu_b_s, w_branch_att, w_branch_sgu, w_out, norm_ffn2, ffn2_w_gate, ffn2_w_up, ffn2_w_down, norm_final, loss_target, m_norm_ffn1, m_ffn1_w_gate, m_ffn1_w_up, m_ffn1_w_down, m_norm_mix, m_w_in, m_b_gate, m_rel_bias, m_sgu_ln_g, m_sgu_ln_b, m_sgu_w_s, m_sgu_b_s, m_w_branch_att, m_w_branch_sgu, m_w_out, m_norm_ffn2, m_ffn2_w_gate, m_ffn2_w_up, m_ffn2_w_down, m_norm_final, v_norm_ffn1, v_ffn1_w_gate, v_ffn1_w_up, v_ffn1_w_down, v_norm_mix, v_w_in, v_b_gate, v_rel_bias, v_sgu_ln_g, v_sgu_ln_b, v_sgu_w_s, v_sgu_b_s, v_w_branch_att, v_w_branch_sgu, v_w_out, v_norm_ffn2, v_ffn2_w_gate, v_ffn2_w_up, v_ffn2_w_down, v_norm_final):
    w = dict(norm_ffn1=norm_ffn1, ffn1_w_gate=ffn1_w_gate, ffn1_w_up=ffn1_w_up, ffn1_w_down=ffn1_w_down, norm_mix=norm_mix,
             w_in=w_in, b_gate=b_gate, rel_bias=rel_bias, sgu_ln_g=sgu_ln_g, sgu_ln_b=sgu_ln_b, sgu_w_s=sgu_w_s,
             sgu_b_s=sgu_b_s, w_branch_att=w_branch_att, w_branch_sgu=w_branch_sgu, w_out=w_out, norm_ffn2=norm_ffn2,
             ffn2_w_gate=ffn2_w_gate, ffn2_w_up=ffn2_w_up, ffn2_w_down=ffn2_w_down, norm_final=norm_final)
    m = dict(norm_ffn1=m_norm_ffn1, ffn1_w_gate=m_ffn1_w_gate, ffn1_w_up=m_ffn1_w_up, ffn1_w_down=m_ffn1_w_down,
             norm_mix=m_norm_mix, w_in=m_w_in, b_gate=m_b_gate, rel_bias=m_rel_bias, sgu_ln_g=m_sgu_ln_g,
             sgu_ln_b=m_sgu_ln_b, sgu_w_s=m_sgu_w_s, sgu_b_s=m_sgu_b_s, w_branch_att=m_w_branch_att,
             w_branch_sgu=m_w_branch_sgu, w_out=m_w_out, norm_ffn2=m_norm_ffn2, ffn2_w_gate=m_ffn2_w_gate,
             ffn2_w_up=m_ffn2_w_up, ffn2_w_down=m_ffn2_w_down, norm_final=m_norm_final)
    v = dict(norm_ffn1=v_norm_ffn1, ffn1_w_gate=v_ffn1_w_gate, ffn1_w_up=v_ffn1_w_up, ffn1_w_down=v_ffn1_w_down,
             norm_mix=v_norm_mix, w_in=v_w_in, b_gate=v_b_gate, rel_bias=v_rel_bias, sgu_ln_g=v_sgu_ln_g,
             sgu_ln_b=v_sgu_ln_b, sgu_w_s=v_sgu_w_s, sgu_b_s=v_sgu_b_s, w_branch_att=v_w_branch_att,
             w_branch_sgu=v_w_branch_sgu, w_out=v_w_out, norm_ffn2=v_norm_ffn2, ffn2_w_gate=v_ffn2_w_gate,
             ffn2_w_up=v_ffn2_w_up, ffn2_w_down=v_ffn2_w_down, norm_final=v_norm_final)

    core = lax.axis_index("c").astype(jnp.int32).reshape(1)
    chip = (2 * lax.axis_index("x") + lax.axis_index("y")).astype(jnp.int32).reshape(1)

    wk = {n: _big_form(n, w[n]) for n in BIG}
    slots = {}
    for names, call in _CAST_GROUPS:
        slots.update(zip(names, _cast_slots([wk[n] for n in names], chip, call)))
    slots.update(zip(G_FFN1, _exchange(_ag_payload([slots[n] for n in G_FFN1]), "allgather_ffn1")))
    ws = {n: _small_form(n, w[n]) for n in SMALL}
    _, gx, shard_grads, small_sums = _local_step(x[0], loss_target[0], slots, ws, (core, chip))

    small, loss = _adam_small(small_sums, ws, {n: _small_form(n, m[n]) for n in SMALL},
                              {n: _small_form(n, v[n]) for n in SMALL})
    grad, delta, new_m, new_v = {}, {}, {}, {}
    for n in SMALL:
        grad[n], delta[n], new_m[n], new_v[n] = (_small_back(n, a, w[n]) for a in small[n])
    for n in BIG:
        g2 = shard_grads[n]
        d2, m2, v2 = _adam_big(wk[n], g2, _big_form(n, m[n]), _big_form(n, v[n]), "adam_" + n)
        grad[n], delta[n], new_m[n], new_v[n] = (_big_back(n, a) for a in (g2, d2, m2, v2))

    return (loss[0, 0], gx.reshape(x.shape), *[grad[n] for n in WEIGHTS], *[delta[n] for n in WEIGHTS],
            *[new_m[n] for n in WEIGHTS], *[new_v[n] for n in WEIGHTS])
```

```python
import functools

import jax
import jax.numpy as jnp
from jax import lax
from jax.experimental import pallas as pl
from jax.experimental.pallas import tpu as pltpu

F32 = jnp.float32
BF16 = jnp.bfloat16

D_MODEL = 1024
N_SHARD = 4
D_FF = 2816
FF_S = D_FF // N_SHARD
D_ATT = 512
D_SGU = 512
D_IN = 3 * D_ATT + 2 * D_SGU + 2 * D_MODEL
IN_S = D_IN // N_SHARD
BR_S = D_MODEL // N_SHARD
HEADS = 8
HEAD_DIM = 64
CHUNK = 64
N_LEFT = 8
BAND = (N_LEFT + 1) * CHUNK
REL_CLIP = 256
N_REL = 2 * REL_CLIP + 1
REL_PAD = 640
SGU_BLOCK = 128
SGU_GROUPS = 8
SGU_GDIM = 64
EPS = 1e-6
NEG_INF = -1e30

ATT_ROWS = 2 * CHUNK
ATT_KEYS = BAND + CHUNK
ATT_PAD = N_LEFT * CHUNK

ADAM_LR = 0.001
ADAM_B1 = 0.9
ADAM_B2 = 0.999
ADAM_EPS = 1e-08
ADAM_WD = 0.01
ADAM_STEP = 10

TM = 256
TW = 512
VMEM_LIMIT = 56 * 1024 * 1024

SMALL_ROWS = 160
LOSS_ROW = 152
MESH = pl.DeviceIdType.MESH

_NT = (((1,), (1,)), ((), ()))
_TN = (((0,), (0,)), ((), ()))


def _params(sem=None):
    return pltpu.CompilerParams(dimension_semantics=sem, vmem_limit_bytes=VMEM_LIMIT)


def _const_spec(shape):
    nd = len(shape)
    return pl.BlockSpec(shape, lambda *_: (0,) * nd, pipeline_mode=pl.Buffered(1))


def _acc_spec(shape):
    nd = len(shape)
    return pl.BlockSpec(shape, lambda *_: (0,) * nd)


def _row_spec(tm, ncols, off=0):
    return pl.BlockSpec((tm, ncols), lambda i: (i + off, 0))


def _row3_spec(tm, ncols):
    return pl.BlockSpec((N_SHARD, tm, ncols), lambda i: (0, i, 0))


def _dot(a, b):
    return jnp.dot(a, b, preferred_element_type=F32)


def _dot_nt(a, b):
    return lax.dot_general(a, b, _NT, preferred_element_type=F32)


def _dot_tn(a, b):
    return lax.dot_general(a, b, _TN, preferred_element_type=F32)


def _rms_fwd(x, g):
    r = lax.rsqrt(jnp.mean(x * x, axis=-1, keepdims=True) + EPS)
    xhat = x * r
    return xhat, r, xhat * g


def _rms_bwd(dh, xhat, r, g):
    dxhat = dh * g
    dx = r * (dxhat - xhat * jnp.mean(dxhat * xhat, axis=-1, keepdims=True))
    dg = jnp.sum(dh * xhat, axis=0, keepdims=True)
    return dx, dg


def _sigmoid(x):
    return 1.0 / (1.0 + jnp.exp(-x))


def _edges(n_steps):
    return [(0, True), (n_steps - 1, False)]


def _ffn_fwd(x, g, wg, wu, wd, name, payload=None):
    T = x.shape[0]

    def body(x_ref, g_ref, wg_ref, wu_ref, wd_ref, xo_ref, h_ref, a_ref, b_ref):
        xv = x_ref[...]
        hb = _rms_fwd(xv, g_ref[...])[2].astype(BF16)
        h_ref[...] = hb
        acc = jnp.zeros((TM, D_MODEL), F32)
        for s in range(N_SHARD):
            a = _dot_nt(hb, wg_ref[s])
            b = _dot_nt(hb, wu_ref[s])
            a_ref[s] = a.astype(BF16)
            b_ref[s] = b.astype(BF16)
            sv = a * _sigmoid(a) * b
            acc += _dot(sv.astype(BF16), wd_ref[s])
        xo_ref[...] = xv + 0.5 * acc

    return _call(
        body, payload, name=name, grid=(T // TM,), when=_edges(T // TM), sem=("arbitrary",),
        in_specs=[_row_spec(TM, D_MODEL), _const_spec((1, D_MODEL)), _const_spec(wg.shape), _const_spec(wu.shape),
                  _const_spec(wd.shape)],
        out_specs=[_row_spec(TM, D_MODEL), _row_spec(TM, D_MODEL), _row3_spec(TM, FF_S), _row3_spec(TM, FF_S)],
        out_shape=[jax.ShapeDtypeStruct((T, D_MODEL), F32), jax.ShapeDtypeStruct((T, D_MODEL), BF16),
                   jax.ShapeDtypeStruct((N_SHARD, T, FF_S), BF16), jax.ShapeDtypeStruct((N_SHARD, T, FF_S), BF16)],
        operands=(x, g, wg, wu, wd))


TG = 512


def _ffn_fwd_gather(x, g, slots, chip, payload=None):
    T = x.shape[0]
    nt = T // TG
    rh = FF_S // 2

    k_in = len(payload.arrays) if payload else 0
    k_out = len(payload.out_shapes) if payload else 0

    def body(me_ref, x_ref, g_ref, *rest):
        p_in, rest = rest[3:3 + k_in], rest[3 + k_in:]
        xo_ref, h_ref, a_ref, b_ref, wg_hbm, wu_hbm, wd_hbm = rest[:7]
        p_out, rest = rest[7:7 + k_out], rest[7 + k_out:]
        acc, wbuf, send_i, recv_i, send_d, recv_d, wsem = rest[:7]
        phases = payload.phases(p_in, p_out, rest[7:]) if payload else None
        k, i = pl.program_id(0), pl.program_id(1)
        mx, my, c = _mesh_pos()
        me = 2 * mx + my
        w_hbm = (wg_hbm, wu_hbm, wd_hbm)

        def ici(j, w):
            t = me ^ j
            mine = w_hbm[w].at[me, pl.ds(c * rh, rh), :]
            return _remote(mine, mine, send_i.at[3 * (j - 1) + w], recv_i.at[3 * (j - 1) + w], (t // 2, t % 2, c))

        def d2d(j, w, core):
            land = w_hbm[w].at[me ^ j, pl.ds(core * rh, rh), :]
            return _remote(land, land, send_d.at[3 * (j - 1) + w], recv_d.at[3 * (j - 1) + w], (mx, my, 1 - c))

        def load(s):
            copies = [pltpu.make_async_copy(w_hbm[w].at[s], wbuf.at[w], wsem.at[w]) for w in range(3)]
            for cp in copies:
                cp.start()
            for cp in copies:
                cp.wait()

        @pl.when((k == 0) & (i == 0))
        def _():
            for j in (1, 2, 3):
                for w in range(3):
                    ici(j, w).start()
            if payload:
                phases[0]()
            load(me)

        for j in (1, 2, 3):
            @pl.when((k == j) & (i == 0))
            def _(j=j):
                for w in range(3):
                    land = w_hbm[w].at[me ^ j, pl.ds(c * rh, rh), :]
                    _remote(land, land, send_i.at[3 * (j - 1) + w], recv_i.at[3 * (j - 1) + w], (mx, my, c)).wait_recv()
                    d2d(j, w, c).start()
                for w in range(3):
                    d2d(j, w, 1 - c).wait_recv()
                load(me ^ j)

        xv = x_ref[...]
        hb = _rms_fwd(xv, g_ref[...])[2].astype(BF16)
        h_ref[...] = hb
        a = _dot_nt(hb, wbuf[0])
        b = _dot_nt(hb, wbuf[1])
        a_ref[0] = a.astype(BF16)
        b_ref[0] = b.astype(BF16)
        part = _dot((a * _sigmoid(a) * b).astype(BF16), wbuf[2])
        rows = pl.ds(pl.multiple_of(i * TG, TG), TG)

        @pl.when(k == 0)
        def _():
            acc[rows, :] = part

        @pl.when(k > 0)
        def _():
            acc[rows, :] += part

        @pl.when(k == N_SHARD - 1)
        def _():
            xo_ref[...] = xv + 0.5 * acc[rows, :]

        @pl.when((k == N_SHARD - 1) & (i == nt - 1))
        def _():
            for j in (1, 2, 3):
                for w in range(3):
                    ici(j, w).wait_send()
                    d2d(j, w, c).wait_send()
            if payload:
                for ph in phases[1:]:
                    ph()

    def last(k, i, me):
        return (jnp.where(k == N_SHARD - 1, i, 0), 0)

    aliases = {3 + w: 4 + w for w in range(3)}
    if payload:
        aliases.update({6 + a: 7 + b for a, b in payload.aliases.items()})
    grid_spec = pltpu.PrefetchScalarGridSpec(
        num_scalar_prefetch=1, grid=(N_SHARD, nt),
        in_specs=[pl.BlockSpec((TG, D_MODEL), lambda k, i, me: (i, 0)),
                  pl.BlockSpec((1, D_MODEL), lambda k, i, me: (0, 0))] + [_ANY] * (3 + k_in),
        out_specs=[pl.BlockSpec((TG, D_MODEL), last), pl.BlockSpec((TG, D_MODEL), last),
                   pl.BlockSpec((1, TG, FF_S), lambda k, i, me: (me[0] ^ k, i, 0)),
                   pl.BlockSpec((1, TG, FF_S), lambda k, i, me: (me[0] ^ k, i, 0))] + [_ANY] * (3 + k_out),
        scratch_shapes=[pltpu.VMEM((T, D_MODEL), F32), pltpu.VMEM((3, FF_S, D_MODEL), BF16)]
        + [pltpu.SemaphoreType.DMA((9,)) for _ in range(4)] + [pltpu.SemaphoreType.DMA((3,))]
        + (payload.scratch if payload else []))
    return pl.pallas_call(
        body, name="ffn1_fwd_gather", grid_spec=grid_spec,
        out_shape=[jax.ShapeDtypeStruct((T, D_MODEL), F32), jax.ShapeDtypeStruct((T, D_MODEL), BF16),
                   jax.ShapeDtypeStruct((N_SHARD, T, FF_S), BF16), jax.ShapeDtypeStruct((N_SHARD, T, FF_S), BF16)]
        + [jax.ShapeDtypeStruct(s.shape, s.dtype) for s in slots] + (payload.out_shapes if payload else []),
        input_output_aliases=aliases,
        compiler_params=_params(("arbitrary", "arbitrary")),
    )(chip, x, g, *slots, *(payload.arrays if payload else []))


def _ffn_dgrad(dout, x, a, b, g, wg, wu, wd, name, payload=None):
    T = x.shape[0]

    def body(do_ref, x_ref, a_ref, b_ref, g_ref, wg_ref, wu_ref, wd_ref, dx_ref, da_ref, db_ref, dg_ref):
        do = do_ref[...]
        dob = do.astype(BF16)
        dh = jnp.zeros((TM, D_MODEL), F32)
        for s in range(N_SHARD):
            ds = 0.5 * _dot_nt(dob, wd_ref[s])
            av = a_ref[s].astype(F32)
            bv = b_ref[s].astype(F32)
            sig = _sigmoid(av)
            da = (ds * bv * (sig * (1.0 + av * (1.0 - sig)))).astype(BF16)
            db = (ds * (av * sig)).astype(BF16)
            da_ref[s] = da
            db_ref[s] = db
            dh += _dot(da, wg_ref[s]) + _dot(db, wu_ref[s])
        gv = g_ref[...]
        xhat, r, _ = _rms_fwd(x_ref[...], gv)
        dxn, dg = _rms_bwd(dh, xhat, r, gv)
        dx_ref[...] = do + dxn

        @pl.when(pl.program_id(0) == 0)
        def _():
            dg_ref[...] = jnp.zeros_like(dg_ref)

        dg_ref[...] += dg

    return _call(
        body, payload, name=name, grid=(T // TM,), when=_edges(T // TM), sem=("arbitrary",),
        in_specs=[_row_spec(TM, D_MODEL), _row_spec(TM, D_MODEL), _row3_spec(TM, FF_S), _row3_spec(TM, FF_S),
                  _const_spec((1, D_MODEL)), _const_spec(wg.shape), _const_spec(wu.shape), _const_spec(wd.shape)],
        out_specs=[_row_spec(TM, D_MODEL), _row3_spec(TM, FF_S), _row3_spec(TM, FF_S), _acc_spec((1, D_MODEL))],
        out_shape=[jax.ShapeDtypeStruct((T, D_MODEL), F32), jax.ShapeDtypeStruct((N_SHARD, T, FF_S), BF16),
                   jax.ShapeDtypeStruct((N_SHARD, T, FF_S), BF16), jax.ShapeDtypeStruct((1, D_MODEL), F32)],
        operands=(dout, x, a, b, g, wg, wu, wd))


def _ffn_wgrad(h, dout, a, b, da, db, name, payload=None):
    T = h.shape[0]

    def body(h_ref, do_ref, a_ref, b_ref, da_ref, db_ref, gwg_ref, gwu_ref, gwd_ref):
        @pl.when(pl.program_id(1) == 0)
        def _():
            gwg_ref[...] = jnp.zeros_like(gwg_ref)
            gwu_ref[...] = jnp.zeros_like(gwu_ref)
            gwd_ref[...] = jnp.zeros_like(gwd_ref)

        hv = h_ref[...]
        dob = do_ref[...].astype(BF16)
        av = a_ref[0].astype(F32)
        sv = (0.5 * av * _sigmoid(av) * b_ref[0].astype(F32)).astype(BF16)
        gwg_ref[0] += _dot_tn(da_ref[0], hv)
        gwu_ref[0] += _dot_tn(db_ref[0], hv)
        gwd_ref[0] += _dot_tn(sv, dob)

    tok = pl.BlockSpec((TW, D_MODEL), lambda s, i: (i, 0))
    act = pl.BlockSpec((1, TW, FF_S), lambda s, i: (s, i, 0))
    return _call(
        body, payload, name=name, grid=(N_SHARD, T // TW), when=_edges(N_SHARD * (T // TW)),
        sem=("arbitrary", "arbitrary"),
        in_specs=[tok, tok, act, act, act, act],
        out_specs=[pl.BlockSpec((1, FF_S, D_MODEL), lambda s, i: (s, 0, 0))] * 3,
        out_shape=[jax.ShapeDtypeStruct((N_SHARD, FF_S, D_MODEL), F32)] * 3,
        operands=(h, dout, a, b, da, db))


def _in_fwd(x, g, w_in, payload=None):
    T = x.shape[0]

    def body(x_ref, g_ref, w_ref, h_ref, qkv_ref, zs_ref, gl_ref):
        hb = _rms_fwd(x_ref[...], g_ref[...])[2].astype(BF16)
        h_ref[...] = hb
        z0 = _dot(hb, w_ref[0])
        qkv_ref[:, 0:IN_S] = z0.astype(BF16)
        z1 = _dot(hb, w_ref[1])
        qkv_ref[:, IN_S:3 * D_ATT] = z1[:, 0:384].astype(BF16)
        zs_ref[:, 0:768] = z1[:, 384:IN_S]
        z2 = _dot(hb, w_ref[2])
        zs_ref[:, 768:1024] = z2[:, 0:256]
        gl_ref[:, 0:896] = z2[:, 256:IN_S]
        gl_ref[:, 896:2048] = _dot(hb, w_ref[3])

    return _call(
        body, payload, name="in_fwd", grid=(T // TM,), when=_edges(T // TM), sem=("arbitrary",),
        in_specs=[_row_spec(TM, D_MODEL), _const_spec((1, D_MODEL)), _const_spec(w_in.shape)],
        out_specs=[_row_spec(TM, D_MODEL), _row_spec(TM, 3 * D_ATT), _row_spec(TM, 2 * D_SGU), _row_spec(TM, 2 * D_MODEL)],
        out_shape=[jax.ShapeDtypeStruct((T, D_MODEL), BF16), jax.ShapeDtypeStruct((T, 3 * D_ATT), BF16),
                   jax.ShapeDtypeStruct((T, 2 * D_SGU), F32), jax.ShapeDtypeStruct((T, 2 * D_MODEL), F32)],
        operands=(x, g, w_in))


def _in_dgrad(dx_res, x, g, w_in, dq, dk, dv, dzs, dgl):
    T = x.shape[0]

    def body(dxr_ref, x_ref, g_ref, w_ref, dq_ref, dk_ref, dv_ref, dzs_ref, dgl_ref, dx_ref, dz_ref, dg_ref):
        dz = jnp.concatenate([dq_ref[...], dk_ref[...].astype(BF16), dv_ref[...].astype(BF16), dzs_ref[...], dgl_ref[...]],
                             axis=1)
        dz_ref[...] = dz
        dh = jnp.zeros((TM, D_MODEL), F32)
        for s in range(N_SHARD):
            dh += _dot_nt(dz[:, s * IN_S:(s + 1) * IN_S], w_ref[s])
        gv = g_ref[...]
        xhat, r, _ = _rms_fwd(x_ref[...], gv)
        dxn, dg = _rms_bwd(dh, xhat, r, gv)
        dx_ref[...] = dxr_ref[...] + dxn

        @pl.when(pl.program_id(0) == 0)
        def _():
            dg_ref[...] = jnp.zeros_like(dg_ref)

        dg_ref[...] += dg

    pad_blocks = ATT_PAD // TM
    return pl.pallas_call(
        body, name="in_dgrad", grid=(T // TM,),
        in_specs=[_row_spec(TM, D_MODEL), _row_spec(TM, D_MODEL), _const_spec((1, D_MODEL)), _const_spec(w_in.shape),
                  _row_spec(TM, D_ATT), _row_spec(TM, D_ATT, pad_blocks), _row_spec(TM, D_ATT, pad_blocks),
                  _row_spec(TM, 2 * D_SGU), _row_spec(TM, 2 * D_MODEL)],
        out_specs=[_row_spec(TM, D_MODEL), _row_spec(TM, D_IN), _acc_spec((1, D_MODEL))],
        out_shape=[jax.ShapeDtypeStruct((T, D_MODEL), F32), jax.ShapeDtypeStruct((T, D_IN), BF16),
                   jax.ShapeDtypeStruct((1, D_MODEL), F32)],
        compiler_params=_params(("arbitrary",)),
    )(dx_res, x, g, w_in, dq, dk, dv, dzs, dgl)


def _in_wgrad(h, dz):
    T = h.shape[0]

    def body(h_ref, dz_ref, gw_ref):
        @pl.when(pl.program_id(1) == 0)
        def _():
            gw_ref[...] = jnp.zeros_like(gw_ref)

        gw_ref[0] += _dot_tn(h_ref[...], dz_ref[...])

    return pl.pallas_call(
        body, name="in_wgrad", grid=(N_SHARD, T // TW),
        in_specs=[pl.BlockSpec((TW, D_MODEL), lambda s, i: (i, 0)), pl.BlockSpec((TW, IN_S), lambda s, i: (i, s))],
        out_specs=pl.BlockSpec((1, D_MODEL, IN_S), lambda s, i: (s, 0, 0)),
        out_shape=jax.ShapeDtypeStruct((N_SHARD, D_MODEL, IN_S), F32),
        compiler_params=_params(("arbitrary", "arbitrary")),
    )(h, dz)


def _rel_onehot():
    r = lax.broadcasted_iota(jnp.int32, (REL_PAD, REL_PAD), 0)
    n = lax.broadcasted_iota(jnp.int32, (REL_PAD, REL_PAD), 1)
    idx = jnp.clip(BAND - 1 - n, -REL_CLIP, REL_CLIP) + REL_CLIP
    return jnp.where(r == idx, 1.0, 0.0).astype(BF16)


def _split3(v):
    p1 = v.astype(BF16)
    r1 = v - p1.astype(F32)
    p2 = r1.astype(BF16)
    p3 = (r1 - p2.astype(F32)).astype(BF16)
    return p1, p2, p3


def _relbias_fwd(tab_pad):
    def body(t_ref, o_ref):
        oh = _rel_onehot()
        acc = jnp.zeros((HEADS, REL_PAD), F32)
        for p in _split3(t_ref[...]):
            acc += _dot(p, oh)
        o_ref[...] = acc

    return pl.pallas_call(body, name="relbias_fwd", out_shape=jax.ShapeDtypeStruct((HEADS, REL_PAD), F32))(tab_pad)


def _relbias_bwd(z):
    def body(z_ref, o_ref):
        oh = _rel_onehot()
        dt2 = jnp.sum(z_ref[...], axis=1)
        acc = jnp.zeros((HEADS, REL_PAD), F32)
        for p in _split3(dt2):
            acc += _dot_nt(p, oh)
        o_ref[...] = acc

    return pl.pallas_call(body, name="relbias_bwd", out_shape=jax.ShapeDtypeStruct((HEADS, REL_PAD), F32))(z)


def _bias_blocks(t2):
    flat = jnp.tile(t2, (1, CHUNK))
    skew = flat[:, :CHUNK * (REL_PAD - 1)].reshape(HEADS, CHUNK, REL_PAD - 1)
    bias = skew[:, :, CHUNK - 1:CHUNK - 1 + BAND]
    slabs = [jnp.pad(bias, ((0, 0), (0, 0), (CHUNK * c, ATT_KEYS - BAND - CHUNK * c)), constant_values=NEG_INF)
             for c in range(2)]
    return jnp.concatenate(slabs, axis=1)


def _unskew(db2):
    out = []
    for c in range(2):
        slab = db2[:, CHUNK * c:CHUNK * (c + 1), CHUNK * c:CHUNK * c + BAND]
        y = jnp.pad(slab, ((0, 0), (0, 0), (CHUNK - 1, REL_PAD - BAND - CHUNK + 1)))
        yf = jnp.pad(y.reshape(HEADS, CHUNK * REL_PAD), ((0, 0), (0, CHUNK)))
        out.append(yf.reshape(HEADS, CHUNK, REL_PAD + 1)[:, :, :REL_PAD])
    return jnp.concatenate(out, axis=1)


def _att_load(qkv_hbm, q_s, k_s, v_s, sem, T):
    copies = [pltpu.make_async_copy(qkv_hbm.at[:, 0:D_ATT], q_s, sem.at[0]),
              pltpu.make_async_copy(qkv_hbm.at[:, D_ATT:2 * D_ATT], k_s.at[pl.ds(ATT_PAD, T), :], sem.at[1]),
              pltpu.make_async_copy(qkv_hbm.at[:, 2 * D_ATT:3 * D_ATT], v_s.at[pl.ds(ATT_PAD, T), :], sem.at[2])]
    for cp in copies:
        cp.start()
    k_s[0:ATT_PAD, :] = jnp.zeros((ATT_PAD, D_ATT), BF16)
    v_s[0:ATT_PAD, :] = jnp.zeros((ATT_PAD, D_ATT), BF16)
    for cp in copies:
        cp.wait()


def _head(v, h):
    return v[:, h * HEAD_DIM:(h + 1) * HEAD_DIM]


def _rows(v, h):
    return v[h * ATT_ROWS:(h + 1) * ATT_ROWS]


def _att_exp(qs, kw, bias_ref, valid):
    s = jnp.concatenate([_dot_nt(_head(qs, h), _head(kw, h)) + bias_ref[h] for h in range(HEADS)], axis=0)
    if valid is not None:
        s = jnp.where(valid, s, NEG_INF)
    e = jnp.exp(s - jnp.max(s, axis=-1, keepdims=True))
    return e, 1.0 / jnp.sum(e, axis=-1, keepdims=True)


def _att_blocks(T, block):
    n_edge = min(ATT_PAD // ATT_ROWS, T // ATT_ROWS)

    def edge(i, carry):
        r0 = i * ATT_ROWS
        block(i, (lax.broadcasted_iota(jnp.int32, (1, ATT_KEYS), 1) + (r0 - ATT_PAD)) >= 0)
        return carry

    def inner(i, carry):
        block(i, None)
        return carry

    lax.fori_loop(0, n_edge, edge, 0)
    lax.fori_loop(n_edge, T // ATT_ROWS, inner, 0)


def _att_fwd(qkv, bias2, payload=None):
    T = qkv.shape[0]

    def body(qkv_hbm, bias_ref, y_ref, q_s, k_s, v_s, sem):
        _att_load(qkv_hbm, q_s, k_s, v_s, sem, T)

        def block(i, valid):
            r0 = pl.multiple_of(i * ATT_ROWS, ATT_ROWS)
            qs = q_s[pl.ds(r0, ATT_ROWS), :] * (HEAD_DIM ** -0.5)
            kw = k_s[pl.ds(r0, ATT_KEYS), :]
            vw = v_s[pl.ds(r0, ATT_KEYS), :]
            e, rinv = _att_exp(qs, kw, bias_ref, valid)
            eb = e.astype(BF16)
            outs = [_dot(_rows(eb, h), _head(vw, h)) * _rows(rinv, h) for h in range(HEADS)]
            y_ref[pl.ds(r0, ATT_ROWS), :] = jnp.concatenate(outs, axis=1).astype(BF16)

        _att_blocks(T, block)

    return _call(
        body, payload, name="att_fwd", grid=None,
        in_specs=[pl.BlockSpec(memory_space=pl.ANY), pl.BlockSpec(memory_space=pltpu.VMEM)],
        out_specs=[pl.BlockSpec(memory_space=pltpu.VMEM)],
        out_shape=[jax.ShapeDtypeStruct((T, D_ATT), BF16)],
        scratch_shapes=[pltpu.VMEM((T, D_ATT), BF16), pltpu.VMEM((T + ATT_PAD, D_ATT), BF16),
                        pltpu.VMEM((T + ATT_PAD, D_ATT), BF16), pltpu.SemaphoreType.DMA((3,))],
        operands=(qkv, bias2))


def _att_bwd(qkv, dy, bias2, payload=None):
    T = qkv.shape[0]

    def body(qkv_hbm, dy_ref, bias_ref, dq_ref, dk_ref, dv_ref, db_ref, q_s, k_s, v_s, sem):
        _att_load(qkv_hbm, q_s, k_s, v_s, sem, T)
        dk_ref[...] = jnp.zeros_like(dk_ref)
        dv_ref[...] = jnp.zeros_like(dv_ref)
        db_ref[...] = jnp.zeros_like(db_ref)


        def block(i, valid):
            r0 = pl.multiple_of(i * ATT_ROWS, ATT_ROWS)
            qs = q_s[pl.ds(r0, ATT_ROWS), :] * (HEAD_DIM ** -0.5)
            kw = k_s[pl.ds(r0, ATT_KEYS), :]
            vw = v_s[pl.ds(r0, ATT_KEYS), :]
            dyb = dy_ref[pl.ds(r0, ATT_ROWS), :]
            e, rinv = _att_exp(qs, kw, bias_ref, valid)
            p = e * rinv
            dp = jnp.concatenate([_dot_nt(_head(dyb, h), _head(vw, h)) for h in range(HEADS)], axis=0)
            ds = p * (dp - jnp.sum(p * dp, axis=-1, keepdims=True))
            db_ref[...] += ds.reshape(HEADS, ATT_ROWS, ATT_KEYS)
            dsb = ds.astype(BF16)
            pb = p.astype(BF16)
            dq = [_dot(_rows(dsb, h), _head(kw, h)) for h in range(HEADS)]
            dk = [_dot_tn(_rows(dsb, h), _head(qs, h)) for h in range(HEADS)]
            dv = [_dot_tn(_rows(pb, h), _head(dyb, h)) for h in range(HEADS)]
            dq_ref[pl.ds(r0, ATT_ROWS), :] = (jnp.concatenate(dq, axis=1) * (HEAD_DIM ** -0.5)).astype(BF16)
            dk_ref[pl.ds(r0, ATT_KEYS), :] += jnp.concatenate(dk, axis=1)
            dv_ref[pl.ds(r0, ATT_KEYS), :] += jnp.concatenate(dv, axis=1)

        _att_blocks(T, block)

    vmem = pl.BlockSpec(memory_space=pltpu.VMEM)
    return _call(
        body, payload, name="att_bwd", grid=None,
        in_specs=[pl.BlockSpec(memory_space=pl.ANY), vmem, vmem],
        out_specs=[vmem, vmem, vmem, vmem],
        out_shape=[jax.ShapeDtypeStruct((T, D_ATT), BF16), jax.ShapeDtypeStruct((T + ATT_PAD, D_ATT), F32),
                   jax.ShapeDtypeStruct((T + ATT_PAD, D_ATT), F32), jax.ShapeDtypeStruct((HEADS, ATT_ROWS, ATT_KEYS), F32)],
        scratch_shapes=[pltpu.VMEM((T, D_ATT), BF16), pltpu.VMEM((T + ATT_PAD, D_ATT), BF16),
                        pltpu.VMEM((T + ATT_PAD, D_ATT), BF16), pltpu.SemaphoreType.DMA((3,))],
        operands=(qkv, dy, bias2))


_GELU_C = 0.7978845608028654
_GELU_A = 0.044715


def _gelu(x):
    t = jnp.tanh(_GELU_C * (x + _GELU_A * x * x * x))
    return 0.5 * x * (1.0 + t), t


def _gelu_grad(x, t):
    return 0.5 * (1.0 + t) + 0.5 * x * (1.0 - t * t) * _GELU_C * (1.0 + 3.0 * _GELU_A * x * x)


def _group_masks():
    col = lax.broadcasted_iota(jnp.int32, (SGU_GROUPS, D_SGU), 1) // SGU_GDIM
    grp = lax.broadcasted_iota(jnp.int32, (SGU_GROUPS, D_SGU), 0)
    return jnp.where(col == grp, 1.0, 0.0).astype(F32)


def _causal_mask(transposed=False):
    i = lax.broadcasted_iota(jnp.int32, (SGU_BLOCK, SGU_BLOCK), 0) // CHUNK
    j = lax.broadcasted_iota(jnp.int32, (SGU_BLOCK, SGU_BLOCK), 1) // CHUNK
    return (j >= i) if transposed else (i >= j)


def _sgu_norm(zs, lng, lnb):
    gz, t = _gelu(zs)
    u = gz[:, 0:D_SGU]
    vs = gz[:, D_SGU:2 * D_SGU]
    xc = vs - jnp.mean(vs, axis=-1, keepdims=True)
    rstd = lax.rsqrt(jnp.mean(xc * xc, axis=-1, keepdims=True) + EPS)
    xhat = xc * rstd
    return t, u, xhat, rstd, xhat * lng + lnb


def _sgu_mix(vn_blk, w_ref, bst, gm):
    mask = _causal_mask()
    s = jnp.zeros((SGU_BLOCK, D_SGU), F32)
    for g in range(SGU_GROUPS):
        wm = jnp.where(mask, w_ref[g], 0.0).astype(BF16)
        s += _dot(wm, (vn_blk * gm[g:g + 1, :]).astype(BF16))
        s += bst[:, g:g + 1] * gm[g:g + 1, :]
    return s


def _sgu_fwd(zs, lng, lnb, w_s, bst):
    T = zs.shape[0]
    nblk = TM // SGU_BLOCK

    def body(zs_ref, lng_ref, lnb_ref, w_ref, bst_ref, y_ref):
        _, u, _, _, vn = _sgu_norm(zs_ref[...], lng_ref[...], lnb_ref[...])
        gm = _group_masks()
        bst_v = bst_ref[...]
        for n in range(nblk):
            rows = slice(n * SGU_BLOCK, (n + 1) * SGU_BLOCK)
            s = _sgu_mix(vn[rows], w_ref, bst_v, gm)
            y_ref[rows, :] = (u[rows] * s).astype(BF16)

    return pl.pallas_call(
        body, name="sgu_fwd", grid=(T // TM,),
        in_specs=[_row_spec(TM, 2 * D_SGU), _const_spec((1, D_SGU)), _const_spec((1, D_SGU)),
                  _const_spec(w_s.shape), _const_spec(bst.shape)],
        out_specs=_row_spec(TM, D_SGU),
        out_shape=jax.ShapeDtypeStruct((T, D_SGU), BF16),
        compiler_params=_params(("arbitrary",)),
    )(zs, lng, lnb, w_s, bst)


def _sgu_bwd(zs, dy, lng, lnb, w_s, w_st, bst):
    T = zs.shape[0]
    nblk = TM // SGU_BLOCK

    def body(zs_ref, dy_ref, lng_ref, lnb_ref, w_ref, wt_ref, bst_ref, dzs_ref, dw_ref, dbt_ref, dlg_ref, dlb_ref):
        @pl.when(pl.program_id(0) == 0)
        def _():
            dw_ref[...] = jnp.zeros_like(dw_ref)
            dbt_ref[...] = jnp.zeros_like(dbt_ref)
            dlg_ref[...] = jnp.zeros_like(dlg_ref)
            dlb_ref[...] = jnp.zeros_like(dlb_ref)

        zs_v = zs_ref[...]
        lng_v = lng_ref[...]
        t, u, xhat, rstd, vn = _sgu_norm(zs_v, lng_v, lnb_ref[...])
        gm = _group_masks()
        bst_v = bst_ref[...]
        mask = _causal_mask()
        mask_t = _causal_mask(transposed=True)
        dyv = dy_ref[...].astype(F32)
        lane8 = lax.broadcasted_iota(jnp.int32, (1, SGU_GROUPS), 1)
        du_rows, dvn_rows = [], []
        for n in range(nblk):
            rows = slice(n * SGU_BLOCK, (n + 1) * SGU_BLOCK)
            vn_b = vn[rows]
            s = _sgu_mix(vn_b, w_ref, bst_v, gm)
            du_rows.append(dyv[rows] * s)
            dsb = dyv[rows] * u[rows]
            vnb16 = vn_b.astype(BF16)
            dvn = jnp.zeros((SGU_BLOCK, D_SGU), F32)
            dbt = jnp.zeros((SGU_BLOCK, SGU_GROUPS), F32)
            for g in range(SGU_GROUPS):
                dsg = dsb * gm[g:g + 1, :]
                dsg16 = dsg.astype(BF16)
                wmt = jnp.where(mask_t, wt_ref[g], 0.0).astype(BF16)
                dvn += _dot(wmt, dsg16)
                dw_ref[g] += jnp.where(mask, _dot_nt(dsg16, vnb16), 0.0)
                dbt += jnp.sum(dsg, axis=-1, keepdims=True) * jnp.where(lane8 == g, 1.0, 0.0)
            dbt_ref[...] += dbt
            dvn_rows.append(dvn)
        du = jnp.concatenate(du_rows, axis=0)
        dvn = jnp.concatenate(dvn_rows, axis=0)
        dlg_ref[...] += jnp.sum(dvn * xhat, axis=0, keepdims=True)
        dlb_ref[...] += jnp.sum(dvn, axis=0, keepdims=True)
        dxhat = dvn * lng_v
        dvs = rstd * (dxhat - jnp.mean(dxhat, axis=-1, keepdims=True)
                      - xhat * jnp.mean(dxhat * xhat, axis=-1, keepdims=True))
        dgz = jnp.concatenate([du, dvs], axis=1)
        dzs_ref[...] = (dgz * _gelu_grad(zs_v, t)).astype(BF16)

    return pl.pallas_call(
        body, name="sgu_bwd", grid=(T // TM,),
        in_specs=[_row_spec(TM, 2 * D_SGU), _row_spec(TM, D_SGU), _const_spec((1, D_SGU)), _const_spec((1, D_SGU)),
                  _const_spec(w_s.shape), _const_spec(w_st.shape), _const_spec(bst.shape)],
        out_specs=[_row_spec(TM, 2 * D_SGU), _acc_spec(w_s.shape), _acc_spec(bst.shape), _acc_spec((1, D_SGU)),
                   _acc_spec((1, D_SGU))],
        out_shape=[jax.ShapeDtypeStruct((T, 2 * D_SGU), BF16), jax.ShapeDtypeStruct(w_s.shape, F32),
                   jax.ShapeDtypeStruct(bst.shape, F32), jax.ShapeDtypeStruct((1, D_SGU), F32),
                   jax.ShapeDtypeStruct((1, D_SGU), F32)],
        compiler_params=_params(("arbitrary",)),
    )(zs, dy, lng, lnb, w_s, w_st, bst)


def _cols(v, s):
    return v[:, s * BR_S:(s + 1) * BR_S]


def _merge_fwd(x, y_att, y_sgu, gl, b_gate, wba, wbs, wo, payload=None):
    T = x.shape[0]

    def body(x_ref, ya_ref, ys_ref, gl_ref, bg_ref, wba_ref, wbs_ref, wo_ref, xo_ref, m_ref, pa_ref, ps_ref):
        ya = ya_ref[...]
        ys = ys_ref[...]
        pa = jnp.concatenate([_dot(ya, wba_ref[s]) for s in range(N_SHARD)], axis=1)
        ps = jnp.concatenate([_dot(ys, wbs_ref[s]) for s in range(N_SHARD)], axis=1)
        g = _sigmoid(gl_ref[...] + bg_ref[...])
        mb = (g[:, 0:D_MODEL] * pa + g[:, D_MODEL:2 * D_MODEL] * ps).astype(BF16)
        m_ref[...] = mb
        pa_ref[...] = pa.astype(BF16)
        ps_ref[...] = ps.astype(BF16)
        acc = jnp.zeros((TM, D_MODEL), F32)
        for s in range(N_SHARD):
            acc += _dot(_cols(mb, s), wo_ref[s])
        xo_ref[...] = x_ref[...] + acc

    tokd = jax.ShapeDtypeStruct((T, D_MODEL), BF16)
    return _call(
        body, payload, name="merge_fwd", grid=(T // TM,), when=_edges(T // TM), sem=("arbitrary",),
        in_specs=[_row_spec(TM, D_MODEL), _row_spec(TM, D_ATT), _row_spec(TM, D_SGU), _row_spec(TM, 2 * D_MODEL),
                  _const_spec((1, 2 * D_MODEL)), _const_spec(wba.shape), _const_spec(wbs.shape), _const_spec(wo.shape)],
        out_specs=[_row_spec(TM, D_MODEL)] * 4,
        out_shape=[jax.ShapeDtypeStruct((T, D_MODEL), F32), tokd, tokd, tokd],
        operands=(x, y_att, y_sgu, gl, b_gate, wba, wbs, wo))


def _merge_bwd(dx, y_att, y_sgu, gl, merged, pa, ps, b_gate, wba, wbs, wo, payload=None):
    T = dx.shape[0]

    def body(dx_ref, ya_ref, ys_ref, gl_ref, m_ref, pa_ref, ps_ref, bg_ref, wba_ref, wbs_ref, wo_ref,
             dya_ref, dys_ref, dgl_ref, dbg_ref, gwba_ref, gwbs_ref, gwo_ref):
        @pl.when(pl.program_id(0) == 0)
        def _():
            dbg_ref[...] = jnp.zeros_like(dbg_ref)
            gwba_ref[...] = jnp.zeros_like(gwba_ref)
            gwbs_ref[...] = jnp.zeros_like(gwbs_ref)
            gwo_ref[...] = jnp.zeros_like(gwo_ref)

        dxb = dx_ref[...].astype(BF16)
        dm = jnp.concatenate([_dot_nt(dxb, wo_ref[s]) for s in range(N_SHARD)], axis=1)
        g = _sigmoid(gl_ref[...] + bg_ref[...])
        ga = g[:, 0:D_MODEL]
        gs = g[:, D_MODEL:2 * D_MODEL]
        dpa = (dm * ga).astype(BF16)
        dps = (dm * gs).astype(BF16)
        dgl = jnp.concatenate([dm * pa_ref[...].astype(F32) * ga * (1.0 - ga),
                               dm * ps_ref[...].astype(F32) * gs * (1.0 - gs)], axis=1)
        dgl_ref[...] = dgl.astype(BF16)
        dbg_ref[...] += jnp.sum(dgl, axis=0, keepdims=True)
        ya = ya_ref[...]
        ys = ys_ref[...]
        mb = m_ref[...]
        dya = jnp.zeros((TM, D_ATT), F32)
        dys = jnp.zeros((TM, D_SGU), F32)
        for s in range(N_SHARD):
            dya += _dot_nt(_cols(dpa, s), wba_ref[s])
            dys += _dot_nt(_cols(dps, s), wbs_ref[s])
            gwo_ref[s] += _dot_tn(_cols(mb, s), dxb)
            gwba_ref[s] += _dot_tn(ya, _cols(dpa, s))
            gwbs_ref[s] += _dot_tn(ys, _cols(dps, s))
        dya_ref[...] = dya.astype(BF16)
        dys_ref[...] = dys.astype(BF16)

    return _call(
        body, payload, name="merge_bwd", grid=(T // TM,), when=_edges(T // TM), sem=("arbitrary",),
        operands=(dx, y_att, y_sgu, gl, merged, pa, ps, b_gate, wba, wbs, wo),
        in_specs=[_row_spec(TM, D_MODEL), _row_spec(TM, D_ATT), _row_spec(TM, D_SGU), _row_spec(TM, 2 * D_MODEL),
                  _row_spec(TM, D_MODEL), _row_spec(TM, D_MODEL), _row_spec(TM, D_MODEL),
                  _const_spec((1, 2 * D_MODEL)), _const_spec(wba.shape), _const_spec(wbs.shape), _const_spec(wo.shape)],
        out_specs=[_row_spec(TM, D_ATT), _row_spec(TM, D_SGU), _row_spec(TM, 2 * D_MODEL), _acc_spec((1, 2 * D_MODEL)),
                   _acc_spec(wba.shape), _acc_spec(wbs.shape), _acc_spec(wo.shape)],
        out_shape=[jax.ShapeDtypeStruct((T, D_ATT), BF16), jax.ShapeDtypeStruct((T, D_SGU), BF16),
                   jax.ShapeDtypeStruct((T, 2 * D_MODEL), BF16), jax.ShapeDtypeStruct((1, 2 * D_MODEL), F32),
                   jax.ShapeDtypeStruct(wba.shape, F32), jax.ShapeDtypeStruct(wbs.shape, F32),
                   jax.ShapeDtypeStruct(wo.shape, F32)])


def _loss_bwd(x, target, g):
    T = x.shape[0]

    def body(x_ref, t_ref, g_ref, dx_ref, loss_ref, dg_ref):
        @pl.when(pl.program_id(0) == 0)
        def _():
            loss_ref[...] = jnp.zeros_like(loss_ref)
            dg_ref[...] = jnp.zeros_like(dg_ref)

        gv = g_ref[...]
        xhat, r, y = _rms_fwd(x_ref[...], gv)
        err = y - t_ref[...]
        per_tok = jnp.mean(err * err, axis=-1, keepdims=True)
        loss_ref[...] += 0.5 * jnp.sum(per_tok, axis=0, keepdims=True)
        dxn, dg = _rms_bwd(err * (1.0 / D_MODEL), xhat, r, gv)
        dx_ref[...] = dxn
        dg_ref[...] += dg

    return pl.pallas_call(
        body, name="loss_bwd", grid=(T // TM,),
        in_specs=[_row_spec(TM, D_MODEL), _row_spec(TM, D_MODEL), _const_spec((1, D_MODEL))],
        out_specs=[_row_spec(TM, D_MODEL), _acc_spec((1, 128)), _acc_spec((1, D_MODEL))],
        out_shape=[jax.ShapeDtypeStruct((T, D_MODEL), F32), jax.ShapeDtypeStruct((1, 128), F32),
                   jax.ShapeDtypeStruct((1, D_MODEL), F32)],
        compiler_params=_params(("arbitrary",)),
    )(x, target, g)


BIG = ("ffn1_w_gate", "ffn1_w_up", "ffn1_w_down", "w_in", "w_branch_att", "w_branch_sgu", "w_out",
       "ffn2_w_gate", "ffn2_w_up", "ffn2_w_down")
SMALL = ("norm_ffn1", "norm_mix", "b_gate", "rel_bias", "sgu_ln_g", "sgu_ln_b", "sgu_w_s", "sgu_b_s", "norm_ffn2",
         "norm_final")


G_FFN1 = ("ffn1_w_gate", "ffn1_w_up", "ffn1_w_down")
G_MIX = ("w_in", "w_branch_att", "w_branch_sgu", "w_out")
G_FFN2 = ("ffn2_w_gate", "ffn2_w_up", "ffn2_w_down")


def _local_step(x, target, wb, ws, dist=None):
    def gather_on(names):
        return _ag_payload([wb[n] for n in names]) if dist else None

    t2 = _relbias_fwd(ws["rel_bias"])
    bias2 = _bias_blocks(t2)
    bst = ws["sgu_b_s"].T
    w_st = jnp.swapaxes(ws["sgu_w_s"], 1, 2)

    if dist:
        x1, h1, a1, b1, *got = _ffn_fwd_gather(x, ws["norm_ffn1"], [wb[n] for n in G_FFN1], dist[1],
                                               gather_on(G_MIX[0:1]))
        wb.update(zip(G_FFN1 + G_MIX[0:1], got))
    else:
        x1, h1, a1, b1 = _ffn_fwd(x, ws["norm_ffn1"], wb["ffn1_w_gate"], wb["ffn1_w_up"], wb["ffn1_w_down"], "ffn1_fwd")
    h2, qkv, zs, gl, *got = _in_fwd(x1, ws["norm_mix"], wb["w_in"], gather_on(G_MIX[1:] + G_FFN2[0:1]))
    wb.update(zip(G_MIX[1:] + G_FFN2[0:1], got))
    y_att, *got = _att_fwd(qkv, bias2, gather_on(G_FFN2[1:2]))
    wb.update(zip(G_FFN2[1:2], got))
    y_sgu = _sgu_fwd(zs, ws["sgu_ln_g"], ws["sgu_ln_b"], ws["sgu_w_s"], bst)
    x2, merged, pa, ps, *got = _merge_fwd(x1, y_att, y_sgu, gl, ws["b_gate"], wb["w_branch_att"], wb["w_branch_sgu"],
                                          wb["w_out"], gather_on(G_FFN2[2:3]))
    wb.update(zip(G_FFN2[2:3], got))
    x3, h3, a3, b3 = _ffn_fwd(x2, ws["norm_ffn2"], wb["ffn2_w_gate"], wb["ffn2_w_up"], wb["ffn2_w_down"], "ffn2_fwd")
    dx3, loss, g_final = _loss_bwd(x3, target, ws["norm_final"])

    gb, gs, sums = {}, {"norm_final": g_final}, {}

    def pair_on(names, small=None):
        return _px_payload([gb[n] for n in names], small) if dist else None

    def pair_add(names, halves):
        for n, rv in zip(names, halves):
            sums[n] = _pair_add(gb[n], rv, dist[0], dist[1], "pair_add_" + n)

    def chips_on(names):
        return _cx_payload([sums[n][1] for n in names], [sums[n][2] for n in names]) if dist else None

    dx2, da3, db3, gs["norm_ffn2"] = _ffn_dgrad(dx3, x2, a3, b3, ws["norm_ffn2"], wb["ffn2_w_gate"], wb["ffn2_w_up"],
                                                wb["ffn2_w_down"], "ffn2_dgrad")
    gb["ffn2_w_gate"], gb["ffn2_w_up"], gb["ffn2_w_down"] = _ffn_wgrad(h3, dx3, a3, b3, da3, db3, "ffn2_wgrad")
    dy_att, dy_sgu, dgl, gs["b_gate"], gb["w_branch_att"], gb["w_branch_sgu"], gb["w_out"], *got = _merge_bwd(
        dx2, y_att, y_sgu, gl, merged, pa, ps, ws["b_gate"], wb["w_branch_att"], wb["w_branch_sgu"], wb["w_out"],
        pair_on(G_FFN2))
    pair_add(G_FFN2, got)
    dq, dk, dv, db2, *lands2 = _att_bwd(qkv, dy_att, bias2, chips_on(G_FFN2))
    gs["rel_bias"] = _relbias_bwd(_unskew(db2))
    dzs, gs["sgu_w_s"], dbt, gs["sgu_ln_g"], gs["sgu_ln_b"] = _sgu_bwd(zs, dy_sgu, ws["sgu_ln_g"], ws["sgu_ln_b"],
                                                                      ws["sgu_w_s"], w_st, bst)
    gs["sgu_b_s"] = dbt.T
    dx1, dz, gs["norm_mix"] = _in_dgrad(dx2, x1, ws["norm_mix"], wb["w_in"], dq, dk, dv, dzs, dgl)
    gb["w_in"] = _in_wgrad(h2, dz)
    gx, da1, db1, gs["norm_ffn1"], *got = _ffn_dgrad(dx1, x, a1, b1, ws["norm_ffn1"], wb["ffn1_w_gate"],
                                                    wb["ffn1_w_up"], wb["ffn1_w_down"], "ffn1_dgrad", pair_on(G_MIX))
    pair_add(G_MIX, got)
    gb["ffn1_w_gate"], gb["ffn1_w_up"], gb["ffn1_w_down"], *lands_mix = _ffn_wgrad(h1, dx1, a1, b1, da1, db1,
                                                                                   "ffn1_wgrad", chips_on(G_MIX))
    if not dist:
        return loss, gx, gb, gs

    tail = _tail_reduce([gb[n] for n in G_FFN1], _pack_small(gs, loss))
    for i, n in enumerate(G_FFN1):
        sums[n] = (tail[i],)
    lands1, small_sums = tail[len(G_FFN1):2 * len(G_FFN1)], tail[-1]
    lands = dict(zip(G_FFN2 + G_MIX + G_FFN1, list(lands2) + list(lands_mix) + list(lands1)))
    fulls = [_final_sum(sums[n][0], lands[n], dist[1], dist[0], "final_sum_" + n) for n in BIG]
    return loss, gx, dict(zip(BIG, _sibling_share(fulls, "sibling_share"))), small_sums


_ANY = pl.BlockSpec(memory_space=pl.ANY)
_VMEM = pl.BlockSpec(memory_space=pltpu.VMEM)


def _mesh_pos():
    return lax.axis_index("x"), lax.axis_index("y"), lax.axis_index("c")


def _cast_slots(shards, chip, name):
    n = len(shards)
    r, ncol = shards[0].shape
    tr = r // 2

    def body(me_ref, *refs):
        for i_ref, o_ref in zip(refs[:n], refs[n:]):
            o_ref[0] = i_ref[...].astype(BF16)

    grid_spec = pltpu.PrefetchScalarGridSpec(
        num_scalar_prefetch=1, grid=(r // tr,),
        in_specs=[pl.BlockSpec((tr, ncol), lambda i, me: (i, 0))] * n,
        out_specs=[pl.BlockSpec((1, tr, ncol), lambda i, me: (me[0], i, 0))] * n)
    return pl.pallas_call(
        body, name=name, grid_spec=grid_spec,
        out_shape=[jax.ShapeDtypeStruct((N_SHARD, r, ncol), BF16)] * n,
        compiler_params=_params(("arbitrary",)),
    )(chip, *shards)


class _Payload:
    def __init__(self, arrays, out_shapes, aliases, scratch, phases):
        self.arrays = list(arrays)
        self.out_shapes = list(out_shapes)
        self.aliases = dict(aliases)
        self.scratch = list(scratch)
        self.phases = phases


def _remote(src, dst, ssem, rsem, dev):
    return pltpu.make_async_remote_copy(src_ref=src, dst_ref=dst, send_sem=ssem, recv_sem=rsem, device_id=dev,
                                        device_id_type=MESH)


def _call(body, payload, *, name, grid, in_specs, out_specs, out_shape, scratch_shapes=(), sem=None, when=None,
          operands=()):
    in_specs, out_specs, out_shape = list(in_specs), list(out_specs), list(out_shape)
    scratch_shapes = list(scratch_shapes)
    n_in, n_out, n_scr = len(in_specs), len(out_specs), len(scratch_shapes)
    kwargs = {}
    kernel = body
    if payload is not None:
        k_in, k_out = len(payload.arrays), len(payload.out_shapes)
        rank = len(grid) if grid else 0

        def kernel(*refs):
            a, b = n_in, n_in + k_in
            c, d = b + n_out, b + n_out + k_out
            e = d + n_scr
            phases = payload.phases(refs[a:b], refs[c:d], refs[e:])

            def run():
                body(*refs[:a], *refs[b:c], *refs[d:e])

            if not grid:
                phases[0]()
                run()
                for ph in phases[1:]:
                    ph()
                return
            step = pl.program_id(0)
            if rank == 2:
                step = step * grid[1] + pl.program_id(1)
            for ph, (at, before) in zip(phases, when):
                if before:
                    pl.when(step == at)(ph)
            run()
            for ph, (at, before) in zip(phases, when):
                if not before:
                    pl.when(step == at)(ph)

        in_specs += [_ANY] * k_in
        out_specs += [_ANY] * k_out
        out_shape += payload.out_shapes
        scratch_shapes += payload.scratch
        kwargs["input_output_aliases"] = {n_in + i: n_out + j for i, j in payload.aliases.items()}
        operands = tuple(operands) + tuple(payload.arrays)
    if grid:
        kwargs["grid"] = grid
    return pl.pallas_call(kernel, name=name, in_specs=in_specs, out_specs=out_specs, out_shape=out_shape,
                          scratch_shapes=scratch_shapes, compiler_params=_params(sem), **kwargs)(*operands)


def _ag_payload(slots):
    n = len(slots)

    def phases(_, refs, sems):
        send_i, recv_i, send_d, recv_d = sems
        x, y, c = _mesh_pos()
        me = 2 * x + y

        def half(w, core):
            rh = slots[w].shape[1] // 2
            return pl.ds(core * rh, rh)

        def ici(w, j):
            t = (me + 1 + j) % N_SHARD
            mine = refs[w].at[me, half(w, c), :]
            return _remote(mine, mine, send_i.at[3 * w + j], recv_i.at[3 * w + j], (t // 2, t % 2, c))

        def d2d(w, j, core):
            s = (me + 3 - j) % N_SHARD
            land = refs[w].at[s, half(w, core), :]
            return _remote(land, land, send_d.at[3 * w + j], recv_d.at[3 * w + j], (x, y, 1 - c))

        def start():
            for w in range(n):
                for j in range(3):
                    ici(w, j).start()

        def finish():
            for w in range(n):
                for j in range(3):
                    s = (me + 3 - j) % N_SHARD
                    land = refs[w].at[s, half(w, c), :]
                    _remote(land, land, send_i.at[3 * w + j], recv_i.at[3 * w + j], (x, y, c)).wait_recv()
                    d2d(w, j, c).start()
            for w in range(n):
                for j in range(3):
                    d2d(w, j, 1 - c).wait_recv()
            for w in range(n):
                for j in range(3):
                    ici(w, j).wait_send()
                    d2d(w, j, c).wait_send()

        return [start, finish]

    return _Payload(slots, [jax.ShapeDtypeStruct(s.shape, s.dtype) for s in slots], {i: i for i in range(n)},
                    [pltpu.SemaphoreType.DMA((3 * n,)) for _ in range(4)], phases)


def _px_payload(grads, small=None):
    arrays = list(grads) + ([small] if small is not None else [])
    n = len(arrays)

    def phases(ins, outs, sems):
        send, recv = sems
        x, y, c = _mesh_pos()

        def copy(w):
            if w < len(grads):
                rh = grads[w].shape[1] // 2
                src = ins[w].at[:, pl.ds((1 - c) * rh, rh), :]
            else:
                src = ins[w]
            return _remote(src, outs[w], send.at[w], recv.at[w], (x, y, 1 - c))

        def start():
            for w in range(n):
                copy(w).start()

        def finish():
            for w in range(n):
                copy(w).wait()

        return [start, finish]

    out_shapes = [jax.ShapeDtypeStruct((N_SHARD, g.shape[1] // 2, g.shape[2]), F32) for g in grads]
    if small is not None:
        out_shapes.append(jax.ShapeDtypeStruct(small.shape, F32))
    return _Payload(arrays, out_shapes, {}, [pltpu.SemaphoreType.DMA((n,)), pltpu.SemaphoreType.DMA((n,))], phases)


def _cx_payload(pbs, lands):
    n = len(pbs)

    def phases(ins, outs, sems):
        send, recv = sems
        x, y, c = _mesh_pos()
        me = 2 * x + y

        def copy(w, j):
            t = (me + 1 + j) % N_SHARD
            return _remote(ins[w].at[t], outs[w].at[me], send.at[3 * w + j], recv.at[3 * w + j], (t // 2, t % 2, c))

        def start():
            for w in range(n):
                for j in range(3):
                    copy(w, j).start()

        def finish():
            for w in range(n):
                for j in range(3):
                    copy(w, j).wait()

        return [start, finish]

    return _Payload(list(pbs) + list(lands), [jax.ShapeDtypeStruct(p.shape, BF16) for p in lands],
                    {n + i: i for i in range(n)},
                    [pltpu.SemaphoreType.DMA((3 * n,)), pltpu.SemaphoreType.DMA((3 * n,))], phases)


def _pair_add(g, rv, core, chip, name):
    _, r, ncol = g.shape
    rh = r // 2

    def body(c_ref, me_ref, g_ref, rv_ref, pf_ref, pb_ref, land_ref):
        s = g_ref[0] + rv_ref[0]
        sb = s.astype(BF16)
        pb_ref[0] = sb
        land_ref[0] = sb

        @pl.when(pl.program_id(0) == me_ref[0])
        def _():
            pf_ref[...] = s

    slot = pl.BlockSpec((1, rh, ncol), lambda s, c, me: (s, 0, 0))
    grid_spec = pltpu.PrefetchScalarGridSpec(
        num_scalar_prefetch=2, grid=(N_SHARD,),
        in_specs=[pl.BlockSpec((1, rh, ncol), lambda s, c, me: (s, c[0], 0)), slot],
        out_specs=[pl.BlockSpec((rh, ncol), lambda s, c, me: (0, 0)), slot, slot])
    return pl.pallas_call(
        body, name=name, grid_spec=grid_spec,
        out_shape=[jax.ShapeDtypeStruct((rh, ncol), F32), jax.ShapeDtypeStruct((N_SHARD, rh, ncol), BF16),
                   jax.ShapeDtypeStruct((N_SHARD, rh, ncol), BF16)],
        compiler_params=_params(("arbitrary",)),
    )(core, chip, g, rv)


def _tail_reduce(grads, small):
    n = len(grads)
    _, r, ncol = grads[0].shape
    rh = r // 2

    def body(*refs):
        g_hbm, sm = refs[:n], refs[n]
        pf, land, sm_out = refs[n + 1:2 * n + 1], refs[2 * n + 1:3 * n + 1], refs[3 * n + 1]
        scr = refs[3 * n + 2:]
        rv, mine, sendb = scr[:n], scr[n:2 * n], scr[2 * n:3 * n]
        sm_rv, sm_sum, d_send, d_recv, load, i_send, i_recv, store = scr[3 * n:]
        x, y, c = _mesh_pos()
        me = 2 * x + y
        sib = (x, y, 1 - c)

        def pair(w):
            src = g_hbm[w].at[:, pl.ds((1 - c) * rh, rh), :] if w < n else sm
            return _remote(src, rv[w] if w < n else sm_rv, d_send.at[w], d_recv.at[w], sib)

        def chips(w, j):
            t = (me + 1 + j) % N_SHARD
            src = sendb[w].at[t] if w < n else sm_sum
            dst = land[w].at[me] if w < n else sm_out.at[me]
            return _remote(src, dst, i_send.at[3 * w + j], i_recv.at[3 * w + j], (t // 2, t % 2, c))

        loads = [pltpu.make_async_copy(g_hbm[w].at[:, pl.ds(c * rh, rh), :], mine[w], load.at[w]) for w in range(n)]
        for w in range(n + 1):
            pair(w).start()
        for cp in loads:
            cp.start()
        stores = []
        for w in range(n):
            loads[w].wait()
            pair(w).wait_recv()
            for k in range(N_SHARD):
                s = mine[w][k] + rv[w][k]
                mine[w][k] = s
                sendb[w][k] = s.astype(BF16)
            stores += [pltpu.make_async_copy(mine[w].at[me], pf[w], store.at[2 * w]),
                       pltpu.make_async_copy(sendb[w].at[me], land[w].at[me], store.at[2 * w + 1])]
            for cp in stores[-2:]:
                cp.start()
            for j in range(3):
                chips(w, j).start()
        pair(n).wait_recv()
        sm_sum[...] = sm[...] + sm_rv[...]
        stores.append(pltpu.make_async_copy(sm_sum, sm_out.at[me], store.at[2 * n]))
        stores[-1].start()
        for j in range(3):
            chips(n, j).start()
        for w in range(n + 1):
            pair(w).wait_send()
            for j in range(3):
                chips(w, j).wait()
        for cp in stores:
            cp.wait()

    half = (N_SHARD, rh, ncol)
    return _call(
        body, None, name="tail_reduce", grid=None,
        in_specs=[_ANY] * n + [_VMEM], out_specs=[_ANY] * (2 * n + 1),
        out_shape=([jax.ShapeDtypeStruct((rh, ncol), F32)] * n + [jax.ShapeDtypeStruct(half, BF16)] * n
                   + [jax.ShapeDtypeStruct((N_SHARD,) + small.shape, F32)]),
        scratch_shapes=([pltpu.VMEM(half, F32)] * (2 * n) + [pltpu.VMEM(half, BF16)] * n
                        + [pltpu.VMEM(small.shape, F32), pltpu.VMEM(small.shape, F32),
                           pltpu.SemaphoreType.DMA((n + 1,)), pltpu.SemaphoreType.DMA((n + 1,)),
                           pltpu.SemaphoreType.DMA((n,)), pltpu.SemaphoreType.DMA((3 * n + 3,)),
                           pltpu.SemaphoreType.DMA((3 * n + 3,)), pltpu.SemaphoreType.DMA((2 * n + 1,))]),
        operands=(*grads, small))


def _final_sum(pf, land, chip, core, name):
    _, rh, ncol = land.shape

    def body(me_ref, c_ref, pf_ref, land_ref, o_ref):
        me = me_ref[0]
        acc = jnp.zeros((rh, ncol), F32)
        for k in range(N_SHARD):
            acc = acc + jnp.where(me == k, pf_ref[...], land_ref[k].astype(F32))
        o_ref[...] = acc

    grid_spec = pltpu.PrefetchScalarGridSpec(
        num_scalar_prefetch=2, grid=(1,),
        in_specs=[pl.BlockSpec((rh, ncol), lambda i, me, c: (0, 0)),
                  pl.BlockSpec((N_SHARD, rh, ncol), lambda i, me, c: (0, 0, 0))],
        out_specs=pl.BlockSpec((rh, ncol), lambda i, me, c: (c[0], 0)))
    return pl.pallas_call(
        body, name=name, grid_spec=grid_spec, out_shape=jax.ShapeDtypeStruct((2 * rh, ncol), F32),
        compiler_params=_params(("arbitrary",)),
    )(chip, core, pf, land)


def _sibling_share(fulls, name):
    n = len(fulls)

    def body(*refs):
        outs = refs[n:2 * n]
        send, recv = refs[2 * n:]
        x, y, c = _mesh_pos()
        cps = []
        for w in range(n):
            rh = fulls[w].shape[0] // 2
            mine = outs[w].at[pl.ds(c * rh, rh), :]
            cp = pltpu.make_async_remote_copy(src_ref=mine, dst_ref=mine, send_sem=send.at[w], recv_sem=recv.at[w],
                                              device_id=(x, y, 1 - c), device_id_type=MESH)
            cp.start()
            cps.append(cp)
        for cp in cps:
            cp.wait()

    return pl.pallas_call(
        body, name=name,
        in_specs=[_ANY] * n, out_specs=[_ANY] * n,
        out_shape=[jax.ShapeDtypeStruct(f.shape, F32) for f in fulls],
        input_output_aliases={i: i for i in range(n)},
        scratch_shapes=[pltpu.SemaphoreType.DMA((n,)), pltpu.SemaphoreType.DMA((n,))],
    )(*fulls)


_ROW = {"rel_bias": 128, "sgu_b_s": 136, "norm_ffn1": 144, "norm_mix": 145, "norm_ffn2": 146, "norm_final": 147,
        "b_gate": 148, "sgu_ln_g": 150, "sgu_ln_b": 151}


def _pack_small(gs, loss):
    def body(ws, rel, bs, n1, nm, n2, nf, bg, lg, lb, loss_ref, o_ref):
        o_ref[...] = jnp.zeros_like(o_ref)
        o_ref[LOSS_ROW:LOSS_ROW + 1, 0:128] = loss_ref[...]
        for g in range(SGU_GROUPS):
            o_ref[0:SGU_BLOCK, g * SGU_BLOCK:(g + 1) * SGU_BLOCK] = ws[g]
        o_ref[128:136, 0:REL_PAD] = rel[...]
        o_ref[136:144, 0:SGU_BLOCK] = bs[...]
        o_ref[144:145, :] = n1[...]
        o_ref[145:146, :] = nm[...]
        o_ref[146:147, :] = n2[...]
        o_ref[147:148, :] = nf[...]
        o_ref[148:149, :] = bg[:, 0:D_MODEL]
        o_ref[149:150, :] = bg[:, D_MODEL:2 * D_MODEL]
        o_ref[150:151, 0:D_SGU] = lg[...]
        o_ref[151:152, 0:D_SGU] = lb[...]

    order = ("sgu_w_s", "rel_bias", "sgu_b_s", "norm_ffn1", "norm_mix", "norm_ffn2", "norm_final", "b_gate", "sgu_ln_g",
             "sgu_ln_b")
    return pl.pallas_call(body, name="pack_small", out_shape=jax.ShapeDtypeStruct((SMALL_ROWS, D_MODEL), F32))(
        *[gs[k] for k in order], loss)


def _adam(w, g, m, v):
    m2 = ADAM_B1 * m + (1.0 - ADAM_B1) * g
    v2 = ADAM_B2 * v + (1.0 - ADAM_B2) * (g * g)
    m_hat = m2 / (1.0 - ADAM_B1 ** ADAM_STEP)
    v_hat = v2 / (1.0 - ADAM_B2 ** ADAM_STEP)
    delta = -ADAM_LR * (m_hat / (jnp.sqrt(v_hat) + ADAM_EPS) + ADAM_WD * w)
    return delta, m2, v2


def _adam_small(sin, w, m, v):
    names = SMALL
    k = len(names)

    def body(*refs):
        sin_ref = refs[0]
        w_r, m_r, v_r = refs[1:1 + k], refs[1 + k:1 + 2 * k], refs[1 + 2 * k:1 + 3 * k]
        outs = refs[1 + 3 * k:]
        tot = sin_ref[0] + sin_ref[1] + sin_ref[2] + sin_ref[3]
        outs[4 * k][...] = tot[LOSS_ROW:LOSS_ROW + 1, 0:128]
        for i, name in enumerate(names):
            o = outs[4 * i:4 * i + 4]
            if name == "sgu_w_s":
                for gi in range(SGU_GROUPS):
                    g = tot[0:SGU_BLOCK, gi * SGU_BLOCK:(gi + 1) * SGU_BLOCK]
                    res = (g,) + _adam(w_r[i][gi], g, m_r[i][gi], v_r[i][gi])
                    for ref, val in zip(o, res):
                        ref[gi] = val
                continue
            r0 = _ROW[name]
            if name == "rel_bias":
                g = tot[r0:r0 + HEADS, 0:REL_PAD]
            elif name == "sgu_b_s":
                g = tot[r0:r0 + SGU_GROUPS, 0:SGU_BLOCK]
            elif name == "b_gate":
                g = jnp.concatenate([tot[r0:r0 + 1, :], tot[r0 + 1:r0 + 2, :]], axis=1)
            elif name in ("sgu_ln_g", "sgu_ln_b"):
                g = tot[r0:r0 + 1, 0:D_SGU]
            else:
                g = tot[r0:r0 + 1, :]
            res = (g,) + _adam(w_r[i][...], g, m_r[i][...], v_r[i][...])
            for ref, val in zip(o, res):
                ref[...] = val

    out_shape = []
    for name in names:
        out_shape += [jax.ShapeDtypeStruct(w[name].shape, F32)] * 4
    out_shape.append(jax.ShapeDtypeStruct((1, 128), F32))
    flat = pl.pallas_call(body, name="adam_small", out_shape=out_shape, compiler_params=_params())(
        sin, *[w[n] for n in names], *[m[n] for n in names], *[v[n] for n in names])
    return {name: tuple(flat[4 * i:4 * i + 4]) for i, name in enumerate(names)}, flat[4 * k]


def _adam_big(w, g, m, v, name):
    r, ncol = w.shape
    tr = 256 if r % 256 == 0 else r // 2

    def body(w_ref, g_ref, m_ref, v_ref, d_ref, m2_ref, v2_ref):
        d_ref[...], m2_ref[...], v2_ref[...] = _adam(w_ref[...], g_ref[...], m_ref[...], v_ref[...])

    spec = pl.BlockSpec((tr, ncol), lambda i: (i, 0))
    return pl.pallas_call(
        body, name=name, grid=(r // tr,), in_specs=[spec] * 4, out_specs=[spec] * 3,
        out_shape=[jax.ShapeDtypeStruct(w.shape, F32)] * 3, compiler_params=_params(("arbitrary",)),
    )(w, g, m, v)


WEIGHTS = ("norm_ffn1", "ffn1_w_gate", "ffn1_w_up", "ffn1_w_down", "norm_mix", "w_in", "b_gate", "rel_bias", "sgu_ln_g",
           "sgu_ln_b", "sgu_w_s", "sgu_b_s", "w_branch_att", "w_branch_sgu", "w_out", "norm_ffn2", "ffn2_w_gate",
           "ffn2_w_up", "ffn2_w_down", "norm_final")


GATE_UP = ("ffn1_w_gate", "ffn1_w_up", "ffn2_w_gate", "ffn2_w_up")
_FFN = ("ffn1_w_gate", "ffn1_w_up", "ffn1_w_down", "ffn2_w_gate", "ffn2_w_up", "ffn2_w_down")
_CAST_GROUPS = ((_FFN, "cast_ffn"), (("w_in",), "cast_w_in"), (("w_branch_att", "w_branch_sgu"), "cast_branch"),
                (("w_out",), "cast_w_out"))


def _big_form(name, a):
    return jnp.swapaxes(a, 1, 2)[0] if name in GATE_UP else a[0]


def _big_back(name, a):
    return jnp.swapaxes(a[None], 1, 2) if name in GATE_UP else a[None]


def _small_form(name, a):
    if name == "norm_final":
        return a.reshape(1, D_MODEL)
    if name == "rel_bias":
        return jnp.pad(a[0], ((0, 0), (0, REL_PAD - N_REL)))
    if name in ("sgu_w_s", "sgu_b_s"):
        return a[0]
    return a


def _small_back(name, a, like):
    if name == "rel_bias":
        a = a[:, :N_REL]
    return a.reshape(like.shape)


def kernel(x, norm_ffn1, ffn1_w_gate, ffn1_w_up, ffn1_w_down, norm_mix, w_in, b_gate, rel_bias, sgu_ln_g, sgu_ln_b, sgu_w_s, sgu_b_s, w_branch_att, w_branch_sgu, w_out, norm_ffn2, ffn2_w_gate, ffn2_w_up, ffn2_w_down, norm_final, loss_target, m_norm_ffn1, m_ffn1_w_gate, m_ffn1_w_up, m_ffn1_w_down, m_norm_mix, m_w_in, m_b_gate, m_rel_bias, m_sgu_ln_g, m_sgu_ln_b, m_sgu_w_s, m_sgu_b_s, m_w_branch_att, m_w_branch_sgu, m_w_out, m_norm_ffn2, m_ffn2_w_gate, m_ffn2_w_up, m_ffn2_w_down, m_norm_final, v_norm_ffn1, v_ffn1_w_gate, v_ffn1_w_up, v_ffn1_w_down, v_norm_mix, v_w_in, v_b_gate, v_rel_bias, v_sgu_ln_g, v_sgu_ln_b, v_sgu_w_s, v_sgu_b_s, v_w_branch_att, v_w_branch_sgu, v_w_out, v_norm_ffn2, v_ffn2_w_gate, v_ffn2_w_up, v_ffn2_w_down, v_norm_final):
    w = dict(norm_ffn1=norm_ffn1, ffn1_w_gate=ffn1_w_gate, ffn1_w_up=ffn1_w_up, ffn1_w_down=ffn1_w_down, norm_mix=norm_mix,
             w_in=w_in, b_gate=b_gate, rel_bias=rel_bias, sgu_ln_g=sgu_ln_g, sgu_ln_b=sgu_ln_b, sgu_w_s=sgu_w_s,
             sgu_b_s=sgu_b_s, w_branch_att=w_branch_att, w_branch_sgu=w_branch_sgu, w_out=w_out, norm_ffn2=norm_ffn2,
             ffn2_w_gate=ffn2_w_gate, ffn2_w_up=ffn2_w_up, ffn2_w_down=ffn2_w_down, norm_final=norm_final)
    m = dict(norm_ffn1=m_norm_ffn1, ffn1_w_gate=m_ffn1_w_gate, ffn1_w_up=m_ffn1_w_up, ffn1_w_down=m_ffn1_w_down,
             norm_mix=m_norm_mix, w_in=m_w_in, b_gate=m_b_gate, rel_bias=m_rel_bias, sgu_ln_g=m_sgu_ln_g,
             sgu_ln_b=m_sgu_ln_b, sgu_w_s=m_sgu_w_s, sgu_b_s=m_sgu_b_s, w_branch_att=m_w_branch_att,
             w_branch_sgu=m_w_branch_sgu, w_out=m_w_out, norm_ffn2=m_norm_ffn2, ffn2_w_gate=m_ffn2_w_gate,
             ffn2_w_up=m_ffn2_w_up, ffn2_w_down=m_ffn2_w_down, norm_final=m_norm_final)
    v = dict(norm_ffn1=v_norm_ffn1, ffn1_w_gate=v_ffn1_w_gate, ffn1_w_up=v_ffn1_w_up, ffn1_w_down=v_ffn1_w_down,
             norm_mix=v_norm_mix, w_in=v_w_in, b_gate=v_b_gate, rel_bias=v_rel_bias, sgu_ln_g=v_sgu_ln_g,
             sgu_ln_b=v_sgu_ln_b, sgu_w_s=v_sgu_w_s, sgu_b_s=v_sgu_b_s, w_branch_att=v_w_branch_att,
             w_branch_sgu=v_w_branch_sgu, w_out=v_w_out, norm_ffn2=v_norm_ffn2, ffn2_w_gate=v_ffn2_w_gate,
             ffn2_w_up=v_ffn2_w_up, ffn2_w_down=v_ffn2_w_down, norm_final=v_norm_final)

    core = lax.axis_index("c").astype(jnp.int32).reshape(1)
    chip = (2 * lax.axis_index("x") + lax.axis_index("y")).astype(jnp.int32).reshape(1)

    wk = {n: _big_form(n, w[n]) for n in BIG}
    slots = {}
    for names, call in _CAST_GROUPS:
        slots.update(zip(names, _cast_slots([wk[n] for n in names], chip, call)))
    ws ={n: _small_form(n, w[n]) for n in SMALL}
    _, gx, shard_grads, small_sums = _local_step(x[0], loss_target[0], slots, ws, (core, chip))

    small, loss = _adam_small(small_sums, ws, {n: _small_form(n, m[n]) for n in SMALL},
                              {n: _small_form(n, v[n]) for n in SMALL})
    grad, delta, new_m, new_v = {}, {}, {}, {}
    for n in SMALL:
        grad[n], delta[n], new_m[n], new_v[n] = (_small_back(n, a, w[n]) for a in small[n])
    for n in BIG:
        g2 = shard_grads[n]
        d2, m2, v2 = _adam_big(wk[n], g2, _big_form(n, m[n]), _big_form(n, v[n]), "adam_" + n)
        grad[n], delta[n], new_m[n], new_v[n] = (_big_back(n, a) for a in (g2, d2, m2, v2))

    return (loss[0, 0], gx.reshape(x.shape), *[grad[n] for n in WEIGHTS], *[delta[n] for n in WEIGHTS],
            *[new_m[n] for n in WEIGHTS], *[new_v[n] for n in WEIGHTS])
```

```python
import functools

import jax
import jax.numpy as jnp
from jax import lax
from jax.experimental import pallas as pl
from jax.experimental.pallas import tpu as pltpu

F32 = jnp.float32
BF16 = jnp.bfloat16

D_MODEL = 1024
N_SHARD = 4
D_FF = 2816
FF_S = D_FF // N_SHARD
D_ATT = 512
D_SGU = 512
D_IN = 3 * D_ATT + 2 * D_SGU + 2 * D_MODEL
IN_S = D_IN // N_SHARD
BR_S = D_MODEL // N_SHARD
HEADS = 8
HEAD_DIM = 64
CHUNK = 64
N_LEFT = 8
BAND = (N_LEFT + 1) * CHUNK
REL_CLIP = 256
N_REL = 2 * REL_CLIP + 1
REL_PAD = 640
SGU_BLOCK = 128
SGU_GROUPS = 8
SGU_GDIM = 64
EPS = 1e-6
NEG_INF = -1e30

ATT_ROWS = 2 * CHUNK
ATT_KEYS = BAND + CHUNK
ATT_PAD = N_LEFT * CHUNK

ADAM_LR = 0.001
ADAM_B1 = 0.9
ADAM_B2 = 0.999
ADAM_EPS = 1e-08
ADAM_WD = 0.01
ADAM_STEP = 10

TM = 256
TW = 512
VMEM_LIMIT = 56 * 1024 * 1024

SMALL_ROWS = 160
LOSS_ROW = 152
MESH = pl.DeviceIdType.MESH

_NT = (((1,), (1,)), ((), ()))
_TN = (((0,), (0,)), ((), ()))


def _params(sem=None):
    return pltpu.CompilerParams(dimension_semantics=sem, vmem_limit_bytes=VMEM_LIMIT)


def _const_spec(shape):
    nd = len(shape)
    return pl.BlockSpec(shape, lambda *_: (0,) * nd, pipeline_mode=pl.Buffered(1))


def _acc_spec(shape):
    nd = len(shape)
    return pl.BlockSpec(shape, lambda *_: (0,) * nd)


def _row_spec(tm, ncols, off=0):
    return pl.BlockSpec((tm, ncols), lambda i: (i + off, 0))


def _row3_spec(tm, ncols):
    return pl.BlockSpec((N_SHARD, tm, ncols), lambda i: (0, i, 0))


def _dot(a, b):
    return jnp.dot(a, b, preferred_element_type=F32)


def _dot_nt(a, b):
    return lax.dot_general(a, b, _NT, preferred_element_type=F32)


def _dot_tn(a, b):
    return lax.dot_general(a, b, _TN, preferred_element_type=F32)


def _rms_fwd(x, g):
    r = lax.rsqrt(jnp.mean(x * x, axis=-1, keepdims=True) + EPS)
    xhat = x * r
    return xhat, r, xhat * g


def _rms_bwd(dh, xhat, r, g):
    dxhat = dh * g
    dx = r * (dxhat - xhat * jnp.mean(dxhat * xhat, axis=-1, keepdims=True))
    dg = jnp.sum(dh * xhat, axis=0, keepdims=True)
    return dx, dg


def _sigmoid(x):
    return 1.0 / (1.0 + jnp.exp(-x))


def _edges(n_steps):
    return [(0, True), (n_steps - 1, False)]


def _ffn_fwd(x, g, wg, wu, wd, name, payload=None):
    T = x.shape[0]

    def body(x_ref, g_ref, wg_ref, wu_ref, wd_ref, xo_ref, h_ref, a_ref, b_ref):
        xv = x_ref[...]
        hb = _rms_fwd(xv, g_ref[...])[2].astype(BF16)
        h_ref[...] = hb
        acc = jnp.zeros((TM, D_MODEL), F32)
        for s in range(N_SHARD):
            a = _dot_nt(hb, wg_ref[s])
            b = _dot_nt(hb, wu_ref[s])
            a_ref[s] = a.astype(BF16)
            b_ref[s] = b.astype(BF16)
            sv = a * _sigmoid(a) * b
            acc += _dot(sv.astype(BF16), wd_ref[s])
        xo_ref[...] = xv + 0.5 * acc

    return _call(
        body, payload, name=name, grid=(T // TM,), when=_edges(T // TM), sem=("arbitrary",),
        in_specs=[_row_spec(TM, D_MODEL), _const_spec((1, D_MODEL)), _const_spec(wg.shape), _const_spec(wu.shape),
                  _const_spec(wd.shape)],
        out_specs=[_row_spec(TM, D_MODEL), _row_spec(TM, D_MODEL), _row3_spec(TM, FF_S), _row3_spec(TM, FF_S)],
        out_shape=[jax.ShapeDtypeStruct((T, D_MODEL), F32), jax.ShapeDtypeStruct((T, D_MODEL), BF16),
                   jax.ShapeDtypeStruct((N_SHARD, T, FF_S), BF16), jax.ShapeDtypeStruct((N_SHARD, T, FF_S), BF16)],
        operands=(x, g, wg, wu, wd))


TG = 512


def _ffn_fwd_gather(x, g, slots, chip, payload=None):
    T = x.shape[0]
    nt = T // TG
    rh = FF_S // 2

    k_in = len(payload.arrays) if payload else 0
    k_out = len(payload.out_shapes) if payload else 0

    def body(me_ref, x_ref, g_ref, *rest):
        p_in, rest = rest[3:3 + k_in], rest[3 + k_in:]
        xo_ref, h_ref, a_ref, b_ref, wg_hbm, wu_hbm, wd_hbm = rest[:7]
        p_out, rest = rest[7:7 + k_out], rest[7 + k_out:]
        acc, wbuf, send_i, recv_i, send_d, recv_d, wsem = rest[:7]
        phases = payload.phases(p_in, p_out, rest[7:]) if payload else None
        k, i = pl.program_id(0), pl.program_id(1)
        mx, my, c = _mesh_pos()
        me = 2 * mx + my
        w_hbm = (wg_hbm, wu_hbm, wd_hbm)

        def ici(j, w):
            t = me ^ j
            mine = w_hbm[w].at[me, pl.ds(c * rh, rh), :]
            return _remote(mine, mine, send_i.at[3 * (j - 1) + w], recv_i.at[3 * (j - 1) + w], (t // 2, t % 2, c))

        def d2d(j, w, core):
            land = w_hbm[w].at[me ^ j, pl.ds(core * rh, rh), :]
            return _remote(land, land, send_d.at[3 * (j - 1) + w], recv_d.at[3 * (j - 1) + w], (mx, my, 1 - c))

        def load(s):
            copies = [pltpu.make_async_copy(w_hbm[w].at[s], wbuf.at[w], wsem.at[w]) for w in range(3)]
            for cp in copies:
                cp.start()
            for cp in copies:
                cp.wait()

        @pl.when((k == 0) & (i == 0))
        def _():
            for j in (1, 2, 3):
                for w in range(3):
                    ici(j, w).start()
            load(me)

        for j in (1, 2, 3):
            @pl.when((k == j) & (i == 0))
            def _(j=j):
                for w in range(3):
                    land = w_hbm[w].at[me ^ j, pl.ds(c * rh, rh), :]
                    _remote(land, land, send_i.at[3 * (j - 1) + w], recv_i.at[3 * (j - 1) + w], (mx, my, c)).wait_recv()
                    d2d(j, w, c).start()
                for w in range(3):
                    d2d(j, w, 1 - c).wait_recv()
                if payload and j == 2:
                    phases[0]()
                load(me ^ j)

        xv = x_ref[...]
        hb = _rms_fwd(xv, g_ref[...])[2].astype(BF16)
        h_ref[...] = hb
        a = _dot_nt(hb, wbuf[0])
        b = _dot_nt(hb, wbuf[1])
        a_ref[0] = a.astype(BF16)
        b_ref[0] = b.astype(BF16)
        part = _dot((a * _sigmoid(a) * b).astype(BF16), wbuf[2])
        rows = pl.ds(pl.multiple_of(i * TG, TG), TG)

        @pl.when(k == 0)
        def _():
            acc[rows, :] = part

        @pl.when(k > 0)
        def _():
            acc[rows, :] += part

        @pl.when(k == N_SHARD - 1)
        def _():
            xo_ref[...] = xv + 0.5 * acc[rows, :]

        @pl.when((k == N_SHARD - 1) & (i == nt - 1))
        def _():
            for j in (1, 2, 3):
                for w in range(3):
                    ici(j, w).wait_send()
                    d2d(j, w, c).wait_send()
            if payload:
                for ph in phases[1:]:
                    ph()

    def last(k, i, me):
        return (jnp.where(k == N_SHARD - 1, i, 0), 0)

    aliases = {3 + w: 4 + w for w in range(3)}
    if payload:
        aliases.update({6 + a: 7 + b for a, b in payload.aliases.items()})
    grid_spec = pltpu.PrefetchScalarGridSpec(
        num_scalar_prefetch=1, grid=(N_SHARD, nt),
        in_specs=[pl.BlockSpec((TG, D_MODEL), lambda k, i, me: (i, 0)),
                  pl.BlockSpec((1, D_MODEL), lambda k, i, me: (0, 0))] + [_ANY] * (3 + k_in),
        out_specs=[pl.BlockSpec((TG, D_MODEL), last), pl.BlockSpec((TG, D_MODEL), last),
                   pl.BlockSpec((1, TG, FF_S), lambda k, i, me: (me[0] ^ k, i, 0)),
                   pl.BlockSpec((1, TG, FF_S), lambda k, i, me: (me[0] ^ k, i, 0))] + [_ANY] * (3 + k_out),
        scratch_shapes=[pltpu.VMEM((T, D_MODEL), F32), pltpu.VMEM((3, FF_S, D_MODEL), BF16)]
        + [pltpu.SemaphoreType.DMA((9,)) for _ in range(4)] + [pltpu.SemaphoreType.DMA((3,))]
        + (payload.scratch if payload else []))
    return pl.pallas_call(
        body, name="ffn1_fwd_gather", grid_spec=grid_spec,
        out_shape=[jax.ShapeDtypeStruct((T, D_MODEL), F32), jax.ShapeDtypeStruct((T, D_MODEL), BF16),
                   jax.ShapeDtypeStruct((N_SHARD, T, FF_S), BF16), jax.ShapeDtypeStruct((N_SHARD, T, FF_S), BF16)]
        + [jax.ShapeDtypeStruct(s.shape, s.dtype) for s in slots] + (payload.out_shapes if payload else []),
        input_output_aliases=aliases,
        compiler_params=_params(("arbitrary", "arbitrary")),
    )(chip, x, g, *slots, *(payload.arrays if payload else []))


def _ffn_dgrad(dout, x, a, b, g, wg, wu, wd, name, payload=None):
    T = x.shape[0]

    def body(do_ref, x_ref, a_ref, b_ref, g_ref, wg_ref, wu_ref, wd_ref, dx_ref, da_ref, db_ref, dg_ref):
        do = do_ref[...]
        dob = do.astype(BF16)
        dh = jnp.zeros((TM, D_MODEL), F32)
        for s in range(N_SHARD):
            ds = 0.5 * _dot_nt(dob, wd_ref[s])
            av = a_ref[s].astype(F32)
            bv = b_ref[s].astype(F32)
            sig = _sigmoid(av)
            da = (ds * bv * (sig * (1.0 + av * (1.0 - sig)))).astype(BF16)
            db = (ds * (av * sig)).astype(BF16)
            da_ref[s] = da
            db_ref[s] = db
            dh += _dot(da, wg_ref[s]) + _dot(db, wu_ref[s])
        gv = g_ref[...]
        xhat, r, _ = _rms_fwd(x_ref[...], gv)
        dxn, dg = _rms_bwd(dh, xhat, r, gv)
        dx_ref[...] = do + dxn

        @pl.when(pl.program_id(0) == 0)
        def _():
            dg_ref[...] = jnp.zeros_like(dg_ref)

        dg_ref[...] += dg

    return _call(
        body, payload, name=name, grid=(T // TM,), when=_edges(T // TM), sem=("arbitrary",),
        in_specs=[_row_spec(TM, D_MODEL), _row_spec(TM, D_MODEL), _row3_spec(TM, FF_S), _row3_spec(TM, FF_S),
                  _const_spec((1, D_MODEL)), _const_spec(wg.shape), _const_spec(wu.shape), _const_spec(wd.shape)],
        out_specs=[_row_spec(TM, D_MODEL), _row3_spec(TM, FF_S), _row3_spec(TM, FF_S), _acc_spec((1, D_MODEL))],
        out_shape=[jax.ShapeDtypeStruct((T, D_MODEL), F32), jax.ShapeDtypeStruct((N_SHARD, T, FF_S), BF16),
                   jax.ShapeDtypeStruct((N_SHARD, T, FF_S), BF16), jax.ShapeDtypeStruct((1, D_MODEL), F32)],
        operands=(dout, x, a, b, g, wg, wu, wd))


def _ffn_wgrad(h, dout, a, b, da, db, name, payload=None):
    T = h.shape[0]

    def body(h_ref, do_ref, a_ref, b_ref, da_ref, db_ref, gwg_ref, gwu_ref, gwd_ref):
        @pl.when(pl.program_id(1) == 0)
        def _():
            gwg_ref[...] = jnp.zeros_like(gwg_ref)
            gwu_ref[...] = jnp.zeros_like(gwu_ref)
            gwd_ref[...] = jnp.zeros_like(gwd_ref)

        hv = h_ref[...]
        dob = do_ref[...].astype(BF16)
        av = a_ref[0].astype(F32)
        sv = (0.5 * av * _sigmoid(av) * b_ref[0].astype(F32)).astype(BF16)
        gwg_ref[0] += _dot_tn(da_ref[0], hv)
        gwu_ref[0] += _dot_tn(db_ref[0], hv)
        gwd_ref[0] += _dot_tn(sv, dob)

    tok = pl.BlockSpec((TW, D_MODEL), lambda s, i: (i, 0))
    act = pl.BlockSpec((1, TW, FF_S), lambda s, i: (s, i, 0))
    return _call(
        body, payload, name=name, grid=(N_SHARD, T // TW), when=_edges(N_SHARD * (T // TW)),
        sem=("arbitrary", "arbitrary"),
        in_specs=[tok, tok, act, act, act, act],
        out_specs=[pl.BlockSpec((1, FF_S, D_MODEL), lambda s, i: (s, 0, 0))] * 3,
        out_shape=[jax.ShapeDtypeStruct((N_SHARD, FF_S, D_MODEL), F32)] * 3,
        operands=(h, dout, a, b, da, db))


def _in_fwd(x, g, w_in, payload=None):
    T = x.shape[0]

    def body(x_ref, g_ref, w_ref, h_ref, qkv_ref, zs_ref, gl_ref):
        hb = _rms_fwd(x_ref[...], g_ref[...])[2].astype(BF16)
        h_ref[...] = hb
        z0 = _dot(hb, w_ref[0])
        qkv_ref[:, 0:IN_S] = z0.astype(BF16)
        z1 = _dot(hb, w_ref[1])
        qkv_ref[:, IN_S:3 * D_ATT] = z1[:, 0:384].astype(BF16)
        zs_ref[:, 0:768] = z1[:, 384:IN_S]
        z2 = _dot(hb, w_ref[2])
        zs_ref[:, 768:1024] = z2[:, 0:256]
        gl_ref[:, 0:896] = z2[:, 256:IN_S]
        gl_ref[:, 896:2048] = _dot(hb, w_ref[3])

    return _call(
        body, payload, name="in_fwd", grid=(T // TM,), when=_edges(T // TM), sem=("arbitrary",),
        in_specs=[_row_spec(TM, D_MODEL), _const_spec((1, D_MODEL)), _const_spec(w_in.shape)],
        out_specs=[_row_spec(TM, D_MODEL), _row_spec(TM, 3 * D_ATT), _row_spec(TM, 2 * D_SGU), _row_spec(TM, 2 * D_MODEL)],
        out_shape=[jax.ShapeDtypeStruct((T, D_MODEL), BF16), jax.ShapeDtypeStruct((T, 3 * D_ATT), BF16),
                   jax.ShapeDtypeStruct((T, 2 * D_SGU), F32), jax.ShapeDtypeStruct((T, 2 * D_MODEL), F32)],
        operands=(x, g, w_in))


def _in_dgrad(dx_res, x, g, w_in, dq, dk, dv, dzs, dgl):
    T = x.shape[0]

    def body(dxr_ref, x_ref, g_ref, w_ref, dq_ref, dk_ref, dv_ref, dzs_ref, dgl_ref, dx_ref, dz_ref, dg_ref):
        dz = jnp.concatenate([dq_ref[...], dk_ref[...].astype(BF16), dv_ref[...].astype(BF16), dzs_ref[...], dgl_ref[...]],
                             axis=1)
        dz_ref[...] = dz
        dh = jnp.zeros((TM, D_MODEL), F32)
        for s in range(N_SHARD):
            dh += _dot_nt(dz[:, s * IN_S:(s + 1) * IN_S], w_ref[s])
        gv = g_ref[...]
        xhat, r, _ = _rms_fwd(x_ref[...], gv)
        dxn, dg = _rms_bwd(dh, xhat, r, gv)
        dx_ref[...] = dxr_ref[...] + dxn

        @pl.when(pl.program_id(0) == 0)
        def _():
            dg_ref[...] = jnp.zeros_like(dg_ref)

        dg_ref[...] += dg

    pad_blocks = ATT_PAD // TM
    return pl.pallas_call(
        body, name="in_dgrad", grid=(T // TM,),
        in_specs=[_row_spec(TM, D_MODEL), _row_spec(TM, D_MODEL), _const_spec((1, D_MODEL)), _const_spec(w_in.shape),
                  _row_spec(TM, D_ATT), _row_spec(TM, D_ATT, pad_blocks), _row_spec(TM, D_ATT, pad_blocks),
                  _row_spec(TM, 2 * D_SGU), _row_spec(TM, 2 * D_MODEL)],
        out_specs=[_row_spec(TM, D_MODEL), _row_spec(TM, D_IN), _acc_spec((1, D_MODEL))],
        out_shape=[jax.ShapeDtypeStruct((T, D_MODEL), F32), jax.ShapeDtypeStruct((T, D_IN), BF16),
                   jax.ShapeDtypeStruct((1, D_MODEL), F32)],
        compiler_params=_params(("arbitrary",)),
    )(dx_res, x, g, w_in, dq, dk, dv, dzs, dgl)


def _in_wgrad(h, dz):
    T = h.shape[0]

    def body(h_ref, dz_ref, gw_ref):
        @pl.when(pl.program_id(1) == 0)
        def _():
            gw_ref[...] = jnp.zeros_like(gw_ref)

        gw_ref[0] += _dot_tn(h_ref[...], dz_ref[...])

    return pl.pallas_call(
        body, name="in_wgrad", grid=(N_SHARD, T // TW),
        in_specs=[pl.BlockSpec((TW, D_MODEL), lambda s, i: (i, 0)), pl.BlockSpec((TW, IN_S), lambda s, i: (i, s))],
        out_specs=pl.BlockSpec((1, D_MODEL, IN_S), lambda s, i: (s, 0, 0)),
        out_shape=jax.ShapeDtypeStruct((N_SHARD, D_MODEL, IN_S), F32),
        compiler_params=_params(("arbitrary", "arbitrary")),
    )(h, dz)


def _rel_onehot():
    r = lax.broadcasted_iota(jnp.int32, (REL_PAD, REL_PAD), 0)
    n = lax.broadcasted_iota(jnp.int32, (REL_PAD, REL_PAD), 1)
    idx = jnp.clip(BAND - 1 - n, -REL_CLIP, REL_CLIP) + REL_CLIP
    return jnp.where(r == idx, 1.0, 0.0).astype(BF16)


def _split3(v):
    p1 = v.astype(BF16)
    r1 = v - p1.astype(F32)
    p2 = r1.astype(BF16)
    p3 = (r1 - p2.astype(F32)).astype(BF16)
    return p1, p2, p3


def _relbias_fwd(tab_pad):
    def body(t_ref, o_ref):
        oh = _rel_onehot()
        acc = jnp.zeros((HEADS, REL_PAD), F32)
        for p in _split3(t_ref[...]):
            acc += _dot(p, oh)
        o_ref[...] = acc

    return pl.pallas_call(body, name="relbias_fwd", out_shape=jax.ShapeDtypeStruct((HEADS, REL_PAD), F32))(tab_pad)


def _relbias_bwd(z):
    def body(z_ref, o_ref):
        oh = _rel_onehot()
        dt2 = jnp.sum(z_ref[...], axis=1)
        acc = jnp.zeros((HEADS, REL_PAD), F32)
        for p in _split3(dt2):
            acc += _dot_nt(p, oh)
        o_ref[...] = acc

    return pl.pallas_call(body, name="relbias_bwd", out_shape=jax.ShapeDtypeStruct((HEADS, REL_PAD), F32))(z)


def _bias_blocks(t2):
    flat = jnp.tile(t2, (1, CHUNK))
    skew = flat[:, :CHUNK * (REL_PAD - 1)].reshape(HEADS, CHUNK, REL_PAD - 1)
    bias = skew[:, :, CHUNK - 1:CHUNK - 1 + BAND]
    slabs = [jnp.pad(bias, ((0, 0), (0, 0), (CHUNK * c, ATT_KEYS - BAND - CHUNK * c)), constant_values=NEG_INF)
             for c in range(2)]
    return jnp.concatenate(slabs, axis=1)


def _unskew(db2):
    out = []
    for c in range(2):
        slab = db2[:, CHUNK * c:CHUNK * (c + 1), CHUNK * c:CHUNK * c + BAND]
        y = jnp.pad(slab, ((0, 0), (0, 0), (CHUNK - 1, REL_PAD - BAND - CHUNK + 1)))
        yf = jnp.pad(y.reshape(HEADS, CHUNK * REL_PAD), ((0, 0), (0, CHUNK)))
        out.append(yf.reshape(HEADS, CHUNK, REL_PAD + 1)[:, :, :REL_PAD])
    return jnp.concatenate(out, axis=1)


def _att_load(qkv_hbm, q_s, k_s, v_s, sem, T):
    copies = [pltpu.make_async_copy(qkv_hbm.at[:, 0:D_ATT], q_s, sem.at[0]),
              pltpu.make_async_copy(qkv_hbm.at[:, D_ATT:2 * D_ATT], k_s.at[pl.ds(ATT_PAD, T), :], sem.at[1]),
              pltpu.make_async_copy(qkv_hbm.at[:, 2 * D_ATT:3 * D_ATT], v_s.at[pl.ds(ATT_PAD, T), :], sem.at[2])]
    for cp in copies:
        cp.start()
    k_s[0:ATT_PAD, :] = jnp.zeros((ATT_PAD, D_ATT), BF16)
    v_s[0:ATT_PAD, :] = jnp.zeros((ATT_PAD, D_ATT), BF16)
    for cp in copies:
        cp.wait()


def _head(v, h):
    return v[:, h * HEAD_DIM:(h + 1) * HEAD_DIM]


def _rows(v, h):
    return v[h * ATT_ROWS:(h + 1) * ATT_ROWS]


def _att_exp(qs, kw, bias_ref, valid):
    s = jnp.concatenate([_dot_nt(_head(qs, h), _head(kw, h)) + bias_ref[h] for h in range(HEADS)], axis=0)
    if valid is not None:
        s = jnp.where(valid, s, NEG_INF)
    e = jnp.exp(s - jnp.max(s, axis=-1, keepdims=True))
    return e, 1.0 / jnp.sum(e, axis=-1, keepdims=True)


def _att_blocks(T, block):
    n_edge = min(ATT_PAD // ATT_ROWS, T // ATT_ROWS)

    def edge(i, carry):
        r0 = i * ATT_ROWS
        block(i, (lax.broadcasted_iota(jnp.int32, (1, ATT_KEYS), 1) + (r0 - ATT_PAD)) >= 0)
        return carry

    def inner(i, carry):
        block(i, None)
        return carry

    lax.fori_loop(0, n_edge, edge, 0)
    lax.fori_loop(n_edge, T // ATT_ROWS, inner, 0)


def _att_fwd(qkv, bias2, payload=None):
    T = qkv.shape[0]

    def body(qkv_hbm, bias_ref, y_ref, q_s, k_s, v_s, sem):
        _att_load(qkv_hbm, q_s, k_s, v_s, sem, T)

        def block(i, valid):
            r0 = pl.multiple_of(i * ATT_ROWS, ATT_ROWS)
            qs = q_s[pl.ds(r0, ATT_ROWS), :] * (HEAD_DIM ** -0.5)
            kw = k_s[pl.ds(r0, ATT_KEYS), :]
            vw = v_s[pl.ds(r0, ATT_KEYS), :]
            e, rinv = _att_exp(qs, kw, bias_ref, valid)
            eb = e.astype(BF16)
            outs = [_dot(_rows(eb, h), _head(vw, h)) * _rows(rinv, h) for h in range(HEADS)]
            y_ref[pl.ds(r0, ATT_ROWS), :] = jnp.concatenate(outs, axis=1).astype(BF16)

        _att_blocks(T, block)

    return _call(
        body, payload, name="att_fwd", grid=None,
        in_specs=[pl.BlockSpec(memory_space=pl.ANY), pl.BlockSpec(memory_space=pltpu.VMEM)],
        out_specs=[pl.BlockSpec(memory_space=pltpu.VMEM)],
        out_shape=[jax.ShapeDtypeStruct((T, D_ATT), BF16)],
        scratch_shapes=[pltpu.VMEM((T, D_ATT), BF16), pltpu.VMEM((T + ATT_PAD, D_ATT), BF16),
                        pltpu.VMEM((T + ATT_PAD, D_ATT), BF16), pltpu.SemaphoreType.DMA((3,))],
        operands=(qkv, bias2))


def _att_bwd(qkv, dy, bias2, payload=None):
    T = qkv.shape[0]

    def body(qkv_hbm, dy_ref, bias_ref, dq_ref, dk_ref, dv_ref, db_ref, q_s, k_s, v_s, sem):
        _att_load(qkv_hbm, q_s, k_s, v_s, sem, T)
        dk_ref[...] = jnp.zeros_like(dk_ref)
        dv_ref[...] = jnp.zeros_like(dv_ref)
        db_ref[...] = jnp.zeros_like(db_ref)


        def block(i, valid):
            r0 = pl.multiple_of(i * ATT_ROWS, ATT_ROWS)
            qs = q_s[pl.ds(r0, ATT_ROWS), :] * (HEAD_DIM ** -0.5)
            kw = k_s[pl.ds(r0, ATT_KEYS), :]
            vw = v_s[pl.ds(r0, ATT_KEYS), :]
            dyb = dy_ref[pl.ds(r0, ATT_ROWS), :]
            e, rinv = _att_exp(qs, kw, bias_ref, valid)
            p = e * rinv
            dp = jnp.concatenate([_dot_nt(_head(dyb, h), _head(vw, h)) for h in range(HEADS)], axis=0)
            ds = p * (dp - jnp.sum(p * dp, axis=-1, keepdims=True))
            db_ref[...] += ds.reshape(HEADS, ATT_ROWS, ATT_KEYS)
            dsb = ds.astype(BF16)
            pb = p.astype(BF16)
            dq = [_dot(_rows(dsb, h), _head(kw, h)) for h in range(HEADS)]
            dk = [_dot_tn(_rows(dsb, h), _head(qs, h)) for h in range(HEADS)]
            dv = [_dot_tn(_rows(pb, h), _head(dyb, h)) for h in range(HEADS)]
            dq_ref[pl.ds(r0, ATT_ROWS), :] = (jnp.concatenate(dq, axis=1) * (HEAD_DIM ** -0.5)).astype(BF16)
            dk_ref[pl.ds(r0, ATT_KEYS), :] += jnp.concatenate(dk, axis=1)
            dv_ref[pl.ds(r0, ATT_KEYS), :] += jnp.concatenate(dv, axis=1)

        _att_blocks(T, block)

    vmem = pl.BlockSpec(memory_space=pltpu.VMEM)
    return _call(
        body, payload, name="att_bwd", grid=None,
        in_specs=[pl.BlockSpec(memory_space=pl.ANY), vmem, vmem],
        out_specs=[vmem, vmem, vmem, vmem],
        out_shape=[jax.ShapeDtypeStruct((T, D_ATT), BF16), jax.ShapeDtypeStruct((T + ATT_PAD, D_ATT), F32),
                   jax.ShapeDtypeStruct((T + ATT_PAD, D_ATT), F32), jax.ShapeDtypeStruct((HEADS, ATT_ROWS, ATT_KEYS), F32)],
        scratch_shapes=[pltpu.VMEM((T, D_ATT), BF16), pltpu.VMEM((T + ATT_PAD, D_ATT), BF16),
                        pltpu.VMEM((T + ATT_PAD, D_ATT), BF16), pltpu.SemaphoreType.DMA((3,))],
        operands=(qkv, dy, bias2))


_GELU_C = 0.7978845608028654
_GELU_A = 0.044715


def _gelu(x):
    t = jnp.tanh(_GELU_C * (x + _GELU_A * x * x * x))
    return 0.5 * x * (1.0 + t), t


def _gelu_grad(x, t):
    return 0.5 * (1.0 + t) + 0.5 * x * (1.0 - t * t) * _GELU_C * (1.0 + 3.0 * _GELU_A * x * x)


def _group_masks():
    col = lax.broadcasted_iota(jnp.int32, (SGU_GROUPS, D_SGU), 1) // SGU_GDIM
    grp = lax.broadcasted_iota(jnp.int32, (SGU_GROUPS, D_SGU), 0)
    return jnp.where(col == grp, 1.0, 0.0).astype(F32)


def _causal_mask(transposed=False):
    i = lax.broadcasted_iota(jnp.int32, (SGU_BLOCK, SGU_BLOCK), 0) // CHUNK
    j = lax.broadcasted_iota(jnp.int32, (SGU_BLOCK, SGU_BLOCK), 1) // CHUNK
    return (j >= i) if transposed else (i >= j)


def _sgu_norm(zs, lng, lnb):
    gz, t = _gelu(zs)
    u = gz[:, 0:D_SGU]
    vs = gz[:, D_SGU:2 * D_SGU]
    xc = vs - jnp.mean(vs, axis=-1, keepdims=True)
    rstd = lax.rsqrt(jnp.mean(xc * xc, axis=-1, keepdims=True) + EPS)
    xhat = xc * rstd
    return t, u, xhat, rstd, xhat * lng + lnb


def _sgu_mix(vn_blk, w_ref, bst, gm):
    mask = _causal_mask()
    s = jnp.zeros((SGU_BLOCK, D_SGU), F32)
    for g in range(SGU_GROUPS):
        wm = jnp.where(mask, w_ref[g], 0.0).astype(BF16)
        s += _dot(wm, (vn_blk * gm[g:g + 1, :]).astype(BF16))
        s += bst[:, g:g + 1] * gm[g:g + 1, :]
    return s


def _sgu_fwd(zs, lng, lnb, w_s, bst):
    T = zs.shape[0]
    nblk = TM // SGU_BLOCK

    def body(zs_ref, lng_ref, lnb_ref, w_ref, bst_ref, y_ref):
        _, u, _, _, vn = _sgu_norm(zs_ref[...], lng_ref[...], lnb_ref[...])
        gm = _group_masks()
        bst_v = bst_ref[...]
        for n in range(nblk):
            rows = slice(n * SGU_BLOCK, (n + 1) * SGU_BLOCK)
            s = _sgu_mix(vn[rows], w_ref, bst_v, gm)
            y_ref[rows, :] = (u[rows] * s).astype(BF16)

    return pl.pallas_call(
        body, name="sgu_fwd", grid=(T // TM,),
        in_specs=[_row_spec(TM, 2 * D_SGU), _const_spec((1, D_SGU)), _const_spec((1, D_SGU)),
                  _const_spec(w_s.shape), _const_spec(bst.shape)],
        out_specs=_row_spec(TM, D_SGU),
        out_shape=jax.ShapeDtypeStruct((T, D_SGU), BF16),
        compiler_params=_params(("arbitrary",)),
    )(zs, lng, lnb, w_s, bst)


def _sgu_bwd(zs, dy, lng, lnb, w_s, w_st, bst):
    T = zs.shape[0]
    nblk = TM // SGU_BLOCK

    def body(zs_ref, dy_ref, lng_ref, lnb_ref, w_ref, wt_ref, bst_ref, dzs_ref, dw_ref, dbt_ref, dlg_ref, dlb_ref):
        @pl.when(pl.program_id(0) == 0)
        def _():
            dw_ref[...] = jnp.zeros_like(dw_ref)
            dbt_ref[...] = jnp.zeros_like(dbt_ref)
            dlg_ref[...] = jnp.zeros_like(dlg_ref)
            dlb_ref[...] = jnp.zeros_like(dlb_ref)

        zs_v = zs_ref[...]
        lng_v = lng_ref[...]
        t, u, xhat, rstd, vn = _sgu_norm(zs_v, lng_v, lnb_ref[...])
        gm = _group_masks()
        bst_v = bst_ref[...]
        mask = _causal_mask()
        mask_t = _causal_mask(transposed=True)
        dyv = dy_ref[...].astype(F32)
        lane8 = lax.broadcasted_iota(jnp.int32, (1, SGU_GROUPS), 1)
        du_rows, dvn_rows = [], []
        for n in range(nblk):
            rows = slice(n * SGU_BLOCK, (n + 1) * SGU_BLOCK)
            vn_b = vn[rows]
            s = _sgu_mix(vn_b, w_ref, bst_v, gm)
            du_rows.append(dyv[rows] * s)
            dsb = dyv[rows] * u[rows]
            vnb16 = vn_b.astype(BF16)
            dvn = jnp.zeros((SGU_BLOCK, D_SGU), F32)
            dbt = jnp.zeros((SGU_BLOCK, SGU_GROUPS), F32)
            for g in range(SGU_GROUPS):
                dsg = dsb * gm[g:g + 1, :]
                dsg16 = dsg.astype(BF16)
                wmt = jnp.where(mask_t, wt_ref[g], 0.0).astype(BF16)
                dvn += _dot(wmt, dsg16)
                dw_ref[g] += jnp.where(mask, _dot_nt(dsg16, vnb16), 0.0)
                dbt += jnp.sum(dsg, axis=-1, keepdims=True) * jnp.where(lane8 == g, 1.0, 0.0)
            dbt_ref[...] += dbt
            dvn_rows.append(dvn)
        du = jnp.concatenate(du_rows, axis=0)
        dvn = jnp.concatenate(dvn_rows, axis=0)
        dlg_ref[...] += jnp.sum(dvn * xhat, axis=0, keepdims=True)
        dlb_ref[...] += jnp.sum(dvn, axis=0, keepdims=True)
        dxhat = dvn * lng_v
        dvs = rstd * (dxhat - jnp.mean(dxhat, axis=-1, keepdims=True)
                      - xhat * jnp.mean(dxhat * xhat, axis=-1, keepdims=True))
        dgz = jnp.concatenate([du, dvs], axis=1)
        dzs_ref[...] = (dgz * _gelu_grad(zs_v, t)).astype(BF16)

    return pl.pallas_call(
        body, name="sgu_bwd", grid=(T // TM,),
        in_specs=[_row_spec(TM, 2 * D_SGU), _row_spec(TM, D_SGU), _const_spec((1, D_SGU)), _const_spec((1, D_SGU)),
                  _const_spec(w_s.shape), _const_spec(w_st.shape), _const_spec(bst.shape)],
        out_specs=[_row_spec(TM, 2 * D_SGU), _acc_spec(w_s.shape), _acc_spec(bst.shape), _acc_spec((1, D_SGU)),
                   _acc_spec((1, D_SGU))],
        out_shape=[jax.ShapeDtypeStruct((T, 2 * D_SGU), BF16), jax.ShapeDtypeStruct(w_s.shape, F32),
                   jax.ShapeDtypeStruct(bst.shape, F32), jax.ShapeDtypeStruct((1, D_SGU), F32),
                   jax.ShapeDtypeStruct((1, D_SGU), F32)],
        compiler_params=_params(("arbitrary",)),
    )(zs, dy, lng, lnb, w_s, w_st, bst)


def _cols(v, s):
    return v[:, s * BR_S:(s + 1) * BR_S]


def _merge_fwd(x, y_att, y_sgu, gl, b_gate, wba, wbs, wo, payload=None):
    T = x.shape[0]

    def body(x_ref, ya_ref, ys_ref, gl_ref, bg_ref, wba_ref, wbs_ref, wo_ref, xo_ref, m_ref, pa_ref, ps_ref):
        ya = ya_ref[...]
        ys = ys_ref[...]
        pa = jnp.concatenate([_dot(ya, wba_ref[s]) for s in range(N_SHARD)], axis=1)
        ps = jnp.concatenate([_dot(ys, wbs_ref[s]) for s in range(N_SHARD)], axis=1)
        g = _sigmoid(gl_ref[...] + bg_ref[...])
        mb = (g[:, 0:D_MODEL] * pa + g[:, D_MODEL:2 * D_MODEL] * ps).astype(BF16)
        m_ref[...] = mb
        pa_ref[...] = pa.astype(BF16)
        ps_ref[...] = ps.astype(BF16)
        acc = jnp.zeros((TM, D_MODEL), F32)
        for s in range(N_SHARD):
            acc += _dot(_cols(mb, s), wo_ref[s])
        xo_ref[...] = x_ref[...] + acc

    tokd = jax.ShapeDtypeStruct((T, D_MODEL), BF16)
    return _call(
        body, payload, name="merge_fwd", grid=(T // TM,), when=_edges(T // TM), sem=("arbitrary",),
        in_specs=[_row_spec(TM, D_MODEL), _row_spec(TM, D_ATT), _row_spec(TM, D_SGU), _row_spec(TM, 2 * D_MODEL),
                  _const_spec((1, 2 * D_MODEL)), _const_spec(wba.shape), _const_spec(wbs.shape), _const_spec(wo.shape)],
        out_specs=[_row_spec(TM, D_MODEL)] * 4,
        out_shape=[jax.ShapeDtypeStruct((T, D_MODEL), F32), tokd, tokd, tokd],
        operands=(x, y_att, y_sgu, gl, b_gate, wba, wbs, wo))


def _merge_bwd(dx, y_att, y_sgu, gl, merged, pa, ps, b_gate, wba, wbs, wo, payload=None):
    T = dx.shape[0]

    def body(dx_ref, ya_ref, ys_ref, gl_ref, m_ref, pa_ref, ps_ref, bg_ref, wba_ref, wbs_ref, wo_ref,
             dya_ref, dys_ref, dgl_ref, dbg_ref, gwba_ref, gwbs_ref, gwo_ref):
        @pl.when(pl.program_id(0) == 0)
        def _():
            dbg_ref[...] = jnp.zeros_like(dbg_ref)
            gwba_ref[...] = jnp.zeros_like(gwba_ref)
            gwbs_ref[...] = jnp.zeros_like(gwbs_ref)
            gwo_ref[...] = jnp.zeros_like(gwo_ref)

        dxb = dx_ref[...].astype(BF16)
        dm = jnp.concatenate([_dot_nt(dxb, wo_ref[s]) for s in range(N_SHARD)], axis=1)
        g = _sigmoid(gl_ref[...] + bg_ref[...])
        ga = g[:, 0:D_MODEL]
        gs = g[:, D_MODEL:2 * D_MODEL]
        dpa = (dm * ga).astype(BF16)
        dps = (dm * gs).astype(BF16)
        dgl = jnp.concatenate([dm * pa_ref[...].astype(F32) * ga * (1.0 - ga),
                               dm * ps_ref[...].astype(F32) * gs * (1.0 - gs)], axis=1)
        dgl_ref[...] = dgl.astype(BF16)
        dbg_ref[...] += jnp.sum(dgl, axis=0, keepdims=True)
        ya = ya_ref[...]
        ys = ys_ref[...]
        mb = m_ref[...]
        dya = jnp.zeros((TM, D_ATT), F32)
        dys = jnp.zeros((TM, D_SGU), F32)
        for s in range(N_SHARD):
            dya += _dot_nt(_cols(dpa, s), wba_ref[s])
            dys += _dot_nt(_cols(dps, s), wbs_ref[s])
            gwo_ref[s] += _dot_tn(_cols(mb, s), dxb)
            gwba_ref[s] += _dot_tn(ya, _cols(dpa, s))
            gwbs_ref[s] += _dot_tn(ys, _cols(dps, s))
        dya_ref[...] = dya.astype(BF16)
        dys_ref[...] = dys.astype(BF16)

    return _call(
        body, payload, name="merge_bwd", grid=(T // TM,), when=_edges(T // TM), sem=("arbitrary",),
        operands=(dx, y_att, y_sgu, gl, merged, pa, ps, b_gate, wba, wbs, wo),
        in_specs=[_row_spec(TM, D_MODEL), _row_spec(TM, D_ATT), _row_spec(TM, D_SGU), _row_spec(TM, 2 * D_MODEL),
                  _row_spec(TM, D_MODEL), _row_spec(TM, D_MODEL), _row_spec(TM, D_MODEL),
                  _const_spec((1, 2 * D_MODEL)), _const_spec(wba.shape), _const_spec(wbs.shape), _const_spec(wo.shape)],
        out_specs=[_row_spec(TM, D_ATT), _row_spec(TM, D_SGU), _row_spec(TM, 2 * D_MODEL), _acc_spec((1, 2 * D_MODEL)),
                   _acc_spec(wba.shape), _acc_spec(wbs.shape), _acc_spec(wo.shape)],
        out_shape=[jax.ShapeDtypeStruct((T, D_ATT), BF16), jax.ShapeDtypeStruct((T, D_SGU), BF16),
                   jax.ShapeDtypeStruct((T, 2 * D_MODEL), BF16), jax.ShapeDtypeStruct((1, 2 * D_MODEL), F32),
                   jax.ShapeDtypeStruct(wba.shape, F32), jax.ShapeDtypeStruct(wbs.shape, F32),
                   jax.ShapeDtypeStruct(wo.shape, F32)])


def _loss_bwd(x, target, g):
    T = x.shape[0]

    def body(x_ref, t_ref, g_ref, dx_ref, loss_ref, dg_ref):
        @pl.when(pl.program_id(0) == 0)
        def _():
            loss_ref[...] = jnp.zeros_like(loss_ref)
            dg_ref[...] = jnp.zeros_like(dg_ref)

        gv = g_ref[...]
        xhat, r, y = _rms_fwd(x_ref[...], gv)
        err = y - t_ref[...]
        per_tok = jnp.mean(err * err, axis=-1, keepdims=True)
        loss_ref[...] += 0.5 * jnp.sum(per_tok, axis=0, keepdims=True)
        dxn, dg = _rms_bwd(err * (1.0 / D_MODEL), xhat, r, gv)
        dx_ref[...] = dxn
        dg_ref[...] += dg

    return pl.pallas_call(
        body, name="loss_bwd", grid=(T // TM,),
        in_specs=[_row_spec(TM, D_MODEL), _row_spec(TM, D_MODEL), _const_spec((1, D_MODEL))],
        out_specs=[_row_spec(TM, D_MODEL), _acc_spec((1, 128)), _acc_spec((1, D_MODEL))],
        out_shape=[jax.ShapeDtypeStruct((T, D_MODEL), F32), jax.ShapeDtypeStruct((1, 128), F32),
                   jax.ShapeDtypeStruct((1, D_MODEL), F32)],
        compiler_params=_params(("arbitrary",)),
    )(x, target, g)


BIG = ("ffn1_w_gate", "ffn1_w_up", "ffn1_w_down", "w_in", "w_branch_att", "w_branch_sgu", "w_out",
       "ffn2_w_gate", "ffn2_w_up", "ffn2_w_down")
SMALL = ("norm_ffn1", "norm_mix", "b_gate", "rel_bias", "sgu_ln_g", "sgu_ln_b", "sgu_w_s", "sgu_b_s", "norm_ffn2",
         "norm_final")


G_FFN1 = ("ffn1_w_gate", "ffn1_w_up", "ffn1_w_down")
G_MIX = ("w_in", "w_branch_att", "w_branch_sgu", "w_out")
G_FFN2 = ("ffn2_w_gate", "ffn2_w_up", "ffn2_w_down")


def _local_step(x, target, wb, ws, dist=None):
    def gather_on(names):
        return _ag_payload([wb[n] for n in names]) if dist else None

    t2 = _relbias_fwd(ws["rel_bias"])
    bias2 = _bias_blocks(t2)
    bst = ws["sgu_b_s"].T
    w_st = jnp.swapaxes(ws["sgu_w_s"], 1, 2)

    if dist:
        x1, h1, a1, b1, *got = _ffn_fwd_gather(x, ws["norm_ffn1"], [wb[n] for n in G_FFN1], dist[1], gather_on(G_MIX))
        wb.update(zip(G_FFN1 + G_MIX, got))
    else:
        x1, h1, a1, b1 = _ffn_fwd(x, ws["norm_ffn1"], wb["ffn1_w_gate"], wb["ffn1_w_up"], wb["ffn1_w_down"], "ffn1_fwd")
    h2, qkv, zs, gl, *got = _in_fwd(x1, ws["norm_mix"], wb["w_in"], gather_on(G_FFN2[0:1]))
    wb.update(zip(G_FFN2[0:1], got))
    y_att, *got = _att_fwd(qkv, bias2, gather_on(G_FFN2[1:2]))
    wb.update(zip(G_FFN2[1:2], got))
    y_sgu = _sgu_fwd(zs, ws["sgu_ln_g"], ws["sgu_ln_b"], ws["sgu_w_s"], bst)
    x2, merged, pa, ps, *got = _merge_fwd(x1, y_att, y_sgu, gl, ws["b_gate"], wb["w_branch_att"], wb["w_branch_sgu"],
                                          wb["w_out"], gather_on(G_FFN2[2:3]))
    wb.update(zip(G_FFN2[2:3], got))
    x3, h3, a3, b3 = _ffn_fwd(x2, ws["norm_ffn2"], wb["ffn2_w_gate"], wb["ffn2_w_up"], wb["ffn2_w_down"], "ffn2_fwd")
    dx3, loss, g_final = _loss_bwd(x3, target, ws["norm_final"])

    gb, gs, sums = {}, {"norm_final": g_final}, {}

    def pair_on(names, small=None):
        return _px_payload([gb[n] for n in names], small) if dist else None

    def pair_add(names, halves):
        for n, rv in zip(names, halves):
            sums[n] = _pair_add(gb[n], rv, dist[0], dist[1], "pair_add_" + n)

    def chips_on(names):
        return _cx_payload([sums[n][1] for n in names], [sums[n][2] for n in names]) if dist else None

    dx2, da3, db3, gs["norm_ffn2"] = _ffn_dgrad(dx3, x2, a3, b3, ws["norm_ffn2"], wb["ffn2_w_gate"], wb["ffn2_w_up"],
                                                wb["ffn2_w_down"], "ffn2_dgrad")
    gb["ffn2_w_gate"], gb["ffn2_w_up"], gb["ffn2_w_down"] = _ffn_wgrad(h3, dx3, a3, b3, da3, db3, "ffn2_wgrad")
    dy_att, dy_sgu, dgl, gs["b_gate"], gb["w_branch_att"], gb["w_branch_sgu"], gb["w_out"], *got = _merge_bwd(
        dx2, y_att, y_sgu, gl, merged, pa, ps, ws["b_gate"], wb["w_branch_att"], wb["w_branch_sgu"], wb["w_out"],
        pair_on(G_FFN2))
    pair_add(G_FFN2, got)
    dq, dk, dv, db2, *lands2 = _att_bwd(qkv, dy_att, bias2, chips_on(G_FFN2))
    gs["rel_bias"] = _relbias_bwd(_unskew(db2))
    dzs, gs["sgu_w_s"], dbt, gs["sgu_ln_g"], gs["sgu_ln_b"] = _sgu_bwd(zs, dy_sgu, ws["sgu_ln_g"], ws["sgu_ln_b"],
                                                                      ws["sgu_w_s"], w_st, bst)
    gs["sgu_b_s"] = dbt.T
    dx1, dz, gs["norm_mix"] = _in_dgrad(dx2, x1, ws["norm_mix"], wb["w_in"], dq, dk, dv, dzs, dgl)
    gb["w_in"] = _in_wgrad(h2, dz)
    gx, da1, db1, gs["norm_ffn1"], *got = _ffn_dgrad(dx1, x, a1, b1, ws["norm_ffn1"], wb["ffn1_w_gate"],
                                                    wb["ffn1_w_up"], wb["ffn1_w_down"], "ffn1_dgrad", pair_on(G_MIX))
    pair_add(G_MIX, got)
    gb["ffn1_w_gate"], gb["ffn1_w_up"], gb["ffn1_w_down"], *lands_mix = _ffn_wgrad(h1, dx1, a1, b1, da1, db1,
                                                                                   "ffn1_wgrad", chips_on(G_MIX))
    if not dist:
        return loss, gx, gb, gs

    tail = _tail_reduce([gb[n] for n in G_FFN1], _pack_small(gs, loss))
    for i, n in enumerate(G_FFN1):
        sums[n] = (tail[i],)
    lands1, small_sums = tail[len(G_FFN1):2 * len(G_FFN1)], tail[-1]
    lands = dict(zip(G_FFN2 + G_MIX + G_FFN1, list(lands2) + list(lands_mix) + list(lands1)))
    fulls = [_final_sum(sums[n][0], lands[n], dist[1], dist[0], "final_sum_" + n) for n in BIG]
    return loss, gx, dict(zip(BIG, _sibling_share(fulls, "sibling_share"))), small_sums


_ANY = pl.BlockSpec(memory_space=pl.ANY)
_VMEM = pl.BlockSpec(memory_space=pltpu.VMEM)


def _mesh_pos():
    return lax.axis_index("x"), lax.axis_index("y"), lax.axis_index("c")


def _cast_slots(shards, chip, name):
    n = len(shards)
    r, ncol = shards[0].shape
    tr = r // 2

    def body(me_ref, *refs):
        for i_ref, o_ref in zip(refs[:n], refs[n:]):
            o_ref[0] = i_ref[...].astype(BF16)

    grid_spec = pltpu.PrefetchScalarGridSpec(
        num_scalar_prefetch=1, grid=(r // tr,),
        in_specs=[pl.BlockSpec((tr, ncol), lambda i, me: (i, 0))] * n,
        out_specs=[pl.BlockSpec((1, tr, ncol), lambda i, me: (me[0], i, 0))] * n)
    return pl.pallas_call(
        body, name=name, grid_spec=grid_spec,
        out_shape=[jax.ShapeDtypeStruct((N_SHARD, r, ncol), BF16)] * n,
        compiler_params=_params(("arbitrary",)),
    )(chip, *shards)


class _Payload:
    def __init__(self, arrays, out_shapes, aliases, scratch, phases):
        self.arrays = list(arrays)
        self.out_shapes = list(out_shapes)
        self.aliases = dict(aliases)
        self.scratch = list(scratch)
        self.phases = phases


def _remote(src, dst, ssem, rsem, dev):
    return pltpu.make_async_remote_copy(src_ref=src, dst_ref=dst, send_sem=ssem, recv_sem=rsem, device_id=dev,
                                        device_id_type=MESH)


def _call(body, payload, *, name, grid, in_specs, out_specs, out_shape, scratch_shapes=(), sem=None, when=None,
          operands=()):
    in_specs, out_specs, out_shape = list(in_specs), list(out_specs), list(out_shape)
    scratch_shapes = list(scratch_shapes)
    n_in, n_out, n_scr = len(in_specs), len(out_specs), len(scratch_shapes)
    kwargs = {}
    kernel = body
    if payload is not None:
        k_in, k_out = len(payload.arrays), len(payload.out_shapes)
        rank = len(grid) if grid else 0

        def kernel(*refs):
            a, b = n_in, n_in + k_in
            c, d = b + n_out, b + n_out + k_out
            e = d + n_scr
            phases = payload.phases(refs[a:b], refs[c:d], refs[e:])

            def run():
                body(*refs[:a], *refs[b:c], *refs[d:e])

            if not grid:
                phases[0]()
                run()
                for ph in phases[1:]:
                    ph()
                return
            step = pl.program_id(0)
            if rank == 2:
                step = step * grid[1] + pl.program_id(1)
            for ph, (at, before) in zip(phases, when):
                if before:
                    pl.when(step == at)(ph)
            run()
            for ph, (at, before) in zip(phases, when):
                if not before:
                    pl.when(step == at)(ph)

        in_specs += [_ANY] * k_in
        out_specs += [_ANY] * k_out
        out_shape += payload.out_shapes
        scratch_shapes += payload.scratch
        kwargs["input_output_aliases"] = {n_in + i: n_out + j for i, j in payload.aliases.items()}
        operands = tuple(operands) + tuple(payload.arrays)
    if grid:
        kwargs["grid"] = grid
    return pl.pallas_call(kernel, name=name, in_specs=in_specs, out_specs=out_specs, out_shape=out_shape,
                          scratch_shapes=scratch_shapes, compiler_params=_params(sem), **kwargs)(*operands)


def _ag_payload(slots):
    n = len(slots)

    def phases(_, refs, sems):
        send_i, recv_i, send_d, recv_d = sems
        x, y, c = _mesh_pos()
        me = 2 * x + y

        def half(w, core):
            rh = slots[w].shape[1] // 2
            return pl.ds(core * rh, rh)

        def ici(w, j):
            t = (me + 1 + j) % N_SHARD
            mine = refs[w].at[me, half(w, c), :]
            return _remote(mine, mine, send_i.at[3 * w + j], recv_i.at[3 * w + j], (t // 2, t % 2, c))

        def d2d(w, j, core):
            s = (me + 3 - j) % N_SHARD
            land = refs[w].at[s, half(w, core), :]
            return _remote(land, land, send_d.at[3 * w + j], recv_d.at[3 * w + j], (x, y, 1 - c))

        def start():
            for w in range(n):
                for j in range(3):
                    ici(w, j).start()

        def finish():
            for w in range(n):
                for j in range(3):
                    s = (me + 3 - j) % N_SHARD
                    land = refs[w].at[s, half(w, c), :]
                    _remote(land, land, send_i.at[3 * w + j], recv_i.at[3 * w + j], (x, y, c)).wait_recv()
                    d2d(w, j, c).start()
            for w in range(n):
                for j in range(3):
                    d2d(w, j, 1 - c).wait_recv()
            for w in range(n):
                for j in range(3):
                    ici(w, j).wait_send()
                    d2d(w, j, c).wait_send()

        return [start, finish]

    return _Payload(slots, [jax.ShapeDtypeStruct(s.shape, s.dtype) for s in slots], {i: i for i in range(n)},
                    [pltpu.SemaphoreType.DMA((3 * n,)) for _ in range(4)], phases)


def _px_payload(grads, small=None):
    arrays = list(grads) + ([small] if small is not None else [])
    n = len(arrays)

    def phases(ins, outs, sems):
        send, recv = sems
        x, y, c = _mesh_pos()

        def copy(w):
            if w < len(grads):
                rh = grads[w].shape[1] // 2
                src = ins[w].at[:, pl.ds((1 - c) * rh, rh), :]
            else:
                src = ins[w]
            return _remote(src, outs[w], send.at[w], recv.at[w], (x, y, 1 - c))

        def start():
            for w in range(n):
                copy(w).start()

        def finish():
            for w in range(n):
                copy(w).wait()

        return [start, finish]

    out_shapes = [jax.ShapeDtypeStruct((N_SHARD, g.shape[1] // 2, g.shape[2]), F32) for g in grads]
    if small is not None:
        out_shapes.append(jax.ShapeDtypeStruct(small.shape, F32))
    return _Payload(arrays, out_shapes, {}, [pltpu.SemaphoreType.DMA((n,)), pltpu.SemaphoreType.DMA((n,))], phases)


def _cx_payload(pbs, lands):
    n = len(pbs)

    def phases(ins, outs, sems):
        send, recv = sems
        x, y, c = _mesh_pos()
        me = 2 * x + y

        def copy(w, j):
            t = (me + 1 + j) % N_SHARD
            return _remote(ins[w].at[t], outs[w].at[me], send.at[3 * w + j], recv.at[3 * w + j], (t // 2, t % 2, c))

        def start():
            for w in range(n):
                for j in range(3):
                    copy(w, j).start()

        def finish():
            for w in range(n):
                for j in range(3):
                    copy(w, j).wait()

        return [start, finish]

    return _Payload(list(pbs) + list(lands), [jax.ShapeDtypeStruct(p.shape, BF16) for p in lands],
                    {n + i: i for i in range(n)},
                    [pltpu.SemaphoreType.DMA((3 * n,)), pltpu.SemaphoreType.DMA((3 * n,))], phases)


def _pair_add(g, rv, core, chip, name):
    _, r, ncol = g.shape
    rh = r // 2

    def body(c_ref, me_ref, g_ref, rv_ref, pf_ref, pb_ref, land_ref):
        s = g_ref[0] + rv_ref[0]
        sb = s.astype(BF16)
        pb_ref[0] = sb
        land_ref[0] = sb

        @pl.when(pl.program_id(0) == me_ref[0])
        def _():
            pf_ref[...] = s

    slot = pl.BlockSpec((1, rh, ncol), lambda s, c, me: (s, 0, 0))
    grid_spec = pltpu.PrefetchScalarGridSpec(
        num_scalar_prefetch=2, grid=(N_SHARD,),
        in_specs=[pl.BlockSpec((1, rh, ncol), lambda s, c, me: (s, c[0], 0)), slot],
        out_specs=[pl.BlockSpec((rh, ncol), lambda s, c, me: (0, 0)), slot, slot])
    return pl.pallas_call(
        body, name=name, grid_spec=grid_spec,
        out_shape=[jax.ShapeDtypeStruct((rh, ncol), F32), jax.ShapeDtypeStruct((N_SHARD, rh, ncol), BF16),
                   jax.ShapeDtypeStruct((N_SHARD, rh, ncol), BF16)],
        compiler_params=_params(("arbitrary",)),
    )(core, chip, g, rv)


def _tail_reduce(grads, small):
    n = len(grads)
    _, r, ncol = grads[0].shape
    rh = r // 2

    def body(*refs):
        g_hbm, sm = refs[:n], refs[n]
        pf, land, sm_out = refs[n + 1:2 * n + 1], refs[2 * n + 1:3 * n + 1], refs[3 * n + 1]
        scr = refs[3 * n + 2:]
        rv, mine, sendb = scr[:n], scr[n:2 * n], scr[2 * n:3 * n]
        sm_rv, sm_sum, d_send, d_recv, load, i_send, i_recv, store = scr[3 * n:]
        x, y, c = _mesh_pos()
        me = 2 * x + y
        sib = (x, y, 1 - c)

        def pair(w):
            src = g_hbm[w].at[:, pl.ds((1 - c) * rh, rh), :] if w < n else sm
            return _remote(src, rv[w] if w < n else sm_rv, d_send.at[w], d_recv.at[w], sib)

        def chips(w, j):
            t = (me + 1 + j) % N_SHARD
            src = sendb[w].at[t] if w < n else sm_sum
            dst = land[w].at[me] if w < n else sm_out.at[me]
            return _remote(src, dst, i_send.at[3 * w + j], i_recv.at[3 * w + j], (t // 2, t % 2, c))

        loads = [pltpu.make_async_copy(g_hbm[w].at[:, pl.ds(c * rh, rh), :], mine[w], load.at[w]) for w in range(n)]
        for w in range(n + 1):
            pair(w).start()
        for cp in loads:
            cp.start()
        stores = []
        for w in range(n):
            loads[w].wait()
            pair(w).wait_recv()
            for k in range(N_SHARD):
                s = mine[w][k] + rv[w][k]
                mine[w][k] = s
                sendb[w][k] = s.astype(BF16)
            stores += [pltpu.make_async_copy(mine[w].at[me], pf[w], store.at[2 * w]),
                       pltpu.make_async_copy(sendb[w].at[me], land[w].at[me], store.at[2 * w + 1])]
            for cp in stores[-2:]:
                cp.start()
            for j in range(3):
                chips(w, j).start()
        pair(n).wait_recv()
        sm_sum[...] = sm[...] + sm_rv[...]
        stores.append(pltpu.make_async_copy(sm_sum, sm_out.at[me], store.at[2 * n]))
        stores[-1].start()
        for j in range(3):
            chips(n, j).start()
        for w in range(n + 1):
            pair(w).wait_send()
            for j in range(3):
                chips(w, j).wait()
        for cp in stores:
            cp.wait()

    half = (N_SHARD, rh, ncol)
    return _call(
        body, None, name="tail_reduce", grid=None,
        in_specs=[_ANY] * n + [_VMEM], out_specs=[_ANY] * (2 * n + 1),
        out_shape=([jax.ShapeDtypeStruct((rh, ncol), F32)] * n + [jax.ShapeDtypeStruct(half, BF16)] * n
                   + [jax.ShapeDtypeStruct((N_SHARD,) + small.shape, F32)]),
        scratch_shapes=([pltpu.VMEM(half, F32)] * (2 * n) + [pltpu.VMEM(half, BF16)] * n
                        + [pltpu.VMEM(small.shape, F32), pltpu.VMEM(small.shape, F32),
                           pltpu.SemaphoreType.DMA((n + 1,)), pltpu.SemaphoreType.DMA((n + 1,)),
                           pltpu.SemaphoreType.DMA((n,)), pltpu.SemaphoreType.DMA((3 * n + 3,)),
                           pltpu.SemaphoreType.DMA((3 * n + 3,)), pltpu.SemaphoreType.DMA((2 * n + 1,))]),
        operands=(*grads, small))


def _final_sum(pf, land, chip, core, name):
    _, rh, ncol = land.shape

    def body(me_ref, c_ref, pf_ref, land_ref, o_ref):
        me = me_ref[0]
        acc = jnp.zeros((rh, ncol), F32)
        for k in range(N_SHARD):
            acc = acc + jnp.where(me == k, pf_ref[...], land_ref[k].astype(F32))
        o_ref[...] = acc

    grid_spec = pltpu.PrefetchScalarGridSpec(
        num_scalar_prefetch=2, grid=(1,),
        in_specs=[pl.BlockSpec((rh, ncol), lambda i, me, c: (0, 0)),
                  pl.BlockSpec((N_SHARD, rh, ncol), lambda i, me, c: (0, 0, 0))],
        out_specs=pl.BlockSpec((rh, ncol), lambda i, me, c: (c[0], 0)))
    return pl.pallas_call(
        body, name=name, grid_spec=grid_spec, out_shape=jax.ShapeDtypeStruct((2 * rh, ncol), F32),
        compiler_params=_params(("arbitrary",)),
    )(chip, core, pf, land)


def _sibling_share(fulls, name):
    n = len(fulls)

    def body(*refs):
        outs = refs[n:2 * n]
        send, recv = refs[2 * n:]
        x, y, c = _mesh_pos()
        cps = []
        for w in range(n):
            rh = fulls[w].shape[0] // 2
            mine = outs[w].at[pl.ds(c * rh, rh), :]
            cp = pltpu.make_async_remote_copy(src_ref=mine, dst_ref=mine, send_sem=send.at[w], recv_sem=recv.at[w],
                                              device_id=(x, y, 1 - c), device_id_type=MESH)
            cp.start()
            cps.append(cp)
        for cp in cps:
            cp.wait()

    return pl.pallas_call(
        body, name=name,
        in_specs=[_ANY] * n, out_specs=[_ANY] * n,
        out_shape=[jax.ShapeDtypeStruct(f.shape, F32) for f in fulls],
        input_output_aliases={i: i for i in range(n)},
        scratch_shapes=[pltpu.SemaphoreType.DMA((n,)), pltpu.SemaphoreType.DMA((n,))],
    )(*fulls)


_ROW = {"rel_bias": 128, "sgu_b_s": 136, "norm_ffn1": 144, "norm_mix": 145, "norm_ffn2": 146, "norm_final": 147,
        "b_gate": 148, "sgu_ln_g": 150, "sgu_ln_b": 151}


def _pack_small(gs, loss):
    def body(ws, rel, bs, n1, nm, n2, nf, bg, lg, lb, loss_ref, o_ref):
        o_ref[...] = jnp.zeros_like(o_ref)
        o_ref[LOSS_ROW:LOSS_ROW + 1, 0:128] = loss_ref[...]
        for g in range(SGU_GROUPS):
            o_ref[0:SGU_BLOCK, g * SGU_BLOCK:(g + 1) * SGU_BLOCK] = ws[g]
        o_ref[128:136, 0:REL_PAD] = rel[...]
        o_ref[136:144, 0:SGU_BLOCK] = bs[...]
        o_ref[144:145, :] = n1[...]
        o_ref[145:146, :] = nm[...]
        o_ref[146:147, :] = n2[...]
        o_ref[147:148, :] = nf[...]
        o_ref[148:149, :] = bg[:, 0:D_MODEL]
        o_ref[149:150, :] = bg[:, D_MODEL:2 * D_MODEL]
        o_ref[150:151, 0:D_SGU] = lg[...]
        o_ref[151:152, 0:D_SGU] = lb[...]

    order = ("sgu_w_s", "rel_bias", "sgu_b_s", "norm_ffn1", "norm_mix", "norm_ffn2", "norm_final", "b_gate", "sgu_ln_g",
             "sgu_ln_b")
    return pl.pallas_call(body, name="pack_small", out_shape=jax.ShapeDtypeStruct((SMALL_ROWS, D_MODEL), F32))(
        *[gs[k] for k in order], loss)


def _adam(w, g, m, v):
    m2 = ADAM_B1 * m + (1.0 - ADAM_B1) * g
    v2 = ADAM_B2 * v + (1.0 - ADAM_B2) * (g * g)
    m_hat = m2 / (1.0 - ADAM_B1 ** ADAM_STEP)
    v_hat = v2 / (1.0 - ADAM_B2 ** ADAM_STEP)
    delta = -ADAM_LR * (m_hat / (jnp.sqrt(v_hat) + ADAM_EPS) + ADAM_WD * w)
    return delta, m2, v2


def _adam_small(sin, w, m, v):
    names = SMALL
    k = len(names)

    def body(*refs):
        sin_ref = refs[0]
        w_r, m_r, v_r = refs[1:1 + k], refs[1 + k:1 + 2 * k], refs[1 + 2 * k:1 + 3 * k]
        outs = refs[1 + 3 * k:]
        tot = sin_ref[0] + sin_ref[1] + sin_ref[2] + sin_ref[3]
        outs[4 * k][...] = tot[LOSS_ROW:LOSS_ROW + 1, 0:128]
        for i, name in enumerate(names):
            o = outs[4 * i:4 * i + 4]
            if name == "sgu_w_s":
                for gi in range(SGU_GROUPS):
                    g = tot[0:SGU_BLOCK, gi * SGU_BLOCK:(gi + 1) * SGU_BLOCK]
                    res = (g,) + _adam(w_r[i][gi], g, m_r[i][gi], v_r[i][gi])
                    for ref, val in zip(o, res):
                        ref[gi] = val
                continue
            r0 = _ROW[name]
            if name == "rel_bias":
                g = tot[r0:r0 + HEADS, 0:REL_PAD]
            elif name == "sgu_b_s":
                g = tot[r0:r0 + SGU_GROUPS, 0:SGU_BLOCK]
            elif name == "b_gate":
                g = jnp.concatenate([tot[r0:r0 + 1, :], tot[r0 + 1:r0 + 2, :]], axis=1)
            elif name in ("sgu_ln_g", "sgu_ln_b"):
                g = tot[r0:r0 + 1, 0:D_SGU]
            else:
                g = tot[r0:r0 + 1, :]
            res = (g,) + _adam(w_r[i][...], g, m_r[i][...], v_r[i][...])
            for ref, val in zip(o, res):
                ref[...] = val

    out_shape = []
    for name in names:
        out_shape += [jax.ShapeDtypeStruct(w[name].shape, F32)] * 4
    out_shape.append(jax.ShapeDtypeStruct((1, 128), F32))
    flat = pl.pallas_call(body, name="adam_small", out_shape=out_shape, compiler_params=_params())(
        sin, *[w[n] for n in names], *[m[n] for n in names], *[v[n] for n in names])
    return {name: tuple(flat[4 * i:4 * i + 4]) for i, name in enumerate(names)}, flat[4 * k]


def _adam_big(w, g, m, v, name):
    r, ncol = w.shape
    tr = 256 if r % 256 == 0 else r // 2

    def body(w_ref, g_ref, m_ref, v_ref, d_ref, m2_ref, v2_ref):
        d_ref[...], m2_ref[...], v2_ref[...] = _adam(w_ref[...], g_ref[...], m_ref[...], v_ref[...])

    spec = pl.BlockSpec((tr, ncol), lambda i: (i, 0))
    return pl.pallas_call(
        body, name=name, grid=(r // tr,), in_specs=[spec] * 4, out_specs=[spec] * 3,
        out_shape=[jax.ShapeDtypeStruct(w.shape, F32)] * 3, compiler_params=_params(("arbitrary",)),
    )(w, g, m, v)


WEIGHTS = ("norm_ffn1", "ffn1_w_gate", "ffn1_w_up", "ffn1_w_down", "norm_mix", "w_in", "b_gate", "rel_bias", "sgu_ln_g",
           "sgu_ln_b", "sgu_w_s", "sgu_b_s", "w_branch_att", "w_branch_sgu", "w_out", "norm_ffn2", "ffn2_w_gate",
           "ffn2_w_up", "ffn2_w_down", "norm_final")


GATE_UP = ("ffn1_w_gate", "ffn1_w_up", "ffn2_w_gate", "ffn2_w_up")
_FFN = ("ffn1_w_gate", "ffn1_w_up", "ffn1_w_down", "ffn2_w_gate", "ffn2_w_up", "ffn2_w_down")
_CAST_GROUPS = ((_FFN, "cast_ffn"), (("w_in",), "cast_w_in"), (("w_branch_att", "w_branch_sgu"), "cast_branch"),
                (("w_out",), "cast_w_out"))


def _big_form(name, a):
    return jnp.swapaxes(a, 1, 2)[0] if name in GATE_UP else a[0]


def _big_back(name, a):
    return jnp.swapaxes(a[None], 1, 2) if name in GATE_UP else a[None]


def _small_form(name, a):
    if name == "norm_final":
        return a.reshape(1, D_MODEL)
    if name == "rel_bias":
        return jnp.pad(a[0], ((0, 0), (0, REL_PAD - N_REL)))
    if name in ("sgu_w_s", "sgu_b_s"):
        return a[0]
    return a


def _small_back(name, a, like):
    if name == "rel_bias":
        a = a[:, :N_REL]
    return a.reshape(like.shape)


def kernel(x, norm_ffn1, ffn1_w_gate, ffn1_w_up, ffn1_w_down, norm_mix, w_in, b_gate, rel_bias, sgu_ln_g, sgu_ln_b, sgu_w_s, sgu_b_s, w_branch_att, w_branch_sgu, w_out, norm_ffn2, ffn2_w_gate, ffn2_w_up, ffn2_w_down, norm_final, loss_target, m_norm_ffn1, m_ffn1_w_gate, m_ffn1_w_up, m_ffn1_w_down, m_norm_mix, m_w_in, m_b_gate, m_rel_bias, m_sgu_ln_g, m_sgu_ln_b, m_sgu_w_s, m_sgu_b_s, m_w_branch_att, m_w_branch_sgu, m_w_out, m_norm_ffn2, m_ffn2_w_gate, m_ffn2_w_up, m_ffn2_w_down, m_norm_final, v_norm_ffn1, v_ffn1_w_gate, v_ffn1_w_up, v_ffn1_w_down, v_norm_mix, v_w_in, v_b_gate, v_rel_bias, v_sgu_ln_g, v_sgu_ln_b, v_sgu_w_s, v_sgu_b_s, v_w_branch_att, v_w_branch_sgu, v_w_out, v_norm_ffn2, v_ffn2_w_gate, v_ffn2_w_up, v_ffn2_w_down, v_norm_final):
    w = dict(norm_ffn1=norm_ffn1, ffn1_w_gate=ffn1_w_gate, ffn1_w_up=ffn1_w_up, ffn1_w_down=ffn1_w_down, norm_mix=norm_mix,
             w_in=w_in, b_gate=b_gate, rel_bias=rel_bias, sgu_ln_g=sgu_ln_g, sgu_ln_b=sgu_ln_b, sgu_w_s=sgu_w_s,
             sgu_b_s=sgu_b_s, w_branch_att=w_branch_att, w_branch_sgu=w_branch_sgu, w_out=w_out, norm_ffn2=norm_ffn2,
             ffn2_w_gate=ffn2_w_gate, ffn2_w_up=ffn2_w_up, ffn2_w_down=ffn2_w_down, norm_final=norm_final)
    m = dict(norm_ffn1=m_norm_ffn1, ffn1_w_gate=m_ffn1_w_gate, ffn1_w_up=m_ffn1_w_up, ffn1_w_down=m_ffn1_w_down,
             norm_mix=m_norm_mix, w_in=m_w_in, b_gate=m_b_gate, rel_bias=m_rel_bias, sgu_ln_g=m_sgu_ln_g,
             sgu_ln_b=m_sgu_ln_b, sgu_w_s=m_sgu_w_s, sgu_b_s=m_sgu_b_s, w_branch_att=m_w_branch_att,
             w_branch_sgu=m_w_branch_sgu, w_out=m_w_out, norm_ffn2=m_norm_ffn2, ffn2_w_gate=m_ffn2_w_gate,
             ffn2_w_up=m_ffn2_w_up, ffn2_w_down=m_ffn2_w_down, norm_final=m_norm_final)
    v = dict(norm_ffn1=v_norm_ffn1, ffn1_w_gate=v_ffn1_w_gate, ffn1_w_up=v_ffn1_w_up, ffn1_w_down=v_ffn1_w_down,
             norm_mix=v_norm_mix, w_in=v_w_in, b_gate=v_b_gate, rel_bias=v_rel_bias, sgu_ln_g=v_sgu_ln_g,
             sgu_ln_b=v_sgu_ln_b, sgu_w_s=v_sgu_w_s, sgu_b_s=v_sgu_b_s, w_branch_att=v_w_branch_att,
             w_branch_sgu=v_w_branch_sgu, w_out=v_w_out, norm_ffn2=v_norm_ffn2, ffn2_w_gate=v_ffn2_w_gate,
             ffn2_w_up=v_ffn2_w_up, ffn2_w_down=v_ffn2_w_down, norm_final=v_norm_final)

    core = lax.axis_index("c").astype(jnp.int32).reshape(1)
    chip = (2 * lax.axis_index("x") + lax.axis_index("y")).astype(jnp.int32).reshape(1)

    wk = {n: _big_form(n, w[n]) for n in BIG}
    slots = {}
    for names, call in _CAST_GROUPS:
        slots.update(zip(names, _cast_slots([wk[n] for n in names], chip, call)))
    ws ={n: _small_form(n, w[n]) for n in SMALL}
    _, gx, shard_grads, small_sums = _local_step(x[0], loss_target[0], slots, ws, (core, chip))

    small, loss = _adam_small(small_sums, ws, {n: _small_form(n, m[n]) for n in SMALL},
                              {n: _small_form(n, v[n]) for n in SMALL})
    grad, delta, new_m, new_v = {}, {}, {}, {}
    for n in SMALL:
        grad[n], delta[n], new_m[n], new_v[n] = (_small_back(n, a, w[n]) for a in small[n])
    for n in BIG:
        g2 = shard_grads[n]
        d2, m2, v2 = _adam_big(wk[n], g2, _big_form(n, m[n]), _big_form(n, v[n]), "adam_" + n)
        grad[n], delta[n], new_m[n], new_v[n] = (_big_back(n, a) for a in (g2, d2, m2, v2))

    return (loss[0, 0], gx.reshape(x.shape), *[grad[n] for n in WEIGHTS], *[delta[n] for n in WEIGHTS],
            *[new_m[n] for n in WEIGHTS], *[new_v[n] for n in WEIGHTS])
```

```python
import functools

import jax
import jax.numpy as jnp
from jax import lax
from jax.experimental import pallas as pl
from jax.experimental.pallas import tpu as pltpu

F32 = jnp.float32
BF16 = jnp.bfloat16

D_MODEL = 1024
N_SHARD = 4
D_FF = 2816
FF_S = D_FF // N_SHARD
D_ATT = 512
D_SGU = 512
D_IN = 3 * D_ATT + 2 * D_SGU + 2 * D_MODEL
IN_S = D_IN // N_SHARD
BR_S = D_MODEL // N_SHARD
HEADS = 8
HEAD_DIM = 64
CHUNK = 64
N_LEFT = 8
BAND = (N_LEFT + 1) * CHUNK
REL_CLIP = 256
N_REL = 2 * REL_CLIP + 1
REL_PAD = 640
SGU_BLOCK = 128
SGU_GROUPS = 8
SGU_GDIM = 64
EPS = 1e-6
NEG_INF = -1e30

ATT_ROWS = 2 * CHUNK
ATT_KEYS = BAND + CHUNK
ATT_PAD = N_LEFT * CHUNK

ADAM_LR = 0.001
ADAM_B1 = 0.9
ADAM_B2 = 0.999
ADAM_EPS = 1e-08
ADAM_WD = 0.01
ADAM_STEP = 10

TM = 256
TW = 512
VMEM_LIMIT = 56 * 1024 * 1024

SMALL_ROWS = 160
LOSS_ROW = 152
MESH = pl.DeviceIdType.MESH

_NT = (((1,), (1,)), ((), ()))
_TN = (((0,), (0,)), ((), ()))


def _params(sem=None):
    return pltpu.CompilerParams(dimension_semantics=sem, vmem_limit_bytes=VMEM_LIMIT)


def _const_spec(shape):
    nd = len(shape)
    return pl.BlockSpec(shape, lambda *_: (0,) * nd, pipeline_mode=pl.Buffered(1))


def _acc_spec(shape):
    nd = len(shape)
    return pl.BlockSpec(shape, lambda *_: (0,) * nd)


def _row_spec(tm, ncols, off=0):
    return pl.BlockSpec((tm, ncols), lambda i: (i + off, 0))


def _row3_spec(tm, ncols):
    return pl.BlockSpec((N_SHARD, tm, ncols), lambda i: (0, i, 0))


def _dot(a, b):
    return jnp.dot(a, b, preferred_element_type=F32)


def _dot_nt(a, b):
    return lax.dot_general(a, b, _NT, preferred_element_type=F32)


def _dot_tn(a, b):
    return lax.dot_general(a, b, _TN, preferred_element_type=F32)


def _rms_fwd(x, g):
    r = lax.rsqrt(jnp.mean(x * x, axis=-1, keepdims=True) + EPS)
    xhat = x * r
    return xhat, r, xhat * g


def _rms_bwd(dh, xhat, r, g):
    dxhat = dh * g
    dx = r * (dxhat - xhat * jnp.mean(dxhat * xhat, axis=-1, keepdims=True))
    dg = jnp.sum(dh * xhat, axis=0, keepdims=True)
    return dx, dg


def _sigmoid(x):
    return 1.0 / (1.0 + jnp.exp(-x))


def _edges(n_steps):
    return [(0, True), (n_steps - 1, False)]


def _ffn_fwd(x, g, wg, wu, wd, name, payload=None):
    T = x.shape[0]

    def body(x_ref, g_ref, wg_ref, wu_ref, wd_ref, xo_ref, h_ref, a_ref, b_ref):
        xv = x_ref[...]
        hb = _rms_fwd(xv, g_ref[...])[2].astype(BF16)
        h_ref[...] = hb
        acc = jnp.zeros((TM, D_MODEL), F32)
        for s in range(N_SHARD):
            a = _dot_nt(hb, wg_ref[s])
            b = _dot_nt(hb, wu_ref[s])
            a_ref[s] = a.astype(BF16)
            b_ref[s] = b.astype(BF16)
            sv = a * _sigmoid(a) * b
            acc += _dot(sv.astype(BF16), wd_ref[s])
        xo_ref[...] = xv + 0.5 * acc

    return _call(
        body, payload, name=name, grid=(T // TM,), when=_edges(T // TM), sem=("arbitrary",),
        in_specs=[_row_spec(TM, D_MODEL), _const_spec((1, D_MODEL)), _const_spec(wg.shape), _const_spec(wu.shape),
                  _const_spec(wd.shape)],
        out_specs=[_row_spec(TM, D_MODEL), _row_spec(TM, D_MODEL), _row3_spec(TM, FF_S), _row3_spec(TM, FF_S)],
        out_shape=[jax.ShapeDtypeStruct((T, D_MODEL), F32), jax.ShapeDtypeStruct((T, D_MODEL), BF16),
                   jax.ShapeDtypeStruct((N_SHARD, T, FF_S), BF16), jax.ShapeDtypeStruct((N_SHARD, T, FF_S), BF16)],
        operands=(x, g, wg, wu, wd))


TG = 512


def _ffn_fwd_gather(x, g, slots, chip, payload=None):
    T = x.shape[0]
    nt = T // TG
    rh = FF_S // 2

    k_in = len(payload.arrays) if payload else 0
    k_out = len(payload.out_shapes) if payload else 0

    def body(me_ref, x_ref, g_ref, *rest):
        p_in, rest = rest[3:3 + k_in], rest[3 + k_in:]
        xo_ref, h_ref, a_ref, b_ref, wg_hbm, wu_hbm, wd_hbm = rest[:7]
        p_out, rest = rest[7:7 + k_out], rest[7 + k_out:]
        acc, wbuf, send_i, recv_i, send_d, recv_d, wsem = rest[:7]
        phases = payload.phases(p_in, p_out, rest[7:]) if payload else None
        k, i = pl.program_id(0), pl.program_id(1)
        mx, my, c = _mesh_pos()
        me = 2 * mx + my
        w_hbm = (wg_hbm, wu_hbm, wd_hbm)

        def ici(j, w):
            t = me ^ j
            mine = w_hbm[w].at[me, pl.ds(c * rh, rh), :]
            return _remote(mine, mine, send_i.at[3 * (j - 1) + w], recv_i.at[3 * (j - 1) + w], (t // 2, t % 2, c))

        def d2d(j, w, core):
            land = w_hbm[w].at[me ^ j, pl.ds(core * rh, rh), :]
            return _remote(land, land, send_d.at[3 * (j - 1) + w], recv_d.at[3 * (j - 1) + w], (mx, my, 1 - c))

        def load(s):
            copies = [pltpu.make_async_copy(w_hbm[w].at[s], wbuf.at[w], wsem.at[w]) for w in range(3)]
            for cp in copies:
                cp.start()
            for cp in copies:
                cp.wait()

        @pl.when((k == 0) & (i == 0))
        def _():
            for j in (1, 2, 3):
                for w in range(3):
                    ici(j, w).start()
            load(me)

        for j in (1, 2, 3):
            @pl.when((k == j) & (i == 0))
            def _(j=j):
                for w in range(3):
                    land = w_hbm[w].at[me ^ j, pl.ds(c * rh, rh), :]
                    _remote(land, land, send_i.at[3 * (j - 1) + w], recv_i.at[3 * (j - 1) + w], (mx, my, c)).wait_recv()
                    d2d(j, w, c).start()
                for w in range(3):
                    d2d(j, w, 1 - c).wait_recv()
                if payload and j == 2:
                    phases[0]()
                load(me ^ j)

        xv = x_ref[...]
        hb = _rms_fwd(xv, g_ref[...])[2].astype(BF16)
        h_ref[...] = hb
        a = _dot_nt(hb, wbuf[0])
        b = _dot_nt(hb, wbuf[1])
        a_ref[0] = a.astype(BF16)
        b_ref[0] = b.astype(BF16)
        part = _dot((a * _sigmoid(a) * b).astype(BF16), wbuf[2])
        rows = pl.ds(pl.multiple_of(i * TG, TG), TG)

        @pl.when(k == 0)
        def _():
            acc[rows, :] = part

        @pl.when(k > 0)
        def _():
            acc[rows, :] += part

        @pl.when(k == N_SHARD - 1)
        def _():
            xo_ref[...] = xv + 0.5 * acc[rows, :]

        @pl.when((k == N_SHARD - 1) & (i == nt - 1))
        def _():
            for j in (1, 2, 3):
                for w in range(3):
                    ici(j, w).wait_send()
                    d2d(j, w, c).wait_send()
            if payload:
                for ph in phases[1:]:
                    ph()

    def last(k, i, me):
        return (jnp.where(k == N_SHARD - 1, i, 0), 0)

    aliases = {3 + w: 4 + w for w in range(3)}
    if payload:
        aliases.update({6 + a: 7 + b for a, b in payload.aliases.items()})
    grid_spec = pltpu.PrefetchScalarGridSpec(
        num_scalar_prefetch=1, grid=(N_SHARD, nt),
        in_specs=[pl.BlockSpec((TG, D_MODEL), lambda k, i, me: (i, 0)),
                  pl.BlockSpec((1, D_MODEL), lambda k, i, me: (0, 0))] + [_ANY] * (3 + k_in),
        out_specs=[pl.BlockSpec((TG, D_MODEL), last), pl.BlockSpec((TG, D_MODEL), last),
                   pl.BlockSpec((1, TG, FF_S), lambda k, i, me: (me[0] ^ k, i, 0)),
                   pl.BlockSpec((1, TG, FF_S), lambda k, i, me: (me[0] ^ k, i, 0))] + [_ANY] * (3 + k_out),
        scratch_shapes=[pltpu.VMEM((T, D_MODEL), F32), pltpu.VMEM((3, FF_S, D_MODEL), BF16)]
        + [pltpu.SemaphoreType.DMA((9,)) for _ in range(4)] + [pltpu.SemaphoreType.DMA((3,))]
        + (payload.scratch if payload else []))
    return pl.pallas_call(
        body, name="ffn1_fwd_gather", grid_spec=grid_spec,
        out_shape=[jax.ShapeDtypeStruct((T, D_MODEL), F32), jax.ShapeDtypeStruct((T, D_MODEL), BF16),
                   jax.ShapeDtypeStruct((N_SHARD, T, FF_S), BF16), jax.ShapeDtypeStruct((N_SHARD, T, FF_S), BF16)]
        + [jax.ShapeDtypeStruct(s.shape, s.dtype) for s in slots] + (payload.out_shapes if payload else []),
        input_output_aliases=aliases,
        compiler_params=_params(("arbitrary", "arbitrary")),
    )(chip, x, g, *slots, *(payload.arrays if payload else []))


def _ffn_dgrad(dout, x, a, b, g, wg, wu, wd, name, payload=None):
    T = x.shape[0]

    def body(do_ref, x_ref, a_ref, b_ref, g_ref, wg_ref, wu_ref, wd_ref, dx_ref, da_ref, db_ref, dg_ref):
        do = do_ref[...]
        dob = do.astype(BF16)
        dh = jnp.zeros((TM, D_MODEL), F32)
        for s in range(N_SHARD):
            ds = 0.5 * _dot_nt(dob, wd_ref[s])
            av = a_ref[s].astype(F32)
            bv = b_ref[s].astype(F32)
            sig = _sigmoid(av)
            da = (ds * bv * (sig * (1.0 + av * (1.0 - sig)))).astype(BF16)
            db = (ds * (av * sig)).astype(BF16)
            da_ref[s] = da
            db_ref[s] = db
            dh += _dot(da, wg_ref[s]) + _dot(db, wu_ref[s])
        gv = g_ref[...]
        xhat, r, _ = _rms_fwd(x_ref[...], gv)
        dxn, dg = _rms_bwd(dh, xhat, r, gv)
        dx_ref[...] = do + dxn

        @pl.when(pl.program_id(0) == 0)
        def _():
            dg_ref[...] = jnp.zeros_like(dg_ref)

        dg_ref[...] += dg

    return _call(
        body, payload, name=name, grid=(T // TM,), when=_edges(T // TM), sem=("arbitrary",),
        in_specs=[_row_spec(TM, D_MODEL), _row_spec(TM, D_MODEL), _row3_spec(TM, FF_S), _row3_spec(TM, FF_S),
                  _const_spec((1, D_MODEL)), _const_spec(wg.shape), _const_spec(wu.shape), _const_spec(wd.shape)],
        out_specs=[_row_spec(TM, D_MODEL), _row3_spec(TM, FF_S), _row3_spec(TM, FF_S), _acc_spec((1, D_MODEL))],
        out_shape=[jax.ShapeDtypeStruct((T, D_MODEL), F32), jax.ShapeDtypeStruct((N_SHARD, T, FF_S), BF16),
                   jax.ShapeDtypeStruct((N_SHARD, T, FF_S), BF16), jax.ShapeDtypeStruct((1, D_MODEL), F32)],
        operands=(dout, x, a, b, g, wg, wu, wd))


def _ffn_wgrad(h, dout, a, b, da, db, name, payload=None):
    T = h.shape[0]

    def body(h_ref, do_ref, a_ref, b_ref, da_ref, db_ref, gwg_ref, gwu_ref, gwd_ref):
        @pl.when(pl.program_id(1) == 0)
        def _():
            gwg_ref[...] = jnp.zeros_like(gwg_ref)
            gwu_ref[...] = jnp.zeros_like(gwu_ref)
            gwd_ref[...] = jnp.zeros_like(gwd_ref)

        hv = h_ref[...]
        dob = do_ref[...].astype(BF16)
        av = a_ref[0].astype(F32)
        sv = (0.5 * av * _sigmoid(av) * b_ref[0].astype(F32)).astype(BF16)
        gwg_ref[0] += _dot_tn(da_ref[0], hv)
        gwu_ref[0] += _dot_tn(db_ref[0], hv)
        gwd_ref[0] += _dot_tn(sv, dob)

    tok = pl.BlockSpec((TW, D_MODEL), lambda s, i: (i, 0))
    act = pl.BlockSpec((1, TW, FF_S), lambda s, i: (s, i, 0))
    return _call(
        body, payload, name=name, grid=(N_SHARD, T // TW), when=_edges(N_SHARD * (T // TW)),
        sem=("arbitrary", "arbitrary"),
        in_specs=[tok, tok, act, act, act, act],
        out_specs=[pl.BlockSpec((1, FF_S, D_MODEL), lambda s, i: (s, 0, 0))] * 3,
        out_shape=[jax.ShapeDtypeStruct((N_SHARD, FF_S, D_MODEL), F32)] * 3,
        operands=(h, dout, a, b, da, db))


def _in_fwd(x, g, w_in, payload=None):
    T = x.shape[0]

    def body(x_ref, g_ref, w_ref, h_ref, qkv_ref, zs_ref, gl_ref):
        hb = _rms_fwd(x_ref[...], g_ref[...])[2].astype(BF16)
        h_ref[...] = hb
        z0 = _dot(hb, w_ref[0])
        qkv_ref[:, 0:IN_S] = z0.astype(BF16)
        z1 = _dot(hb, w_ref[1])
        qkv_ref[:, IN_S:3 * D_ATT] = z1[:, 0:384].astype(BF16)
        zs_ref[:, 0:768] = z1[:, 384:IN_S]
        z2 = _dot(hb, w_ref[2])
        zs_ref[:, 768:1024] = z2[:, 0:256]
        gl_ref[:, 0:896] = z2[:, 256:IN_S]
        gl_ref[:, 896:2048] = _dot(hb, w_ref[3])

    return _call(
        body, payload, name="in_fwd", grid=(T // TM,), when=_edges(T // TM), sem=("arbitrary",),
        in_specs=[_row_spec(TM, D_MODEL), _const_spec((1, D_MODEL)), _const_spec(w_in.shape)],
        out_specs=[_row_spec(TM, D_MODEL), _row_spec(TM, 3 * D_ATT), _row_spec(TM, 2 * D_SGU), _row_spec(TM, 2 * D_MODEL)],
        out_shape=[jax.ShapeDtypeStruct((T, D_MODEL), BF16), jax.ShapeDtypeStruct((T, 3 * D_ATT), BF16),
                   jax.ShapeDtypeStruct((T, 2 * D_SGU), F32), jax.ShapeDtypeStruct((T, 2 * D_MODEL), F32)],
        operands=(x, g, w_in))


def _in_dgrad(dx_res, x, g, w_in, dq, dk, dv, dzs, dgl):
    T = x.shape[0]

    def body(dxr_ref, x_ref, g_ref, w_ref, dq_ref, dk_ref, dv_ref, dzs_ref, dgl_ref, dx_ref, dz_ref, dg_ref):
        dz = jnp.concatenate([dq_ref[...], dk_ref[...].astype(BF16), dv_ref[...].astype(BF16), dzs_ref[...], dgl_ref[...]],
                             axis=1)
        dz_ref[...] = dz
        dh = jnp.zeros((TM, D_MODEL), F32)
        for s in range(N_SHARD):
            dh += _dot_nt(dz[:, s * IN_S:(s + 1) * IN_S], w_ref[s])
        gv = g_ref[...]
        xhat, r, _ = _rms_fwd(x_ref[...], gv)
        dxn, dg = _rms_bwd(dh, xhat, r, gv)
        dx_ref[...] = dxr_ref[...] + dxn

        @pl.when(pl.program_id(0) == 0)
        def _():
            dg_ref[...] = jnp.zeros_like(dg_ref)

        dg_ref[...] += dg

    pad_blocks = ATT_PAD // TM
    return pl.pallas_call(
        body, name="in_dgrad", grid=(T // TM,),
        in_specs=[_row_spec(TM, D_MODEL), _row_spec(TM, D_MODEL), _const_spec((1, D_MODEL)), _const_spec(w_in.shape),
                  _row_spec(TM, D_ATT), _row_spec(TM, D_ATT, pad_blocks), _row_spec(TM, D_ATT, pad_blocks),
                  _row_spec(TM, 2 * D_SGU), _row_spec(TM, 2 * D_MODEL)],
        out_specs=[_row_spec(TM, D_MODEL), _row_spec(TM, D_IN), _acc_spec((1, D_MODEL))],
        out_shape=[jax.ShapeDtypeStruct((T, D_MODEL), F32), jax.ShapeDtypeStruct((T, D_IN), BF16),
                   jax.ShapeDtypeStruct((1, D_MODEL), F32)],
        compiler_params=_params(("arbitrary",)),
    )(dx_res, x, g, w_in, dq, dk, dv, dzs, dgl)


def _in_wgrad(h, dz):
    T = h.shape[0]

    def body(h_ref, dz_ref, gw_ref):
        @pl.when(pl.program_id(1) == 0)
        def _():
            gw_ref[...] = jnp.zeros_like(gw_ref)

        gw_ref[0] += _dot_tn(h_ref[...], dz_ref[...])

    return pl.pallas_call(
        body, name="in_wgrad", grid=(N_SHARD, T // TW),
        in_specs=[pl.BlockSpec((TW, D_MODEL), lambda s, i: (i, 0)), pl.BlockSpec((TW, IN_S), lambda s, i: (i, s))],
        out_specs=pl.BlockSpec((1, D_MODEL, IN_S), lambda s, i: (s, 0, 0)),
        out_shape=jax.ShapeDtypeStruct((N_SHARD, D_MODEL, IN_S), F32),
        compiler_params=_params(("arbitrary", "arbitrary")),
    )(h, dz)


def _rel_onehot():
    r = lax.broadcasted_iota(jnp.int32, (REL_PAD, REL_PAD), 0)
    n = lax.broadcasted_iota(jnp.int32, (REL_PAD, REL_PAD), 1)
    idx = jnp.clip(BAND - 1 - n, -REL_CLIP, REL_CLIP) + REL_CLIP
    return jnp.where(r == idx, 1.0, 0.0).astype(BF16)


def _split3(v):
    p1 = v.astype(BF16)
    r1 = v - p1.astype(F32)
    p2 = r1.astype(BF16)
    p3 = (r1 - p2.astype(F32)).astype(BF16)
    return p1, p2, p3


def _relbias_fwd(tab_pad):
    def body(t_ref, o_ref):
        oh = _rel_onehot()
        acc = jnp.zeros((HEADS, REL_PAD), F32)
        for p in _split3(t_ref[...]):
            acc += _dot(p, oh)
        o_ref[...] = acc

    return pl.pallas_call(body, name="relbias_fwd", out_shape=jax.ShapeDtypeStruct((HEADS, REL_PAD), F32))(tab_pad)


def _relbias_bwd(z):
    def body(z_ref, o_ref):
        oh = _rel_onehot()
        dt2 = jnp.sum(z_ref[...], axis=1)
        acc = jnp.zeros((HEADS, REL_PAD), F32)
        for p in _split3(dt2):
            acc += _dot_nt(p, oh)
        o_ref[...] = acc

    return pl.pallas_call(body, name="relbias_bwd", out_shape=jax.ShapeDtypeStruct((HEADS, REL_PAD), F32))(z)


def _bias_blocks(t2):
    flat = jnp.tile(t2, (1, CHUNK))
    skew = flat[:, :CHUNK * (REL_PAD - 1)].reshape(HEADS, CHUNK, REL_PAD - 1)
    bias = skew[:, :, CHUNK - 1:CHUNK - 1 + BAND]
    slabs = [jnp.pad(bias, ((0, 0), (0, 0), (CHUNK * c, ATT_KEYS - BAND - CHUNK * c)), constant_values=NEG_INF)
             for c in range(2)]
    return jnp.concatenate(slabs, axis=1)


def _unskew(db2):
    out = []
    for c in range(2):
        slab = db2[:, CHUNK * c:CHUNK * (c + 1), CHUNK * c:CHUNK * c + BAND]
        y = jnp.pad(slab, ((0, 0), (0, 0), (CHUNK - 1, REL_PAD - BAND - CHUNK + 1)))
        yf = jnp.pad(y.reshape(HEADS, CHUNK * REL_PAD), ((0, 0), (0, CHUNK)))
        out.append(yf.reshape(HEADS, CHUNK, REL_PAD + 1)[:, :, :REL_PAD])
    return jnp.concatenate(out, axis=1)


def _att_load(qkv_hbm, q_s, k_s, v_s, sem, T):
    copies = [pltpu.make_async_copy(qkv_hbm.at[:, 0:D_ATT], q_s, sem.at[0]),
              pltpu.make_async_copy(qkv_hbm.at[:, D_ATT:2 * D_ATT], k_s.at[pl.ds(ATT_PAD, T), :], sem.at[1]),
              pltpu.make_async_copy(qkv_hbm.at[:, 2 * D_ATT:3 * D_ATT], v_s.at[pl.ds(ATT_PAD, T), :], sem.at[2])]
    for cp in copies:
        cp.start()
    k_s[0:ATT_PAD, :] = jnp.zeros((ATT_PAD, D_ATT), BF16)
    v_s[0:ATT_PAD, :] = jnp.zeros((ATT_PAD, D_ATT), BF16)
    for cp in copies:
        cp.wait()


def _head(v, h):
    return v[:, h * HEAD_DIM:(h + 1) * HEAD_DIM]


def _rows(v, h):
    return v[h * ATT_ROWS:(h + 1) * ATT_ROWS]


def _att_exp(qs, kw, bias_ref, valid):
    s = jnp.concatenate([_dot_nt(_head(qs, h), _head(kw, h)) + bias_ref[h] for h in range(HEADS)], axis=0)
    if valid is not None:
        s = jnp.where(valid, s, NEG_INF)
    e = jnp.exp(s - jnp.max(s, axis=-1, keepdims=True))
    return e, 1.0 / jnp.sum(e, axis=-1, keepdims=True)


def _att_blocks(T, block):
    n_edge = min(ATT_PAD // ATT_ROWS, T // ATT_ROWS)

    def edge(i, carry):
        r0 = i * ATT_ROWS
        block(i, (lax.broadcasted_iota(jnp.int32, (1, ATT_KEYS), 1) + (r0 - ATT_PAD)) >= 0)
        return carry

    def inner(i, carry):
        block(i, None)
        return carry

    lax.fori_loop(0, n_edge, edge, 0)
    lax.fori_loop(n_edge, T // ATT_ROWS, inner, 0)


def _att_fwd(qkv, bias2, payload=None):
    T = qkv.shape[0]

    def body(qkv_hbm, bias_ref, y_ref, q_s, k_s, v_s, sem):
        _att_load(qkv_hbm, q_s, k_s, v_s, sem, T)

        def block(i, valid):
            r0 = pl.multiple_of(i * ATT_ROWS, ATT_ROWS)
            qs = q_s[pl.ds(r0, ATT_ROWS), :] * (HEAD_DIM ** -0.5)
            kw = k_s[pl.ds(r0, ATT_KEYS), :]
            vw = v_s[pl.ds(r0, ATT_KEYS), :]
            e, rinv = _att_exp(qs, kw, bias_ref, valid)
            eb = e.astype(BF16)
            outs = [_dot(_rows(eb, h), _head(vw, h)) * _rows(rinv, h) for h in range(HEADS)]
            y_ref[pl.ds(r0, ATT_ROWS), :] = jnp.concatenate(outs, axis=1).astype(BF16)

        _att_blocks(T, block)

    return _call(
        body, payload, name="att_fwd", grid=None,
        in_specs=[pl.BlockSpec(memory_space=pl.ANY), pl.BlockSpec(memory_space=pltpu.VMEM)],
        out_specs=[pl.BlockSpec(memory_space=pltpu.VMEM)],
        out_shape=[jax.ShapeDtypeStruct((T, D_ATT), BF16)],
        scratch_shapes=[pltpu.VMEM((T, D_ATT), BF16), pltpu.VMEM((T + ATT_PAD, D_ATT), BF16),
                        pltpu.VMEM((T + ATT_PAD, D_ATT), BF16), pltpu.SemaphoreType.DMA((3,))],
        operands=(qkv, bias2))


def _att_bwd(qkv, dy, bias2, payload=None):
    T = qkv.shape[0]

    def body(qkv_hbm, dy_ref, bias_ref, dq_ref, dk_ref, dv_ref, db_ref, q_s, k_s, v_s, sem):
        _att_load(qkv_hbm, q_s, k_s, v_s, sem, T)
        dk_ref[...] = jnp.zeros_like(dk_ref)
        dv_ref[...] = jnp.zeros_like(dv_ref)
        db_ref[...] = jnp.zeros_like(db_ref)


        def block(i, valid):
            r0 = pl.multiple_of(i * ATT_ROWS, ATT_ROWS)
            qs = q_s[pl.ds(r0, ATT_ROWS), :] * (HEAD_DIM ** -0.5)
            kw = k_s[pl.ds(r0, ATT_KEYS), :]
            vw = v_s[pl.ds(r0, ATT_KEYS), :]
            dyb = dy_ref[pl.ds(r0, ATT_ROWS), :]
            e, rinv = _att_exp(qs, kw, bias_ref, valid)
            p = e * rinv
            dp = jnp.concatenate([_dot_nt(_head(dyb, h), _head(vw, h)) for h in range(HEADS)], axis=0)
            ds = p * (dp - jnp.sum(p * dp, axis=-1, keepdims=True))
            db_ref[...] += ds.reshape(HEADS, ATT_ROWS, ATT_KEYS)
            dsb = ds.astype(BF16)
            pb = p.astype(BF16)
            dq = [_dot(_rows(dsb, h), _head(kw, h)) for h in range(HEADS)]
            dk = [_dot_tn(_rows(dsb, h), _head(qs, h)) for h in range(HEADS)]
            dv = [_dot_tn(_rows(pb, h), _head(dyb, h)) for h in range(HEADS)]
            dq_ref[pl.ds(r0, ATT_ROWS), :] = (jnp.concatenate(dq, axis=1) * (HEAD_DIM ** -0.5)).astype(BF16)
            dk_ref[pl.ds(r0, ATT_KEYS), :] += jnp.concatenate(dk, axis=1)
            dv_ref[pl.ds(r0, ATT_KEYS), :] += jnp.concatenate(dv, axis=1)

        _att_blocks(T, block)

    vmem = pl.BlockSpec(memory_space=pltpu.VMEM)
    return _call(
        body, payload, name="att_bwd", grid=None,
        in_specs=[pl.BlockSpec(memory_space=pl.ANY), vmem, vmem],
        out_specs=[vmem, vmem, vmem, vmem],
        out_shape=[jax.ShapeDtypeStruct((T, D_ATT), BF16), jax.ShapeDtypeStruct((T + ATT_PAD, D_ATT), F32),
                   jax.ShapeDtypeStruct((T + ATT_PAD, D_ATT), F32), jax.ShapeDtypeStruct((HEADS, ATT_ROWS, ATT_KEYS), F32)],
        scratch_shapes=[pltpu.VMEM((T, D_ATT), BF16), pltpu.VMEM((T + ATT_PAD, D_ATT), BF16),
                        pltpu.VMEM((T + ATT_PAD, D_ATT), BF16), pltpu.SemaphoreType.DMA((3,))],
        operands=(qkv, dy, bias2))


_GELU_C = 0.7978845608028654
_GELU_A = 0.044715


def _gelu(x):
    t = jnp.tanh(_GELU_C * (x + _GELU_A * x * x * x))
    return 0.5 * x * (1.0 + t), t


def _gelu_grad(x, t):
    return 0.5 * (1.0 + t) + 0.5 * x * (1.0 - t * t) * _GELU_C * (1.0 + 3.0 * _GELU_A * x * x)


def _group_masks():
    col = lax.broadcasted_iota(jnp.int32, (SGU_GROUPS, D_SGU), 1) // SGU_GDIM
    grp = lax.broadcasted_iota(jnp.int32, (SGU_GROUPS, D_SGU), 0)
    return jnp.where(col == grp, 1.0, 0.0).astype(F32)


def _causal_mask(transposed=False):
    i = lax.broadcasted_iota(jnp.int32, (SGU_BLOCK, SGU_BLOCK), 0) // CHUNK
    j = lax.broadcasted_iota(jnp.int32, (SGU_BLOCK, SGU_BLOCK), 1) // CHUNK
    return (j >= i) if transposed else (i >= j)


def _sgu_norm(zs, lng, lnb):
    gz, t = _gelu(zs)
    u = gz[:, 0:D_SGU]
    vs = gz[:, D_SGU:2 * D_SGU]
    xc = vs - jnp.mean(vs, axis=-1, keepdims=True)
    rstd = lax.rsqrt(jnp.mean(xc * xc, axis=-1, keepdims=True) + EPS)
    xhat = xc * rstd
    return t, u, xhat, rstd, xhat * lng + lnb


def _sgu_mix(vn_blk, w_ref, bst, gm):
    mask = _causal_mask()
    s = jnp.zeros((SGU_BLOCK, D_SGU), F32)
    for g in range(SGU_GROUPS):
        wm = jnp.where(mask, w_ref[g], 0.0).astype(BF16)
        s += _dot(wm, (vn_blk * gm[g:g + 1, :]).astype(BF16))
        s += bst[:, g:g + 1] * gm[g:g + 1, :]
    return s


def _sgu_fwd(zs, lng, lnb, w_s, bst):
    T = zs.shape[0]
    nblk = TM // SGU_BLOCK

    def body(zs_ref, lng_ref, lnb_ref, w_ref, bst_ref, y_ref):
        _, u, _, _, vn = _sgu_norm(zs_ref[...], lng_ref[...], lnb_ref[...])
        gm = _group_masks()
        bst_v = bst_ref[...]
        for n in range(nblk):
            rows = slice(n * SGU_BLOCK, (n + 1) * SGU_BLOCK)
            s = _sgu_mix(vn[rows], w_ref, bst_v, gm)
            y_ref[rows, :] = (u[rows] * s).astype(BF16)

    return pl.pallas_call(
        body, name="sgu_fwd", grid=(T // TM,),
        in_specs=[_row_spec(TM, 2 * D_SGU), _const_spec((1, D_SGU)), _const_spec((1, D_SGU)),
                  _const_spec(w_s.shape), _const_spec(bst.shape)],
        out_specs=_row_spec(TM, D_SGU),
        out_shape=jax.ShapeDtypeStruct((T, D_SGU), BF16),
        compiler_params=_params(("arbitrary",)),
    )(zs, lng, lnb, w_s, bst)


def _sgu_bwd(zs, dy, lng, lnb, w_s, w_st, bst):
    T = zs.shape[0]
    nblk = TM // SGU_BLOCK

    def body(zs_ref, dy_ref, lng_ref, lnb_ref, w_ref, wt_ref, bst_ref, dzs_ref, dw_ref, dbt_ref, dlg_ref, dlb_ref):
        @pl.when(pl.program_id(0) == 0)
        def _():
            dw_ref[...] = jnp.zeros_like(dw_ref)
            dbt_ref[...] = jnp.zeros_like(dbt_ref)
            dlg_ref[...] = jnp.zeros_like(dlg_ref)
            dlb_ref[...] = jnp.zeros_like(dlb_ref)

        zs_v = zs_ref[...]
        lng_v = lng_ref[...]
        t, u, xhat, rstd, vn = _sgu_norm(zs_v, lng_v, lnb_ref[...])
        gm = _group_masks()
        bst_v = bst_ref[...]
        mask = _causal_mask()
        mask_t = _causal_mask(transposed=True)
        dyv = dy_ref[...].astype(F32)
        lane8 = lax.broadcasted_iota(jnp.int32, (1, SGU_GROUPS), 1)
        du_rows, dvn_rows = [], []
        for n in range(nblk):
            rows = slice(n * SGU_BLOCK, (n + 1) * SGU_BLOCK)
            vn_b = vn[rows]
            s = _sgu_mix(vn_b, w_ref, bst_v, gm)
            du_rows.append(dyv[rows] * s)
            dsb = dyv[rows] * u[rows]
            vnb16 = vn_b.astype(BF16)
            dvn = jnp.zeros((SGU_BLOCK, D_SGU), F32)
            dbt = jnp.zeros((SGU_BLOCK, SGU_GROUPS), F32)
            for g in range(SGU_GROUPS):
                dsg = dsb * gm[g:g + 1, :]
                dsg16 = dsg.astype(BF16)
                wmt = jnp.where(mask_t, wt_ref[g], 0.0).astype(BF16)
                dvn += _dot(wmt, dsg16)
                dw_ref[g] += jnp.where(mask, _dot_nt(dsg16, vnb16), 0.0)
                dbt += jnp.sum(dsg, axis=-1, keepdims=True) * jnp.where(lane8 == g, 1.0, 0.0)
            dbt_ref[...] += dbt
            dvn_rows.append(dvn)
        du = jnp.concatenate(du_rows, axis=0)
        dvn = jnp.concatenate(dvn_rows, axis=0)
        dlg_ref[...] += jnp.sum(dvn * xhat, axis=0, keepdims=True)
        dlb_ref[...] += jnp.sum(dvn, axis=0, keepdims=True)
        dxhat = dvn * lng_v
        dvs = rstd * (dxhat - jnp.mean(dxhat, axis=-1, keepdims=True)
                      - xhat * jnp.mean(dxhat * xhat, axis=-1, keepdims=True))
        dgz = jnp.concatenate([du, dvs], axis=1)
        dzs_ref[...] = (dgz * _gelu_grad(zs_v, t)).astype(BF16)

    return pl.pallas_call(
        body, name="sgu_bwd", grid=(T // TM,),
        in_specs=[_row_spec(TM, 2 * D_SGU), _row_spec(TM, D_SGU), _const_spec((1, D_SGU)), _const_spec((1, D_SGU)),
                  _const_spec(w_s.shape), _const_spec(w_st.shape), _const_spec(bst.shape)],
        out_specs=[_row_spec(TM, 2 * D_SGU), _acc_spec(w_s.shape), _acc_spec(bst.shape), _acc_spec((1, D_SGU)),
                   _acc_spec((1, D_SGU))],
        out_shape=[jax.ShapeDtypeStruct((T, 2 * D_SGU), BF16), jax.ShapeDtypeStruct(w_s.shape, F32),
                   jax.ShapeDtypeStruct(bst.shape, F32), jax.ShapeDtypeStruct((1, D_SGU), F32),
                   jax.ShapeDtypeStruct((1, D_SGU), F32)],
        compiler_params=_params(("arbitrary",)),
    )(zs, dy, lng, lnb, w_s, w_st, bst)


def _cols(v, s):
    return v[:, s * BR_S:(s + 1) * BR_S]


def _merge_fwd(x, y_att, y_sgu, gl, b_gate, wba, wbs, wo, payload=None):
    T = x.shape[0]

    def body(x_ref, ya_ref, ys_ref, gl_ref, bg_ref, wba_ref, wbs_ref, wo_ref, xo_ref, m_ref, pa_ref, ps_ref):
        ya = ya_ref[...]
        ys = ys_ref[...]
        pa = jnp.concatenate([_dot(ya, wba_ref[s]) for s in range(N_SHARD)], axis=1)
        ps = jnp.concatenate([_dot(ys, wbs_ref[s]) for s in range(N_SHARD)], axis=1)
        g = _sigmoid(gl_ref[...] + bg_ref[...])
        mb = (g[:, 0:D_MODEL] * pa + g[:, D_MODEL:2 * D_MODEL] * ps).astype(BF16)
        m_ref[...] = mb
        pa_ref[...] = pa.astype(BF16)
        ps_ref[...] = ps.astype(BF16)
        acc = jnp.zeros((TM, D_MODEL), F32)
        for s in range(N_SHARD):
            acc += _dot(_cols(mb, s), wo_ref[s])
        xo_ref[...] = x_ref[...] + acc

    tokd = jax.ShapeDtypeStruct((T, D_MODEL), BF16)
    return _call(
        body, payload, name="merge_fwd", grid=(T // TM,), when=_edges(T // TM), sem=("arbitrary",),
        in_specs=[_row_spec(TM, D_MODEL), _row_spec(TM, D_ATT), _row_spec(TM, D_SGU), _row_spec(TM, 2 * D_MODEL),
                  _const_spec((1, 2 * D_MODEL)), _const_spec(wba.shape), _const_spec(wbs.shape), _const_spec(wo.shape)],
        out_specs=[_row_spec(TM, D_MODEL)] * 4,
        out_shape=[jax.ShapeDtypeStruct((T, D_MODEL), F32), tokd, tokd, tokd],
        operands=(x, y_att, y_sgu, gl, b_gate, wba, wbs, wo))


def _merge_bwd(dx, y_att, y_sgu, gl, merged, pa, ps, b_gate, wba, wbs, wo, payload=None):
    T = dx.shape[0]

    def body(dx_ref, ya_ref, ys_ref, gl_ref, m_ref, pa_ref, ps_ref, bg_ref, wba_ref, wbs_ref, wo_ref,
             dya_ref, dys_ref, dgl_ref, dbg_ref, gwba_ref, gwbs_ref, gwo_ref):
        @pl.when(pl.program_id(0) == 0)
        def _():
            dbg_ref[...] = jnp.zeros_like(dbg_ref)
            gwba_ref[...] = jnp.zeros_like(gwba_ref)
            gwbs_ref[...] = jnp.zeros_like(gwbs_ref)
            gwo_ref[...] = jnp.zeros_like(gwo_ref)

        dxb = dx_ref[...].astype(BF16)
        dm = jnp.concatenate([_dot_nt(dxb, wo_ref[s]) for s in range(N_SHARD)], axis=1)
        g = _sigmoid(gl_ref[...] + bg_ref[...])
        ga = g[:, 0:D_MODEL]
        gs = g[:, D_MODEL:2 * D_MODEL]
        dpa = (dm * ga).astype(BF16)
        dps = (dm * gs).astype(BF16)
        dgl = jnp.concatenate([dm * pa_ref[...].astype(F32) * ga * (1.0 - ga),
                               dm * ps_ref[...].astype(F32) * gs * (1.0 - gs)], axis=1)
        dgl_ref[...] = dgl.astype(BF16)
        dbg_ref[...] += jnp.sum(dgl, axis=0, keepdims=True)
        ya = ya_ref[...]
        ys = ys_ref[...]
        mb = m_ref[...]
        dya = jnp.zeros((TM, D_ATT), F32)
        dys = jnp.zeros((TM, D_SGU), F32)
        for s in range(N_SHARD):
            dya += _dot_nt(_cols(dpa, s), wba_ref[s])
            dys += _dot_nt(_cols(dps, s), wbs_ref[s])
            gwo_ref[s] += _dot_tn(_cols(mb, s), dxb)
            gwba_ref[s] += _dot_tn(ya, _cols(dpa, s))
            gwbs_ref[s] += _dot_tn(ys, _cols(dps, s))
        dya_ref[...] = dya.astype(BF16)
        dys_ref[...] = dys.astype(BF16)

    return _call(
        body, payload, name="merge_bwd", grid=(T // TM,), when=_edges(T // TM), sem=("arbitrary",),
        operands=(dx, y_att, y_sgu, gl, merged, pa, ps, b_gate, wba, wbs, wo),
        in_specs=[_row_spec(TM, D_MODEL), _row_spec(TM, D_ATT), _row_spec(TM, D_SGU), _row_spec(TM, 2 * D_MODEL),
                  _row_spec(TM, D_MODEL), _row_spec(TM, D_MODEL), _row_spec(TM, D_MODEL),
                  _const_spec((1, 2 * D_MODEL)), _const_spec(wba.shape), _const_spec(wbs.shape), _const_spec(wo.shape)],
        out_specs=[_row_spec(TM, D_ATT), _row_spec(TM, D_SGU), _row_spec(TM, 2 * D_MODEL), _acc_spec((1, 2 * D_MODEL)),
                   _acc_spec(wba.shape), _acc_spec(wbs.shape), _acc_spec(wo.shape)],
        out_shape=[jax.ShapeDtypeStruct((T, D_ATT), BF16), jax.ShapeDtypeStruct((T, D_SGU), BF16),
                   jax.ShapeDtypeStruct((T, 2 * D_MODEL), BF16), jax.ShapeDtypeStruct((1, 2 * D_MODEL), F32),
                   jax.ShapeDtypeStruct(wba.shape, F32), jax.ShapeDtypeStruct(wbs.shape, F32),
                   jax.ShapeDtypeStruct(wo.shape, F32)])


def _loss_bwd(x, target, g):
    T = x.shape[0]

    def body(x_ref, t_ref, g_ref, dx_ref, loss_ref, dg_ref):
        @pl.when(pl.program_id(0) == 0)
        def _():
            loss_ref[...] = jnp.zeros_like(loss_ref)
            dg_ref[...] = jnp.zeros_like(dg_ref)

        gv = g_ref[...]
        xhat, r, y = _rms_fwd(x_ref[...], gv)
        err = y - t_ref[...]
        per_tok = jnp.mean(err * err, axis=-1, keepdims=True)
        loss_ref[...] += 0.5 * jnp.sum(per_tok, axis=0, keepdims=True)
        dxn, dg = _rms_bwd(err * (1.0 / D_MODEL), xhat, r, gv)
        dx_ref[...] = dxn
        dg_ref[...] += dg

    return pl.pallas_call(
        body, name="loss_bwd", grid=(T // TM,),
        in_specs=[_row_spec(TM, D_MODEL), _row_spec(TM, D_MODEL), _const_spec((1, D_MODEL))],
        out_specs=[_row_spec(TM, D_MODEL), _acc_spec((1, 128)), _acc_spec((1, D_MODEL))],
        out_shape=[jax.ShapeDtypeStruct((T, D_MODEL), F32), jax.ShapeDtypeStruct((1, 128), F32),
                   jax.ShapeDtypeStruct((1, D_MODEL), F32)],
        compiler_params=_params(("arbitrary",)),
    )(x, target, g)


BIG = ("ffn1_w_gate", "ffn1_w_up", "ffn1_w_down", "w_in", "w_branch_att", "w_branch_sgu", "w_out",
       "ffn2_w_gate", "ffn2_w_up", "ffn2_w_down")
SMALL = ("norm_ffn1", "norm_mix", "b_gate", "rel_bias", "sgu_ln_g", "sgu_ln_b", "sgu_w_s", "sgu_b_s", "norm_ffn2",
         "norm_final")


BRANCH = ("w_branch_att", "w_branch_sgu")
G_FFN1 = ("ffn1_w_gate", "ffn1_w_up", "ffn1_w_down")
G_MIX = ("w_in", "w_branch_att", "w_branch_sgu", "w_out")
G_FFN2 = ("ffn2_w_gate", "ffn2_w_up", "ffn2_w_down")


def _local_step(x, target, wb, ws, dist=None):
    def gather_on(names):
        return _ag_payload([wb[n] for n in names]) if dist else None

    t2 = _relbias_fwd(ws["rel_bias"])
    bias2 = _bias_blocks(t2)
    bst = ws["sgu_b_s"].T
    w_st = jnp.swapaxes(ws["sgu_w_s"], 1, 2)

    if dist:
        x1, h1, a1, b1, *got = _ffn_fwd_gather(x, ws["norm_ffn1"], [wb[n] for n in G_FFN1], dist[1], gather_on(G_MIX))
        wb.update(zip(G_FFN1 + G_MIX, got))
        for n in BRANCH:
            wb[n] = wb[n].reshape(N_SHARD, D_ATT, BR_S)
    else:
        x1, h1, a1, b1 = _ffn_fwd(x, ws["norm_ffn1"], wb["ffn1_w_gate"], wb["ffn1_w_up"], wb["ffn1_w_down"], "ffn1_fwd")
    h2, qkv, zs, gl, *got = _in_fwd(x1, ws["norm_mix"], wb["w_in"], gather_on(G_FFN2[0:1]))
    wb.update(zip(G_FFN2[0:1], got))
    y_att, *got = _att_fwd(qkv, bias2, gather_on(G_FFN2[1:2]))
    wb.update(zip(G_FFN2[1:2], got))
    y_sgu = _sgu_fwd(zs, ws["sgu_ln_g"], ws["sgu_ln_b"], ws["sgu_w_s"], bst)
    x2, merged, pa, ps, *got = _merge_fwd(x1, y_att, y_sgu, gl, ws["b_gate"], wb["w_branch_att"], wb["w_branch_sgu"],
                                          wb["w_out"], gather_on(G_FFN2[2:3]))
    wb.update(zip(G_FFN2[2:3], got))
    x3, h3, a3, b3 = _ffn_fwd(x2, ws["norm_ffn2"], wb["ffn2_w_gate"], wb["ffn2_w_up"], wb["ffn2_w_down"], "ffn2_fwd")
    dx3, loss, g_final = _loss_bwd(x3, target, ws["norm_final"])

    gb, gs, sums = {}, {"norm_final": g_final}, {}

    def pair_on(names, small=None):
        return _px_payload([gb[n] for n in names], small) if dist else None

    def pair_add(names, halves):
        for n, rv in zip(names, halves):
            sums[n] = _pair_add(gb[n], rv, dist[0], dist[1], "pair_add_" + n)

    def chips_on(names):
        return _cx_payload([sums[n][1] for n in names], [sums[n][2] for n in names]) if dist else None

    dx2, da3, db3, gs["norm_ffn2"] = _ffn_dgrad(dx3, x2, a3, b3, ws["norm_ffn2"], wb["ffn2_w_gate"], wb["ffn2_w_up"],
                                                wb["ffn2_w_down"], "ffn2_dgrad")
    gb["ffn2_w_gate"], gb["ffn2_w_up"], gb["ffn2_w_down"] = _ffn_wgrad(h3, dx3, a3, b3, da3, db3, "ffn2_wgrad")
    dy_att, dy_sgu, dgl, gs["b_gate"], gb["w_branch_att"], gb["w_branch_sgu"], gb["w_out"], *got = _merge_bwd(
        dx2, y_att, y_sgu, gl, merged, pa, ps, ws["b_gate"], wb["w_branch_att"], wb["w_branch_sgu"], wb["w_out"],
        pair_on(G_FFN2))
    pair_add(G_FFN2, got)
    dq, dk, dv, db2, *lands2 = _att_bwd(qkv, dy_att, bias2, chips_on(G_FFN2))
    gs["rel_bias"] = _relbias_bwd(_unskew(db2))
    dzs, gs["sgu_w_s"], dbt, gs["sgu_ln_g"], gs["sgu_ln_b"] = _sgu_bwd(zs, dy_sgu, ws["sgu_ln_g"], ws["sgu_ln_b"],
                                                                      ws["sgu_w_s"], w_st, bst)
    gs["sgu_b_s"] = dbt.T
    dx1, dz, gs["norm_mix"] = _in_dgrad(dx2, x1, ws["norm_mix"], wb["w_in"], dq, dk, dv, dzs, dgl)
    gb["w_in"] = _in_wgrad(h2, dz)
    gx, da1, db1, gs["norm_ffn1"], *got = _ffn_dgrad(dx1, x, a1, b1, ws["norm_ffn1"], wb["ffn1_w_gate"],
                                                    wb["ffn1_w_up"], wb["ffn1_w_down"], "ffn1_dgrad", pair_on(G_MIX))
    pair_add(G_MIX, got)
    gb["ffn1_w_gate"], gb["ffn1_w_up"], gb["ffn1_w_down"], *lands_mix = _ffn_wgrad(h1, dx1, a1, b1, da1, db1,
                                                                                   "ffn1_wgrad", chips_on(G_MIX))
    if not dist:
        return loss, gx, gb, gs

    tail = _tail_reduce([gb[n] for n in G_FFN1], _pack_small(gs, loss))
    for i, n in enumerate(G_FFN1):
        sums[n] = (tail[i],)
    lands1, small_sums = tail[len(G_FFN1):2 * len(G_FFN1)], tail[-1]
    lands = dict(zip(G_FFN2 + G_MIX + G_FFN1, list(lands2) + list(lands_mix) + list(lands1)))
    fulls = [_final_sum(sums[n][0], lands[n], dist[1], dist[0], "final_sum_" + n) for n in BIG]
    return loss, gx, dict(zip(BIG, _sibling_share(fulls, "sibling_share"))), small_sums


_ANY = pl.BlockSpec(memory_space=pl.ANY)
_VMEM = pl.BlockSpec(memory_space=pltpu.VMEM)


def _mesh_pos():
    return lax.axis_index("x"), lax.axis_index("y"), lax.axis_index("c")


def _cast_slots(shards, chip, name):
    n = len(shards)
    r, ncol = shards[0].shape
    tr = r // 2

    def body(me_ref, *refs):
        for i_ref, o_ref in zip(refs[:n], refs[n:]):
            o_ref[0] = i_ref[...].astype(BF16)

    grid_spec = pltpu.PrefetchScalarGridSpec(
        num_scalar_prefetch=1, grid=(r // tr,),
        in_specs=[pl.BlockSpec((tr, ncol), lambda i, me: (i, 0))] * n,
        out_specs=[pl.BlockSpec((1, tr, ncol), lambda i, me: (me[0], i, 0))] * n)
    return pl.pallas_call(
        body, name=name, grid_spec=grid_spec,
        out_shape=[jax.ShapeDtypeStruct((N_SHARD, r, ncol), BF16)] * n,
        compiler_params=_params(("arbitrary",)),
    )(chip, *shards)


class _Payload:
    def __init__(self, arrays, out_shapes, aliases, scratch, phases):
        self.arrays = list(arrays)
        self.out_shapes = list(out_shapes)
        self.aliases = dict(aliases)
        self.scratch = list(scratch)
        self.phases = phases


def _remote(src, dst, ssem, rsem, dev):
    return pltpu.make_async_remote_copy(src_ref=src, dst_ref=dst, send_sem=ssem, recv_sem=rsem, device_id=dev,
                                        device_id_type=MESH)


def _call(body, payload, *, name, grid, in_specs, out_specs, out_shape, scratch_shapes=(), sem=None, when=None,
          operands=()):
    in_specs, out_specs, out_shape = list(in_specs), list(out_specs), list(out_shape)
    scratch_shapes = list(scratch_shapes)
    n_in, n_out, n_scr = len(in_specs), len(out_specs), len(scratch_shapes)
    kwargs = {}
    kernel = body
    if payload is not None:
        k_in, k_out = len(payload.arrays), len(payload.out_shapes)
        rank = len(grid) if grid else 0

        def kernel(*refs):
            a, b = n_in, n_in + k_in
            c, d = b + n_out, b + n_out + k_out
            e = d + n_scr
            phases = payload.phases(refs[a:b], refs[c:d], refs[e:])

            def run():
                body(*refs[:a], *refs[b:c], *refs[d:e])

            if not grid:
                phases[0]()
                run()
                for ph in phases[1:]:
                    ph()
                return
            step = pl.program_id(0)
            if rank == 2:
                step = step * grid[1] + pl.program_id(1)
            for ph, (at, before) in zip(phases, when):
                if before:
                    pl.when(step == at)(ph)
            run()
            for ph, (at, before) in zip(phases, when):
                if not before:
                    pl.when(step == at)(ph)

        in_specs += [_ANY] * k_in
        out_specs += [_ANY] * k_out
        out_shape += payload.out_shapes
        scratch_shapes += payload.scratch
        kwargs["input_output_aliases"] = {n_in + i: n_out + j for i, j in payload.aliases.items()}
        operands = tuple(operands) + tuple(payload.arrays)
    if grid:
        kwargs["grid"] = grid
    return pl.pallas_call(kernel, name=name, in_specs=in_specs, out_specs=out_specs, out_shape=out_shape,
                          scratch_shapes=scratch_shapes, compiler_params=_params(sem), **kwargs)(*operands)


def _ag_payload(slots):
    n = len(slots)

    def phases(_, refs, sems):
        send_i, recv_i, send_d, recv_d = sems
        x, y, c = _mesh_pos()
        me = 2 * x + y

        def half(w, core):
            rh = slots[w].shape[1] // 2
            return pl.ds(core * rh, rh)

        def ici(w, j):
            t = (me + 1 + j) % N_SHARD
            mine = refs[w].at[me, half(w, c), :]
            return _remote(mine, mine, send_i.at[3 * w + j], recv_i.at[3 * w + j], (t // 2, t % 2, c))

        def d2d(w, j, core):
            s = (me + 3 - j) % N_SHARD
            land = refs[w].at[s, half(w, core), :]
            return _remote(land, land, send_d.at[3 * w + j], recv_d.at[3 * w + j], (x, y, 1 - c))

        def start():
            for w in range(n):
                for j in range(3):
                    ici(w, j).start()

        def finish():
            for w in range(n):
                for j in range(3):
                    s = (me + 3 - j) % N_SHARD
                    land = refs[w].at[s, half(w, c), :]
                    _remote(land, land, send_i.at[3 * w + j], recv_i.at[3 * w + j], (x, y, c)).wait_recv()
                    d2d(w, j, c).start()
            for w in range(n):
                for j in range(3):
                    d2d(w, j, 1 - c).wait_recv()
            for w in range(n):
                for j in range(3):
                    ici(w, j).wait_send()
                    d2d(w, j, c).wait_send()

        return [start, finish]

    return _Payload(slots, [jax.ShapeDtypeStruct(s.shape, s.dtype) for s in slots], {i: i for i in range(n)},
                    [pltpu.SemaphoreType.DMA((3 * n,)) for _ in range(4)], phases)


def _px_payload(grads, small=None):
    arrays = list(grads) + ([small] if small is not None else [])
    n = len(arrays)

    def phases(ins, outs, sems):
        send, recv = sems
        x, y, c = _mesh_pos()

        def copy(w):
            if w < len(grads):
                rh = grads[w].shape[1] // 2
                src = ins[w].at[:, pl.ds((1 - c) * rh, rh), :]
            else:
                src = ins[w]
            return _remote(src, outs[w], send.at[w], recv.at[w], (x, y, 1 - c))

        def start():
            for w in range(n):
                copy(w).start()

        def finish():
            for w in range(n):
                copy(w).wait()

        return [start, finish]

    out_shapes = [jax.ShapeDtypeStruct((N_SHARD, g.shape[1] // 2, g.shape[2]), F32) for g in grads]
    if small is not None:
        out_shapes.append(jax.ShapeDtypeStruct(small.shape, F32))
    return _Payload(arrays, out_shapes, {}, [pltpu.SemaphoreType.DMA((n,)), pltpu.SemaphoreType.DMA((n,))], phases)


def _cx_payload(pbs, lands):
    n = len(pbs)

    def phases(ins, outs, sems):
        send, recv = sems
        x, y, c = _mesh_pos()
        me = 2 * x + y

        def copy(w, j):
            t = (me + 1 + j) % N_SHARD
            return _remote(ins[w].at[t], outs[w].at[me], send.at[3 * w + j], recv.at[3 * w + j], (t // 2, t % 2, c))

        def start():
            for w in range(n):
                for j in range(3):
                    copy(w, j).start()

        def finish():
            for w in range(n):
                for j in range(3):
                    copy(w, j).wait()

        return [start, finish]

    return _Payload(list(pbs) + list(lands), [jax.ShapeDtypeStruct(p.shape, BF16) for p in lands],
                    {n + i: i for i in range(n)},
                    [pltpu.SemaphoreType.DMA((3 * n,)), pltpu.SemaphoreType.DMA((3 * n,))], phases)


def _pair_add(g, rv, core, chip, name):
    _, r, ncol = g.shape
    rh = r // 2

    def body(c_ref, me_ref, g_ref, rv_ref, pf_ref, pb_ref, land_ref):
        s = g_ref[0] + rv_ref[0]
        sb = s.astype(BF16)
        pb_ref[0] = sb
        land_ref[0] = sb

        @pl.when(pl.program_id(0) == me_ref[0])
        def _():
            pf_ref[...] = s

    slot = pl.BlockSpec((1, rh, ncol), lambda s, c, me: (s, 0, 0))
    grid_spec = pltpu.PrefetchScalarGridSpec(
        num_scalar_prefetch=2, grid=(N_SHARD,),
        in_specs=[pl.BlockSpec((1, rh, ncol), lambda s, c, me: (s, c[0], 0)), slot],
        out_specs=[pl.BlockSpec((rh, ncol), lambda s, c, me: (0, 0)), slot, slot])
    return pl.pallas_call(
        body, name=name, grid_spec=grid_spec,
        out_shape=[jax.ShapeDtypeStruct((rh, ncol), F32), jax.ShapeDtypeStruct((N_SHARD, rh, ncol), BF16),
                   jax.ShapeDtypeStruct((N_SHARD, rh, ncol), BF16)],
        compiler_params=_params(("arbitrary",)),
    )(core, chip, g, rv)


def _tail_reduce(grads, small):
    n = len(grads)
    _, r, ncol = grads[0].shape
    rh = r // 2

    def body(*refs):
        g_hbm, sm = refs[:n], refs[n]
        pf, land, sm_out = refs[n + 1:2 * n + 1], refs[2 * n + 1:3 * n + 1], refs[3 * n + 1]
        scr = refs[3 * n + 2:]
        rv, mine, sendb = scr[:n], scr[n:2 * n], scr[2 * n:3 * n]
        sm_rv, sm_sum, d_send, d_recv, load, i_send, i_recv, store = scr[3 * n:]
        x, y, c = _mesh_pos()
        me = 2 * x + y
        sib = (x, y, 1 - c)

        def pair(w):
            src = g_hbm[w].at[:, pl.ds((1 - c) * rh, rh), :] if w < n else sm
            return _remote(src, rv[w] if w < n else sm_rv, d_send.at[w], d_recv.at[w], sib)

        def chips(w, j):
            t = (me + 1 + j) % N_SHARD
            src = sendb[w].at[t] if w < n else sm_sum
            dst = land[w].at[me] if w < n else sm_out.at[me]
            return _remote(src, dst, i_send.at[3 * w + j], i_recv.at[3 * w + j], (t // 2, t % 2, c))

        loads = [pltpu.make_async_copy(g_hbm[w].at[:, pl.ds(c * rh, rh), :], mine[w], load.at[w]) for w in range(n)]
        for w in range(n + 1):
            pair(w).start()
        for cp in loads:
            cp.start()
        stores = []
        for w in range(n):
            loads[w].wait()
            pair(w).wait_recv()
            for k in range(N_SHARD):
                s = mine[w][k] + rv[w][k]
                mine[w][k] = s
                sendb[w][k] = s.astype(BF16)
            stores += [pltpu.make_async_copy(mine[w].at[me], pf[w], store.at[2 * w]),
                       pltpu.make_async_copy(sendb[w].at[me], land[w].at[me], store.at[2 * w + 1])]
            for cp in stores[-2:]:
                cp.start()
            for j in range(3):
                chips(w, j).start()
        pair(n).wait_recv()
        sm_sum[...] = sm[...] + sm_rv[...]
        stores.append(pltpu.make_async_copy(sm_sum, sm_out.at[me], store.at[2 * n]))
        stores[-1].start()
        for j in range(3):
            chips(n, j).start()
        for w in range(n + 1):
            pair(w).wait_send()
            for j in range(3):
                chips(w, j).wait()
        for cp in stores:
            cp.wait()

    half = (N_SHARD, rh, ncol)
    return _call(
        body, None, name="tail_reduce", grid=None,
        in_specs=[_ANY] * n + [_VMEM], out_specs=[_ANY] * (2 * n + 1),
        out_shape=([jax.ShapeDtypeStruct((rh, ncol), F32)] * n + [jax.ShapeDtypeStruct(half, BF16)] * n
                   + [jax.ShapeDtypeStruct((N_SHARD,) + small.shape, F32)]),
        scratch_shapes=([pltpu.VMEM(half, F32)] * (2 * n) + [pltpu.VMEM(half, BF16)] * n
                        + [pltpu.VMEM(small.shape, F32), pltpu.VMEM(small.shape, F32),
                           pltpu.SemaphoreType.DMA((n + 1,)), pltpu.SemaphoreType.DMA((n + 1,)),
                           pltpu.SemaphoreType.DMA((n,)), pltpu.SemaphoreType.DMA((3 * n + 3,)),
                           pltpu.SemaphoreType.DMA((3 * n + 3,)), pltpu.SemaphoreType.DMA((2 * n + 1,))]),
        operands=(*grads, small))


def _final_sum(pf, land, chip, core, name):
    _, rh, ncol = land.shape

    def body(me_ref, c_ref, pf_ref, land_ref, o_ref):
        me = me_ref[0]
        acc = jnp.zeros((rh, ncol), F32)
        for k in range(N_SHARD):
            acc = acc + jnp.where(me == k, pf_ref[...], land_ref[k].astype(F32))
        o_ref[...] = acc

    grid_spec = pltpu.PrefetchScalarGridSpec(
        num_scalar_prefetch=2, grid=(1,),
        in_specs=[pl.BlockSpec((rh, ncol), lambda i, me, c: (0, 0)),
                  pl.BlockSpec((N_SHARD, rh, ncol), lambda i, me, c: (0, 0, 0))],
        out_specs=pl.BlockSpec((rh, ncol), lambda i, me, c: (c[0], 0)))
    return pl.pallas_call(
        body, name=name, grid_spec=grid_spec, out_shape=jax.ShapeDtypeStruct((2 * rh, ncol), F32),
        compiler_params=_params(("arbitrary",)),
    )(chip, core, pf, land)


def _sibling_share(fulls, name):
    n = len(fulls)

    def body(*refs):
        outs = refs[n:2 * n]
        send, recv = refs[2 * n:]
        x, y, c = _mesh_pos()
        cps = []
        for w in range(n):
            rh = fulls[w].shape[0] // 2
            mine = outs[w].at[pl.ds(c * rh, rh), :]
            cp = pltpu.make_async_remote_copy(src_ref=mine, dst_ref=mine, send_sem=send.at[w], recv_sem=recv.at[w],
                                              device_id=(x, y, 1 - c), device_id_type=MESH)
            cp.start()
            cps.append(cp)
        for cp in cps:
            cp.wait()

    return pl.pallas_call(
        body, name=name,
        in_specs=[_ANY] * n, out_specs=[_ANY] * n,
        out_shape=[jax.ShapeDtypeStruct(f.shape, F32) for f in fulls],
        input_output_aliases={i: i for i in range(n)},
        scratch_shapes=[pltpu.SemaphoreType.DMA((n,)), pltpu.SemaphoreType.DMA((n,))],
    )(*fulls)


_ROW = {"rel_bias": 128, "sgu_b_s": 136, "norm_ffn1": 144, "norm_mix": 145, "norm_ffn2": 146, "norm_final": 147,
        "b_gate": 148, "sgu_ln_g": 150, "sgu_ln_b": 151}


def _pack_small(gs, loss):
    def body(ws, rel, bs, n1, nm, n2, nf, bg, lg, lb, loss_ref, o_ref):
        o_ref[...] = jnp.zeros_like(o_ref)
        o_ref[LOSS_ROW:LOSS_ROW + 1, 0:128] = loss_ref[...]
        for g in range(SGU_GROUPS):
            o_ref[0:SGU_BLOCK, g * SGU_BLOCK:(g + 1) * SGU_BLOCK] = ws[g]
        o_ref[128:136, 0:REL_PAD] = rel[...]
        o_ref[136:144, 0:SGU_BLOCK] = bs[...]
        o_ref[144:145, :] = n1[...]
        o_ref[145:146, :] = nm[...]
        o_ref[146:147, :] = n2[...]
        o_ref[147:148, :] = nf[...]
        o_ref[148:149, :] = bg[:, 0:D_MODEL]
        o_ref[149:150, :] = bg[:, D_MODEL:2 * D_MODEL]
        o_ref[150:151, 0:D_SGU] = lg[...]
        o_ref[151:152, 0:D_SGU] = lb[...]

    order = ("sgu_w_s", "rel_bias", "sgu_b_s", "norm_ffn1", "norm_mix", "norm_ffn2", "norm_final", "b_gate", "sgu_ln_g",
             "sgu_ln_b")
    return pl.pallas_call(body, name="pack_small", out_shape=jax.ShapeDtypeStruct((SMALL_ROWS, D_MODEL), F32))(
        *[gs[k] for k in order], loss)


def _adam(w, g, m, v):
    m2 = ADAM_B1 * m + (1.0 - ADAM_B1) * g
    v2 = ADAM_B2 * v + (1.0 - ADAM_B2) * (g * g)
    m_hat = m2 / (1.0 - ADAM_B1 ** ADAM_STEP)
    v_hat = v2 / (1.0 - ADAM_B2 ** ADAM_STEP)
    delta = -ADAM_LR * (m_hat / (jnp.sqrt(v_hat) + ADAM_EPS) + ADAM_WD * w)
    return delta, m2, v2


def _adam_small(sin, w, m, v):
    names = SMALL
    k = len(names)

    def body(*refs):
        sin_ref = refs[0]
        w_r, m_r, v_r = refs[1:1 + k], refs[1 + k:1 + 2 * k], refs[1 + 2 * k:1 + 3 * k]
        outs = refs[1 + 3 * k:]
        tot = sin_ref[0] + sin_ref[1] + sin_ref[2] + sin_ref[3]
        outs[4 * k][...] = tot[LOSS_ROW:LOSS_ROW + 1, 0:128]
        for i, name in enumerate(names):
            o = outs[4 * i:4 * i + 4]
            if name == "sgu_w_s":
                for gi in range(SGU_GROUPS):
                    g = tot[0:SGU_BLOCK, gi * SGU_BLOCK:(gi + 1) * SGU_BLOCK]
                    res = (g,) + _adam(w_r[i][gi], g, m_r[i][gi], v_r[i][gi])
                    for ref, val in zip(o, res):
                        ref[gi] = val
                continue
            r0 = _ROW[name]
            if name == "rel_bias":
                g = tot[r0:r0 + HEADS, 0:REL_PAD]
            elif name == "sgu_b_s":
                g = tot[r0:r0 + SGU_GROUPS, 0:SGU_BLOCK]
            elif name == "b_gate":
                g = jnp.concatenate([tot[r0:r0 + 1, :], tot[r0 + 1:r0 + 2, :]], axis=1)
            elif name in ("sgu_ln_g", "sgu_ln_b"):
                g = tot[r0:r0 + 1, 0:D_SGU]
            else:
                g = tot[r0:r0 + 1, :]
            res = (g,) + _adam(w_r[i][...], g, m_r[i][...], v_r[i][...])
            for ref, val in zip(o, res):
                ref[...] = val

    out_shape = []
    for name in names:
        out_shape += [jax.ShapeDtypeStruct(w[name].shape, F32)] * 4
    out_shape.append(jax.ShapeDtypeStruct((1, 128), F32))
    flat = pl.pallas_call(body, name="adam_small", out_shape=out_shape, compiler_params=_params())(
        sin, *[w[n] for n in names], *[m[n] for n in names], *[v[n] for n in names])
    return {name: tuple(flat[4 * i:4 * i + 4]) for i, name in enumerate(names)}, flat[4 * k]


def _adam_big(w, g, m, v, name):
    r, ncol = w.shape
    tr = 256 if r % 256 == 0 else r // 2

    def body(w_ref, g_ref, m_ref, v_ref, d_ref, m2_ref, v2_ref):
        d_ref[...], m2_ref[...], v2_ref[...] = _adam(w_ref[...], g_ref[...], m_ref[...], v_ref[...])

    spec = pl.BlockSpec((tr, ncol), lambda i: (i, 0))
    return pl.pallas_call(
        body, name=name, grid=(r // tr,), in_specs=[spec] * 4, out_specs=[spec] * 3,
        out_shape=[jax.ShapeDtypeStruct(w.shape, F32)] * 3, compiler_params=_params(("arbitrary",)),
    )(w, g, m, v)


WEIGHTS = ("norm_ffn1", "ffn1_w_gate", "ffn1_w_up", "ffn1_w_down", "norm_mix", "w_in", "b_gate", "rel_bias", "sgu_ln_g",
           "sgu_ln_b", "sgu_w_s", "sgu_b_s", "w_branch_att", "w_branch_sgu", "w_out", "norm_ffn2", "ffn2_w_gate",
           "ffn2_w_up", "ffn2_w_down", "norm_final")


GATE_UP = ("ffn1_w_gate", "ffn1_w_up", "ffn2_w_gate", "ffn2_w_up")
_FFN = ("ffn1_w_gate", "ffn1_w_up", "ffn1_w_down", "ffn2_w_gate", "ffn2_w_up", "ffn2_w_down")
_CAST_GROUPS = ((_FFN, "cast_ffn"), (("w_in",), "cast_w_in"), (("w_branch_att", "w_branch_sgu"), "cast_branch"),
                (("w_out",), "cast_w_out"))


def _big_form(name, a):
    return jnp.swapaxes(a, 1, 2)[0] if name in GATE_UP else a[0]


def _big_back(name, a):
    return jnp.swapaxes(a[None], 1, 2) if name in GATE_UP else a[None]


def _small_form(name, a):
    if name == "norm_final":
        return a.reshape(1, D_MODEL)
    if name == "rel_bias":
        return jnp.pad(a[0], ((0, 0), (0, REL_PAD - N_REL)))
    if name in ("sgu_w_s", "sgu_b_s"):
        return a[0]
    return a


def _small_back(name, a, like):
    if name == "rel_bias":
        a = a[:, :N_REL]
    return a.reshape(like.shape)


def kernel(x, norm_ffn1, ffn1_w_gate, ffn1_w_up, ffn1_w_down, norm_mix, w_in, b_gate, rel_bias, sgu_ln_g, sgu_ln_b, sgu_w_s, sgu_b_s, w_branch_att, w_branch_sgu, w_out, norm_ffn2, ffn2_w_gate, ffn2_w_up, ffn2_w_down, norm_final, loss_target, m_norm_ffn1, m_ffn1_w_gate, m_ffn1_w_up, m_ffn1_w_down, m_norm_mix, m_w_in, m_b_gate, m_rel_bias, m_sgu_ln_g, m_sgu_ln_b, m_sgu_w_s, m_sgu_b_s, m_w_branch_att, m_w_branch_sgu, m_w_out, m_norm_ffn2, m_ffn2_w_gate, m_ffn2_w_up, m_ffn2_w_down, m_norm_final, v_norm_ffn1, v_ffn1_w_gate, v_ffn1_w_up, v_ffn1_w_down, v_norm_mix, v_w_in, v_b_gate, v_rel_bias, v_sgu_ln_g, v_sgu_ln_b, v_sgu_w_s, v_sgu_b_s, v_w_branch_att, v_w_branch_sgu, v_w_out, v_norm_ffn2, v_ffn2_w_gate, v_ffn2_w_up, v_ffn2_w_down, v_norm_final):
    w = dict(norm_ffn1=norm_ffn1, ffn1_w_gate=ffn1_w_gate, ffn1_w_up=ffn1_w_up, ffn1_w_down=ffn1_w_down, norm_mix=norm_mix,
             w_in=w_in, b_gate=b_gate, rel_bias=rel_bias, sgu_ln_g=sgu_ln_g, sgu_ln_b=sgu_ln_b, sgu_w_s=sgu_w_s,
             sgu_b_s=sgu_b_s, w_branch_att=w_branch_att, w_branch_sgu=w_branch_sgu, w_out=w_out, norm_ffn2=norm_ffn2,
             ffn2_w_gate=ffn2_w_gate, ffn2_w_up=ffn2_w_up, ffn2_w_down=ffn2_w_down, norm_final=norm_final)
    m = dict(norm_ffn1=m_norm_ffn1, ffn1_w_gate=m_ffn1_w_gate, ffn1_w_up=m_ffn1_w_up, ffn1_w_down=m_ffn1_w_down,
             norm_mix=m_norm_mix, w_in=m_w_in, b_gate=m_b_gate, rel_bias=m_rel_bias, sgu_ln_g=m_sgu_ln_g,
             sgu_ln_b=m_sgu_ln_b, sgu_w_s=m_sgu_w_s, sgu_b_s=m_sgu_b_s, w_branch_att=m_w_branch_att,
             w_branch_sgu=m_w_branch_sgu, w_out=m_w_out, norm_ffn2=m_norm_ffn2, ffn2_w_gate=m_ffn2_w_gate,
             ffn2_w_up=m_ffn2_w_up, ffn2_w_down=m_ffn2_w_down, norm_final=m_norm_final)
    v = dict(norm_ffn1=v_norm_ffn1, ffn1_w_gate=v_ffn1_w_gate, ffn1_w_up=v_ffn1_w_up, ffn1_w_down=v_ffn1_w_down,
             norm_mix=v_norm_mix, w_in=v_w_in, b_gate=v_b_gate, rel_bias=v_rel_bias, sgu_ln_g=v_sgu_ln_g,
             sgu_ln_b=v_sgu_ln_b, sgu_w_s=v_sgu_w_s, sgu_b_s=v_sgu_b_s, w_branch_att=v_w_branch_att,
             w_branch_sgu=v_w_branch_sgu, w_out=v_w_out, norm_ffn2=v_norm_ffn2, ffn2_w_gate=v_ffn2_w_gate,
             ffn2_w_up=v_ffn2_w_up, ffn2_w_down=v_ffn2_w_down, norm_final=v_norm_final)

    core = lax.axis_index("c").astype(jnp.int32).reshape(1)
    chip = (2 * lax.axis_index("x") + lax.axis_index("y")).astype(jnp.int32).reshape(1)

    wk = {n: _big_form(n, w[n]) for n in BIG}
    slots = {}
    for names, call in _CAST_GROUPS:
        shards = [wk[n].reshape(-1, D_MODEL) if n in BRANCH else wk[n] for n in names]
        slots.update(zip(names, _cast_slots(shards, chip, call)))
    ws = {n: _small_form(n, w[n]) for n in SMALL}
    _, gx, shard_grads, small_sums = _local_step(x[0], loss_target[0], slots, ws, (core, chip))

    small, loss = _adam_small(small_sums, ws, {n: _small_form(n, m[n]) for n in SMALL},
                              {n: _small_form(n, v[n]) for n in SMALL})
    grad, delta, new_m, new_v = {}, {}, {}, {}
    for n in SMALL:
        grad[n], delta[n], new_m[n], new_v[n] = (_small_back(n, a, w[n]) for a in small[n])
    for n in BIG:
        g2 = shard_grads[n]
        d2, m2, v2 = _adam_big(wk[n], g2, _big_form(n, m[n]), _big_form(n, v[n]), "adam_" + n)
        grad[n], delta[n], new_m[n], new_v[n] = (_big_back(n, a) for a in (g2, d2, m2, v2))

    return (loss[0, 0], gx.reshape(x.shape), *[grad[n] for n in WEIGHTS], *[delta[n] for n in WEIGHTS],
            *[new_m[n] for n in WEIGHTS], *[new_v[n] for n in WEIGHTS])
```

```python
import functools

import jax
import jax.numpy as jnp
from jax import lax
from jax.experimental import pallas as pl
from jax.experimental.pallas import tpu as pltpu

F32 = jnp.float32
BF16 = jnp.bfloat16

D_MODEL = 1024
N_SHARD = 4
D_FF = 2816
FF_S = D_FF // N_SHARD
D_ATT = 512
D_SGU = 512
D_IN = 3 * D_ATT + 2 * D_SGU + 2 * D_MODEL
IN_S = D_IN // N_SHARD
BR_S = D_MODEL // N_SHARD
HEADS = 8
HEAD_DIM = 64
CHUNK = 64
N_LEFT = 8
BAND = (N_LEFT + 1) * CHUNK
REL_CLIP = 256
N_REL = 2 * REL_CLIP + 1
REL_PAD = 640
SGU_BLOCK = 128
SGU_GROUPS = 8
SGU_GDIM = 64
EPS = 1e-6
NEG_INF = -1e30

ATT_ROWS = 2 * CHUNK
ATT_KEYS = BAND + CHUNK
ATT_PAD = N_LEFT * CHUNK

ADAM_LR = 0.001
ADAM_B1 = 0.9
ADAM_B2 = 0.999
ADAM_EPS = 1e-08
ADAM_WD = 0.01
ADAM_STEP = 10

TM = 256
TW = 512
VMEM_LIMIT = 56 * 1024 * 1024

SMALL_ROWS = 160
LOSS_ROW = 152
MESH = pl.DeviceIdType.MESH

_NT = (((1,), (1,)), ((), ()))
_TN = (((0,), (0,)), ((), ()))


def _params(sem=None):
    return pltpu.CompilerParams(dimension_semantics=sem, vmem_limit_bytes=VMEM_LIMIT)


def _const_spec(shape):
    nd = len(shape)
    return pl.BlockSpec(shape, lambda *_: (0,) * nd, pipeline_mode=pl.Buffered(1))


def _acc_spec(shape):
    nd = len(shape)
    return pl.BlockSpec(shape, lambda *_: (0,) * nd)


def _row_spec(tm, ncols, off=0):
    return pl.BlockSpec((tm, ncols), lambda i: (i + off, 0))


def _row3_spec(tm, ncols):
    return pl.BlockSpec((N_SHARD, tm, ncols), lambda i: (0, i, 0))


def _dot(a, b):
    return jnp.dot(a, b, preferred_element_type=F32)


def _dot_nt(a, b):
    return lax.dot_general(a, b, _NT, preferred_element_type=F32)


def _dot_tn(a, b):
    return lax.dot_general(a, b, _TN, preferred_element_type=F32)


def _rms_fwd(x, g):
    r = lax.rsqrt(jnp.mean(x * x, axis=-1, keepdims=True) + EPS)
    xhat = x * r
    return xhat, r, xhat * g


def _rms_bwd(dh, xhat, r, g):
    dxhat = dh * g
    dx = r * (dxhat - xhat * jnp.mean(dxhat * xhat, axis=-1, keepdims=True))
    dg = jnp.sum(dh * xhat, axis=0, keepdims=True)
    return dx, dg


def _sigmoid(x):
    return 1.0 / (1.0 + jnp.exp(-x))


def _edges(n_steps):
    return [(0, True), (n_steps - 1, False)]


def _ffn_fwd(x, g, wg, wu, wd, name, payload=None):
    T = x.shape[0]

    def body(x_ref, g_ref, wg_ref, wu_ref, wd_ref, xo_ref, h_ref, a_ref, b_ref):
        xv = x_ref[...]
        hb = _rms_fwd(xv, g_ref[...])[2].astype(BF16)
        h_ref[...] = hb
        acc = jnp.zeros((TM, D_MODEL), F32)
        for s in range(N_SHARD):
            a = _dot_nt(hb, wg_ref[s])
            b = _dot_nt(hb, wu_ref[s])
            a_ref[s] = a.astype(BF16)
            b_ref[s] = b.astype(BF16)
            sv = a * _sigmoid(a) * b
            acc += _dot(sv.astype(BF16), wd_ref[s])
        xo_ref[...] = xv + 0.5 * acc

    return _call(
        body, payload, name=name, grid=(T // TM,), when=_edges(T // TM), sem=("arbitrary",),
        in_specs=[_row_spec(TM, D_MODEL), _const_spec((1, D_MODEL)), _const_spec(wg.shape), _const_spec(wu.shape),
                  _const_spec(wd.shape)],
        out_specs=[_row_spec(TM, D_MODEL), _row_spec(TM, D_MODEL), _row3_spec(TM, FF_S), _row3_spec(TM, FF_S)],
        out_shape=[jax.ShapeDtypeStruct((T, D_MODEL), F32), jax.ShapeDtypeStruct((T, D_MODEL), BF16),
                   jax.ShapeDtypeStruct((N_SHARD, T, FF_S), BF16), jax.ShapeDtypeStruct((N_SHARD, T, FF_S), BF16)],
        operands=(x, g, wg, wu, wd))


def _ffn_dgrad(dout, x, a, b, g, wg, wu, wd, name, payload=None):
    T = x.shape[0]

    def body(do_ref, x_ref, a_ref, b_ref, g_ref, wg_ref, wu_ref, wd_ref, dx_ref, da_ref, db_ref, dg_ref):
        do = do_ref[...]
        dob = do.astype(BF16)
        dh = jnp.zeros((TM, D_MODEL), F32)
        for s in range(N_SHARD):
            ds = 0.5 * _dot_nt(dob, wd_ref[s])
            av = a_ref[s].astype(F32)
            bv = b_ref[s].astype(F32)
            sig = _sigmoid(av)
            da = (ds * bv * (sig * (1.0 + av * (1.0 - sig)))).astype(BF16)
            db = (ds * (av * sig)).astype(BF16)
            da_ref[s] = da
            db_ref[s] = db
            dh += _dot(da, wg_ref[s]) + _dot(db, wu_ref[s])
        gv = g_ref[...]
        xhat, r, _ = _rms_fwd(x_ref[...], gv)
        dxn, dg = _rms_bwd(dh, xhat, r, gv)
        dx_ref[...] = do + dxn

        @pl.when(pl.program_id(0) == 0)
        def _():
            dg_ref[...] = jnp.zeros_like(dg_ref)

        dg_ref[...] += dg

    return _call(
        body, payload, name=name, grid=(T // TM,), when=_edges(T // TM), sem=("arbitrary",),
        in_specs=[_row_spec(TM, D_MODEL), _row_spec(TM, D_MODEL), _row3_spec(TM, FF_S), _row3_spec(TM, FF_S),
                  _const_spec((1, D_MODEL)), _const_spec(wg.shape), _const_spec(wu.shape), _const_spec(wd.shape)],
        out_specs=[_row_spec(TM, D_MODEL), _row3_spec(TM, FF_S), _row3_spec(TM, FF_S), _acc_spec((1, D_MODEL))],
        out_shape=[jax.ShapeDtypeStruct((T, D_MODEL), F32), jax.ShapeDtypeStruct((N_SHARD, T, FF_S), BF16),
                   jax.ShapeDtypeStruct((N_SHARD, T, FF_S), BF16), jax.ShapeDtypeStruct((1, D_MODEL), F32)],
        operands=(dout, x, a, b, g, wg, wu, wd))


def _ffn_wgrad(h, dout, a, b, da, db, name, payload=None):
    T = h.shape[0]

    def body(h_ref, do_ref, a_ref, b_ref, da_ref, db_ref, gwg_ref, gwu_ref, gwd_ref):
        @pl.when(pl.program_id(1) == 0)
        def _():
            gwg_ref[...] = jnp.zeros_like(gwg_ref)
            gwu_ref[...] = jnp.zeros_like(gwu_ref)
            gwd_ref[...] = jnp.zeros_like(gwd_ref)

        hv = h_ref[...]
        dob = do_ref[...].astype(BF16)
        av = a_ref[0].astype(F32)
        sv = (0.5 * av * _sigmoid(av) * b_ref[0].astype(F32)).astype(BF16)
        gwg_ref[0] += _dot_tn(da_ref[0], hv)
        gwu_ref[0] += _dot_tn(db_ref[0], hv)
        gwd_ref[0] += _dot_tn(sv, dob)

    tok = pl.BlockSpec((TW, D_MODEL), lambda s, i: (i, 0))
    act = pl.BlockSpec((1, TW, FF_S), lambda s, i: (s, i, 0))
    return _call(
        body, payload, name=name, grid=(N_SHARD, T // TW), when=_edges(N_SHARD * (T // TW)),
        sem=("arbitrary", "arbitrary"),
        in_specs=[tok, tok, act, act, act, act],
        out_specs=[pl.BlockSpec((1, FF_S, D_MODEL), lambda s, i: (s, 0, 0))] * 3,
        out_shape=[jax.ShapeDtypeStruct((N_SHARD, FF_S, D_MODEL), F32)] * 3,
        operands=(h, dout, a, b, da, db))


def _in_fwd(x, g, w_in, payload=None):
    T = x.shape[0]

    def body(x_ref, g_ref, w_ref, h_ref, qkv_ref, zs_ref, gl_ref):
        hb = _rms_fwd(x_ref[...], g_ref[...])[2].astype(BF16)
        h_ref[...] = hb
        z0 = _dot(hb, w_ref[0])
        qkv_ref[:, 0:IN_S] = z0.astype(BF16)
        z1 = _dot(hb, w_ref[1])
        qkv_ref[:, IN_S:3 * D_ATT] = z1[:, 0:384].astype(BF16)
        zs_ref[:, 0:768] = z1[:, 384:IN_S]
        z2 = _dot(hb, w_ref[2])
        zs_ref[:, 768:1024] = z2[:, 0:256]
        gl_ref[:, 0:896] = z2[:, 256:IN_S]
        gl_ref[:, 896:2048] = _dot(hb, w_ref[3])

    return _call(
        body, payload, name="in_fwd", grid=(T // TM,), when=_edges(T // TM), sem=("arbitrary",),
        in_specs=[_row_spec(TM, D_MODEL), _const_spec((1, D_MODEL)), _const_spec(w_in.shape)],
        out_specs=[_row_spec(TM, D_MODEL), _row_spec(TM, 3 * D_ATT), _row_spec(TM, 2 * D_SGU), _row_spec(TM, 2 * D_MODEL)],
        out_shape=[jax.ShapeDtypeStruct((T, D_MODEL), BF16), jax.ShapeDtypeStruct((T, 3 * D_ATT), BF16),
                   jax.ShapeDtypeStruct((T, 2 * D_SGU), F32), jax.ShapeDtypeStruct((T, 2 * D_MODEL), F32)],
        operands=(x, g, w_in))


def _in_dgrad(dx_res, x, g, w_in, dq, dk, dv, dzs, dgl):
    T = x.shape[0]

    def body(dxr_ref, x_ref, g_ref, w_ref, dq_ref, dk_ref, dv_ref, dzs_ref, dgl_ref, dx_ref, dz_ref, dg_ref):
        dz = jnp.concatenate([dq_ref[...], dk_ref[...].astype(BF16), dv_ref[...].astype(BF16), dzs_ref[...], dgl_ref[...]],
                             axis=1)
        dz_ref[...] = dz
        dh = jnp.zeros((TM, D_MODEL), F32)
        for s in range(N_SHARD):
            dh += _dot_nt(dz[:, s * IN_S:(s + 1) * IN_S], w_ref[s])
        gv = g_ref[...]
        xhat, r, _ = _rms_fwd(x_ref[...], gv)
        dxn, dg = _rms_bwd(dh, xhat, r, gv)
        dx_ref[...] = dxr_ref[...] + dxn

        @pl.when(pl.program_id(0) == 0)
        def _():
            dg_ref[...] = jnp.zeros_like(dg_ref)

        dg_ref[...] += dg

    pad_blocks = ATT_PAD // TM
    return pl.pallas_call(
        body, name="in_dgrad", grid=(T // TM,),
        in_specs=[_row_spec(TM, D_MODEL), _row_spec(TM, D_MODEL), _const_spec((1, D_MODEL)), _const_spec(w_in.shape),
                  _row_spec(TM, D_ATT), _row_spec(TM, D_ATT, pad_blocks), _row_spec(TM, D_ATT, pad_blocks),
                  _row_spec(TM, 2 * D_SGU), _row_spec(TM, 2 * D_MODEL)],
        out_specs=[_row_spec(TM, D_MODEL), _row_spec(TM, D_IN), _acc_spec((1, D_MODEL))],
        out_shape=[jax.ShapeDtypeStruct((T, D_MODEL), F32), jax.ShapeDtypeStruct((T, D_IN), BF16),
                   jax.ShapeDtypeStruct((1, D_MODEL), F32)],
        compiler_params=_params(("arbitrary",)),
    )(dx_res, x, g, w_in, dq, dk, dv, dzs, dgl)


def _in_wgrad(h, dz):
    T = h.shape[0]

    def body(h_ref, dz_ref, gw_ref):
        @pl.when(pl.program_id(1) == 0)
        def _():
            gw_ref[...] = jnp.zeros_like(gw_ref)

        gw_ref[0] += _dot_tn(h_ref[...], dz_ref[...])

    return pl.pallas_call(
        body, name="in_wgrad", grid=(N_SHARD, T // TW),
        in_specs=[pl.BlockSpec((TW, D_MODEL), lambda s, i: (i, 0)), pl.BlockSpec((TW, IN_S), lambda s, i: (i, s))],
        out_specs=pl.BlockSpec((1, D_MODEL, IN_S), lambda s, i: (s, 0, 0)),
        out_shape=jax.ShapeDtypeStruct((N_SHARD, D_MODEL, IN_S), F32),
        compiler_params=_params(("arbitrary", "arbitrary")),
    )(h, dz)


def _rel_onehot():
    r = lax.broadcasted_iota(jnp.int32, (REL_PAD, REL_PAD), 0)
    n = lax.broadcasted_iota(jnp.int32, (REL_PAD, REL_PAD), 1)
    idx = jnp.clip(BAND - 1 - n, -REL_CLIP, REL_CLIP) + REL_CLIP
    return jnp.where(r == idx, 1.0, 0.0).astype(BF16)


def _split3(v):
    p1 = v.astype(BF16)
    r1 = v - p1.astype(F32)
    p2 = r1.astype(BF16)
    p3 = (r1 - p2.astype(F32)).astype(BF16)
    return p1, p2, p3


def _relbias_fwd(tab_pad):
    def body(t_ref, o_ref):
        oh = _rel_onehot()
        acc = jnp.zeros((HEADS, REL_PAD), F32)
        for p in _split3(t_ref[...]):
            acc += _dot(p, oh)
        o_ref[...] = acc

    return pl.pallas_call(body, name="relbias_fwd", out_shape=jax.ShapeDtypeStruct((HEADS, REL_PAD), F32))(tab_pad)


def _relbias_bwd(z):
    def body(z_ref, o_ref):
        oh = _rel_onehot()
        dt2 = jnp.sum(z_ref[...], axis=1)
        acc = jnp.zeros((HEADS, REL_PAD), F32)
        for p in _split3(dt2):
            acc += _dot_nt(p, oh)
        o_ref[...] = acc

    return pl.pallas_call(body, name="relbias_bwd", out_shape=jax.ShapeDtypeStruct((HEADS, REL_PAD), F32))(z)


def _bias_blocks(t2):
    flat = jnp.tile(t2, (1, CHUNK))
    skew = flat[:, :CHUNK * (REL_PAD - 1)].reshape(HEADS, CHUNK, REL_PAD - 1)
    bias = skew[:, :, CHUNK - 1:CHUNK - 1 + BAND]
    slabs = [jnp.pad(bias, ((0, 0), (0, 0), (CHUNK * c, ATT_KEYS - BAND - CHUNK * c)), constant_values=NEG_INF)
             for c in range(2)]
    return jnp.concatenate(slabs, axis=1)


def _unskew(db2):
    out = []
    for c in range(2):
        slab = db2[:, CHUNK * c:CHUNK * (c + 1), CHUNK * c:CHUNK * c + BAND]
        y = jnp.pad(slab, ((0, 0), (0, 0), (CHUNK - 1, REL_PAD - BAND - CHUNK + 1)))
        yf = jnp.pad(y.reshape(HEADS, CHUNK * REL_PAD), ((0, 0), (0, CHUNK)))
        out.append(yf.reshape(HEADS, CHUNK, REL_PAD + 1)[:, :, :REL_PAD])
    return jnp.concatenate(out, axis=1)


def _att_load(qkv_hbm, q_s, k_s, v_s, sem, T):
    copies = [pltpu.make_async_copy(qkv_hbm.at[:, 0:D_ATT], q_s, sem.at[0]),
              pltpu.make_async_copy(qkv_hbm.at[:, D_ATT:2 * D_ATT], k_s.at[pl.ds(ATT_PAD, T), :], sem.at[1]),
              pltpu.make_async_copy(qkv_hbm.at[:, 2 * D_ATT:3 * D_ATT], v_s.at[pl.ds(ATT_PAD, T), :], sem.at[2])]
    for cp in copies:
        cp.start()
    k_s[0:ATT_PAD, :] = jnp.zeros((ATT_PAD, D_ATT), BF16)
    v_s[0:ATT_PAD, :] = jnp.zeros((ATT_PAD, D_ATT), BF16)
    for cp in copies:
        cp.wait()


def _head(v, h):
    return v[:, h * HEAD_DIM:(h + 1) * HEAD_DIM]


def _rows(v, h):
    return v[h * ATT_ROWS:(h + 1) * ATT_ROWS]


def _att_exp(qs, kw, bias_ref, valid):
    s = jnp.concatenate([_dot_nt(_head(qs, h), _head(kw, h)) + bias_ref[h] for h in range(HEADS)], axis=0)
    if valid is not None:
        s = jnp.where(valid, s, NEG_INF)
    e = jnp.exp(s - jnp.max(s, axis=-1, keepdims=True))
    return e, 1.0 / jnp.sum(e, axis=-1, keepdims=True)


def _att_blocks(T, block, keys_on_rows=False):
    n_edge = min(ATT_PAD // ATT_ROWS, T // ATT_ROWS)
    shape, axis = ((ATT_KEYS, 1), 0) if keys_on_rows else ((1, ATT_KEYS), 1)

    def edge(i, carry):
        r0 = i * ATT_ROWS
        block(i, (lax.broadcasted_iota(jnp.int32, shape, axis) + (r0 - ATT_PAD)) >= 0)
        return carry

    def inner(i, carry):
        block(i, None)
        return carry

    lax.fori_loop(0, n_edge, edge, 0)
    lax.fori_loop(n_edge, T // ATT_ROWS, inner, 0)


def _att_fwd(qkv, bias2, payload=None):
    T = qkv.shape[0]

    def body(qkv_hbm, bias_ref, y_ref, q_s, k_s, v_s, sem):
        _att_load(qkv_hbm, q_s, k_s, v_s, sem, T)

        def block(i, valid):
            r0 = pl.multiple_of(i * ATT_ROWS, ATT_ROWS)
            qs = q_s[pl.ds(r0, ATT_ROWS), :] * (HEAD_DIM ** -0.5)
            kw = k_s[pl.ds(r0, ATT_KEYS), :]
            vw = v_s[pl.ds(r0, ATT_KEYS), :]
            e, rinv = _att_exp(qs, kw, bias_ref, valid)
            eb = e.astype(BF16)
            outs = [_dot(_rows(eb, h), _head(vw, h)) * _rows(rinv, h) for h in range(HEADS)]
            y_ref[pl.ds(r0, ATT_ROWS), :] = jnp.concatenate(outs, axis=1).astype(BF16)

        _att_blocks(T, block)

    return _call(
        body, payload, name="att_fwd", grid=None,
        in_specs=[pl.BlockSpec(memory_space=pl.ANY), pl.BlockSpec(memory_space=pltpu.VMEM)],
        out_specs=[pl.BlockSpec(memory_space=pltpu.VMEM)],
        out_shape=[jax.ShapeDtypeStruct((T, D_ATT), BF16)],
        scratch_shapes=[pltpu.VMEM((T, D_ATT), BF16), pltpu.VMEM((T + ATT_PAD, D_ATT), BF16),
                        pltpu.VMEM((T + ATT_PAD, D_ATT), BF16), pltpu.SemaphoreType.DMA((3,))],
        operands=(qkv, bias2))


def _lanes(v, h):
    return v[:, h * ATT_ROWS:(h + 1) * ATT_ROWS]


def _att_bwd(qkv, dy, bias2t, payload=None):
    T = qkv.shape[0]

    def body(qkv_hbm, dy_ref, bias_ref, dq_ref, dk_ref, dv_ref, db_ref, q_s, k_s, v_s, sem):
        _att_load(qkv_hbm, q_s, k_s, v_s, sem, T)
        dk_ref[...] = jnp.zeros_like(dk_ref)
        dv_ref[...] = jnp.zeros_like(dv_ref)
        db_ref[...] = jnp.zeros_like(db_ref)

        def block(i, valid):
            r0 = pl.multiple_of(i * ATT_ROWS, ATT_ROWS)
            qs = q_s[pl.ds(r0, ATT_ROWS), :] * (HEAD_DIM ** -0.5)
            kw = k_s[pl.ds(r0, ATT_KEYS), :]
            vw = v_s[pl.ds(r0, ATT_KEYS), :]
            dyb = dy_ref[pl.ds(r0, ATT_ROWS), :]
            s = jnp.concatenate([_dot_nt(_head(kw, h), _head(qs, h)) + bias_ref[h] for h in range(HEADS)], axis=1)
            if valid is not None:
                s = jnp.where(valid, s, NEG_INF)
            e = jnp.exp(s - jnp.max(s, axis=0, keepdims=True))
            p = e * (1.0 / jnp.sum(e, axis=0, keepdims=True))
            dp = jnp.concatenate([_dot_nt(_head(vw, h), _head(dyb, h)) for h in range(HEADS)], axis=1)
            ds = p * (dp - jnp.sum(p * dp, axis=0, keepdims=True))
            for h in range(HEADS):
                db_ref[h] += _lanes(ds, h)
            dsb = ds.astype(BF16)
            pb = p.astype(BF16)
            dq = [_dot_tn(_lanes(dsb, h), _head(kw, h)) for h in range(HEADS)]
            dk = [_dot(_lanes(dsb, h), _head(qs, h)) for h in range(HEADS)]
            dv = [_dot(_lanes(pb, h), _head(dyb, h)) for h in range(HEADS)]
            dq_ref[pl.ds(r0, ATT_ROWS), :] = (jnp.concatenate(dq, axis=1) * (HEAD_DIM ** -0.5)).astype(BF16)
            dk_ref[pl.ds(r0, ATT_KEYS), :] += jnp.concatenate(dk, axis=1)
            dv_ref[pl.ds(r0, ATT_KEYS), :] += jnp.concatenate(dv, axis=1)

        _att_blocks(T, block, keys_on_rows=True)

    vmem = pl.BlockSpec(memory_space=pltpu.VMEM)
    return _call(
        body, payload, name="att_bwd", grid=None,
        in_specs=[pl.BlockSpec(memory_space=pl.ANY), vmem, vmem],
        out_specs=[vmem, vmem, vmem, vmem],
        out_shape=[jax.ShapeDtypeStruct((T, D_ATT), BF16), jax.ShapeDtypeStruct((T + ATT_PAD, D_ATT), F32),
                   jax.ShapeDtypeStruct((T + ATT_PAD, D_ATT), F32), jax.ShapeDtypeStruct((HEADS, ATT_KEYS, ATT_ROWS), F32)],
        scratch_shapes=[pltpu.VMEM((T, D_ATT), BF16), pltpu.VMEM((T + ATT_PAD, D_ATT), BF16),
                        pltpu.VMEM((T + ATT_PAD, D_ATT), BF16), pltpu.SemaphoreType.DMA((3,))],
        operands=(qkv, dy, bias2t))


_GELU_C = 0.7978845608028654
_GELU_A = 0.044715


def _gelu(x):
    t = jnp.tanh(_GELU_C * (x + _GELU_A * x * x * x))
    return 0.5 * x * (1.0 + t), t


def _gelu_grad(x, t):
    return 0.5 * (1.0 + t) + 0.5 * x * (1.0 - t * t) * _GELU_C * (1.0 + 3.0 * _GELU_A * x * x)


def _group_masks():
    col = lax.broadcasted_iota(jnp.int32, (SGU_GROUPS, D_SGU), 1) // SGU_GDIM
    grp = lax.broadcasted_iota(jnp.int32, (SGU_GROUPS, D_SGU), 0)
    return jnp.where(col == grp, 1.0, 0.0).astype(F32)


def _causal_mask(transposed=False):
    i = lax.broadcasted_iota(jnp.int32, (SGU_BLOCK, SGU_BLOCK), 0) // CHUNK
    j = lax.broadcasted_iota(jnp.int32, (SGU_BLOCK, SGU_BLOCK), 1) // CHUNK
    return (j >= i) if transposed else (i >= j)


def _sgu_norm(zs, lng, lnb):
    gz, t = _gelu(zs)
    u = gz[:, 0:D_SGU]
    vs = gz[:, D_SGU:2 * D_SGU]
    xc = vs - jnp.mean(vs, axis=-1, keepdims=True)
    rstd = lax.rsqrt(jnp.mean(xc * xc, axis=-1, keepdims=True) + EPS)
    xhat = xc * rstd
    return t, u, xhat, rstd, xhat * lng + lnb


def _sgu_mix(vn_blk, w_ref, bst, gm):
    mask = _causal_mask()
    s = jnp.zeros((SGU_BLOCK, D_SGU), F32)
    for g in range(SGU_GROUPS):
        wm = jnp.where(mask, w_ref[g], 0.0).astype(BF16)
        s += _dot(wm, (vn_blk * gm[g:g + 1, :]).astype(BF16))
        s += bst[:, g:g + 1] * gm[g:g + 1, :]
    return s


def _sgu_fwd(zs, lng, lnb, w_s, bst):
    T = zs.shape[0]
    nblk = TM // SGU_BLOCK

    def body(zs_ref, lng_ref, lnb_ref, w_ref, bst_ref, y_ref):
        _, u, _, _, vn = _sgu_norm(zs_ref[...], lng_ref[...], lnb_ref[...])
        gm = _group_masks()
        bst_v = bst_ref[...]
        for n in range(nblk):
            rows = slice(n * SGU_BLOCK, (n + 1) * SGU_BLOCK)
            s = _sgu_mix(vn[rows], w_ref, bst_v, gm)
            y_ref[rows, :] = (u[rows] * s).astype(BF16)

    return pl.pallas_call(
        body, name="sgu_fwd", grid=(T // TM,),
        in_specs=[_row_spec(TM, 2 * D_SGU), _const_spec((1, D_SGU)), _const_spec((1, D_SGU)),
                  _const_spec(w_s.shape), _const_spec(bst.shape)],
        out_specs=_row_spec(TM, D_SGU),
        out_shape=jax.ShapeDtypeStruct((T, D_SGU), BF16),
        compiler_params=_params(("arbitrary",)),
    )(zs, lng, lnb, w_s, bst)


def _sgu_bwd(zs, dy, lng, lnb, w_s, w_st, bst):
    T = zs.shape[0]
    nblk = TM // SGU_BLOCK

    def body(zs_ref, dy_ref, lng_ref, lnb_ref, w_ref, wt_ref, bst_ref, dzs_ref, dw_ref, dbt_ref, dlg_ref, dlb_ref):
        @pl.when(pl.program_id(0) == 0)
        def _():
            dw_ref[...] = jnp.zeros_like(dw_ref)
            dbt_ref[...] = jnp.zeros_like(dbt_ref)
            dlg_ref[...] = jnp.zeros_like(dlg_ref)
            dlb_ref[...] = jnp.zeros_like(dlb_ref)

        zs_v = zs_ref[...]
        lng_v = lng_ref[...]
        t, u, xhat, rstd, vn = _sgu_norm(zs_v, lng_v, lnb_ref[...])
        gm = _group_masks()
        bst_v = bst_ref[...]
        mask = _causal_mask()
        mask_t = _causal_mask(transposed=True)
        dyv = dy_ref[...].astype(F32)
        lane8 = lax.broadcasted_iota(jnp.int32, (1, SGU_GROUPS), 1)
        du_rows, dvn_rows = [], []
        for n in range(nblk):
            rows = slice(n * SGU_BLOCK, (n + 1) * SGU_BLOCK)
            vn_b = vn[rows]
            s = _sgu_mix(vn_b, w_ref, bst_v, gm)
            du_rows.append(dyv[rows] * s)
            dsb = dyv[rows] * u[rows]
            vnb16 = vn_b.astype(BF16)
            dvn = jnp.zeros((SGU_BLOCK, D_SGU), F32)
            dbt = jnp.zeros((SGU_BLOCK, SGU_GROUPS), F32)
            for g in range(SGU_GROUPS):
                dsg = dsb * gm[g:g + 1, :]
                dsg16 = dsg.astype(BF16)
                wmt = jnp.where(mask_t, wt_ref[g], 0.0).astype(BF16)
                dvn += _dot(wmt, dsg16)
                dw_ref[g] += jnp.where(mask, _dot_nt(dsg16, vnb16), 0.0)
                dbt += jnp.sum(dsg, axis=-1, keepdims=True) * jnp.where(lane8 == g, 1.0, 0.0)
            dbt_ref[...] += dbt
            dvn_rows.append(dvn)
        du = jnp.concatenate(du_rows, axis=0)
        dvn = jnp.concatenate(dvn_rows, axis=0)
        dlg_ref[...] += jnp.sum(dvn * xhat, axis=0, keepdims=True)
        dlb_ref[...] += jnp.sum(dvn, axis=0, keepdims=True)
        dxhat = dvn * lng_v
        dvs = rstd * (dxhat - jnp.mean(dxhat, axis=-1, keepdims=True)
                      - xhat * jnp.mean(dxhat * xhat, axis=-1, keepdims=True))
        dgz = jnp.concatenate([du, dvs], axis=1)
        dzs_ref[...] = (dgz * _gelu_grad(zs_v, t)).astype(BF16)

    return pl.pallas_call(
        body, name="sgu_bwd", grid=(T // TM,),
        in_specs=[_row_spec(TM, 2 * D_SGU), _row_spec(TM, D_SGU), _const_spec((1, D_SGU)), _const_spec((1, D_SGU)),
                  _const_spec(w_s.shape), _const_spec(w_st.shape), _const_spec(bst.shape)],
        out_specs=[_row_spec(TM, 2 * D_SGU), _acc_spec(w_s.shape), _acc_spec(bst.shape), _acc_spec((1, D_SGU)),
                   _acc_spec((1, D_SGU))],
        out_shape=[jax.ShapeDtypeStruct((T, 2 * D_SGU), BF16), jax.ShapeDtypeStruct(w_s.shape, F32),
                   jax.ShapeDtypeStruct(bst.shape, F32), jax.ShapeDtypeStruct((1, D_SGU), F32),
                   jax.ShapeDtypeStruct((1, D_SGU), F32)],
        compiler_params=_params(("arbitrary",)),
    )(zs, dy, lng, lnb, w_s, w_st, bst)


def _cols(v, s):
    return v[:, s * BR_S:(s + 1) * BR_S]


def _merge_fwd(x, y_att, y_sgu, gl, b_gate, wba, wbs, wo, payload=None):
    T = x.shape[0]

    def body(x_ref, ya_ref, ys_ref, gl_ref, bg_ref, wba_ref, wbs_ref, wo_ref, xo_ref, m_ref, pa_ref, ps_ref):
        ya = ya_ref[...]
        ys = ys_ref[...]
        pa = jnp.concatenate([_dot(ya, wba_ref[s]) for s in range(N_SHARD)], axis=1)
        ps = jnp.concatenate([_dot(ys, wbs_ref[s]) for s in range(N_SHARD)], axis=1)
        g = _sigmoid(gl_ref[...] + bg_ref[...])
        mb = (g[:, 0:D_MODEL] * pa + g[:, D_MODEL:2 * D_MODEL] * ps).astype(BF16)
        m_ref[...] = mb
        pa_ref[...] = pa.astype(BF16)
        ps_ref[...] = ps.astype(BF16)
        acc = jnp.zeros((TM, D_MODEL), F32)
        for s in range(N_SHARD):
            acc += _dot(_cols(mb, s), wo_ref[s])
        xo_ref[...] = x_ref[...] + acc

    tokd = jax.ShapeDtypeStruct((T, D_MODEL), BF16)
    return _call(
        body, payload, name="merge_fwd", grid=(T // TM,), when=_edges(T // TM), sem=("arbitrary",),
        in_specs=[_row_spec(TM, D_MODEL), _row_spec(TM, D_ATT), _row_spec(TM, D_SGU), _row_spec(TM, 2 * D_MODEL),
                  _const_spec((1, 2 * D_MODEL)), _const_spec(wba.shape), _const_spec(wbs.shape), _const_spec(wo.shape)],
        out_specs=[_row_spec(TM, D_MODEL)] * 4,
        out_shape=[jax.ShapeDtypeStruct((T, D_MODEL), F32), tokd, tokd, tokd],
        operands=(x, y_att, y_sgu, gl, b_gate, wba, wbs, wo))


def _merge_bwd(dx, y_att, y_sgu, gl, merged, pa, ps, b_gate, wba, wbs, wo, payload=None):
    T = dx.shape[0]

    def body(dx_ref, ya_ref, ys_ref, gl_ref, m_ref, pa_ref, ps_ref, bg_ref, wba_ref, wbs_ref, wo_ref,
             dya_ref, dys_ref, dgl_ref, dbg_ref, gwba_ref, gwbs_ref, gwo_ref):
        @pl.when(pl.program_id(0) == 0)
        def _():
            dbg_ref[...] = jnp.zeros_like(dbg_ref)
            gwba_ref[...] = jnp.zeros_like(gwba_ref)
            gwbs_ref[...] = jnp.zeros_like(gwbs_ref)
            gwo_ref[...] = jnp.zeros_like(gwo_ref)

        dxb = dx_ref[...].astype(BF16)
        dm = jnp.concatenate([_dot_nt(dxb, wo_ref[s]) for s in range(N_SHARD)], axis=1)
        g = _sigmoid(gl_ref[...] + bg_ref[...])
        ga = g[:, 0:D_MODEL]
        gs = g[:, D_MODEL:2 * D_MODEL]
        dpa = (dm * ga).astype(BF16)
        dps = (dm * gs).astype(BF16)
        dgl = jnp.concatenate([dm * pa_ref[...].astype(F32) * ga * (1.0 - ga),
                               dm * ps_ref[...].astype(F32) * gs * (1.0 - gs)], axis=1)
        dgl_ref[...] = dgl.astype(BF16)
        dbg_ref[...] += jnp.sum(dgl, axis=0, keepdims=True)
        ya = ya_ref[...]
        ys = ys_ref[...]
        mb = m_ref[...]
        dya = jnp.zeros((TM, D_ATT), F32)
        dys = jnp.zeros((TM, D_SGU), F32)
        for s in range(N_SHARD):
            dya += _dot_nt(_cols(dpa, s), wba_ref[s])
            dys += _dot_nt(_cols(dps, s), wbs_ref[s])
            gwo_ref[s] += _dot_tn(_cols(mb, s), dxb)
            gwba_ref[s] += _dot_tn(ya, _cols(dpa, s))
            gwbs_ref[s] += _dot_tn(ys, _cols(dps, s))
        dya_ref[...] = dya.astype(BF16)
        dys_ref[...] = dys.astype(BF16)

    return _call(
        body, payload, name="merge_bwd", grid=(T // TM,), when=_edges(T // TM), sem=("arbitrary",),
        operands=(dx, y_att, y_sgu, gl, merged, pa, ps, b_gate, wba, wbs, wo),
        in_specs=[_row_spec(TM, D_MODEL), _row_spec(TM, D_ATT), _row_spec(TM, D_SGU), _row_spec(TM, 2 * D_MODEL),
                  _row_spec(TM, D_MODEL), _row_spec(TM, D_MODEL), _row_spec(TM, D_MODEL),
                  _const_spec((1, 2 * D_MODEL)), _const_spec(wba.shape), _const_spec(wbs.shape), _const_spec(wo.shape)],
        out_specs=[_row_spec(TM, D_ATT), _row_spec(TM, D_SGU), _row_spec(TM, 2 * D_MODEL), _acc_spec((1, 2 * D_MODEL)),
                   _acc_spec(wba.shape), _acc_spec(wbs.shape), _acc_spec(wo.shape)],
        out_shape=[jax.ShapeDtypeStruct((T, D_ATT), BF16), jax.ShapeDtypeStruct((T, D_SGU), BF16),
                   jax.ShapeDtypeStruct((T, 2 * D_MODEL), BF16), jax.ShapeDtypeStruct((1, 2 * D_MODEL), F32),
                   jax.ShapeDtypeStruct(wba.shape, F32), jax.ShapeDtypeStruct(wbs.shape, F32),
                   jax.ShapeDtypeStruct(wo.shape, F32)])


def _loss_bwd(x, target, g):
    T = x.shape[0]

    def body(x_ref, t_ref, g_ref, dx_ref, loss_ref, dg_ref):
        @pl.when(pl.program_id(0) == 0)
        def _():
            loss_ref[...] = jnp.zeros_like(loss_ref)
            dg_ref[...] = jnp.zeros_like(dg_ref)

        gv = g_ref[...]
        xhat, r, y = _rms_fwd(x_ref[...], gv)
        err = y - t_ref[...]
        per_tok = jnp.mean(err * err, axis=-1, keepdims=True)
        loss_ref[...] += 0.5 * jnp.sum(per_tok, axis=0, keepdims=True)
        dxn, dg = _rms_bwd(err * (1.0 / D_MODEL), xhat, r, gv)
        dx_ref[...] = dxn
        dg_ref[...] += dg

    return pl.pallas_call(
        body, name="loss_bwd", grid=(T // TM,),
        in_specs=[_row_spec(TM, D_MODEL), _row_spec(TM, D_MODEL), _const_spec((1, D_MODEL))],
        out_specs=[_row_spec(TM, D_MODEL), _acc_spec((1, 128)), _acc_spec((1, D_MODEL))],
        out_shape=[jax.ShapeDtypeStruct((T, D_MODEL), F32), jax.ShapeDtypeStruct((1, 128), F32),
                   jax.ShapeDtypeStruct((1, D_MODEL), F32)],
        compiler_params=_params(("arbitrary",)),
    )(x, target, g)


BIG = ("ffn1_w_gate", "ffn1_w_up", "ffn1_w_down", "w_in", "w_branch_att", "w_branch_sgu", "w_out",
       "ffn2_w_gate", "ffn2_w_up", "ffn2_w_down")
SMALL = ("norm_ffn1", "norm_mix", "b_gate", "rel_bias", "sgu_ln_g", "sgu_ln_b", "sgu_w_s", "sgu_b_s", "norm_ffn2",
         "norm_final")


G_FFN1 = ("ffn1_w_gate", "ffn1_w_up", "ffn1_w_down")
G_MIX = ("w_in", "w_branch_att", "w_branch_sgu", "w_out")
G_FFN2 = ("ffn2_w_gate", "ffn2_w_up", "ffn2_w_down")


def _local_step(x, target, wb, ws, dist=None):
    def gather_on(names):
        return _ag_payload([wb[n] for n in names]) if dist else None

    t2 = _relbias_fwd(ws["rel_bias"])
    bias2 = _bias_blocks(t2)
    bst = ws["sgu_b_s"].T
    w_st = jnp.swapaxes(ws["sgu_w_s"], 1, 2)

    if dist:
        wb.update(zip(G_FFN1, _call(lambda: None, gather_on(G_FFN1), name="allgather_ffn1", grid=None, in_specs=[],
                                    out_specs=[], out_shape=[])))
    x1, h1, a1, b1, *got = _ffn_fwd(x, ws["norm_ffn1"], wb["ffn1_w_gate"], wb["ffn1_w_up"], wb["ffn1_w_down"],
                                    "ffn1_fwd", gather_on(G_MIX))
    wb.update(zip(G_MIX, got))
    h2, qkv, zs, gl, *got = _in_fwd(x1, ws["norm_mix"], wb["w_in"], gather_on(G_FFN2[0:1]))
    wb.update(zip(G_FFN2[0:1], got))
    y_att, *got = _att_fwd(qkv, bias2, gather_on(G_FFN2[1:2]))
    wb.update(zip(G_FFN2[1:2], got))
    y_sgu = _sgu_fwd(zs, ws["sgu_ln_g"], ws["sgu_ln_b"], ws["sgu_w_s"], bst)
    x2, merged, pa, ps, *got = _merge_fwd(x1, y_att, y_sgu, gl, ws["b_gate"], wb["w_branch_att"], wb["w_branch_sgu"],
                                          wb["w_out"], gather_on(G_FFN2[2:3]))
    wb.update(zip(G_FFN2[2:3], got))
    x3, h3, a3, b3 = _ffn_fwd(x2, ws["norm_ffn2"], wb["ffn2_w_gate"], wb["ffn2_w_up"], wb["ffn2_w_down"], "ffn2_fwd")
    dx3, loss, g_final = _loss_bwd(x3, target, ws["norm_final"])

    gb, gs, sums = {}, {"norm_final": g_final}, {}

    def pair_on(names, small=None):
        return _px_payload([gb[n] for n in names], small) if dist else None

    def pair_add(names, halves):
        for n, rv in zip(names, halves):
            sums[n] = _pair_add(gb[n], rv, dist[0], dist[1], "pair_add_" + n)

    def chips_on(names):
        return _cx_payload([sums[n][1] for n in names], [sums[n][2] for n in names]) if dist else None

    dx2, da3, db3, gs["norm_ffn2"] = _ffn_dgrad(dx3, x2, a3, b3, ws["norm_ffn2"], wb["ffn2_w_gate"], wb["ffn2_w_up"],
                                                wb["ffn2_w_down"], "ffn2_dgrad")
    gb["ffn2_w_gate"], gb["ffn2_w_up"], gb["ffn2_w_down"] = _ffn_wgrad(h3, dx3, a3, b3, da3, db3, "ffn2_wgrad")
    dy_att, dy_sgu, dgl, gs["b_gate"], gb["w_branch_att"], gb["w_branch_sgu"], gb["w_out"], *got = _merge_bwd(
        dx2, y_att, y_sgu, gl, merged, pa, ps, ws["b_gate"], wb["w_branch_att"], wb["w_branch_sgu"], wb["w_out"],
        pair_on(G_FFN2))
    pair_add(G_FFN2, got)
    dq, dk, dv, db2t, *lands2 = _att_bwd(qkv, dy_att, jnp.swapaxes(bias2, 1, 2), chips_on(G_FFN2))
    gs["rel_bias"] = _relbias_bwd(_unskew(jnp.swapaxes(db2t, 1, 2)))
    dzs, gs["sgu_w_s"], dbt, gs["sgu_ln_g"], gs["sgu_ln_b"] = _sgu_bwd(zs, dy_sgu, ws["sgu_ln_g"], ws["sgu_ln_b"],
                                                                      ws["sgu_w_s"], w_st, bst)
    gs["sgu_b_s"] = dbt.T
    dx1, dz, gs["norm_mix"] = _in_dgrad(dx2, x1, ws["norm_mix"], wb["w_in"], dq, dk, dv, dzs, dgl)
    gb["w_in"] = _in_wgrad(h2, dz)
    gx, da1, db1, gs["norm_ffn1"], *got = _ffn_dgrad(dx1, x, a1, b1, ws["norm_ffn1"], wb["ffn1_w_gate"],
                                                    wb["ffn1_w_up"], wb["ffn1_w_down"], "ffn1_dgrad", pair_on(G_MIX))
    pair_add(G_MIX, got)
    gb["ffn1_w_gate"], gb["ffn1_w_up"], gb["ffn1_w_down"], *lands_mix = _ffn_wgrad(h1, dx1, a1, b1, da1, db1,
                                                                                   "ffn1_wgrad", chips_on(G_MIX))
    if not dist:
        return loss, gx, gb, gs

    tail = _tail_reduce([gb[n] for n in G_FFN1], _pack_small(gs, loss))
    for i, n in enumerate(G_FFN1):
        sums[n] = (tail[i],)
    lands1, small_sums = tail[len(G_FFN1):2 * len(G_FFN1)], tail[-1]
    lands = dict(zip(G_FFN2 + G_MIX + G_FFN1, list(lands2) + list(lands_mix) + list(lands1)))
    fulls = [_final_sum(sums[n][0], lands[n], dist[1], dist[0], "final_sum_" + n) for n in BIG]
    return loss, gx, dict(zip(BIG, _sibling_share(fulls, "sibling_share"))), small_sums


_ANY = pl.BlockSpec(memory_space=pl.ANY)
_VMEM = pl.BlockSpec(memory_space=pltpu.VMEM)


def _mesh_pos():
    return lax.axis_index("x"), lax.axis_index("y"), lax.axis_index("c")


def _cast_slots(shards, chip, name):
    n = len(shards)
    r, ncol = shards[0].shape
    tr = r // 2

    def body(me_ref, *refs):
        for i_ref, o_ref in zip(refs[:n], refs[n:]):
            o_ref[0] = i_ref[...].astype(BF16)

    grid_spec = pltpu.PrefetchScalarGridSpec(
        num_scalar_prefetch=1, grid=(r // tr,),
        in_specs=[pl.BlockSpec((tr, ncol), lambda i, me: (i, 0))] * n,
        out_specs=[pl.BlockSpec((1, tr, ncol), lambda i, me: (me[0], i, 0))] * n)
    return pl.pallas_call(
        body, name=name, grid_spec=grid_spec,
        out_shape=[jax.ShapeDtypeStruct((N_SHARD, r, ncol), BF16)] * n,
        compiler_params=_params(("arbitrary",)),
    )(chip, *shards)


class _Payload:
    def __init__(self, arrays, out_shapes, aliases, scratch, phases):
        self.arrays = list(arrays)
        self.out_shapes = list(out_shapes)
        self.aliases = dict(aliases)
        self.scratch = list(scratch)
        self.phases = phases


def _remote(src, dst, ssem, rsem, dev):
    return pltpu.make_async_remote_copy(src_ref=src, dst_ref=dst, send_sem=ssem, recv_sem=rsem, device_id=dev,
                                        device_id_type=MESH)


def _call(body, payload, *, name, grid, in_specs, out_specs, out_shape, scratch_shapes=(), sem=None, when=None,
          operands=()):
    in_specs, out_specs, out_shape = list(in_specs), list(out_specs), list(out_shape)
    scratch_shapes = list(scratch_shapes)
    n_in, n_out, n_scr = len(in_specs), len(out_specs), len(scratch_shapes)
    kwargs = {}
    kernel = body
    if payload is not None:
        k_in, k_out = len(payload.arrays), len(payload.out_shapes)
        rank = len(grid) if grid else 0

        def kernel(*refs):
            a, b = n_in, n_in + k_in
            c, d = b + n_out, b + n_out + k_out
            e = d + n_scr
            phases = payload.phases(refs[a:b], refs[c:d], refs[e:])

            def run():
                body(*refs[:a], *refs[b:c], *refs[d:e])

            if not grid:
                phases[0]()
                run()
                for ph in phases[1:]:
                    ph()
                return
            step = pl.program_id(0)
            if rank == 2:
                step = step * grid[1] + pl.program_id(1)
            for ph, (at, before) in zip(phases, when):
                if before:
                    pl.when(step == at)(ph)
            run()
            for ph, (at, before) in zip(phases, when):
                if not before:
                    pl.when(step == at)(ph)

        in_specs += [_ANY] * k_in
        out_specs += [_ANY] * k_out
        out_shape += payload.out_shapes
        scratch_shapes += payload.scratch
        kwargs["input_output_aliases"] = {n_in + i: n_out + j for i, j in payload.aliases.items()}
        operands = tuple(operands) + tuple(payload.arrays)
    if grid:
        kwargs["grid"] = grid
    return pl.pallas_call(kernel, name=name, in_specs=in_specs, out_specs=out_specs, out_shape=out_shape,
                          scratch_shapes=scratch_shapes, compiler_params=_params(sem), **kwargs)(*operands)


def _ag_payload(slots):
    n = len(slots)

    def phases(_, refs, sems):
        send_i, recv_i, send_d, recv_d = sems
        x, y, c = _mesh_pos()
        me = 2 * x + y

        def half(w, core):
            rh = slots[w].shape[1] // 2
            return pl.ds(core * rh, rh)

        def ici(w, j):
            t = (me + 1 + j) % N_SHARD
            mine = refs[w].at[me, half(w, c), :]
            return _remote(mine, mine, send_i.at[3 * w + j], recv_i.at[3 * w + j], (t // 2, t % 2, c))

        def d2d(w, j, core):
            s = (me + 3 - j) % N_SHARD
            land = refs[w].at[s, half(w, core), :]
            return _remote(land, land, send_d.at[3 * w + j], recv_d.at[3 * w + j], (x, y, 1 - c))

        def start():
            for w in range(n):
                for j in range(3):
                    ici(w, j).start()

        def finish():
            for w in range(n):
                for j in range(3):
                    s = (me + 3 - j) % N_SHARD
                    land = refs[w].at[s, half(w, c), :]
                    _remote(land, land, send_i.at[3 * w + j], recv_i.at[3 * w + j], (x, y, c)).wait_recv()
                    d2d(w, j, c).start()
            for w in range(n):
                for j in range(3):
                    d2d(w, j, 1 - c).wait_recv()
            for w in range(n):
                for j in range(3):
                    ici(w, j).wait_send()
                    d2d(w, j, c).wait_send()

        return [start, finish]

    return _Payload(slots, [jax.ShapeDtypeStruct(s.shape, s.dtype) for s in slots], {i: i for i in range(n)},
                    [pltpu.SemaphoreType.DMA((3 * n,)) for _ in range(4)], phases)


def _px_payload(grads, small=None):
    arrays = list(grads) + ([small] if small is not None else [])
    n = len(arrays)

    def phases(ins, outs, sems):
        send, recv = sems
        x, y, c = _mesh_pos()

        def copy(w):
            if w < len(grads):
                rh = grads[w].shape[1] // 2
                src = ins[w].at[:, pl.ds((1 - c) * rh, rh), :]
            else:
                src = ins[w]
            return _remote(src, outs[w], send.at[w], recv.at[w], (x, y, 1 - c))

        def start():
            for w in range(n):
                copy(w).start()

        def finish():
            for w in range(n):
                copy(w).wait()

        return [start, finish]

    out_shapes = [jax.ShapeDtypeStruct((N_SHARD, g.shape[1] // 2, g.shape[2]), F32) for g in grads]
    if small is not None:
        out_shapes.append(jax.ShapeDtypeStruct(small.shape, F32))
    return _Payload(arrays, out_shapes, {}, [pltpu.SemaphoreType.DMA((n,)), pltpu.SemaphoreType.DMA((n,))], phases)


def _cx_payload(pbs, lands):
    n = len(pbs)

    def phases(ins, outs, sems):
        send, recv = sems
        x, y, c = _mesh_pos()
        me = 2 * x + y

        def copy(w, j):
            t = (me + 1 + j) % N_SHARD
            return _remote(ins[w].at[t], outs[w].at[me], send.at[3 * w + j], recv.at[3 * w + j], (t // 2, t % 2, c))

        def start():
            for w in range(n):
                for j in range(3):
                    copy(w, j).start()

        def finish():
            for w in range(n):
                for j in range(3):
                    copy(w, j).wait()

        return [start, finish]

    return _Payload(list(pbs) + list(lands), [jax.ShapeDtypeStruct(p.shape, BF16) for p in lands],
                    {n + i: i for i in range(n)},
                    [pltpu.SemaphoreType.DMA((3 * n,)), pltpu.SemaphoreType.DMA((3 * n,))], phases)


def _pair_add(g, rv, core, chip, name):
    _, r, ncol = g.shape
    rh = r // 2

    def body(c_ref, me_ref, g_ref, rv_ref, pf_ref, pb_ref, land_ref):
        s = g_ref[0] + rv_ref[0]
        sb = s.astype(BF16)
        pb_ref[0] = sb

        @pl.when(pl.program_id(0) == me_ref[0])
        def _():
            pf_ref[...] = s
            land_ref[0] = sb

    slot = pl.BlockSpec((1, rh, ncol), lambda s, c, me: (s, 0, 0))
    grid_spec = pltpu.PrefetchScalarGridSpec(
        num_scalar_prefetch=2, grid=(N_SHARD,),
        in_specs=[pl.BlockSpec((1, rh, ncol), lambda s, c, me: (s, c[0], 0)), slot],
        out_specs=[pl.BlockSpec((rh, ncol), lambda s, c, me: (0, 0)), slot,
                   pl.BlockSpec((1, rh, ncol), lambda s, c, me: (me[0], 0, 0))])
    return pl.pallas_call(
        body, name=name, grid_spec=grid_spec,
        out_shape=[jax.ShapeDtypeStruct((rh, ncol), F32), jax.ShapeDtypeStruct((N_SHARD, rh, ncol), BF16),
                   jax.ShapeDtypeStruct((N_SHARD, rh, ncol), BF16)],
        compiler_params=_params(("arbitrary",)),
    )(core, chip, g, rv)


def _tail_reduce(grads, small):
    n = len(grads)
    _, r, ncol = grads[0].shape
    rh = r // 2

    def body(*refs):
        g_hbm, sm = refs[:n], refs[n]
        pf, land, sm_out = refs[n + 1:2 * n + 1], refs[2 * n + 1:3 * n + 1], refs[3 * n + 1]
        scr = refs[3 * n + 2:]
        rv, mine, sendb = scr[:n], scr[n:2 * n], scr[2 * n:3 * n]
        sm_rv, sm_sum, d_send, d_recv, load, i_send, i_recv, store = scr[3 * n:]
        x, y, c = _mesh_pos()
        me = 2 * x + y
        sib = (x, y, 1 - c)

        def pair(w):
            src = g_hbm[w].at[:, pl.ds((1 - c) * rh, rh), :] if w < n else sm
            return _remote(src, rv[w] if w < n else sm_rv, d_send.at[w], d_recv.at[w], sib)

        def chips(w, j):
            t = (me + 1 + j) % N_SHARD
            src = sendb[w].at[t] if w < n else sm_sum
            dst = land[w].at[me] if w < n else sm_out.at[me]
            return _remote(src, dst, i_send.at[3 * w + j], i_recv.at[3 * w + j], (t // 2, t % 2, c))

        loads = [pltpu.make_async_copy(g_hbm[w].at[:, pl.ds(c * rh, rh), :], mine[w], load.at[w]) for w in range(n)]
        for w in range(n + 1):
            pair(w).start()
        for cp in loads:
            cp.start()
        stores = []
        for w in range(n):
            loads[w].wait()
            pair(w).wait_recv()
            for k in range(N_SHARD):
                s = mine[w][k] + rv[w][k]
                mine[w][k] = s
                sendb[w][k] = s.astype(BF16)
            stores += [pltpu.make_async_copy(mine[w].at[me], pf[w], store.at[2 * w]),
                       pltpu.make_async_copy(sendb[w].at[me], land[w].at[me], store.at[2 * w + 1])]
            for cp in stores[-2:]:
                cp.start()
            for j in range(3):
                chips(w, j).start()
        pair(n).wait_recv()
        sm_sum[...] = sm[...] + sm_rv[...]
        stores.append(pltpu.make_async_copy(sm_sum, sm_out.at[me], store.at[2 * n]))
        stores[-1].start()
        for j in range(3):
            chips(n, j).start()
        for w in range(n + 1):
            pair(w).wait_send()
            for j in range(3):
                chips(w, j).wait()
        for cp in stores:
            cp.wait()

    half = (N_SHARD, rh, ncol)
    return _call(
        body, None, name="tail_reduce", grid=None,
        in_specs=[_ANY] * n + [_VMEM], out_specs=[_ANY] * (2 * n + 1),
        out_shape=([jax.ShapeDtypeStruct((rh, ncol), F32)] * n + [jax.ShapeDtypeStruct(half, BF16)] * n
                   + [jax.ShapeDtypeStruct((N_SHARD,) + small.shape, F32)]),
        scratch_shapes=([pltpu.VMEM(half, F32)] * (2 * n) + [pltpu.VMEM(half, BF16)] * n
                        + [pltpu.VMEM(small.shape, F32), pltpu.VMEM(small.shape, F32),
                           pltpu.SemaphoreType.DMA((n + 1,)), pltpu.SemaphoreType.DMA((n + 1,)),
                           pltpu.SemaphoreType.DMA((n,)), pltpu.SemaphoreType.DMA((3 * n + 3,)),
                           pltpu.SemaphoreType.DMA((3 * n + 3,)), pltpu.SemaphoreType.DMA((2 * n + 1,))]),
        operands=(*grads, small))


def _final_sum(pf, land, chip, core, name):
    _, rh, ncol = land.shape

    def body(me_ref, c_ref, pf_ref, land_ref, o_ref):
        me = me_ref[0]
        acc = jnp.zeros((rh, ncol), F32)
        for k in range(N_SHARD):
            acc = acc + jnp.where(me == k, pf_ref[...], land_ref[k].astype(F32))
        o_ref[...] = acc

    grid_spec = pltpu.PrefetchScalarGridSpec(
        num_scalar_prefetch=2, grid=(1,),
        in_specs=[pl.BlockSpec((rh, ncol), lambda i, me, c: (0, 0)),
                  pl.BlockSpec((N_SHARD, rh, ncol), lambda i, me, c: (0, 0, 0))],
        out_specs=pl.BlockSpec((rh, ncol), lambda i, me, c: (c[0], 0)))
    return pl.pallas_call(
        body, name=name, grid_spec=grid_spec, out_shape=jax.ShapeDtypeStruct((2 * rh, ncol), F32),
        compiler_params=_params(("arbitrary",)),
    )(chip, core, pf, land)


def _sibling_share(fulls, name):
    n = len(fulls)

    def body(*refs):
        outs = refs[n:2 * n]
        send, recv = refs[2 * n:]
        x, y, c = _mesh_pos()
        cps = []
        for w in range(n):
            rh = fulls[w].shape[0] // 2
            mine = outs[w].at[pl.ds(c * rh, rh), :]
            cp = pltpu.make_async_remote_copy(src_ref=mine, dst_ref=mine, send_sem=send.at[w], recv_sem=recv.at[w],
                                              device_id=(x, y, 1 - c), device_id_type=MESH)
            cp.start()
            cps.append(cp)
        for cp in cps:
            cp.wait()

    return pl.pallas_call(
        body, name=name,
        in_specs=[_ANY] * n, out_specs=[_ANY] * n,
        out_shape=[jax.ShapeDtypeStruct(f.shape, F32) for f in fulls],
        input_output_aliases={i: i for i in range(n)},
        scratch_shapes=[pltpu.SemaphoreType.DMA((n,)), pltpu.SemaphoreType.DMA((n,))],
    )(*fulls)


_ROW = {"rel_bias": 128, "sgu_b_s": 136, "norm_ffn1": 144, "norm_mix": 145, "norm_ffn2": 146, "norm_final": 147,
        "b_gate": 148, "sgu_ln_g": 150, "sgu_ln_b": 151}


def _pack_small(gs, loss):
    def body(ws, rel, bs, n1, nm, n2, nf, bg, lg, lb, loss_ref, o_ref):
        o_ref[...] = jnp.zeros_like(o_ref)
        o_ref[LOSS_ROW:LOSS_ROW + 1, 0:128] = loss_ref[...]
        for g in range(SGU_GROUPS):
            o_ref[0:SGU_BLOCK, g * SGU_BLOCK:(g + 1) * SGU_BLOCK] = ws[g]
        o_ref[128:136, 0:REL_PAD] = rel[...]
        o_ref[136:144, 0:SGU_BLOCK] = bs[...]
        o_ref[144:145, :] = n1[...]
        o_ref[145:146, :] = nm[...]
        o_ref[146:147, :] = n2[...]
        o_ref[147:148, :] = nf[...]
        o_ref[148:149, :] = bg[:, 0:D_MODEL]
        o_ref[149:150, :] = bg[:, D_MODEL:2 * D_MODEL]
        o_ref[150:151, 0:D_SGU] = lg[...]
        o_ref[151:152, 0:D_SGU] = lb[...]

    order = ("sgu_w_s", "rel_bias", "sgu_b_s", "norm_ffn1", "norm_mix", "norm_ffn2", "norm_final", "b_gate", "sgu_ln_g",
             "sgu_ln_b")
    return pl.pallas_call(body, name="pack_small", out_shape=jax.ShapeDtypeStruct((SMALL_ROWS, D_MODEL), F32))(
        *[gs[k] for k in order], loss)


def _adam(w, g, m, v):
    m2 = ADAM_B1 * m + (1.0 - ADAM_B1) * g
    v2 = ADAM_B2 * v + (1.0 - ADAM_B2) * (g * g)
    m_hat = m2 / (1.0 - ADAM_B1 ** ADAM_STEP)
    v_hat = v2 / (1.0 - ADAM_B2 ** ADAM_STEP)
    delta = -ADAM_LR * (m_hat / (jnp.sqrt(v_hat) + ADAM_EPS) + ADAM_WD * w)
    return delta, m2, v2


def _adam_small(sin, w, m, v):
    names = SMALL
    k = len(names)

    def body(*refs):
        sin_ref = refs[0]
        w_r, m_r, v_r = refs[1:1 + k], refs[1 + k:1 + 2 * k], refs[1 + 2 * k:1 + 3 * k]
        outs = refs[1 + 3 * k:]
        tot = sin_ref[0] + sin_ref[1] + sin_ref[2] + sin_ref[3]
        outs[4 * k][...] = tot[LOSS_ROW:LOSS_ROW + 1, 0:128]
        for i, name in enumerate(names):
            o = outs[4 * i:4 * i + 4]
            if name == "sgu_w_s":
                for gi in range(SGU_GROUPS):
                    g = tot[0:SGU_BLOCK, gi * SGU_BLOCK:(gi + 1) * SGU_BLOCK]
                    res = (g,) + _adam(w_r[i][gi], g, m_r[i][gi], v_r[i][gi])
                    for ref, val in zip(o, res):
                        ref[gi] = val
                continue
            r0 = _ROW[name]
            if name == "rel_bias":
                g = tot[r0:r0 + HEADS, 0:REL_PAD]
            elif name == "sgu_b_s":
                g = tot[r0:r0 + SGU_GROUPS, 0:SGU_BLOCK]
            elif name == "b_gate":
                g = jnp.concatenate([tot[r0:r0 + 1, :], tot[r0 + 1:r0 + 2, :]], axis=1)
            elif name in ("sgu_ln_g", "sgu_ln_b"):
                g = tot[r0:r0 + 1, 0:D_SGU]
            else:
                g = tot[r0:r0 + 1, :]
            res = (g,) + _adam(w_r[i][...], g, m_r[i][...], v_r[i][...])
            for ref, val in zip(o, res):
                ref[...] = val

    out_shape = []
    for name in names:
        out_shape += [jax.ShapeDtypeStruct(w[name].shape, F32)] * 4
    out_shape.append(jax.ShapeDtypeStruct((1, 128), F32))
    flat = pl.pallas_call(body, name="adam_small", out_shape=out_shape, compiler_params=_params())(
        sin, *[w[n] for n in names], *[m[n] for n in names], *[v[n] for n in names])
    return {name: tuple(flat[4 * i:4 * i + 4]) for i, name in enumerate(names)}, flat[4 * k]


def _adam_big(w, g, m, v, name):
    r, ncol = w.shape
    tr = 256 if r % 256 == 0 else r // 2

    def body(w_ref, g_ref, m_ref, v_ref, g2_ref, d_ref, m2_ref, v2_ref):
        gv = g_ref[...]
        g2_ref[...] = gv
        d_ref[...], m2_ref[...], v2_ref[...] = _adam(w_ref[...], gv, m_ref[...], v_ref[...])

    spec = pl.BlockSpec((tr, ncol), lambda i: (i, 0))
    return pl.pallas_call(
        body, name=name, grid=(r // tr,), in_specs=[spec] * 4, out_specs=[spec] * 4,
        out_shape=[jax.ShapeDtypeStruct(w.shape, F32)] * 4, compiler_params=_params(("arbitrary",)),
    )(w, g, m, v)


WEIGHTS = ("norm_ffn1", "ffn1_w_gate", "ffn1_w_up", "ffn1_w_down", "norm_mix", "w_in", "b_gate", "rel_bias", "sgu_ln_g",
           "sgu_ln_b", "sgu_w_s", "sgu_b_s", "w_branch_att", "w_branch_sgu", "w_out", "norm_ffn2", "ffn2_w_gate",
           "ffn2_w_up", "ffn2_w_down", "norm_final")


GATE_UP = ("ffn1_w_gate", "ffn1_w_up", "ffn2_w_gate", "ffn2_w_up")
_FFN = ("ffn1_w_gate", "ffn1_w_up", "ffn1_w_down", "ffn2_w_gate", "ffn2_w_up", "ffn2_w_down")
_CAST_GROUPS = ((_FFN, "cast_ffn"), (("w_in",), "cast_w_in"), (("w_branch_att", "w_branch_sgu"), "cast_branch"),
                (("w_out",), "cast_w_out"))


def _big_form(name, a):
    return jnp.swapaxes(a, 1, 2)[0] if name in GATE_UP else a[0]


def _big_back(name, a):
    return jnp.swapaxes(a[None], 1, 2) if name in GATE_UP else a[None]


def _small_form(name, a):
    if name == "norm_final":
        return a.reshape(1, D_MODEL)
    if name == "rel_bias":
        return jnp.pad(a[0], ((0, 0), (0, REL_PAD - N_REL)))
    if name in ("sgu_w_s", "sgu_b_s"):
        return a[0]
    return a


def _small_back(name, a, like):
    if name == "rel_bias":
        a = a[:, :N_REL]
    return a.reshape(like.shape)


def kernel(x, norm_ffn1, ffn1_w_gate, ffn1_w_up, ffn1_w_down, norm_mix, w_in, b_gate, rel_bias, sgu_ln_g, sgu_ln_b, sgu_w_s, sgu_b_s, w_branch_att, w_branch_sgu, w_out, norm_ffn2, ffn2_w_gate, ffn2_w_up, ffn2_w_down, norm_final, loss_target, m_norm_ffn1, m_ffn1_w_gate, m_ffn1_w_up, m_ffn1_w_down, m_norm_mix, m_w_in, m_b_gate, m_rel_bias, m_sgu_ln_g, m_sgu_ln_b, m_sgu_w_s, m_sgu_b_s, m_w_branch_att, m_w_branch_sgu, m_w_out, m_norm_ffn2, m_ffn2_w_gate, m_ffn2_w_up, m_ffn2_w_down, m_norm_final, v_norm_ffn1, v_ffn1_w_gate, v_ffn1_w_up, v_ffn1_w_down, v_norm_mix, v_w_in, v_b_gate, v_rel_bias, v_sgu_ln_g, v_sgu_ln_b, v_sgu_w_s, v_sgu_b_s, v_w_branch_att, v_w_branch_sgu, v_w_out, v_norm_ffn2, v_ffn2_w_gate, v_ffn2_w_up, v_ffn2_w_down, v_norm_final):
    w = dict(norm_ffn1=norm_ffn1, ffn1_w_gate=ffn1_w_gate, ffn1_w_up=ffn1_w_up, ffn1_w_down=ffn1_w_down, norm_mix=norm_mix,
             w_in=w_in, b_gate=b_gate, rel_bias=rel_bias, sgu_ln_g=sgu_ln_g, sgu_ln_b=sgu_ln_b, sgu_w_s=sgu_w_s,
             sgu_b_s=sgu_b_s, w_branch_att=w_branch_att, w_branch_sgu=w_branch_sgu, w_out=w_out, norm_ffn2=norm_ffn2,
             ffn2_w_gate=ffn2_w_gate, ffn2_w_up=ffn2_w_up, ffn2_w_down=ffn2_w_down, norm_final=norm_final)
    m = dict(norm_ffn1=m_norm_ffn1, ffn1_w_gate=m_ffn1_w_gate, ffn1_w_up=m_ffn1_w_up, ffn1_w_down=m_ffn1_w_down,
             norm_mix=m_norm_mix, w_in=m_w_in, b_gate=m_b_gate, rel_bias=m_rel_bias, sgu_ln_g=m_sgu_ln_g,
             sgu_ln_b=m_sgu_ln_b, sgu_w_s=m_sgu_w_s, sgu_b_s=m_sgu_b_s, w_branch_att=m_w_branch_att,
             w_branch_sgu=m_w_branch_sgu, w_out=m_w_out, norm_ffn2=m_norm_ffn2, ffn2_w_gate=m_ffn2_w_gate,
             ffn2_w_up=m_ffn2_w_up, ffn2_w_down=m_ffn2_w_down, norm_final=m_norm_final)
    v = dict(norm_ffn1=v_norm_ffn1, ffn1_w_gate=v_ffn1_w_gate, ffn1_w_up=v_ffn1_w_up, ffn1_w_down=v_ffn1_w_down,
             norm_mix=v_norm_mix, w_in=v_w_in, b_gate=v_b_gate, rel_bias=v_rel_bias, sgu_ln_g=v_sgu_ln_g,
             sgu_ln_b=v_sgu_ln_b, sgu_w_s=v_sgu_w_s, sgu_b_s=v_sgu_b_s, w_branch_att=v_w_branch_att,
             w_branch_sgu=v_w_branch_sgu, w_out=v_w_out, norm_ffn2=v_norm_ffn2, ffn2_w_gate=v_ffn2_w_gate,
             ffn2_w_up=v_ffn2_w_up, ffn2_w_down=v_ffn2_w_down, norm_final=v_norm_final)

    core = lax.axis_index("c").astype(jnp.int32).reshape(1)
    chip = (2 * lax.axis_index("x") + lax.axis_index("y")).astype(jnp.int32).reshape(1)

    wk = {n: _big_form(n, w[n]) for n in BIG}
    slots = {}
    for names, call in _CAST_GROUPS:
        slots.update(zip(names, _cast_slots([wk[n] for n in names], chip, call)))
    ws = {n: _small_form(n, w[n]) for n in SMALL}
    _, gx, shard_grads, small_sums = _local_step(x[0], loss_target[0], slots, ws, (core, chip))

    small, loss = _adam_small(small_sums, ws, {n: _small_form(n, m[n]) for n in SMALL},
                              {n: _small_form(n, v[n]) for n in SMALL})
    grad, delta, new_m, new_v = {}, {}, {}, {}
    for n in SMALL:
        grad[n], delta[n], new_m[n], new_v[n] = (_small_back(n, a, w[n]) for a in small[n])
    for n in BIG:
        g2, d2, m2, v2 = _adam_big(wk[n], shard_grads[n], _big_form(n, m[n]), _big_form(n, v[n]), "adam_" + n)
        grad[n], delta[n], new_m[n], new_v[n] = (_big_back(n, a) for a in (g2, d2, m2, v2))

    return (loss[0, 0], gx.reshape(x.shape), *[grad[n] for n in WEIGHTS], *[delta[n] for n in WEIGHTS],
            *[new_m[n] for n in WEIGHTS], *[new_v[n] for n in WEIGHTS])
```

```python
import functools

import jax
import jax.numpy as jnp
from jax import lax
from jax.experimental import pallas as pl
from jax.experimental.pallas import tpu as pltpu

F32 = jnp.float32
BF16 = jnp.bfloat16

D_MODEL = 1024
N_SHARD = 4
D_FF = 2816
FF_S = D_FF // N_SHARD
D_ATT = 512
D_SGU = 512
D_IN = 3 * D_ATT + 2 * D_SGU + 2 * D_MODEL
IN_S = D_IN // N_SHARD
BR_S = D_MODEL // N_SHARD
HEADS = 8
HEAD_DIM = 64
CHUNK = 64
N_LEFT = 8
BAND = (N_LEFT + 1) * CHUNK
REL_CLIP = 256
N_REL = 2 * REL_CLIP + 1
REL_PAD = 640
SGU_BLOCK = 128
SGU_GROUPS = 8
SGU_GDIM = 64
EPS = 1e-6
NEG_INF = -1e30

ATT_ROWS = 2 * CHUNK
ATT_KEYS = BAND + CHUNK
ATT_PAD = N_LEFT * CHUNK

ADAM_LR = 0.001
ADAM_B1 = 0.9
ADAM_B2 = 0.999
ADAM_EPS = 1e-08
ADAM_WD = 0.01
ADAM_STEP = 10

TM = 256
TW = 512
VMEM_LIMIT = 56 * 1024 * 1024

SMALL_ROWS = 160
LOSS_ROW = 152
MESH = pl.DeviceIdType.MESH

_NT = (((1,), (1,)), ((), ()))
_TN = (((0,), (0,)), ((), ()))


def _params(sem=None):
    return pltpu.CompilerParams(dimension_semantics=sem, vmem_limit_bytes=VMEM_LIMIT)


def _const_spec(shape):
    nd = len(shape)
    return pl.BlockSpec(shape, lambda *_: (0,) * nd, pipeline_mode=pl.Buffered(1))


def _acc_spec(shape):
    nd = len(shape)
    return pl.BlockSpec(shape, lambda *_: (0,) * nd)


def _row_spec(tm, ncols, off=0):
    return pl.BlockSpec((tm, ncols), lambda i: (i + off, 0))


def _row3_spec(tm, ncols):
    return pl.BlockSpec((N_SHARD, tm, ncols), lambda i: (0, i, 0))


def _dot(a, b):
    return jnp.dot(a, b, preferred_element_type=F32)


def _dot_nt(a, b):
    return lax.dot_general(a, b, _NT, preferred_element_type=F32)


def _dot_tn(a, b):
    return lax.dot_general(a, b, _TN, preferred_element_type=F32)


def _rms_fwd(x, g):
    r = lax.rsqrt(jnp.mean(x * x, axis=-1, keepdims=True) + EPS)
    xhat = x * r
    return xhat, r, xhat * g


def _rms_bwd(dh, xhat, r, g):
    dxhat = dh * g
    dx = r * (dxhat - xhat * jnp.mean(dxhat * xhat, axis=-1, keepdims=True))
    dg = jnp.sum(dh * xhat, axis=0, keepdims=True)
    return dx, dg


def _sigmoid(x):
    return 1.0 / (1.0 + jnp.exp(-x))


def _edges(n_steps):
    return [(0, True), (n_steps - 1, False)]


def _ffn_fwd(x, g, wg, wu, wd, name, payload=None):
    T = x.shape[0]

    def body(x_ref, g_ref, wg_ref, wu_ref, wd_ref, xo_ref, h_ref, a_ref, b_ref):
        xv = x_ref[...]
        hb = _rms_fwd(xv, g_ref[...])[2].astype(BF16)
        h_ref[...] = hb
        acc = jnp.zeros((TM, D_MODEL), F32)
        for s in range(N_SHARD):
            a = _dot_nt(hb, wg_ref[s])
            b = _dot_nt(hb, wu_ref[s])
            a_ref[s] = a.astype(BF16)
            b_ref[s] = b.astype(BF16)
            sv = a * _sigmoid(a) * b
            acc += _dot(sv.astype(BF16), wd_ref[s])
        xo_ref[...] = xv + 0.5 * acc

    return _call(
        body, payload, name=name, grid=(T // TM,), when=_edges(T // TM), sem=("arbitrary",),
        in_specs=[_row_spec(TM, D_MODEL), _const_spec((1, D_MODEL)), _const_spec(wg.shape), _const_spec(wu.shape),
                  _const_spec(wd.shape)],
        out_specs=[_row_spec(TM, D_MODEL), _row_spec(TM, D_MODEL), _row3_spec(TM, FF_S), _row3_spec(TM, FF_S)],
        out_shape=[jax.ShapeDtypeStruct((T, D_MODEL), F32), jax.ShapeDtypeStruct((T, D_MODEL), BF16),
                   jax.ShapeDtypeStruct((N_SHARD, T, FF_S), BF16), jax.ShapeDtypeStruct((N_SHARD, T, FF_S), BF16)],
        operands=(x, g, wg, wu, wd))


def _ffn_dgrad(dout, x, a, b, g, wg, wu, wd, name, payload=None):
    T = x.shape[0]

    def body(do_ref, x_ref, a_ref, b_ref, g_ref, wg_ref, wu_ref, wd_ref, dx_ref, da_ref, db_ref, dg_ref):
        do = do_ref[...]
        dob = do.astype(BF16)
        dh = jnp.zeros((TM, D_MODEL), F32)
        for s in range(N_SHARD):
            ds = 0.5 * _dot_nt(dob, wd_ref[s])
            av = a_ref[s].astype(F32)
            bv = b_ref[s].astype(F32)
            sig = _sigmoid(av)
            da = (ds * bv * (sig * (1.0 + av * (1.0 - sig)))).astype(BF16)
            db = (ds * (av * sig)).astype(BF16)
            da_ref[s] = da
            db_ref[s] = db
            dh += _dot(da, wg_ref[s]) + _dot(db, wu_ref[s])
        gv = g_ref[...]
        xhat, r, _ = _rms_fwd(x_ref[...], gv)
        dxn, dg = _rms_bwd(dh, xhat, r, gv)
        dx_ref[...] = do + dxn

        @pl.when(pl.program_id(0) == 0)
        def _():
            dg_ref[...] = jnp.zeros_like(dg_ref)

        dg_ref[...] += dg

    return _call(
        body, payload, name=name, grid=(T // TM,), when=_edges(T // TM), sem=("arbitrary",),
        in_specs=[_row_spec(TM, D_MODEL), _row_spec(TM, D_MODEL), _row3_spec(TM, FF_S), _row3_spec(TM, FF_S),
                  _const_spec((1, D_MODEL)), _const_spec(wg.shape), _const_spec(wu.shape), _const_spec(wd.shape)],
        out_specs=[_row_spec(TM, D_MODEL), _row3_spec(TM, FF_S), _row3_spec(TM, FF_S), _acc_spec((1, D_MODEL))],
        out_shape=[jax.ShapeDtypeStruct((T, D_MODEL), F32), jax.ShapeDtypeStruct((N_SHARD, T, FF_S), BF16),
                   jax.ShapeDtypeStruct((N_SHARD, T, FF_S), BF16), jax.ShapeDtypeStruct((1, D_MODEL), F32)],
        operands=(dout, x, a, b, g, wg, wu, wd))


def _ffn_wgrad(h, dout, a, b, da, db, name, payload=None):
    T = h.shape[0]

    def body(h_ref, do_ref, a_ref, b_ref, da_ref, db_ref, gwg_ref, gwu_ref, gwd_ref):
        @pl.when(pl.program_id(1) == 0)
        def _():
            gwg_ref[...] = jnp.zeros_like(gwg_ref)
            gwu_ref[...] = jnp.zeros_like(gwu_ref)
            gwd_ref[...] = jnp.zeros_like(gwd_ref)

        hv = h_ref[...]
        dob = do_ref[...].astype(BF16)
        av = a_ref[0].astype(F32)
        sv = (0.5 * av * _sigmoid(av) * b_ref[0].astype(F32)).astype(BF16)
        gwg_ref[0] += _dot_tn(da_ref[0], hv)
        gwu_ref[0] += _dot_tn(db_ref[0], hv)
        gwd_ref[0] += _dot_tn(sv, dob)

    tok = pl.BlockSpec((TW, D_MODEL), lambda s, i: (i, 0))
    act = pl.BlockSpec((1, TW, FF_S), lambda s, i: (s, i, 0))
    return _call(
        body, payload, name=name, grid=(N_SHARD, T // TW), when=_edges(N_SHARD * (T // TW)),
        sem=("arbitrary", "arbitrary"),
        in_specs=[tok, tok, act, act, act, act],
        out_specs=[pl.BlockSpec((1, FF_S, D_MODEL), lambda s, i: (s, 0, 0))] * 3,
        out_shape=[jax.ShapeDtypeStruct((N_SHARD, FF_S, D_MODEL), F32)] * 3,
        operands=(h, dout, a, b, da, db))


def _in_fwd(x, g, w_in, payload=None):
    T = x.shape[0]

    def body(x_ref, g_ref, w_ref, h_ref, qkv_ref, zs_ref, gl_ref):
        hb = _rms_fwd(x_ref[...], g_ref[...])[2].astype(BF16)
        h_ref[...] = hb
        z0 = _dot(hb, w_ref[0])
        qkv_ref[:, 0:IN_S] = z0.astype(BF16)
        z1 = _dot(hb, w_ref[1])
        qkv_ref[:, IN_S:3 * D_ATT] = z1[:, 0:384].astype(BF16)
        zs_ref[:, 0:768] = z1[:, 384:IN_S]
        z2 = _dot(hb, w_ref[2])
        zs_ref[:, 768:1024] = z2[:, 0:256]
        gl_ref[:, 0:896] = z2[:, 256:IN_S]
        gl_ref[:, 896:2048] = _dot(hb, w_ref[3])

    return _call(
        body, payload, name="in_fwd", grid=(T // TM,), when=_edges(T // TM), sem=("arbitrary",),
        in_specs=[_row_spec(TM, D_MODEL), _const_spec((1, D_MODEL)), _const_spec(w_in.shape)],
        out_specs=[_row_spec(TM, D_MODEL), _row_spec(TM, 3 * D_ATT), _row_spec(TM, 2 * D_SGU), _row_spec(TM, 2 * D_MODEL)],
        out_shape=[jax.ShapeDtypeStruct((T, D_MODEL), BF16), jax.ShapeDtypeStruct((T, 3 * D_ATT), BF16),
                   jax.ShapeDtypeStruct((T, 2 * D_SGU), F32), jax.ShapeDtypeStruct((T, 2 * D_MODEL), F32)],
        operands=(x, g, w_in))


def _in_dgrad(dx_res, x, g, w_in, dq, dk, dv, dzs, dgl):
    T = x.shape[0]

    def body(dxr_ref, x_ref, g_ref, w_ref, dq_ref, dk_ref, dv_ref, dzs_ref, dgl_ref, dx_ref, dz_ref, dg_ref):
        dz = jnp.concatenate([dq_ref[...], dk_ref[...].astype(BF16), dv_ref[...].astype(BF16), dzs_ref[...], dgl_ref[...]],
                             axis=1)
        dz_ref[...] = dz
        dh = jnp.zeros((TM, D_MODEL), F32)
        for s in range(N_SHARD):
            dh += _dot_nt(dz[:, s * IN_S:(s + 1) * IN_S], w_ref[s])
        gv = g_ref[...]
        xhat, r, _ = _rms_fwd(x_ref[...], gv)
        dxn, dg = _rms_bwd(dh, xhat, r, gv)
        dx_ref[...] = dxr_ref[...] + dxn

        @pl.when(pl.program_id(0) == 0)
        def _():
            dg_ref[...] = jnp.zeros_like(dg_ref)

        dg_ref[...] += dg

    pad_blocks = ATT_PAD // TM
    return pl.pallas_call(
        body, name="in_dgrad", grid=(T // TM,),
        in_specs=[_row_spec(TM, D_MODEL), _row_spec(TM, D_MODEL), _const_spec((1, D_MODEL)), _const_spec(w_in.shape),
                  _row_spec(TM, D_ATT), _row_spec(TM, D_ATT, pad_blocks), _row_spec(TM, D_ATT, pad_blocks),
                  _row_spec(TM, 2 * D_SGU), _row_spec(TM, 2 * D_MODEL)],
        out_specs=[_row_spec(TM, D_MODEL), _row_spec(TM, D_IN), _acc_spec((1, D_MODEL))],
        out_shape=[jax.ShapeDtypeStruct((T, D_MODEL), F32), jax.ShapeDtypeStruct((T, D_IN), BF16),
                   jax.ShapeDtypeStruct((1, D_MODEL), F32)],
        compiler_params=_params(("arbitrary",)),
    )(dx_res, x, g, w_in, dq, dk, dv, dzs, dgl)


def _in_wgrad(h, dz):
    T = h.shape[0]

    def body(h_ref, dz_ref, gw_ref):
        @pl.when(pl.program_id(1) == 0)
        def _():
            gw_ref[...] = jnp.zeros_like(gw_ref)

        gw_ref[0] += _dot_tn(h_ref[...], dz_ref[...])

    return pl.pallas_call(
        body, name="in_wgrad", grid=(N_SHARD, T // TW),
        in_specs=[pl.BlockSpec((TW, D_MODEL), lambda s, i: (i, 0)), pl.BlockSpec((TW, IN_S), lambda s, i: (i, s))],
        out_specs=pl.BlockSpec((1, D_MODEL, IN_S), lambda s, i: (s, 0, 0)),
        out_shape=jax.ShapeDtypeStruct((N_SHARD, D_MODEL, IN_S), F32),
        compiler_params=_params(("arbitrary", "arbitrary")),
    )(h, dz)


def _rel_onehot():
    r = lax.broadcasted_iota(jnp.int32, (REL_PAD, REL_PAD), 0)
    n = lax.broadcasted_iota(jnp.int32, (REL_PAD, REL_PAD), 1)
    idx = jnp.clip(BAND - 1 - n, -REL_CLIP, REL_CLIP) + REL_CLIP
    return jnp.where(r == idx, 1.0, 0.0).astype(BF16)


def _split3(v):
    p1 = v.astype(BF16)
    r1 = v - p1.astype(F32)
    p2 = r1.astype(BF16)
    p3 = (r1 - p2.astype(F32)).astype(BF16)
    return p1, p2, p3


def _relbias_fwd(tab_pad):
    def body(t_ref, o_ref):
        oh = _rel_onehot()
        acc = jnp.zeros((HEADS, REL_PAD), F32)
        for p in _split3(t_ref[...]):
            acc += _dot(p, oh)
        o_ref[...] = acc

    return pl.pallas_call(body, name="relbias_fwd", out_shape=jax.ShapeDtypeStruct((HEADS, REL_PAD), F32))(tab_pad)


def _relbias_bwd(z):
    def body(z_ref, o_ref):
        oh = _rel_onehot()
        dt2 = jnp.sum(z_ref[...], axis=1)
        acc = jnp.zeros((HEADS, REL_PAD), F32)
        for p in _split3(dt2):
            acc += _dot_nt(p, oh)
        o_ref[...] = acc

    return pl.pallas_call(body, name="relbias_bwd", out_shape=jax.ShapeDtypeStruct((HEADS, REL_PAD), F32))(z)


def _bias_blocks(t2):
    flat = jnp.tile(t2, (1, CHUNK))
    skew = flat[:, :CHUNK * (REL_PAD - 1)].reshape(HEADS, CHUNK, REL_PAD - 1)
    bias = skew[:, :, CHUNK - 1:CHUNK - 1 + BAND]
    slabs = [jnp.pad(bias, ((0, 0), (0, 0), (CHUNK * c, ATT_KEYS - BAND - CHUNK * c)), constant_values=NEG_INF)
             for c in range(2)]
    return jnp.concatenate(slabs, axis=1)


def _unskew(db2):
    out = []
    for c in range(2):
        slab = db2[:, CHUNK * c:CHUNK * (c + 1), CHUNK * c:CHUNK * c + BAND]
        y = jnp.pad(slab, ((0, 0), (0, 0), (CHUNK - 1, REL_PAD - BAND - CHUNK + 1)))
        yf = jnp.pad(y.reshape(HEADS, CHUNK * REL_PAD), ((0, 0), (0, CHUNK)))
        out.append(yf.reshape(HEADS, CHUNK, REL_PAD + 1)[:, :, :REL_PAD])
    return jnp.concatenate(out, axis=1)


def _att_load(qkv_hbm, q_s, k_s, v_s, sem, T):
    copies = [pltpu.make_async_copy(qkv_hbm.at[:, 0:D_ATT], q_s, sem.at[0]),
              pltpu.make_async_copy(qkv_hbm.at[:, D_ATT:2 * D_ATT], k_s.at[pl.ds(ATT_PAD, T), :], sem.at[1]),
              pltpu.make_async_copy(qkv_hbm.at[:, 2 * D_ATT:3 * D_ATT], v_s.at[pl.ds(ATT_PAD, T), :], sem.at[2])]
    for cp in copies:
        cp.start()
    k_s[0:ATT_PAD, :] = jnp.zeros((ATT_PAD, D_ATT), BF16)
    v_s[0:ATT_PAD, :] = jnp.zeros((ATT_PAD, D_ATT), BF16)
    for cp in copies:
        cp.wait()


def _head(v, h):
    return v[:, h * HEAD_DIM:(h + 1) * HEAD_DIM]


def _rows(v, h):
    return v[h * ATT_ROWS:(h + 1) * ATT_ROWS]


def _att_exp(qs, kw, bias_ref, valid):
    s = jnp.concatenate([_dot_nt(_head(qs, h), _head(kw, h)) + bias_ref[h] for h in range(HEADS)], axis=0)
    if valid is not None:
        s = jnp.where(valid, s, NEG_INF)
    e = jnp.exp(s - jnp.max(s, axis=-1, keepdims=True))
    return e, 1.0 / jnp.sum(e, axis=-1, keepdims=True)


def _att_blocks(T, block, keys_on_rows=False):
    n_edge = min(ATT_PAD // ATT_ROWS, T // ATT_ROWS)
    shape, axis = ((ATT_KEYS, 1), 0) if keys_on_rows else ((1, ATT_KEYS), 1)

    def edge(i, carry):
        r0 = i * ATT_ROWS
        block(i, (lax.broadcasted_iota(jnp.int32, shape, axis) + (r0 - ATT_PAD)) >= 0)
        return carry

    def inner(i, carry):
        block(i, None)
        return carry

    lax.fori_loop(0, n_edge, edge, 0)
    lax.fori_loop(n_edge, T // ATT_ROWS, inner, 0)


def _att_fwd(qkv, bias2, payload=None):
    T = qkv.shape[0]

    def body(qkv_hbm, bias_ref, y_ref, q_s, k_s, v_s, sem):
        _att_load(qkv_hbm, q_s, k_s, v_s, sem, T)

        def block(i, valid):
            r0 = pl.multiple_of(i * ATT_ROWS, ATT_ROWS)
            qs = q_s[pl.ds(r0, ATT_ROWS), :] * (HEAD_DIM ** -0.5)
            kw = k_s[pl.ds(r0, ATT_KEYS), :]
            vw = v_s[pl.ds(r0, ATT_KEYS), :]
            e, rinv = _att_exp(qs, kw, bias_ref, valid)
            eb = e.astype(BF16)
            outs = [_dot(_rows(eb, h), _head(vw, h)) * _rows(rinv, h) for h in range(HEADS)]
            y_ref[pl.ds(r0, ATT_ROWS), :] = jnp.concatenate(outs, axis=1).astype(BF16)

        _att_blocks(T, block)

    return _call(
        body, payload, name="att_fwd", grid=None,
        in_specs=[pl.BlockSpec(memory_space=pl.ANY), pl.BlockSpec(memory_space=pltpu.VMEM)],
        out_specs=[pl.BlockSpec(memory_space=pltpu.VMEM)],
        out_shape=[jax.ShapeDtypeStruct((T, D_ATT), BF16)],
        scratch_shapes=[pltpu.VMEM((T, D_ATT), BF16), pltpu.VMEM((T + ATT_PAD, D_ATT), BF16),
                        pltpu.VMEM((T + ATT_PAD, D_ATT), BF16), pltpu.SemaphoreType.DMA((3,))],
        operands=(qkv, bias2))


def _lanes(v, h):
    return v[:, h * ATT_ROWS:(h + 1) * ATT_ROWS]


def _att_bwd(qkv, dy, bias2t, payload=None):
    T = qkv.shape[0]

    def body(qkv_hbm, dy_ref, bias_ref, dq_ref, dk_ref, dv_ref, db_ref, q_s, k_s, v_s, sem):
        _att_load(qkv_hbm, q_s, k_s, v_s, sem, T)
        dk_ref[...] = jnp.zeros_like(dk_ref)
        dv_ref[...] = jnp.zeros_like(dv_ref)
        db_ref[...] = jnp.zeros_like(db_ref)

        def block(i, valid):
            r0 = pl.multiple_of(i * ATT_ROWS, ATT_ROWS)
            qs = q_s[pl.ds(r0, ATT_ROWS), :] * (HEAD_DIM ** -0.5)
            kw = k_s[pl.ds(r0, ATT_KEYS), :]
            vw = v_s[pl.ds(r0, ATT_KEYS), :]
            dyb = dy_ref[pl.ds(r0, ATT_ROWS), :]
            s = jnp.concatenate([_dot_nt(_head(kw, h), _head(qs, h)) + bias_ref[h] for h in range(HEADS)], axis=1)
            if valid is not None:
                s = jnp.where(valid, s, NEG_INF)
            e = jnp.exp(s - jnp.max(s, axis=0, keepdims=True))
            p = e * (1.0 / jnp.sum(e, axis=0, keepdims=True))
            dp = jnp.concatenate([_dot_nt(_head(vw, h), _head(dyb, h)) for h in range(HEADS)], axis=1)
            ds = p * (dp - jnp.sum(p * dp, axis=0, keepdims=True))
            for h in range(HEADS):
                db_ref[h] += _lanes(ds, h)
            dsb = ds.astype(BF16)
            pb = p.astype(BF16)
            dq = [_dot_tn(_lanes(dsb, h), _head(kw, h)) for h in range(HEADS)]
            dk = [_dot(_lanes(dsb, h), _head(qs, h)) for h in range(HEADS)]
            dv = [_dot(_lanes(pb, h), _head(dyb, h)) for h in range(HEADS)]
            dq_ref[pl.ds(r0, ATT_ROWS), :] = (jnp.concatenate(dq, axis=1) * (HEAD_DIM ** -0.5)).astype(BF16)
            dk_ref[pl.ds(r0, ATT_KEYS), :] += jnp.concatenate(dk, axis=1)
            dv_ref[pl.ds(r0, ATT_KEYS), :] += jnp.concatenate(dv, axis=1)

        _att_blocks(T, block, keys_on_rows=True)

    vmem = pl.BlockSpec(memory_space=pltpu.VMEM)
    return _call(
        body, payload, name="att_bwd", grid=None,
        in_specs=[pl.BlockSpec(memory_space=pl.ANY), vmem, vmem],
        out_specs=[vmem, vmem, vmem, vmem],
        out_shape=[jax.ShapeDtypeStruct((T, D_ATT), BF16), jax.ShapeDtypeStruct((T + ATT_PAD, D_ATT), F32),
                   jax.ShapeDtypeStruct((T + ATT_PAD, D_ATT), F32), jax.ShapeDtypeStruct((HEADS, ATT_KEYS, ATT_ROWS), F32)],
        scratch_shapes=[pltpu.VMEM((T, D_ATT), BF16), pltpu.VMEM((T + ATT_PAD, D_ATT), BF16),
                        pltpu.VMEM((T + ATT_PAD, D_ATT), BF16), pltpu.SemaphoreType.DMA((3,))],
        operands=(qkv, dy, bias2t))


_GELU_C = 0.7978845608028654
_GELU_A = 0.044715


def _gelu(x):
    t = jnp.tanh(_GELU_C * (x + _GELU_A * x * x * x))
    return 0.5 * x * (1.0 + t), t


def _gelu_grad(x, t):
    return 0.5 * (1.0 + t) + 0.5 * x * (1.0 - t * t) * _GELU_C * (1.0 + 3.0 * _GELU_A * x * x)


def _group_masks():
    col = lax.broadcasted_iota(jnp.int32, (SGU_GROUPS, D_SGU), 1) // SGU_GDIM
    grp = lax.broadcasted_iota(jnp.int32, (SGU_GROUPS, D_SGU), 0)
    return jnp.where(col == grp, 1.0, 0.0).astype(F32)


def _causal_mask(transposed=False):
    i = lax.broadcasted_iota(jnp.int32, (SGU_BLOCK, SGU_BLOCK), 0) // CHUNK
    j = lax.broadcasted_iota(jnp.int32, (SGU_BLOCK, SGU_BLOCK), 1) // CHUNK
    return (j >= i) if transposed else (i >= j)


def _sgu_norm(zs, lng, lnb):
    gz, t = _gelu(zs)
    u = gz[:, 0:D_SGU]
    vs = gz[:, D_SGU:2 * D_SGU]
    xc = vs - jnp.mean(vs, axis=-1, keepdims=True)
    rstd = lax.rsqrt(jnp.mean(xc * xc, axis=-1, keepdims=True) + EPS)
    xhat = xc * rstd
    return t, u, xhat, rstd, xhat * lng + lnb


def _sgu_mix(vn_blk, w_ref, bst, gm):
    mask = _causal_mask()
    s = jnp.zeros((SGU_BLOCK, D_SGU), F32)
    for g in range(SGU_GROUPS):
        wm = jnp.where(mask, w_ref[g], 0.0).astype(BF16)
        s += _dot(wm, (vn_blk * gm[g:g + 1, :]).astype(BF16))
        s += bst[:, g:g + 1] * gm[g:g + 1, :]
    return s


def _sgu_fwd(zs, lng, lnb, w_s, bst):
    T = zs.shape[0]
    nblk = TM // SGU_BLOCK

    def body(zs_ref, lng_ref, lnb_ref, w_ref, bst_ref, y_ref):
        _, u, _, _, vn = _sgu_norm(zs_ref[...], lng_ref[...], lnb_ref[...])
        gm = _group_masks()
        bst_v = bst_ref[...]
        for n in range(nblk):
            rows = slice(n * SGU_BLOCK, (n + 1) * SGU_BLOCK)
            s = _sgu_mix(vn[rows], w_ref, bst_v, gm)
            y_ref[rows, :] = (u[rows] * s).astype(BF16)

    return pl.pallas_call(
        body, name="sgu_fwd", grid=(T // TM,),
        in_specs=[_row_spec(TM, 2 * D_SGU), _const_spec((1, D_SGU)), _const_spec((1, D_SGU)),
                  _const_spec(w_s.shape), _const_spec(bst.shape)],
        out_specs=_row_spec(TM, D_SGU),
        out_shape=jax.ShapeDtypeStruct((T, D_SGU), BF16),
        compiler_params=_params(("arbitrary",)),
    )(zs, lng, lnb, w_s, bst)


def _sgu_bwd(zs, dy, lng, lnb, w_s, w_st, bst):
    T = zs.shape[0]
    nblk = TM // SGU_BLOCK

    def body(zs_ref, dy_ref, lng_ref, lnb_ref, w_ref, wt_ref, bst_ref, dzs_ref, dw_ref, dbt_ref, dlg_ref, dlb_ref):
        @pl.when(pl.program_id(0) == 0)
        def _():
            dw_ref[...] = jnp.zeros_like(dw_ref)
            dbt_ref[...] = jnp.zeros_like(dbt_ref)
            dlg_ref[...] = jnp.zeros_like(dlg_ref)
            dlb_ref[...] = jnp.zeros_like(dlb_ref)

        zs_v = zs_ref[...]
        lng_v = lng_ref[...]
        t, u, xhat, rstd, vn = _sgu_norm(zs_v, lng_v, lnb_ref[...])
        gm = _group_masks()
        bst_v = bst_ref[...]
        mask = _causal_mask()
        mask_t = _causal_mask(transposed=True)
        dyv = dy_ref[...].astype(F32)
        lane8 = lax.broadcasted_iota(jnp.int32, (1, SGU_GROUPS), 1)
        du_rows, dvn_rows = [], []
        for n in range(nblk):
            rows = slice(n * SGU_BLOCK, (n + 1) * SGU_BLOCK)
            vn_b = vn[rows]
            s = _sgu_mix(vn_b, w_ref, bst_v, gm)
            du_rows.append(dyv[rows] * s)
            dsb = dyv[rows] * u[rows]
            vnb16 = vn_b.astype(BF16)
            dvn = jnp.zeros((SGU_BLOCK, D_SGU), F32)
            dbt = jnp.zeros((SGU_BLOCK, SGU_GROUPS), F32)
            for g in range(SGU_GROUPS):
                dsg = dsb * gm[g:g + 1, :]
                dsg16 = dsg.astype(BF16)
                wmt = jnp.where(mask_t, wt_ref[g], 0.0).astype(BF16)
                dvn += _dot(wmt, dsg16)
                dw_ref[g] += jnp.where(mask, _dot_nt(dsg16, vnb16), 0.0)
                dbt += jnp.sum(dsg, axis=-1, keepdims=True) * jnp.where(lane8 == g, 1.0, 0.0)
            dbt_ref[...] += dbt
            dvn_rows.append(dvn)
        du = jnp.concatenate(du_rows, axis=0)
        dvn = jnp.concatenate(dvn_rows, axis=0)
        dlg_ref[...] += jnp.sum(dvn * xhat, axis=0, keepdims=True)
        dlb_ref[...] += jnp.sum(dvn, axis=0, keepdims=True)
        dxhat = dvn * lng_v
        dvs = rstd * (dxhat - jnp.mean(dxhat, axis=-1, keepdims=True)
                      - xhat * jnp.mean(dxhat * xhat, axis=-1, keepdims=True))
        dgz = jnp.concatenate([du, dvs], axis=1)
        dzs_ref[...] = (dgz * _gelu_grad(zs_v, t)).astype(BF16)

    return pl.pallas_call(
        body, name="sgu_bwd", grid=(T // TM,),
        in_specs=[_row_spec(TM, 2 * D_SGU), _row_spec(TM, D_SGU), _const_spec((1, D_SGU)), _const_spec((1, D_SGU)),
                  _const_spec(w_s.shape), _const_spec(w_st.shape), _const_spec(bst.shape)],
        out_specs=[_row_spec(TM, 2 * D_SGU), _acc_spec(w_s.shape), _acc_spec(bst.shape), _acc_spec((1, D_SGU)),
                   _acc_spec((1, D_SGU))],
        out_shape=[jax.ShapeDtypeStruct((T, 2 * D_SGU), BF16), jax.ShapeDtypeStruct(w_s.shape, F32),
                   jax.ShapeDtypeStruct(bst.shape, F32), jax.ShapeDtypeStruct((1, D_SGU), F32),
                   jax.ShapeDtypeStruct((1, D_SGU), F32)],
        compiler_params=_params(("arbitrary",)),
    )(zs, dy, lng, lnb, w_s, w_st, bst)


def _cols(v, s):
    return v[:, s * BR_S:(s + 1) * BR_S]


def _merge_fwd(x, y_att, y_sgu, gl, b_gate, wba, wbs, wo, payload=None):
    T = x.shape[0]

    def body(x_ref, ya_ref, ys_ref, gl_ref, bg_ref, wba_ref, wbs_ref, wo_ref, xo_ref, m_ref, pa_ref, ps_ref):
        ya = ya_ref[...]
        ys = ys_ref[...]
        pa = jnp.concatenate([_dot(ya, wba_ref[s]) for s in range(N_SHARD)], axis=1)
        ps = jnp.concatenate([_dot(ys, wbs_ref[s]) for s in range(N_SHARD)], axis=1)
        g = _sigmoid(gl_ref[...] + bg_ref[...])
        mb = (g[:, 0:D_MODEL] * pa + g[:, D_MODEL:2 * D_MODEL] * ps).astype(BF16)
        m_ref[...] = mb
        pa_ref[...] = pa.astype(BF16)
        ps_ref[...] = ps.astype(BF16)
        acc = jnp.zeros((TM, D_MODEL), F32)
        for s in range(N_SHARD):
            acc += _dot(_cols(mb, s), wo_ref[s])
        xo_ref[...] = x_ref[...] + acc

    tokd = jax.ShapeDtypeStruct((T, D_MODEL), BF16)
    return _call(
        body, payload, name="merge_fwd", grid=(T // TM,), when=_edges(T // TM), sem=("arbitrary",),
        in_specs=[_row_spec(TM, D_MODEL), _row_spec(TM, D_ATT), _row_spec(TM, D_SGU), _row_spec(TM, 2 * D_MODEL),
                  _const_spec((1, 2 * D_MODEL)), _const_spec(wba.shape), _const_spec(wbs.shape), _const_spec(wo.shape)],
        out_specs=[_row_spec(TM, D_MODEL)] * 4,
        out_shape=[jax.ShapeDtypeStruct((T, D_MODEL), F32), tokd, tokd, tokd],
        operands=(x, y_att, y_sgu, gl, b_gate, wba, wbs, wo))


def _merge_bwd(dx, y_att, y_sgu, gl, merged, pa, ps, b_gate, wba, wbs, wo, payload=None):
    T = dx.shape[0]

    def body(dx_ref, ya_ref, ys_ref, gl_ref, m_ref, pa_ref, ps_ref, bg_ref, wba_ref, wbs_ref, wo_ref,
             dya_ref, dys_ref, dgl_ref, dbg_ref, gwba_ref, gwbs_ref, gwo_ref):
        @pl.when(pl.program_id(0) == 0)
        def _():
            dbg_ref[...] = jnp.zeros_like(dbg_ref)
            gwba_ref[...] = jnp.zeros_like(gwba_ref)
            gwbs_ref[...] = jnp.zeros_like(gwbs_ref)
            gwo_ref[...] = jnp.zeros_like(gwo_ref)

        dxb = dx_ref[...].astype(BF16)
        dm = jnp.concatenate([_dot_nt(dxb, wo_ref[s]) for s in range(N_SHARD)], axis=1)
        g = _sigmoid(gl_ref[...] + bg_ref[...])
        ga = g[:, 0:D_MODEL]
        gs = g[:, D_MODEL:2 * D_MODEL]
        dpa = (dm * ga).astype(BF16)
        dps = (dm * gs).astype(BF16)
        dgl = jnp.concatenate([dm * pa_ref[...].astype(F32) * ga * (1.0 - ga),
                               dm * ps_ref[...].astype(F32) * gs * (1.0 - gs)], axis=1)
        dgl_ref[...] = dgl.astype(BF16)
        dbg_ref[...] += jnp.sum(dgl, axis=0, keepdims=True)
        ya = ya_ref[...]
        ys = ys_ref[...]
        mb = m_ref[...]
        dya = jnp.zeros((TM, D_ATT), F32)
        dys = jnp.zeros((TM, D_SGU), F32)
        for s in range(N_SHARD):
            dya += _dot_nt(_cols(dpa, s), wba_ref[s])
            dys += _dot_nt(_cols(dps, s), wbs_ref[s])
            gwo_ref[s] += _dot_tn(_cols(mb, s), dxb)
            gwba_ref[s] += _dot_tn(ya, _cols(dpa, s))
            gwbs_ref[s] += _dot_tn(ys, _cols(dps, s))
        dya_ref[...] = dya.astype(BF16)
        dys_ref[...] = dys.astype(BF16)

    return _call(
        body, payload, name="merge_bwd", grid=(T // TM,), when=_edges(T // TM), sem=("arbitrary",),
        operands=(dx, y_att, y_sgu, gl, merged, pa, ps, b_gate, wba, wbs, wo),
        in_specs=[_row_spec(TM, D_MODEL), _row_spec(TM, D_ATT), _row_spec(TM, D_SGU), _row_spec(TM, 2 * D_MODEL),
                  _row_spec(TM, D_MODEL), _row_spec(TM, D_MODEL), _row_spec(TM, D_MODEL),
                  _const_spec((1, 2 * D_MODEL)), _const_spec(wba.shape), _const_spec(wbs.shape), _const_spec(wo.shape)],
        out_specs=[_row_spec(TM, D_ATT), _row_spec(TM, D_SGU), _row_spec(TM, 2 * D_MODEL), _acc_spec((1, 2 * D_MODEL)),
                   _acc_spec(wba.shape), _acc_spec(wbs.shape), _acc_spec(wo.shape)],
        out_shape=[jax.ShapeDtypeStruct((T, D_ATT), BF16), jax.ShapeDtypeStruct((T, D_SGU), BF16),
                   jax.ShapeDtypeStruct((T, 2 * D_MODEL), BF16), jax.ShapeDtypeStruct((1, 2 * D_MODEL), F32),
                   jax.ShapeDtypeStruct(wba.shape, F32), jax.ShapeDtypeStruct(wbs.shape, F32),
                   jax.ShapeDtypeStruct(wo.shape, F32)])


def _loss_bwd(x, target, g):
    T = x.shape[0]

    def body(x_ref, t_ref, g_ref, dx_ref, loss_ref, dg_ref):
        @pl.when(pl.program_id(0) == 0)
        def _():
            loss_ref[...] = jnp.zeros_like(loss_ref)
            dg_ref[...] = jnp.zeros_like(dg_ref)

        gv = g_ref[...]
        xhat, r, y = _rms_fwd(x_ref[...], gv)
        err = y - t_ref[...]
        per_tok = jnp.mean(err * err, axis=-1, keepdims=True)
        loss_ref[...] += 0.5 * jnp.sum(per_tok, axis=0, keepdims=True)
        dxn, dg = _rms_bwd(err * (1.0 / D_MODEL), xhat, r, gv)
        dx_ref[...] = dxn
        dg_ref[...] += dg

    return pl.pallas_call(
        body, name="loss_bwd", grid=(T // TM,),
        in_specs=[_row_spec(TM, D_MODEL), _row_spec(TM, D_MODEL), _const_spec((1, D_MODEL))],
        out_specs=[_row_spec(TM, D_MODEL), _acc_spec((1, 128)), _acc_spec((1, D_MODEL))],
        out_shape=[jax.ShapeDtypeStruct((T, D_MODEL), F32), jax.ShapeDtypeStruct((1, 128), F32),
                   jax.ShapeDtypeStruct((1, D_MODEL), F32)],
        compiler_params=_params(("arbitrary",)),
    )(x, target, g)


BIG = ("ffn1_w_gate", "ffn1_w_up", "ffn1_w_down", "w_in", "w_branch_att", "w_branch_sgu", "w_out",
       "ffn2_w_gate", "ffn2_w_up", "ffn2_w_down")
SMALL = ("norm_ffn1", "norm_mix", "b_gate", "rel_bias", "sgu_ln_g", "sgu_ln_b", "sgu_w_s", "sgu_b_s", "norm_ffn2",
         "norm_final")


G_FFN1 = ("ffn1_w_gate", "ffn1_w_up", "ffn1_w_down")
G_MIX = ("w_in", "w_branch_att", "w_branch_sgu", "w_out")
G_FFN2 = ("ffn2_w_gate", "ffn2_w_up", "ffn2_w_down")


def _local_step(x, target, wb, ws, dist=None):
    def gather_on(names):
        return _ag_payload([wb[n] for n in names]) if dist else None

    t2 = _relbias_fwd(ws["rel_bias"])
    bias2 = _bias_blocks(t2)
    bst = ws["sgu_b_s"].T
    w_st = jnp.swapaxes(ws["sgu_w_s"], 1, 2)

    if dist:
        wb.update(zip(G_FFN1, _call(lambda: None, gather_on(G_FFN1), name="allgather_ffn1", grid=None, in_specs=[],
                                    out_specs=[], out_shape=[])))
    x1, h1, a1, b1, *got = _ffn_fwd(x, ws["norm_ffn1"], wb["ffn1_w_gate"], wb["ffn1_w_up"], wb["ffn1_w_down"],
                                    "ffn1_fwd", gather_on(G_MIX))
    wb.update(zip(G_MIX, got))
    h2, qkv, zs, gl, *got = _in_fwd(x1, ws["norm_mix"], wb["w_in"], gather_on(G_FFN2[0:1]))
    wb.update(zip(G_FFN2[0:1], got))
    y_att, *got = _att_fwd(qkv, bias2, gather_on(G_FFN2[1:2]))
    wb.update(zip(G_FFN2[1:2], got))
    y_sgu = _sgu_fwd(zs, ws["sgu_ln_g"], ws["sgu_ln_b"], ws["sgu_w_s"], bst)
    x2, merged, pa, ps, *got = _merge_fwd(x1, y_att, y_sgu, gl, ws["b_gate"], wb["w_branch_att"], wb["w_branch_sgu"],
                                          wb["w_out"], gather_on(G_FFN2[2:3]))
    wb.update(zip(G_FFN2[2:3], got))
    x3, h3, a3, b3 = _ffn_fwd(x2, ws["norm_ffn2"], wb["ffn2_w_gate"], wb["ffn2_w_up"], wb["ffn2_w_down"], "ffn2_fwd")
    dx3, loss, g_final = _loss_bwd(x3, target, ws["norm_final"])

    gb, gs, sums = {}, {"norm_final": g_final}, {}

    def pair_on(names, small=None):
        return _px_payload([gb[n] for n in names], small) if dist else None

    def pair_add(names, halves):
        for n, rv in zip(names, halves):
            sums[n] = _pair_add(gb[n], rv, dist[0], dist[1], "pair_add_" + n)

    def chips_on(names):
        return _cx_payload([sums[n][1] for n in names], [sums[n][2] for n in names]) if dist else None

    dx2, da3, db3, gs["norm_ffn2"] = _ffn_dgrad(dx3, x2, a3, b3, ws["norm_ffn2"], wb["ffn2_w_gate"], wb["ffn2_w_up"],
                                                wb["ffn2_w_down"], "ffn2_dgrad")
    gb["ffn2_w_gate"], gb["ffn2_w_up"], gb["ffn2_w_down"] = _ffn_wgrad(h3, dx3, a3, b3, da3, db3, "ffn2_wgrad")
    dy_att, dy_sgu, dgl, gs["b_gate"], gb["w_branch_att"], gb["w_branch_sgu"], gb["w_out"], *got = _merge_bwd(
        dx2, y_att, y_sgu, gl, merged, pa, ps, ws["b_gate"], wb["w_branch_att"], wb["w_branch_sgu"], wb["w_out"],
        pair_on(G_FFN2))
    pair_add(G_FFN2, got)
    dq, dk, dv, db2t, *lands2 = _att_bwd(qkv, dy_att, jnp.swapaxes(bias2, 1, 2), chips_on(G_FFN2))
    gs["rel_bias"] = _relbias_bwd(_unskew(jnp.swapaxes(db2t, 1, 2)))
    dzs, gs["sgu_w_s"], dbt, gs["sgu_ln_g"], gs["sgu_ln_b"] = _sgu_bwd(zs, dy_sgu, ws["sgu_ln_g"], ws["sgu_ln_b"],
                                                                      ws["sgu_w_s"], w_st, bst)
    gs["sgu_b_s"] = dbt.T
    dx1, dz, gs["norm_mix"] = _in_dgrad(dx2, x1, ws["norm_mix"], wb["w_in"], dq, dk, dv, dzs, dgl)
    gb["w_in"] = _in_wgrad(h2, dz)
    gx, da1, db1, gs["norm_ffn1"], *got = _ffn_dgrad(dx1, x, a1, b1, ws["norm_ffn1"], wb["ffn1_w_gate"],
                                                    wb["ffn1_w_up"], wb["ffn1_w_down"], "ffn1_dgrad", pair_on(G_MIX))
    pair_add(G_MIX, got)
    gb["ffn1_w_gate"], gb["ffn1_w_up"], gb["ffn1_w_down"], *lands_mix = _ffn_wgrad(h1, dx1, a1, b1, da1, db1,
                                                                                   "ffn1_wgrad", chips_on(G_MIX))
    if not dist:
        return loss, gx, gb, gs

    tail = _tail_reduce([gb[n] for n in G_FFN1], _pack_small(gs, loss))
    for i, n in enumerate(G_FFN1):
        sums[n] = (tail[i],)
    lands1, small_sums = tail[len(G_FFN1):2 * len(G_FFN1)], tail[-1]
    lands = dict(zip(G_FFN2 + G_MIX + G_FFN1, list(lands2) + list(lands_mix) + list(lands1)))
    fulls = [_final_sum(sums[n][0], lands[n], dist[1], dist[0], "final_sum_" + n) for n in BIG]
    return loss, gx, dict(zip(BIG, _sibling_share(fulls, "sibling_share"))), small_sums


_ANY = pl.BlockSpec(memory_space=pl.ANY)
_VMEM = pl.BlockSpec(memory_space=pltpu.VMEM)


def _mesh_pos():
    return lax.axis_index("x"), lax.axis_index("y"), lax.axis_index("c")


def _cast_slots(shards, chip, name):
    n = len(shards)
    r, ncol = shards[0].shape
    tr = r // 2

    def body(me_ref, *refs):
        for i_ref, o_ref in zip(refs[:n], refs[n:]):
            o_ref[0] = i_ref[...].astype(BF16)

    grid_spec = pltpu.PrefetchScalarGridSpec(
        num_scalar_prefetch=1, grid=(r // tr,),
        in_specs=[pl.BlockSpec((tr, ncol), lambda i, me: (i, 0))] * n,
        out_specs=[pl.BlockSpec((1, tr, ncol), lambda i, me: (me[0], i, 0))] * n)
    return pl.pallas_call(
        body, name=name, grid_spec=grid_spec,
        out_shape=[jax.ShapeDtypeStruct((N_SHARD, r, ncol), BF16)] * n,
        compiler_params=_params(("arbitrary",)),
    )(chip, *shards)


class _Payload:
    def __init__(self, arrays, out_shapes, aliases, scratch, phases):
        self.arrays = list(arrays)
        self.out_shapes = list(out_shapes)
        self.aliases = dict(aliases)
        self.scratch = list(scratch)
        self.phases = phases


def _remote(src, dst, ssem, rsem, dev):
    return pltpu.make_async_remote_copy(src_ref=src, dst_ref=dst, send_sem=ssem, recv_sem=rsem, device_id=dev,
                                        device_id_type=MESH)


def _call(body, payload, *, name, grid, in_specs, out_specs, out_shape, scratch_shapes=(), sem=None, when=None,
          operands=()):
    in_specs, out_specs, out_shape = list(in_specs), list(out_specs), list(out_shape)
    scratch_shapes = list(scratch_shapes)
    n_in, n_out, n_scr = len(in_specs), len(out_specs), len(scratch_shapes)
    kwargs = {}
    kernel = body
    if payload is not None:
        k_in, k_out = len(payload.arrays), len(payload.out_shapes)
        rank = len(grid) if grid else 0

        def kernel(*refs):
            a, b = n_in, n_in + k_in
            c, d = b + n_out, b + n_out + k_out
            e = d + n_scr
            phases = payload.phases(refs[a:b], refs[c:d], refs[e:])

            def run():
                body(*refs[:a], *refs[b:c], *refs[d:e])

            if not grid:
                phases[0]()
                run()
                for ph in phases[1:]:
                    ph()
                return
            step = pl.program_id(0)
            if rank == 2:
                step = step * grid[1] + pl.program_id(1)
            marks = list(when)
            if len(phases) == 3:
                marks = [when[0], (max(when[1][0] - 3, 0), False), when[1]]
            for ph, (at, before) in zip(phases, marks):
                if before:
                    pl.when(step == at)(ph)
            run()
            for ph, (at, before) in zip(phases, marks):
                if not before:
                    pl.when(step == at)(ph)

        in_specs += [_ANY] * k_in
        out_specs += [_ANY] * k_out
        out_shape += payload.out_shapes
        scratch_shapes += payload.scratch
        kwargs["input_output_aliases"] = {n_in + i: n_out + j for i, j in payload.aliases.items()}
        operands = tuple(operands) + tuple(payload.arrays)
    if grid:
        kwargs["grid"] = grid
    return pl.pallas_call(kernel, name=name, in_specs=in_specs, out_specs=out_specs, out_shape=out_shape,
                          scratch_shapes=scratch_shapes, compiler_params=_params(sem), **kwargs)(*operands)


def _ag_payload(slots):
    n = len(slots)

    def phases(_, refs, sems):
        send_i, recv_i, send_d, recv_d = sems
        x, y, c = _mesh_pos()
        me = 2 * x + y

        def half(w, core):
            rh = slots[w].shape[1] // 2
            return pl.ds(core * rh, rh)

        def ici(w, j):
            t = (me + 1 + j) % N_SHARD
            mine = refs[w].at[me, half(w, c), :]
            return _remote(mine, mine, send_i.at[3 * w + j], recv_i.at[3 * w + j], (t // 2, t % 2, c))

        def d2d(w, j, core):
            s = (me + 3 - j) % N_SHARD
            land = refs[w].at[s, half(w, core), :]
            return _remote(land, land, send_d.at[3 * w + j], recv_d.at[3 * w + j], (x, y, 1 - c))

        def start():
            for w in range(n):
                for j in range(3):
                    ici(w, j).start()

        def forward():
            for w in range(n):
                for j in range(3):
                    s = (me + 3 - j) % N_SHARD
                    land = refs[w].at[s, half(w, c), :]
                    _remote(land, land, send_i.at[3 * w + j], recv_i.at[3 * w + j], (x, y, c)).wait_recv()
                    d2d(w, j, c).start()

        def finish():
            for w in range(n):
                for j in range(3):
                    d2d(w, j, 1 - c).wait_recv()
            for w in range(n):
                for j in range(3):
                    ici(w, j).wait_send()
                    d2d(w, j, c).wait_send()

        return [start, forward, finish]

    return _Payload(slots, [jax.ShapeDtypeStruct(s.shape, s.dtype) for s in slots], {i: i for i in range(n)},
                    [pltpu.SemaphoreType.DMA((3 * n,)) for _ in range(4)], phases)


def _px_payload(grads, small=None):
    arrays = list(grads) + ([small] if small is not None else [])
    n = len(arrays)

    def phases(ins, outs, sems):
        send, recv = sems
        x, y, c = _mesh_pos()

        def copy(w):
            if w < len(grads):
                rh = grads[w].shape[1] // 2
                src = ins[w].at[:, pl.ds((1 - c) * rh, rh), :]
            else:
                src = ins[w]
            return _remote(src, outs[w], send.at[w], recv.at[w], (x, y, 1 - c))

        def start():
            for w in range(n):
                copy(w).start()

        def finish():
            for w in range(n):
                copy(w).wait()

        return [start, finish]

    out_shapes = [jax.ShapeDtypeStruct((N_SHARD, g.shape[1] // 2, g.shape[2]), F32) for g in grads]
    if small is not None:
        out_shapes.append(jax.ShapeDtypeStruct(small.shape, F32))
    return _Payload(arrays, out_shapes, {}, [pltpu.SemaphoreType.DMA((n,)), pltpu.SemaphoreType.DMA((n,))], phases)


def _cx_payload(pbs, lands):
    n = len(pbs)

    def phases(ins, outs, sems):
        send, recv = sems
        x, y, c = _mesh_pos()
        me = 2 * x + y

        def copy(w, j):
            t = (me + 1 + j) % N_SHARD
            return _remote(ins[w].at[t], outs[w].at[me], send.at[3 * w + j], recv.at[3 * w + j], (t // 2, t % 2, c))

        def start():
            for w in range(n):
                for j in range(3):
                    copy(w, j).start()

        def finish():
            for w in range(n):
                for j in range(3):
                    copy(w, j).wait()

        return [start, finish]

    return _Payload(list(pbs) + list(lands), [jax.ShapeDtypeStruct(p.shape, BF16) for p in lands],
                    {n + i: i for i in range(n)},
                    [pltpu.SemaphoreType.DMA((3 * n,)), pltpu.SemaphoreType.DMA((3 * n,))], phases)


def _pair_add(g, rv, core, chip, name):
    _, r, ncol = g.shape
    rh = r // 2

    def body(c_ref, me_ref, g_ref, rv_ref, pf_ref, pb_ref, land_ref):
        s = g_ref[0] + rv_ref[0]
        sb = s.astype(BF16)
        pb_ref[0] = sb

        @pl.when(pl.program_id(0) == me_ref[0])
        def _():
            pf_ref[...] = s
            land_ref[0] = sb

    slot = pl.BlockSpec((1, rh, ncol), lambda s, c, me: (s, 0, 0))
    grid_spec = pltpu.PrefetchScalarGridSpec(
        num_scalar_prefetch=2, grid=(N_SHARD,),
        in_specs=[pl.BlockSpec((1, rh, ncol), lambda s, c, me: (s, c[0], 0)), slot],
        out_specs=[pl.BlockSpec((rh, ncol), lambda s, c, me: (0, 0)), slot,
                   pl.BlockSpec((1, rh, ncol), lambda s, c, me: (me[0], 0, 0))])
    return pl.pallas_call(
        body, name=name, grid_spec=grid_spec,
        out_shape=[jax.ShapeDtypeStruct((rh, ncol), F32), jax.ShapeDtypeStruct((N_SHARD, rh, ncol), BF16),
                   jax.ShapeDtypeStruct((N_SHARD, rh, ncol), BF16)],
        compiler_params=_params(("arbitrary",)),
    )(core, chip, g, rv)


def _tail_reduce(grads, small):
    n = len(grads)
    _, r, ncol = grads[0].shape
    rh = r // 2

    def body(*refs):
        g_hbm, sm = refs[:n], refs[n]
        pf, land, sm_out = refs[n + 1:2 * n + 1], refs[2 * n + 1:3 * n + 1], refs[3 * n + 1]
        scr = refs[3 * n + 2:]
        rv, mine, sendb = scr[:n], scr[n:2 * n], scr[2 * n:3 * n]
        sm_rv, sm_sum, d_send, d_recv, load, i_send, i_recv, store = scr[3 * n:]
        x, y, c = _mesh_pos()
        me = 2 * x + y
        sib = (x, y, 1 - c)

        def pair(w):
            src = g_hbm[w].at[:, pl.ds((1 - c) * rh, rh), :] if w < n else sm
            return _remote(src, rv[w] if w < n else sm_rv, d_send.at[w], d_recv.at[w], sib)

        def chips(w, j):
            t = (me + 1 + j) % N_SHARD
            src = sendb[w].at[t] if w < n else sm_sum
            dst = land[w].at[me] if w < n else sm_out.at[me]
            return _remote(src, dst, i_send.at[3 * w + j], i_recv.at[3 * w + j], (t // 2, t % 2, c))

        loads = [pltpu.make_async_copy(g_hbm[w].at[:, pl.ds(c * rh, rh), :], mine[w], load.at[w]) for w in range(n)]
        for w in range(n + 1):
            pair(w).start()
        for cp in loads:
            cp.start()
        stores = []
        for w in range(n):
            loads[w].wait()
            pair(w).wait_recv()
            for k in range(N_SHARD):
                s = mine[w][k] + rv[w][k]
                mine[w][k] = s
                sendb[w][k] = s.astype(BF16)
            stores += [pltpu.make_async_copy(mine[w].at[me], pf[w], store.at[2 * w]),
                       pltpu.make_async_copy(sendb[w].at[me], land[w].at[me], store.at[2 * w + 1])]
            for cp in stores[-2:]:
                cp.start()
            for j in range(3):
                chips(w, j).start()
        pair(n).wait_recv()
        sm_sum[...] = sm[...] + sm_rv[...]
        stores.append(pltpu.make_async_copy(sm_sum, sm_out.at[me], store.at[2 * n]))
        stores[-1].start()
        for j in range(3):
            chips(n, j).start()
        for w in range(n + 1):
            pair(w).wait_send()
            for j in range(3):
                chips(w, j).wait()
        for cp in stores:
            cp.wait()

    half = (N_SHARD, rh, ncol)
    return _call(
        body, None, name="tail_reduce", grid=None,
        in_specs=[_ANY] * n + [_VMEM], out_specs=[_ANY] * (2 * n + 1),
        out_shape=([jax.ShapeDtypeStruct((rh, ncol), F32)] * n + [jax.ShapeDtypeStruct(half, BF16)] * n
                   + [jax.ShapeDtypeStruct((N_SHARD,) + small.shape, F32)]),
        scratch_shapes=([pltpu.VMEM(half, F32)] * (2 * n) + [pltpu.VMEM(half, BF16)] * n
                        + [pltpu.VMEM(small.shape, F32), pltpu.VMEM(small.shape, F32),
                           pltpu.SemaphoreType.DMA((n + 1,)), pltpu.SemaphoreType.DMA((n + 1,)),
                           pltpu.SemaphoreType.DMA((n,)), pltpu.SemaphoreType.DMA((3 * n + 3,)),
                           pltpu.SemaphoreType.DMA((3 * n + 3,)), pltpu.SemaphoreType.DMA((2 * n + 1,))]),
        operands=(*grads, small))


def _final_sum(pf, land, chip, core, name):
    _, rh, ncol = land.shape

    def body(me_ref, c_ref, pf_ref, land_ref, o_ref):
        me = me_ref[0]
        acc = jnp.zeros((rh, ncol), F32)
        for k in range(N_SHARD):
            acc = acc + jnp.where(me == k, pf_ref[...], land_ref[k].astype(F32))
        o_ref[...] = acc

    grid_spec = pltpu.PrefetchScalarGridSpec(
        num_scalar_prefetch=2, grid=(1,),
        in_specs=[pl.BlockSpec((rh, ncol), lambda i, me, c: (0, 0)),
                  pl.BlockSpec((N_SHARD, rh, ncol), lambda i, me, c: (0, 0, 0))],
        out_specs=pl.BlockSpec((rh, ncol), lambda i, me, c: (c[0], 0)))
    return pl.pallas_call(
        body, name=name, grid_spec=grid_spec, out_shape=jax.ShapeDtypeStruct((2 * rh, ncol), F32),
        compiler_params=_params(("arbitrary",)),
    )(chip, core, pf, land)


def _sibling_share(fulls, name):
    n = len(fulls)

    def body(*refs):
        outs = refs[n:2 * n]
        send, recv = refs[2 * n:]
        x, y, c = _mesh_pos()
        cps = []
        for w in range(n):
            rh = fulls[w].shape[0] // 2
            mine = outs[w].at[pl.ds(c * rh, rh), :]
            cp = pltpu.make_async_remote_copy(src_ref=mine, dst_ref=mine, send_sem=send.at[w], recv_sem=recv.at[w],
                                              device_id=(x, y, 1 - c), device_id_type=MESH)
            cp.start()
            cps.append(cp)
        for cp in cps:
            cp.wait()

    return pl.pallas_call(
        body, name=name,
        in_specs=[_ANY] * n, out_specs=[_ANY] * n,
        out_shape=[jax.ShapeDtypeStruct(f.shape, F32) for f in fulls],
        input_output_aliases={i: i for i in range(n)},
        scratch_shapes=[pltpu.SemaphoreType.DMA((n,)), pltpu.SemaphoreType.DMA((n,))],
    )(*fulls)


_ROW = {"rel_bias": 128, "sgu_b_s": 136, "norm_ffn1": 144, "norm_mix": 145, "norm_ffn2": 146, "norm_final": 147,
        "b_gate": 148, "sgu_ln_g": 150, "sgu_ln_b": 151}


def _pack_small(gs, loss):
    def body(ws, rel, bs, n1, nm, n2, nf, bg, lg, lb, loss_ref, o_ref):
        o_ref[...] = jnp.zeros_like(o_ref)
        o_ref[LOSS_ROW:LOSS_ROW + 1, 0:128] = loss_ref[...]
        for g in range(SGU_GROUPS):
            o_ref[0:SGU_BLOCK, g * SGU_BLOCK:(g + 1) * SGU_BLOCK] = ws[g]
        o_ref[128:136, 0:REL_PAD] = rel[...]
        o_ref[136:144, 0:SGU_BLOCK] = bs[...]
        o_ref[144:145, :] = n1[...]
        o_ref[145:146, :] = nm[...]
        o_ref[146:147, :] = n2[...]
        o_ref[147:148, :] = nf[...]
        o_ref[148:149, :] = bg[:, 0:D_MODEL]
        o_ref[149:150, :] = bg[:, D_MODEL:2 * D_MODEL]
        o_ref[150:151, 0:D_SGU] = lg[...]
        o_ref[151:152, 0:D_SGU] = lb[...]

    order = ("sgu_w_s", "rel_bias", "sgu_b_s", "norm_ffn1", "norm_mix", "norm_ffn2", "norm_final", "b_gate", "sgu_ln_g",
             "sgu_ln_b")
    return pl.pallas_call(body, name="pack_small", out_shape=jax.ShapeDtypeStruct((SMALL_ROWS, D_MODEL), F32))(
        *[gs[k] for k in order], loss)


def _adam(w, g, m, v):
    m2 = ADAM_B1 * m + (1.0 - ADAM_B1) * g
    v2 = ADAM_B2 * v + (1.0 - ADAM_B2) * (g * g)
    m_hat = m2 / (1.0 - ADAM_B1 ** ADAM_STEP)
    v_hat = v2 / (1.0 - ADAM_B2 ** ADAM_STEP)
    delta = -ADAM_LR * (m_hat / (jnp.sqrt(v_hat) + ADAM_EPS) + ADAM_WD * w)
    return delta, m2, v2


def _adam_small(sin, w, m, v):
    names = SMALL
    k = len(names)

    def body(*refs):
        sin_ref = refs[0]
        w_r, m_r, v_r = refs[1:1 + k], refs[1 + k:1 + 2 * k], refs[1 + 2 * k:1 + 3 * k]
        outs = refs[1 + 3 * k:]
        tot = sin_ref[0] + sin_ref[1] + sin_ref[2] + sin_ref[3]
        outs[4 * k][...] = tot[LOSS_ROW:LOSS_ROW + 1, 0:128]
        for i, name in enumerate(names):
            o = outs[4 * i:4 * i + 4]
            if name == "sgu_w_s":
                for gi in range(SGU_GROUPS):
                    g = tot[0:SGU_BLOCK, gi * SGU_BLOCK:(gi + 1) * SGU_BLOCK]
                    res = (g,) + _adam(w_r[i][gi], g, m_r[i][gi], v_r[i][gi])
                    for ref, val in zip(o, res):
                        ref[gi] = val
                continue
            r0 = _ROW[name]
            if name == "rel_bias":
                g = tot[r0:r0 + HEADS, 0:REL_PAD]
            elif name == "sgu_b_s":
                g = tot[r0:r0 + SGU_GROUPS, 0:SGU_BLOCK]
            elif name == "b_gate":
                g = jnp.concatenate([tot[r0:r0 + 1, :], tot[r0 + 1:r0 + 2, :]], axis=1)
            elif name in ("sgu_ln_g", "sgu_ln_b"):
                g = tot[r0:r0 + 1, 0:D_SGU]
            else:
                g = tot[r0:r0 + 1, :]
            res = (g,) + _adam(w_r[i][...], g, m_r[i][...], v_r[i][...])
            for ref, val in zip(o, res):
                ref[...] = val

    out_shape = []
    for name in names:
        out_shape += [jax.ShapeDtypeStruct(w[name].shape, F32)] * 4
    out_shape.append(jax.ShapeDtypeStruct((1, 128), F32))
    flat = pl.pallas_call(body, name="adam_small", out_shape=out_shape, compiler_params=_params())(
        sin, *[w[n] for n in names], *[m[n] for n in names], *[v[n] for n in names])
    return {name: tuple(flat[4 * i:4 * i + 4]) for i, name in enumerate(names)}, flat[4 * k]


def _adam_big(w, g, m, v, name):
    r, ncol = w.shape
    tr = 256 if r % 256 == 0 else r // 2

    def body(w_ref, g_ref, m_ref, v_ref, g2_ref, d_ref, m2_ref, v2_ref):
        gv = g_ref[...]
        g2_ref[...] = gv
        d_ref[...], m2_ref[...], v2_ref[...] = _adam(w_ref[...], gv, m_ref[...], v_ref[...])

    spec = pl.BlockSpec((tr, ncol), lambda i: (i, 0))
    return pl.pallas_call(
        body, name=name, grid=(r // tr,), in_specs=[spec] * 4, out_specs=[spec] * 4,
        out_shape=[jax.ShapeDtypeStruct(w.shape, F32)] * 4, compiler_params=_params(("arbitrary",)),
    )(w, g, m, v)


WEIGHTS = ("norm_ffn1", "ffn1_w_gate", "ffn1_w_up", "ffn1_w_down", "norm_mix", "w_in", "b_gate", "rel_bias", "sgu_ln_g",
           "sgu_ln_b", "sgu_w_s", "sgu_b_s", "w_branch_att", "w_branch_sgu", "w_out", "norm_ffn2", "ffn2_w_gate",
           "ffn2_w_up", "ffn2_w_down", "norm_final")


GATE_UP = ("ffn1_w_gate", "ffn1_w_up", "ffn2_w_gate", "ffn2_w_up")
_FFN = ("ffn1_w_gate", "ffn1_w_up", "ffn1_w_down", "ffn2_w_gate", "ffn2_w_up", "ffn2_w_down")
_CAST_GROUPS = ((_FFN, "cast_ffn"), (("w_in",), "cast_w_in"), (("w_branch_att", "w_branch_sgu"), "cast_branch"),
                (("w_out",), "cast_w_out"))


def _big_form(name, a):
    return jnp.swapaxes(a, 1, 2)[0] if name in GATE_UP else a[0]


def _big_back(name, a):
    return jnp.swapaxes(a[None], 1, 2) if name in GATE_UP else a[None]


def _small_form(name, a):
    if name == "norm_final":
        return a.reshape(1, D_MODEL)
    if name == "rel_bias":
        return jnp.pad(a[0], ((0, 0), (0, REL_PAD - N_REL)))
    if name in ("sgu_w_s", "sgu_b_s"):
        return a[0]
    return a


def _small_back(name, a, like):
    if name == "rel_bias":
        a = a[:, :N_REL]
    return a.reshape(like.shape)


def kernel(x, norm_ffn1, ffn1_w_gate, ffn1_w_up, ffn1_w_down, norm_mix, w_in, b_gate, rel_bias, sgu_ln_g, sgu_ln_b, sgu_w_s, sgu_b_s, w_branch_att, w_branch_sgu, w_out, norm_ffn2, ffn2_w_gate, ffn2_w_up, ffn2_w_down, norm_final, loss_target, m_norm_ffn1, m_ffn1_w_gate, m_ffn1_w_up, m_ffn1_w_down, m_norm_mix, m_w_in, m_b_gate, m_rel_bias, m_sgu_ln_g, m_sgu_ln_b, m_sgu_w_s, m_sgu_b_s, m_w_branch_att, m_w_branch_sgu, m_w_out, m_norm_ffn2, m_ffn2_w_gate, m_ffn2_w_up, m_ffn2_w_down, m_norm_final, v_norm_ffn1, v_ffn1_w_gate, v_ffn1_w_up, v_ffn1_w_down, v_norm_mix, v_w_in, v_b_gate, v_rel_bias, v_sgu_ln_g, v_sgu_ln_b, v_sgu_w_s, v_sgu_b_s, v_w_branch_att, v_w_branch_sgu, v_w_out, v_norm_ffn2, v_ffn2_w_gate, v_ffn2_w_up, v_ffn2_w_down, v_norm_final):
    w = dict(norm_ffn1=norm_ffn1, ffn1_w_gate=ffn1_w_gate, ffn1_w_up=ffn1_w_up, ffn1_w_down=ffn1_w_down, norm_mix=norm_mix,
             w_in=w_in, b_gate=b_gate, rel_bias=rel_bias, sgu_ln_g=sgu_ln_g, sgu_ln_b=sgu_ln_b, sgu_w_s=sgu_w_s,
             sgu_b_s=sgu_b_s, w_branch_att=w_branch_att, w_branch_sgu=w_branch_sgu, w_out=w_out, norm_ffn2=norm_ffn2,
             ffn2_w_gate=ffn2_w_gate, ffn2_w_up=ffn2_w_up, ffn2_w_down=ffn2_w_down, norm_final=norm_final)
    m = dict(norm_ffn1=m_norm_ffn1, ffn1_w_gate=m_ffn1_w_gate, ffn1_w_up=m_ffn1_w_up, ffn1_w_down=m_ffn1_w_down,
             norm_mix=m_norm_mix, w_in=m_w_in, b_gate=m_b_gate, rel_bias=m_rel_bias, sgu_ln_g=m_sgu_ln_g,
             sgu_ln_b=m_sgu_ln_b, sgu_w_s=m_sgu_w_s, sgu_b_s=m_sgu_b_s, w_branch_att=m_w_branch_att,
             w_branch_sgu=m_w_branch_sgu, w_out=m_w_out, norm_ffn2=m_norm_ffn2, ffn2_w_gate=m_ffn2_w_gate,
             ffn2_w_up=m_ffn2_w_up, ffn2_w_down=m_ffn2_w_down, norm_final=m_norm_final)
    v = dict(norm_ffn1=v_norm_ffn1, ffn1_w_gate=v_ffn1_w_gate, ffn1_w_up=v_ffn1_w_up, ffn1_w_down=v_ffn1_w_down,
             norm_mix=v_norm_mix, w_in=v_w_in, b_gate=v_b_gate, rel_bias=v_rel_bias, sgu_ln_g=v_sgu_ln_g,
             sgu_ln_b=v_sgu_ln_b, sgu_w_s=v_sgu_w_s, sgu_b_s=v_sgu_b_s, w_branch_att=v_w_branch_att,
             w_branch_sgu=v_w_branch_sgu, w_out=v_w_out, norm_ffn2=v_norm_ffn2, ffn2_w_gate=v_ffn2_w_gate,
             ffn2_w_up=v_ffn2_w_up, ffn2_w_down=v_ffn2_w_down, norm_final=v_norm_final)

    core = lax.axis_index("c").astype(jnp.int32).reshape(1)
    chip = (2 * lax.axis_index("x") + lax.axis_index("y")).astype(jnp.int32).reshape(1)

    wk = {n: _big_form(n, w[n]) for n in BIG}
    slots = {}
    for names, call in _CAST_GROUPS:
        slots.update(zip(names, _cast_slots([wk[n] for n in names], chip, call)))
    ws = {n: _small_form(n, w[n]) for n in SMALL}
    _, gx, shard_grads, small_sums = _local_step(x[0], loss_target[0], slots, ws, (core, chip))

    small, loss = _adam_small(small_sums, ws, {n: _small_form(n, m[n]) for n in SMALL},
                              {n: _small_form(n, v[n]) for n in SMALL})
    grad, delta, new_m, new_v = {}, {}, {}, {}
    for n in SMALL:
        grad[n], delta[n], new_m[n], new_v[n] = (_small_back(n, a, w[n]) for a in small[n])
    for n in BIG:
        g2, d2, m2, v2 = _adam_big(wk[n], shard_grads[n], _big_form(n, m[n]), _big_form(n, v[n]), "adam_" + n)
        grad[n], delta[n], new_m[n], new_v[n] = (_big_back(n, a) for a in (g2, d2, m2, v2))

    return (loss[0, 0], gx.reshape(x.shape), *[grad[n] for n in WEIGHTS], *[delta[n] for n in WEIGHTS],
            *[new_m[n] for n in WEIGHTS], *[new_v[n] for n in WEIGHTS])
```

```python
import functools

import jax
import jax.numpy as jnp
from jax import lax
from jax.experimental import pallas as pl
from jax.experimental.pallas import tpu as pltpu

F32 = jnp.float32
BF16 = jnp.bfloat16

D_MODEL = 1024
N_SHARD = 4
D_FF = 2816
FF_S = D_FF // N_SHARD
D_ATT = 512
D_SGU = 512
D_IN = 3 * D_ATT + 2 * D_SGU + 2 * D_MODEL
IN_S = D_IN // N_SHARD
BR_S = D_MODEL // N_SHARD
HEADS = 8
HEAD_DIM = 64
CHUNK = 64
N_LEFT = 8
BAND = (N_LEFT + 1) * CHUNK
REL_CLIP = 256
N_REL = 2 * REL_CLIP + 1
REL_PAD = 640
SGU_BLOCK = 128
SGU_GROUPS = 8
SGU_GDIM = 64
EPS = 1e-6
NEG_INF = -1e30

ATT_ROWS = 2 * CHUNK
ATT_KEYS = BAND + CHUNK
ATT_PAD = N_LEFT * CHUNK

ADAM_LR = 0.001
ADAM_B1 = 0.9
ADAM_B2 = 0.999
ADAM_EPS = 1e-08
ADAM_WD = 0.01
ADAM_STEP = 10

TM = 256
TW = 512
VMEM_LIMIT = 56 * 1024 * 1024

SMALL_ROWS = 160
LOSS_ROW = 152
MESH = pl.DeviceIdType.MESH

_NT = (((1,), (1,)), ((), ()))
_TN = (((0,), (0,)), ((), ()))


def _params(sem=None):
    return pltpu.CompilerParams(dimension_semantics=sem, vmem_limit_bytes=VMEM_LIMIT)


def _const_spec(shape):
    nd = len(shape)
    return pl.BlockSpec(shape, lambda *_: (0,) * nd, pipeline_mode=pl.Buffered(1))


def _acc_spec(shape):
    nd = len(shape)
    return pl.BlockSpec(shape, lambda *_: (0,) * nd)


def _row_spec(tm, ncols, off=0):
    return pl.BlockSpec((tm, ncols), lambda i: (i + off, 0))


def _row3_spec(tm, ncols):
    return pl.BlockSpec((N_SHARD, tm, ncols), lambda i: (0, i, 0))


def _dot(a, b):
    return jnp.dot(a, b, preferred_element_type=F32)


def _dot_nt(a, b):
    return lax.dot_general(a, b, _NT, preferred_element_type=F32)


def _dot_tn(a, b):
    return lax.dot_general(a, b, _TN, preferred_element_type=F32)


def _rms_fwd(x, g):
    r = lax.rsqrt(jnp.mean(x * x, axis=-1, keepdims=True) + EPS)
    xhat = x * r
    return xhat, r, xhat * g


def _rms_bwd(dh, xhat, r, g):
    dxhat = dh * g
    dx = r * (dxhat - xhat * jnp.mean(dxhat * xhat, axis=-1, keepdims=True))
    dg = jnp.sum(dh * xhat, axis=0, keepdims=True)
    return dx, dg


def _sigmoid(x):
    return 1.0 / (1.0 + jnp.exp(-x))


def _edges(n_steps):
    return [(0, True), (n_steps - 1, False)]


def _ffn_fwd(x, g, wg, wu, wd, name, payload=None):
    T = x.shape[0]

    def body(x_ref, g_ref, wg_ref, wu_ref, wd_ref, xo_ref, h_ref, a_ref, b_ref):
        xv = x_ref[...]
        hb = _rms_fwd(xv, g_ref[...])[2].astype(BF16)
        h_ref[...] = hb
        acc = jnp.zeros((TM, D_MODEL), F32)
        for s in range(N_SHARD):
            a = _dot_nt(hb, wg_ref[s])
            b = _dot_nt(hb, wu_ref[s])
            a_ref[s] = a.astype(BF16)
            b_ref[s] = b.astype(BF16)
            sv = a * _sigmoid(a) * b
            acc += _dot(sv.astype(BF16), wd_ref[s])
        xo_ref[...] = xv + 0.5 * acc

    return _call(
        body, payload, name=name, grid=(T // TM,), when=_edges(T // TM), sem=("arbitrary",),
        in_specs=[_row_spec(TM, D_MODEL), _const_spec((1, D_MODEL)), _const_spec(wg.shape), _const_spec(wu.shape),
                  _const_spec(wd.shape)],
        out_specs=[_row_spec(TM, D_MODEL), _row_spec(TM, D_MODEL), _row3_spec(TM, FF_S), _row3_spec(TM, FF_S)],
        out_shape=[jax.ShapeDtypeStruct((T, D_MODEL), F32), jax.ShapeDtypeStruct((T, D_MODEL), BF16),
                   jax.ShapeDtypeStruct((N_SHARD, T, FF_S), BF16), jax.ShapeDtypeStruct((N_SHARD, T, FF_S), BF16)],
        operands=(x, g, wg, wu, wd))


def _ffn_dgrad(dout, x, a, b, g, wg, wu, wd, name, payload=None):
    T = x.shape[0]

    def body(do_ref, x_ref, a_ref, b_ref, g_ref, wg_ref, wu_ref, wd_ref, dx_ref, da_ref, db_ref, dg_ref):
        do = do_ref[...]
        dob = do.astype(BF16)
        dh = jnp.zeros((TM, D_MODEL), F32)
        for s in range(N_SHARD):
            ds = 0.5 * _dot_nt(dob, wd_ref[s])
            av = a_ref[s].astype(F32)
            bv = b_ref[s].astype(F32)
            sig = _sigmoid(av)
            da = (ds * bv * (sig * (1.0 + av * (1.0 - sig)))).astype(BF16)
            db = (ds * (av * sig)).astype(BF16)
            da_ref[s] = da
            db_ref[s] = db
            dh += _dot(da, wg_ref[s]) + _dot(db, wu_ref[s])
        gv = g_ref[...]
        xhat, r, _ = _rms_fwd(x_ref[...], gv)
        dxn, dg = _rms_bwd(dh, xhat, r, gv)
        dx_ref[...] = do + dxn

        @pl.when(pl.program_id(0) == 0)
        def _():
            dg_ref[...] = jnp.zeros_like(dg_ref)

        dg_ref[...] += dg

    return _call(
        body, payload, name=name, grid=(T // TM,), when=_edges(T // TM), sem=("arbitrary",),
        in_specs=[_row_spec(TM, D_MODEL), _row_spec(TM, D_MODEL), _row3_spec(TM, FF_S), _row3_spec(TM, FF_S),
                  _const_spec((1, D_MODEL)), _const_spec(wg.shape), _const_spec(wu.shape), _const_spec(wd.shape)],
        out_specs=[_row_spec(TM, D_MODEL), _row3_spec(TM, FF_S), _row3_spec(TM, FF_S), _acc_spec((1, D_MODEL))],
        out_shape=[jax.ShapeDtypeStruct((T, D_MODEL), F32), jax.ShapeDtypeStruct((N_SHARD, T, FF_S), BF16),
                   jax.ShapeDtypeStruct((N_SHARD, T, FF_S), BF16), jax.ShapeDtypeStruct((1, D_MODEL), F32)],
        operands=(dout, x, a, b, g, wg, wu, wd))


def _ffn_wgrad(h, dout, a, b, da, db, name, payload=None):
    T = h.shape[0]

    def body(h_ref, do_ref, a_ref, b_ref, da_ref, db_ref, gwg_ref, gwu_ref, gwd_ref):
        @pl.when(pl.program_id(1) == 0)
        def _():
            gwg_ref[...] = jnp.zeros_like(gwg_ref)
            gwu_ref[...] = jnp.zeros_like(gwu_ref)
            gwd_ref[...] = jnp.zeros_like(gwd_ref)

        hv = h_ref[...]
        dob = do_ref[...].astype(BF16)
        av = a_ref[0].astype(F32)
        sv = (0.5 * av * _sigmoid(av) * b_ref[0].astype(F32)).astype(BF16)
        gwg_ref[0] += _dot_tn(da_ref[0], hv)
        gwu_ref[0] += _dot_tn(db_ref[0], hv)
        gwd_ref[0] += _dot_tn(sv, dob)

    tok = pl.BlockSpec((TW, D_MODEL), lambda s, i: (i, 0))
    act = pl.BlockSpec((1, TW, FF_S), lambda s, i: (s, i, 0))
    return _call(
        body, payload, name=name, grid=(N_SHARD, T // TW), when=_edges(N_SHARD * (T // TW)),
        sem=("arbitrary", "arbitrary"),
        in_specs=[tok, tok, act, act, act, act],
        out_specs=[pl.BlockSpec((1, FF_S, D_MODEL), lambda s, i: (s, 0, 0))] * 3,
        out_shape=[jax.ShapeDtypeStruct((N_SHARD, FF_S, D_MODEL), F32)] * 3,
        operands=(h, dout, a, b, da, db))


def _in_fwd(x, g, w_in, payload=None):
    T = x.shape[0]

    def body(x_ref, g_ref, w_ref, h_ref, qkv_ref, zs_ref, gl_ref):
        hb = _rms_fwd(x_ref[...], g_ref[...])[2].astype(BF16)
        h_ref[...] = hb
        z0 = _dot(hb, w_ref[0])
        qkv_ref[:, 0:IN_S] = z0.astype(BF16)
        z1 = _dot(hb, w_ref[1])
        qkv_ref[:, IN_S:3 * D_ATT] = z1[:, 0:384].astype(BF16)
        zs_ref[:, 0:768] = z1[:, 384:IN_S]
        z2 = _dot(hb, w_ref[2])
        zs_ref[:, 768:1024] = z2[:, 0:256]
        gl_ref[:, 0:896] = z2[:, 256:IN_S]
        gl_ref[:, 896:2048] = _dot(hb, w_ref[3])

    return _call(
        body, payload, name="in_fwd", grid=(T // TM,), when=_edges(T // TM), sem=("arbitrary",),
        in_specs=[_row_spec(TM, D_MODEL), _const_spec((1, D_MODEL)), _const_spec(w_in.shape)],
        out_specs=[_row_spec(TM, D_MODEL), _row_spec(TM, 3 * D_ATT), _row_spec(TM, 2 * D_SGU), _row_spec(TM, 2 * D_MODEL)],
        out_shape=[jax.ShapeDtypeStruct((T, D_MODEL), BF16), jax.ShapeDtypeStruct((T, 3 * D_ATT), BF16),
                   jax.ShapeDtypeStruct((T, 2 * D_SGU), F32), jax.ShapeDtypeStruct((T, 2 * D_MODEL), F32)],
        operands=(x, g, w_in))


def _in_dgrad(dx_res, x, g, w_in, dq, dk, dv, dzs, dgl):
    T = x.shape[0]

    def body(dxr_ref, x_ref, g_ref, w_ref, dq_ref, dk_ref, dv_ref, dzs_ref, dgl_ref, dx_ref, dz_ref, dg_ref):
        dz = jnp.concatenate([dq_ref[...], dk_ref[...].astype(BF16), dv_ref[...].astype(BF16), dzs_ref[...], dgl_ref[...]],
                             axis=1)
        dz_ref[...] = dz
        dh = jnp.zeros((TM, D_MODEL), F32)
        for s in range(N_SHARD):
            dh += _dot_nt(dz[:, s * IN_S:(s + 1) * IN_S], w_ref[s])
        gv = g_ref[...]
        xhat, r, _ = _rms_fwd(x_ref[...], gv)
        dxn, dg = _rms_bwd(dh, xhat, r, gv)
        dx_ref[...] = dxr_ref[...] + dxn

        @pl.when(pl.program_id(0) == 0)
        def _():
            dg_ref[...] = jnp.zeros_like(dg_ref)

        dg_ref[...] += dg

    pad_blocks = ATT_PAD // TM
    return pl.pallas_call(
        body, name="in_dgrad", grid=(T // TM,),
        in_specs=[_row_spec(TM, D_MODEL), _row_spec(TM, D_MODEL), _const_spec((1, D_MODEL)), _const_spec(w_in.shape),
                  _row_spec(TM, D_ATT), _row_spec(TM, D_ATT, pad_blocks), _row_spec(TM, D_ATT, pad_blocks),
                  _row_spec(TM, 2 * D_SGU), _row_spec(TM, 2 * D_MODEL)],
        out_specs=[_row_spec(TM, D_MODEL), _row_spec(TM, D_IN), _acc_spec((1, D_MODEL))],
        out_shape=[jax.ShapeDtypeStruct((T, D_MODEL), F32), jax.ShapeDtypeStruct((T, D_IN), BF16),
                   jax.ShapeDtypeStruct((1, D_MODEL), F32)],
        compiler_params=_params(("arbitrary",)),
    )(dx_res, x, g, w_in, dq, dk, dv, dzs, dgl)


def _in_wgrad(h, dz):
    T = h.shape[0]

    def body(h_ref, dz_ref, gw_ref):
        @pl.when(pl.program_id(1) == 0)
        def _():
            gw_ref[...] = jnp.zeros_like(gw_ref)

        gw_ref[0] += _dot_tn(h_ref[...], dz_ref[...])

    return pl.pallas_call(
        body, name="in_wgrad", grid=(N_SHARD, T // TW),
        in_specs=[pl.BlockSpec((TW, D_MODEL), lambda s, i: (i, 0)), pl.BlockSpec((TW, IN_S), lambda s, i: (i, s))],
        out_specs=pl.BlockSpec((1, D_MODEL, IN_S), lambda s, i: (s, 0, 0)),
        out_shape=jax.ShapeDtypeStruct((N_SHARD, D_MODEL, IN_S), F32),
        compiler_params=_params(("arbitrary", "arbitrary")),
    )(h, dz)


def _rel_onehot():
    r = lax.broadcasted_iota(jnp.int32, (REL_PAD, REL_PAD), 0)
    n = lax.broadcasted_iota(jnp.int32, (REL_PAD, REL_PAD), 1)
    idx = jnp.clip(BAND - 1 - n, -REL_CLIP, REL_CLIP) + REL_CLIP
    return jnp.where(r == idx, 1.0, 0.0).astype(BF16)


def _split3(v):
    p1 = v.astype(BF16)
    r1 = v - p1.astype(F32)
    p2 = r1.astype(BF16)
    p3 = (r1 - p2.astype(F32)).astype(BF16)
    return p1, p2, p3


def _relbias_fwd(tab_pad):
    def body(t_ref, o_ref):
        oh = _rel_onehot()
        acc = jnp.zeros((HEADS, REL_PAD), F32)
        for p in _split3(t_ref[...]):
            acc += _dot(p, oh)
        o_ref[...] = acc

    return pl.pallas_call(body, name="relbias_fwd", out_shape=jax.ShapeDtypeStruct((HEADS, REL_PAD), F32))(tab_pad)


def _relbias_bwd(z):
    def body(z_ref, o_ref):
        oh = _rel_onehot()
        dt2 = jnp.sum(z_ref[...], axis=1)
        acc = jnp.zeros((HEADS, REL_PAD), F32)
        for p in _split3(dt2):
            acc += _dot_nt(p, oh)
        o_ref[...] = acc

    return pl.pallas_call(body, name="relbias_bwd", out_shape=jax.ShapeDtypeStruct((HEADS, REL_PAD), F32))(z)


def _bias_blocks(t2):
    flat = jnp.tile(t2, (1, CHUNK))
    skew = flat[:, :CHUNK * (REL_PAD - 1)].reshape(HEADS, CHUNK, REL_PAD - 1)
    bias = skew[:, :, CHUNK - 1:CHUNK - 1 + BAND]
    slabs = [jnp.pad(bias, ((0, 0), (0, 0), (CHUNK * c, ATT_KEYS - BAND - CHUNK * c)), constant_values=NEG_INF)
             for c in range(2)]
    return jnp.concatenate(slabs, axis=1)


def _unskew(db2):
    out = []
    for c in range(2):
        slab = db2[:, CHUNK * c:CHUNK * (c + 1), CHUNK * c:CHUNK * c + BAND]
        y = jnp.pad(slab, ((0, 0), (0, 0), (CHUNK - 1, REL_PAD - BAND - CHUNK + 1)))
        yf = jnp.pad(y.reshape(HEADS, CHUNK * REL_PAD), ((0, 0), (0, CHUNK)))
        out.append(yf.reshape(HEADS, CHUNK, REL_PAD + 1)[:, :, :REL_PAD])
    return jnp.concatenate(out, axis=1)


def _att_load(qkv_hbm, q_s, k_s, v_s, sem, T):
    copies = [pltpu.make_async_copy(qkv_hbm.at[:, 0:D_ATT], q_s, sem.at[0]),
              pltpu.make_async_copy(qkv_hbm.at[:, D_ATT:2 * D_ATT], k_s.at[pl.ds(ATT_PAD, T), :], sem.at[1]),
              pltpu.make_async_copy(qkv_hbm.at[:, 2 * D_ATT:3 * D_ATT], v_s.at[pl.ds(ATT_PAD, T), :], sem.at[2])]
    for cp in copies:
        cp.start()
    k_s[0:ATT_PAD, :] = jnp.zeros((ATT_PAD, D_ATT), BF16)
    v_s[0:ATT_PAD, :] = jnp.zeros((ATT_PAD, D_ATT), BF16)
    for cp in copies:
        cp.wait()


def _head(v, h):
    return v[:, h * HEAD_DIM:(h + 1) * HEAD_DIM]


def _rows(v, h):
    return v[h * ATT_ROWS:(h + 1) * ATT_ROWS]


def _att_exp(qs, kw, bias_ref, valid):
    s = jnp.concatenate([_dot_nt(_head(qs, h), _head(kw, h)) + bias_ref[h] for h in range(HEADS)], axis=0)
    if valid is not None:
        s = jnp.where(valid, s, NEG_INF)
    e = jnp.exp(s - jnp.max(s, axis=-1, keepdims=True))
    return e, 1.0 / jnp.sum(e, axis=-1, keepdims=True)


def _att_blocks(T, block, keys_on_rows=False, middle=None):
    n_edge = min(ATT_PAD // ATT_ROWS, T // ATT_ROWS)
    shape, axis = ((ATT_KEYS, 1), 0) if keys_on_rows else ((1, ATT_KEYS), 1)

    def edge(i, carry):
        r0 = i * ATT_ROWS
        block(i, (lax.broadcasted_iota(jnp.int32, shape, axis) + (r0 - ATT_PAD)) >= 0)
        return carry

    def inner(i, carry):
        block(i, None)
        return carry

    n_blocks = T // ATT_ROWS
    lax.fori_loop(0, n_edge, edge, 0)
    if middle is None:
        lax.fori_loop(n_edge, n_blocks, inner, 0)
        return
    n_late = max(n_blocks - n_blocks // 4, n_edge)
    lax.fori_loop(n_edge, n_late, inner, 0)
    middle()
    lax.fori_loop(n_late, n_blocks, inner, 0)


def _att_fwd(qkv, bias2, payload=None):
    T = qkv.shape[0]

    def body(qkv_hbm, bias_ref, y_ref, q_s, k_s, v_s, sem, middle=None):
        _att_load(qkv_hbm, q_s, k_s, v_s, sem, T)

        def block(i, valid):
            r0 = pl.multiple_of(i * ATT_ROWS, ATT_ROWS)
            qs = q_s[pl.ds(r0, ATT_ROWS), :] * (HEAD_DIM ** -0.5)
            kw = k_s[pl.ds(r0, ATT_KEYS), :]
            vw = v_s[pl.ds(r0, ATT_KEYS), :]
            e, rinv = _att_exp(qs, kw, bias_ref, valid)
            eb = e.astype(BF16)
            outs = [_dot(_rows(eb, h), _head(vw, h)) * _rows(rinv, h) for h in range(HEADS)]
            y_ref[pl.ds(r0, ATT_ROWS), :] = jnp.concatenate(outs, axis=1).astype(BF16)

        _att_blocks(T, block, middle=middle)

    return _call(
        body, payload, name="att_fwd", grid=None, takes_middle=True,
        in_specs=[pl.BlockSpec(memory_space=pl.ANY), pl.BlockSpec(memory_space=pltpu.VMEM)],
        out_specs=[pl.BlockSpec(memory_space=pltpu.VMEM)],
        out_shape=[jax.ShapeDtypeStruct((T, D_ATT), BF16)],
        scratch_shapes=[pltpu.VMEM((T, D_ATT), BF16), pltpu.VMEM((T + ATT_PAD, D_ATT), BF16),
                        pltpu.VMEM((T + ATT_PAD, D_ATT), BF16), pltpu.SemaphoreType.DMA((3,))],
        operands=(qkv, bias2))


def _lanes(v, h):
    return v[:, h * ATT_ROWS:(h + 1) * ATT_ROWS]


def _att_bwd(qkv, dy, bias2t, payload=None):
    T = qkv.shape[0]

    def body(qkv_hbm, dy_ref, bias_ref, dq_ref, dk_ref, dv_ref, db_ref, q_s, k_s, v_s, sem):
        _att_load(qkv_hbm, q_s, k_s, v_s, sem, T)
        dk_ref[...] = jnp.zeros_like(dk_ref)
        dv_ref[...] = jnp.zeros_like(dv_ref)
        db_ref[...] = jnp.zeros_like(db_ref)

        def block(i, valid):
            r0 = pl.multiple_of(i * ATT_ROWS, ATT_ROWS)
            qs = q_s[pl.ds(r0, ATT_ROWS), :] * (HEAD_DIM ** -0.5)
            kw = k_s[pl.ds(r0, ATT_KEYS), :]
            vw = v_s[pl.ds(r0, ATT_KEYS), :]
            dyb = dy_ref[pl.ds(r0, ATT_ROWS), :]
            s = jnp.concatenate([_dot_nt(_head(kw, h), _head(qs, h)) + bias_ref[h] for h in range(HEADS)], axis=1)
            if valid is not None:
                s = jnp.where(valid, s, NEG_INF)
            e = jnp.exp(s - jnp.max(s, axis=0, keepdims=True))
            p = e * (1.0 / jnp.sum(e, axis=0, keepdims=True))
            dp = jnp.concatenate([_dot_nt(_head(vw, h), _head(dyb, h)) for h in range(HEADS)], axis=1)
            ds = p * (dp - jnp.sum(p * dp, axis=0, keepdims=True))
            for h in range(HEADS):
                db_ref[h] += _lanes(ds, h)
            dsb = ds.astype(BF16)
            pb = p.astype(BF16)
            dq = [_dot_tn(_lanes(dsb, h), _head(kw, h)) for h in range(HEADS)]
            dk = [_dot(_lanes(dsb, h), _head(qs, h)) for h in range(HEADS)]
            dv = [_dot(_lanes(pb, h), _head(dyb, h)) for h in range(HEADS)]
            dq_ref[pl.ds(r0, ATT_ROWS), :] = (jnp.concatenate(dq, axis=1) * (HEAD_DIM ** -0.5)).astype(BF16)
            dk_ref[pl.ds(r0, ATT_KEYS), :] += jnp.concatenate(dk, axis=1)
            dv_ref[pl.ds(r0, ATT_KEYS), :] += jnp.concatenate(dv, axis=1)

        _att_blocks(T, block, keys_on_rows=True)

    vmem = pl.BlockSpec(memory_space=pltpu.VMEM)
    return _call(
        body, payload, name="att_bwd", grid=None,
        in_specs=[pl.BlockSpec(memory_space=pl.ANY), vmem, vmem],
        out_specs=[vmem, vmem, vmem, vmem],
        out_shape=[jax.ShapeDtypeStruct((T, D_ATT), BF16), jax.ShapeDtypeStruct((T + ATT_PAD, D_ATT), F32),
                   jax.ShapeDtypeStruct((T + ATT_PAD, D_ATT), F32), jax.ShapeDtypeStruct((HEADS, ATT_KEYS, ATT_ROWS), F32)],
        scratch_shapes=[pltpu.VMEM((T, D_ATT), BF16), pltpu.VMEM((T + ATT_PAD, D_ATT), BF16),
                        pltpu.VMEM((T + ATT_PAD, D_ATT), BF16), pltpu.SemaphoreType.DMA((3,))],
        operands=(qkv, dy, bias2t))


_GELU_C = 0.7978845608028654
_GELU_A = 0.044715


def _gelu(x):
    t = jnp.tanh(_GELU_C * (x + _GELU_A * x * x * x))
    return 0.5 * x * (1.0 + t), t


def _gelu_grad(x, t):
    return 0.5 * (1.0 + t) + 0.5 * x * (1.0 - t * t) * _GELU_C * (1.0 + 3.0 * _GELU_A * x * x)


def _group_masks():
    col = lax.broadcasted_iota(jnp.int32, (SGU_GROUPS, D_SGU), 1) // SGU_GDIM
    grp = lax.broadcasted_iota(jnp.int32, (SGU_GROUPS, D_SGU), 0)
    return jnp.where(col == grp, 1.0, 0.0).astype(F32)


def _causal_mask(transposed=False):
    i = lax.broadcasted_iota(jnp.int32, (SGU_BLOCK, SGU_BLOCK), 0) // CHUNK
    j = lax.broadcasted_iota(jnp.int32, (SGU_BLOCK, SGU_BLOCK), 1) // CHUNK
    return (j >= i) if transposed else (i >= j)


def _sgu_norm(zs, lng, lnb):
    gz, t = _gelu(zs)
    u = gz[:, 0:D_SGU]
    vs = gz[:, D_SGU:2 * D_SGU]
    xc = vs - jnp.mean(vs, axis=-1, keepdims=True)
    rstd = lax.rsqrt(jnp.mean(xc * xc, axis=-1, keepdims=True) + EPS)
    xhat = xc * rstd
    return t, u, xhat, rstd, xhat * lng + lnb


def _sgu_mix(vn_blk, w_ref, bst, gm):
    mask = _causal_mask()
    s = jnp.zeros((SGU_BLOCK, D_SGU), F32)
    for g in range(SGU_GROUPS):
        wm = jnp.where(mask, w_ref[g], 0.0).astype(BF16)
        s += _dot(wm, (vn_blk * gm[g:g + 1, :]).astype(BF16))
        s += bst[:, g:g + 1] * gm[g:g + 1, :]
    return s


def _sgu_fwd(zs, lng, lnb, w_s, bst):
    T = zs.shape[0]
    nblk = TM // SGU_BLOCK

    def body(zs_ref, lng_ref, lnb_ref, w_ref, bst_ref, y_ref):
        _, u, _, _, vn = _sgu_norm(zs_ref[...], lng_ref[...], lnb_ref[...])
        gm = _group_masks()
        bst_v = bst_ref[...]
        for n in range(nblk):
            rows = slice(n * SGU_BLOCK, (n + 1) * SGU_BLOCK)
            s = _sgu_mix(vn[rows], w_ref, bst_v, gm)
            y_ref[rows, :] = (u[rows] * s).astype(BF16)

    return pl.pallas_call(
        body, name="sgu_fwd", grid=(T // TM,),
        in_specs=[_row_spec(TM, 2 * D_SGU), _const_spec((1, D_SGU)), _const_spec((1, D_SGU)),
                  _const_spec(w_s.shape), _const_spec(bst.shape)],
        out_specs=_row_spec(TM, D_SGU),
        out_shape=jax.ShapeDtypeStruct((T, D_SGU), BF16),
        compiler_params=_params(("arbitrary",)),
    )(zs, lng, lnb, w_s, bst)


def _sgu_bwd(zs, dy, lng, lnb, w_s, w_st, bst):
    T = zs.shape[0]
    nblk = TM // SGU_BLOCK

    def body(zs_ref, dy_ref, lng_ref, lnb_ref, w_ref, wt_ref, bst_ref, dzs_ref, dw_ref, dbt_ref, dlg_ref, dlb_ref):
        @pl.when(pl.program_id(0) == 0)
        def _():
            dw_ref[...] = jnp.zeros_like(dw_ref)
            dbt_ref[...] = jnp.zeros_like(dbt_ref)
            dlg_ref[...] = jnp.zeros_like(dlg_ref)
            dlb_ref[...] = jnp.zeros_like(dlb_ref)

        zs_v = zs_ref[...]
        lng_v = lng_ref[...]
        t, u, xhat, rstd, vn = _sgu_norm(zs_v, lng_v, lnb_ref[...])
        gm = _group_masks()
        bst_v = bst_ref[...]
        mask = _causal_mask()
        mask_t = _causal_mask(transposed=True)
        dyv = dy_ref[...].astype(F32)
        lane8 = lax.broadcasted_iota(jnp.int32, (1, SGU_GROUPS), 1)
        du_rows, dvn_rows = [], []
        for n in range(nblk):
            rows = slice(n * SGU_BLOCK, (n + 1) * SGU_BLOCK)
            vn_b = vn[rows]
            s = _sgu_mix(vn_b, w_ref, bst_v, gm)
            du_rows.append(dyv[rows] * s)
            dsb = dyv[rows] * u[rows]
            vnb16 = vn_b.astype(BF16)
            dvn = jnp.zeros((SGU_BLOCK, D_SGU), F32)
            dbt = jnp.zeros((SGU_BLOCK, SGU_GROUPS), F32)
            for g in range(SGU_GROUPS):
                dsg = dsb * gm[g:g + 1, :]
                dsg16 = dsg.astype(BF16)
                wmt = jnp.where(mask_t, wt_ref[g], 0.0).astype(BF16)
                dvn += _dot(wmt, dsg16)
                dw_ref[g] += jnp.where(mask, _dot_nt(dsg16, vnb16), 0.0)
                dbt += jnp.sum(dsg, axis=-1, keepdims=True) * jnp.where(lane8 == g, 1.0, 0.0)
            dbt_ref[...] += dbt
            dvn_rows.append(dvn)
        du = jnp.concatenate(du_rows, axis=0)
        dvn = jnp.concatenate(dvn_rows, axis=0)
        dlg_ref[...] += jnp.sum(dvn * xhat, axis=0, keepdims=True)
        dlb_ref[...] += jnp.sum(dvn, axis=0, keepdims=True)
        dxhat = dvn * lng_v
        dvs = rstd * (dxhat - jnp.mean(dxhat, axis=-1, keepdims=True)
                      - xhat * jnp.mean(dxhat * xhat, axis=-1, keepdims=True))
        dgz = jnp.concatenate([du, dvs], axis=1)
        dzs_ref[...] = (dgz * _gelu_grad(zs_v, t)).astype(BF16)

    return pl.pallas_call(
        body, name="sgu_bwd", grid=(T // TM,),
        in_specs=[_row_spec(TM, 2 * D_SGU), _row_spec(TM, D_SGU), _const_spec((1, D_SGU)), _const_spec((1, D_SGU)),
                  _const_spec(w_s.shape), _const_spec(w_st.shape), _const_spec(bst.shape)],
        out_specs=[_row_spec(TM, 2 * D_SGU), _acc_spec(w_s.shape), _acc_spec(bst.shape), _acc_spec((1, D_SGU)),
                   _acc_spec((1, D_SGU))],
        out_shape=[jax.ShapeDtypeStruct((T, 2 * D_SGU), BF16), jax.ShapeDtypeStruct(w_s.shape, F32),
                   jax.ShapeDtypeStruct(bst.shape, F32), jax.ShapeDtypeStruct((1, D_SGU), F32),
                   jax.ShapeDtypeStruct((1, D_SGU), F32)],
        compiler_params=_params(("arbitrary",)),
    )(zs, dy, lng, lnb, w_s, w_st, bst)


def _cols(v, s):
    return v[:, s * BR_S:(s + 1) * BR_S]


def _merge_fwd(x, y_att, y_sgu, gl, b_gate, wba, wbs, wo, payload=None):
    T = x.shape[0]

    def body(x_ref, ya_ref, ys_ref, gl_ref, bg_ref, wba_ref, wbs_ref, wo_ref, xo_ref, m_ref, pa_ref, ps_ref):
        ya = ya_ref[...]
        ys = ys_ref[...]
        pa = jnp.concatenate([_dot(ya, wba_ref[s]) for s in range(N_SHARD)], axis=1)
        ps = jnp.concatenate([_dot(ys, wbs_ref[s]) for s in range(N_SHARD)], axis=1)
        g = _sigmoid(gl_ref[...] + bg_ref[...])
        mb = (g[:, 0:D_MODEL] * pa + g[:, D_MODEL:2 * D_MODEL] * ps).astype(BF16)
        m_ref[...] = mb
        pa_ref[...] = pa.astype(BF16)
        ps_ref[...] = ps.astype(BF16)
        acc = jnp.zeros((TM, D_MODEL), F32)
        for s in range(N_SHARD):
            acc += _dot(_cols(mb, s), wo_ref[s])
        xo_ref[...] = x_ref[...] + acc

    tokd = jax.ShapeDtypeStruct((T, D_MODEL), BF16)
    return _call(
        body, payload, name="merge_fwd", grid=(T // TM,), when=_edges(T // TM), sem=("arbitrary",),
        in_specs=[_row_spec(TM, D_MODEL), _row_spec(TM, D_ATT), _row_spec(TM, D_SGU), _row_spec(TM, 2 * D_MODEL),
                  _const_spec((1, 2 * D_MODEL)), _const_spec(wba.shape), _const_spec(wbs.shape), _const_spec(wo.shape)],
        out_specs=[_row_spec(TM, D_MODEL)] * 4,
        out_shape=[jax.ShapeDtypeStruct((T, D_MODEL), F32), tokd, tokd, tokd],
        operands=(x, y_att, y_sgu, gl, b_gate, wba, wbs, wo))


def _merge_bwd(dx, y_att, y_sgu, gl, merged, pa, ps, b_gate, wba, wbs, wo, payload=None):
    T = dx.shape[0]

    def body(dx_ref, ya_ref, ys_ref, gl_ref, m_ref, pa_ref, ps_ref, bg_ref, wba_ref, wbs_ref, wo_ref,
             dya_ref, dys_ref, dgl_ref, dbg_ref, gwba_ref, gwbs_ref, gwo_ref):
        @pl.when(pl.program_id(0) == 0)
        def _():
            dbg_ref[...] = jnp.zeros_like(dbg_ref)
            gwba_ref[...] = jnp.zeros_like(gwba_ref)
            gwbs_ref[...] = jnp.zeros_like(gwbs_ref)
            gwo_ref[...] = jnp.zeros_like(gwo_ref)

        dxb = dx_ref[...].astype(BF16)
        dm = jnp.concatenate([_dot_nt(dxb, wo_ref[s]) for s in range(N_SHARD)], axis=1)
        g = _sigmoid(gl_ref[...] + bg_ref[...])
        ga = g[:, 0:D_MODEL]
        gs = g[:, D_MODEL:2 * D_MODEL]
        dpa = (dm * ga).astype(BF16)
        dps = (dm * gs).astype(BF16)
        dgl = jnp.concatenate([dm * pa_ref[...].astype(F32) * ga * (1.0 - ga),
                               dm * ps_ref[...].astype(F32) * gs * (1.0 - gs)], axis=1)
        dgl_ref[...] = dgl.astype(BF16)
        dbg_ref[...] += jnp.sum(dgl, axis=0, keepdims=True)
        ya = ya_ref[...]
        ys = ys_ref[...]
        mb = m_ref[...]
        dya = jnp.zeros((TM, D_ATT), F32)
        dys = jnp.zeros((TM, D_SGU), F32)
        for s in range(N_SHARD):
            dya += _dot_nt(_cols(dpa, s), wba_ref[s])
            dys += _dot_nt(_cols(dps, s), wbs_ref[s])
            gwo_ref[s] += _dot_tn(_cols(mb, s), dxb)
            gwba_ref[s] += _dot_tn(ya, _cols(dpa, s))
            gwbs_ref[s] += _dot_tn(ys, _cols(dps, s))
        dya_ref[...] = dya.astype(BF16)
        dys_ref[...] = dys.astype(BF16)

    return _call(
        body, payload, name="merge_bwd", grid=(T // TM,), when=_edges(T // TM), sem=("arbitrary",),
        operands=(dx, y_att, y_sgu, gl, merged, pa, ps, b_gate, wba, wbs, wo),
        in_specs=[_row_spec(TM, D_MODEL), _row_spec(TM, D_ATT), _row_spec(TM, D_SGU), _row_spec(TM, 2 * D_MODEL),
                  _row_spec(TM, D_MODEL), _row_spec(TM, D_MODEL), _row_spec(TM, D_MODEL),
                  _const_spec((1, 2 * D_MODEL)), _const_spec(wba.shape), _const_spec(wbs.shape), _const_spec(wo.shape)],
        out_specs=[_row_spec(TM, D_ATT), _row_spec(TM, D_SGU), _row_spec(TM, 2 * D_MODEL), _acc_spec((1, 2 * D_MODEL)),
                   _acc_spec(wba.shape), _acc_spec(wbs.shape), _acc_spec(wo.shape)],
        out_shape=[jax.ShapeDtypeStruct((T, D_ATT), BF16), jax.ShapeDtypeStruct((T, D_SGU), BF16),
                   jax.ShapeDtypeStruct((T, 2 * D_MODEL), BF16), jax.ShapeDtypeStruct((1, 2 * D_MODEL), F32),
                   jax.ShapeDtypeStruct(wba.shape, F32), jax.ShapeDtypeStruct(wbs.shape, F32),
                   jax.ShapeDtypeStruct(wo.shape, F32)])


def _loss_bwd(x, target, g):
    T = x.shape[0]

    def body(x_ref, t_ref, g_ref, dx_ref, loss_ref, dg_ref):
        @pl.when(pl.program_id(0) == 0)
        def _():
            loss_ref[...] = jnp.zeros_like(loss_ref)
            dg_ref[...] = jnp.zeros_like(dg_ref)

        gv = g_ref[...]
        xhat, r, y = _rms_fwd(x_ref[...], gv)
        err = y - t_ref[...]
        per_tok = jnp.mean(err * err, axis=-1, keepdims=True)
        loss_ref[...] += 0.5 * jnp.sum(per_tok, axis=0, keepdims=True)
        dxn, dg = _rms_bwd(err * (1.0 / D_MODEL), xhat, r, gv)
        dx_ref[...] = dxn
        dg_ref[...] += dg

    return pl.pallas_call(
        body, name="loss_bwd", grid=(T // TM,),
        in_specs=[_row_spec(TM, D_MODEL), _row_spec(TM, D_MODEL), _const_spec((1, D_MODEL))],
        out_specs=[_row_spec(TM, D_MODEL), _acc_spec((1, 128)), _acc_spec((1, D_MODEL))],
        out_shape=[jax.ShapeDtypeStruct((T, D_MODEL), F32), jax.ShapeDtypeStruct((1, 128), F32),
                   jax.ShapeDtypeStruct((1, D_MODEL), F32)],
        compiler_params=_params(("arbitrary",)),
    )(x, target, g)


BIG = ("ffn1_w_gate", "ffn1_w_up", "ffn1_w_down", "w_in", "w_branch_att", "w_branch_sgu", "w_out",
       "ffn2_w_gate", "ffn2_w_up", "ffn2_w_down")
SMALL = ("norm_ffn1", "norm_mix", "b_gate", "rel_bias", "sgu_ln_g", "sgu_ln_b", "sgu_w_s", "sgu_b_s", "norm_ffn2",
         "norm_final")


G_FFN1 = ("ffn1_w_gate", "ffn1_w_up", "ffn1_w_down")
G_MIX = ("w_in", "w_branch_att", "w_branch_sgu", "w_out")
G_FFN2 = ("ffn2_w_gate", "ffn2_w_up", "ffn2_w_down")


def _local_step(x, target, wb, ws, dist=None):
    def gather_on(names):
        return _ag_payload([wb[n] for n in names]) if dist else None

    t2 = _relbias_fwd(ws["rel_bias"])
    bias2 = _bias_blocks(t2)
    bst = ws["sgu_b_s"].T
    w_st = jnp.swapaxes(ws["sgu_w_s"], 1, 2)

    if dist:
        wb.update(zip(G_FFN1, _call(lambda: None, gather_on(G_FFN1), name="allgather_ffn1", grid=None, in_specs=[],
                                    out_specs=[], out_shape=[])))
    x1, h1, a1, b1, *got = _ffn_fwd(x, ws["norm_ffn1"], wb["ffn1_w_gate"], wb["ffn1_w_up"], wb["ffn1_w_down"],
                                    "ffn1_fwd", gather_on(G_MIX))
    wb.update(zip(G_MIX, got))
    h2, qkv, zs, gl, *got = _in_fwd(x1, ws["norm_mix"], wb["w_in"], gather_on(G_FFN2[0:1]))
    wb.update(zip(G_FFN2[0:1], got))
    y_att, *got = _att_fwd(qkv, bias2, gather_on(G_FFN2[1:2]))
    wb.update(zip(G_FFN2[1:2], got))
    y_sgu = _sgu_fwd(zs, ws["sgu_ln_g"], ws["sgu_ln_b"], ws["sgu_w_s"], bst)
    x2, merged, pa, ps, *got = _merge_fwd(x1, y_att, y_sgu, gl, ws["b_gate"], wb["w_branch_att"], wb["w_branch_sgu"],
                                          wb["w_out"], gather_on(G_FFN2[2:3]))
    wb.update(zip(G_FFN2[2:3], got))
    x3, h3, a3, b3 = _ffn_fwd(x2, ws["norm_ffn2"], wb["ffn2_w_gate"], wb["ffn2_w_up"], wb["ffn2_w_down"], "ffn2_fwd")
    dx3, loss, g_final = _loss_bwd(x3, target, ws["norm_final"])

    gb, gs, sums = {}, {"norm_final": g_final}, {}

    def pair_on(names, small=None):
        return _px_payload([gb[n] for n in names], small) if dist else None

    def pair_add(names, halves):
        for n, rv in zip(names, halves):
            sums[n] = _pair_add(gb[n], rv, dist[0], dist[1], "pair_add_" + n)

    def chips_on(names):
        return _cx_payload([sums[n][1] for n in names], [sums[n][2] for n in names]) if dist else None

    dx2, da3, db3, gs["norm_ffn2"] = _ffn_dgrad(dx3, x2, a3, b3, ws["norm_ffn2"], wb["ffn2_w_gate"], wb["ffn2_w_up"],
                                                wb["ffn2_w_down"], "ffn2_dgrad")
    gb["ffn2_w_gate"], gb["ffn2_w_up"], gb["ffn2_w_down"] = _ffn_wgrad(h3, dx3, a3, b3, da3, db3, "ffn2_wgrad")
    dy_att, dy_sgu, dgl, gs["b_gate"], gb["w_branch_att"], gb["w_branch_sgu"], gb["w_out"], *got = _merge_bwd(
        dx2, y_att, y_sgu, gl, merged, pa, ps, ws["b_gate"], wb["w_branch_att"], wb["w_branch_sgu"], wb["w_out"],
        pair_on(G_FFN2))
    pair_add(G_FFN2, got)
    dq, dk, dv, db2t, *lands2 = _att_bwd(qkv, dy_att, jnp.swapaxes(bias2, 1, 2), chips_on(G_FFN2))
    gs["rel_bias"] = _relbias_bwd(_unskew(jnp.swapaxes(db2t, 1, 2)))
    dzs, gs["sgu_w_s"], dbt, gs["sgu_ln_g"], gs["sgu_ln_b"] = _sgu_bwd(zs, dy_sgu, ws["sgu_ln_g"], ws["sgu_ln_b"],
                                                                      ws["sgu_w_s"], w_st, bst)
    gs["sgu_b_s"] = dbt.T
    dx1, dz, gs["norm_mix"] = _in_dgrad(dx2, x1, ws["norm_mix"], wb["w_in"], dq, dk, dv, dzs, dgl)
    gb["w_in"] = _in_wgrad(h2, dz)
    gx, da1, db1, gs["norm_ffn1"], *got = _ffn_dgrad(dx1, x, a1, b1, ws["norm_ffn1"], wb["ffn1_w_gate"],
                                                    wb["ffn1_w_up"], wb["ffn1_w_down"], "ffn1_dgrad", pair_on(G_MIX))
    pair_add(G_MIX, got)
    gb["ffn1_w_gate"], gb["ffn1_w_up"], gb["ffn1_w_down"], *lands_mix = _ffn_wgrad(h1, dx1, a1, b1, da1, db1,
                                                                                   "ffn1_wgrad", chips_on(G_MIX))
    if not dist:
        return loss, gx, gb, gs

    tail = _tail_reduce([gb[n] for n in G_FFN1], _pack_small(gs, loss))
    for i, n in enumerate(G_FFN1):
        sums[n] = (tail[i],)
    lands1, small_sums = tail[len(G_FFN1):2 * len(G_FFN1)], tail[-1]
    lands = dict(zip(G_FFN2 + G_MIX + G_FFN1, list(lands2) + list(lands_mix) + list(lands1)))
    fulls = [_final_sum(sums[n][0], lands[n], dist[1], dist[0], "final_sum_" + n) for n in BIG]
    return loss, gx, dict(zip(BIG, _sibling_share(fulls, "sibling_share"))), small_sums


_ANY = pl.BlockSpec(memory_space=pl.ANY)
_VMEM = pl.BlockSpec(memory_space=pltpu.VMEM)


def _mesh_pos():
    return lax.axis_index("x"), lax.axis_index("y"), lax.axis_index("c")


def _cast_slots(shards, chip, name):
    n = len(shards)
    r, ncol = shards[0].shape
    tr = r // 2

    def body(me_ref, *refs):
        for i_ref, o_ref in zip(refs[:n], refs[n:]):
            o_ref[0] = i_ref[...].astype(BF16)

    grid_spec = pltpu.PrefetchScalarGridSpec(
        num_scalar_prefetch=1, grid=(r // tr,),
        in_specs=[pl.BlockSpec((tr, ncol), lambda i, me: (i, 0))] * n,
        out_specs=[pl.BlockSpec((1, tr, ncol), lambda i, me: (me[0], i, 0))] * n)
    return pl.pallas_call(
        body, name=name, grid_spec=grid_spec,
        out_shape=[jax.ShapeDtypeStruct((N_SHARD, r, ncol), BF16)] * n,
        compiler_params=_params(("arbitrary",)),
    )(chip, *shards)


class _Payload:
    def __init__(self, arrays, out_shapes, aliases, scratch, phases):
        self.arrays = list(arrays)
        self.out_shapes = list(out_shapes)
        self.aliases = dict(aliases)
        self.scratch = list(scratch)
        self.phases = phases


def _remote(src, dst, ssem, rsem, dev):
    return pltpu.make_async_remote_copy(src_ref=src, dst_ref=dst, send_sem=ssem, recv_sem=rsem, device_id=dev,
                                        device_id_type=MESH)


def _call(body, payload, *, name, grid, in_specs, out_specs, out_shape, scratch_shapes=(), sem=None, when=None,
          operands=(), takes_middle=False):
    in_specs, out_specs, out_shape = list(in_specs), list(out_specs), list(out_shape)
    scratch_shapes = list(scratch_shapes)
    n_in, n_out, n_scr = len(in_specs), len(out_specs), len(scratch_shapes)
    kwargs = {}
    kernel = body
    if payload is not None:
        k_in, k_out = len(payload.arrays), len(payload.out_shapes)
        rank = len(grid) if grid else 0

        def kernel(*refs):
            a, b = n_in, n_in + k_in
            c, d = b + n_out, b + n_out + k_out
            e = d + n_scr
            phases = payload.phases(refs[a:b], refs[c:d], refs[e:])

            def run():
                body(*refs[:a], *refs[b:c], *refs[d:e])

            if not grid:
                phases[0]()
                if len(phases) == 3 and takes_middle:
                    body(*refs[:a], *refs[b:c], *refs[d:e], middle=phases[1])
                    phases[2]()
                    return
                run()
                for ph in phases[1:]:
                    ph()
                return
            step = pl.program_id(0)
            if rank == 2:
                step = step * grid[1] + pl.program_id(1)
            marks = list(when)
            if len(phases) == 3:
                marks = [when[0], (max(when[1][0] - 3, 0), False), when[1]]
            for ph, (at, before) in zip(phases, marks):
                if before:
                    pl.when(step == at)(ph)
            run()
            for ph, (at, before) in zip(phases, marks):
                if not before:
                    pl.when(step == at)(ph)

        in_specs += [_ANY] * k_in
        out_specs += [_ANY] * k_out
        out_shape += payload.out_shapes
        scratch_shapes += payload.scratch
        kwargs["input_output_aliases"] = {n_in + i: n_out + j for i, j in payload.aliases.items()}
        operands = tuple(operands) + tuple(payload.arrays)
    if grid:
        kwargs["grid"] = grid
    return pl.pallas_call(kernel, name=name, in_specs=in_specs, out_specs=out_specs, out_shape=out_shape,
                          scratch_shapes=scratch_shapes, compiler_params=_params(sem), **kwargs)(*operands)


def _ag_payload(slots):
    n = len(slots)

    def phases(_, refs, sems):
        send_i, recv_i, send_d, recv_d = sems
        x, y, c = _mesh_pos()
        me = 2 * x + y

        def half(w, core):
            rh = slots[w].shape[1] // 2
            return pl.ds(core * rh, rh)

        def ici(w, j):
            t = (me + 1 + j) % N_SHARD
            mine = refs[w].at[me, half(w, c), :]
            return _remote(mine, mine, send_i.at[3 * w + j], recv_i.at[3 * w + j], (t // 2, t % 2, c))

        def d2d(w, j, core):
            s = (me + 3 - j) % N_SHARD
            land = refs[w].at[s, half(w, core), :]
            return _remote(land, land, send_d.at[3 * w + j], recv_d.at[3 * w + j], (x, y, 1 - c))

        def start():
            for w in range(n):
                for j in range(3):
                    ici(w, j).start()

        def forward():
            for w in range(n):
                for j in range(3):
                    s = (me + 3 - j) % N_SHARD
                    land = refs[w].at[s, half(w, c), :]
                    _remote(land, land, send_i.at[3 * w + j], recv_i.at[3 * w + j], (x, y, c)).wait_recv()
                    d2d(w, j, c).start()

        def finish():
            for w in range(n):
                for j in range(3):
                    d2d(w, j, 1 - c).wait_recv()
            for w in range(n):
                for j in range(3):
                    ici(w, j).wait_send()
                    d2d(w, j, c).wait_send()

        return [start, forward, finish]

    return _Payload(slots, [jax.ShapeDtypeStruct(s.shape, s.dtype) for s in slots], {i: i for i in range(n)},
                    [pltpu.SemaphoreType.DMA((3 * n,)) for _ in range(4)], phases)


def _px_payload(grads, small=None):
    arrays = list(grads) + ([small] if small is not None else [])
    n = len(arrays)

    def phases(ins, outs, sems):
        send, recv = sems
        x, y, c = _mesh_pos()

        def copy(w):
            if w < len(grads):
                rh = grads[w].shape[1] // 2
                src = ins[w].at[:, pl.ds((1 - c) * rh, rh), :]
            else:
                src = ins[w]
            return _remote(src, outs[w], send.at[w], recv.at[w], (x, y, 1 - c))

        def start():
            for w in range(n):
                copy(w).start()

        def finish():
            for w in range(n):
                copy(w).wait()

        return [start, finish]

    out_shapes = [jax.ShapeDtypeStruct((N_SHARD, g.shape[1] // 2, g.shape[2]), F32) for g in grads]
    if small is not None:
        out_shapes.append(jax.ShapeDtypeStruct(small.shape, F32))
    return _Payload(arrays, out_shapes, {}, [pltpu.SemaphoreType.DMA((n,)), pltpu.SemaphoreType.DMA((n,))], phases)


def _cx_payload(pbs, lands):
    n = len(pbs)

    def phases(ins, outs, sems):
        send, recv = sems
        x, y, c = _mesh_pos()
        me = 2 * x + y

        def copy(w, j):
            t = (me + 1 + j) % N_SHARD
            return _remote(ins[w].at[t], outs[w].at[me], send.at[3 * w + j], recv.at[3 * w + j], (t // 2, t % 2, c))

        def start():
            for w in range(n):
                for j in range(3):
                    copy(w, j).start()

        def finish():
            for w in range(n):
                for j in range(3):
                    copy(w, j).wait()

        return [start, finish]

    return _Payload(list(pbs) + list(lands), [jax.ShapeDtypeStruct(p.shape, BF16) for p in lands],
                    {n + i: i for i in range(n)},
                    [pltpu.SemaphoreType.DMA((3 * n,)), pltpu.SemaphoreType.DMA((3 * n,))], phases)


def _pair_add(g, rv, core, chip, name):
    _, r, ncol = g.shape
    rh = r // 2

    def body(c_ref, me_ref, g_ref, rv_ref, pf_ref, pb_ref, land_ref):
        s = g_ref[0] + rv_ref[0]
        sb = s.astype(BF16)
        pb_ref[0] = sb

        @pl.when(pl.program_id(0) == me_ref[0])
        def _():
            pf_ref[...] = s
            land_ref[0] = sb

    slot = pl.BlockSpec((1, rh, ncol), lambda s, c, me: (s, 0, 0))
    grid_spec = pltpu.PrefetchScalarGridSpec(
        num_scalar_prefetch=2, grid=(N_SHARD,),
        in_specs=[pl.BlockSpec((1, rh, ncol), lambda s, c, me: (s, c[0], 0)), slot],
        out_specs=[pl.BlockSpec((rh, ncol), lambda s, c, me: (0, 0)), slot,
                   pl.BlockSpec((1, rh, ncol), lambda s, c, me: (me[0], 0, 0))])
    return pl.pallas_call(
        body, name=name, grid_spec=grid_spec,
        out_shape=[jax.ShapeDtypeStruct((rh, ncol), F32), jax.ShapeDtypeStruct((N_SHARD, rh, ncol), BF16),
                   jax.ShapeDtypeStruct((N_SHARD, rh, ncol), BF16)],
        compiler_params=_params(("arbitrary",)),
    )(core, chip, g, rv)


def _tail_reduce(grads, small):
    n = len(grads)
    _, r, ncol = grads[0].shape
    rh = r // 2

    def body(*refs):
        g_hbm, sm = refs[:n], refs[n]
        pf, land, sm_out = refs[n + 1:2 * n + 1], refs[2 * n + 1:3 * n + 1], refs[3 * n + 1]
        scr = refs[3 * n + 2:]
        rv, mine, sendb = scr[:n], scr[n:2 * n], scr[2 * n:3 * n]
        sm_rv, sm_sum, d_send, d_recv, load, i_send, i_recv, store = scr[3 * n:]
        x, y, c = _mesh_pos()
        me = 2 * x + y
        sib = (x, y, 1 - c)

        def pair(w):
            src = g_hbm[w].at[:, pl.ds((1 - c) * rh, rh), :] if w < n else sm
            return _remote(src, rv[w] if w < n else sm_rv, d_send.at[w], d_recv.at[w], sib)

        def chips(w, j):
            t = (me + 1 + j) % N_SHARD
            src = sendb[w].at[t] if w < n else sm_sum
            dst = land[w].at[me] if w < n else sm_out.at[me]
            return _remote(src, dst, i_send.at[3 * w + j], i_recv.at[3 * w + j], (t // 2, t % 2, c))

        loads = [pltpu.make_async_copy(g_hbm[w].at[:, pl.ds(c * rh, rh), :], mine[w], load.at[w]) for w in range(n)]
        for w in range(n + 1):
            pair(w).start()
        for cp in loads:
            cp.start()
        stores = []
        for w in range(n):
            loads[w].wait()
            pair(w).wait_recv()
            for k in range(N_SHARD):
                s = mine[w][k] + rv[w][k]
                mine[w][k] = s
                sendb[w][k] = s.astype(BF16)
            stores += [pltpu.make_async_copy(mine[w].at[me], pf[w], store.at[2 * w]),
                       pltpu.make_async_copy(sendb[w].at[me], land[w].at[me], store.at[2 * w + 1])]
            for cp in stores[-2:]:
                cp.start()
            for j in range(3):
                chips(w, j).start()
        pair(n).wait_recv()
        sm_sum[...] = sm[...] + sm_rv[...]
        stores.append(pltpu.make_async_copy(sm_sum, sm_out.at[me], store.at[2 * n]))
        stores[-1].start()
        for j in range(3):
            chips(n, j).start()
        for w in range(n + 1):
            pair(w).wait_send()
            for j in range(3):
                chips(w, j).wait()
        for cp in stores:
            cp.wait()

    half = (N_SHARD, rh, ncol)
    return _call(
        body, None, name="tail_reduce", grid=None,
        in_specs=[_ANY] * n + [_VMEM], out_specs=[_ANY] * (2 * n + 1),
        out_shape=([jax.ShapeDtypeStruct((rh, ncol), F32)] * n + [jax.ShapeDtypeStruct(half, BF16)] * n
                   + [jax.ShapeDtypeStruct((N_SHARD,) + small.shape, F32)]),
        scratch_shapes=([pltpu.VMEM(half, F32)] * (2 * n) + [pltpu.VMEM(half, BF16)] * n
                        + [pltpu.VMEM(small.shape, F32), pltpu.VMEM(small.shape, F32),
                           pltpu.SemaphoreType.DMA((n + 1,)), pltpu.SemaphoreType.DMA((n + 1,)),
                           pltpu.SemaphoreType.DMA((n,)), pltpu.SemaphoreType.DMA((3 * n + 3,)),
                           pltpu.SemaphoreType.DMA((3 * n + 3,)), pltpu.SemaphoreType.DMA((2 * n + 1,))]),
        operands=(*grads, small))


def _final_sum(pf, land, chip, core, name):
    _, rh, ncol = land.shape

    def body(me_ref, c_ref, pf_ref, land_ref, o_ref):
        me = me_ref[0]
        acc = jnp.zeros((rh, ncol), F32)
        for k in range(N_SHARD):
            acc = acc + jnp.where(me == k, pf_ref[...], land_ref[k].astype(F32))
        o_ref[...] = acc

    grid_spec = pltpu.PrefetchScalarGridSpec(
        num_scalar_prefetch=2, grid=(1,),
        in_specs=[pl.BlockSpec((rh, ncol), lambda i, me, c: (0, 0)),
                  pl.BlockSpec((N_SHARD, rh, ncol), lambda i, me, c: (0, 0, 0))],
        out_specs=pl.BlockSpec((rh, ncol), lambda i, me, c: (c[0], 0)))
    return pl.pallas_call(
        body, name=name, grid_spec=grid_spec, out_shape=jax.ShapeDtypeStruct((2 * rh, ncol), F32),
        compiler_params=_params(("arbitrary",)),
    )(chip, core, pf, land)


def _sibling_share(fulls, name):
    n = len(fulls)

    def body(*refs):
        outs = refs[n:2 * n]
        send, recv = refs[2 * n:]
        x, y, c = _mesh_pos()
        cps = []
        for w in range(n):
            rh = fulls[w].shape[0] // 2
            mine = outs[w].at[pl.ds(c * rh, rh), :]
            cp = pltpu.make_async_remote_copy(src_ref=mine, dst_ref=mine, send_sem=send.at[w], recv_sem=recv.at[w],
                                              device_id=(x, y, 1 - c), device_id_type=MESH)
            cp.start()
            cps.append(cp)
        for cp in cps:
            cp.wait()

    return pl.pallas_call(
        body, name=name,
        in_specs=[_ANY] * n, out_specs=[_ANY] * n,
        out_shape=[jax.ShapeDtypeStruct(f.shape, F32) for f in fulls],
        input_output_aliases={i: i for i in range(n)},
        scratch_shapes=[pltpu.SemaphoreType.DMA((n,)), pltpu.SemaphoreType.DMA((n,))],
    )(*fulls)


_ROW = {"rel_bias": 128, "sgu_b_s": 136, "norm_ffn1": 144, "norm_mix": 145, "norm_ffn2": 146, "norm_final": 147,
        "b_gate": 148, "sgu_ln_g": 150, "sgu_ln_b": 151}


def _pack_small(gs, loss):
    def body(ws, rel, bs, n1, nm, n2, nf, bg, lg, lb, loss_ref, o_ref):
        o_ref[...] = jnp.zeros_like(o_ref)
        o_ref[LOSS_ROW:LOSS_ROW + 1, 0:128] = loss_ref[...]
        for g in range(SGU_GROUPS):
            o_ref[0:SGU_BLOCK, g * SGU_BLOCK:(g + 1) * SGU_BLOCK] = ws[g]
        o_ref[128:136, 0:REL_PAD] = rel[...]
        o_ref[136:144, 0:SGU_BLOCK] = bs[...]
        o_ref[144:145, :] = n1[...]
        o_ref[145:146, :] = nm[...]
        o_ref[146:147, :] = n2[...]
        o_ref[147:148, :] = nf[...]
        o_ref[148:149, :] = bg[:, 0:D_MODEL]
        o_ref[149:150, :] = bg[:, D_MODEL:2 * D_MODEL]
        o_ref[150:151, 0:D_SGU] = lg[...]
        o_ref[151:152, 0:D_SGU] = lb[...]

    order = ("sgu_w_s", "rel_bias", "sgu_b_s", "norm_ffn1", "norm_mix", "norm_ffn2", "norm_final", "b_gate", "sgu_ln_g",
             "sgu_ln_b")
    return pl.pallas_call(body, name="pack_small", out_shape=jax.ShapeDtypeStruct((SMALL_ROWS, D_MODEL), F32))(
        *[gs[k] for k in order], loss)


def _adam(w, g, m, v):
    m2 = ADAM_B1 * m + (1.0 - ADAM_B1) * g
    v2 = ADAM_B2 * v + (1.0 - ADAM_B2) * (g * g)
    m_hat = m2 / (1.0 - ADAM_B1 ** ADAM_STEP)
    v_hat = v2 / (1.0 - ADAM_B2 ** ADAM_STEP)
    delta = -ADAM_LR * (m_hat / (jnp.sqrt(v_hat) + ADAM_EPS) + ADAM_WD * w)
    return delta, m2, v2


def _adam_small(sin, w, m, v):
    names = SMALL
    k = len(names)

    def body(*refs):
        sin_ref = refs[0]
        w_r, m_r, v_r = refs[1:1 + k], refs[1 + k:1 + 2 * k], refs[1 + 2 * k:1 + 3 * k]
        outs = refs[1 + 3 * k:]
        tot = sin_ref[0] + sin_ref[1] + sin_ref[2] + sin_ref[3]
        outs[4 * k][...] = tot[LOSS_ROW:LOSS_ROW + 1, 0:128]
        for i, name in enumerate(names):
            o = outs[4 * i:4 * i + 4]
            if name == "sgu_w_s":
                for gi in range(SGU_GROUPS):
                    g = tot[0:SGU_BLOCK, gi * SGU_BLOCK:(gi + 1) * SGU_BLOCK]
                    res = (g,) + _adam(w_r[i][gi], g, m_r[i][gi], v_r[i][gi])
                    for ref, val in zip(o, res):
                        ref[gi] = val
                continue
            r0 = _ROW[name]
            if name == "rel_bias":
                g = tot[r0:r0 + HEADS, 0:REL_PAD]
            elif name == "sgu_b_s":
                g = tot[r0:r0 + SGU_GROUPS, 0:SGU_BLOCK]
            elif name == "b_gate":
                g = jnp.concatenate([tot[r0:r0 + 1, :], tot[r0 + 1:r0 + 2, :]], axis=1)
            elif name in ("sgu_ln_g", "sgu_ln_b"):
                g = tot[r0:r0 + 1, 0:D_SGU]
            else:
                g = tot[r0:r0 + 1, :]
            res = (g,) + _adam(w_r[i][...], g, m_r[i][...], v_r[i][...])
            for ref, val in zip(o, res):
                ref[...] = val

    out_shape = []
    for name in names:
        out_shape += [jax.ShapeDtypeStruct(w[name].shape, F32)] * 4
    out_shape.append(jax.ShapeDtypeStruct((1, 128), F32))
    flat = pl.pallas_call(body, name="adam_small", out_shape=out_shape, compiler_params=_params())(
        sin, *[w[n] for n in names], *[m[n] for n in names], *[v[n] for n in names])
    return {name: tuple(flat[4 * i:4 * i + 4]) for i, name in enumerate(names)}, flat[4 * k]


def _adam_big(w, g, m, v, name):
    r, ncol = w.shape
    tr = 256 if r % 256 == 0 else r // 2

    def body(w_ref, g_ref, m_ref, v_ref, g2_ref, d_ref, m2_ref, v2_ref):
        gv = g_ref[...]
        g2_ref[...] = gv
        d_ref[...], m2_ref[...], v2_ref[...] = _adam(w_ref[...], gv, m_ref[...], v_ref[...])

    spec = pl.BlockSpec((tr, ncol), lambda i: (i, 0))
    return pl.pallas_call(
        body, name=name, grid=(r // tr,), in_specs=[spec] * 4, out_specs=[spec] * 4,
        out_shape=[jax.ShapeDtypeStruct(w.shape, F32)] * 4, compiler_params=_params(("arbitrary",)),
    )(w, g, m, v)


WEIGHTS = ("norm_ffn1", "ffn1_w_gate", "ffn1_w_up", "ffn1_w_down", "norm_mix", "w_in", "b_gate", "rel_bias", "sgu_ln_g",
           "sgu_ln_b", "sgu_w_s", "sgu_b_s", "w_branch_att", "w_branch_sgu", "w_out", "norm_ffn2", "ffn2_w_gate",
           "ffn2_w_up", "ffn2_w_down", "norm_final")


GATE_UP = ("ffn1_w_gate", "ffn1_w_up", "ffn2_w_gate", "ffn2_w_up")
_FFN = ("ffn1_w_gate", "ffn1_w_up", "ffn1_w_down", "ffn2_w_gate", "ffn2_w_up", "ffn2_w_down")
_CAST_GROUPS = ((_FFN, "cast_ffn"), (("w_in",), "cast_w_in"), (("w_branch_att", "w_branch_sgu"), "cast_branch"),
                (("w_out",), "cast_w_out"))


def _big_form(name, a):
    return jnp.swapaxes(a, 1, 2)[0] if name in GATE_UP else a[0]


def _big_back(name, a):
    return jnp.swapaxes(a[None], 1, 2) if name in GATE_UP else a[None]


def _small_form(name, a):
    if name == "norm_final":
        return a.reshape(1, D_MODEL)
    if name == "rel_bias":
        return jnp.pad(a[0], ((0, 0), (0, REL_PAD - N_REL)))
    if name in ("sgu_w_s", "sgu_b_s"):
        return a[0]
    return a


def _small_back(name, a, like):
    if name == "rel_bias":
        a = a[:, :N_REL]
    return a.reshape(like.shape)


def kernel(x, norm_ffn1, ffn1_w_gate, ffn1_w_up, ffn1_w_down, norm_mix, w_in, b_gate, rel_bias, sgu_ln_g, sgu_ln_b, sgu_w_s, sgu_b_s, w_branch_att, w_branch_sgu, w_out, norm_ffn2, ffn2_w_gate, ffn2_w_up, ffn2_w_down, norm_final, loss_target, m_norm_ffn1, m_ffn1_w_gate, m_ffn1_w_up, m_ffn1_w_down, m_norm_mix, m_w_in, m_b_gate, m_rel_bias, m_sgu_ln_g, m_sgu_ln_b, m_sgu_w_s, m_sgu_b_s, m_w_branch_att, m_w_branch_sgu, m_w_out, m_norm_ffn2, m_ffn2_w_gate, m_ffn2_w_up, m_ffn2_w_down, m_norm_final, v_norm_ffn1, v_ffn1_w_gate, v_ffn1_w_up, v_ffn1_w_down, v_norm_mix, v_w_in, v_b_gate, v_rel_bias, v_sgu_ln_g, v_sgu_ln_b, v_sgu_w_s, v_sgu_b_s, v_w_branch_att, v_w_branch_sgu, v_w_out, v_norm_ffn2, v_ffn2_w_gate, v_ffn2_w_up, v_ffn2_w_down, v_norm_final):
    w = dict(norm_ffn1=norm_ffn1, ffn1_w_gate=ffn1_w_gate, ffn1_w_up=ffn1_w_up, ffn1_w_down=ffn1_w_down, norm_mix=norm_mix,
             w_in=w_in, b_gate=b_gate, rel_bias=rel_bias, sgu_ln_g=sgu_ln_g, sgu_ln_b=sgu_ln_b, sgu_w_s=sgu_w_s,
             sgu_b_s=sgu_b_s, w_branch_att=w_branch_att, w_branch_sgu=w_branch_sgu, w_out=w_out, norm_ffn2=norm_ffn2,
             ffn2_w_gate=ffn2_w_gate, ffn2_w_up=ffn2_w_up, ffn2_w_down=ffn2_w_down, norm_final=norm_final)
    m = dict(norm_ffn1=m_norm_ffn1, ffn1_w_gate=m_ffn1_w_gate, ffn1_w_up=m_ffn1_w_up, ffn1_w_down=m_ffn1_w_down,
             norm_mix=m_norm_mix, w_in=m_w_in, b_gate=m_b_gate, rel_bias=m_rel_bias, sgu_ln_g=m_sgu_ln_g,
             sgu_ln_b=m_sgu_ln_b, sgu_w_s=m_sgu_w_s, sgu_b_s=m_sgu_b_s, w_branch_att=m_w_branch_att,
             w_branch_sgu=m_w_branch_sgu, w_out=m_w_out, norm_ffn2=m_norm_ffn2, ffn2_w_gate=m_ffn2_w_gate,
             ffn2_w_up=m_ffn2_w_up, ffn2_w_down=m_ffn2_w_down, norm_final=m_norm_final)
    v = dict(norm_ffn1=v_norm_ffn1, ffn1_w_gate=v_ffn1_w_gate, ffn1_w_up=v_ffn1_w_up, ffn1_w_down=v_ffn1_w_down,
             norm_mix=v_norm_mix, w_in=v_w_in, b_gate=v_b_gate, rel_bias=v_rel_bias, sgu_ln_g=v_sgu_ln_g,
             sgu_ln_b=v_sgu_ln_b, sgu_w_s=v_sgu_w_s, sgu_b_s=v_sgu_b_s, w_branch_att=v_w_branch_att,
             w_branch_sgu=v_w_branch_sgu, w_out=v_w_out, norm_ffn2=v_norm_ffn2, ffn2_w_gate=v_ffn2_w_gate,
             ffn2_w_up=v_ffn2_w_up, ffn2_w_down=v_ffn2_w_down, norm_final=v_norm_final)

    core = lax.axis_index("c").astype(jnp.int32).reshape(1)
    chip = (2 * lax.axis_index("x") + lax.axis_index("y")).astype(jnp.int32).reshape(1)

    wk = {n: _big_form(n, w[n]) for n in BIG}
    slots = {}
    for names, call in _CAST_GROUPS:
        slots.update(zip(names, _cast_slots([wk[n] for n in names], chip, call)))
    ws = {n: _small_form(n, w[n]) for n in SMALL}
    _, gx, shard_grads, small_sums = _local_step(x[0], loss_target[0], slots, ws, (core, chip))

    small, loss = _adam_small(small_sums, ws, {n: _small_form(n, m[n]) for n in SMALL},
                              {n: _small_form(n, v[n]) for n in SMALL})
    grad, delta, new_m, new_v = {}, {}, {}, {}
    for n in SMALL:
        grad[n], delta[n], new_m[n], new_v[n] = (_small_back(n, a, w[n]) for a in small[n])
    for n in BIG:
        g2, d2, m2, v2 = _adam_big(wk[n], shard_grads[n], _big_form(n, m[n]), _big_form(n, v[n]), "adam_" + n)
        grad[n], delta[n], new_m[n], new_v[n] = (_big_back(n, a) for a in (g2, d2, m2, v2))

    return (loss[0, 0], gx.reshape(x.shape), *[grad[n] for n in WEIGHTS], *[delta[n] for n in WEIGHTS],
            *[new_m[n] for n in WEIGHTS], *[new_v[n] for n in WEIGHTS])
```

```python
import functools

import jax
import jax.numpy as jnp
from jax import lax
from jax.experimental import pallas as pl
from jax.experimental.pallas import tpu as pltpu

F32 = jnp.float32
BF16 = jnp.bfloat16

D_MODEL = 1024
N_SHARD = 4
D_FF = 2816
FF_S = D_FF // N_SHARD
D_ATT = 512
D_SGU = 512
D_IN = 3 * D_ATT + 2 * D_SGU + 2 * D_MODEL
IN_S = D_IN // N_SHARD
BR_S = D_MODEL // N_SHARD
HEADS = 8
HEAD_DIM = 64
CHUNK = 64
N_LEFT = 8
BAND = (N_LEFT + 1) * CHUNK
REL_CLIP = 256
N_REL = 2 * REL_CLIP + 1
REL_PAD = 640
SGU_BLOCK = 128
SGU_GROUPS = 8
SGU_GDIM = 64
EPS = 1e-6
NEG_INF = -1e30

ATT_ROWS = 2 * CHUNK
ATT_KEYS = BAND + CHUNK
ATT_PAD = N_LEFT * CHUNK

ADAM_LR = 0.001
ADAM_B1 = 0.9
ADAM_B2 = 0.999
ADAM_EPS = 1e-08
ADAM_WD = 0.01
ADAM_STEP = 10

TM = 256
TW = 1024
VMEM_LIMIT = 56 * 1024 * 1024

SMALL_ROWS = 160
LOSS_ROW = 152
MESH = pl.DeviceIdType.MESH

_NT = (((1,), (1,)), ((), ()))
_TN = (((0,), (0,)), ((), ()))


def _params(sem=None):
    return pltpu.CompilerParams(dimension_semantics=sem, vmem_limit_bytes=VMEM_LIMIT)


def _const_spec(shape):
    nd = len(shape)
    return pl.BlockSpec(shape, lambda *_: (0,) * nd, pipeline_mode=pl.Buffered(1))


def _acc_spec(shape):
    nd = len(shape)
    return pl.BlockSpec(shape, lambda *_: (0,) * nd)


def _row_spec(tm, ncols, off=0):
    return pl.BlockSpec((tm, ncols), lambda i: (i + off, 0))


def _row3_spec(tm, ncols):
    return pl.BlockSpec((N_SHARD, tm, ncols), lambda i: (0, i, 0))


def _dot(a, b):
    return jnp.dot(a, b, preferred_element_type=F32)


def _dot_nt(a, b):
    return lax.dot_general(a, b, _NT, preferred_element_type=F32)


def _dot_tn(a, b):
    return lax.dot_general(a, b, _TN, preferred_element_type=F32)


def _rms_fwd(x, g):
    r = lax.rsqrt(jnp.mean(x * x, axis=-1, keepdims=True) + EPS)
    xhat = x * r
    return xhat, r, xhat * g


def _rms_bwd(dh, xhat, r, g):
    dxhat = dh * g
    dx = r * (dxhat - xhat * jnp.mean(dxhat * xhat, axis=-1, keepdims=True))
    dg = jnp.sum(dh * xhat, axis=0, keepdims=True)
    return dx, dg


def _sigmoid(x):
    return 1.0 / (1.0 + jnp.exp(-x))


def _edges(n_steps):
    return [(0, True), (n_steps - 1, False)]


def _ffn_fwd(x, g, wg, wu, wd, name, payload=None):
    T = x.shape[0]

    def body(x_ref, g_ref, wg_ref, wu_ref, wd_ref, xo_ref, h_ref, a_ref, b_ref):
        xv = x_ref[...]
        hb = _rms_fwd(xv, g_ref[...])[2].astype(BF16)
        h_ref[...] = hb
        acc = jnp.zeros((TM, D_MODEL), F32)
        for s in range(N_SHARD):
            a = _dot_nt(hb, wg_ref[s])
            b = _dot_nt(hb, wu_ref[s])
            a_ref[s] = a.astype(BF16)
            b_ref[s] = b.astype(BF16)
            sv = a * _sigmoid(a) * b
            acc += _dot(sv.astype(BF16), wd_ref[s])
        xo_ref[...] = xv + 0.5 * acc

    return _call(
        body, payload, name=name, grid=(T // TM,), when=_edges(T // TM), sem=("arbitrary",),
        in_specs=[_row_spec(TM, D_MODEL), _const_spec((1, D_MODEL)), _const_spec(wg.shape), _const_spec(wu.shape),
                  _const_spec(wd.shape)],
        out_specs=[_row_spec(TM, D_MODEL), _row_spec(TM, D_MODEL), _row3_spec(TM, FF_S), _row3_spec(TM, FF_S)],
        out_shape=[jax.ShapeDtypeStruct((T, D_MODEL), F32), jax.ShapeDtypeStruct((T, D_MODEL), BF16),
                   jax.ShapeDtypeStruct((N_SHARD, T, FF_S), BF16), jax.ShapeDtypeStruct((N_SHARD, T, FF_S), BF16)],
        operands=(x, g, wg, wu, wd))


def _ffn_dgrad(dout, x, a, b, g, wg, wu, wd, name, payload=None):
    T = x.shape[0]

    def body(do_ref, x_ref, a_ref, b_ref, g_ref, wg_ref, wu_ref, wd_ref, dx_ref, da_ref, db_ref, dg_ref):
        do = do_ref[...]
        dob = do.astype(BF16)
        dh = jnp.zeros((TM, D_MODEL), F32)
        for s in range(N_SHARD):
            ds = 0.5 * _dot_nt(dob, wd_ref[s])
            av = a_ref[s].astype(F32)
            bv = b_ref[s].astype(F32)
            sig = _sigmoid(av)
            da = (ds * bv * (sig * (1.0 + av * (1.0 - sig)))).astype(BF16)
            db = (ds * (av * sig)).astype(BF16)
            da_ref[s] = da
            db_ref[s] = db
            dh += _dot(da, wg_ref[s]) + _dot(db, wu_ref[s])
        gv = g_ref[...]
        xhat, r, _ = _rms_fwd(x_ref[...], gv)
        dxn, dg = _rms_bwd(dh, xhat, r, gv)
        dx_ref[...] = do + dxn

        @pl.when(pl.program_id(0) == 0)
        def _():
            dg_ref[...] = jnp.zeros_like(dg_ref)

        dg_ref[...] += dg

    return _call(
        body, payload, name=name, grid=(T // TM,), when=_edges(T // TM), sem=("arbitrary",),
        in_specs=[_row_spec(TM, D_MODEL), _row_spec(TM, D_MODEL), _row3_spec(TM, FF_S), _row3_spec(TM, FF_S),
                  _const_spec((1, D_MODEL)), _const_spec(wg.shape), _const_spec(wu.shape), _const_spec(wd.shape)],
        out_specs=[_row_spec(TM, D_MODEL), _row3_spec(TM, FF_S), _row3_spec(TM, FF_S), _acc_spec((1, D_MODEL))],
        out_shape=[jax.ShapeDtypeStruct((T, D_MODEL), F32), jax.ShapeDtypeStruct((N_SHARD, T, FF_S), BF16),
                   jax.ShapeDtypeStruct((N_SHARD, T, FF_S), BF16), jax.ShapeDtypeStruct((1, D_MODEL), F32)],
        operands=(dout, x, a, b, g, wg, wu, wd))


def _ffn_wgrad(h, dout, a, b, da, db, name, payload=None):
    T = h.shape[0]

    def body(h_ref, do_ref, a_ref, b_ref, da_ref, db_ref, gwg_ref, gwu_ref, gwd_ref):
        @pl.when(pl.program_id(1) == 0)
        def _():
            gwg_ref[...] = jnp.zeros_like(gwg_ref)
            gwu_ref[...] = jnp.zeros_like(gwu_ref)
            gwd_ref[...] = jnp.zeros_like(gwd_ref)

        hv = h_ref[...]
        dob = do_ref[...].astype(BF16)
        av = a_ref[0].astype(F32)
        sv = (0.5 * av * _sigmoid(av) * b_ref[0].astype(F32)).astype(BF16)
        gwg_ref[0] += _dot_tn(da_ref[0], hv)
        gwu_ref[0] += _dot_tn(db_ref[0], hv)
        gwd_ref[0] += _dot_tn(sv, dob)

    tw = min(TW, T)
    tok = pl.BlockSpec((tw, D_MODEL), lambda s, i: (i, 0))
    act = pl.BlockSpec((1, tw, FF_S), lambda s, i: (s, i, 0))
    return _call(
        body, payload, name=name, grid=(N_SHARD, T // tw), when=_edges(N_SHARD * (T // tw)),
        sem=("arbitrary", "arbitrary"),
        in_specs=[tok, tok, act, act, act, act],
        out_specs=[pl.BlockSpec((1, FF_S, D_MODEL), lambda s, i: (s, 0, 0))] * 3,
        out_shape=[jax.ShapeDtypeStruct((N_SHARD, FF_S, D_MODEL), F32)] * 3,
        operands=(h, dout, a, b, da, db))


def _in_fwd(x, g, w_in, payload=None):
    T = x.shape[0]

    def body(x_ref, g_ref, w_ref, h_ref, qkv_ref, zs_ref, gl_ref):
        hb = _rms_fwd(x_ref[...], g_ref[...])[2].astype(BF16)
        h_ref[...] = hb
        z0 = _dot(hb, w_ref[0])
        qkv_ref[:, 0:IN_S] = z0.astype(BF16)
        z1 = _dot(hb, w_ref[1])
        qkv_ref[:, IN_S:3 * D_ATT] = z1[:, 0:384].astype(BF16)
        zs_ref[:, 0:768] = z1[:, 384:IN_S]
        z2 = _dot(hb, w_ref[2])
        zs_ref[:, 768:1024] = z2[:, 0:256]
        gl_ref[:, 0:896] = z2[:, 256:IN_S]
        gl_ref[:, 896:2048] = _dot(hb, w_ref[3])

    return _call(
        body, payload, name="in_fwd", grid=(T // TM,), when=_edges(T // TM), sem=("arbitrary",),
        in_specs=[_row_spec(TM, D_MODEL), _const_spec((1, D_MODEL)), _const_spec(w_in.shape)],
        out_specs=[_row_spec(TM, D_MODEL), _row_spec(TM, 3 * D_ATT), _row_spec(TM, 2 * D_SGU), _row_spec(TM, 2 * D_MODEL)],
        out_shape=[jax.ShapeDtypeStruct((T, D_MODEL), BF16), jax.ShapeDtypeStruct((T, 3 * D_ATT), BF16),
                   jax.ShapeDtypeStruct((T, 2 * D_SGU), F32), jax.ShapeDtypeStruct((T, 2 * D_MODEL), F32)],
        operands=(x, g, w_in))


def _in_dgrad(dx_res, x, g, w_in, dq, dk, dv, dzs, dgl):
    T = x.shape[0]

    def body(dxr_ref, x_ref, g_ref, w_ref, dq_ref, dk_ref, dv_ref, dzs_ref, dgl_ref, dx_ref, dz_ref, dg_ref):
        dz = jnp.concatenate([dq_ref[...], dk_ref[...].astype(BF16), dv_ref[...].astype(BF16), dzs_ref[...], dgl_ref[...]],
                             axis=1)
        dz_ref[...] = dz
        dh = jnp.zeros((TM, D_MODEL), F32)
        for s in range(N_SHARD):
            dh += _dot_nt(dz[:, s * IN_S:(s + 1) * IN_S], w_ref[s])
        gv = g_ref[...]
        xhat, r, _ = _rms_fwd(x_ref[...], gv)
        dxn, dg = _rms_bwd(dh, xhat, r, gv)
        dx_ref[...] = dxr_ref[...] + dxn

        @pl.when(pl.program_id(0) == 0)
        def _():
            dg_ref[...] = jnp.zeros_like(dg_ref)

        dg_ref[...] += dg

    pad_blocks = ATT_PAD // TM
    return pl.pallas_call(
        body, name="in_dgrad", grid=(T // TM,),
        in_specs=[_row_spec(TM, D_MODEL), _row_spec(TM, D_MODEL), _const_spec((1, D_MODEL)), _const_spec(w_in.shape),
                  _row_spec(TM, D_ATT), _row_spec(TM, D_ATT, pad_blocks), _row_spec(TM, D_ATT, pad_blocks),
                  _row_spec(TM, 2 * D_SGU), _row_spec(TM, 2 * D_MODEL)],
        out_specs=[_row_spec(TM, D_MODEL), _row_spec(TM, D_IN), _acc_spec((1, D_MODEL))],
        out_shape=[jax.ShapeDtypeStruct((T, D_MODEL), F32), jax.ShapeDtypeStruct((T, D_IN), BF16),
                   jax.ShapeDtypeStruct((1, D_MODEL), F32)],
        compiler_params=_params(("arbitrary",)),
    )(dx_res, x, g, w_in, dq, dk, dv, dzs, dgl)


def _in_wgrad(h, dz):
    T = h.shape[0]

    def body(h_ref, dz_ref, gw_ref):
        @pl.when(pl.program_id(1) == 0)
        def _():
            gw_ref[...] = jnp.zeros_like(gw_ref)

        gw_ref[0] += _dot_tn(h_ref[...], dz_ref[...])

    return pl.pallas_call(
        body, name="in_wgrad", grid=(N_SHARD, T // min(TW, T)),
        in_specs=[pl.BlockSpec((min(TW, T), D_MODEL), lambda s, i: (i, 0)),
                  pl.BlockSpec((min(TW, T), IN_S), lambda s, i: (i, s))],
        out_specs=pl.BlockSpec((1, D_MODEL, IN_S), lambda s, i: (s, 0, 0)),
        out_shape=jax.ShapeDtypeStruct((N_SHARD, D_MODEL, IN_S), F32),
        compiler_params=_params(("arbitrary", "arbitrary")),
    )(h, dz)


def _rel_onehot():
    r = lax.broadcasted_iota(jnp.int32, (REL_PAD, REL_PAD), 0)
    n = lax.broadcasted_iota(jnp.int32, (REL_PAD, REL_PAD), 1)
    idx = jnp.clip(BAND - 1 - n, -REL_CLIP, REL_CLIP) + REL_CLIP
    return jnp.where(r == idx, 1.0, 0.0).astype(BF16)


def _split3(v):
    p1 = v.astype(BF16)
    r1 = v - p1.astype(F32)
    p2 = r1.astype(BF16)
    p3 = (r1 - p2.astype(F32)).astype(BF16)
    return p1, p2, p3


def _relbias_fwd(tab_pad):
    def body(t_ref, o_ref):
        oh = _rel_onehot()
        acc = jnp.zeros((HEADS, REL_PAD), F32)
        for p in _split3(t_ref[...]):
            acc += _dot(p, oh)
        o_ref[...] = acc

    return pl.pallas_call(body, name="relbias_fwd", out_shape=jax.ShapeDtypeStruct((HEADS, REL_PAD), F32))(tab_pad)


def _relbias_bwd(z):
    def body(z_ref, o_ref):
        oh = _rel_onehot()
        dt2 = jnp.sum(z_ref[...], axis=1)
        acc = jnp.zeros((HEADS, REL_PAD), F32)
        for p in _split3(dt2):
            acc += _dot_nt(p, oh)
        o_ref[...] = acc

    return pl.pallas_call(body, name="relbias_bwd", out_shape=jax.ShapeDtypeStruct((HEADS, REL_PAD), F32))(z)


def _bias_blocks(t2):
    flat = jnp.tile(t2, (1, CHUNK))
    skew = flat[:, :CHUNK * (REL_PAD - 1)].reshape(HEADS, CHUNK, REL_PAD - 1)
    bias = skew[:, :, CHUNK - 1:CHUNK - 1 + BAND]
    slabs = [jnp.pad(bias, ((0, 0), (0, 0), (CHUNK * c, ATT_KEYS - BAND - CHUNK * c)), constant_values=NEG_INF)
             for c in range(2)]
    return jnp.concatenate(slabs, axis=1)


def _unskew(db2):
    out = []
    for c in range(2):
        slab = db2[:, CHUNK * c:CHUNK * (c + 1), CHUNK * c:CHUNK * c + BAND]
        y = jnp.pad(slab, ((0, 0), (0, 0), (CHUNK - 1, REL_PAD - BAND - CHUNK + 1)))
        yf = jnp.pad(y.reshape(HEADS, CHUNK * REL_PAD), ((0, 0), (0, CHUNK)))
        out.append(yf.reshape(HEADS, CHUNK, REL_PAD + 1)[:, :, :REL_PAD])
    return jnp.concatenate(out, axis=1)


def _att_load(qkv_hbm, q_s, k_s, v_s, sem, T):
    copies = [pltpu.make_async_copy(qkv_hbm.at[:, 0:D_ATT], q_s, sem.at[0]),
              pltpu.make_async_copy(qkv_hbm.at[:, D_ATT:2 * D_ATT], k_s.at[pl.ds(ATT_PAD, T), :], sem.at[1]),
              pltpu.make_async_copy(qkv_hbm.at[:, 2 * D_ATT:3 * D_ATT], v_s.at[pl.ds(ATT_PAD, T), :], sem.at[2])]
    for cp in copies:
        cp.start()
    k_s[0:ATT_PAD, :] = jnp.zeros((ATT_PAD, D_ATT), BF16)
    v_s[0:ATT_PAD, :] = jnp.zeros((ATT_PAD, D_ATT), BF16)
    for cp in copies:
        cp.wait()


def _head(v, h):
    return v[:, h * HEAD_DIM:(h + 1) * HEAD_DIM]


def _rows(v, h):
    return v[h * ATT_ROWS:(h + 1) * ATT_ROWS]


def _att_exp(qs, kw, bias_ref, valid):
    s = jnp.concatenate([_dot_nt(_head(qs, h), _head(kw, h)) + bias_ref[h] for h in range(HEADS)], axis=0)
    if valid is not None:
        s = jnp.where(valid, s, NEG_INF)
    e = jnp.exp(s - jnp.max(s, axis=-1, keepdims=True))
    return e, 1.0 / jnp.sum(e, axis=-1, keepdims=True)


def _att_blocks(T, block, keys_on_rows=False, middle=None):
    n_edge = min(ATT_PAD // ATT_ROWS, T // ATT_ROWS)
    shape, axis = ((ATT_KEYS, 1), 0) if keys_on_rows else ((1, ATT_KEYS), 1)

    def edge(i, carry):
        r0 = i * ATT_ROWS
        block(i, (lax.broadcasted_iota(jnp.int32, shape, axis) + (r0 - ATT_PAD)) >= 0)
        return carry

    def inner(i, carry):
        block(i, None)
        return carry

    n_blocks = T // ATT_ROWS
    lax.fori_loop(0, n_edge, edge, 0)
    if middle is None:
        lax.fori_loop(n_edge, n_blocks, inner, 0)
        return
    n_late = max(n_blocks - n_blocks // 4, n_edge)
    lax.fori_loop(n_edge, n_late, inner, 0)
    middle()
    lax.fori_loop(n_late, n_blocks, inner, 0)


def _att_fwd(qkv, bias2, payload=None):
    T = qkv.shape[0]

    def body(qkv_hbm, bias_ref, y_ref, q_s, k_s, v_s, sem, middle=None):
        _att_load(qkv_hbm, q_s, k_s, v_s, sem, T)

        def block(i, valid):
            r0 = pl.multiple_of(i * ATT_ROWS, ATT_ROWS)
            qs = q_s[pl.ds(r0, ATT_ROWS), :] * (HEAD_DIM ** -0.5)
            kw = k_s[pl.ds(r0, ATT_KEYS), :]
            vw = v_s[pl.ds(r0, ATT_KEYS), :]
            e, rinv = _att_exp(qs, kw, bias_ref, valid)
            eb = e.astype(BF16)
            outs = [_dot(_rows(eb, h), _head(vw, h)) * _rows(rinv, h) for h in range(HEADS)]
            y_ref[pl.ds(r0, ATT_ROWS), :] = jnp.concatenate(outs, axis=1).astype(BF16)

        _att_blocks(T, block, middle=middle)

    return _call(
        body, payload, name="att_fwd", grid=None, takes_middle=True,
        in_specs=[pl.BlockSpec(memory_space=pl.ANY), pl.BlockSpec(memory_space=pltpu.VMEM)],
        out_specs=[pl.BlockSpec(memory_space=pltpu.VMEM)],
        out_shape=[jax.ShapeDtypeStruct((T, D_ATT), BF16)],
        scratch_shapes=[pltpu.VMEM((T, D_ATT), BF16), pltpu.VMEM((T + ATT_PAD, D_ATT), BF16),
                        pltpu.VMEM((T + ATT_PAD, D_ATT), BF16), pltpu.SemaphoreType.DMA((3,))],
        operands=(qkv, bias2))


def _lanes(v, h):
    return v[:, h * ATT_ROWS:(h + 1) * ATT_ROWS]


def _att_bwd(qkv, dy, bias2t, payload=None):
    T = qkv.shape[0]

    def body(qkv_hbm, dy_ref, bias_ref, dq_ref, dk_ref, dv_ref, db_ref, q_s, k_s, v_s, sem):
        _att_load(qkv_hbm, q_s, k_s, v_s, sem, T)
        dk_ref[...] = jnp.zeros_like(dk_ref)
        dv_ref[...] = jnp.zeros_like(dv_ref)
        db_ref[...] = jnp.zeros_like(db_ref)

        def block(i, valid):
            r0 = pl.multiple_of(i * ATT_ROWS, ATT_ROWS)
            qs = q_s[pl.ds(r0, ATT_ROWS), :] * (HEAD_DIM ** -0.5)
            kw = k_s[pl.ds(r0, ATT_KEYS), :]
            vw = v_s[pl.ds(r0, ATT_KEYS), :]
            dyb = dy_ref[pl.ds(r0, ATT_ROWS), :]
            s = jnp.concatenate([_dot_nt(_head(kw, h), _head(qs, h)) + bias_ref[h] for h in range(HEADS)], axis=1)
            if valid is not None:
                s = jnp.where(valid, s, NEG_INF)
            e = jnp.exp(s - jnp.max(s, axis=0, keepdims=True))
            p = e * (1.0 / jnp.sum(e, axis=0, keepdims=True))
            dp = jnp.concatenate([_dot_nt(_head(vw, h), _head(dyb, h)) for h in range(HEADS)], axis=1)
            ds = p * (dp - jnp.sum(p * dp, axis=0, keepdims=True))
            for h in range(HEADS):
                db_ref[h] += _lanes(ds, h)
            dsb = ds.astype(BF16)
            pb = p.astype(BF16)
            dq = [_dot_tn(_lanes(dsb, h), _head(kw, h)) for h in range(HEADS)]
            dk = [_dot(_lanes(dsb, h), _head(qs, h)) for h in range(HEADS)]
            dv = [_dot(_lanes(pb, h), _head(dyb, h)) for h in range(HEADS)]
            dq_ref[pl.ds(r0, ATT_ROWS), :] = (jnp.concatenate(dq, axis=1) * (HEAD_DIM ** -0.5)).astype(BF16)
            dk_ref[pl.ds(r0, ATT_KEYS), :] += jnp.concatenate(dk, axis=1)
            dv_ref[pl.ds(r0, ATT_KEYS), :] += jnp.concatenate(dv, axis=1)

        _att_blocks(T, block, keys_on_rows=True)

    vmem = pl.BlockSpec(memory_space=pltpu.VMEM)
    return _call(
        body, payload, name="att_bwd", grid=None,
        in_specs=[pl.BlockSpec(memory_space=pl.ANY), vmem, vmem],
        out_specs=[vmem, vmem, vmem, vmem],
        out_shape=[jax.ShapeDtypeStruct((T, D_ATT), BF16), jax.ShapeDtypeStruct((T + ATT_PAD, D_ATT), F32),
                   jax.ShapeDtypeStruct((T + ATT_PAD, D_ATT), F32), jax.ShapeDtypeStruct((HEADS, ATT_KEYS, ATT_ROWS), F32)],
        scratch_shapes=[pltpu.VMEM((T, D_ATT), BF16), pltpu.VMEM((T + ATT_PAD, D_ATT), BF16),
                        pltpu.VMEM((T + ATT_PAD, D_ATT), BF16), pltpu.SemaphoreType.DMA((3,))],
        operands=(qkv, dy, bias2t))


_GELU_C = 0.7978845608028654
_GELU_A = 0.044715


def _gelu(x):
    t = jnp.tanh(_GELU_C * (x + _GELU_A * x * x * x))
    return 0.5 * x * (1.0 + t), t


def _gelu_grad(x, t):
    return 0.5 * (1.0 + t) + 0.5 * x * (1.0 - t * t) * _GELU_C * (1.0 + 3.0 * _GELU_A * x * x)


def _group_masks():
    col = lax.broadcasted_iota(jnp.int32, (SGU_GROUPS, D_SGU), 1) // SGU_GDIM
    grp = lax.broadcasted_iota(jnp.int32, (SGU_GROUPS, D_SGU), 0)
    return jnp.where(col == grp, 1.0, 0.0).astype(F32)


def _causal_mask(transposed=False):
    i = lax.broadcasted_iota(jnp.int32, (SGU_BLOCK, SGU_BLOCK), 0) // CHUNK
    j = lax.broadcasted_iota(jnp.int32, (SGU_BLOCK, SGU_BLOCK), 1) // CHUNK
    return (j >= i) if transposed else (i >= j)


def _sgu_norm(zs, lng, lnb):
    gz, t = _gelu(zs)
    u = gz[:, 0:D_SGU]
    vs = gz[:, D_SGU:2 * D_SGU]
    xc = vs - jnp.mean(vs, axis=-1, keepdims=True)
    rstd = lax.rsqrt(jnp.mean(xc * xc, axis=-1, keepdims=True) + EPS)
    xhat = xc * rstd
    return t, u, xhat, rstd, xhat * lng + lnb


def _sgu_mix(vn_blk, w_ref, bst, gm):
    mask = _causal_mask()
    s = jnp.zeros((SGU_BLOCK, D_SGU), F32)
    for g in range(SGU_GROUPS):
        wm = jnp.where(mask, w_ref[g], 0.0).astype(BF16)
        s += _dot(wm, (vn_blk * gm[g:g + 1, :]).astype(BF16))
        s += bst[:, g:g + 1] * gm[g:g + 1, :]
    return s


def _sgu_fwd(zs, lng, lnb, w_s, bst):
    T = zs.shape[0]
    nblk = TM // SGU_BLOCK

    def body(zs_ref, lng_ref, lnb_ref, w_ref, bst_ref, y_ref):
        _, u, _, _, vn = _sgu_norm(zs_ref[...], lng_ref[...], lnb_ref[...])
        gm = _group_masks()
        bst_v = bst_ref[...]
        for n in range(nblk):
            rows = slice(n * SGU_BLOCK, (n + 1) * SGU_BLOCK)
            s = _sgu_mix(vn[rows], w_ref, bst_v, gm)
            y_ref[rows, :] = (u[rows] * s).astype(BF16)

    return pl.pallas_call(
        body, name="sgu_fwd", grid=(T // TM,),
        in_specs=[_row_spec(TM, 2 * D_SGU), _const_spec((1, D_SGU)), _const_spec((1, D_SGU)),
                  _const_spec(w_s.shape), _const_spec(bst.shape)],
        out_specs=_row_spec(TM, D_SGU),
        out_shape=jax.ShapeDtypeStruct((T, D_SGU), BF16),
        compiler_params=_params(("arbitrary",)),
    )(zs, lng, lnb, w_s, bst)


def _sgu_bwd(zs, dy, lng, lnb, w_s, w_st, bst):
    T = zs.shape[0]
    nblk = TM // SGU_BLOCK

    def body(zs_ref, dy_ref, lng_ref, lnb_ref, w_ref, wt_ref, bst_ref, dzs_ref, dw_ref, dbt_ref, dlg_ref, dlb_ref):
        @pl.when(pl.program_id(0) == 0)
        def _():
            dw_ref[...] = jnp.zeros_like(dw_ref)
            dbt_ref[...] = jnp.zeros_like(dbt_ref)
            dlg_ref[...] = jnp.zeros_like(dlg_ref)
            dlb_ref[...] = jnp.zeros_like(dlb_ref)

        zs_v = zs_ref[...]
        lng_v = lng_ref[...]
        t, u, xhat, rstd, vn = _sgu_norm(zs_v, lng_v, lnb_ref[...])
        gm = _group_masks()
        bst_v = bst_ref[...]
        mask = _causal_mask()
        mask_t = _causal_mask(transposed=True)
        dyv = dy_ref[...].astype(F32)
        lane8 = lax.broadcasted_iota(jnp.int32, (1, SGU_GROUPS), 1)
        du_rows, dvn_rows = [], []
        for n in range(nblk):
            rows = slice(n * SGU_BLOCK, (n + 1) * SGU_BLOCK)
            vn_b = vn[rows]
            s = _sgu_mix(vn_b, w_ref, bst_v, gm)
            du_rows.append(dyv[rows] * s)
            dsb = dyv[rows] * u[rows]
            vnb16 = vn_b.astype(BF16)
            dvn = jnp.zeros((SGU_BLOCK, D_SGU), F32)
            dbt = jnp.zeros((SGU_BLOCK, SGU_GROUPS), F32)
            for g in range(SGU_GROUPS):
                dsg = dsb * gm[g:g + 1, :]
                dsg16 = dsg.astype(BF16)
                wmt = jnp.where(mask_t, wt_ref[g], 0.0).astype(BF16)
                dvn += _dot(wmt, dsg16)
                dw_ref[g] += jnp.where(mask, _dot_nt(dsg16, vnb16), 0.0)
                dbt += jnp.sum(dsg, axis=-1, keepdims=True) * jnp.where(lane8 == g, 1.0, 0.0)
            dbt_ref[...] += dbt
            dvn_rows.append(dvn)
        du = jnp.concatenate(du_rows, axis=0)
        dvn = jnp.concatenate(dvn_rows, axis=0)
        dlg_ref[...] += jnp.sum(dvn * xhat, axis=0, keepdims=True)
        dlb_ref[...] += jnp.sum(dvn, axis=0, keepdims=True)
        dxhat = dvn * lng_v
        dvs = rstd * (dxhat - jnp.mean(dxhat, axis=-1, keepdims=True)
                      - xhat * jnp.mean(dxhat * xhat, axis=-1, keepdims=True))
        dgz = jnp.concatenate([du, dvs], axis=1)
        dzs_ref[...] = (dgz * _gelu_grad(zs_v, t)).astype(BF16)

    return pl.pallas_call(
        body, name="sgu_bwd", grid=(T // TM,),
        in_specs=[_row_spec(TM, 2 * D_SGU), _row_spec(TM, D_SGU), _const_spec((1, D_SGU)), _const_spec((1, D_SGU)),
                  _const_spec(w_s.shape), _const_spec(w_st.shape), _const_spec(bst.shape)],
        out_specs=[_row_spec(TM, 2 * D_SGU), _acc_spec(w_s.shape), _acc_spec(bst.shape), _acc_spec((1, D_SGU)),
                   _acc_spec((1, D_SGU))],
        out_shape=[jax.ShapeDtypeStruct((T, 2 * D_SGU), BF16), jax.ShapeDtypeStruct(w_s.shape, F32),
                   jax.ShapeDtypeStruct(bst.shape, F32), jax.ShapeDtypeStruct((1, D_SGU), F32),
                   jax.ShapeDtypeStruct((1, D_SGU), F32)],
        compiler_params=_params(("arbitrary",)),
    )(zs, dy, lng, lnb, w_s, w_st, bst)


def _cols(v, s):
    return v[:, s * BR_S:(s + 1) * BR_S]


def _merge_fwd(x, y_att, y_sgu, gl, b_gate, wba, wbs, wo, payload=None):
    T = x.shape[0]

    def body(x_ref, ya_ref, ys_ref, gl_ref, bg_ref, wba_ref, wbs_ref, wo_ref, xo_ref, m_ref, pa_ref, ps_ref):
        ya = ya_ref[...]
        ys = ys_ref[...]
        pa = jnp.concatenate([_dot(ya, wba_ref[s]) for s in range(N_SHARD)], axis=1)
        ps = jnp.concatenate([_dot(ys, wbs_ref[s]) for s in range(N_SHARD)], axis=1)
        g = _sigmoid(gl_ref[...] + bg_ref[...])
        mb = (g[:, 0:D_MODEL] * pa + g[:, D_MODEL:2 * D_MODEL] * ps).astype(BF16)
        m_ref[...] = mb
        pa_ref[...] = pa.astype(BF16)
        ps_ref[...] = ps.astype(BF16)
        acc = jnp.zeros((TM, D_MODEL), F32)
        for s in range(N_SHARD):
            acc += _dot(_cols(mb, s), wo_ref[s])
        xo_ref[...] = x_ref[...] + acc

    tokd = jax.ShapeDtypeStruct((T, D_MODEL), BF16)
    return _call(
        body, payload, name="merge_fwd", grid=(T // TM,), when=_edges(T // TM), sem=("arbitrary",),
        in_specs=[_row_spec(TM, D_MODEL), _row_spec(TM, D_ATT), _row_spec(TM, D_SGU), _row_spec(TM, 2 * D_MODEL),
                  _const_spec((1, 2 * D_MODEL)), _const_spec(wba.shape), _const_spec(wbs.shape), _const_spec(wo.shape)],
        out_specs=[_row_spec(TM, D_MODEL)] * 4,
        out_shape=[jax.ShapeDtypeStruct((T, D_MODEL), F32), tokd, tokd, tokd],
        operands=(x, y_att, y_sgu, gl, b_gate, wba, wbs, wo))


def _merge_bwd(dx, y_att, y_sgu, gl, merged, pa, ps, b_gate, wba, wbs, wo, payload=None):
    T = dx.shape[0]

    def body(dx_ref, ya_ref, ys_ref, gl_ref, m_ref, pa_ref, ps_ref, bg_ref, wba_ref, wbs_ref, wo_ref,
             dya_ref, dys_ref, dgl_ref, dbg_ref, gwba_ref, gwbs_ref, gwo_ref):
        @pl.when(pl.program_id(0) == 0)
        def _():
            dbg_ref[...] = jnp.zeros_like(dbg_ref)
            gwba_ref[...] = jnp.zeros_like(gwba_ref)
            gwbs_ref[...] = jnp.zeros_like(gwbs_ref)
            gwo_ref[...] = jnp.zeros_like(gwo_ref)

        dxb = dx_ref[...].astype(BF16)
        dm = jnp.concatenate([_dot_nt(dxb, wo_ref[s]) for s in range(N_SHARD)], axis=1)
        g = _sigmoid(gl_ref[...] + bg_ref[...])
        ga = g[:, 0:D_MODEL]
        gs = g[:, D_MODEL:2 * D_MODEL]
        dpa = (dm * ga).astype(BF16)
        dps = (dm * gs).astype(BF16)
        dgl = jnp.concatenate([dm * pa_ref[...].astype(F32) * ga * (1.0 - ga),
                               dm * ps_ref[...].astype(F32) * gs * (1.0 - gs)], axis=1)
        dgl_ref[...] = dgl.astype(BF16)
        dbg_ref[...] += jnp.sum(dgl, axis=0, keepdims=True)
        ya = ya_ref[...]
        ys = ys_ref[...]
        mb = m_ref[...]
        dya = jnp.zeros((TM, D_ATT), F32)
        dys = jnp.zeros((TM, D_SGU), F32)
        for s in range(N_SHARD):
            dya += _dot_nt(_cols(dpa, s), wba_ref[s])
            dys += _dot_nt(_cols(dps, s), wbs_ref[s])
            gwo_ref[s] += _dot_tn(_cols(mb, s), dxb)
            gwba_ref[s] += _dot_tn(ya, _cols(dpa, s))
            gwbs_ref[s] += _dot_tn(ys, _cols(dps, s))
        dya_ref[...] = dya.astype(BF16)
        dys_ref[...] = dys.astype(BF16)

    return _call(
        body, payload, name="merge_bwd", grid=(T // TM,), when=_edges(T // TM), sem=("arbitrary",),
        operands=(dx, y_att, y_sgu, gl, merged, pa, ps, b_gate, wba, wbs, wo),
        in_specs=[_row_spec(TM, D_MODEL), _row_spec(TM, D_ATT), _row_spec(TM, D_SGU), _row_spec(TM, 2 * D_MODEL),
                  _row_spec(TM, D_MODEL), _row_spec(TM, D_MODEL), _row_spec(TM, D_MODEL),
                  _const_spec((1, 2 * D_MODEL)), _const_spec(wba.shape), _const_spec(wbs.shape), _const_spec(wo.shape)],
        out_specs=[_row_spec(TM, D_ATT), _row_spec(TM, D_SGU), _row_spec(TM, 2 * D_MODEL), _acc_spec((1, 2 * D_MODEL)),
                   _acc_spec(wba.shape), _acc_spec(wbs.shape), _acc_spec(wo.shape)],
        out_shape=[jax.ShapeDtypeStruct((T, D_ATT), BF16), jax.ShapeDtypeStruct((T, D_SGU), BF16),
                   jax.ShapeDtypeStruct((T, 2 * D_MODEL), BF16), jax.ShapeDtypeStruct((1, 2 * D_MODEL), F32),
                   jax.ShapeDtypeStruct(wba.shape, F32), jax.ShapeDtypeStruct(wbs.shape, F32),
                   jax.ShapeDtypeStruct(wo.shape, F32)])


def _loss_bwd(x, target, g):
    T = x.shape[0]

    def body(x_ref, t_ref, g_ref, dx_ref, loss_ref, dg_ref):
        @pl.when(pl.program_id(0) == 0)
        def _():
            loss_ref[...] = jnp.zeros_like(loss_ref)
            dg_ref[...] = jnp.zeros_like(dg_ref)

        gv = g_ref[...]
        xhat, r, y = _rms_fwd(x_ref[...], gv)
        err = y - t_ref[...]
        per_tok = jnp.mean(err * err, axis=-1, keepdims=True)
        loss_ref[...] += 0.5 * jnp.sum(per_tok, axis=0, keepdims=True)
        dxn, dg = _rms_bwd(err * (1.0 / D_MODEL), xhat, r, gv)
        dx_ref[...] = dxn
        dg_ref[...] += dg

    return pl.pallas_call(
        body, name="loss_bwd", grid=(T // TM,),
        in_specs=[_row_spec(TM, D_MODEL), _row_spec(TM, D_MODEL), _const_spec((1, D_MODEL))],
        out_specs=[_row_spec(TM, D_MODEL), _acc_spec((1, 128)), _acc_spec((1, D_MODEL))],
        out_shape=[jax.ShapeDtypeStruct((T, D_MODEL), F32), jax.ShapeDtypeStruct((1, 128), F32),
                   jax.ShapeDtypeStruct((1, D_MODEL), F32)],
        compiler_params=_params(("arbitrary",)),
    )(x, target, g)


BIG = ("ffn1_w_gate", "ffn1_w_up", "ffn1_w_down", "w_in", "w_branch_att", "w_branch_sgu", "w_out",
       "ffn2_w_gate", "ffn2_w_up", "ffn2_w_down")
SMALL = ("norm_ffn1", "norm_mix", "b_gate", "rel_bias", "sgu_ln_g", "sgu_ln_b", "sgu_w_s", "sgu_b_s", "norm_ffn2",
         "norm_final")


G_FFN1 = ("ffn1_w_gate", "ffn1_w_up", "ffn1_w_down")
G_MIX = ("w_in", "w_branch_att", "w_branch_sgu", "w_out")
G_FFN2 = ("ffn2_w_gate", "ffn2_w_up", "ffn2_w_down")


def _local_step(x, target, wb, ws, dist=None):
    def gather_on(names):
        return _ag_payload([wb[n] for n in names]) if dist else None

    t2 = _relbias_fwd(ws["rel_bias"])
    bias2 = _bias_blocks(t2)
    bst = ws["sgu_b_s"].T
    w_st = jnp.swapaxes(ws["sgu_w_s"], 1, 2)

    if dist:
        wb.update(zip(G_FFN1, _call(lambda: None, gather_on(G_FFN1), name="allgather_ffn1", grid=None, in_specs=[],
                                    out_specs=[], out_shape=[])))
    x1, h1, a1, b1, *got = _ffn_fwd(x, ws["norm_ffn1"], wb["ffn1_w_gate"], wb["ffn1_w_up"], wb["ffn1_w_down"],
                                    "ffn1_fwd", gather_on(G_MIX))
    wb.update(zip(G_MIX, got))
    h2, qkv, zs, gl, *got = _in_fwd(x1, ws["norm_mix"], wb["w_in"], gather_on(G_FFN2[0:1]))
    wb.update(zip(G_FFN2[0:1], got))
    y_att, *got = _att_fwd(qkv, bias2, gather_on(G_FFN2[1:2]))
    wb.update(zip(G_FFN2[1:2], got))
    y_sgu = _sgu_fwd(zs, ws["sgu_ln_g"], ws["sgu_ln_b"], ws["sgu_w_s"], bst)
    x2, merged, pa, ps, *got = _merge_fwd(x1, y_att, y_sgu, gl, ws["b_gate"], wb["w_branch_att"], wb["w_branch_sgu"],
                                          wb["w_out"], gather_on(G_FFN2[2:3]))
    wb.update(zip(G_FFN2[2:3], got))
    x3, h3, a3, b3 = _ffn_fwd(x2, ws["norm_ffn2"], wb["ffn2_w_gate"], wb["ffn2_w_up"], wb["ffn2_w_down"], "ffn2_fwd")
    dx3, loss, g_final = _loss_bwd(x3, target, ws["norm_final"])

    gb, gs, sums = {}, {"norm_final": g_final}, {}

    def pair_on(names, small=None):
        return _px_payload([gb[n] for n in names], small) if dist else None

    def pair_add(names, halves):
        for n, rv in zip(names, halves):
            sums[n] = _pair_add(gb[n], rv, dist[0], dist[1], "pair_add_" + n)

    def chips_on(names):
        return _cx_payload([sums[n][1] for n in names], [sums[n][2] for n in names]) if dist else None

    dx2, da3, db3, gs["norm_ffn2"] = _ffn_dgrad(dx3, x2, a3, b3, ws["norm_ffn2"], wb["ffn2_w_gate"], wb["ffn2_w_up"],
                                                wb["ffn2_w_down"], "ffn2_dgrad")
    gb["ffn2_w_gate"], gb["ffn2_w_up"], gb["ffn2_w_down"] = _ffn_wgrad(h3, dx3, a3, b3, da3, db3, "ffn2_wgrad")
    dy_att, dy_sgu, dgl, gs["b_gate"], gb["w_branch_att"], gb["w_branch_sgu"], gb["w_out"], *got = _merge_bwd(
        dx2, y_att, y_sgu, gl, merged, pa, ps, ws["b_gate"], wb["w_branch_att"], wb["w_branch_sgu"], wb["w_out"],
        pair_on(G_FFN2))
    pair_add(G_FFN2, got)
    dq, dk, dv, db2t, *lands2 = _att_bwd(qkv, dy_att, jnp.swapaxes(bias2, 1, 2), chips_on(G_FFN2))
    gs["rel_bias"] = _relbias_bwd(_unskew(jnp.swapaxes(db2t, 1, 2)))
    dzs, gs["sgu_w_s"], dbt, gs["sgu_ln_g"], gs["sgu_ln_b"] = _sgu_bwd(zs, dy_sgu, ws["sgu_ln_g"], ws["sgu_ln_b"],
                                                                      ws["sgu_w_s"], w_st, bst)
    gs["sgu_b_s"] = dbt.T
    dx1, dz, gs["norm_mix"] = _in_dgrad(dx2, x1, ws["norm_mix"], wb["w_in"], dq, dk, dv, dzs, dgl)
    gb["w_in"] = _in_wgrad(h2, dz)
    gx, da1, db1, gs["norm_ffn1"], *got = _ffn_dgrad(dx1, x, a1, b1, ws["norm_ffn1"], wb["ffn1_w_gate"],
                                                    wb["ffn1_w_up"], wb["ffn1_w_down"], "ffn1_dgrad", pair_on(G_MIX))
    pair_add(G_MIX, got)
    gb["ffn1_w_gate"], gb["ffn1_w_up"], gb["ffn1_w_down"], *lands_mix = _ffn_wgrad(h1, dx1, a1, b1, da1, db1,
                                                                                   "ffn1_wgrad", chips_on(G_MIX))
    if not dist:
        return loss, gx, gb, gs

    tail = _tail_reduce([gb[n] for n in G_FFN1], _pack_small(gs, loss))
    for i, n in enumerate(G_FFN1):
        sums[n] = (tail[i],)
    lands1, small_sums = tail[len(G_FFN1):2 * len(G_FFN1)], tail[-1]
    lands = dict(zip(G_FFN2 + G_MIX + G_FFN1, list(lands2) + list(lands_mix) + list(lands1)))
    fulls = [_final_sum(sums[n][0], lands[n], dist[1], dist[0], "final_sum_" + n) for n in BIG]
    return loss, gx, dict(zip(BIG, _sibling_share(fulls, "sibling_share"))), small_sums


_ANY = pl.BlockSpec(memory_space=pl.ANY)
_VMEM = pl.BlockSpec(memory_space=pltpu.VMEM)


def _mesh_pos():
    return lax.axis_index("x"), lax.axis_index("y"), lax.axis_index("c")


def _cast_slots(shards, chip, name):
    n = len(shards)
    r, ncol = shards[0].shape
    tr = r // 2

    def body(me_ref, *refs):
        for i_ref, o_ref in zip(refs[:n], refs[n:]):
            o_ref[0] = i_ref[...].astype(BF16)

    grid_spec = pltpu.PrefetchScalarGridSpec(
        num_scalar_prefetch=1, grid=(r // tr,),
        in_specs=[pl.BlockSpec((tr, ncol), lambda i, me: (i, 0))] * n,
        out_specs=[pl.BlockSpec((1, tr, ncol), lambda i, me: (me[0], i, 0))] * n)
    return pl.pallas_call(
        body, name=name, grid_spec=grid_spec,
        out_shape=[jax.ShapeDtypeStruct((N_SHARD, r, ncol), BF16)] * n,
        compiler_params=_params(("arbitrary",)),
    )(chip, *shards)


class _Payload:
    def __init__(self, arrays, out_shapes, aliases, scratch, phases):
        self.arrays = list(arrays)
        self.out_shapes = list(out_shapes)
        self.aliases = dict(aliases)
        self.scratch = list(scratch)
        self.phases = phases


def _remote(src, dst, ssem, rsem, dev):
    return pltpu.make_async_remote_copy(src_ref=src, dst_ref=dst, send_sem=ssem, recv_sem=rsem, device_id=dev,
                                        device_id_type=MESH)


def _call(body, payload, *, name, grid, in_specs, out_specs, out_shape, scratch_shapes=(), sem=None, when=None,
          operands=(), takes_middle=False):
    in_specs, out_specs, out_shape = list(in_specs), list(out_specs), list(out_shape)
    scratch_shapes = list(scratch_shapes)
    n_in, n_out, n_scr = len(in_specs), len(out_specs), len(scratch_shapes)
    kwargs = {}
    kernel = body
    if payload is not None:
        k_in, k_out = len(payload.arrays), len(payload.out_shapes)
        rank = len(grid) if grid else 0

        def kernel(*refs):
            a, b = n_in, n_in + k_in
            c, d = b + n_out, b + n_out + k_out
            e = d + n_scr
            phases = payload.phases(refs[a:b], refs[c:d], refs[e:])

            def run():
                body(*refs[:a], *refs[b:c], *refs[d:e])

            if not grid:
                phases[0]()
                if len(phases) == 3 and takes_middle:
                    body(*refs[:a], *refs[b:c], *refs[d:e], middle=phases[1])
                    phases[2]()
                    return
                run()
                for ph in phases[1:]:
                    ph()
                return
            step = pl.program_id(0)
            if rank == 2:
                step = step * grid[1] + pl.program_id(1)
            marks = list(when)
            if len(phases) == 3:
                marks = [when[0], (max(when[1][0] - 3, 0), False), when[1]]
            for ph, (at, before) in zip(phases, marks):
                if before:
                    pl.when(step == at)(ph)
            run()
            for ph, (at, before) in zip(phases, marks):
                if not before:
                    pl.when(step == at)(ph)

        in_specs += [_ANY] * k_in
        out_specs += [_ANY] * k_out
        out_shape += payload.out_shapes
        scratch_shapes += payload.scratch
        kwargs["input_output_aliases"] = {n_in + i: n_out + j for i, j in payload.aliases.items()}
        operands = tuple(operands) + tuple(payload.arrays)
    if grid:
        kwargs["grid"] = grid
    return pl.pallas_call(kernel, name=name, in_specs=in_specs, out_specs=out_specs, out_shape=out_shape,
                          scratch_shapes=scratch_shapes, compiler_params=_params(sem), **kwargs)(*operands)


def _ag_payload(slots):
    n = len(slots)

    def phases(_, refs, sems):
        send_i, recv_i, send_d, recv_d = sems
        x, y, c = _mesh_pos()
        me = 2 * x + y

        def half(w, core):
            rh = slots[w].shape[1] // 2
            return pl.ds(core * rh, rh)

        def ici(w, j):
            t = (me + 1 + j) % N_SHARD
            mine = refs[w].at[me, half(w, c), :]
            return _remote(mine, mine, send_i.at[3 * w + j], recv_i.at[3 * w + j], (t // 2, t % 2, c))

        def d2d(w, j, core):
            s = (me + 3 - j) % N_SHARD
            land = refs[w].at[s, half(w, core), :]
            return _remote(land, land, send_d.at[3 * w + j], recv_d.at[3 * w + j], (x, y, 1 - c))

        def start():
            for w in range(n):
                for j in range(3):
                    ici(w, j).start()

        def forward():
            for w in range(n):
                for j in range(3):
                    s = (me + 3 - j) % N_SHARD
                    land = refs[w].at[s, half(w, c), :]
                    _remote(land, land, send_i.at[3 * w + j], recv_i.at[3 * w + j], (x, y, c)).wait_recv()
                    d2d(w, j, c).start()

        def finish():
            for w in range(n):
                for j in range(3):
                    d2d(w, j, 1 - c).wait_recv()
            for w in range(n):
                for j in range(3):
                    ici(w, j).wait_send()
                    d2d(w, j, c).wait_send()

        return [start, forward, finish]

    return _Payload(slots, [jax.ShapeDtypeStruct(s.shape, s.dtype) for s in slots], {i: i for i in range(n)},
                    [pltpu.SemaphoreType.DMA((3 * n,)) for _ in range(4)], phases)


def _px_payload(grads, small=None):
    arrays = list(grads) + ([small] if small is not None else [])
    n = len(arrays)

    def phases(ins, outs, sems):
        send, recv = sems
        x, y, c = _mesh_pos()

        def copy(w):
            if w < len(grads):
                rh = grads[w].shape[1] // 2
                src = ins[w].at[:, pl.ds((1 - c) * rh, rh), :]
            else:
                src = ins[w]
            return _remote(src, outs[w], send.at[w], recv.at[w], (x, y, 1 - c))

        def start():
            for w in range(n):
                copy(w).start()

        def finish():
            for w in range(n):
                copy(w).wait()

        return [start, finish]

    out_shapes = [jax.ShapeDtypeStruct((N_SHARD, g.shape[1] // 2, g.shape[2]), F32) for g in grads]
    if small is not None:
        out_shapes.append(jax.ShapeDtypeStruct(small.shape, F32))
    return _Payload(arrays, out_shapes, {}, [pltpu.SemaphoreType.DMA((n,)), pltpu.SemaphoreType.DMA((n,))], phases)


def _cx_payload(pbs, lands):
    n = len(pbs)

    def phases(ins, outs, sems):
        send, recv = sems
        x, y, c = _mesh_pos()
        me = 2 * x + y

        def copy(w, j):
            t = (me + 1 + j) % N_SHARD
            return _remote(ins[w].at[t], outs[w].at[me], send.at[3 * w + j], recv.at[3 * w + j], (t // 2, t % 2, c))

        def start():
            for w in range(n):
                for j in range(3):
                    copy(w, j).start()

        def finish():
            for w in range(n):
                for j in range(3):
                    copy(w, j).wait()

        return [start, finish]

    return _Payload(list(pbs) + list(lands), [jax.ShapeDtypeStruct(p.shape, BF16) for p in lands],
                    {n + i: i for i in range(n)},
                    [pltpu.SemaphoreType.DMA((3 * n,)), pltpu.SemaphoreType.DMA((3 * n,))], phases)


def _pair_add(g, rv, core, chip, name):
    _, r, ncol = g.shape
    rh = r // 2

    def body(c_ref, me_ref, g_ref, rv_ref, pf_ref, pb_ref, land_ref):
        s = g_ref[0] + rv_ref[0]
        sb = s.astype(BF16)
        pb_ref[0] = sb

        @pl.when(pl.program_id(0) == me_ref[0])
        def _():
            pf_ref[...] = s
            land_ref[0] = sb

    slot = pl.BlockSpec((1, rh, ncol), lambda s, c, me: (s, 0, 0))
    grid_spec = pltpu.PrefetchScalarGridSpec(
        num_scalar_prefetch=2, grid=(N_SHARD,),
        in_specs=[pl.BlockSpec((1, rh, ncol), lambda s, c, me: (s, c[0], 0)), slot],
        out_specs=[pl.BlockSpec((rh, ncol), lambda s, c, me: (0, 0)), slot,
                   pl.BlockSpec((1, rh, ncol), lambda s, c, me: (me[0], 0, 0))])
    return pl.pallas_call(
        body, name=name, grid_spec=grid_spec,
        out_shape=[jax.ShapeDtypeStruct((rh, ncol), F32), jax.ShapeDtypeStruct((N_SHARD, rh, ncol), BF16),
                   jax.ShapeDtypeStruct((N_SHARD, rh, ncol), BF16)],
        compiler_params=_params(("arbitrary",)),
    )(core, chip, g, rv)


def _tail_reduce(grads, small):
    n = len(grads)
    _, r, ncol = grads[0].shape
    rh = r // 2

    def body(*refs):
        g_hbm, sm = refs[:n], refs[n]
        pf, land, sm_out = refs[n + 1:2 * n + 1], refs[2 * n + 1:3 * n + 1], refs[3 * n + 1]
        scr = refs[3 * n + 2:]
        rv, mine, sendb = scr[:n], scr[n:2 * n], scr[2 * n:3 * n]
        sm_rv, sm_sum, d_send, d_recv, load, i_send, i_recv, store = scr[3 * n:]
        x, y, c = _mesh_pos()
        me = 2 * x + y
        sib = (x, y, 1 - c)

        def pair(w):
            src = g_hbm[w].at[:, pl.ds((1 - c) * rh, rh), :] if w < n else sm
            return _remote(src, rv[w] if w < n else sm_rv, d_send.at[w], d_recv.at[w], sib)

        def chips(w, j):
            t = (me + 1 + j) % N_SHARD
            src = sendb[w].at[t] if w < n else sm_sum
            dst = land[w].at[me] if w < n else sm_out.at[me]
            return _remote(src, dst, i_send.at[3 * w + j], i_recv.at[3 * w + j], (t // 2, t % 2, c))

        loads = [pltpu.make_async_copy(g_hbm[w].at[:, pl.ds(c * rh, rh), :], mine[w], load.at[w]) for w in range(n)]
        for w in range(n + 1):
            pair(w).start()
        for cp in loads:
            cp.start()
        stores = []
        for w in range(n):
            loads[w].wait()
            pair(w).wait_recv()
            for k in range(N_SHARD):
                s = mine[w][k] + rv[w][k]
                mine[w][k] = s
                sendb[w][k] = s.astype(BF16)
            stores += [pltpu.make_async_copy(mine[w].at[me], pf[w], store.at[2 * w]),
                       pltpu.make_async_copy(sendb[w].at[me], land[w].at[me], store.at[2 * w + 1])]
            for cp in stores[-2:]:
                cp.start()
            for j in range(3):
                chips(w, j).start()
        pair(n).wait_recv()
        sm_sum[...] = sm[...] + sm_rv[...]
        stores.append(pltpu.make_async_copy(sm_sum, sm_out.at[me], store.at[2 * n]))
        stores[-1].start()
        for j in range(3):
            chips(n, j).start()
        for w in range(n + 1):
            pair(w).wait_send()
            for j in range(3):
                chips(w, j).wait()
        for cp in stores:
            cp.wait()

    half = (N_SHARD, rh, ncol)
    return _call(
        body, None, name="tail_reduce", grid=None,
        in_specs=[_ANY] * n + [_VMEM], out_specs=[_ANY] * (2 * n + 1),
        out_shape=([jax.ShapeDtypeStruct((rh, ncol), F32)] * n + [jax.ShapeDtypeStruct(half, BF16)] * n
                   + [jax.ShapeDtypeStruct((N_SHARD,) + small.shape, F32)]),
        scratch_shapes=([pltpu.VMEM(half, F32)] * (2 * n) + [pltpu.VMEM(half, BF16)] * n
                        + [pltpu.VMEM(small.shape, F32), pltpu.VMEM(small.shape, F32),
                           pltpu.SemaphoreType.DMA((n + 1,)), pltpu.SemaphoreType.DMA((n + 1,)),
                           pltpu.SemaphoreType.DMA((n,)), pltpu.SemaphoreType.DMA((3 * n + 3,)),
                           pltpu.SemaphoreType.DMA((3 * n + 3,)), pltpu.SemaphoreType.DMA((2 * n + 1,))]),
        operands=(*grads, small))


def _final_sum(pf, land, chip, core, name):
    _, rh, ncol = land.shape

    def body(me_ref, c_ref, pf_ref, land_ref, o_ref):
        me = me_ref[0]
        acc = jnp.zeros((rh, ncol), F32)
        for k in range(N_SHARD):
            acc = acc + jnp.where(me == k, pf_ref[...], land_ref[k].astype(F32))
        o_ref[...] = acc

    grid_spec = pltpu.PrefetchScalarGridSpec(
        num_scalar_prefetch=2, grid=(1,),
        in_specs=[pl.BlockSpec((rh, ncol), lambda i, me, c: (0, 0)),
                  pl.BlockSpec((N_SHARD, rh, ncol), lambda i, me, c: (0, 0, 0))],
        out_specs=pl.BlockSpec((rh, ncol), lambda i, me, c: (c[0], 0)))
    return pl.pallas_call(
        body, name=name, grid_spec=grid_spec, out_shape=jax.ShapeDtypeStruct((2 * rh, ncol), F32),
        compiler_params=_params(("arbitrary",)),
    )(chip, core, pf, land)


def _sibling_share(fulls, name):
    n = len(fulls)

    def body(*refs):
        outs = refs[n:2 * n]
        send, recv = refs[2 * n:]
        x, y, c = _mesh_pos()
        cps = []
        for w in range(n):
            rh = fulls[w].shape[0] // 2
            mine = outs[w].at[pl.ds(c * rh, rh), :]
            cp = pltpu.make_async_remote_copy(src_ref=mine, dst_ref=mine, send_sem=send.at[w], recv_sem=recv.at[w],
                                              device_id=(x, y, 1 - c), device_id_type=MESH)
            cp.start()
            cps.append(cp)
        for cp in cps:
            cp.wait()

    return pl.pallas_call(
        body, name=name,
        in_specs=[_ANY] * n, out_specs=[_ANY] * n,
        out_shape=[jax.ShapeDtypeStruct(f.shape, F32) for f in fulls],
        input_output_aliases={i: i for i in range(n)},
        scratch_shapes=[pltpu.SemaphoreType.DMA((n,)), pltpu.SemaphoreType.DMA((n,))],
    )(*fulls)


_ROW = {"rel_bias": 128, "sgu_b_s": 136, "norm_ffn1": 144, "norm_mix": 145, "norm_ffn2": 146, "norm_final": 147,
        "b_gate": 148, "sgu_ln_g": 150, "sgu_ln_b": 151}


def _pack_small(gs, loss):
    def body(ws, rel, bs, n1, nm, n2, nf, bg, lg, lb, loss_ref, o_ref):
        o_ref[...] = jnp.zeros_like(o_ref)
        o_ref[LOSS_ROW:LOSS_ROW + 1, 0:128] = loss_ref[...]
        for g in range(SGU_GROUPS):
            o_ref[0:SGU_BLOCK, g * SGU_BLOCK:(g + 1) * SGU_BLOCK] = ws[g]
        o_ref[128:136, 0:REL_PAD] = rel[...]
        o_ref[136:144, 0:SGU_BLOCK] = bs[...]
        o_ref[144:145, :] = n1[...]
        o_ref[145:146, :] = nm[...]
        o_ref[146:147, :] = n2[...]
        o_ref[147:148, :] = nf[...]
        o_ref[148:149, :] = bg[:, 0:D_MODEL]
        o_ref[149:150, :] = bg[:, D_MODEL:2 * D_MODEL]
        o_ref[150:151, 0:D_SGU] = lg[...]
        o_ref[151:152, 0:D_SGU] = lb[...]

    order = ("sgu_w_s", "rel_bias", "sgu_b_s", "norm_ffn1", "norm_mix", "norm_ffn2", "norm_final", "b_gate", "sgu_ln_g",
             "sgu_ln_b")
    return pl.pallas_call(body, name="pack_small", out_shape=jax.ShapeDtypeStruct((SMALL_ROWS, D_MODEL), F32))(
        *[gs[k] for k in order], loss)


def _adam(w, g, m, v):
    m2 = ADAM_B1 * m + (1.0 - ADAM_B1) * g
    v2 = ADAM_B2 * v + (1.0 - ADAM_B2) * (g * g)
    m_hat = m2 / (1.0 - ADAM_B1 ** ADAM_STEP)
    v_hat = v2 / (1.0 - ADAM_B2 ** ADAM_STEP)
    delta = -ADAM_LR * (m_hat / (jnp.sqrt(v_hat) + ADAM_EPS) + ADAM_WD * w)
    return delta, m2, v2


def _adam_small(sin, w, m, v):
    names = SMALL
    k = len(names)

    def body(*refs):
        sin_ref = refs[0]
        w_r, m_r, v_r = refs[1:1 + k], refs[1 + k:1 + 2 * k], refs[1 + 2 * k:1 + 3 * k]
        outs = refs[1 + 3 * k:]
        tot = sin_ref[0] + sin_ref[1] + sin_ref[2] + sin_ref[3]
        outs[4 * k][...] = tot[LOSS_ROW:LOSS_ROW + 1, 0:128]
        for i, name in enumerate(names):
            o = outs[4 * i:4 * i + 4]
            if name == "sgu_w_s":
                for gi in range(SGU_GROUPS):
                    g = tot[0:SGU_BLOCK, gi * SGU_BLOCK:(gi + 1) * SGU_BLOCK]
                    res = (g,) + _adam(w_r[i][gi], g, m_r[i][gi], v_r[i][gi])
                    for ref, val in zip(o, res):
                        ref[gi] = val
                continue
            r0 = _ROW[name]
            if name == "rel_bias":
                g = tot[r0:r0 + HEADS, 0:REL_PAD]
            elif name == "sgu_b_s":
                g = tot[r0:r0 + SGU_GROUPS, 0:SGU_BLOCK]
            elif name == "b_gate":
                g = jnp.concatenate([tot[r0:r0 + 1, :], tot[r0 + 1:r0 + 2, :]], axis=1)
            elif name in ("sgu_ln_g", "sgu_ln_b"):
                g = tot[r0:r0 + 1, 0:D_SGU]
            else:
                g = tot[r0:r0 + 1, :]
            res = (g,) + _adam(w_r[i][...], g, m_r[i][...], v_r[i][...])
            for ref, val in zip(o, res):
                ref[...] = val

    out_shape = []
    for name in names:
        out_shape += [jax.ShapeDtypeStruct(w[name].shape, F32)] * 4
    out_shape.append(jax.ShapeDtypeStruct((1, 128), F32))
    flat = pl.pallas_call(body, name="adam_small", out_shape=out_shape, compiler_params=_params())(
        sin, *[w[n] for n in names], *[m[n] for n in names], *[v[n] for n in names])
    return {name: tuple(flat[4 * i:4 * i + 4]) for i, name in enumerate(names)}, flat[4 * k]


def _adam_big(w, g, m, v, name):
    r, ncol = w.shape
    tr = 256 if r % 256 == 0 else r // 2

    def body(w_ref, g_ref, m_ref, v_ref, g2_ref, d_ref, m2_ref, v2_ref):
        gv = g_ref[...]
        g2_ref[...] = gv
        d_ref[...], m2_ref[...], v2_ref[...] = _adam(w_ref[...], gv, m_ref[...], v_ref[...])

    spec = pl.BlockSpec((tr, ncol), lambda i: (i, 0))
    return pl.pallas_call(
        body, name=name, grid=(r // tr,), in_specs=[spec] * 4, out_specs=[spec] * 4,
        out_shape=[jax.ShapeDtypeStruct(w.shape, F32)] * 4, compiler_params=_params(("arbitrary",)),
    )(w, g, m, v)


WEIGHTS = ("norm_ffn1", "ffn1_w_gate", "ffn1_w_up", "ffn1_w_down", "norm_mix", "w_in", "b_gate", "rel_bias", "sgu_ln_g",
           "sgu_ln_b", "sgu_w_s", "sgu_b_s", "w_branch_att", "w_branch_sgu", "w_out", "norm_ffn2", "ffn2_w_gate",
           "ffn2_w_up", "ffn2_w_down", "norm_final")


GATE_UP = ("ffn1_w_gate", "ffn1_w_up", "ffn2_w_gate", "ffn2_w_up")
_FFN = ("ffn1_w_gate", "ffn1_w_up", "ffn1_w_down", "ffn2_w_gate", "ffn2_w_up", "ffn2_w_down")
_CAST_GROUPS = ((_FFN, "cast_ffn"), (("w_in",), "cast_w_in"), (("w_branch_att", "w_branch_sgu"), "cast_branch"),
                (("w_out",), "cast_w_out"))


def _big_form(name, a):
    return jnp.swapaxes(a, 1, 2)[0] if name in GATE_UP else a[0]


def _big_back(name, a):
    return jnp.swapaxes(a[None], 1, 2) if name in GATE_UP else a[None]


def _small_form(name, a):
    if name == "norm_final":
        return a.reshape(1, D_MODEL)
    if name == "rel_bias":
        return jnp.pad(a[0], ((0, 0), (0, REL_PAD - N_REL)))
    if name in ("sgu_w_s", "sgu_b_s"):
        return a[0]
    return a


def _small_back(name, a, like):
    if name == "rel_bias":
        a = a[:, :N_REL]
    return a.reshape(like.shape)


def kernel(x, norm_ffn1, ffn1_w_gate, ffn1_w_up, ffn1_w_down, norm_mix, w_in, b_gate, rel_bias, sgu_ln_g, sgu_ln_b, sgu_w_s, sgu_b_s, w_branch_att, w_branch_sgu, w_out, norm_ffn2, ffn2_w_gate, ffn2_w_up, ffn2_w_down, norm_final, loss_target, m_norm_ffn1, m_ffn1_w_gate, m_ffn1_w_up, m_ffn1_w_down, m_norm_mix, m_w_in, m_b_gate, m_rel_bias, m_sgu_ln_g, m_sgu_ln_b, m_sgu_w_s, m_sgu_b_s, m_w_branch_att, m_w_branch_sgu, m_w_out, m_norm_ffn2, m_ffn2_w_gate, m_ffn2_w_up, m_ffn2_w_down, m_norm_final, v_norm_ffn1, v_ffn1_w_gate, v_ffn1_w_up, v_ffn1_w_down, v_norm_mix, v_w_in, v_b_gate, v_rel_bias, v_sgu_ln_g, v_sgu_ln_b, v_sgu_w_s, v_sgu_b_s, v_w_branch_att, v_w_branch_sgu, v_w_out, v_norm_ffn2, v_ffn2_w_gate, v_ffn2_w_up, v_ffn2_w_down, v_norm_final):
    w = dict(norm_ffn1=norm_ffn1, ffn1_w_gate=ffn1_w_gate, ffn1_w_up=ffn1_w_up, ffn1_w_down=ffn1_w_down, norm_mix=norm_mix,
             w_in=w_in, b_gate=b_gate, rel_bias=rel_bias, sgu_ln_g=sgu_ln_g, sgu_ln_b=sgu_ln_b, sgu_w_s=sgu_w_s,
             sgu_b_s=sgu_b_s, w_branch_att=w_branch_att, w_branch_sgu=w_branch_sgu, w_out=w_out, norm_ffn2=norm_ffn2,
             ffn2_w_gate=ffn2_w_gate, ffn2_w_up=ffn2_w_up, ffn2_w_down=ffn2_w_down, norm_final=norm_final)
    m = dict(norm_ffn1=m_norm_ffn1, ffn1_w_gate=m_ffn1_w_gate, ffn1_w_up=m_ffn1_w_up, ffn1_w_down=m_ffn1_w_down,
             norm_mix=m_norm_mix, w_in=m_w_in, b_gate=m_b_gate, rel_bias=m_rel_bias, sgu_ln_g=m_sgu_ln_g,
             sgu_ln_b=m_sgu_ln_b, sgu_w_s=m_sgu_w_s, sgu_b_s=m_sgu_b_s, w_branch_att=m_w_branch_att,
             w_branch_sgu=m_w_branch_sgu, w_out=m_w_out, norm_ffn2=m_norm_ffn2, ffn2_w_gate=m_ffn2_w_gate,
             ffn2_w_up=m_ffn2_w_up, ffn2_w_down=m_ffn2_w_down, norm_final=m_norm_final)
    v = dict(norm_ffn1=v_norm_ffn1, ffn1_w_gate=v_ffn1_w_gate, ffn1_w_up=v_ffn1_w_up, ffn1_w_down=v_ffn1_w_down,
             norm_mix=v_norm_mix, w_in=v_w_in, b_gate=v_b_gate, rel_bias=v_rel_bias, sgu_ln_g=v_sgu_ln_g,
             sgu_ln_b=v_sgu_ln_b, sgu_w_s=v_sgu_w_s, sgu_b_s=v_sgu_b_s, w_branch_att=v_w_branch_att,
             w_branch_sgu=v_w_branch_sgu, w_out=v_w_out, norm_ffn2=v_norm_ffn2, ffn2_w_gate=v_ffn2_w_gate,
             ffn2_w_up=v_ffn2_w_up, ffn2_w_down=v_ffn2_w_down, norm_final=v_norm_final)

    core = lax.axis_index("c").astype(jnp.int32).reshape(1)
    chip = (2 * lax.axis_index("x") + lax.axis_index("y")).astype(jnp.int32).reshape(1)

    wk = {n: _big_form(n, w[n]) for n in BIG}
    slots = {}
    for names, call in _CAST_GROUPS:
        slots.update(zip(names, _cast_slots([wk[n] for n in names], chip, call)))
    ws = {n: _small_form(n, w[n]) for n in SMALL}
    _, gx, shard_grads, small_sums = _local_step(x[0], loss_target[0], slots, ws, (core, chip))

    small, loss = _adam_small(small_sums, ws, {n: _small_form(n, m[n]) for n in SMALL},
                              {n: _small_form(n, v[n]) for n in SMALL})
    grad, delta, new_m, new_v = {}, {}, {}, {}
    for n in SMALL:
        grad[n], delta[n], new_m[n], new_v[n] = (_small_back(n, a, w[n]) for a in small[n])
    for n in BIG:
        g2, d2, m2, v2 = _adam_big(wk[n], shard_grads[n], _big_form(n, m[n]), _big_form(n, v[n]), "adam_" + n)
        grad[n], delta[n], new_m[n], new_v[n] = (_big_back(n, a) for a in (g2, d2, m2, v2))

    return (loss[0, 0], gx.reshape(x.shape), *[grad[n] for n in WEIGHTS], *[delta[n] for n in WEIGHTS],
            *[new_m[n] for n in WEIGHTS], *[new_v[n] for n in WEIGHTS])
```

```python
import functools

import jax
import jax.numpy as jnp
from jax import lax
from jax.experimental import pallas as pl
from jax.experimental.pallas import tpu as pltpu

F32 = jnp.float32
BF16 = jnp.bfloat16

D_MODEL = 1024
N_SHARD = 4
D_FF = 2816
FF_S = D_FF // N_SHARD
D_ATT = 512
D_SGU = 512
D_IN = 3 * D_ATT + 2 * D_SGU + 2 * D_MODEL
IN_S = D_IN // N_SHARD
BR_S = D_MODEL // N_SHARD
HEADS = 8
HEAD_DIM = 64
CHUNK = 64
N_LEFT = 8
BAND = (N_LEFT + 1) * CHUNK
REL_CLIP = 256
N_REL = 2 * REL_CLIP + 1
REL_PAD = 640
SGU_BLOCK = 128
SGU_GROUPS = 8
SGU_GDIM = 64
EPS = 1e-6
NEG_INF = -1e30

ATT_ROWS = 2 * CHUNK
ATT_KEYS = BAND + CHUNK
ATT_PAD = N_LEFT * CHUNK

ADAM_LR = 0.001
ADAM_B1 = 0.9
ADAM_B2 = 0.999
ADAM_EPS = 1e-08
ADAM_WD = 0.01
ADAM_STEP = 10

TM = 256
TW = 1024
DGRAD_ROWS = 64
VMEM_LIMIT = 56 * 1024 * 1024

SMALL_ROWS = 160
LOSS_ROW = 152
MESH = pl.DeviceIdType.MESH

_NT = (((1,), (1,)), ((), ()))
_TN = (((0,), (0,)), ((), ()))


def _params(sem=None):
    return pltpu.CompilerParams(dimension_semantics=sem, vmem_limit_bytes=VMEM_LIMIT)


def _const_spec(shape):
    nd = len(shape)
    return pl.BlockSpec(shape, lambda *_: (0,) * nd, pipeline_mode=pl.Buffered(1))


def _acc_spec(shape):
    nd = len(shape)
    return pl.BlockSpec(shape, lambda *_: (0,) * nd)


def _row_spec(tm, ncols, off=0):
    return pl.BlockSpec((tm, ncols), lambda i: (i + off, 0))


def _row3_spec(tm, ncols):
    return pl.BlockSpec((N_SHARD, tm, ncols), lambda i: (0, i, 0))


def _dot(a, b):
    return jnp.dot(a, b, preferred_element_type=F32)


def _dot_nt(a, b):
    return lax.dot_general(a, b, _NT, preferred_element_type=F32)


def _dot_tn(a, b):
    return lax.dot_general(a, b, _TN, preferred_element_type=F32)


def _rms_fwd(x, g):
    r = lax.rsqrt(jnp.mean(x * x, axis=-1, keepdims=True) + EPS)
    xhat = x * r
    return xhat, r, xhat * g


def _rms_bwd(dh, xhat, r, g):
    dxhat = dh * g
    dx = r * (dxhat - xhat * jnp.mean(dxhat * xhat, axis=-1, keepdims=True))
    dg = jnp.sum(dh * xhat, axis=0, keepdims=True)
    return dx, dg


def _sigmoid(x):
    return 1.0 / (1.0 + jnp.exp(-x))


def _edges(n_steps):
    return [(0, True), (n_steps - 1, False)]


def _ffn_fwd(x, g, wg, wu, wd, name, payload=None):
    T = x.shape[0]

    def body(x_ref, g_ref, wg_ref, wu_ref, wd_ref, xo_ref, h_ref, a_ref, b_ref):
        xv = x_ref[...]
        hb = _rms_fwd(xv, g_ref[...])[2].astype(BF16)
        h_ref[...] = hb
        acc = jnp.zeros((TM, D_MODEL), F32)
        for s in range(N_SHARD):
            a = _dot_nt(hb, wg_ref[s])
            b = _dot_nt(hb, wu_ref[s])
            a_ref[s] = a.astype(BF16)
            b_ref[s] = b.astype(BF16)
            sv = a * _sigmoid(a) * b
            acc += _dot(sv.astype(BF16), wd_ref[s])
        xo_ref[...] = xv + 0.5 * acc

    return _call(
        body, payload, name=name, grid=(T // TM,), when=_edges(T // TM), sem=("arbitrary",),
        in_specs=[_row_spec(TM, D_MODEL), _const_spec((1, D_MODEL)), _const_spec(wg.shape), _const_spec(wu.shape),
                  _const_spec(wd.shape)],
        out_specs=[_row_spec(TM, D_MODEL), _row_spec(TM, D_MODEL), _row3_spec(TM, FF_S), _row3_spec(TM, FF_S)],
        out_shape=[jax.ShapeDtypeStruct((T, D_MODEL), F32), jax.ShapeDtypeStruct((T, D_MODEL), BF16),
                   jax.ShapeDtypeStruct((N_SHARD, T, FF_S), BF16), jax.ShapeDtypeStruct((N_SHARD, T, FF_S), BF16)],
        operands=(x, g, wg, wu, wd))


def _ffn_dgrad(dout, x, a, b, g, wg, wu, wd, name, payload=None):
    T = x.shape[0]

    def body(do_ref, x_ref, a_ref, b_ref, g_ref, wg_ref, wu_ref, wd_ref, dx_ref, da_ref, db_ref, dg_ref):
        do = do_ref[...]
        dob = (0.5 * do).astype(BF16)
        dh = jnp.zeros((TM, D_MODEL), F32)
        ds_next = _dot_nt(dob, wd_ref[0])
        for s in range(N_SHARD):
            ds = ds_next
            if s + 1 < N_SHARD:
                ds_next = _dot_nt(dob, wd_ref[s + 1])
            for r0 in range(0, TM, DGRAD_ROWS):
                rows = slice(r0, r0 + DGRAD_ROWS)
                av = a_ref[s, rows, :].astype(F32)
                bv = b_ref[s, rows, :].astype(F32)
                sig = _sigmoid(av)
                dsr = ds[rows]
                da_ref[s, rows, :] = (dsr * bv * (sig * (1.0 + av * (1.0 - sig)))).astype(BF16)
                db_ref[s, rows, :] = (dsr * (av * sig)).astype(BF16)
            dh += _dot(da_ref[s], wg_ref[s]) + _dot(db_ref[s], wu_ref[s])
        gv = g_ref[...]
        xhat, r, _ = _rms_fwd(x_ref[...], gv)
        dxn, dg = _rms_bwd(dh, xhat, r, gv)
        dx_ref[...] = do + dxn

        @pl.when(pl.program_id(0) == 0)
        def _():
            dg_ref[...] = jnp.zeros_like(dg_ref)

        dg_ref[...] += dg

    return _call(
        body, payload, name=name, grid=(T // TM,), when=_edges(T // TM), sem=("arbitrary",),
        in_specs=[_row_spec(TM, D_MODEL), _row_spec(TM, D_MODEL), _row3_spec(TM, FF_S), _row3_spec(TM, FF_S),
                  _const_spec((1, D_MODEL)), _const_spec(wg.shape), _const_spec(wu.shape), _const_spec(wd.shape)],
        out_specs=[_row_spec(TM, D_MODEL), _row3_spec(TM, FF_S), _row3_spec(TM, FF_S), _acc_spec((1, D_MODEL))],
        out_shape=[jax.ShapeDtypeStruct((T, D_MODEL), F32), jax.ShapeDtypeStruct((N_SHARD, T, FF_S), BF16),
                   jax.ShapeDtypeStruct((N_SHARD, T, FF_S), BF16), jax.ShapeDtypeStruct((1, D_MODEL), F32)],
        operands=(dout, x, a, b, g, wg, wu, wd))


def _ffn_wgrad(h, dout, a, b, da, db, name, payload=None):
    T = h.shape[0]

    def body(h_ref, do_ref, a_ref, b_ref, da_ref, db_ref, gwg_ref, gwu_ref, gwd_ref):
        @pl.when(pl.program_id(1) == 0)
        def _():
            gwg_ref[...] = jnp.zeros_like(gwg_ref)
            gwu_ref[...] = jnp.zeros_like(gwu_ref)
            gwd_ref[...] = jnp.zeros_like(gwd_ref)

        hv = h_ref[...]
        dob = do_ref[...].astype(BF16)
        av = a_ref[0].astype(F32)
        sv = (0.5 * av * _sigmoid(av) * b_ref[0].astype(F32)).astype(BF16)
        gwg_ref[0] += _dot_tn(da_ref[0], hv)
        gwu_ref[0] += _dot_tn(db_ref[0], hv)
        gwd_ref[0] += _dot_tn(sv, dob)

    tw = min(TW, T)
    tok = pl.BlockSpec((tw, D_MODEL), lambda s, i: (i, 0))
    act = pl.BlockSpec((1, tw, FF_S), lambda s, i: (s, i, 0))
    return _call(
        body, payload, name=name, grid=(N_SHARD, T // tw), when=_edges(N_SHARD * (T // tw)),
        sem=("arbitrary", "arbitrary"),
        in_specs=[tok, tok, act, act, act, act],
        out_specs=[pl.BlockSpec((1, FF_S, D_MODEL), lambda s, i: (s, 0, 0))] * 3,
        out_shape=[jax.ShapeDtypeStruct((N_SHARD, FF_S, D_MODEL), F32)] * 3,
        operands=(h, dout, a, b, da, db))


def _in_fwd(x, g, w_in, payload=None):
    T = x.shape[0]

    def body(x_ref, g_ref, w_ref, h_ref, qkv_ref, zs_ref, gl_ref):
        hb = _rms_fwd(x_ref[...], g_ref[...])[2].astype(BF16)
        h_ref[...] = hb
        z0 = _dot(hb, w_ref[0])
        qkv_ref[:, 0:IN_S] = z0.astype(BF16)
        z1 = _dot(hb, w_ref[1])
        qkv_ref[:, IN_S:3 * D_ATT] = z1[:, 0:384].astype(BF16)
        zs_ref[:, 0:768] = z1[:, 384:IN_S]
        z2 = _dot(hb, w_ref[2])
        zs_ref[:, 768:1024] = z2[:, 0:256]
        gl_ref[:, 0:896] = z2[:, 256:IN_S]
        gl_ref[:, 896:2048] = _dot(hb, w_ref[3])

    return _call(
        body, payload, name="in_fwd", grid=(T // TM,), when=_edges(T // TM), sem=("arbitrary",),
        in_specs=[_row_spec(TM, D_MODEL), _const_spec((1, D_MODEL)), _const_spec(w_in.shape)],
        out_specs=[_row_spec(TM, D_MODEL), _row_spec(TM, 3 * D_ATT), _row_spec(TM, 2 * D_SGU), _row_spec(TM, 2 * D_MODEL)],
        out_shape=[jax.ShapeDtypeStruct((T, D_MODEL), BF16), jax.ShapeDtypeStruct((T, 3 * D_ATT), BF16),
                   jax.ShapeDtypeStruct((T, 2 * D_SGU), F32), jax.ShapeDtypeStruct((T, 2 * D_MODEL), F32)],
        operands=(x, g, w_in))


def _in_dgrad(dx_res, x, g, w_in, dq, dk, dv, dzs, dgl):
    T = x.shape[0]

    def body(dxr_ref, x_ref, g_ref, w_ref, dq_ref, dk_ref, dv_ref, dzs_ref, dgl_ref, dx_ref, dz_ref, dg_ref):
        dz = jnp.concatenate([dq_ref[...], dk_ref[...].astype(BF16), dv_ref[...].astype(BF16), dzs_ref[...], dgl_ref[...]],
                             axis=1)
        dz_ref[...] = dz
        dh = jnp.zeros((TM, D_MODEL), F32)
        for s in range(N_SHARD):
            dh += _dot_nt(dz[:, s * IN_S:(s + 1) * IN_S], w_ref[s])
        gv = g_ref[...]
        xhat, r, _ = _rms_fwd(x_ref[...], gv)
        dxn, dg = _rms_bwd(dh, xhat, r, gv)
        dx_ref[...] = dxr_ref[...] + dxn

        @pl.when(pl.program_id(0) == 0)
        def _():
            dg_ref[...] = jnp.zeros_like(dg_ref)

        dg_ref[...] += dg

    pad_blocks = ATT_PAD // TM
    return pl.pallas_call(
        body, name="in_dgrad", grid=(T // TM,),
        in_specs=[_row_spec(TM, D_MODEL), _row_spec(TM, D_MODEL), _const_spec((1, D_MODEL)), _const_spec(w_in.shape),
                  _row_spec(TM, D_ATT), _row_spec(TM, D_ATT, pad_blocks), _row_spec(TM, D_ATT, pad_blocks),
                  _row_spec(TM, 2 * D_SGU), _row_spec(TM, 2 * D_MODEL)],
        out_specs=[_row_spec(TM, D_MODEL), _row_spec(TM, D_IN), _acc_spec((1, D_MODEL))],
        out_shape=[jax.ShapeDtypeStruct((T, D_MODEL), F32), jax.ShapeDtypeStruct((T, D_IN), BF16),
                   jax.ShapeDtypeStruct((1, D_MODEL), F32)],
        compiler_params=_params(("arbitrary",)),
    )(dx_res, x, g, w_in, dq, dk, dv, dzs, dgl)


def _in_wgrad(h, dz):
    T = h.shape[0]

    def body(h_ref, dz_ref, gw_ref):
        @pl.when(pl.program_id(1) == 0)
        def _():
            gw_ref[...] = jnp.zeros_like(gw_ref)

        gw_ref[0] += _dot_tn(h_ref[...], dz_ref[...])

    return pl.pallas_call(
        body, name="in_wgrad", grid=(N_SHARD, T // min(TW, T)),
        in_specs=[pl.BlockSpec((min(TW, T), D_MODEL), lambda s, i: (i, 0)),
                  pl.BlockSpec((min(TW, T), IN_S), lambda s, i: (i, s))],
        out_specs=pl.BlockSpec((1, D_MODEL, IN_S), lambda s, i: (s, 0, 0)),
        out_shape=jax.ShapeDtypeStruct((N_SHARD, D_MODEL, IN_S), F32),
        compiler_params=_params(("arbitrary", "arbitrary")),
    )(h, dz)


def _rel_onehot():
    r = lax.broadcasted_iota(jnp.int32, (REL_PAD, REL_PAD), 0)
    n = lax.broadcasted_iota(jnp.int32, (REL_PAD, REL_PAD), 1)
    idx = jnp.clip(BAND - 1 - n, -REL_CLIP, REL_CLIP) + REL_CLIP
    return jnp.where(r == idx, 1.0, 0.0).astype(BF16)


def _split3(v):
    p1 = v.astype(BF16)
    r1 = v - p1.astype(F32)
    p2 = r1.astype(BF16)
    p3 = (r1 - p2.astype(F32)).astype(BF16)
    return p1, p2, p3


def _relbias_fwd(tab_pad):
    def body(t_ref, o_ref):
        oh = _rel_onehot()
        acc = jnp.zeros((HEADS, REL_PAD), F32)
        for p in _split3(t_ref[...]):
            acc += _dot(p, oh)
        o_ref[...] = acc

    return pl.pallas_call(body, name="relbias_fwd", out_shape=jax.ShapeDtypeStruct((HEADS, REL_PAD), F32))(tab_pad)


def _relbias_bwd(z):
    def body(z_ref, o_ref):
        oh = _rel_onehot()
        dt2 = jnp.sum(z_ref[...], axis=1)
        acc = jnp.zeros((HEADS, REL_PAD), F32)
        for p in _split3(dt2):
            acc += _dot_nt(p, oh)
        o_ref[...] = acc

    return pl.pallas_call(body, name="relbias_bwd", out_shape=jax.ShapeDtypeStruct((HEADS, REL_PAD), F32))(z)


def _bias_blocks(t2):
    flat = jnp.tile(t2, (1, CHUNK))
    skew = flat[:, :CHUNK * (REL_PAD - 1)].reshape(HEADS, CHUNK, REL_PAD - 1)
    bias = skew[:, :, CHUNK - 1:CHUNK - 1 + BAND]
    slabs = [jnp.pad(bias, ((0, 0), (0, 0), (CHUNK * c, ATT_KEYS - BAND - CHUNK * c)), constant_values=NEG_INF)
             for c in range(2)]
    return jnp.concatenate(slabs, axis=1)


def _unskew(db2):
    out = []
    for c in range(2):
        slab = db2[:, CHUNK * c:CHUNK * (c + 1), CHUNK * c:CHUNK * c + BAND]
        y = jnp.pad(slab, ((0, 0), (0, 0), (CHUNK - 1, REL_PAD - BAND - CHUNK + 1)))
        yf = jnp.pad(y.reshape(HEADS, CHUNK * REL_PAD), ((0, 0), (0, CHUNK)))
        out.append(yf.reshape(HEADS, CHUNK, REL_PAD + 1)[:, :, :REL_PAD])
    return jnp.concatenate(out, axis=1)


def _att_load(qkv_hbm, q_s, k_s, v_s, sem, T):
    copies = [pltpu.make_async_copy(qkv_hbm.at[:, 0:D_ATT], q_s, sem.at[0]),
              pltpu.make_async_copy(qkv_hbm.at[:, D_ATT:2 * D_ATT], k_s.at[pl.ds(ATT_PAD, T), :], sem.at[1]),
              pltpu.make_async_copy(qkv_hbm.at[:, 2 * D_ATT:3 * D_ATT], v_s.at[pl.ds(ATT_PAD, T), :], sem.at[2])]
    for cp in copies:
        cp.start()
    k_s[0:ATT_PAD, :] = jnp.zeros((ATT_PAD, D_ATT), BF16)
    v_s[0:ATT_PAD, :] = jnp.zeros((ATT_PAD, D_ATT), BF16)
    for cp in copies:
        cp.wait()


def _head(v, h):
    return v[:, h * HEAD_DIM:(h + 1) * HEAD_DIM]


def _rows(v, h):
    return v[h * ATT_ROWS:(h + 1) * ATT_ROWS]


def _att_exp(qs, kw, bias_ref, valid):
    s = jnp.concatenate([_dot_nt(_head(qs, h), _head(kw, h)) + bias_ref[h] for h in range(HEADS)], axis=0)
    if valid is not None:
        s = jnp.where(valid, s, NEG_INF)
    e = jnp.exp(s - jnp.max(s, axis=-1, keepdims=True))
    return e, 1.0 / jnp.sum(e, axis=-1, keepdims=True)


def _att_blocks(T, block, keys_on_rows=False, middle=None):
    n_edge = min(ATT_PAD // ATT_ROWS, T // ATT_ROWS)
    shape, axis = ((ATT_KEYS, 1), 0) if keys_on_rows else ((1, ATT_KEYS), 1)

    def edge(i, carry):
        r0 = i * ATT_ROWS
        block(i, (lax.broadcasted_iota(jnp.int32, shape, axis) + (r0 - ATT_PAD)) >= 0)
        return carry

    def inner(i, carry):
        block(i, None)
        return carry

    n_blocks = T // ATT_ROWS
    lax.fori_loop(0, n_edge, edge, 0)
    if middle is None:
        lax.fori_loop(n_edge, n_blocks, inner, 0)
        return
    n_late = max(n_blocks - n_blocks // 4, n_edge)
    lax.fori_loop(n_edge, n_late, inner, 0)
    middle()
    lax.fori_loop(n_late, n_blocks, inner, 0)


def _att_fwd(qkv, bias2, payload=None):
    T = qkv.shape[0]

    def body(qkv_hbm, bias_ref, y_ref, q_s, k_s, v_s, sem, middle=None):
        _att_load(qkv_hbm, q_s, k_s, v_s, sem, T)

        def block(i, valid):
            r0 = pl.multiple_of(i * ATT_ROWS, ATT_ROWS)
            qs = q_s[pl.ds(r0, ATT_ROWS), :] * (HEAD_DIM ** -0.5)
            kw = k_s[pl.ds(r0, ATT_KEYS), :]
            vw = v_s[pl.ds(r0, ATT_KEYS), :]
            e, rinv = _att_exp(qs, kw, bias_ref, valid)
            eb = e.astype(BF16)
            outs = [_dot(_rows(eb, h), _head(vw, h)) * _rows(rinv, h) for h in range(HEADS)]
            y_ref[pl.ds(r0, ATT_ROWS), :] = jnp.concatenate(outs, axis=1).astype(BF16)

        _att_blocks(T, block, middle=middle)

    return _call(
        body, payload, name="att_fwd", grid=None, takes_middle=True,
        in_specs=[pl.BlockSpec(memory_space=pl.ANY), pl.BlockSpec(memory_space=pltpu.VMEM)],
        out_specs=[pl.BlockSpec(memory_space=pltpu.VMEM)],
        out_shape=[jax.ShapeDtypeStruct((T, D_ATT), BF16)],
        scratch_shapes=[pltpu.VMEM((T, D_ATT), BF16), pltpu.VMEM((T + ATT_PAD, D_ATT), BF16),
                        pltpu.VMEM((T + ATT_PAD, D_ATT), BF16), pltpu.SemaphoreType.DMA((3,))],
        operands=(qkv, bias2))


def _lanes(v, h):
    return v[:, h * ATT_ROWS:(h + 1) * ATT_ROWS]


def _att_bwd(qkv, dy, bias2t, payload=None):
    T = qkv.shape[0]

    def body(qkv_hbm, dy_ref, bias_ref, dq_ref, dk_ref, dv_ref, db_ref, q_s, k_s, v_s, sem):
        _att_load(qkv_hbm, q_s, k_s, v_s, sem, T)
        dk_ref[...] = jnp.zeros_like(dk_ref)
        dv_ref[...] = jnp.zeros_like(dv_ref)
        db_ref[...] = jnp.zeros_like(db_ref)

        def block(i, valid):
            r0 = pl.multiple_of(i * ATT_ROWS, ATT_ROWS)
            qs = q_s[pl.ds(r0, ATT_ROWS), :] * (HEAD_DIM ** -0.5)
            kw = k_s[pl.ds(r0, ATT_KEYS), :]
            vw = v_s[pl.ds(r0, ATT_KEYS), :]
            dyb = dy_ref[pl.ds(r0, ATT_ROWS), :]
            s = jnp.concatenate([_dot_nt(_head(kw, h), _head(qs, h)) + bias_ref[h] for h in range(HEADS)], axis=1)
            if valid is not None:
                s = jnp.where(valid, s, NEG_INF)
            e = jnp.exp(s - jnp.max(s, axis=0, keepdims=True))
            p = e * (1.0 / jnp.sum(e, axis=0, keepdims=True))
            dp = jnp.concatenate([_dot_nt(_head(vw, h), _head(dyb, h)) for h in range(HEADS)], axis=1)
            ds = p * (dp - jnp.sum(p * dp, axis=0, keepdims=True))
            for h in range(HEADS):
                db_ref[h] += _lanes(ds, h)
            dsb = ds.astype(BF16)
            pb = p.astype(BF16)
            dq = [_dot_tn(_lanes(dsb, h), _head(kw, h)) for h in range(HEADS)]
            dk = [_dot(_lanes(dsb, h), _head(qs, h)) for h in range(HEADS)]
            dv = [_dot(_lanes(pb, h), _head(dyb, h)) for h in range(HEADS)]
            dq_ref[pl.ds(r0, ATT_ROWS), :] = (jnp.concatenate(dq, axis=1) * (HEAD_DIM ** -0.5)).astype(BF16)
            dk_ref[pl.ds(r0, ATT_KEYS), :] += jnp.concatenate(dk, axis=1)
            dv_ref[pl.ds(r0, ATT_KEYS), :] += jnp.concatenate(dv, axis=1)

        _att_blocks(T, block, keys_on_rows=True)

    vmem = pl.BlockSpec(memory_space=pltpu.VMEM)
    return _call(
        body, payload, name="att_bwd", grid=None,
        in_specs=[pl.BlockSpec(memory_space=pl.ANY), vmem, vmem],
        out_specs=[vmem, vmem, vmem, vmem],
        out_shape=[jax.ShapeDtypeStruct((T, D_ATT), BF16), jax.ShapeDtypeStruct((T + ATT_PAD, D_ATT), F32),
                   jax.ShapeDtypeStruct((T + ATT_PAD, D_ATT), F32), jax.ShapeDtypeStruct((HEADS, ATT_KEYS, ATT_ROWS), F32)],
        scratch_shapes=[pltpu.VMEM((T, D_ATT), BF16), pltpu.VMEM((T + ATT_PAD, D_ATT), BF16),
                        pltpu.VMEM((T + ATT_PAD, D_ATT), BF16), pltpu.SemaphoreType.DMA((3,))],
        operands=(qkv, dy, bias2t))


_GELU_C = 0.7978845608028654
_GELU_A = 0.044715


def _gelu(x):
    t = jnp.tanh(_GELU_C * (x + _GELU_A * x * x * x))
    return 0.5 * x * (1.0 + t), t


def _gelu_grad(x, t):
    return 0.5 * (1.0 + t) + 0.5 * x * (1.0 - t * t) * _GELU_C * (1.0 + 3.0 * _GELU_A * x * x)


def _group_masks():
    col = lax.broadcasted_iota(jnp.int32, (SGU_GROUPS, D_SGU), 1) // SGU_GDIM
    grp = lax.broadcasted_iota(jnp.int32, (SGU_GROUPS, D_SGU), 0)
    return jnp.where(col == grp, 1.0, 0.0).astype(F32)


def _causal_mask(transposed=False):
    i = lax.broadcasted_iota(jnp.int32, (SGU_BLOCK, SGU_BLOCK), 0) // CHUNK
    j = lax.broadcasted_iota(jnp.int32, (SGU_BLOCK, SGU_BLOCK), 1) // CHUNK
    return (j >= i) if transposed else (i >= j)


def _sgu_norm(zs, lng, lnb):
    gz, t = _gelu(zs)
    u = gz[:, 0:D_SGU]
    vs = gz[:, D_SGU:2 * D_SGU]
    xc = vs - jnp.mean(vs, axis=-1, keepdims=True)
    rstd = lax.rsqrt(jnp.mean(xc * xc, axis=-1, keepdims=True) + EPS)
    xhat = xc * rstd
    return t, u, xhat, rstd, xhat * lng + lnb


def _sgu_mix(vn_blk, w_ref, bst, gm):
    mask = _causal_mask()
    s = jnp.zeros((SGU_BLOCK, D_SGU), F32)
    for g in range(SGU_GROUPS):
        wm = jnp.where(mask, w_ref[g], 0.0).astype(BF16)
        s += _dot(wm, (vn_blk * gm[g:g + 1, :]).astype(BF16))
        s += bst[:, g:g + 1] * gm[g:g + 1, :]
    return s


def _sgu_fwd(zs, lng, lnb, w_s, bst):
    T = zs.shape[0]
    nblk = TM // SGU_BLOCK

    def body(zs_ref, lng_ref, lnb_ref, w_ref, bst_ref, y_ref):
        _, u, _, _, vn = _sgu_norm(zs_ref[...], lng_ref[...], lnb_ref[...])
        gm = _group_masks()
        bst_v = bst_ref[...]
        for n in range(nblk):
            rows = slice(n * SGU_BLOCK, (n + 1) * SGU_BLOCK)
            s = _sgu_mix(vn[rows], w_ref, bst_v, gm)
            y_ref[rows, :] = (u[rows] * s).astype(BF16)

    return pl.pallas_call(
        body, name="sgu_fwd", grid=(T // TM,),
        in_specs=[_row_spec(TM, 2 * D_SGU), _const_spec((1, D_SGU)), _const_spec((1, D_SGU)),
                  _const_spec(w_s.shape), _const_spec(bst.shape)],
        out_specs=_row_spec(TM, D_SGU),
        out_shape=jax.ShapeDtypeStruct((T, D_SGU), BF16),
        compiler_params=_params(("arbitrary",)),
    )(zs, lng, lnb, w_s, bst)


def _sgu_bwd(zs, dy, lng, lnb, w_s, w_st, bst):
    T = zs.shape[0]
    nblk = TM // SGU_BLOCK

    def body(zs_ref, dy_ref, lng_ref, lnb_ref, w_ref, wt_ref, bst_ref, dzs_ref, dw_ref, dbt_ref, dlg_ref, dlb_ref):
        @pl.when(pl.program_id(0) == 0)
        def _():
            dw_ref[...] = jnp.zeros_like(dw_ref)
            dbt_ref[...] = jnp.zeros_like(dbt_ref)
            dlg_ref[...] = jnp.zeros_like(dlg_ref)
            dlb_ref[...] = jnp.zeros_like(dlb_ref)

        zs_v = zs_ref[...]
        lng_v = lng_ref[...]
        t, u, xhat, rstd, vn = _sgu_norm(zs_v, lng_v, lnb_ref[...])
        gm = _group_masks()
        bst_v = bst_ref[...]
        mask = _causal_mask()
        mask_t = _causal_mask(transposed=True)
        dyv = dy_ref[...].astype(F32)
        lane8 = lax.broadcasted_iota(jnp.int32, (1, SGU_GROUPS), 1)
        du_rows, dvn_rows = [], []
        for n in range(nblk):
            rows = slice(n * SGU_BLOCK, (n + 1) * SGU_BLOCK)
            vn_b = vn[rows]
            s = _sgu_mix(vn_b, w_ref, bst_v, gm)
            du_rows.append(dyv[rows] * s)
            dsb = dyv[rows] * u[rows]
            vnb16 = vn_b.astype(BF16)
            dvn = jnp.zeros((SGU_BLOCK, D_SGU), F32)
            dbt = jnp.zeros((SGU_BLOCK, SGU_GROUPS), F32)
            for g in range(SGU_GROUPS):
                dsg = dsb * gm[g:g + 1, :]
                dsg16 = dsg.astype(BF16)
                wmt = jnp.where(mask_t, wt_ref[g], 0.0).astype(BF16)
                dvn += _dot(wmt, dsg16)
                dw_ref[g] += jnp.where(mask, _dot_nt(dsg16, vnb16), 0.0)
                dbt += jnp.sum(dsg, axis=-1, keepdims=True) * jnp.where(lane8 == g, 1.0, 0.0)
            dbt_ref[...] += dbt
            dvn_rows.append(dvn)
        du = jnp.concatenate(du_rows, axis=0)
        dvn = jnp.concatenate(dvn_rows, axis=0)
        dlg_ref[...] += jnp.sum(dvn * xhat, axis=0, keepdims=True)
        dlb_ref[...] += jnp.sum(dvn, axis=0, keepdims=True)
        dxhat = dvn * lng_v
        dvs = rstd * (dxhat - jnp.mean(dxhat, axis=-1, keepdims=True)
                      - xhat * jnp.mean(dxhat * xhat, axis=-1, keepdims=True))
        dgz = jnp.concatenate([du, dvs], axis=1)
        dzs_ref[...] = (dgz * _gelu_grad(zs_v, t)).astype(BF16)

    return pl.pallas_call(
        body, name="sgu_bwd", grid=(T // TM,),
        in_specs=[_row_spec(TM, 2 * D_SGU), _row_spec(TM, D_SGU), _const_spec((1, D_SGU)), _const_spec((1, D_SGU)),
                  _const_spec(w_s.shape), _const_spec(w_st.shape), _const_spec(bst.shape)],
        out_specs=[_row_spec(TM, 2 * D_SGU), _acc_spec(w_s.shape), _acc_spec(bst.shape), _acc_spec((1, D_SGU)),
                   _acc_spec((1, D_SGU))],
        out_shape=[jax.ShapeDtypeStruct((T, 2 * D_SGU), BF16), jax.ShapeDtypeStruct(w_s.shape, F32),
                   jax.ShapeDtypeStruct(bst.shape, F32), jax.ShapeDtypeStruct((1, D_SGU), F32),
                   jax.ShapeDtypeStruct((1, D_SGU), F32)],
        compiler_params=_params(("arbitrary",)),
    )(zs, dy, lng, lnb, w_s, w_st, bst)


def _cols(v, s):
    return v[:, s * BR_S:(s + 1) * BR_S]


def _merge_fwd(x, y_att, y_sgu, gl, b_gate, wba, wbs, wo, payload=None):
    T = x.shape[0]

    def body(x_ref, ya_ref, ys_ref, gl_ref, bg_ref, wba_ref, wbs_ref, wo_ref, xo_ref, m_ref, pa_ref, ps_ref):
        ya = ya_ref[...]
        ys = ys_ref[...]
        pa = jnp.concatenate([_dot(ya, wba_ref[s]) for s in range(N_SHARD)], axis=1)
        ps = jnp.concatenate([_dot(ys, wbs_ref[s]) for s in range(N_SHARD)], axis=1)
        g = _sigmoid(gl_ref[...] + bg_ref[...])
        mb = (g[:, 0:D_MODEL] * pa + g[:, D_MODEL:2 * D_MODEL] * ps).astype(BF16)
        m_ref[...] = mb
        pa_ref[...] = pa.astype(BF16)
        ps_ref[...] = ps.astype(BF16)
        acc = jnp.zeros((TM, D_MODEL), F32)
        for s in range(N_SHARD):
            acc += _dot(_cols(mb, s), wo_ref[s])
        xo_ref[...] = x_ref[...] + acc

    tokd = jax.ShapeDtypeStruct((T, D_MODEL), BF16)
    return _call(
        body, payload, name="merge_fwd", grid=(T // TM,), when=_edges(T // TM), sem=("arbitrary",),
        in_specs=[_row_spec(TM, D_MODEL), _row_spec(TM, D_ATT), _row_spec(TM, D_SGU), _row_spec(TM, 2 * D_MODEL),
                  _const_spec((1, 2 * D_MODEL)), _const_spec(wba.shape), _const_spec(wbs.shape), _const_spec(wo.shape)],
        out_specs=[_row_spec(TM, D_MODEL)] * 4,
        out_shape=[jax.ShapeDtypeStruct((T, D_MODEL), F32), tokd, tokd, tokd],
        operands=(x, y_att, y_sgu, gl, b_gate, wba, wbs, wo))


def _merge_bwd(dx, y_att, y_sgu, gl, merged, pa, ps, b_gate, wba, wbs, wo, payload=None):
    T = dx.shape[0]

    def body(dx_ref, ya_ref, ys_ref, gl_ref, m_ref, pa_ref, ps_ref, bg_ref, wba_ref, wbs_ref, wo_ref,
             dya_ref, dys_ref, dgl_ref, dbg_ref, gwba_ref, gwbs_ref, gwo_ref):
        @pl.when(pl.program_id(0) == 0)
        def _():
            dbg_ref[...] = jnp.zeros_like(dbg_ref)
            gwba_ref[...] = jnp.zeros_like(gwba_ref)
            gwbs_ref[...] = jnp.zeros_like(gwbs_ref)
            gwo_ref[...] = jnp.zeros_like(gwo_ref)

        dxb = dx_ref[...].astype(BF16)
        dm = jnp.concatenate([_dot_nt(dxb, wo_ref[s]) for s in range(N_SHARD)], axis=1)
        g = _sigmoid(gl_ref[...] + bg_ref[...])
        ga = g[:, 0:D_MODEL]
        gs = g[:, D_MODEL:2 * D_MODEL]
        dpa = (dm * ga).astype(BF16)
        dps = (dm * gs).astype(BF16)
        dgl = jnp.concatenate([dm * pa_ref[...].astype(F32) * ga * (1.0 - ga),
                               dm * ps_ref[...].astype(F32) * gs * (1.0 - gs)], axis=1)
        dgl_ref[...] = dgl.astype(BF16)
        dbg_ref[...] += jnp.sum(dgl, axis=0, keepdims=True)
        ya = ya_ref[...]
        ys = ys_ref[...]
        mb = m_ref[...]
        dya = jnp.zeros((TM, D_ATT), F32)
        dys = jnp.zeros((TM, D_SGU), F32)
        for s in range(N_SHARD):
            dya += _dot_nt(_cols(dpa, s), wba_ref[s])
            dys += _dot_nt(_cols(dps, s), wbs_ref[s])
            gwo_ref[s] += _dot_tn(_cols(mb, s), dxb)
            gwba_ref[s] += _dot_tn(ya, _cols(dpa, s))
            gwbs_ref[s] += _dot_tn(ys, _cols(dps, s))
        dya_ref[...] = dya.astype(BF16)
        dys_ref[...] = dys.astype(BF16)

    return _call(
        body, payload, name="merge_bwd", grid=(T // TM,), when=_edges(T // TM), sem=("arbitrary",),
        operands=(dx, y_att, y_sgu, gl, merged, pa, ps, b_gate, wba, wbs, wo),
        in_specs=[_row_spec(TM, D_MODEL), _row_spec(TM, D_ATT), _row_spec(TM, D_SGU), _row_spec(TM, 2 * D_MODEL),
                  _row_spec(TM, D_MODEL), _row_spec(TM, D_MODEL), _row_spec(TM, D_MODEL),
                  _const_spec((1, 2 * D_MODEL)), _const_spec(wba.shape), _const_spec(wbs.shape), _const_spec(wo.shape)],
        out_specs=[_row_spec(TM, D_ATT), _row_spec(TM, D_SGU), _row_spec(TM, 2 * D_MODEL), _acc_spec((1, 2 * D_MODEL)),
                   _acc_spec(wba.shape), _acc_spec(wbs.shape), _acc_spec(wo.shape)],
        out_shape=[jax.ShapeDtypeStruct((T, D_ATT), BF16), jax.ShapeDtypeStruct((T, D_SGU), BF16),
                   jax.ShapeDtypeStruct((T, 2 * D_MODEL), BF16), jax.ShapeDtypeStruct((1, 2 * D_MODEL), F32),
                   jax.ShapeDtypeStruct(wba.shape, F32), jax.ShapeDtypeStruct(wbs.shape, F32),
                   jax.ShapeDtypeStruct(wo.shape, F32)])


def _loss_bwd(x, target, g):
    T = x.shape[0]

    def body(x_ref, t_ref, g_ref, dx_ref, loss_ref, dg_ref):
        @pl.when(pl.program_id(0) == 0)
        def _():
            loss_ref[...] = jnp.zeros_like(loss_ref)
            dg_ref[...] = jnp.zeros_like(dg_ref)

        gv = g_ref[...]
        xhat, r, y = _rms_fwd(x_ref[...], gv)
        err = y - t_ref[...]
        per_tok = jnp.mean(err * err, axis=-1, keepdims=True)
        loss_ref[...] += 0.5 * jnp.sum(per_tok, axis=0, keepdims=True)
        dxn, dg = _rms_bwd(err * (1.0 / D_MODEL), xhat, r, gv)
        dx_ref[...] = dxn
        dg_ref[...] += dg

    return pl.pallas_call(
        body, name="loss_bwd", grid=(T // TM,),
        in_specs=[_row_spec(TM, D_MODEL), _row_spec(TM, D_MODEL), _const_spec((1, D_MODEL))],
        out_specs=[_row_spec(TM, D_MODEL), _acc_spec((1, 128)), _acc_spec((1, D_MODEL))],
        out_shape=[jax.ShapeDtypeStruct((T, D_MODEL), F32), jax.ShapeDtypeStruct((1, 128), F32),
                   jax.ShapeDtypeStruct((1, D_MODEL), F32)],
        compiler_params=_params(("arbitrary",)),
    )(x, target, g)


BIG = ("ffn1_w_gate", "ffn1_w_up", "ffn1_w_down", "w_in", "w_branch_att", "w_branch_sgu", "w_out",
       "ffn2_w_gate", "ffn2_w_up", "ffn2_w_down")
SMALL = ("norm_ffn1", "norm_mix", "b_gate", "rel_bias", "sgu_ln_g", "sgu_ln_b", "sgu_w_s", "sgu_b_s", "norm_ffn2",
         "norm_final")


G_FFN1 = ("ffn1_w_gate", "ffn1_w_up", "ffn1_w_down")
G_MIX = ("w_in", "w_branch_att", "w_branch_sgu", "w_out")
G_FFN2 = ("ffn2_w_gate", "ffn2_w_up", "ffn2_w_down")


def _local_step(x, target, wb, ws, dist=None):
    def gather_on(names):
        return _ag_payload([wb[n] for n in names]) if dist else None

    t2 = _relbias_fwd(ws["rel_bias"])
    bias2 = _bias_blocks(t2)
    bst = ws["sgu_b_s"].T
    w_st = jnp.swapaxes(ws["sgu_w_s"], 1, 2)

    if dist:
        wb.update(zip(G_FFN1, _call(lambda: None, gather_on(G_FFN1), name="allgather_ffn1", grid=None, in_specs=[],
                                    out_specs=[], out_shape=[])))
    x1, h1, a1, b1, *got = _ffn_fwd(x, ws["norm_ffn1"], wb["ffn1_w_gate"], wb["ffn1_w_up"], wb["ffn1_w_down"],
                                    "ffn1_fwd", gather_on(G_MIX))
    wb.update(zip(G_MIX, got))
    h2, qkv, zs, gl, *got = _in_fwd(x1, ws["norm_mix"], wb["w_in"], gather_on(G_FFN2[0:1]))
    wb.update(zip(G_FFN2[0:1], got))
    y_att, *got = _att_fwd(qkv, bias2, gather_on(G_FFN2[1:2]))
    wb.update(zip(G_FFN2[1:2], got))
    y_sgu = _sgu_fwd(zs, ws["sgu_ln_g"], ws["sgu_ln_b"], ws["sgu_w_s"], bst)
    x2, merged, pa, ps, *got = _merge_fwd(x1, y_att, y_sgu, gl, ws["b_gate"], wb["w_branch_att"], wb["w_branch_sgu"],
                                          wb["w_out"], gather_on(G_FFN2[2:3]))
    wb.update(zip(G_FFN2[2:3], got))
    x3, h3, a3, b3 = _ffn_fwd(x2, ws["norm_ffn2"], wb["ffn2_w_gate"], wb["ffn2_w_up"], wb["ffn2_w_down"], "ffn2_fwd")
    dx3, loss, g_final = _loss_bwd(x3, target, ws["norm_final"])

    gb, gs, sums = {}, {"norm_final": g_final}, {}

    def pair_on(names, small=None):
        return _px_payload([gb[n] for n in names], small) if dist else None

    def pair_add(names, halves):
        for n, rv in zip(names, halves):
            sums[n] = _pair_add(gb[n], rv, dist[0], dist[1], "pair_add_" + n)

    def chips_on(names):
        return _cx_payload([sums[n][1] for n in names], [sums[n][2] for n in names]) if dist else None

    dx2, da3, db3, gs["norm_ffn2"] = _ffn_dgrad(dx3, x2, a3, b3, ws["norm_ffn2"], wb["ffn2_w_gate"], wb["ffn2_w_up"],
                                                wb["ffn2_w_down"], "ffn2_dgrad")
    gb["ffn2_w_gate"], gb["ffn2_w_up"], gb["ffn2_w_down"] = _ffn_wgrad(h3, dx3, a3, b3, da3, db3, "ffn2_wgrad")
    dy_att, dy_sgu, dgl, gs["b_gate"], gb["w_branch_att"], gb["w_branch_sgu"], gb["w_out"], *got = _merge_bwd(
        dx2, y_att, y_sgu, gl, merged, pa, ps, ws["b_gate"], wb["w_branch_att"], wb["w_branch_sgu"], wb["w_out"],
        pair_on(G_FFN2))
    pair_add(G_FFN2, got)
    dq, dk, dv, db2t, *lands2 = _att_bwd(qkv, dy_att, jnp.swapaxes(bias2, 1, 2), chips_on(G_FFN2))
    gs["rel_bias"] = _relbias_bwd(_unskew(jnp.swapaxes(db2t, 1, 2)))
    dzs, gs["sgu_w_s"], dbt, gs["sgu_ln_g"], gs["sgu_ln_b"] = _sgu_bwd(zs, dy_sgu, ws["sgu_ln_g"], ws["sgu_ln_b"],
                                                                      ws["sgu_w_s"], w_st, bst)
    gs["sgu_b_s"] = dbt.T
    dx1, dz, gs["norm_mix"] = _in_dgrad(dx2, x1, ws["norm_mix"], wb["w_in"], dq, dk, dv, dzs, dgl)
    gb["w_in"] = _in_wgrad(h2, dz)
    gx, da1, db1, gs["norm_ffn1"], *got = _ffn_dgrad(dx1, x, a1, b1, ws["norm_ffn1"], wb["ffn1_w_gate"],
                                                    wb["ffn1_w_up"], wb["ffn1_w_down"], "ffn1_dgrad", pair_on(G_MIX))
    pair_add(G_MIX, got)
    gb["ffn1_w_gate"], gb["ffn1_w_up"], gb["ffn1_w_down"], *lands_mix = _ffn_wgrad(h1, dx1, a1, b1, da1, db1,
                                                                                   "ffn1_wgrad", chips_on(G_MIX))
    if not dist:
        return loss, gx, gb, gs

    tail = _tail_reduce([gb[n] for n in G_FFN1], _pack_small(gs, loss))
    for i, n in enumerate(G_FFN1):
        sums[n] = (tail[i],)
    lands1, small_sums = tail[len(G_FFN1):2 * len(G_FFN1)], tail[-1]
    lands = dict(zip(G_FFN2 + G_MIX + G_FFN1, list(lands2) + list(lands_mix) + list(lands1)))
    fulls = [_final_sum(sums[n][0], lands[n], dist[1], dist[0], "final_sum_" + n) for n in BIG]
    return loss, gx, dict(zip(BIG, _sibling_share(fulls, "sibling_share"))), small_sums


_ANY = pl.BlockSpec(memory_space=pl.ANY)
_VMEM = pl.BlockSpec(memory_space=pltpu.VMEM)


def _mesh_pos():
    return lax.axis_index("x"), lax.axis_index("y"), lax.axis_index("c")


def _cast_slots(shards, chip, name):
    n = len(shards)
    r, ncol = shards[0].shape
    tr = r // 2

    def body(me_ref, *refs):
        for i_ref, o_ref in zip(refs[:n], refs[n:]):
            o_ref[0] = i_ref[...].astype(BF16)

    grid_spec = pltpu.PrefetchScalarGridSpec(
        num_scalar_prefetch=1, grid=(r // tr,),
        in_specs=[pl.BlockSpec((tr, ncol), lambda i, me: (i, 0))] * n,
        out_specs=[pl.BlockSpec((1, tr, ncol), lambda i, me: (me[0], i, 0))] * n)
    return pl.pallas_call(
        body, name=name, grid_spec=grid_spec,
        out_shape=[jax.ShapeDtypeStruct((N_SHARD, r, ncol), BF16)] * n,
        compiler_params=_params(("arbitrary",)),
    )(chip, *shards)


class _Payload:
    def __init__(self, arrays, out_shapes, aliases, scratch, phases):
        self.arrays = list(arrays)
        self.out_shapes = list(out_shapes)
        self.aliases = dict(aliases)
        self.scratch = list(scratch)
        self.phases = phases


def _remote(src, dst, ssem, rsem, dev):
    return pltpu.make_async_remote_copy(src_ref=src, dst_ref=dst, send_sem=ssem, recv_sem=rsem, device_id=dev,
                                        device_id_type=MESH)


def _call(body, payload, *, name, grid, in_specs, out_specs, out_shape, scratch_shapes=(), sem=None, when=None,
          operands=(), takes_middle=False):
    in_specs, out_specs, out_shape = list(in_specs), list(out_specs), list(out_shape)
    scratch_shapes = list(scratch_shapes)
    n_in, n_out, n_scr = len(in_specs), len(out_specs), len(scratch_shapes)
    kwargs = {}
    kernel = body
    if payload is not None:
        k_in, k_out = len(payload.arrays), len(payload.out_shapes)
        rank = len(grid) if grid else 0

        def kernel(*refs):
            a, b = n_in, n_in + k_in
            c, d = b + n_out, b + n_out + k_out
            e = d + n_scr
            phases = payload.phases(refs[a:b], refs[c:d], refs[e:])

            def run():
                body(*refs[:a], *refs[b:c], *refs[d:e])

            if not grid:
                phases[0]()
                if len(phases) == 3 and takes_middle:
                    body(*refs[:a], *refs[b:c], *refs[d:e], middle=phases[1])
                    phases[2]()
                    return
                run()
                for ph in phases[1:]:
                    ph()
                return
            step = pl.program_id(0)
            if rank == 2:
                step = step * grid[1] + pl.program_id(1)
            marks = list(when)
            if len(phases) == 3:
                marks = [when[0], (max(when[1][0] - 3, 0), False), when[1]]
            for ph, (at, before) in zip(phases, marks):
                if before:
                    pl.when(step == at)(ph)
            run()
            for ph, (at, before) in zip(phases, marks):
                if not before:
                    pl.when(step == at)(ph)

        in_specs += [_ANY] * k_in
        out_specs += [_ANY] * k_out
        out_shape += payload.out_shapes
        scratch_shapes += payload.scratch
        kwargs["input_output_aliases"] = {n_in + i: n_out + j for i, j in payload.aliases.items()}
        operands = tuple(operands) + tuple(payload.arrays)
    if grid:
        kwargs["grid"] = grid
    return pl.pallas_call(kernel, name=name, in_specs=in_specs, out_specs=out_specs, out_shape=out_shape,
                          scratch_shapes=scratch_shapes, compiler_params=_params(sem), **kwargs)(*operands)


def _ag_payload(slots):
    n = len(slots)

    def phases(_, refs, sems):
        send_i, recv_i, send_d, recv_d = sems
        x, y, c = _mesh_pos()
        me = 2 * x + y

        def half(w, core):
            rh = slots[w].shape[1] // 2
            return pl.ds(core * rh, rh)

        def ici(w, j):
            t = (me + 1 + j) % N_SHARD
            mine = refs[w].at[me, half(w, c), :]
            return _remote(mine, mine, send_i.at[3 * w + j], recv_i.at[3 * w + j], (t // 2, t % 2, c))

        def d2d(w, j, core):
            s = (me + 3 - j) % N_SHARD
            land = refs[w].at[s, half(w, core), :]
            return _remote(land, land, send_d.at[3 * w + j], recv_d.at[3 * w + j], (x, y, 1 - c))

        def start():
            for w in range(n):
                for j in range(3):
                    ici(w, j).start()

        def forward():
            for w in range(n):
                for j in range(3):
                    s = (me + 3 - j) % N_SHARD
                    land = refs[w].at[s, half(w, c), :]
                    _remote(land, land, send_i.at[3 * w + j], recv_i.at[3 * w + j], (x, y, c)).wait_recv()
                    d2d(w, j, c).start()

        def finish():
            for w in range(n):
                for j in range(3):
                    d2d(w, j, 1 - c).wait_recv()
            for w in range(n):
                for j in range(3):
                    ici(w, j).wait_send()
                    d2d(w, j, c).wait_send()

        return [start, forward, finish]

    return _Payload(slots, [jax.ShapeDtypeStruct(s.shape, s.dtype) for s in slots], {i: i for i in range(n)},
                    [pltpu.SemaphoreType.DMA((3 * n,)) for _ in range(4)], phases)


def _px_payload(grads, small=None):
    arrays = list(grads) + ([small] if small is not None else [])
    n = len(arrays)

    def phases(ins, outs, sems):
        send, recv = sems
        x, y, c = _mesh_pos()

        def copy(w):
            if w < len(grads):
                rh = grads[w].shape[1] // 2
                src = ins[w].at[:, pl.ds((1 - c) * rh, rh), :]
            else:
                src = ins[w]
            return _remote(src, outs[w], send.at[w], recv.at[w], (x, y, 1 - c))

        def start():
            for w in range(n):
                copy(w).start()

        def finish():
            for w in range(n):
                copy(w).wait()

        return [start, finish]

    out_shapes = [jax.ShapeDtypeStruct((N_SHARD, g.shape[1] // 2, g.shape[2]), F32) for g in grads]
    if small is not None:
        out_shapes.append(jax.ShapeDtypeStruct(small.shape, F32))
    return _Payload(arrays, out_shapes, {}, [pltpu.SemaphoreType.DMA((n,)), pltpu.SemaphoreType.DMA((n,))], phases)


def _cx_payload(pbs, lands):
    n = len(pbs)

    def phases(ins, outs, sems):
        send, recv = sems
        x, y, c = _mesh_pos()
        me = 2 * x + y

        def copy(w, j):
            t = (me + 1 + j) % N_SHARD
            return _remote(ins[w].at[t], outs[w].at[me], send.at[3 * w + j], recv.at[3 * w + j], (t // 2, t % 2, c))

        def start():
            for w in range(n):
                for j in range(3):
                    copy(w, j).start()

        def finish():
            for w in range(n):
                for j in range(3):
                    copy(w, j).wait()

        return [start, finish]

    return _Payload(list(pbs) + list(lands), [jax.ShapeDtypeStruct(p.shape, BF16) for p in lands],
                    {n + i: i for i in range(n)},
                    [pltpu.SemaphoreType.DMA((3 * n,)), pltpu.SemaphoreType.DMA((3 * n,))], phases)


def _pair_add(g, rv, core, chip, name):
    _, r, ncol = g.shape
    rh = r // 2

    def body(c_ref, me_ref, g_ref, rv_ref, pf_ref, pb_ref, land_ref):
        s = g_ref[0] + rv_ref[0]
        sb = s.astype(BF16)
        pb_ref[0] = sb

        @pl.when(pl.program_id(0) == me_ref[0])
        def _():
            pf_ref[...] = s
            land_ref[0] = sb

    slot = pl.BlockSpec((1, rh, ncol), lambda s, c, me: (s, 0, 0))
    grid_spec = pltpu.PrefetchScalarGridSpec(
        num_scalar_prefetch=2, grid=(N_SHARD,),
        in_specs=[pl.BlockSpec((1, rh, ncol), lambda s, c, me: (s, c[0], 0)), slot],
        out_specs=[pl.BlockSpec((rh, ncol), lambda s, c, me: (0, 0)), slot,
                   pl.BlockSpec((1, rh, ncol), lambda s, c, me: (me[0], 0, 0))])
    return pl.pallas_call(
        body, name=name, grid_spec=grid_spec,
        out_shape=[jax.ShapeDtypeStruct((rh, ncol), F32), jax.ShapeDtypeStruct((N_SHARD, rh, ncol), BF16),
                   jax.ShapeDtypeStruct((N_SHARD, rh, ncol), BF16)],
        compiler_params=_params(("arbitrary",)),
    )(core, chip, g, rv)


def _tail_reduce(grads, small):
    n = len(grads)
    _, r, ncol = grads[0].shape
    rh = r // 2

    def body(*refs):
        g_hbm, sm = refs[:n], refs[n]
        pf, land, sm_out = refs[n + 1:2 * n + 1], refs[2 * n + 1:3 * n + 1], refs[3 * n + 1]
        scr = refs[3 * n + 2:]
        rv, mine, sendb = scr[:n], scr[n:2 * n], scr[2 * n:3 * n]
        sm_rv, sm_sum, d_send, d_recv, load, i_send, i_recv, store = scr[3 * n:]
        x, y, c = _mesh_pos()
        me = 2 * x + y
        sib = (x, y, 1 - c)

        def pair(w):
            src = g_hbm[w].at[:, pl.ds((1 - c) * rh, rh), :] if w < n else sm
            return _remote(src, rv[w] if w < n else sm_rv, d_send.at[w], d_recv.at[w], sib)

        def chips(w, j):
            t = (me + 1 + j) % N_SHARD
            src = sendb[w].at[t] if w < n else sm_sum
            dst = land[w].at[me] if w < n else sm_out.at[me]
            return _remote(src, dst, i_send.at[3 * w + j], i_recv.at[3 * w + j], (t // 2, t % 2, c))

        loads = [pltpu.make_async_copy(g_hbm[w].at[:, pl.ds(c * rh, rh), :], mine[w], load.at[w]) for w in range(n)]
        for w in range(n + 1):
            pair(w).start()
        for cp in loads:
            cp.start()
        stores = []
        for w in range(n):
            loads[w].wait()
            pair(w).wait_recv()
            for k in range(N_SHARD):
                s = mine[w][k] + rv[w][k]
                mine[w][k] = s
                sendb[w][k] = s.astype(BF16)
            stores += [pltpu.make_async_copy(mine[w].at[me], pf[w], store.at[2 * w]),
                       pltpu.make_async_copy(sendb[w].at[me], land[w].at[me], store.at[2 * w + 1])]
            for cp in stores[-2:]:
                cp.start()
            for j in range(3):
                chips(w, j).start()
        pair(n).wait_recv()
        sm_sum[...] = sm[...] + sm_rv[...]
        stores.append(pltpu.make_async_copy(sm_sum, sm_out.at[me], store.at[2 * n]))
        stores[-1].start()
        for j in range(3):
            chips(n, j).start()
        for w in range(n + 1):
            pair(w).wait_send()
            for j in range(3):
                chips(w, j).wait()
        for cp in stores:
            cp.wait()

    half = (N_SHARD, rh, ncol)
    return _call(
        body, None, name="tail_reduce", grid=None,
        in_specs=[_ANY] * n + [_VMEM], out_specs=[_ANY] * (2 * n + 1),
        out_shape=([jax.ShapeDtypeStruct((rh, ncol), F32)] * n + [jax.ShapeDtypeStruct(half, BF16)] * n
                   + [jax.ShapeDtypeStruct((N_SHARD,) + small.shape, F32)]),
        scratch_shapes=([pltpu.VMEM(half, F32)] * (2 * n) + [pltpu.VMEM(half, BF16)] * n
                        + [pltpu.VMEM(small.shape, F32), pltpu.VMEM(small.shape, F32),
                           pltpu.SemaphoreType.DMA((n + 1,)), pltpu.SemaphoreType.DMA((n + 1,)),
                           pltpu.SemaphoreType.DMA((n,)), pltpu.SemaphoreType.DMA((3 * n + 3,)),
                           pltpu.SemaphoreType.DMA((3 * n + 3,)), pltpu.SemaphoreType.DMA((2 * n + 1,))]),
        operands=(*grads, small))


def _final_sum(pf, land, chip, core, name):
    _, rh, ncol = land.shape

    def body(me_ref, c_ref, pf_ref, land_ref, o_ref):
        me = me_ref[0]
        acc = jnp.zeros((rh, ncol), F32)
        for k in range(N_SHARD):
            acc = acc + jnp.where(me == k, pf_ref[...], land_ref[k].astype(F32))
        o_ref[...] = acc

    grid_spec = pltpu.PrefetchScalarGridSpec(
        num_scalar_prefetch=2, grid=(1,),
        in_specs=[pl.BlockSpec((rh, ncol), lambda i, me, c: (0, 0)),
                  pl.BlockSpec((N_SHARD, rh, ncol), lambda i, me, c: (0, 0, 0))],
        out_specs=pl.BlockSpec((rh, ncol), lambda i, me, c: (c[0], 0)))
    return pl.pallas_call(
        body, name=name, grid_spec=grid_spec, out_shape=jax.ShapeDtypeStruct((2 * rh, ncol), F32),
        compiler_params=_params(("arbitrary",)),
    )(chip, core, pf, land)


def _sibling_share(fulls, name):
    n = len(fulls)

    def body(*refs):
        outs = refs[n:2 * n]
        send, recv = refs[2 * n:]
        x, y, c = _mesh_pos()
        cps = []
        for w in range(n):
            rh = fulls[w].shape[0] // 2
            mine = outs[w].at[pl.ds(c * rh, rh), :]
            cp = pltpu.make_async_remote_copy(src_ref=mine, dst_ref=mine, send_sem=send.at[w], recv_sem=recv.at[w],
                                              device_id=(x, y, 1 - c), device_id_type=MESH)
            cp.start()
            cps.append(cp)
        for cp in cps:
            cp.wait()

    return pl.pallas_call(
        body, name=name,
        in_specs=[_ANY] * n, out_specs=[_ANY] * n,
        out_shape=[jax.ShapeDtypeStruct(f.shape, F32) for f in fulls],
        input_output_aliases={i: i for i in range(n)},
        scratch_shapes=[pltpu.SemaphoreType.DMA((n,)), pltpu.SemaphoreType.DMA((n,))],
    )(*fulls)


_ROW = {"rel_bias": 128, "sgu_b_s": 136, "norm_ffn1": 144, "norm_mix": 145, "norm_ffn2": 146, "norm_final": 147,
        "b_gate": 148, "sgu_ln_g": 150, "sgu_ln_b": 151}


def _pack_small(gs, loss):
    def body(ws, rel, bs, n1, nm, n2, nf, bg, lg, lb, loss_ref, o_ref):
        o_ref[...] = jnp.zeros_like(o_ref)
        o_ref[LOSS_ROW:LOSS_ROW + 1, 0:128] = loss_ref[...]
        for g in range(SGU_GROUPS):
            o_ref[0:SGU_BLOCK, g * SGU_BLOCK:(g + 1) * SGU_BLOCK] = ws[g]
        o_ref[128:136, 0:REL_PAD] = rel[...]
        o_ref[136:144, 0:SGU_BLOCK] = bs[...]
        o_ref[144:145, :] = n1[...]
        o_ref[145:146, :] = nm[...]
        o_ref[146:147, :] = n2[...]
        o_ref[147:148, :] = nf[...]
        o_ref[148:149, :] = bg[:, 0:D_MODEL]
        o_ref[149:150, :] = bg[:, D_MODEL:2 * D_MODEL]
        o_ref[150:151, 0:D_SGU] = lg[...]
        o_ref[151:152, 0:D_SGU] = lb[...]

    order = ("sgu_w_s", "rel_bias", "sgu_b_s", "norm_ffn1", "norm_mix", "norm_ffn2", "norm_final", "b_gate", "sgu_ln_g",
             "sgu_ln_b")
    return pl.pallas_call(body, name="pack_small", out_shape=jax.ShapeDtypeStruct((SMALL_ROWS, D_MODEL), F32))(
        *[gs[k] for k in order], loss)


def _adam(w, g, m, v):
    m2 = ADAM_B1 * m + (1.0 - ADAM_B1) * g
    v2 = ADAM_B2 * v + (1.0 - ADAM_B2) * (g * g)
    m_hat = m2 / (1.0 - ADAM_B1 ** ADAM_STEP)
    v_hat = v2 / (1.0 - ADAM_B2 ** ADAM_STEP)
    delta = -ADAM_LR * (m_hat / (jnp.sqrt(v_hat) + ADAM_EPS) + ADAM_WD * w)
    return delta, m2, v2


def _adam_small(sin, w, m, v):
    names = SMALL
    k = len(names)

    def body(*refs):
        sin_ref = refs[0]
        w_r, m_r, v_r = refs[1:1 + k], refs[1 + k:1 + 2 * k], refs[1 + 2 * k:1 + 3 * k]
        outs = refs[1 + 3 * k:]
        tot = sin_ref[0] + sin_ref[1] + sin_ref[2] + sin_ref[3]
        outs[4 * k][...] = tot[LOSS_ROW:LOSS_ROW + 1, 0:128]
        for i, name in enumerate(names):
            o = outs[4 * i:4 * i + 4]
            if name == "sgu_w_s":
                for gi in range(SGU_GROUPS):
                    g = tot[0:SGU_BLOCK, gi * SGU_BLOCK:(gi + 1) * SGU_BLOCK]
                    res = (g,) + _adam(w_r[i][gi], g, m_r[i][gi], v_r[i][gi])
                    for ref, val in zip(o, res):
                        ref[gi] = val
                continue
            r0 = _ROW[name]
            if name == "rel_bias":
                g = tot[r0:r0 + HEADS, 0:REL_PAD]
            elif name == "sgu_b_s":
                g = tot[r0:r0 + SGU_GROUPS, 0:SGU_BLOCK]
            elif name == "b_gate":
                g = jnp.concatenate([tot[r0:r0 + 1, :], tot[r0 + 1:r0 + 2, :]], axis=1)
            elif name in ("sgu_ln_g", "sgu_ln_b"):
                g = tot[r0:r0 + 1, 0:D_SGU]
            else:
                g = tot[r0:r0 + 1, :]
            res = (g,) + _adam(w_r[i][...], g, m_r[i][...], v_r[i][...])
            for ref, val in zip(o, res):
                ref[...] = val

    out_shape = []
    for name in names:
        out_shape += [jax.ShapeDtypeStruct(w[name].shape, F32)] * 4
    out_shape.append(jax.ShapeDtypeStruct((1, 128), F32))
    flat = pl.pallas_call(body, name="adam_small", out_shape=out_shape, compiler_params=_params())(
        sin, *[w[n] for n in names], *[m[n] for n in names], *[v[n] for n in names])
    return {name: tuple(flat[4 * i:4 * i + 4]) for i, name in enumerate(names)}, flat[4 * k]


def _adam_big(w, g, m, v, name):
    r, ncol = w.shape
    tr = 256 if r % 256 == 0 else r // 2

    def body(w_ref, g_ref, m_ref, v_ref, g2_ref, d_ref, m2_ref, v2_ref):
        gv = g_ref[...]
        g2_ref[...] = gv
        d_ref[...], m2_ref[...], v2_ref[...] = _adam(w_ref[...], gv, m_ref[...], v_ref[...])

    spec = pl.BlockSpec((tr, ncol), lambda i: (i, 0))
    return pl.pallas_call(
        body, name=name, grid=(r // tr,), in_specs=[spec] * 4, out_specs=[spec] * 4,
        out_shape=[jax.ShapeDtypeStruct(w.shape, F32)] * 4, compiler_params=_params(("arbitrary",)),
    )(w, g, m, v)


WEIGHTS = ("norm_ffn1", "ffn1_w_gate", "ffn1_w_up", "ffn1_w_down", "norm_mix", "w_in", "b_gate", "rel_bias", "sgu_ln_g",
           "sgu_ln_b", "sgu_w_s", "sgu_b_s", "w_branch_att", "w_branch_sgu", "w_out", "norm_ffn2", "ffn2_w_gate",
           "ffn2_w_up", "ffn2_w_down", "norm_final")


GATE_UP = ("ffn1_w_gate", "ffn1_w_up", "ffn2_w_gate", "ffn2_w_up")
_FFN = ("ffn1_w_gate", "ffn1_w_up", "ffn1_w_down", "ffn2_w_gate", "ffn2_w_up", "ffn2_w_down")
_CAST_GROUPS = ((_FFN, "cast_ffn"), (("w_in",), "cast_w_in"), (("w_branch_att", "w_branch_sgu"), "cast_branch"),
                (("w_out",), "cast_w_out"))


def _big_form(name, a):
    return jnp.swapaxes(a, 1, 2)[0] if name in GATE_UP else a[0]


def _big_back(name, a):
    return jnp.swapaxes(a[None], 1, 2) if name in GATE_UP else a[None]


def _small_form(name, a):
    if name == "norm_final":
        return a.reshape(1, D_MODEL)
    if name == "rel_bias":
        return jnp.pad(a[0], ((0, 0), (0, REL_PAD - N_REL)))
    if name in ("sgu_w_s", "sgu_b_s"):
        return a[0]
    return a


def _small_back(name, a, like):
    if name == "rel_bias":
        a = a[:, :N_REL]
    return a.reshape(like.shape)


def kernel(x, norm_ffn1, ffn1_w_gate, ffn1_w_up, ffn1_w_down, norm_mix, w_in, b_gate, rel_bias, sgu_ln_g, sgu_ln_b, sgu_w_s, sgu_b_s, w_branch_att, w_branch_sgu, w_out, norm_ffn2, ffn2_w_gate, ffn2_w_up, ffn2_w_down, norm_final, loss_target, m_norm_ffn1, m_ffn1_w_gate, m_ffn1_w_up, m_ffn1_w_down, m_norm_mix, m_w_in, m_b_gate, m_rel_bias, m_sgu_ln_g, m_sgu_ln_b, m_sgu_w_s, m_sgu_b_s, m_w_branch_att, m_w_branch_sgu, m_w_out, m_norm_ffn2, m_ffn2_w_gate, m_ffn2_w_up, m_ffn2_w_down, m_norm_final, v_norm_ffn1, v_ffn1_w_gate, v_ffn1_w_up, v_ffn1_w_down, v_norm_mix, v_w_in, v_b_gate, v_rel_bias, v_sgu_ln_g, v_sgu_ln_b, v_sgu_w_s, v_sgu_b_s, v_w_branch_att, v_w_branch_sgu, v_w_out, v_norm_ffn2, v_ffn2_w_gate, v_ffn2_w_up, v_ffn2_w_down, v_norm_final):
    w = dict(norm_ffn1=norm_ffn1, ffn1_w_gate=ffn1_w_gate, ffn1_w_up=ffn1_w_up, ffn1_w_down=ffn1_w_down, norm_mix=norm_mix,
             w_in=w_in, b_gate=b_gate, rel_bias=rel_bias, sgu_ln_g=sgu_ln_g, sgu_ln_b=sgu_ln_b, sgu_w_s=sgu_w_s,
             sgu_b_s=sgu_b_s, w_branch_att=w_branch_att, w_branch_sgu=w_branch_sgu, w_out=w_out, norm_ffn2=norm_ffn2,
             ffn2_w_gate=ffn2_w_gate, ffn2_w_up=ffn2_w_up, ffn2_w_down=ffn2_w_down, norm_final=norm_final)
    m = dict(norm_ffn1=m_norm_ffn1, ffn1_w_gate=m_ffn1_w_gate, ffn1_w_up=m_ffn1_w_up, ffn1_w_down=m_ffn1_w_down,
             norm_mix=m_norm_mix, w_in=m_w_in, b_gate=m_b_gate, rel_bias=m_rel_bias, sgu_ln_g=m_sgu_ln_g,
             sgu_ln_b=m_sgu_ln_b, sgu_w_s=m_sgu_w_s, sgu_b_s=m_sgu_b_s, w_branch_att=m_w_branch_att,
             w_branch_sgu=m_w_branch_sgu, w_out=m_w_out, norm_ffn2=m_norm_ffn2, ffn2_w_gate=m_ffn2_w_gate,
             ffn2_w_up=m_ffn2_w_up, ffn2_w_down=m_ffn2_w_down, norm_final=m_norm_final)
    v = dict(norm_ffn1=v_norm_ffn1, ffn1_w_gate=v_ffn1_w_gate, ffn1_w_up=v_ffn1_w_up, ffn1_w_down=v_ffn1_w_down,
             norm_mix=v_norm_mix, w_in=v_w_in, b_gate=v_b_gate, rel_bias=v_rel_bias, sgu_ln_g=v_sgu_ln_g,
             sgu_ln_b=v_sgu_ln_b, sgu_w_s=v_sgu_w_s, sgu_b_s=v_sgu_b_s, w_branch_att=v_w_branch_att,
             w_branch_sgu=v_w_branch_sgu, w_out=v_w_out, norm_ffn2=v_norm_ffn2, ffn2_w_gate=v_ffn2_w_gate,
             ffn2_w_up=v_ffn2_w_up, ffn2_w_down=v_ffn2_w_down, norm_final=v_norm_final)

    core = lax.axis_index("c").astype(jnp.int32).reshape(1)
    chip = (2 * lax.axis_index("x") + lax.axis_index("y")).astype(jnp.int32).reshape(1)

    wk = {n: _big_form(n, w[n]) for n in BIG}
    slots = {}
    for names, call in _CAST_GROUPS:
        slots.update(zip(names, _cast_slots([wk[n] for n in names], chip, call)))
    ws = {n: _small_form(n, w[n]) for n in SMALL}
    _, gx, shard_grads, small_sums = _local_step(x[0], loss_target[0], slots, ws, (core, chip))

    small, loss = _adam_small(small_sums, ws, {n: _small_form(n, m[n]) for n in SMALL},
                              {n: _small_form(n, v[n]) for n in SMALL})
    grad, delta, new_m, new_v = {}, {}, {}, {}
    for n in SMALL:
        grad[n], delta[n], new_m[n], new_v[n] = (_small_back(n, a, w[n]) for a in small[n])
    for n in BIG:
        g2, d2, m2, v2 = _adam_big(wk[n], shard_grads[n], _big_form(n, m[n]), _big_form(n, v[n]), "adam_" + n)
        grad[n], delta[n], new_m[n], new_v[n] = (_big_back(n, a) for a in (g2, d2, m2, v2))

    return (loss[0, 0], gx.reshape(x.shape), *[grad[n] for n in WEIGHTS], *[delta[n] for n in WEIGHTS],
            *[new_m[n] for n in WEIGHTS], *[new_v[n] for n in WEIGHTS])
```

```python
import functools

import jax
import jax.numpy as jnp
from jax import lax
from jax.experimental import pallas as pl
from jax.experimental.pallas import tpu as pltpu

F32 = jnp.float32
BF16 = jnp.bfloat16

D_MODEL = 1024
N_SHARD = 4
D_FF = 2816
FF_S = D_FF // N_SHARD
D_ATT = 512
D_SGU = 512
D_IN = 3 * D_ATT + 2 * D_SGU + 2 * D_MODEL
IN_S = D_IN // N_SHARD
BR_S = D_MODEL // N_SHARD
HEADS = 8
HEAD_DIM = 64
CHUNK = 64
N_LEFT = 8
BAND = (N_LEFT + 1) * CHUNK
REL_CLIP = 256
N_REL = 2 * REL_CLIP + 1
REL_PAD = 640
SGU_BLOCK = 128
SGU_GROUPS = 8
SGU_GDIM = 64
EPS = 1e-6
NEG_INF = -1e30

ATT_ROWS = 2 * CHUNK
ATT_KEYS = BAND + CHUNK
ATT_PAD = N_LEFT * CHUNK

ADAM_LR = 0.001
ADAM_B1 = 0.9
ADAM_B2 = 0.999
ADAM_EPS = 1e-08
ADAM_WD = 0.01
ADAM_STEP = 10

TM = 256
TW = 1024
DGRAD_ROWS = 64
VMEM_LIMIT = 56 * 1024 * 1024

SMALL_ROWS = 160
LOSS_ROW = 152
MESH = pl.DeviceIdType.MESH

_NT = (((1,), (1,)), ((), ()))
_TN = (((0,), (0,)), ((), ()))


def _params(sem=None):
    return pltpu.CompilerParams(dimension_semantics=sem, vmem_limit_bytes=VMEM_LIMIT)


def _const_spec(shape):
    nd = len(shape)
    return pl.BlockSpec(shape, lambda *_: (0,) * nd, pipeline_mode=pl.Buffered(1))


def _acc_spec(shape):
    nd = len(shape)
    return pl.BlockSpec(shape, lambda *_: (0,) * nd)


def _row_spec(tm, ncols, off=0):
    return pl.BlockSpec((tm, ncols), lambda i: (i + off, 0))


def _row3_spec(tm, ncols):
    return pl.BlockSpec((N_SHARD, tm, ncols), lambda i: (0, i, 0))


def _dot(a, b):
    return jnp.dot(a, b, preferred_element_type=F32)


def _dot_nt(a, b):
    return lax.dot_general(a, b, _NT, preferred_element_type=F32)


def _dot_tn(a, b):
    return lax.dot_general(a, b, _TN, preferred_element_type=F32)


def _rms_fwd(x, g):
    r = lax.rsqrt(jnp.mean(x * x, axis=-1, keepdims=True) + EPS)
    xhat = x * r
    return xhat, r, xhat * g


def _rms_bwd(dh, xhat, r, g):
    dxhat = dh * g
    dx = r * (dxhat - xhat * jnp.mean(dxhat * xhat, axis=-1, keepdims=True))
    dg = jnp.sum(dh * xhat, axis=0, keepdims=True)
    return dx, dg


def _sigmoid(x):
    return 1.0 / (1.0 + jnp.exp(-x))


def _edges(n_steps):
    return [(0, True), (n_steps - 1, False)]


def _ffn_fwd(x, g, wg, wu, wd, name, payload=None):
    T = x.shape[0]

    def body(x_ref, g_ref, wg_ref, wu_ref, wd_ref, xo_ref, h_ref, a_ref, b_ref):
        xv = x_ref[...]
        hb = _rms_fwd(xv, g_ref[...])[2].astype(BF16)
        h_ref[...] = hb
        acc = jnp.zeros((TM, D_MODEL), F32)
        for s in range(N_SHARD):
            a = _dot_nt(hb, wg_ref[s])
            b = _dot_nt(hb, wu_ref[s])
            a_ref[s] = a.astype(BF16)
            b_ref[s] = b.astype(BF16)
            sv = a * _sigmoid(a) * b
            acc += _dot(sv.astype(BF16), wd_ref[s])
        xo_ref[...] = xv + 0.5 * acc

    return _call(
        body, payload, name=name, grid=(T // TM,), when=_edges(T // TM), sem=("arbitrary",),
        in_specs=[_row_spec(TM, D_MODEL), _const_spec((1, D_MODEL)), _const_spec(wg.shape), _const_spec(wu.shape),
                  _const_spec(wd.shape)],
        out_specs=[_row_spec(TM, D_MODEL), _row_spec(TM, D_MODEL), _row3_spec(TM, FF_S), _row3_spec(TM, FF_S)],
        out_shape=[jax.ShapeDtypeStruct((T, D_MODEL), F32), jax.ShapeDtypeStruct((T, D_MODEL), BF16),
                   jax.ShapeDtypeStruct((N_SHARD, T, FF_S), BF16), jax.ShapeDtypeStruct((N_SHARD, T, FF_S), BF16)],
        operands=(x, g, wg, wu, wd))


def _ffn_dgrad(dout, x, a, b, g, wg, wu, wd, name, payload=None):
    T = x.shape[0]

    def body(do_ref, x_ref, a_ref, b_ref, g_ref, wg_ref, wu_ref, wd_ref, dx_ref, da_ref, db_ref, dg_ref):
        do = do_ref[...]
        dob = (0.5 * do).astype(BF16)
        dh = jnp.zeros((TM, D_MODEL), F32)
        ds_next = _dot_nt(dob, wd_ref[0])
        for s in range(N_SHARD):
            ds = ds_next
            if s + 1 < N_SHARD:
                ds_next = _dot_nt(dob, wd_ref[s + 1])
            for r0 in range(0, TM, DGRAD_ROWS):
                rows = slice(r0, r0 + DGRAD_ROWS)
                av = a_ref[s, rows, :].astype(F32)
                bv = b_ref[s, rows, :].astype(F32)
                sig = _sigmoid(av)
                dsr = ds[rows]
                da_ref[s, rows, :] = (dsr * bv * (sig * (1.0 + av * (1.0 - sig)))).astype(BF16)
                db_ref[s, rows, :] = (dsr * (av * sig)).astype(BF16)
            dh += _dot(da_ref[s], wg_ref[s]) + _dot(db_ref[s], wu_ref[s])
        gv = g_ref[...]
        xhat, r, _ = _rms_fwd(x_ref[...], gv)
        dxn, dg = _rms_bwd(dh, xhat, r, gv)
        dx_ref[...] = do + dxn

        @pl.when(pl.program_id(0) == 0)
        def _():
            dg_ref[...] = jnp.zeros_like(dg_ref)

        dg_ref[...] += dg

    return _call(
        body, payload, name=name, grid=(T // TM,), when=_edges(T // TM), sem=("arbitrary",),
        in_specs=[_row_spec(TM, D_MODEL), _row_spec(TM, D_MODEL), _row3_spec(TM, FF_S), _row3_spec(TM, FF_S),
                  _const_spec((1, D_MODEL)), _const_spec(wg.shape), _const_spec(wu.shape), _const_spec(wd.shape)],
        out_specs=[_row_spec(TM, D_MODEL), _row3_spec(TM, FF_S), _row3_spec(TM, FF_S), _acc_spec((1, D_MODEL))],
        out_shape=[jax.ShapeDtypeStruct((T, D_MODEL), F32), jax.ShapeDtypeStruct((N_SHARD, T, FF_S), BF16),
                   jax.ShapeDtypeStruct((N_SHARD, T, FF_S), BF16), jax.ShapeDtypeStruct((1, D_MODEL), F32)],
        operands=(dout, x, a, b, g, wg, wu, wd))


def _ffn_wgrad(h, dout, a, b, da, db, name, payload=None):
    T = h.shape[0]

    def body(h_ref, do_ref, a_ref, b_ref, da_ref, db_ref, gwg_ref, gwu_ref, gwd_ref):
        @pl.when(pl.program_id(1) == 0)
        def _():
            gwg_ref[...] = jnp.zeros_like(gwg_ref)
            gwu_ref[...] = jnp.zeros_like(gwu_ref)
            gwd_ref[...] = jnp.zeros_like(gwd_ref)

        hv = h_ref[...]
        dob = do_ref[...].astype(BF16)
        av = a_ref[0].astype(F32)
        sv = (0.5 * av * _sigmoid(av) * b_ref[0].astype(F32)).astype(BF16)
        gwg_ref[0] += _dot_tn(da_ref[0], hv)
        gwu_ref[0] += _dot_tn(db_ref[0], hv)
        gwd_ref[0] += _dot_tn(sv, dob)

    tw = min(TW, T)
    tok = pl.BlockSpec((tw, D_MODEL), lambda s, i: (i, 0))
    act = pl.BlockSpec((1, tw, FF_S), lambda s, i: (s, i, 0))
    return _call(
        body, payload, name=name, grid=(N_SHARD, T // tw), when=_edges(N_SHARD * (T // tw)),
        sem=("arbitrary", "arbitrary"),
        in_specs=[tok, tok, act, act, act, act],
        out_specs=[pl.BlockSpec((1, FF_S, D_MODEL), lambda s, i: (s, 0, 0))] * 3,
        out_shape=[jax.ShapeDtypeStruct((N_SHARD, FF_S, D_MODEL), F32)] * 3,
        operands=(h, dout, a, b, da, db))


def _in_fwd(x, g, w_in, payload=None):
    T = x.shape[0]

    def body(x_ref, g_ref, w_ref, h_ref, qkv_ref, zs_ref, gl_ref):
        hb = _rms_fwd(x_ref[...], g_ref[...])[2].astype(BF16)
        h_ref[...] = hb
        z0 = _dot(hb, w_ref[0])
        qkv_ref[:, 0:IN_S] = z0.astype(BF16)
        z1 = _dot(hb, w_ref[1])
        qkv_ref[:, IN_S:3 * D_ATT] = z1[:, 0:384].astype(BF16)
        zs_ref[:, 0:768] = z1[:, 384:IN_S]
        z2 = _dot(hb, w_ref[2])
        zs_ref[:, 768:1024] = z2[:, 0:256]
        gl_ref[:, 0:896] = z2[:, 256:IN_S]
        gl_ref[:, 896:2048] = _dot(hb, w_ref[3])

    return _call(
        body, payload, name="in_fwd", grid=(T // TM,), when=_edges(T // TM), sem=("arbitrary",),
        in_specs=[_row_spec(TM, D_MODEL), _const_spec((1, D_MODEL)), _const_spec(w_in.shape)],
        out_specs=[_row_spec(TM, D_MODEL), _row_spec(TM, 3 * D_ATT), _row_spec(TM, 2 * D_SGU), _row_spec(TM, 2 * D_MODEL)],
        out_shape=[jax.ShapeDtypeStruct((T, D_MODEL), BF16), jax.ShapeDtypeStruct((T, 3 * D_ATT), BF16),
                   jax.ShapeDtypeStruct((T, 2 * D_SGU), F32), jax.ShapeDtypeStruct((T, 2 * D_MODEL), F32)],
        operands=(x, g, w_in))


def _in_dgrad(dx_res, x, g, w_in, dq, dk, dv, dzs, dgl):
    T = x.shape[0]

    def body(dxr_ref, x_ref, g_ref, w_ref, dq_ref, dk_ref, dv_ref, dzs_ref, dgl_ref, dx_ref, dz_ref, dg_ref):
        dz = jnp.concatenate([dq_ref[...], dk_ref[...].astype(BF16), dv_ref[...].astype(BF16), dzs_ref[...], dgl_ref[...]],
                             axis=1)
        dz_ref[...] = dz
        dh = jnp.zeros((TM, D_MODEL), F32)
        for s in range(N_SHARD):
            dh += _dot_nt(dz[:, s * IN_S:(s + 1) * IN_S], w_ref[s])
        gv = g_ref[...]
        xhat, r, _ = _rms_fwd(x_ref[...], gv)
        dxn, dg = _rms_bwd(dh, xhat, r, gv)
        dx_ref[...] = dxr_ref[...] + dxn

        @pl.when(pl.program_id(0) == 0)
        def _():
            dg_ref[...] = jnp.zeros_like(dg_ref)

        dg_ref[...] += dg

    pad_blocks = ATT_PAD // TM
    return pl.pallas_call(
        body, name="in_dgrad", grid=(T // TM,),
        in_specs=[_row_spec(TM, D_MODEL), _row_spec(TM, D_MODEL), _const_spec((1, D_MODEL)), _const_spec(w_in.shape),
                  _row_spec(TM, D_ATT), _row_spec(TM, D_ATT, pad_blocks), _row_spec(TM, D_ATT, pad_blocks),
                  _row_spec(TM, 2 * D_SGU), _row_spec(TM, 2 * D_MODEL)],
        out_specs=[_row_spec(TM, D_MODEL), _row_spec(TM, D_IN), _acc_spec((1, D_MODEL))],
        out_shape=[jax.ShapeDtypeStruct((T, D_MODEL), F32), jax.ShapeDtypeStruct((T, D_IN), BF16),
                   jax.ShapeDtypeStruct((1, D_MODEL), F32)],
        compiler_params=_params(("arbitrary",)),
    )(dx_res, x, g, w_in, dq, dk, dv, dzs, dgl)


def _in_wgrad(h, dz):
    T = h.shape[0]

    def body(h_ref, dz_ref, gw_ref):
        @pl.when(pl.program_id(1) == 0)
        def _():
            gw_ref[...] = jnp.zeros_like(gw_ref)

        gw_ref[0] += _dot_tn(h_ref[...], dz_ref[...])

    return pl.pallas_call(
        body, name="in_wgrad", grid=(N_SHARD, T // min(TW, T)),
        in_specs=[pl.BlockSpec((min(TW, T), D_MODEL), lambda s, i: (i, 0)),
                  pl.BlockSpec((min(TW, T), IN_S), lambda s, i: (i, s))],
        out_specs=pl.BlockSpec((1, D_MODEL, IN_S), lambda s, i: (s, 0, 0)),
        out_shape=jax.ShapeDtypeStruct((N_SHARD, D_MODEL, IN_S), F32),
        compiler_params=_params(("arbitrary", "arbitrary")),
    )(h, dz)


def _rel_onehot():
    r = lax.broadcasted_iota(jnp.int32, (REL_PAD, REL_PAD), 0)
    n = lax.broadcasted_iota(jnp.int32, (REL_PAD, REL_PAD), 1)
    idx = jnp.clip(BAND - 1 - n, -REL_CLIP, REL_CLIP) + REL_CLIP
    return jnp.where(r == idx, 1.0, 0.0).astype(BF16)


def _split3(v):
    p1 = v.astype(BF16)
    r1 = v - p1.astype(F32)
    p2 = r1.astype(BF16)
    p3 = (r1 - p2.astype(F32)).astype(BF16)
    return p1, p2, p3


def _relbias_fwd(tab_pad):
    def body(t_ref, o_ref):
        oh = _rel_onehot()
        acc = jnp.zeros((HEADS, REL_PAD), F32)
        for p in _split3(t_ref[...]):
            acc += _dot(p, oh)
        o_ref[...] = acc

    return pl.pallas_call(body, name="relbias_fwd", out_shape=jax.ShapeDtypeStruct((HEADS, REL_PAD), F32))(tab_pad)


def _relbias_bwd(z):
    def body(z_ref, o_ref):
        oh = _rel_onehot()
        dt2 = jnp.sum(z_ref[...], axis=1)
        acc = jnp.zeros((HEADS, REL_PAD), F32)
        for p in _split3(dt2):
            acc += _dot_nt(p, oh)
        o_ref[...] = acc

    return pl.pallas_call(body, name="relbias_bwd", out_shape=jax.ShapeDtypeStruct((HEADS, REL_PAD), F32))(z)


def _bias_blocks(t2):
    flat = jnp.tile(t2, (1, CHUNK))
    skew = flat[:, :CHUNK * (REL_PAD - 1)].reshape(HEADS, CHUNK, REL_PAD - 1)
    bias = skew[:, :, CHUNK - 1:CHUNK - 1 + BAND]
    slabs = [jnp.pad(bias, ((0, 0), (0, 0), (CHUNK * c, ATT_KEYS - BAND - CHUNK * c)), constant_values=NEG_INF)
             for c in range(2)]
    return jnp.concatenate(slabs, axis=1)


def _unskew(db2):
    out = []
    for c in range(2):
        slab = db2[:, CHUNK * c:CHUNK * (c + 1), CHUNK * c:CHUNK * c + BAND]
        y = jnp.pad(slab, ((0, 0), (0, 0), (CHUNK - 1, REL_PAD - BAND - CHUNK + 1)))
        yf = jnp.pad(y.reshape(HEADS, CHUNK * REL_PAD), ((0, 0), (0, CHUNK)))
        out.append(yf.reshape(HEADS, CHUNK, REL_PAD + 1)[:, :, :REL_PAD])
    return jnp.concatenate(out, axis=1)


def _att_load(qkv_hbm, q_s, k_s, v_s, sem, T):
    copies = [pltpu.make_async_copy(qkv_hbm.at[:, 0:D_ATT], q_s, sem.at[0]),
              pltpu.make_async_copy(qkv_hbm.at[:, D_ATT:2 * D_ATT], k_s.at[pl.ds(ATT_PAD, T), :], sem.at[1]),
              pltpu.make_async_copy(qkv_hbm.at[:, 2 * D_ATT:3 * D_ATT], v_s.at[pl.ds(ATT_PAD, T), :], sem.at[2])]
    for cp in copies:
        cp.start()
    k_s[0:ATT_PAD, :] = jnp.zeros((ATT_PAD, D_ATT), BF16)
    v_s[0:ATT_PAD, :] = jnp.zeros((ATT_PAD, D_ATT), BF16)
    for cp in copies:
        cp.wait()


def _head(v, h):
    return v[:, h * HEAD_DIM:(h + 1) * HEAD_DIM]


def _rows(v, h):
    return v[h * ATT_ROWS:(h + 1) * ATT_ROWS]


def _att_exp(qs, kw, bias_ref, valid):
    s = jnp.concatenate([_dot_nt(_head(qs, h), _head(kw, h)) + bias_ref[h] for h in range(HEADS)], axis=0)
    if valid is not None:
        s = jnp.where(valid, s, NEG_INF)
    e = jnp.exp(s - jnp.max(s, axis=-1, keepdims=True))
    return e, 1.0 / jnp.sum(e, axis=-1, keepdims=True)


def _att_blocks(T, block, keys_on_rows=False, middle=None):
    n_edge = min(ATT_PAD // ATT_ROWS, T // ATT_ROWS)
    shape, axis = ((ATT_KEYS, 1), 0) if keys_on_rows else ((1, ATT_KEYS), 1)

    def edge(i, carry):
        r0 = i * ATT_ROWS
        block(i, (lax.broadcasted_iota(jnp.int32, shape, axis) + (r0 - ATT_PAD)) >= 0)
        return carry

    def inner(i, carry):
        block(i, None)
        return carry

    n_blocks = T // ATT_ROWS
    lax.fori_loop(0, n_edge, edge, 0)
    if middle is None:
        lax.fori_loop(n_edge, n_blocks, inner, 0)
        return
    n_late = max(n_blocks - n_blocks // 4, n_edge)
    lax.fori_loop(n_edge, n_late, inner, 0)
    middle()
    lax.fori_loop(n_late, n_blocks, inner, 0)


def _att_fwd(qkv, bias2, payload=None):
    T = qkv.shape[0]

    def body(qkv_hbm, bias_ref, y_ref, q_s, k_s, v_s, sem, middle=None):
        _att_load(qkv_hbm, q_s, k_s, v_s, sem, T)

        def block(i, valid):
            r0 = pl.multiple_of(i * ATT_ROWS, ATT_ROWS)
            qs = q_s[pl.ds(r0, ATT_ROWS), :] * (HEAD_DIM ** -0.5)
            kw = k_s[pl.ds(r0, ATT_KEYS), :]
            vw = v_s[pl.ds(r0, ATT_KEYS), :]
            e, rinv = _att_exp(qs, kw, bias_ref, valid)
            eb = e.astype(BF16)
            outs = [_dot(_rows(eb, h), _head(vw, h)) * _rows(rinv, h) for h in range(HEADS)]
            y_ref[pl.ds(r0, ATT_ROWS), :] = jnp.concatenate(outs, axis=1).astype(BF16)

        _att_blocks(T, block, middle=middle)

    return _call(
        body, payload, name="att_fwd", grid=None, takes_middle=True,
        in_specs=[pl.BlockSpec(memory_space=pl.ANY), pl.BlockSpec(memory_space=pltpu.VMEM)],
        out_specs=[pl.BlockSpec(memory_space=pltpu.VMEM)],
        out_shape=[jax.ShapeDtypeStruct((T, D_ATT), BF16)],
        scratch_shapes=[pltpu.VMEM((T, D_ATT), BF16), pltpu.VMEM((T + ATT_PAD, D_ATT), BF16),
                        pltpu.VMEM((T + ATT_PAD, D_ATT), BF16), pltpu.SemaphoreType.DMA((3,))],
        operands=(qkv, bias2))


def _lanes(v, h):
    return v[:, h * ATT_ROWS:(h + 1) * ATT_ROWS]


def _att_bwd(qkv, dy, bias2t, payload=None):
    T = qkv.shape[0]

    def body(qkv_hbm, dy_ref, bias_ref, dq_ref, dk_ref, dv_ref, db_ref, q_s, k_s, v_s, sem):
        _att_load(qkv_hbm, q_s, k_s, v_s, sem, T)
        dk_ref[...] = jnp.zeros_like(dk_ref)
        dv_ref[...] = jnp.zeros_like(dv_ref)
        db_ref[...] = jnp.zeros_like(db_ref)

        def block(i, valid):
            r0 = pl.multiple_of(i * ATT_ROWS, ATT_ROWS)
            qs = q_s[pl.ds(r0, ATT_ROWS), :] * (HEAD_DIM ** -0.5)
            kw = k_s[pl.ds(r0, ATT_KEYS), :]
            vw = v_s[pl.ds(r0, ATT_KEYS), :]
            dyb = dy_ref[pl.ds(r0, ATT_ROWS), :]
            s = jnp.concatenate([_dot_nt(_head(kw, h), _head(qs, h)) + bias_ref[h] for h in range(HEADS)], axis=1)
            if valid is not None:
                s = jnp.where(valid, s, NEG_INF)
            e = jnp.exp(s - jnp.max(s, axis=0, keepdims=True))
            p = e * (1.0 / jnp.sum(e, axis=0, keepdims=True))
            dp = jnp.concatenate([_dot_nt(_head(vw, h), _head(dyb, h)) for h in range(HEADS)], axis=1)
            ds = p * (dp - jnp.sum(p * dp, axis=0, keepdims=True))
            for h in range(HEADS):
                db_ref[h] += _lanes(ds, h)
            dsb = ds.astype(BF16)
            pb = p.astype(BF16)
            dq = [_dot_tn(_lanes(dsb, h), _head(kw, h)) for h in range(HEADS)]
            dk = [_dot(_lanes(dsb, h), _head(qs, h)) for h in range(HEADS)]
            dv = [_dot(_lanes(pb, h), _head(dyb, h)) for h in range(HEADS)]
            dq_ref[pl.ds(r0, ATT_ROWS), :] = (jnp.concatenate(dq, axis=1) * (HEAD_DIM ** -0.5)).astype(BF16)
            dk_ref[pl.ds(r0, ATT_KEYS), :] += jnp.concatenate(dk, axis=1)
            dv_ref[pl.ds(r0, ATT_KEYS), :] += jnp.concatenate(dv, axis=1)

        _att_blocks(T, block, keys_on_rows=True)

    vmem = pl.BlockSpec(memory_space=pltpu.VMEM)
    return _call(
        body, payload, name="att_bwd", grid=None,
        in_specs=[pl.BlockSpec(memory_space=pl.ANY), vmem, vmem],
        out_specs=[vmem, vmem, vmem, vmem],
        out_shape=[jax.ShapeDtypeStruct((T, D_ATT), BF16), jax.ShapeDtypeStruct((T + ATT_PAD, D_ATT), F32),
                   jax.ShapeDtypeStruct((T + ATT_PAD, D_ATT), F32), jax.ShapeDtypeStruct((HEADS, ATT_KEYS, ATT_ROWS), F32)],
        scratch_shapes=[pltpu.VMEM((T, D_ATT), BF16), pltpu.VMEM((T + ATT_PAD, D_ATT), BF16),
                        pltpu.VMEM((T + ATT_PAD, D_ATT), BF16), pltpu.SemaphoreType.DMA((3,))],
        operands=(qkv, dy, bias2t))


_GELU_C = 0.7978845608028654
_GELU_A = 0.044715


def _gelu(x):
    t = jnp.tanh(_GELU_C * (x + _GELU_A * x * x * x))
    return 0.5 * x * (1.0 + t), t


def _gelu_grad(x, t):
    return 0.5 * (1.0 + t) + 0.5 * x * (1.0 - t * t) * _GELU_C * (1.0 + 3.0 * _GELU_A * x * x)


def _group_masks():
    col = lax.broadcasted_iota(jnp.int32, (SGU_GROUPS, D_SGU), 1) // SGU_GDIM
    grp = lax.broadcasted_iota(jnp.int32, (SGU_GROUPS, D_SGU), 0)
    return jnp.where(col == grp, 1.0, 0.0).astype(F32)


def _causal_mask(transposed=False):
    i = lax.broadcasted_iota(jnp.int32, (SGU_BLOCK, SGU_BLOCK), 0) // CHUNK
    j = lax.broadcasted_iota(jnp.int32, (SGU_BLOCK, SGU_BLOCK), 1) // CHUNK
    return (j >= i) if transposed else (i >= j)


def _sgu_norm(zs, lng, lnb):
    gz, t = _gelu(zs)
    u = gz[:, 0:D_SGU]
    vs = gz[:, D_SGU:2 * D_SGU]
    xc = vs - jnp.mean(vs, axis=-1, keepdims=True)
    rstd = lax.rsqrt(jnp.mean(xc * xc, axis=-1, keepdims=True) + EPS)
    xhat = xc * rstd
    return t, u, xhat, rstd, xhat * lng + lnb


def _sgu_mix(vn_blk, w_ref, bst, gm):
    mask = _causal_mask()
    s = jnp.zeros((SGU_BLOCK, D_SGU), F32)
    for g in range(SGU_GROUPS):
        wm = jnp.where(mask, w_ref[g], 0.0).astype(BF16)
        s += _dot(wm, (vn_blk * gm[g:g + 1, :]).astype(BF16))
        s += bst[:, g:g + 1] * gm[g:g + 1, :]
    return s


def _sgu_fwd(zs, lng, lnb, w_s, bst):
    T = zs.shape[0]
    nblk = TM // SGU_BLOCK

    def body(zs_ref, lng_ref, lnb_ref, w_ref, bst_ref, y_ref):
        _, u, _, _, vn = _sgu_norm(zs_ref[...], lng_ref[...], lnb_ref[...])
        gm = _group_masks()
        bst_v = bst_ref[...]
        for n in range(nblk):
            rows = slice(n * SGU_BLOCK, (n + 1) * SGU_BLOCK)
            s = _sgu_mix(vn[rows], w_ref, bst_v, gm)
            y_ref[rows, :] = (u[rows] * s).astype(BF16)

    return pl.pallas_call(
        body, name="sgu_fwd", grid=(T // TM,),
        in_specs=[_row_spec(TM, 2 * D_SGU), _const_spec((1, D_SGU)), _const_spec((1, D_SGU)),
                  _const_spec(w_s.shape), _const_spec(bst.shape)],
        out_specs=_row_spec(TM, D_SGU),
        out_shape=jax.ShapeDtypeStruct((T, D_SGU), BF16),
        compiler_params=_params(("arbitrary",)),
    )(zs, lng, lnb, w_s, bst)


def _sgu_bwd(zs, dy, lng, lnb, w_s, w_st, bst):
    T = zs.shape[0]
    nblk = TM // SGU_BLOCK

    def body(zs_ref, dy_ref, lng_ref, lnb_ref, w_ref, wt_ref, bst_ref, dzs_ref, dw_ref, dbt_ref, dlg_ref, dlb_ref):
        @pl.when(pl.program_id(0) == 0)
        def _():
            dw_ref[...] = jnp.zeros_like(dw_ref)
            dbt_ref[...] = jnp.zeros_like(dbt_ref)
            dlg_ref[...] = jnp.zeros_like(dlg_ref)
            dlb_ref[...] = jnp.zeros_like(dlb_ref)

        zs_v = zs_ref[...]
        lng_v = lng_ref[...]
        t, u, xhat, rstd, vn = _sgu_norm(zs_v, lng_v, lnb_ref[...])
        gm = _group_masks()
        bst_v = bst_ref[...]
        mask = _causal_mask()
        mask_t = _causal_mask(transposed=True)
        dyv = dy_ref[...].astype(F32)
        lane8 = lax.broadcasted_iota(jnp.int32, (1, SGU_GROUPS), 1)
        du_rows, dvn_rows = [], []
        for n in range(nblk):
            rows = slice(n * SGU_BLOCK, (n + 1) * SGU_BLOCK)
            vn_b = vn[rows]
            s = _sgu_mix(vn_b, w_ref, bst_v, gm)
            du_rows.append(dyv[rows] * s)
            dsb = dyv[rows] * u[rows]
            vnb16 = vn_b.astype(BF16)
            dvn = jnp.zeros((SGU_BLOCK, D_SGU), F32)
            dbt = jnp.zeros((SGU_BLOCK, SGU_GROUPS), F32)
            for g in range(SGU_GROUPS):
                dsg = dsb * gm[g:g + 1, :]
                dsg16 = dsg.astype(BF16)
                wmt = jnp.where(mask_t, wt_ref[g], 0.0).astype(BF16)
                dvn += _dot(wmt, dsg16)
                dw_ref[g] += jnp.where(mask, _dot_nt(dsg16, vnb16), 0.0)
                dbt += jnp.sum(dsg, axis=-1, keepdims=True) * jnp.where(lane8 == g, 1.0, 0.0)
            dbt_ref[...] += dbt
            dvn_rows.append(dvn)
        du = jnp.concatenate(du_rows, axis=0)
        dvn = jnp.concatenate(dvn_rows, axis=0)
        dlg_ref[...] += jnp.sum(dvn * xhat, axis=0, keepdims=True)
        dlb_ref[...] += jnp.sum(dvn, axis=0, keepdims=True)
        dxhat = dvn * lng_v
        dvs = rstd * (dxhat - jnp.mean(dxhat, axis=-1, keepdims=True)
                      - xhat * jnp.mean(dxhat * xhat, axis=-1, keepdims=True))
        dgz = jnp.concatenate([du, dvs], axis=1)
        dzs_ref[...] = (dgz * _gelu_grad(zs_v, t)).astype(BF16)

    return pl.pallas_call(
        body, name="sgu_bwd", grid=(T // TM,),
        in_specs=[_row_spec(TM, 2 * D_SGU), _row_spec(TM, D_SGU), _const_spec((1, D_SGU)), _const_spec((1, D_SGU)),
                  _const_spec(w_s.shape), _const_spec(w_st.shape), _const_spec(bst.shape)],
        out_specs=[_row_spec(TM, 2 * D_SGU), _acc_spec(w_s.shape), _acc_spec(bst.shape), _acc_spec((1, D_SGU)),
                   _acc_spec((1, D_SGU))],
        out_shape=[jax.ShapeDtypeStruct((T, 2 * D_SGU), BF16), jax.ShapeDtypeStruct(w_s.shape, F32),
                   jax.ShapeDtypeStruct(bst.shape, F32), jax.ShapeDtypeStruct((1, D_SGU), F32),
                   jax.ShapeDtypeStruct((1, D_SGU), F32)],
        compiler_params=_params(("arbitrary",)),
    )(zs, dy, lng, lnb, w_s, w_st, bst)


def _cols(v, s):
    return v[:, s * BR_S:(s + 1) * BR_S]


def _merge_fwd(x, y_att, y_sgu, gl, b_gate, wba, wbs, wo, payload=None):
    T = x.shape[0]

    def body(x_ref, ya_ref, ys_ref, gl_ref, bg_ref, wba_ref, wbs_ref, wo_ref, xo_ref, m_ref, pa_ref, ps_ref):
        ya = ya_ref[...]
        ys = ys_ref[...]
        pa = jnp.concatenate([_dot(ya, wba_ref[s]) for s in range(N_SHARD)], axis=1)
        ps = jnp.concatenate([_dot(ys, wbs_ref[s]) for s in range(N_SHARD)], axis=1)
        g = _sigmoid(gl_ref[...] + bg_ref[...])
        mb = (g[:, 0:D_MODEL] * pa + g[:, D_MODEL:2 * D_MODEL] * ps).astype(BF16)
        m_ref[...] = mb
        pa_ref[...] = pa.astype(BF16)
        ps_ref[...] = ps.astype(BF16)
        acc = jnp.zeros((TM, D_MODEL), F32)
        for s in range(N_SHARD):
            acc += _dot(_cols(mb, s), wo_ref[s])
        xo_ref[...] = x_ref[...] + acc

    tokd = jax.ShapeDtypeStruct((T, D_MODEL), BF16)
    return _call(
        body, payload, name="merge_fwd", grid=(T // TM,), when=_edges(T // TM), sem=("arbitrary",),
        in_specs=[_row_spec(TM, D_MODEL), _row_spec(TM, D_ATT), _row_spec(TM, D_SGU), _row_spec(TM, 2 * D_MODEL),
                  _const_spec((1, 2 * D_MODEL)), _const_spec(wba.shape), _const_spec(wbs.shape), _const_spec(wo.shape)],
        out_specs=[_row_spec(TM, D_MODEL)] * 4,
        out_shape=[jax.ShapeDtypeStruct((T, D_MODEL), F32), tokd, tokd, tokd],
        operands=(x, y_att, y_sgu, gl, b_gate, wba, wbs, wo))


def _merge_bwd(dx, y_att, y_sgu, gl, merged, pa, ps, b_gate, wba, wbs, wo, payload=None):
    T = dx.shape[0]

    def body(dx_ref, ya_ref, ys_ref, gl_ref, m_ref, pa_ref, ps_ref, bg_ref, wba_ref, wbs_ref, wo_ref,
             dya_ref, dys_ref, dgl_ref, dbg_ref, gwba_ref, gwbs_ref, gwo_ref):
        @pl.when(pl.program_id(0) == 0)
        def _():
            dbg_ref[...] = jnp.zeros_like(dbg_ref)
            gwba_ref[...] = jnp.zeros_like(gwba_ref)
            gwbs_ref[...] = jnp.zeros_like(gwbs_ref)
            gwo_ref[...] = jnp.zeros_like(gwo_ref)

        dxb = dx_ref[...].astype(BF16)
        dm = jnp.concatenate([_dot_nt(dxb, wo_ref[s]) for s in range(N_SHARD)], axis=1)
        g = _sigmoid(gl_ref[...] + bg_ref[...])
        ga = g[:, 0:D_MODEL]
        gs = g[:, D_MODEL:2 * D_MODEL]
        dpa = (dm * ga).astype(BF16)
        dps = (dm * gs).astype(BF16)
        dgl = jnp.concatenate([dm * pa_ref[...].astype(F32) * ga * (1.0 - ga),
                               dm * ps_ref[...].astype(F32) * gs * (1.0 - gs)], axis=1)
        dgl_ref[...] = dgl.astype(BF16)
        dbg_ref[...] += jnp.sum(dgl, axis=0, keepdims=True)
        ya = ya_ref[...]
        ys = ys_ref[...]
        mb = m_ref[...]
        dya = jnp.zeros((TM, D_ATT), F32)
        dys = jnp.zeros((TM, D_SGU), F32)
        for s in range(N_SHARD):
            dya += _dot_nt(_cols(dpa, s), wba_ref[s])
            dys += _dot_nt(_cols(dps, s), wbs_ref[s])
            gwo_ref[s] += _dot_tn(_cols(mb, s), dxb)
            gwba_ref[s] += _dot_tn(ya, _cols(dpa, s))
            gwbs_ref[s] += _dot_tn(ys, _cols(dps, s))
        dya_ref[...] = dya.astype(BF16)
        dys_ref[...] = dys.astype(BF16)

    return _call(
        body, payload, name="merge_bwd", grid=(T // TM,), when=_edges(T // TM), sem=("arbitrary",),
        operands=(dx, y_att, y_sgu, gl, merged, pa, ps, b_gate, wba, wbs, wo),
        in_specs=[_row_spec(TM, D_MODEL), _row_spec(TM, D_ATT), _row_spec(TM, D_SGU), _row_spec(TM, 2 * D_MODEL),
                  _row_spec(TM, D_MODEL), _row_spec(TM, D_MODEL), _row_spec(TM, D_MODEL),
                  _const_spec((1, 2 * D_MODEL)), _const_spec(wba.shape), _const_spec(wbs.shape), _const_spec(wo.shape)],
        out_specs=[_row_spec(TM, D_ATT), _row_spec(TM, D_SGU), _row_spec(TM, 2 * D_MODEL), _acc_spec((1, 2 * D_MODEL)),
                   _acc_spec(wba.shape), _acc_spec(wbs.shape), _acc_spec(wo.shape)],
        out_shape=[jax.ShapeDtypeStruct((T, D_ATT), BF16), jax.ShapeDtypeStruct((T, D_SGU), BF16),
                   jax.ShapeDtypeStruct((T, 2 * D_MODEL), BF16), jax.ShapeDtypeStruct((1, 2 * D_MODEL), F32),
                   jax.ShapeDtypeStruct(wba.shape, F32), jax.ShapeDtypeStruct(wbs.shape, F32),
                   jax.ShapeDtypeStruct(wo.shape, F32)])


def _loss_bwd(x, target, g):
    T = x.shape[0]

    def body(x_ref, t_ref, g_ref, dx_ref, loss_ref, dg_ref):
        @pl.when(pl.program_id(0) == 0)
        def _():
            loss_ref[...] = jnp.zeros_like(loss_ref)
            dg_ref[...] = jnp.zeros_like(dg_ref)

        gv = g_ref[...]
        xhat, r, y = _rms_fwd(x_ref[...], gv)
        err = y - t_ref[...]
        per_tok = jnp.mean(err * err, axis=-1, keepdims=True)
        loss_ref[...] += 0.5 * jnp.sum(per_tok, axis=0, keepdims=True)
        dxn, dg = _rms_bwd(err * (1.0 / D_MODEL), xhat, r, gv)
        dx_ref[...] = dxn
        dg_ref[...] += dg

    return pl.pallas_call(
        body, name="loss_bwd", grid=(T // TM,),
        in_specs=[_row_spec(TM, D_MODEL), _row_spec(TM, D_MODEL), _const_spec((1, D_MODEL))],
        out_specs=[_row_spec(TM, D_MODEL), _acc_spec((1, 128)), _acc_spec((1, D_MODEL))],
        out_shape=[jax.ShapeDtypeStruct((T, D_MODEL), F32), jax.ShapeDtypeStruct((1, 128), F32),
                   jax.ShapeDtypeStruct((1, D_MODEL), F32)],
        compiler_params=_params(("arbitrary",)),
    )(x, target, g)


BIG = ("ffn1_w_gate", "ffn1_w_up", "ffn1_w_down", "w_in", "w_branch_att", "w_branch_sgu", "w_out",
       "ffn2_w_gate", "ffn2_w_up", "ffn2_w_down")
SMALL = ("norm_ffn1", "norm_mix", "b_gate", "rel_bias", "sgu_ln_g", "sgu_ln_b", "sgu_w_s", "sgu_b_s", "norm_ffn2",
         "norm_final")


G_FFN1 = ("ffn1_w_gate", "ffn1_w_up", "ffn1_w_down")
G_MIX = ("w_in", "w_branch_att", "w_branch_sgu", "w_out")
G_FFN2 = ("ffn2_w_gate", "ffn2_w_up", "ffn2_w_down")


def _local_step(x, target, wb, ws, dist=None):
    def gather_on(names):
        return _ag_payload([wb[n] for n in names]) if dist else None

    t2 = _relbias_fwd(ws["rel_bias"])
    bias2 = _bias_blocks(t2)
    bst = ws["sgu_b_s"].T
    w_st = jnp.swapaxes(ws["sgu_w_s"], 1, 2)

    if dist:
        wb.update(zip(G_FFN1, _call(lambda: None, gather_on(G_FFN1), name="allgather_ffn1", grid=None, in_specs=[],
                                    out_specs=[], out_shape=[])))
    x1, h1, a1, b1, *got = _ffn_fwd(x, ws["norm_ffn1"], wb["ffn1_w_gate"], wb["ffn1_w_up"], wb["ffn1_w_down"],
                                    "ffn1_fwd", gather_on(G_MIX))
    wb.update(zip(G_MIX, got))
    h2, qkv, zs, gl, *got = _in_fwd(x1, ws["norm_mix"], wb["w_in"], gather_on(G_FFN2[0:1]))
    wb.update(zip(G_FFN2[0:1], got))
    y_att, *got = _att_fwd(qkv, bias2, gather_on(G_FFN2[1:2]))
    wb.update(zip(G_FFN2[1:2], got))
    y_sgu = _sgu_fwd(zs, ws["sgu_ln_g"], ws["sgu_ln_b"], ws["sgu_w_s"], bst)
    x2, merged, pa, ps, *got = _merge_fwd(x1, y_att, y_sgu, gl, ws["b_gate"], wb["w_branch_att"], wb["w_branch_sgu"],
                                          wb["w_out"], gather_on(G_FFN2[2:3]))
    wb.update(zip(G_FFN2[2:3], got))
    x3, h3, a3, b3 = _ffn_fwd(x2, ws["norm_ffn2"], wb["ffn2_w_gate"], wb["ffn2_w_up"], wb["ffn2_w_down"], "ffn2_fwd")
    dx3, loss, g_final = _loss_bwd(x3, target, ws["norm_final"])

    gb, gs, sums = {}, {"norm_final": g_final}, {}

    def pair_on(names, small=None):
        return _px_payload([gb[n] for n in names], small) if dist else None

    def pair_add(names, halves):
        for n, rv in zip(names, halves):
            sums[n] = _pair_add(gb[n], rv, dist[0], dist[1], "pair_add_" + n)

    def chips_on(names):
        return _cx_payload([sums[n][1] for n in names], [sums[n][2] for n in names]) if dist else None

    dx2, da3, db3, gs["norm_ffn2"] = _ffn_dgrad(dx3, x2, a3, b3, ws["norm_ffn2"], wb["ffn2_w_gate"], wb["ffn2_w_up"],
                                                wb["ffn2_w_down"], "ffn2_dgrad")
    gb["ffn2_w_gate"], gb["ffn2_w_up"], gb["ffn2_w_down"] = _ffn_wgrad(h3, dx3, a3, b3, da3, db3, "ffn2_wgrad")
    dy_att, dy_sgu, dgl, gs["b_gate"], gb["w_branch_att"], gb["w_branch_sgu"], gb["w_out"], *got = _merge_bwd(
        dx2, y_att, y_sgu, gl, merged, pa, ps, ws["b_gate"], wb["w_branch_att"], wb["w_branch_sgu"], wb["w_out"],
        pair_on(G_FFN2))
    pair_add(G_FFN2, got)
    dq, dk, dv, db2t, *lands2 = _att_bwd(qkv, dy_att, jnp.swapaxes(bias2, 1, 2), chips_on(G_FFN2))
    gs["rel_bias"] = _relbias_bwd(_unskew(jnp.swapaxes(db2t, 1, 2)))
    dzs, gs["sgu_w_s"], dbt, gs["sgu_ln_g"], gs["sgu_ln_b"] = _sgu_bwd(zs, dy_sgu, ws["sgu_ln_g"], ws["sgu_ln_b"],
                                                                      ws["sgu_w_s"], w_st, bst)
    gs["sgu_b_s"] = dbt.T
    dx1, dz, gs["norm_mix"] = _in_dgrad(dx2, x1, ws["norm_mix"], wb["w_in"], dq, dk, dv, dzs, dgl)
    gb["w_in"] = _in_wgrad(h2, dz)
    gx, da1, db1, gs["norm_ffn1"], *got = _ffn_dgrad(dx1, x, a1, b1, ws["norm_ffn1"], wb["ffn1_w_gate"],
                                                    wb["ffn1_w_up"], wb["ffn1_w_down"], "ffn1_dgrad", pair_on(G_MIX))
    pair_add(G_MIX, got)
    gb["ffn1_w_gate"], gb["ffn1_w_up"], gb["ffn1_w_down"], *lands_mix = _ffn_wgrad(h1, dx1, a1, b1, da1, db1,
                                                                                   "ffn1_wgrad", chips_on(G_MIX))
    if not dist:
        return loss, gx, gb, gs

    def final_sums(names, lands):
        return [_final_sum(sums[n][0], land, dist[1], dist[0], "final_sum_" + n) for n, land in zip(names, lands)]

    early = G_FFN2 + G_MIX
    k = len(G_FFN1)
    tail = _tail_reduce([gb[n] for n in G_FFN1], _pack_small(gs, loss),
                        _ss_payload(final_sums(early, list(lands2) + list(lands_mix))))
    for i, n in enumerate(G_FFN1):
        sums[n] = (tail[i],)
    small_sums, shared = tail[2 * k], dict(zip(early, tail[2 * k + 1:]))
    shared.update(zip(G_FFN1, _sibling_share(final_sums(G_FFN1, tail[k:2 * k]), "sibling_share")))
    return loss, gx, shared, small_sums


_ANY = pl.BlockSpec(memory_space=pl.ANY)
_VMEM = pl.BlockSpec(memory_space=pltpu.VMEM)


def _mesh_pos():
    return lax.axis_index("x"), lax.axis_index("y"), lax.axis_index("c")


def _cast_slots(shards, chip, name):
    n = len(shards)
    r, ncol = shards[0].shape
    tr = r // 2

    def body(me_ref, *refs):
        for i_ref, o_ref in zip(refs[:n], refs[n:]):
            o_ref[0] = i_ref[...].astype(BF16)

    grid_spec = pltpu.PrefetchScalarGridSpec(
        num_scalar_prefetch=1, grid=(r // tr,),
        in_specs=[pl.BlockSpec((tr, ncol), lambda i, me: (i, 0))] * n,
        out_specs=[pl.BlockSpec((1, tr, ncol), lambda i, me: (me[0], i, 0))] * n)
    return pl.pallas_call(
        body, name=name, grid_spec=grid_spec,
        out_shape=[jax.ShapeDtypeStruct((N_SHARD, r, ncol), BF16)] * n,
        compiler_params=_params(("arbitrary",)),
    )(chip, *shards)


class _Payload:
    def __init__(self, arrays, out_shapes, aliases, scratch, phases):
        self.arrays = list(arrays)
        self.out_shapes = list(out_shapes)
        self.aliases = dict(aliases)
        self.scratch = list(scratch)
        self.phases = phases


def _remote(src, dst, ssem, rsem, dev):
    return pltpu.make_async_remote_copy(src_ref=src, dst_ref=dst, send_sem=ssem, recv_sem=rsem, device_id=dev,
                                        device_id_type=MESH)


def _call(body, payload, *, name, grid, in_specs, out_specs, out_shape, scratch_shapes=(), sem=None, when=None,
          operands=(), takes_middle=False):
    in_specs, out_specs, out_shape = list(in_specs), list(out_specs), list(out_shape)
    scratch_shapes = list(scratch_shapes)
    n_in, n_out, n_scr = len(in_specs), len(out_specs), len(scratch_shapes)
    kwargs = {}
    kernel = body
    if payload is not None:
        k_in, k_out = len(payload.arrays), len(payload.out_shapes)
        rank = len(grid) if grid else 0

        def kernel(*refs):
            a, b = n_in, n_in + k_in
            c, d = b + n_out, b + n_out + k_out
            e = d + n_scr
            phases = payload.phases(refs[a:b], refs[c:d], refs[e:])

            def run():
                body(*refs[:a], *refs[b:c], *refs[d:e])

            if not grid:
                phases[0]()
                if len(phases) == 3 and takes_middle:
                    body(*refs[:a], *refs[b:c], *refs[d:e], middle=phases[1])
                    phases[2]()
                    return
                run()
                for ph in phases[1:]:
                    ph()
                return
            step = pl.program_id(0)
            if rank == 2:
                step = step * grid[1] + pl.program_id(1)
            marks = list(when)
            if len(phases) == 3:
                marks = [when[0], (max(when[1][0] - 3, 0), False), when[1]]
            for ph, (at, before) in zip(phases, marks):
                if before:
                    pl.when(step == at)(ph)
            run()
            for ph, (at, before) in zip(phases, marks):
                if not before:
                    pl.when(step == at)(ph)

        in_specs += [_ANY] * k_in
        out_specs += [_ANY] * k_out
        out_shape += payload.out_shapes
        scratch_shapes += payload.scratch
        kwargs["input_output_aliases"] = {n_in + i: n_out + j for i, j in payload.aliases.items()}
        operands = tuple(operands) + tuple(payload.arrays)
    if grid:
        kwargs["grid"] = grid
    return pl.pallas_call(kernel, name=name, in_specs=in_specs, out_specs=out_specs, out_shape=out_shape,
                          scratch_shapes=scratch_shapes, compiler_params=_params(sem), **kwargs)(*operands)


def _ag_payload(slots):
    n = len(slots)

    def phases(_, refs, sems):
        send_i, recv_i, send_d, recv_d = sems
        x, y, c = _mesh_pos()
        me = 2 * x + y

        def half(w, core):
            rh = slots[w].shape[1] // 2
            return pl.ds(core * rh, rh)

        def ici(w, j):
            t = (me + 1 + j) % N_SHARD
            mine = refs[w].at[me, half(w, c), :]
            return _remote(mine, mine, send_i.at[3 * w + j], recv_i.at[3 * w + j], (t // 2, t % 2, c))

        def d2d(w, j, core):
            s = (me + 3 - j) % N_SHARD
            land = refs[w].at[s, half(w, core), :]
            return _remote(land, land, send_d.at[3 * w + j], recv_d.at[3 * w + j], (x, y, 1 - c))

        def start():
            for w in range(n):
                for j in range(3):
                    ici(w, j).start()

        def forward():
            for w in range(n):
                for j in range(3):
                    s = (me + 3 - j) % N_SHARD
                    land = refs[w].at[s, half(w, c), :]
                    _remote(land, land, send_i.at[3 * w + j], recv_i.at[3 * w + j], (x, y, c)).wait_recv()
                    d2d(w, j, c).start()

        def finish():
            for w in range(n):
                for j in range(3):
                    d2d(w, j, 1 - c).wait_recv()
            for w in range(n):
                for j in range(3):
                    ici(w, j).wait_send()
                    d2d(w, j, c).wait_send()

        return [start, forward, finish]

    return _Payload(slots, [jax.ShapeDtypeStruct(s.shape, s.dtype) for s in slots], {i: i for i in range(n)},
                    [pltpu.SemaphoreType.DMA((3 * n,)) for _ in range(4)], phases)


def _px_payload(grads, small=None):
    arrays = list(grads) + ([small] if small is not None else [])
    n = len(arrays)

    def phases(ins, outs, sems):
        send, recv = sems
        x, y, c = _mesh_pos()

        def copy(w):
            if w < len(grads):
                rh = grads[w].shape[1] // 2
                src = ins[w].at[:, pl.ds((1 - c) * rh, rh), :]
            else:
                src = ins[w]
            return _remote(src, outs[w], send.at[w], recv.at[w], (x, y, 1 - c))

        def start():
            for w in range(n):
                copy(w).start()

        def finish():
            for w in range(n):
                copy(w).wait()

        return [start, finish]

    out_shapes = [jax.ShapeDtypeStruct((N_SHARD, g.shape[1] // 2, g.shape[2]), F32) for g in grads]
    if small is not None:
        out_shapes.append(jax.ShapeDtypeStruct(small.shape, F32))
    return _Payload(arrays, out_shapes, {}, [pltpu.SemaphoreType.DMA((n,)), pltpu.SemaphoreType.DMA((n,))], phases)


def _cx_payload(pbs, lands):
    n = len(pbs)

    def phases(ins, outs, sems):
        send, recv = sems
        x, y, c = _mesh_pos()
        me = 2 * x + y

        def copy(w, j):
            t = (me + 1 + j) % N_SHARD
            return _remote(ins[w].at[t], outs[w].at[me], send.at[3 * w + j], recv.at[3 * w + j], (t // 2, t % 2, c))

        def start():
            for w in range(n):
                for j in range(3):
                    copy(w, j).start()

        def finish():
            for w in range(n):
                for j in range(3):
                    copy(w, j).wait()

        return [start, finish]

    return _Payload(list(pbs) + list(lands), [jax.ShapeDtypeStruct(p.shape, BF16) for p in lands],
                    {n + i: i for i in range(n)},
                    [pltpu.SemaphoreType.DMA((3 * n,)), pltpu.SemaphoreType.DMA((3 * n,))], phases)


def _pair_add(g, rv, core, chip, name):
    _, r, ncol = g.shape
    rh = r // 2

    def body(c_ref, me_ref, g_ref, rv_ref, pf_ref, pb_ref, land_ref):
        s = g_ref[0] + rv_ref[0]
        sb = s.astype(BF16)
        pb_ref[0] = sb

        @pl.when(pl.program_id(0) == me_ref[0])
        def _():
            pf_ref[...] = s
            land_ref[0] = sb

    slot = pl.BlockSpec((1, rh, ncol), lambda s, c, me: (s, 0, 0))
    grid_spec = pltpu.PrefetchScalarGridSpec(
        num_scalar_prefetch=2, grid=(N_SHARD,),
        in_specs=[pl.BlockSpec((1, rh, ncol), lambda s, c, me: (s, c[0], 0)), slot],
        out_specs=[pl.BlockSpec((rh, ncol), lambda s, c, me: (0, 0)), slot,
                   pl.BlockSpec((1, rh, ncol), lambda s, c, me: (me[0], 0, 0))])
    return pl.pallas_call(
        body, name=name, grid_spec=grid_spec,
        out_shape=[jax.ShapeDtypeStruct((rh, ncol), F32), jax.ShapeDtypeStruct((N_SHARD, rh, ncol), BF16),
                   jax.ShapeDtypeStruct((N_SHARD, rh, ncol), BF16)],
        compiler_params=_params(("arbitrary",)),
    )(core, chip, g, rv)


def _tail_reduce(grads, small, payload=None):
    n = len(grads)
    _, r, ncol = grads[0].shape
    rh = r // 2

    def body(*refs, middle=None):
        g_hbm, sm = refs[:n], refs[n]
        pf, land, sm_out = refs[n + 1:2 * n + 1], refs[2 * n + 1:3 * n + 1], refs[3 * n + 1]
        scr = refs[3 * n + 2:]
        rv, mine, sendb = scr[:n], scr[n:2 * n], scr[2 * n:3 * n]
        sm_rv, sm_sum, d_send, d_recv, load, i_send, i_recv, store = scr[3 * n:]
        x, y, c = _mesh_pos()
        me = 2 * x + y
        sib = (x, y, 1 - c)

        def pair(w):
            src = g_hbm[w].at[:, pl.ds((1 - c) * rh, rh), :] if w < n else sm
            return _remote(src, rv[w] if w < n else sm_rv, d_send.at[w], d_recv.at[w], sib)

        def chips(w, j):
            t = (me + 1 + j) % N_SHARD
            src = sendb[w].at[t] if w < n else sm_sum
            dst = land[w].at[me] if w < n else sm_out.at[me]
            return _remote(src, dst, i_send.at[3 * w + j], i_recv.at[3 * w + j], (t // 2, t % 2, c))

        loads = [pltpu.make_async_copy(g_hbm[w].at[:, pl.ds(c * rh, rh), :], mine[w], load.at[w]) for w in range(n)]
        for w in range(n + 1):
            pair(w).start()
        for cp in loads:
            cp.start()
        if middle is not None:
            middle()
        stores = []
        for w in range(n):
            loads[w].wait()
            pair(w).wait_recv()
            for k in range(N_SHARD):
                s = mine[w][k] + rv[w][k]
                mine[w][k] = s
                sendb[w][k] = s.astype(BF16)
            stores += [pltpu.make_async_copy(mine[w].at[me], pf[w], store.at[2 * w]),
                       pltpu.make_async_copy(sendb[w].at[me], land[w].at[me], store.at[2 * w + 1])]
            for cp in stores[-2:]:
                cp.start()
            for j in range(3):
                chips(w, j).start()
        pair(n).wait_recv()
        sm_sum[...] = sm[...] + sm_rv[...]
        stores.append(pltpu.make_async_copy(sm_sum, sm_out.at[me], store.at[2 * n]))
        stores[-1].start()
        for j in range(3):
            chips(n, j).start()
        for w in range(n + 1):
            pair(w).wait_send()
            for j in range(3):
                chips(w, j).wait()
        for cp in stores:
            cp.wait()

    half = (N_SHARD, rh, ncol)
    return _call(
        body, payload, name="tail_reduce", grid=None, takes_middle=True,
        in_specs=[_ANY] * n + [_VMEM], out_specs=[_ANY] * (2 * n + 1),
        out_shape=([jax.ShapeDtypeStruct((rh, ncol), F32)] * n + [jax.ShapeDtypeStruct(half, BF16)] * n
                   + [jax.ShapeDtypeStruct((N_SHARD,) + small.shape, F32)]),
        scratch_shapes=([pltpu.VMEM(half, F32)] * (2 * n) + [pltpu.VMEM(half, BF16)] * n
                        + [pltpu.VMEM(small.shape, F32), pltpu.VMEM(small.shape, F32),
                           pltpu.SemaphoreType.DMA((n + 1,)), pltpu.SemaphoreType.DMA((n + 1,)),
                           pltpu.SemaphoreType.DMA((n,)), pltpu.SemaphoreType.DMA((3 * n + 3,)),
                           pltpu.SemaphoreType.DMA((3 * n + 3,)), pltpu.SemaphoreType.DMA((2 * n + 1,))]),
        operands=(*grads, small))


def _final_sum(pf, land, chip, core, name):
    _, rh, ncol = land.shape

    def body(me_ref, c_ref, pf_ref, land_ref, o_ref):
        me = me_ref[0]
        acc = jnp.zeros((rh, ncol), F32)
        for k in range(N_SHARD):
            acc = acc + jnp.where(me == k, pf_ref[...], land_ref[k].astype(F32))
        o_ref[...] = acc

    grid_spec = pltpu.PrefetchScalarGridSpec(
        num_scalar_prefetch=2, grid=(1,),
        in_specs=[pl.BlockSpec((rh, ncol), lambda i, me, c: (0, 0)),
                  pl.BlockSpec((N_SHARD, rh, ncol), lambda i, me, c: (0, 0, 0))],
        out_specs=pl.BlockSpec((rh, ncol), lambda i, me, c: (c[0], 0)))
    return pl.pallas_call(
        body, name=name, grid_spec=grid_spec, out_shape=jax.ShapeDtypeStruct((2 * rh, ncol), F32),
        compiler_params=_params(("arbitrary",)),
    )(chip, core, pf, land)


def _ss_payload(fulls):
    n = len(fulls)

    def phases(_, outs, sems):
        send, recv = sems
        x, y, c = _mesh_pos()

        def copy(w):
            rh = fulls[w].shape[0] // 2
            mine = outs[w].at[pl.ds(c * rh, rh), :]
            return _remote(mine, mine, send.at[w], recv.at[w], (x, y, 1 - c))

        def start():
            for w in range(n):
                copy(w).start()

        def finish():
            for w in range(n):
                copy(w).wait()

        return [lambda: None, start, finish]

    return _Payload(fulls, [jax.ShapeDtypeStruct(f.shape, F32) for f in fulls], {i: i for i in range(n)},
                    [pltpu.SemaphoreType.DMA((n,)), pltpu.SemaphoreType.DMA((n,))], phases)


def _sibling_share(fulls, name):
    return _call(lambda: None, _ss_payload(fulls), name=name, grid=None, in_specs=[], out_specs=[], out_shape=[])


_ROW = {"rel_bias": 128, "sgu_b_s": 136, "norm_ffn1": 144, "norm_mix": 145, "norm_ffn2": 146, "norm_final": 147,
        "b_gate": 148, "sgu_ln_g": 150, "sgu_ln_b": 151}


def _pack_small(gs, loss):
    def body(ws, rel, bs, n1, nm, n2, nf, bg, lg, lb, loss_ref, o_ref):
        o_ref[...] = jnp.zeros_like(o_ref)
        o_ref[LOSS_ROW:LOSS_ROW + 1, 0:128] = loss_ref[...]
        for g in range(SGU_GROUPS):
            o_ref[0:SGU_BLOCK, g * SGU_BLOCK:(g + 1) * SGU_BLOCK] = ws[g]
        o_ref[128:136, 0:REL_PAD] = rel[...]
        o_ref[136:144, 0:SGU_BLOCK] = bs[...]
        o_ref[144:145, :] = n1[...]
        o_ref[145:146, :] = nm[...]
        o_ref[146:147, :] = n2[...]
        o_ref[147:148, :] = nf[...]
        o_ref[148:149, :] = bg[:, 0:D_MODEL]
        o_ref[149:150, :] = bg[:, D_MODEL:2 * D_MODEL]
        o_ref[150:151, 0:D_SGU] = lg[...]
        o_ref[151:152, 0:D_SGU] = lb[...]

    order = ("sgu_w_s", "rel_bias", "sgu_b_s", "norm_ffn1", "norm_mix", "norm_ffn2", "norm_final", "b_gate", "sgu_ln_g",
             "sgu_ln_b")
    return pl.pallas_call(body, name="pack_small", out_shape=jax.ShapeDtypeStruct((SMALL_ROWS, D_MODEL), F32))(
        *[gs[k] for k in order], loss)


def _adam(w, g, m, v):
    m2 = ADAM_B1 * m + (1.0 - ADAM_B1) * g
    v2 = ADAM_B2 * v + (1.0 - ADAM_B2) * (g * g)
    m_hat = m2 / (1.0 - ADAM_B1 ** ADAM_STEP)
    v_hat = v2 / (1.0 - ADAM_B2 ** ADAM_STEP)
    delta = -ADAM_LR * (m_hat / (jnp.sqrt(v_hat) + ADAM_EPS) + ADAM_WD * w)
    return delta, m2, v2


def _adam_small(sin, w, m, v):
    names = SMALL
    k = len(names)

    def body(*refs):
        sin_ref = refs[0]
        w_r, m_r, v_r = refs[1:1 + k], refs[1 + k:1 + 2 * k], refs[1 + 2 * k:1 + 3 * k]
        outs = refs[1 + 3 * k:]
        tot = sin_ref[0] + sin_ref[1] + sin_ref[2] + sin_ref[3]
        outs[4 * k][...] = tot[LOSS_ROW:LOSS_ROW + 1, 0:128]
        for i, name in enumerate(names):
            o = outs[4 * i:4 * i + 4]
            if name == "sgu_w_s":
                for gi in range(SGU_GROUPS):
                    g = tot[0:SGU_BLOCK, gi * SGU_BLOCK:(gi + 1) * SGU_BLOCK]
                    res = (g,) + _adam(w_r[i][gi], g, m_r[i][gi], v_r[i][gi])
                    for ref, val in zip(o, res):
                        ref[gi] = val
                continue
            r0 = _ROW[name]
            if name == "rel_bias":
                g = tot[r0:r0 + HEADS, 0:REL_PAD]
            elif name == "sgu_b_s":
                g = tot[r0:r0 + SGU_GROUPS, 0:SGU_BLOCK]
            elif name == "b_gate":
                g = jnp.concatenate([tot[r0:r0 + 1, :], tot[r0 + 1:r0 + 2, :]], axis=1)
            elif name in ("sgu_ln_g", "sgu_ln_b"):
                g = tot[r0:r0 + 1, 0:D_SGU]
            else:
                g = tot[r0:r0 + 1, :]
            res = (g,) + _adam(w_r[i][...], g, m_r[i][...], v_r[i][...])
            for ref, val in zip(o, res):
                ref[...] = val

    out_shape = []
    for name in names:
        out_shape += [jax.ShapeDtypeStruct(w[name].shape, F32)] * 4
    out_shape.append(jax.ShapeDtypeStruct((1, 128), F32))
    flat = pl.pallas_call(body, name="adam_small", out_shape=out_shape, compiler_params=_params())(
        sin, *[w[n] for n in names], *[m[n] for n in names], *[v[n] for n in names])
    return {name: tuple(flat[4 * i:4 * i + 4]) for i, name in enumerate(names)}, flat[4 * k]


def _adam_big(w, g, m, v, name):
    r, ncol = w.shape
    tr = 256 if r % 256 == 0 else r // 2

    def body(w_ref, g_ref, m_ref, v_ref, g2_ref, d_ref, m2_ref, v2_ref):
        gv = g_ref[...]
        g2_ref[...] = gv
        d_ref[...], m2_ref[...], v2_ref[...] = _adam(w_ref[...], gv, m_ref[...], v_ref[...])

    spec = pl.BlockSpec((tr, ncol), lambda i: (i, 0))
    return pl.pallas_call(
        body, name=name, grid=(r // tr,), in_specs=[spec] * 4, out_specs=[spec] * 4,
        out_shape=[jax.ShapeDtypeStruct(w.shape, F32)] * 4, compiler_params=_params(("arbitrary",)),
    )(w, g, m, v)


WEIGHTS = ("norm_ffn1", "ffn1_w_gate", "ffn1_w_up", "ffn1_w_down", "norm_mix", "w_in", "b_gate", "rel_bias", "sgu_ln_g",
           "sgu_ln_b", "sgu_w_s", "sgu_b_s", "w_branch_att", "w_branch_sgu", "w_out", "norm_ffn2", "ffn2_w_gate",
           "ffn2_w_up", "ffn2_w_down", "norm_final")


GATE_UP = ("ffn1_w_gate", "ffn1_w_up", "ffn2_w_gate", "ffn2_w_up")
_FFN = ("ffn1_w_gate", "ffn1_w_up", "ffn1_w_down", "ffn2_w_gate", "ffn2_w_up", "ffn2_w_down")
_CAST_GROUPS = ((_FFN, "cast_ffn"), (("w_in",), "cast_w_in"), (("w_branch_att", "w_branch_sgu"), "cast_branch"),
                (("w_out",), "cast_w_out"))


def _big_form(name, a):
    return jnp.swapaxes(a, 1, 2)[0] if name in GATE_UP else a[0]


def _big_back(name, a):
    return jnp.swapaxes(a[None], 1, 2) if name in GATE_UP else a[None]


def _small_form(name, a):
    if name == "norm_final":
        return a.reshape(1, D_MODEL)
    if name == "rel_bias":
        return jnp.pad(a[0], ((0, 0), (0, REL_PAD - N_REL)))
    if name in ("sgu_w_s", "sgu_b_s"):
        return a[0]
    return a


def _small_back(name, a, like):
    if name == "rel_bias":
        a = a[:, :N_REL]
    return a.reshape(like.shape)


def kernel(x, norm_ffn1, ffn1_w_gate, ffn1_w_up, ffn1_w_down, norm_mix, w_in, b_gate, rel_bias, sgu_ln_g, sgu_ln_b, sgu_w_s, sgu_b_s, w_branch_att, w_branch_sgu, w_out, norm_ffn2, ffn2_w_gate, ffn2_w_up, ffn2_w_down, norm_final, loss_target, m_norm_ffn1, m_ffn1_w_gate, m_ffn1_w_up, m_ffn1_w_down, m_norm_mix, m_w_in, m_b_gate, m_rel_bias, m_sgu_ln_g, m_sgu_ln_b, m_sgu_w_s, m_sgu_b_s, m_w_branch_att, m_w_branch_sgu, m_w_out, m_norm_ffn2, m_ffn2_w_gate, m_ffn2_w_up, m_ffn2_w_down, m_norm_final, v_norm_ffn1, v_ffn1_w_gate, v_ffn1_w_up, v_ffn1_w_down, v_norm_mix, v_w_in, v_b_gate, v_rel_bias, v_sgu_ln_g, v_sgu_ln_b, v_sgu_w_s, v_sgu_b_s, v_w_branch_att, v_w_branch_sgu, v_w_out, v_norm_ffn2, v_ffn2_w_gate, v_ffn2_w_up, v_ffn2_w_down, v_norm_final):
    w = dict(norm_ffn1=norm_ffn1, ffn1_w_gate=ffn1_w_gate, ffn1_w_up=ffn1_w_up, ffn1_w_down=ffn1_w_down, norm_mix=norm_mix,
             w_in=w_in, b_gate=b_gate, rel_bias=rel_bias, sgu_ln_g=sgu_ln_g, sgu_ln_b=sgu_ln_b, sgu_w_s=sgu_w_s,
             sgu_b_s=sgu_b_s, w_branch_att=w_branch_att, w_branch_sgu=w_branch_sgu, w_out=w_out, norm_ffn2=norm_ffn2,
             ffn2_w_gate=ffn2_w_gate, ffn2_w_up=ffn2_w_up, ffn2_w_down=ffn2_w_down, norm_final=norm_final)
    m = dict(norm_ffn1=m_norm_ffn1, ffn1_w_gate=m_ffn1_w_gate, ffn1_w_up=m_ffn1_w_up, ffn1_w_down=m_ffn1_w_down,
             norm_mix=m_norm_mix, w_in=m_w_in, b_gate=m_b_gate, rel_bias=m_rel_bias, sgu_ln_g=m_sgu_ln_g,
             sgu_ln_b=m_sgu_ln_b, sgu_w_s=m_sgu_w_s, sgu_b_s=m_sgu_b_s, w_branch_att=m_w_branch_att,
             w_branch_sgu=m_w_branch_sgu, w_out=m_w_out, norm_ffn2=m_norm_ffn2, ffn2_w_gate=m_ffn2_w_gate,
             ffn2_w_up=m_ffn2_w_up, ffn2_w_down=m_ffn2_w_down, norm_final=m_norm_final)
    v = dict(norm_ffn1=v_norm_ffn1, ffn1_w_gate=v_ffn1_w_gate, ffn1_w_up=v_ffn1_w_up, ffn1_w_down=v_ffn1_w_down,
             norm_mix=v_norm_mix, w_in=v_w_in, b_gate=v_b_gate, rel_bias=v_rel_bias, sgu_ln_g=v_sgu_ln_g,
             sgu_ln_b=v_sgu_ln_b, sgu_w_s=v_sgu_w_s, sgu_b_s=v_sgu_b_s, w_branch_att=v_w_branch_att,
             w_branch_sgu=v_w_branch_sgu, w_out=v_w_out, norm_ffn2=v_norm_ffn2, ffn2_w_gate=v_ffn2_w_gate,
             ffn2_w_up=v_ffn2_w_up, ffn2_w_down=v_ffn2_w_down, norm_final=v_norm_final)

    core = lax.axis_index("c").astype(jnp.int32).reshape(1)
    chip = (2 * lax.axis_index("x") + lax.axis_index("y")).astype(jnp.int32).reshape(1)

    wk = {n: _big_form(n, w[n]) for n in BIG}
    slots = {}
    for names, call in _CAST_GROUPS:
        slots.update(zip(names, _cast_slots([wk[n] for n in names], chip, call)))
    ws = {n: _small_form(n, w[n]) for n in SMALL}
    _, gx, shard_grads, small_sums = _local_step(x[0], loss_target[0], slots, ws, (core, chip))

    small, loss = _adam_small(small_sums, ws, {n: _small_form(n, m[n]) for n in SMALL},
                              {n: _small_form(n, v[n]) for n in SMALL})
    grad, delta, new_m, new_v = {}, {}, {}, {}
    for n in SMALL:
        grad[n], delta[n], new_m[n], new_v[n] = (_small_back(n, a, w[n]) for a in small[n])
    for n in BIG:
        g2, d2, m2, v2 = _adam_big(wk[n], shard_grads[n], _big_form(n, m[n]), _big_form(n, v[n]), "adam_" + n)
        grad[n], delta[n], new_m[n], new_v[n] = (_big_back(n, a) for a in (g2, d2, m2, v2))

    return (loss[0, 0], gx.reshape(x.shape), *[grad[n] for n in WEIGHTS], *[delta[n] for n in WEIGHTS],
            *[new_m[n] for n in WEIGHTS], *[new_v[n] for n in WEIGHTS])
```

```python
import functools

import jax
import jax.numpy as jnp
from jax import lax
from jax.experimental import pallas as pl
from jax.experimental.pallas import tpu as pltpu

F32 = jnp.float32
BF16 = jnp.bfloat16

D_MODEL = 1024
N_SHARD = 4
D_FF = 2816
FF_S = D_FF // N_SHARD
D_ATT = 512
D_SGU = 512
D_IN = 3 * D_ATT + 2 * D_SGU + 2 * D_MODEL
IN_S = D_IN // N_SHARD
BR_S = D_MODEL // N_SHARD
HEADS = 8
HEAD_DIM = 64
CHUNK = 64
N_LEFT = 8
BAND = (N_LEFT + 1) * CHUNK
REL_CLIP = 256
N_REL = 2 * REL_CLIP + 1
REL_PAD = 640
SGU_BLOCK = 128
SGU_GROUPS = 8
SGU_GDIM = 64
EPS = 1e-6
NEG_INF = -1e30

ATT_ROWS = 2 * CHUNK
ATT_KEYS = BAND + CHUNK
ATT_PAD = N_LEFT * CHUNK

ADAM_LR = 0.001
ADAM_B1 = 0.9
ADAM_B2 = 0.999
ADAM_EPS = 1e-08
ADAM_WD = 0.01
ADAM_STEP = 10

TM = 256
TW = 1024
DGRAD_ROWS = 64
VMEM_LIMIT = 56 * 1024 * 1024

SMALL_ROWS = 160
LOSS_ROW = 152
MESH = pl.DeviceIdType.MESH

_NT = (((1,), (1,)), ((), ()))
_TN = (((0,), (0,)), ((), ()))


def _params(sem=None):
    return pltpu.CompilerParams(dimension_semantics=sem, vmem_limit_bytes=VMEM_LIMIT)


def _const_spec(shape):
    nd = len(shape)
    return pl.BlockSpec(shape, lambda *_: (0,) * nd, pipeline_mode=pl.Buffered(1))


def _acc_spec(shape):
    nd = len(shape)
    return pl.BlockSpec(shape, lambda *_: (0,) * nd)


def _row_spec(tm, ncols, off=0):
    return pl.BlockSpec((tm, ncols), lambda i: (i + off, 0))


def _row3_spec(tm, ncols):
    return pl.BlockSpec((N_SHARD, tm, ncols), lambda i: (0, i, 0))


def _dot(a, b):
    return jnp.dot(a, b, preferred_element_type=F32)


def _dot_nt(a, b):
    return lax.dot_general(a, b, _NT, preferred_element_type=F32)


def _dot_tn(a, b):
    return lax.dot_general(a, b, _TN, preferred_element_type=F32)


def _rms_fwd(x, g):
    r = lax.rsqrt(jnp.mean(x * x, axis=-1, keepdims=True) + EPS)
    xhat = x * r
    return xhat, r, xhat * g


def _rms_bwd(dh, xhat, r, g):
    dxhat = dh * g
    dx = r * (dxhat - xhat * jnp.mean(dxhat * xhat, axis=-1, keepdims=True))
    dg = jnp.sum(dh * xhat, axis=0, keepdims=True)
    return dx, dg


def _sigmoid(x):
    return 1.0 / (1.0 + jnp.exp(-x))


def _edges(n_steps):
    return [(0, True), (n_steps - 1, False)]


def _ffn_fwd(x, g, wg, wu, wd, name, payload=None):
    T = x.shape[0]

    def body(x_ref, g_ref, wg_ref, wu_ref, wd_ref, xo_ref, h_ref, a_ref, b_ref):
        xv = x_ref[...]
        hb = _rms_fwd(xv, g_ref[...])[2].astype(BF16)
        h_ref[...] = hb
        acc = jnp.zeros((TM, D_MODEL), F32)
        for s in range(N_SHARD):
            a = _dot_nt(hb, wg_ref[s])
            b = _dot_nt(hb, wu_ref[s])
            a_ref[s] = a.astype(BF16)
            b_ref[s] = b.astype(BF16)
            sv = a * _sigmoid(a) * b
            acc += _dot(sv.astype(BF16), wd_ref[s])
        xo_ref[...] = xv + 0.5 * acc

    return _call(
        body, payload, name=name, grid=(T // TM,), when=_edges(T // TM), sem=("arbitrary",),
        in_specs=[_row_spec(TM, D_MODEL), _const_spec((1, D_MODEL)), _const_spec(wg.shape), _const_spec(wu.shape),
                  _const_spec(wd.shape)],
        out_specs=[_row_spec(TM, D_MODEL), _row_spec(TM, D_MODEL), _row3_spec(TM, FF_S), _row3_spec(TM, FF_S)],
        out_shape=[jax.ShapeDtypeStruct((T, D_MODEL), F32), jax.ShapeDtypeStruct((T, D_MODEL), BF16),
                   jax.ShapeDtypeStruct((N_SHARD, T, FF_S), BF16), jax.ShapeDtypeStruct((N_SHARD, T, FF_S), BF16)],
        operands=(x, g, wg, wu, wd))


def _ffn_dgrad(dout, x, a, b, g, wg, wu, wd, name, payload=None):
    T = x.shape[0]

    def body(do_ref, x_ref, a_ref, b_ref, g_ref, wg_ref, wu_ref, wd_ref, dx_ref, da_ref, db_ref, dg_ref):
        do = do_ref[...]
        dob = (0.5 * do).astype(BF16)
        dh = jnp.zeros((TM, D_MODEL), F32)
        ds_next = _dot_nt(dob, wd_ref[0])
        for s in range(N_SHARD):
            ds = ds_next
            if s + 1 < N_SHARD:
                ds_next = _dot_nt(dob, wd_ref[s + 1])
            for r0 in range(0, TM, DGRAD_ROWS):
                rows = slice(r0, r0 + DGRAD_ROWS)
                av = a_ref[s, rows, :].astype(F32)
                bv = b_ref[s, rows, :].astype(F32)
                sig = _sigmoid(av)
                dsr = ds[rows]
                da_ref[s, rows, :] = (dsr * bv * (sig * (1.0 + av * (1.0 - sig)))).astype(BF16)
                db_ref[s, rows, :] = (dsr * (av * sig)).astype(BF16)
            dh += _dot(da_ref[s], wg_ref[s]) + _dot(db_ref[s], wu_ref[s])
        gv = g_ref[...]
        xhat, r, _ = _rms_fwd(x_ref[...], gv)
        dxn, dg = _rms_bwd(dh, xhat, r, gv)
        dx_ref[...] = do + dxn

        @pl.when(pl.program_id(0) == 0)
        def _():
            dg_ref[...] = jnp.zeros_like(dg_ref)

        dg_ref[...] += dg

    return _call(
        body, payload, name=name, grid=(T // TM,), when=_edges(T // TM), sem=("arbitrary",),
        in_specs=[_row_spec(TM, D_MODEL), _row_spec(TM, D_MODEL), _row3_spec(TM, FF_S), _row3_spec(TM, FF_S),
                  _const_spec((1, D_MODEL)), _const_spec(wg.shape), _const_spec(wu.shape), _const_spec(wd.shape)],
        out_specs=[_row_spec(TM, D_MODEL), _row3_spec(TM, FF_S), _row3_spec(TM, FF_S), _acc_spec((1, D_MODEL))],
        out_shape=[jax.ShapeDtypeStruct((T, D_MODEL), F32), jax.ShapeDtypeStruct((N_SHARD, T, FF_S), BF16),
                   jax.ShapeDtypeStruct((N_SHARD, T, FF_S), BF16), jax.ShapeDtypeStruct((1, D_MODEL), F32)],
        operands=(dout, x, a, b, g, wg, wu, wd))


def _ffn_wgrad(h, dout, a, b, da, db, name, payload=None):
    T = h.shape[0]

    def body(h_ref, do_ref, a_ref, b_ref, da_ref, db_ref, gwg_ref, gwu_ref, gwd_ref):
        @pl.when(pl.program_id(1) == 0)
        def _():
            gwg_ref[...] = jnp.zeros_like(gwg_ref)
            gwu_ref[...] = jnp.zeros_like(gwu_ref)
            gwd_ref[...] = jnp.zeros_like(gwd_ref)

        hv = h_ref[...]
        gwg_ref[0] += _dot_tn(da_ref[0], hv)
        gwu_ref[0] += _dot_tn(db_ref[0], hv)
        dob = do_ref[...].astype(BF16)
        av = a_ref[0].astype(F32)
        sv = (0.5 * av * _sigmoid(av) * b_ref[0].astype(F32)).astype(BF16)
        gwd_ref[0] += _dot_tn(sv, dob)

    tw = min(TW, T)
    tok = pl.BlockSpec((tw, D_MODEL), lambda s, i: (i, 0))
    act = pl.BlockSpec((1, tw, FF_S), lambda s, i: (s, i, 0))
    return _call(
        body, payload, name=name, grid=(N_SHARD, T // tw), when=_edges(N_SHARD * (T // tw)),
        sem=("arbitrary", "arbitrary"),
        in_specs=[tok, tok, act, act, act, act],
        out_specs=[pl.BlockSpec((1, FF_S, D_MODEL), lambda s, i: (s, 0, 0))] * 3,
        out_shape=[jax.ShapeDtypeStruct((N_SHARD, FF_S, D_MODEL), F32)] * 3,
        operands=(h, dout, a, b, da, db))


def _in_fwd(x, g, w_in, payload=None):
    T = x.shape[0]

    def body(x_ref, g_ref, w_ref, h_ref, qkv_ref, zs_ref, gl_ref):
        hb = _rms_fwd(x_ref[...], g_ref[...])[2].astype(BF16)
        h_ref[...] = hb
        z0 = _dot(hb, w_ref[0])
        qkv_ref[:, 0:IN_S] = z0.astype(BF16)
        z1 = _dot(hb, w_ref[1])
        qkv_ref[:, IN_S:3 * D_ATT] = z1[:, 0:384].astype(BF16)
        zs_ref[:, 0:768] = z1[:, 384:IN_S]
        z2 = _dot(hb, w_ref[2])
        zs_ref[:, 768:1024] = z2[:, 0:256]
        gl_ref[:, 0:896] = z2[:, 256:IN_S]
        gl_ref[:, 896:2048] = _dot(hb, w_ref[3])

    return _call(
        body, payload, name="in_fwd", grid=(T // TM,), when=_edges(T // TM), sem=("arbitrary",),
        in_specs=[_row_spec(TM, D_MODEL), _const_spec((1, D_MODEL)), _const_spec(w_in.shape)],
        out_specs=[_row_spec(TM, D_MODEL), _row_spec(TM, 3 * D_ATT), _row_spec(TM, 2 * D_SGU), _row_spec(TM, 2 * D_MODEL)],
        out_shape=[jax.ShapeDtypeStruct((T, D_MODEL), BF16), jax.ShapeDtypeStruct((T, 3 * D_ATT), BF16),
                   jax.ShapeDtypeStruct((T, 2 * D_SGU), F32), jax.ShapeDtypeStruct((T, 2 * D_MODEL), F32)],
        operands=(x, g, w_in))


def _in_dgrad(dx_res, x, g, w_in, dq, dk, dv, dzs, dgl):
    T = x.shape[0]

    def body(dxr_ref, x_ref, g_ref, w_ref, dq_ref, dk_ref, dv_ref, dzs_ref, dgl_ref, dx_ref, dz_ref, dg_ref):
        dz = jnp.concatenate([dq_ref[...], dk_ref[...].astype(BF16), dv_ref[...].astype(BF16), dzs_ref[...], dgl_ref[...]],
                             axis=1)
        dz_ref[...] = dz
        dh = jnp.zeros((TM, D_MODEL), F32)
        for s in range(N_SHARD):
            dh += _dot_nt(dz[:, s * IN_S:(s + 1) * IN_S], w_ref[s])
        gv = g_ref[...]
        xhat, r, _ = _rms_fwd(x_ref[...], gv)
        dxn, dg = _rms_bwd(dh, xhat, r, gv)
        dx_ref[...] = dxr_ref[...] + dxn

        @pl.when(pl.program_id(0) == 0)
        def _():
            dg_ref[...] = jnp.zeros_like(dg_ref)

        dg_ref[...] += dg

    pad_blocks = ATT_PAD // TM
    return pl.pallas_call(
        body, name="in_dgrad", grid=(T // TM,),
        in_specs=[_row_spec(TM, D_MODEL), _row_spec(TM, D_MODEL), _const_spec((1, D_MODEL)), _const_spec(w_in.shape),
                  _row_spec(TM, D_ATT), _row_spec(TM, D_ATT, pad_blocks), _row_spec(TM, D_ATT, pad_blocks),
                  _row_spec(TM, 2 * D_SGU), _row_spec(TM, 2 * D_MODEL)],
        out_specs=[_row_spec(TM, D_MODEL), _row_spec(TM, D_IN), _acc_spec((1, D_MODEL))],
        out_shape=[jax.ShapeDtypeStruct((T, D_MODEL), F32), jax.ShapeDtypeStruct((T, D_IN), BF16),
                   jax.ShapeDtypeStruct((1, D_MODEL), F32)],
        compiler_params=_params(("arbitrary",)),
    )(dx_res, x, g, w_in, dq, dk, dv, dzs, dgl)


def _in_wgrad(h, dz):
    T = h.shape[0]

    def body(h_ref, dz_ref, gw_ref):
        @pl.when(pl.program_id(1) == 0)
        def _():
            gw_ref[...] = jnp.zeros_like(gw_ref)

        gw_ref[0] += _dot_tn(h_ref[...], dz_ref[...])

    return pl.pallas_call(
        body, name="in_wgrad", grid=(N_SHARD, T // min(TW, T)),
        in_specs=[pl.BlockSpec((min(TW, T), D_MODEL), lambda s, i: (i, 0)),
                  pl.BlockSpec((min(TW, T), IN_S), lambda s, i: (i, s))],
        out_specs=pl.BlockSpec((1, D_MODEL, IN_S), lambda s, i: (s, 0, 0)),
        out_shape=jax.ShapeDtypeStruct((N_SHARD, D_MODEL, IN_S), F32),
        compiler_params=_params(("arbitrary", "arbitrary")),
    )(h, dz)


def _rel_onehot():
    r = lax.broadcasted_iota(jnp.int32, (REL_PAD, REL_PAD), 0)
    n = lax.broadcasted_iota(jnp.int32, (REL_PAD, REL_PAD), 1)
    idx = jnp.clip(BAND - 1 - n, -REL_CLIP, REL_CLIP) + REL_CLIP
    return jnp.where(r == idx, 1.0, 0.0).astype(BF16)


def _split3(v):
    p1 = v.astype(BF16)
    r1 = v - p1.astype(F32)
    p2 = r1.astype(BF16)
    p3 = (r1 - p2.astype(F32)).astype(BF16)
    return p1, p2, p3


def _relbias_fwd(tab_pad):
    def body(t_ref, o_ref):
        oh = _rel_onehot()
        acc = jnp.zeros((HEADS, REL_PAD), F32)
        for p in _split3(t_ref[...]):
            acc += _dot(p, oh)
        o_ref[...] = acc

    return pl.pallas_call(body, name="relbias_fwd", out_shape=jax.ShapeDtypeStruct((HEADS, REL_PAD), F32))(tab_pad)


def _relbias_bwd(z):
    def body(z_ref, o_ref):
        oh = _rel_onehot()
        dt2 = jnp.sum(z_ref[...], axis=1)
        acc = jnp.zeros((HEADS, REL_PAD), F32)
        for p in _split3(dt2):
            acc += _dot_nt(p, oh)
        o_ref[...] = acc

    return pl.pallas_call(body, name="relbias_bwd", out_shape=jax.ShapeDtypeStruct((HEADS, REL_PAD), F32))(z)


def _bias_blocks(t2):
    flat = jnp.tile(t2, (1, CHUNK))
    skew = flat[:, :CHUNK * (REL_PAD - 1)].reshape(HEADS, CHUNK, REL_PAD - 1)
    bias = skew[:, :, CHUNK - 1:CHUNK - 1 + BAND]
    slabs = [jnp.pad(bias, ((0, 0), (0, 0), (CHUNK * c, ATT_KEYS - BAND - CHUNK * c)), constant_values=NEG_INF)
             for c in range(2)]
    return jnp.concatenate(slabs, axis=1)


def _unskew(db2):
    out = []
    for c in range(2):
        slab = db2[:, CHUNK * c:CHUNK * (c + 1), CHUNK * c:CHUNK * c + BAND]
        y = jnp.pad(slab, ((0, 0), (0, 0), (CHUNK - 1, REL_PAD - BAND - CHUNK + 1)))
        yf = jnp.pad(y.reshape(HEADS, CHUNK * REL_PAD), ((0, 0), (0, CHUNK)))
        out.append(yf.reshape(HEADS, CHUNK, REL_PAD + 1)[:, :, :REL_PAD])
    return jnp.concatenate(out, axis=1)


def _att_load(qkv_hbm, q_s, k_s, v_s, sem, T):
    copies = [pltpu.make_async_copy(qkv_hbm.at[:, 0:D_ATT], q_s, sem.at[0]),
              pltpu.make_async_copy(qkv_hbm.at[:, D_ATT:2 * D_ATT], k_s.at[pl.ds(ATT_PAD, T), :], sem.at[1]),
              pltpu.make_async_copy(qkv_hbm.at[:, 2 * D_ATT:3 * D_ATT], v_s.at[pl.ds(ATT_PAD, T), :], sem.at[2])]
    for cp in copies:
        cp.start()
    k_s[0:ATT_PAD, :] = jnp.zeros((ATT_PAD, D_ATT), BF16)
    v_s[0:ATT_PAD, :] = jnp.zeros((ATT_PAD, D_ATT), BF16)
    for cp in copies:
        cp.wait()


def _head(v, h):
    return v[:, h * HEAD_DIM:(h + 1) * HEAD_DIM]


def _rows(v, h):
    return v[h * ATT_ROWS:(h + 1) * ATT_ROWS]


def _att_exp(qs, kw, bias_ref, valid):
    s = jnp.concatenate([_dot_nt(_head(qs, h), _head(kw, h)) + bias_ref[h] for h in range(HEADS)], axis=0)
    if valid is not None:
        s = jnp.where(valid, s, NEG_INF)
    e = jnp.exp(s - jnp.max(s, axis=-1, keepdims=True))
    return e, 1.0 / jnp.sum(e, axis=-1, keepdims=True)


def _att_blocks(T, block, keys_on_rows=False, middle=None):
    n_edge = min(ATT_PAD // ATT_ROWS, T // ATT_ROWS)
    shape, axis = ((ATT_KEYS, 1), 0) if keys_on_rows else ((1, ATT_KEYS), 1)

    def edge(i, carry):
        r0 = i * ATT_ROWS
        block(i, (lax.broadcasted_iota(jnp.int32, shape, axis) + (r0 - ATT_PAD)) >= 0)
        return carry

    def inner(i, carry):
        block(i, None)
        return carry

    n_blocks = T // ATT_ROWS
    lax.fori_loop(0, n_edge, edge, 0)
    if middle is None:
        lax.fori_loop(n_edge, n_blocks, inner, 0)
        return
    n_late = max(n_blocks - n_blocks // 4, n_edge)
    lax.fori_loop(n_edge, n_late, inner, 0)
    middle()
    lax.fori_loop(n_late, n_blocks, inner, 0)


def _att_fwd(qkv, bias2, payload=None):
    T = qkv.shape[0]

    def body(qkv_hbm, bias_ref, y_ref, q_s, k_s, v_s, sem, middle=None):
        _att_load(qkv_hbm, q_s, k_s, v_s, sem, T)

        def block(i, valid):
            r0 = pl.multiple_of(i * ATT_ROWS, ATT_ROWS)
            qs = q_s[pl.ds(r0, ATT_ROWS), :] * (HEAD_DIM ** -0.5)
            kw = k_s[pl.ds(r0, ATT_KEYS), :]
            vw = v_s[pl.ds(r0, ATT_KEYS), :]
            e, rinv = _att_exp(qs, kw, bias_ref, valid)
            eb = e.astype(BF16)
            outs = [_dot(_rows(eb, h), _head(vw, h)) * _rows(rinv, h) for h in range(HEADS)]
            y_ref[pl.ds(r0, ATT_ROWS), :] = jnp.concatenate(outs, axis=1).astype(BF16)

        _att_blocks(T, block, middle=middle)

    return _call(
        body, payload, name="att_fwd", grid=None, takes_middle=True,
        in_specs=[pl.BlockSpec(memory_space=pl.ANY), pl.BlockSpec(memory_space=pltpu.VMEM)],
        out_specs=[pl.BlockSpec(memory_space=pltpu.VMEM)],
        out_shape=[jax.ShapeDtypeStruct((T, D_ATT), BF16)],
        scratch_shapes=[pltpu.VMEM((T, D_ATT), BF16), pltpu.VMEM((T + ATT_PAD, D_ATT), BF16),
                        pltpu.VMEM((T + ATT_PAD, D_ATT), BF16), pltpu.SemaphoreType.DMA((3,))],
        operands=(qkv, bias2))


def _lanes(v, h):
    return v[:, h * ATT_ROWS:(h + 1) * ATT_ROWS]


def _att_bwd(qkv, dy, bias2t, payload=None):
    T = qkv.shape[0]

    def body(qkv_hbm, dy_ref, bias_ref, dq_ref, dk_ref, dv_ref, db_ref, q_s, k_s, v_s, sem):
        _att_load(qkv_hbm, q_s, k_s, v_s, sem, T)
        dk_ref[...] = jnp.zeros_like(dk_ref)
        dv_ref[...] = jnp.zeros_like(dv_ref)
        db_ref[...] = jnp.zeros_like(db_ref)

        def block(i, valid):
            r0 = pl.multiple_of(i * ATT_ROWS, ATT_ROWS)
            qs = q_s[pl.ds(r0, ATT_ROWS), :] * (HEAD_DIM ** -0.5)
            kw = k_s[pl.ds(r0, ATT_KEYS), :]
            vw = v_s[pl.ds(r0, ATT_KEYS), :]
            dyb = dy_ref[pl.ds(r0, ATT_ROWS), :]
            s = jnp.concatenate([_dot_nt(_head(kw, h), _head(qs, h)) + bias_ref[h] for h in range(HEADS)], axis=1)
            if valid is not None:
                s = jnp.where(valid, s, NEG_INF)
            e = jnp.exp(s - jnp.max(s, axis=0, keepdims=True))
            p = e * (1.0 / jnp.sum(e, axis=0, keepdims=True))
            dp = jnp.concatenate([_dot_nt(_head(vw, h), _head(dyb, h)) for h in range(HEADS)], axis=1)
            ds = p * (dp - jnp.sum(p * dp, axis=0, keepdims=True))
            for h in range(HEADS):
                db_ref[h] += _lanes(ds, h)
            dsb = ds.astype(BF16)
            pb = p.astype(BF16)
            dq = [_dot_tn(_lanes(dsb, h), _head(kw, h)) for h in range(HEADS)]
            dk = [_dot(_lanes(dsb, h), _head(qs, h)) for h in range(HEADS)]
            dv = [_dot(_lanes(pb, h), _head(dyb, h)) for h in range(HEADS)]
            dq_ref[pl.ds(r0, ATT_ROWS), :] = (jnp.concatenate(dq, axis=1) * (HEAD_DIM ** -0.5)).astype(BF16)
            dk_ref[pl.ds(r0, ATT_KEYS), :] += jnp.concatenate(dk, axis=1)
            dv_ref[pl.ds(r0, ATT_KEYS), :] += jnp.concatenate(dv, axis=1)

        _att_blocks(T, block, keys_on_rows=True)

    vmem = pl.BlockSpec(memory_space=pltpu.VMEM)
    return _call(
        body, payload, name="att_bwd", grid=None,
        in_specs=[pl.BlockSpec(memory_space=pl.ANY), vmem, vmem],
        out_specs=[vmem, vmem, vmem, vmem],
        out_shape=[jax.ShapeDtypeStruct((T, D_ATT), BF16), jax.ShapeDtypeStruct((T + ATT_PAD, D_ATT), F32),
                   jax.ShapeDtypeStruct((T + ATT_PAD, D_ATT), F32), jax.ShapeDtypeStruct((HEADS, ATT_KEYS, ATT_ROWS), F32)],
        scratch_shapes=[pltpu.VMEM((T, D_ATT), BF16), pltpu.VMEM((T + ATT_PAD, D_ATT), BF16),
                        pltpu.VMEM((T + ATT_PAD, D_ATT), BF16), pltpu.SemaphoreType.DMA((3,))],
        operands=(qkv, dy, bias2t))


_GELU_C = 0.7978845608028654
_GELU_A = 0.044715


def _gelu(x):
    t = jnp.tanh(_GELU_C * (x + _GELU_A * x * x * x))
    return 0.5 * x * (1.0 + t), t


def _gelu_grad(x, t):
    return 0.5 * (1.0 + t) + 0.5 * x * (1.0 - t * t) * _GELU_C * (1.0 + 3.0 * _GELU_A * x * x)


def _group_masks():
    col = lax.broadcasted_iota(jnp.int32, (SGU_GROUPS, D_SGU), 1) // SGU_GDIM
    grp = lax.broadcasted_iota(jnp.int32, (SGU_GROUPS, D_SGU), 0)
    return jnp.where(col == grp, 1.0, 0.0).astype(F32)


def _causal_mask(transposed=False):
    i = lax.broadcasted_iota(jnp.int32, (SGU_BLOCK, SGU_BLOCK), 0) // CHUNK
    j = lax.broadcasted_iota(jnp.int32, (SGU_BLOCK, SGU_BLOCK), 1) // CHUNK
    return (j >= i) if transposed else (i >= j)


def _sgu_norm(zs, lng, lnb):
    gz, t = _gelu(zs)
    u = gz[:, 0:D_SGU]
    vs = gz[:, D_SGU:2 * D_SGU]
    xc = vs - jnp.mean(vs, axis=-1, keepdims=True)
    rstd = lax.rsqrt(jnp.mean(xc * xc, axis=-1, keepdims=True) + EPS)
    xhat = xc * rstd
    return t, u, xhat, rstd, xhat * lng + lnb


def _sgu_mix(vn_blk, w_ref, bst, gm):
    mask = _causal_mask()
    s = jnp.zeros((SGU_BLOCK, D_SGU), F32)
    for g in range(SGU_GROUPS):
        wm = jnp.where(mask, w_ref[g], 0.0).astype(BF16)
        s += _dot(wm, (vn_blk * gm[g:g + 1, :]).astype(BF16))
        s += bst[:, g:g + 1] * gm[g:g + 1, :]
    return s


def _sgu_fwd(zs, lng, lnb, w_s, bst):
    T = zs.shape[0]
    nblk = TM // SGU_BLOCK

    def body(zs_ref, lng_ref, lnb_ref, w_ref, bst_ref, y_ref):
        _, u, _, _, vn = _sgu_norm(zs_ref[...], lng_ref[...], lnb_ref[...])
        gm = _group_masks()
        bst_v = bst_ref[...]
        for n in range(nblk):
            rows = slice(n * SGU_BLOCK, (n + 1) * SGU_BLOCK)
            s = _sgu_mix(vn[rows], w_ref, bst_v, gm)
            y_ref[rows, :] = (u[rows] * s).astype(BF16)

    return pl.pallas_call(
        body, name="sgu_fwd", grid=(T // TM,),
        in_specs=[_row_spec(TM, 2 * D_SGU), _const_spec((1, D_SGU)), _const_spec((1, D_SGU)),
                  _const_spec(w_s.shape), _const_spec(bst.shape)],
        out_specs=_row_spec(TM, D_SGU),
        out_shape=jax.ShapeDtypeStruct((T, D_SGU), BF16),
        compiler_params=_params(("arbitrary",)),
    )(zs, lng, lnb, w_s, bst)


def _sgu_bwd(zs, dy, lng, lnb, w_s, w_st, bst):
    T = zs.shape[0]
    nblk = TM // SGU_BLOCK

    def body(zs_ref, dy_ref, lng_ref, lnb_ref, w_ref, wt_ref, bst_ref, dzs_ref, dw_ref, dbt_ref, dlg_ref, dlb_ref):
        @pl.when(pl.program_id(0) == 0)
        def _():
            dw_ref[...] = jnp.zeros_like(dw_ref)
            dbt_ref[...] = jnp.zeros_like(dbt_ref)
            dlg_ref[...] = jnp.zeros_like(dlg_ref)
            dlb_ref[...] = jnp.zeros_like(dlb_ref)

        zs_v = zs_ref[...]
        lng_v = lng_ref[...]
        t, u, xhat, rstd, vn = _sgu_norm(zs_v, lng_v, lnb_ref[...])
        gm = _group_masks()
        bst_v = bst_ref[...]
        mask = _causal_mask()
        mask_t = _causal_mask(transposed=True)
        dyv = dy_ref[...].astype(F32)
        lane8 = lax.broadcasted_iota(jnp.int32, (1, SGU_GROUPS), 1)
        du_rows, dvn_rows = [], []
        for n in range(nblk):
            rows = slice(n * SGU_BLOCK, (n + 1) * SGU_BLOCK)
            vn_b = vn[rows]
            s = _sgu_mix(vn_b, w_ref, bst_v, gm)
            du_rows.append(dyv[rows] * s)
            dsb = dyv[rows] * u[rows]
            vnb16 = vn_b.astype(BF16)
            dvn = jnp.zeros((SGU_BLOCK, D_SGU), F32)
            dbt = jnp.zeros((SGU_BLOCK, SGU_GROUPS), F32)
            for g in range(SGU_GROUPS):
                dsg = dsb * gm[g:g + 1, :]
                dsg16 = dsg.astype(BF16)
                wmt = jnp.where(mask_t, wt_ref[g], 0.0).astype(BF16)
                dvn += _dot(wmt, dsg16)
                dw_ref[g] += jnp.where(mask, _dot_nt(dsg16, vnb16), 0.0)
                dbt += jnp.sum(dsg, axis=-1, keepdims=True) * jnp.where(lane8 == g, 1.0, 0.0)
            dbt_ref[...] += dbt
            dvn_rows.append(dvn)
        du = jnp.concatenate(du_rows, axis=0)
        dvn = jnp.concatenate(dvn_rows, axis=0)
        dlg_ref[...] += jnp.sum(dvn * xhat, axis=0, keepdims=True)
        dlb_ref[...] += jnp.sum(dvn, axis=0, keepdims=True)
        dxhat = dvn * lng_v
        dvs = rstd * (dxhat - jnp.mean(dxhat, axis=-1, keepdims=True)
                      - xhat * jnp.mean(dxhat * xhat, axis=-1, keepdims=True))
        dgz = jnp.concatenate([du, dvs], axis=1)
        dzs_ref[...] = (dgz * _gelu_grad(zs_v, t)).astype(BF16)

    return pl.pallas_call(
        body, name="sgu_bwd", grid=(T // TM,),
        in_specs=[_row_spec(TM, 2 * D_SGU), _row_spec(TM, D_SGU), _const_spec((1, D_SGU)), _const_spec((1, D_SGU)),
                  _const_spec(w_s.shape), _const_spec(w_st.shape), _const_spec(bst.shape)],
        out_specs=[_row_spec(TM, 2 * D_SGU), _acc_spec(w_s.shape), _acc_spec(bst.shape), _acc_spec((1, D_SGU)),
                   _acc_spec((1, D_SGU))],
        out_shape=[jax.ShapeDtypeStruct((T, 2 * D_SGU), BF16), jax.ShapeDtypeStruct(w_s.shape, F32),
                   jax.ShapeDtypeStruct(bst.shape, F32), jax.ShapeDtypeStruct((1, D_SGU), F32),
                   jax.ShapeDtypeStruct((1, D_SGU), F32)],
        compiler_params=_params(("arbitrary",)),
    )(zs, dy, lng, lnb, w_s, w_st, bst)


def _cols(v, s):
    return v[:, s * BR_S:(s + 1) * BR_S]


def _merge_fwd(x, y_att, y_sgu, gl, b_gate, wba, wbs, wo, payload=None):
    T = x.shape[0]

    def body(x_ref, ya_ref, ys_ref, gl_ref, bg_ref, wba_ref, wbs_ref, wo_ref, xo_ref, m_ref, pa_ref, ps_ref):
        ya = ya_ref[...]
        ys = ys_ref[...]
        pa = jnp.concatenate([_dot(ya, wba_ref[s]) for s in range(N_SHARD)], axis=1)
        ps = jnp.concatenate([_dot(ys, wbs_ref[s]) for s in range(N_SHARD)], axis=1)
        g = _sigmoid(gl_ref[...] + bg_ref[...])
        mb = (g[:, 0:D_MODEL] * pa + g[:, D_MODEL:2 * D_MODEL] * ps).astype(BF16)
        m_ref[...] = mb
        pa_ref[...] = pa.astype(BF16)
        ps_ref[...] = ps.astype(BF16)
        acc = jnp.zeros((TM, D_MODEL), F32)
        for s in range(N_SHARD):
            acc += _dot(_cols(mb, s), wo_ref[s])
        xo_ref[...] = x_ref[...] + acc

    tokd = jax.ShapeDtypeStruct((T, D_MODEL), BF16)
    return _call(
        body, payload, name="merge_fwd", grid=(T // TM,), when=_edges(T // TM), sem=("arbitrary",),
        in_specs=[_row_spec(TM, D_MODEL), _row_spec(TM, D_ATT), _row_spec(TM, D_SGU), _row_spec(TM, 2 * D_MODEL),
                  _const_spec((1, 2 * D_MODEL)), _const_spec(wba.shape), _const_spec(wbs.shape), _const_spec(wo.shape)],
        out_specs=[_row_spec(TM, D_MODEL)] * 4,
        out_shape=[jax.ShapeDtypeStruct((T, D_MODEL), F32), tokd, tokd, tokd],
        operands=(x, y_att, y_sgu, gl, b_gate, wba, wbs, wo))


def _merge_bwd(dx, y_att, y_sgu, gl, merged, pa, ps, b_gate, wba, wbs, wo, payload=None):
    T = dx.shape[0]

    def body(dx_ref, ya_ref, ys_ref, gl_ref, m_ref, pa_ref, ps_ref, bg_ref, wba_ref, wbs_ref, wo_ref,
             dya_ref, dys_ref, dgl_ref, dbg_ref, gwba_ref, gwbs_ref, gwo_ref):
        @pl.when(pl.program_id(0) == 0)
        def _():
            dbg_ref[...] = jnp.zeros_like(dbg_ref)
            gwba_ref[...] = jnp.zeros_like(gwba_ref)
            gwbs_ref[...] = jnp.zeros_like(gwbs_ref)
            gwo_ref[...] = jnp.zeros_like(gwo_ref)

        dxb = dx_ref[...].astype(BF16)
        dm = jnp.concatenate([_dot_nt(dxb, wo_ref[s]) for s in range(N_SHARD)], axis=1)
        g = _sigmoid(gl_ref[...] + bg_ref[...])
        ga = g[:, 0:D_MODEL]
        gs = g[:, D_MODEL:2 * D_MODEL]
        dpa = (dm * ga).astype(BF16)
        dps = (dm * gs).astype(BF16)
        dgl = jnp.concatenate([dm * pa_ref[...].astype(F32) * ga * (1.0 - ga),
                               dm * ps_ref[...].astype(F32) * gs * (1.0 - gs)], axis=1)
        dgl_ref[...] = dgl.astype(BF16)
        dbg_ref[...] += jnp.sum(dgl, axis=0, keepdims=True)
        ya = ya_ref[...]
        ys = ys_ref[...]
        mb = m_ref[...]
        dya = jnp.zeros((TM, D_ATT), F32)
        dys = jnp.zeros((TM, D_SGU), F32)
        for s in range(N_SHARD):
            dya += _dot_nt(_cols(dpa, s), wba_ref[s])
            dys += _dot_nt(_cols(dps, s), wbs_ref[s])
            gwo_ref[s] += _dot_tn(_cols(mb, s), dxb)
            gwba_ref[s] += _dot_tn(ya, _cols(dpa, s))
            gwbs_ref[s] += _dot_tn(ys, _cols(dps, s))
        dya_ref[...] = dya.astype(BF16)
        dys_ref[...] = dys.astype(BF16)

    return _call(
        body, payload, name="merge_bwd", grid=(T // TM,), when=_edges(T // TM), sem=("arbitrary",),
        operands=(dx, y_att, y_sgu, gl, merged, pa, ps, b_gate, wba, wbs, wo),
        in_specs=[_row_spec(TM, D_MODEL), _row_spec(TM, D_ATT), _row_spec(TM, D_SGU), _row_spec(TM, 2 * D_MODEL),
                  _row_spec(TM, D_MODEL), _row_spec(TM, D_MODEL), _row_spec(TM, D_MODEL),
                  _const_spec((1, 2 * D_MODEL)), _const_spec(wba.shape), _const_spec(wbs.shape), _const_spec(wo.shape)],
        out_specs=[_row_spec(TM, D_ATT), _row_spec(TM, D_SGU), _row_spec(TM, 2 * D_MODEL), _acc_spec((1, 2 * D_MODEL)),
                   _acc_spec(wba.shape), _acc_spec(wbs.shape), _acc_spec(wo.shape)],
        out_shape=[jax.ShapeDtypeStruct((T, D_ATT), BF16), jax.ShapeDtypeStruct((T, D_SGU), BF16),
                   jax.ShapeDtypeStruct((T, 2 * D_MODEL), BF16), jax.ShapeDtypeStruct((1, 2 * D_MODEL), F32),
                   jax.ShapeDtypeStruct(wba.shape, F32), jax.ShapeDtypeStruct(wbs.shape, F32),
                   jax.ShapeDtypeStruct(wo.shape, F32)])


def _loss_bwd(x, target, g):
    T = x.shape[0]

    def body(x_ref, t_ref, g_ref, dx_ref, loss_ref, dg_ref):
        @pl.when(pl.program_id(0) == 0)
        def _():
            loss_ref[...] = jnp.zeros_like(loss_ref)
            dg_ref[...] = jnp.zeros_like(dg_ref)

        gv = g_ref[...]
        xhat, r, y = _rms_fwd(x_ref[...], gv)
        err = y - t_ref[...]
        per_tok = jnp.mean(err * err, axis=-1, keepdims=True)
        loss_ref[...] += 0.5 * jnp.sum(per_tok, axis=0, keepdims=True)
        dxn, dg = _rms_bwd(err * (1.0 / D_MODEL), xhat, r, gv)
        dx_ref[...] = dxn
        dg_ref[...] += dg

    return pl.pallas_call(
        body, name="loss_bwd", grid=(T // TM,),
        in_specs=[_row_spec(TM, D_MODEL), _row_spec(TM, D_MODEL), _const_spec((1, D_MODEL))],
        out_specs=[_row_spec(TM, D_MODEL), _acc_spec((1, 128)), _acc_spec((1, D_MODEL))],
        out_shape=[jax.ShapeDtypeStruct((T, D_MODEL), F32), jax.ShapeDtypeStruct((1, 128), F32),
                   jax.ShapeDtypeStruct((1, D_MODEL), F32)],
        compiler_params=_params(("arbitrary",)),
    )(x, target, g)


BIG = ("ffn1_w_gate", "ffn1_w_up", "ffn1_w_down", "w_in", "w_branch_att", "w_branch_sgu", "w_out",
       "ffn2_w_gate", "ffn2_w_up", "ffn2_w_down")
SMALL = ("norm_ffn1", "norm_mix", "b_gate", "rel_bias", "sgu_ln_g", "sgu_ln_b", "sgu_w_s", "sgu_b_s", "norm_ffn2",
         "norm_final")


G_FFN1 = ("ffn1_w_gate", "ffn1_w_up", "ffn1_w_down")
G_MIX = ("w_in", "w_branch_att", "w_branch_sgu", "w_out")
G_FFN2 = ("ffn2_w_gate", "ffn2_w_up", "ffn2_w_down")


def _local_step(x, target, wb, ws, dist=None):
    def gather_on(names):
        return _ag_payload([wb[n] for n in names]) if dist else None

    t2 = _relbias_fwd(ws["rel_bias"])
    bias2 = _bias_blocks(t2)
    bst = ws["sgu_b_s"].T
    w_st = jnp.swapaxes(ws["sgu_w_s"], 1, 2)

    if dist:
        wb.update(zip(G_FFN1, _call(lambda: None, gather_on(G_FFN1), name="allgather_ffn1", grid=None, in_specs=[],
                                    out_specs=[], out_shape=[])))
    x1, h1, a1, b1, *got = _ffn_fwd(x, ws["norm_ffn1"], wb["ffn1_w_gate"], wb["ffn1_w_up"], wb["ffn1_w_down"],
                                    "ffn1_fwd", gather_on(G_MIX))
    wb.update(zip(G_MIX, got))
    h2, qkv, zs, gl, *got = _in_fwd(x1, ws["norm_mix"], wb["w_in"], gather_on(G_FFN2[0:1]))
    wb.update(zip(G_FFN2[0:1], got))
    y_att, *got = _att_fwd(qkv, bias2, gather_on(G_FFN2[1:2]))
    wb.update(zip(G_FFN2[1:2], got))
    y_sgu = _sgu_fwd(zs, ws["sgu_ln_g"], ws["sgu_ln_b"], ws["sgu_w_s"], bst)
    x2, merged, pa, ps, *got = _merge_fwd(x1, y_att, y_sgu, gl, ws["b_gate"], wb["w_branch_att"], wb["w_branch_sgu"],
                                          wb["w_out"], gather_on(G_FFN2[2:3]))
    wb.update(zip(G_FFN2[2:3], got))
    x3, h3, a3, b3 = _ffn_fwd(x2, ws["norm_ffn2"], wb["ffn2_w_gate"], wb["ffn2_w_up"], wb["ffn2_w_down"], "ffn2_fwd")
    dx3, loss, g_final = _loss_bwd(x3, target, ws["norm_final"])

    gb, gs, sums = {}, {"norm_final": g_final}, {}

    def pair_on(names, small=None):
        return _px_payload([gb[n] for n in names], small) if dist else None

    def pair_add(names, halves):
        for n, rv in zip(names, halves):
            sums[n] = _pair_add(gb[n], rv, dist[0], dist[1], "pair_add_" + n)

    def chips_on(names):
        return _cx_payload([sums[n][1] for n in names], [sums[n][2] for n in names]) if dist else None

    dx2, da3, db3, gs["norm_ffn2"] = _ffn_dgrad(dx3, x2, a3, b3, ws["norm_ffn2"], wb["ffn2_w_gate"], wb["ffn2_w_up"],
                                                wb["ffn2_w_down"], "ffn2_dgrad")
    gb["ffn2_w_gate"], gb["ffn2_w_up"], gb["ffn2_w_down"] = _ffn_wgrad(h3, dx3, a3, b3, da3, db3, "ffn2_wgrad")
    dy_att, dy_sgu, dgl, gs["b_gate"], gb["w_branch_att"], gb["w_branch_sgu"], gb["w_out"], *got = _merge_bwd(
        dx2, y_att, y_sgu, gl, merged, pa, ps, ws["b_gate"], wb["w_branch_att"], wb["w_branch_sgu"], wb["w_out"],
        pair_on(G_FFN2))
    pair_add(G_FFN2, got)
    dq, dk, dv, db2t, *lands2 = _att_bwd(qkv, dy_att, jnp.swapaxes(bias2, 1, 2), chips_on(G_FFN2))
    gs["rel_bias"] = _relbias_bwd(_unskew(jnp.swapaxes(db2t, 1, 2)))
    dzs, gs["sgu_w_s"], dbt, gs["sgu_ln_g"], gs["sgu_ln_b"] = _sgu_bwd(zs, dy_sgu, ws["sgu_ln_g"], ws["sgu_ln_b"],
                                                                      ws["sgu_w_s"], w_st, bst)
    gs["sgu_b_s"] = dbt.T
    dx1, dz, gs["norm_mix"] = _in_dgrad(dx2, x1, ws["norm_mix"], wb["w_in"], dq, dk, dv, dzs, dgl)
    gb["w_in"] = _in_wgrad(h2, dz)
    gx, da1, db1, gs["norm_ffn1"], *got = _ffn_dgrad(dx1, x, a1, b1, ws["norm_ffn1"], wb["ffn1_w_gate"],
                                                    wb["ffn1_w_up"], wb["ffn1_w_down"], "ffn1_dgrad", pair_on(G_MIX))
    pair_add(G_MIX, got)
    gb["ffn1_w_gate"], gb["ffn1_w_up"], gb["ffn1_w_down"], *lands_mix = _ffn_wgrad(h1, dx1, a1, b1, da1, db1,
                                                                                   "ffn1_wgrad", chips_on(G_MIX))
    if not dist:
        return loss, gx, gb, gs

    def final_sums(names, lands):
        return [_final_sum(sums[n][0], land, dist[1], dist[0], "final_sum_" + n) for n, land in zip(names, lands)]

    early = G_FFN2 + G_MIX
    k = len(G_FFN1)
    tail = _tail_reduce([gb[n] for n in G_FFN1], _pack_small(gs, loss),
                        _ss_payload(final_sums(early, list(lands2) + list(lands_mix))))
    for i, n in enumerate(G_FFN1):
        sums[n] = (tail[i],)
    small_sums, shared = tail[2 * k], dict(zip(early, tail[2 * k + 1:]))
    shared.update(zip(G_FFN1, _sibling_share(final_sums(G_FFN1, tail[k:2 * k]), "sibling_share")))
    return loss, gx, shared, small_sums


_ANY = pl.BlockSpec(memory_space=pl.ANY)
_VMEM = pl.BlockSpec(memory_space=pltpu.VMEM)


def _mesh_pos():
    return lax.axis_index("x"), lax.axis_index("y"), lax.axis_index("c")


def _cast_slots(shards, chip, name):
    n = len(shards)
    r, ncol = shards[0].shape
    tr = r // 2

    def body(me_ref, *refs):
        for i_ref, o_ref in zip(refs[:n], refs[n:]):
            o_ref[0] = i_ref[...].astype(BF16)

    grid_spec = pltpu.PrefetchScalarGridSpec(
        num_scalar_prefetch=1, grid=(r // tr,),
        in_specs=[pl.BlockSpec((tr, ncol), lambda i, me: (i, 0))] * n,
        out_specs=[pl.BlockSpec((1, tr, ncol), lambda i, me: (me[0], i, 0))] * n)
    return pl.pallas_call(
        body, name=name, grid_spec=grid_spec,
        out_shape=[jax.ShapeDtypeStruct((N_SHARD, r, ncol), BF16)] * n,
        compiler_params=_params(("arbitrary",)),
    )(chip, *shards)


class _Payload:
    def __init__(self, arrays, out_shapes, aliases, scratch, phases):
        self.arrays = list(arrays)
        self.out_shapes = list(out_shapes)
        self.aliases = dict(aliases)
        self.scratch = list(scratch)
        self.phases = phases


def _remote(src, dst, ssem, rsem, dev):
    return pltpu.make_async_remote_copy(src_ref=src, dst_ref=dst, send_sem=ssem, recv_sem=rsem, device_id=dev,
                                        device_id_type=MESH)


def _call(body, payload, *, name, grid, in_specs, out_specs, out_shape, scratch_shapes=(), sem=None, when=None,
          operands=(), takes_middle=False):
    in_specs, out_specs, out_shape = list(in_specs), list(out_specs), list(out_shape)
    scratch_shapes = list(scratch_shapes)
    n_in, n_out, n_scr = len(in_specs), len(out_specs), len(scratch_shapes)
    kwargs = {}
    kernel = body
    if payload is not None:
        k_in, k_out = len(payload.arrays), len(payload.out_shapes)
        rank = len(grid) if grid else 0

        def kernel(*refs):
            a, b = n_in, n_in + k_in
            c, d = b + n_out, b + n_out + k_out
            e = d + n_scr
            phases = payload.phases(refs[a:b], refs[c:d], refs[e:])

            def run():
                body(*refs[:a], *refs[b:c], *refs[d:e])

            if not grid:
                phases[0]()
                if len(phases) == 3 and takes_middle:
                    body(*refs[:a], *refs[b:c], *refs[d:e], middle=phases[1])
                    phases[2]()
                    return
                run()
                for ph in phases[1:]:
                    ph()
                return
            step = pl.program_id(0)
            if rank == 2:
                step = step * grid[1] + pl.program_id(1)
            marks = list(when)
            if len(phases) == 3:
                marks = [when[0], (max(when[1][0] - 3, 0), False), when[1]]
            for ph, (at, before) in zip(phases, marks):
                if before:
                    pl.when(step == at)(ph)
            run()
            for ph, (at, before) in zip(phases, marks):
                if not before:
                    pl.when(step == at)(ph)

        in_specs += [_ANY] * k_in
        out_specs += [_ANY] * k_out
        out_shape += payload.out_shapes
        scratch_shapes += payload.scratch
        kwargs["input_output_aliases"] = {n_in + i: n_out + j for i, j in payload.aliases.items()}
        operands = tuple(operands) + tuple(payload.arrays)
    if grid:
        kwargs["grid"] = grid
    return pl.pallas_call(kernel, name=name, in_specs=in_specs, out_specs=out_specs, out_shape=out_shape,
                          scratch_shapes=scratch_shapes, compiler_params=_params(sem), **kwargs)(*operands)


def _ag_payload(slots):
    n = len(slots)

    def phases(_, refs, sems):
        send_i, recv_i, send_d, recv_d = sems
        x, y, c = _mesh_pos()
        me = 2 * x + y

        def half(w, core):
            rh = slots[w].shape[1] // 2
            return pl.ds(core * rh, rh)

        def ici(w, j):
            t = (me + 1 + j) % N_SHARD
            mine = refs[w].at[me, half(w, c), :]
            return _remote(mine, mine, send_i.at[3 * w + j], recv_i.at[3 * w + j], (t // 2, t % 2, c))

        def d2d(w, j, core):
            s = (me + 3 - j) % N_SHARD
            land = refs[w].at[s, half(w, core), :]
            return _remote(land, land, send_d.at[3 * w + j], recv_d.at[3 * w + j], (x, y, 1 - c))

        def start():
            for w in range(n):
                for j in range(3):
                    ici(w, j).start()

        def forward():
            for w in range(n):
                for j in range(3):
                    s = (me + 3 - j) % N_SHARD
                    land = refs[w].at[s, half(w, c), :]
                    _remote(land, land, send_i.at[3 * w + j], recv_i.at[3 * w + j], (x, y, c)).wait_recv()
                    d2d(w, j, c).start()

        def finish():
            for w in range(n):
                for j in range(3):
                    d2d(w, j, 1 - c).wait_recv()
            for w in range(n):
                for j in range(3):
                    ici(w, j).wait_send()
                    d2d(w, j, c).wait_send()

        return [start, forward, finish]

    return _Payload(slots, [jax.ShapeDtypeStruct(s.shape, s.dtype) for s in slots], {i: i for i in range(n)},
                    [pltpu.SemaphoreType.DMA((3 * n,)) for _ in range(4)], phases)


def _px_payload(grads, small=None):
    arrays = list(grads) + ([small] if small is not None else [])
    n = len(arrays)

    def phases(ins, outs, sems):
        send, recv = sems
        x, y, c = _mesh_pos()

        def copy(w):
            if w < len(grads):
                rh = grads[w].shape[1] // 2
                src = ins[w].at[:, pl.ds((1 - c) * rh, rh), :]
            else:
                src = ins[w]
            return _remote(src, outs[w], send.at[w], recv.at[w], (x, y, 1 - c))

        def start():
            for w in range(n):
                copy(w).start()

        def finish():
            for w in range(n):
                copy(w).wait()

        return [start, finish]

    out_shapes = [jax.ShapeDtypeStruct((N_SHARD, g.shape[1] // 2, g.shape[2]), F32) for g in grads]
    if small is not None:
        out_shapes.append(jax.ShapeDtypeStruct(small.shape, F32))
    return _Payload(arrays, out_shapes, {}, [pltpu.SemaphoreType.DMA((n,)), pltpu.SemaphoreType.DMA((n,))], phases)


def _cx_payload(pbs, lands):
    n = len(pbs)

    def phases(ins, outs, sems):
        send, recv = sems
        x, y, c = _mesh_pos()
        me = 2 * x + y

        def copy(w, j):
            t = (me + 1 + j) % N_SHARD
            return _remote(ins[w].at[t], outs[w].at[me], send.at[3 * w + j], recv.at[3 * w + j], (t // 2, t % 2, c))

        def start():
            for w in range(n):
                for j in range(3):
                    copy(w, j).start()

        def finish():
            for w in range(n):
                for j in range(3):
                    copy(w, j).wait()

        return [start, finish]

    return _Payload(list(pbs) + list(lands), [jax.ShapeDtypeStruct(p.shape, BF16) for p in lands],
                    {n + i: i for i in range(n)},
                    [pltpu.SemaphoreType.DMA((3 * n,)), pltpu.SemaphoreType.DMA((3 * n,))], phases)


def _pair_add(g, rv, core, chip, name):
    _, r, ncol = g.shape
    rh = r // 2

    def body(c_ref, me_ref, g_ref, rv_ref, pf_ref, pb_ref, land_ref):
        s = g_ref[0] + rv_ref[0]
        sb = s.astype(BF16)
        pb_ref[0] = sb

        @pl.when(pl.program_id(0) == me_ref[0])
        def _():
            pf_ref[...] = s
            land_ref[0] = sb

    slot = pl.BlockSpec((1, rh, ncol), lambda s, c, me: (s, 0, 0))
    grid_spec = pltpu.PrefetchScalarGridSpec(
        num_scalar_prefetch=2, grid=(N_SHARD,),
        in_specs=[pl.BlockSpec((1, rh, ncol), lambda s, c, me: (s, c[0], 0)), slot],
        out_specs=[pl.BlockSpec((rh, ncol), lambda s, c, me: (0, 0)), slot,
                   pl.BlockSpec((1, rh, ncol), lambda s, c, me: (me[0], 0, 0))])
    return pl.pallas_call(
        body, name=name, grid_spec=grid_spec,
        out_shape=[jax.ShapeDtypeStruct((rh, ncol), F32), jax.ShapeDtypeStruct((N_SHARD, rh, ncol), BF16),
                   jax.ShapeDtypeStruct((N_SHARD, rh, ncol), BF16)],
        compiler_params=_params(("arbitrary",)),
    )(core, chip, g, rv)


def _tail_reduce(grads, small, payload=None):
    n = len(grads)
    _, r, ncol = grads[0].shape
    rh = r // 2

    def body(*refs, middle=None):
        g_hbm, sm = refs[:n], refs[n]
        pf, land, sm_out = refs[n + 1:2 * n + 1], refs[2 * n + 1:3 * n + 1], refs[3 * n + 1]
        scr = refs[3 * n + 2:]
        rv, mine, sendb = scr[:n], scr[n:2 * n], scr[2 * n:3 * n]
        sm_rv, sm_sum, d_send, d_recv, load, i_send, i_recv, store = scr[3 * n:]
        x, y, c = _mesh_pos()
        me = 2 * x + y
        sib = (x, y, 1 - c)

        def pair(w):
            src = g_hbm[w].at[:, pl.ds((1 - c) * rh, rh), :] if w < n else sm
            return _remote(src, rv[w] if w < n else sm_rv, d_send.at[w], d_recv.at[w], sib)

        def chips(w, j):
            t = (me + 1 + j) % N_SHARD
            src = sendb[w].at[t] if w < n else sm_sum
            dst = land[w].at[me] if w < n else sm_out.at[me]
            return _remote(src, dst, i_send.at[3 * w + j], i_recv.at[3 * w + j], (t // 2, t % 2, c))

        loads = [pltpu.make_async_copy(g_hbm[w].at[:, pl.ds(c * rh, rh), :], mine[w], load.at[w]) for w in range(n)]
        for w in range(n + 1):
            pair(w).start()
        for cp in loads:
            cp.start()
        stores = []
        for w in range(n):
            loads[w].wait()
            pair(w).wait_recv()
            for k in range(N_SHARD):
                s = mine[w][k] + rv[w][k]
                mine[w][k] = s
                sendb[w][k] = s.astype(BF16)
            stores += [pltpu.make_async_copy(mine[w].at[me], pf[w], store.at[2 * w]),
                       pltpu.make_async_copy(sendb[w].at[me], land[w].at[me], store.at[2 * w + 1])]
            for cp in stores[-2:]:
                cp.start()
            for j in range(3):
                chips(w, j).start()
        pair(n).wait_recv()
        if middle is not None:
            middle()
        sm_sum[...] = sm[...] + sm_rv[...]
        stores.append(pltpu.make_async_copy(sm_sum, sm_out.at[me], store.at[2 * n]))
        stores[-1].start()
        for j in range(3):
            chips(n, j).start()
        for w in range(n + 1):
            pair(w).wait_send()
            for j in range(3):
                chips(w, j).wait()
        for cp in stores:
            cp.wait()

    half = (N_SHARD, rh, ncol)
    return _call(
        body, payload, name="tail_reduce", grid=None, takes_middle=True,
        in_specs=[_ANY] * n + [_VMEM], out_specs=[_ANY] * (2 * n + 1),
        out_shape=([jax.ShapeDtypeStruct((rh, ncol), F32)] * n + [jax.ShapeDtypeStruct(half, BF16)] * n
                   + [jax.ShapeDtypeStruct((N_SHARD,) + small.shape, F32)]),
        scratch_shapes=([pltpu.VMEM(half, F32)] * (2 * n) + [pltpu.VMEM(half, BF16)] * n
                        + [pltpu.VMEM(small.shape, F32), pltpu.VMEM(small.shape, F32),
                           pltpu.SemaphoreType.DMA((n + 1,)), pltpu.SemaphoreType.DMA((n + 1,)),
                           pltpu.SemaphoreType.DMA((n,)), pltpu.SemaphoreType.DMA((3 * n + 3,)),
                           pltpu.SemaphoreType.DMA((3 * n + 3,)), pltpu.SemaphoreType.DMA((2 * n + 1,))]),
        operands=(*grads, small))


def _final_sum(pf, land, chip, core, name):
    _, rh, ncol = land.shape

    def body(me_ref, c_ref, pf_ref, land_ref, o_ref):
        me = me_ref[0]
        acc = jnp.zeros((rh, ncol), F32)
        for k in range(N_SHARD):
            acc = acc + jnp.where(me == k, pf_ref[...], land_ref[k].astype(F32))
        o_ref[...] = acc

    grid_spec = pltpu.PrefetchScalarGridSpec(
        num_scalar_prefetch=2, grid=(1,),
        in_specs=[pl.BlockSpec((rh, ncol), lambda i, me, c: (0, 0)),
                  pl.BlockSpec((N_SHARD, rh, ncol), lambda i, me, c: (0, 0, 0))],
        out_specs=pl.BlockSpec((rh, ncol), lambda i, me, c: (c[0], 0)))
    return pl.pallas_call(
        body, name=name, grid_spec=grid_spec, out_shape=jax.ShapeDtypeStruct((2 * rh, ncol), F32),
        compiler_params=_params(("arbitrary",)),
    )(chip, core, pf, land)


def _ss_payload(fulls):
    n = len(fulls)

    def phases(_, outs, sems):
        send, recv = sems
        x, y, c = _mesh_pos()

        def copy(w):
            rh = fulls[w].shape[0] // 2
            mine = outs[w].at[pl.ds(c * rh, rh), :]
            return _remote(mine, mine, send.at[w], recv.at[w], (x, y, 1 - c))

        def start():
            for w in range(n):
                copy(w).start()

        def finish():
            for w in range(n):
                copy(w).wait()

        return [lambda: None, start, finish]

    return _Payload(fulls, [jax.ShapeDtypeStruct(f.shape, F32) for f in fulls], {i: i for i in range(n)},
                    [pltpu.SemaphoreType.DMA((n,)), pltpu.SemaphoreType.DMA((n,))], phases)


def _sibling_share(fulls, name):
    return _call(lambda: None, _ss_payload(fulls), name=name, grid=None, in_specs=[], out_specs=[], out_shape=[])


_ROW = {"rel_bias": 128, "sgu_b_s": 136, "norm_ffn1": 144, "norm_mix": 145, "norm_ffn2": 146, "norm_final": 147,
        "b_gate": 148, "sgu_ln_g": 150, "sgu_ln_b": 151}


def _pack_small(gs, loss):
    def body(ws, rel, bs, n1, nm, n2, nf, bg, lg, lb, loss_ref, o_ref):
        o_ref[...] = jnp.zeros_like(o_ref)
        o_ref[LOSS_ROW:LOSS_ROW + 1, 0:128] = loss_ref[...]
        for g in range(SGU_GROUPS):
            o_ref[0:SGU_BLOCK, g * SGU_BLOCK:(g + 1) * SGU_BLOCK] = ws[g]
        o_ref[128:136, 0:REL_PAD] = rel[...]
        o_ref[136:144, 0:SGU_BLOCK] = bs[...]
        o_ref[144:145, :] = n1[...]
        o_ref[145:146, :] = nm[...]
        o_ref[146:147, :] = n2[...]
        o_ref[147:148, :] = nf[...]
        o_ref[148:149, :] = bg[:, 0:D_MODEL]
        o_ref[149:150, :] = bg[:, D_MODEL:2 * D_MODEL]
        o_ref[150:151, 0:D_SGU] = lg[...]
        o_ref[151:152, 0:D_SGU] = lb[...]

    order = ("sgu_w_s", "rel_bias", "sgu_b_s", "norm_ffn1", "norm_mix", "norm_ffn2", "norm_final", "b_gate", "sgu_ln_g",
             "sgu_ln_b")
    return pl.pallas_call(body, name="pack_small", out_shape=jax.ShapeDtypeStruct((SMALL_ROWS, D_MODEL), F32))(
        *[gs[k] for k in order], loss)


def _adam(w, g, m, v):
    m2 = ADAM_B1 * m + (1.0 - ADAM_B1) * g
    v2 = ADAM_B2 * v + (1.0 - ADAM_B2) * (g * g)
    m_hat = m2 / (1.0 - ADAM_B1 ** ADAM_STEP)
    v_hat = v2 / (1.0 - ADAM_B2 ** ADAM_STEP)
    delta = -ADAM_LR * (m_hat / (jnp.sqrt(v_hat) + ADAM_EPS) + ADAM_WD * w)
    return delta, m2, v2


def _adam_small(sin, w, m, v):
    names = SMALL
    k = len(names)

    def body(*refs):
        sin_ref = refs[0]
        w_r, m_r, v_r = refs[1:1 + k], refs[1 + k:1 + 2 * k], refs[1 + 2 * k:1 + 3 * k]
        outs = refs[1 + 3 * k:]
        tot = sin_ref[0] + sin_ref[1] + sin_ref[2] + sin_ref[3]
        outs[4 * k][...] = tot[LOSS_ROW:LOSS_ROW + 1, 0:128]
        for i, name in enumerate(names):
            o = outs[4 * i:4 * i + 4]
            if name == "sgu_w_s":
                for gi in range(SGU_GROUPS):
                    g = tot[0:SGU_BLOCK, gi * SGU_BLOCK:(gi + 1) * SGU_BLOCK]
                    res = (g,) + _adam(w_r[i][gi], g, m_r[i][gi], v_r[i][gi])
                    for ref, val in zip(o, res):
                        ref[gi] = val
                continue
            r0 = _ROW[name]
            if name == "rel_bias":
                g = tot[r0:r0 + HEADS, 0:REL_PAD]
            elif name == "sgu_b_s":
                g = tot[r0:r0 + SGU_GROUPS, 0:SGU_BLOCK]
            elif name == "b_gate":
                g = jnp.concatenate([tot[r0:r0 + 1, :], tot[r0 + 1:r0 + 2, :]], axis=1)
            elif name in ("sgu_ln_g", "sgu_ln_b"):
                g = tot[r0:r0 + 1, 0:D_SGU]
            else:
                g = tot[r0:r0 + 1, :]
            res = (g,) + _adam(w_r[i][...], g, m_r[i][...], v_r[i][...])
            for ref, val in zip(o, res):
                ref[...] = val

    out_shape = []
    for name in names:
        out_shape += [jax.ShapeDtypeStruct(w[name].shape, F32)] * 4
    out_shape.append(jax.ShapeDtypeStruct((1, 128), F32))
    flat = pl.pallas_call(body, name="adam_small", out_shape=out_shape, compiler_params=_params())(
        sin, *[w[n] for n in names], *[m[n] for n in names], *[v[n] for n in names])
    return {name: tuple(flat[4 * i:4 * i + 4]) for i, name in enumerate(names)}, flat[4 * k]


def _adam_big(w, g, m, v, name):
    r, ncol = w.shape
    tr = 256 if r % 256 == 0 else r // 2

    def body(w_ref, g_ref, m_ref, v_ref, g2_ref, d_ref, m2_ref, v2_ref):
        gv = g_ref[...]
        g2_ref[...] = gv
        d_ref[...], m2_ref[...], v2_ref[...] = _adam(w_ref[...], gv, m_ref[...], v_ref[...])

    spec = pl.BlockSpec((tr, ncol), lambda i: (i, 0))
    return pl.pallas_call(
        body, name=name, grid=(r // tr,), in_specs=[spec] * 4, out_specs=[spec] * 4,
        out_shape=[jax.ShapeDtypeStruct(w.shape, F32)] * 4, compiler_params=_params(("arbitrary",)),
    )(w, g, m, v)


WEIGHTS = ("norm_ffn1", "ffn1_w_gate", "ffn1_w_up", "ffn1_w_down", "norm_mix", "w_in", "b_gate", "rel_bias", "sgu_ln_g",
           "sgu_ln_b", "sgu_w_s", "sgu_b_s", "w_branch_att", "w_branch_sgu", "w_out", "norm_ffn2", "ffn2_w_gate",
           "ffn2_w_up", "ffn2_w_down", "norm_final")


GATE_UP = ("ffn1_w_gate", "ffn1_w_up", "ffn2_w_gate", "ffn2_w_up")
_FFN = ("ffn1_w_gate", "ffn1_w_up", "ffn1_w_down", "ffn2_w_gate", "ffn2_w_up", "ffn2_w_down")
_CAST_GROUPS = ((_FFN, "cast_ffn"), (("w_in",), "cast_w_in"), (("w_branch_att", "w_branch_sgu"), "cast_branch"),
                (("w_out",), "cast_w_out"))


def _big_form(name, a):
    return jnp.swapaxes(a, 1, 2)[0] if name in GATE_UP else a[0]


def _big_back(name, a):
    return jnp.swapaxes(a[None], 1, 2) if name in GATE_UP else a[None]


def _small_form(name, a):
    if name == "norm_final":
        return a.reshape(1, D_MODEL)
    if name == "rel_bias":
        return jnp.pad(a[0], ((0, 0), (0, REL_PAD - N_REL)))
    if name in ("sgu_w_s", "sgu_b_s"):
        return a[0]
    return a


def _small_back(name, a, like):
    if name == "rel_bias":
        a = a[:, :N_REL]
    return a.reshape(like.shape)


def kernel(x, norm_ffn1, ffn1_w_gate, ffn1_w_up, ffn1_w_down, norm_mix, w_in, b_gate, rel_bias, sgu_ln_g, sgu_ln_b, sgu_w_s, sgu_b_s, w_branch_att, w_branch_sgu, w_out, norm_ffn2, ffn2_w_gate, ffn2_w_up, ffn2_w_down, norm_final, loss_target, m_norm_ffn1, m_ffn1_w_gate, m_ffn1_w_up, m_ffn1_w_down, m_norm_mix, m_w_in, m_b_gate, m_rel_bias, m_sgu_ln_g, m_sgu_ln_b, m_sgu_w_s, m_sgu_b_s, m_w_branch_att, m_w_branch_sgu, m_w_out, m_norm_ffn2, m_ffn2_w_gate, m_ffn2_w_up, m_ffn2_w_down, m_norm_final, v_norm_ffn1, v_ffn1_w_gate, v_ffn1_w_up, v_ffn1_w_down, v_norm_mix, v_w_in, v_b_gate, v_rel_bias, v_sgu_ln_g, v_sgu_ln_b, v_sgu_w_s, v_sgu_b_s, v_w_branch_att, v_w_branch_sgu, v_w_out, v_norm_ffn2, v_ffn2_w_gate, v_ffn2_w_up, v_ffn2_w_down, v_norm_final):
    w = dict(norm_ffn1=norm_ffn1, ffn1_w_gate=ffn1_w_gate, ffn1_w_up=ffn1_w_up, ffn1_w_down=ffn1_w_down, norm_mix=norm_mix,
             w_in=w_in, b_gate=b_gate, rel_bias=rel_bias, sgu_ln_g=sgu_ln_g, sgu_ln_b=sgu_ln_b, sgu_w_s=sgu_w_s,
             sgu_b_s=sgu_b_s, w_branch_att=w_branch_att, w_branch_sgu=w_branch_sgu, w_out=w_out, norm_ffn2=norm_ffn2,
             ffn2_w_gate=ffn2_w_gate, ffn2_w_up=ffn2_w_up, ffn2_w_down=ffn2_w_down, norm_final=norm_final)
    m = dict(norm_ffn1=m_norm_ffn1, ffn1_w_gate=m_ffn1_w_gate, ffn1_w_up=m_ffn1_w_up, ffn1_w_down=m_ffn1_w_down,
             norm_mix=m_norm_mix, w_in=m_w_in, b_gate=m_b_gate, rel_bias=m_rel_bias, sgu_ln_g=m_sgu_ln_g,
             sgu_ln_b=m_sgu_ln_b, sgu_w_s=m_sgu_w_s, sgu_b_s=m_sgu_b_s, w_branch_att=m_w_branch_att,
             w_branch_sgu=m_w_branch_sgu, w_out=m_w_out, norm_ffn2=m_norm_ffn2, ffn2_w_gate=m_ffn2_w_gate,
             ffn2_w_up=m_ffn2_w_up, ffn2_w_down=m_ffn2_w_down, norm_final=m_norm_final)
    v = dict(norm_ffn1=v_norm_ffn1, ffn1_w_gate=v_ffn1_w_gate, ffn1_w_up=v_ffn1_w_up, ffn1_w_down=v_ffn1_w_down,
             norm_mix=v_norm_mix, w_in=v_w_in, b_gate=v_b_gate, rel_bias=v_rel_bias, sgu_ln_g=v_sgu_ln_g,
             sgu_ln_b=v_sgu_ln_b, sgu_w_s=v_sgu_w_s, sgu_b_s=v_sgu_b_s, w_branch_att=v_w_branch_att,
             w_branch_sgu=v_w_branch_sgu, w_out=v_w_out, norm_ffn2=v_norm_ffn2, ffn2_w_gate=v_ffn2_w_gate,
             ffn2_w_up=v_ffn2_w_up, ffn2_w_down=v_ffn2_w_down, norm_final=v_norm_final)

    core = lax.axis_index("c").astype(jnp.int32).reshape(1)
    chip = (2 * lax.axis_index("x") + lax.axis_index("y")).astype(jnp.int32).reshape(1)

    wk = {n: _big_form(n, w[n]) for n in BIG}
    slots = {}
    for names, call in _CAST_GROUPS:
        slots.update(zip(names, _cast_slots([wk[n] for n in names], chip, call)))
    ws = {n: _small_form(n, w[n]) for n in SMALL}
    _, gx, shard_grads, small_sums = _local_step(x[0], loss_target[0], slots, ws, (core, chip))

    small, loss = _adam_small(small_sums, ws, {n: _small_form(n, m[n]) for n in SMALL},
                              {n: _small_form(n, v[n]) for n in SMALL})
    grad, delta, new_m, new_v = {}, {}, {}, {}
    for n in SMALL:
        grad[n], delta[n], new_m[n], new_v[n] = (_small_back(n, a, w[n]) for a in small[n])
    for n in BIG:
        g2, d2, m2, v2 = _adam_big(wk[n], shard_grads[n], _big_form(n, m[n]), _big_form(n, v[n]), "adam_" + n)
        grad[n], delta[n], new_m[n], new_v[n] = (_big_back(n, a) for a in (g2, d2, m2, v2))

    return (loss[0, 0], gx.reshape(x.shape), *[grad[n] for n in WEIGHTS], *[delta[n] for n in WEIGHTS],
            *[new_m[n] for n in WEIGHTS], *[new_v[n] for n in WEIGHTS])
```

```python
import functools

import jax
import jax.numpy as jnp
from jax import lax
from jax.experimental import pallas as pl
from jax.experimental.pallas import tpu as pltpu

F32 = jnp.float32
BF16 = jnp.bfloat16

D_MODEL = 1024
N_SHARD = 4
D_FF = 2816
FF_S = D_FF // N_SHARD
D_ATT = 512
D_SGU = 512
D_IN = 3 * D_ATT + 2 * D_SGU + 2 * D_MODEL
IN_S = D_IN // N_SHARD
BR_S = D_MODEL // N_SHARD
HEADS = 8
HEAD_DIM = 64
CHUNK = 64
N_LEFT = 8
BAND = (N_LEFT + 1) * CHUNK
REL_CLIP = 256
N_REL = 2 * REL_CLIP + 1
REL_PAD = 640
SGU_BLOCK = 128
SGU_GROUPS = 8
SGU_GDIM = 64
EPS = 1e-6
NEG_INF = -1e30

ATT_ROWS = 2 * CHUNK
ATT_KEYS = BAND + CHUNK
ATT_PAD = N_LEFT * CHUNK

ADAM_LR = 0.001
ADAM_B1 = 0.9
ADAM_B2 = 0.999
ADAM_EPS = 1e-08
ADAM_WD = 0.01
ADAM_STEP = 10

TM = 256
TW = 1024
DGRAD_ROWS = 64
VMEM_LIMIT = 56 * 1024 * 1024

SMALL_ROWS = 160
LOSS_ROW = 152
MESH = pl.DeviceIdType.MESH

_NT = (((1,), (1,)), ((), ()))
_TN = (((0,), (0,)), ((), ()))


def _params(sem=None):
    return pltpu.CompilerParams(dimension_semantics=sem, vmem_limit_bytes=VMEM_LIMIT)


def _const_spec(shape):
    nd = len(shape)
    return pl.BlockSpec(shape, lambda *_: (0,) * nd, pipeline_mode=pl.Buffered(1))


def _acc_spec(shape):
    nd = len(shape)
    return pl.BlockSpec(shape, lambda *_: (0,) * nd)


def _row_spec(tm, ncols, off=0):
    return pl.BlockSpec((tm, ncols), lambda i: (i + off, 0))


def _row3_spec(tm, ncols):
    return pl.BlockSpec((N_SHARD, tm, ncols), lambda i: (0, i, 0))


def _dot(a, b):
    return jnp.dot(a, b, preferred_element_type=F32)


def _dot_nt(a, b):
    return lax.dot_general(a, b, _NT, preferred_element_type=F32)


def _dot_tn(a, b):
    return lax.dot_general(a, b, _TN, preferred_element_type=F32)


def _rms_fwd(x, g):
    r = lax.rsqrt(jnp.mean(x * x, axis=-1, keepdims=True) + EPS)
    xhat = x * r
    return xhat, r, xhat * g


def _rms_bwd(dh, xhat, r, g):
    dxhat = dh * g
    dx = r * (dxhat - xhat * jnp.mean(dxhat * xhat, axis=-1, keepdims=True))
    dg = jnp.sum(dh * xhat, axis=0, keepdims=True)
    return dx, dg


def _sigmoid(x):
    return 1.0 / (1.0 + jnp.exp(-x))


def _edges(n_steps):
    return [(0, True), (n_steps - 1, False)]


def _ffn_fwd(x, g, wg, wu, wd, name, payload=None, head=None):
    T = x.shape[0]

    def body(x_ref, g_ref, wg_ref, wu_ref, wd_ref, *rest):
        if head:
            t_ref, gf_ref, xo_ref, h_ref, a_ref, b_ref, loss_ref, dgf_ref = rest
        else:
            xo_ref, h_ref, a_ref, b_ref = rest
        xv = x_ref[...]
        hb = _rms_fwd(xv, g_ref[...])[2].astype(BF16)
        h_ref[...] = hb
        acc = jnp.zeros((TM, D_MODEL), F32)
        for s in range(N_SHARD):
            a = _dot_nt(hb, wg_ref[s])
            b = _dot_nt(hb, wu_ref[s])
            a_ref[s] = a.astype(BF16)
            b_ref[s] = b.astype(BF16)
            sv = a * _sigmoid(a) * b
            acc += _dot(sv.astype(BF16), wd_ref[s])
        xo = xv + 0.5 * acc
        if not head:
            xo_ref[...] = xo
            return

        @pl.when(pl.program_id(0) == 0)
        def _():
            loss_ref[...] = jnp.zeros_like(loss_ref)
            dgf_ref[...] = jnp.zeros_like(dgf_ref)

        gf = gf_ref[...]
        xhat, r, y = _rms_fwd(xo, gf)
        err = y - t_ref[...]
        loss_ref[...] += 0.5 * jnp.sum(jnp.mean(err * err, axis=-1, keepdims=True), axis=0, keepdims=True)
        dxn, dgf = _rms_bwd(err * (1.0 / D_MODEL), xhat, r, gf)
        xo_ref[...] = dxn
        dgf_ref[...] += dgf

    tok = jax.ShapeDtypeStruct((T, D_MODEL), F32)
    act = jax.ShapeDtypeStruct((N_SHARD, T, FF_S), BF16)
    return _call(
        body, payload, name=name, grid=(T // TM,), when=_edges(T // TM), sem=("arbitrary",),
        in_specs=[_row_spec(TM, D_MODEL), _const_spec((1, D_MODEL)), _const_spec(wg.shape), _const_spec(wu.shape),
                  _const_spec(wd.shape)] + ([_row_spec(TM, D_MODEL), _const_spec((1, D_MODEL))] if head else []),
        out_specs=[_row_spec(TM, D_MODEL), _row_spec(TM, D_MODEL), _row3_spec(TM, FF_S), _row3_spec(TM, FF_S)]
        + ([_acc_spec((1, 128)), _acc_spec((1, D_MODEL))] if head else []),
        out_shape=[tok, jax.ShapeDtypeStruct((T, D_MODEL), BF16), act, act]
        + ([jax.ShapeDtypeStruct((1, 128), F32), jax.ShapeDtypeStruct((1, D_MODEL), F32)] if head else []),
        operands=(x, g, wg, wu, wd) + (tuple(head) if head else ()))


def _ffn_dgrad(dout, x, a, b, g, wg, wu, wd, name, payload=None):
    T = x.shape[0]

    def body(do_ref, x_ref, a_ref, b_ref, g_ref, wg_ref, wu_ref, wd_ref, dx_ref, da_ref, db_ref, dg_ref):
        do = do_ref[...]
        dob = (0.5 * do).astype(BF16)
        dh = jnp.zeros((TM, D_MODEL), F32)
        ds_next = _dot_nt(dob, wd_ref[0])
        for s in range(N_SHARD):
            ds = ds_next
            if s + 1 < N_SHARD:
                ds_next = _dot_nt(dob, wd_ref[s + 1])
            for r0 in range(0, TM, DGRAD_ROWS):
                rows = slice(r0, r0 + DGRAD_ROWS)
                av = a_ref[s, rows, :].astype(F32)
                bv = b_ref[s, rows, :].astype(F32)
                sig = _sigmoid(av)
                dsr = ds[rows]
                da_ref[s, rows, :] = (dsr * bv * (sig * (1.0 + av * (1.0 - sig)))).astype(BF16)
                db_ref[s, rows, :] = (dsr * (av * sig)).astype(BF16)
            dh += _dot(da_ref[s], wg_ref[s]) + _dot(db_ref[s], wu_ref[s])
        gv = g_ref[...]
        xhat, r, _ = _rms_fwd(x_ref[...], gv)
        dxn, dg = _rms_bwd(dh, xhat, r, gv)
        dx_ref[...] = do + dxn

        @pl.when(pl.program_id(0) == 0)
        def _():
            dg_ref[...] = jnp.zeros_like(dg_ref)

        dg_ref[...] += dg

    return _call(
        body, payload, name=name, grid=(T // TM,), when=_edges(T // TM), sem=("arbitrary",),
        in_specs=[_row_spec(TM, D_MODEL), _row_spec(TM, D_MODEL), _row3_spec(TM, FF_S), _row3_spec(TM, FF_S),
                  _const_spec((1, D_MODEL)), _const_spec(wg.shape), _const_spec(wu.shape), _const_spec(wd.shape)],
        out_specs=[_row_spec(TM, D_MODEL), _row3_spec(TM, FF_S), _row3_spec(TM, FF_S), _acc_spec((1, D_MODEL))],
        out_shape=[jax.ShapeDtypeStruct((T, D_MODEL), F32), jax.ShapeDtypeStruct((N_SHARD, T, FF_S), BF16),
                   jax.ShapeDtypeStruct((N_SHARD, T, FF_S), BF16), jax.ShapeDtypeStruct((1, D_MODEL), F32)],
        operands=(dout, x, a, b, g, wg, wu, wd))


def _ffn_wgrad(h, dout, a, b, da, db, name, payload=None):
    T = h.shape[0]

    def body(h_ref, do_ref, a_ref, b_ref, da_ref, db_ref, gwg_ref, gwu_ref, gwd_ref):
        @pl.when(pl.program_id(1) == 0)
        def _():
            gwg_ref[...] = jnp.zeros_like(gwg_ref)
            gwu_ref[...] = jnp.zeros_like(gwu_ref)
            gwd_ref[...] = jnp.zeros_like(gwd_ref)

        hv = h_ref[...]
        gwg_ref[0] += _dot_tn(da_ref[0], hv)
        gwu_ref[0] += _dot_tn(db_ref[0], hv)
        dob = do_ref[...].astype(BF16)
        av = a_ref[0].astype(F32)
        sv = (0.5 * av * _sigmoid(av) * b_ref[0].astype(F32)).astype(BF16)
        gwd_ref[0] += _dot_tn(sv, dob)

    tw = min(TW, T)
    tok = pl.BlockSpec((tw, D_MODEL), lambda s, i: (i, 0))
    act = pl.BlockSpec((1, tw, FF_S), lambda s, i: (s, i, 0))
    return _call(
        body, payload, name=name, grid=(N_SHARD, T // tw), when=_edges(N_SHARD * (T // tw)),
        sem=("arbitrary", "arbitrary"),
        in_specs=[tok, tok, act, act, act, act],
        out_specs=[pl.BlockSpec((1, FF_S, D_MODEL), lambda s, i: (s, 0, 0))] * 3,
        out_shape=[jax.ShapeDtypeStruct((N_SHARD, FF_S, D_MODEL), F32)] * 3,
        operands=(h, dout, a, b, da, db))


def _in_fwd(x, g, w_in, payload=None):
    T = x.shape[0]

    def body(x_ref, g_ref, w_ref, h_ref, qkv_ref, zs_ref, gl_ref):
        hb = _rms_fwd(x_ref[...], g_ref[...])[2].astype(BF16)
        h_ref[...] = hb
        z0 = _dot(hb, w_ref[0])
        qkv_ref[:, 0:IN_S] = z0.astype(BF16)
        z1 = _dot(hb, w_ref[1])
        qkv_ref[:, IN_S:3 * D_ATT] = z1[:, 0:384].astype(BF16)
        zs_ref[:, 0:768] = z1[:, 384:IN_S]
        z2 = _dot(hb, w_ref[2])
        zs_ref[:, 768:1024] = z2[:, 0:256]
        gl_ref[:, 0:896] = z2[:, 256:IN_S]
        gl_ref[:, 896:2048] = _dot(hb, w_ref[3])

    return _call(
        body, payload, name="in_fwd", grid=(T // TM,), when=_edges(T // TM), sem=("arbitrary",),
        in_specs=[_row_spec(TM, D_MODEL), _const_spec((1, D_MODEL)), _const_spec(w_in.shape)],
        out_specs=[_row_spec(TM, D_MODEL), _row_spec(TM, 3 * D_ATT), _row_spec(TM, 2 * D_SGU), _row_spec(TM, 2 * D_MODEL)],
        out_shape=[jax.ShapeDtypeStruct((T, D_MODEL), BF16), jax.ShapeDtypeStruct((T, 3 * D_ATT), BF16),
                   jax.ShapeDtypeStruct((T, 2 * D_SGU), F32), jax.ShapeDtypeStruct((T, 2 * D_MODEL), F32)],
        operands=(x, g, w_in))


def _in_dgrad(dx_res, x, g, w_in, dq, dk, dv, dzs, dgl):
    T = x.shape[0]

    def body(dxr_ref, x_ref, g_ref, w_ref, dq_ref, dk_ref, dv_ref, dzs_ref, dgl_ref, dx_ref, dz_ref, dg_ref):
        dz = jnp.concatenate([dq_ref[...], dk_ref[...].astype(BF16), dv_ref[...].astype(BF16), dzs_ref[...], dgl_ref[...]],
                             axis=1)
        dz_ref[...] = dz
        dh = jnp.zeros((TM, D_MODEL), F32)
        for s in range(N_SHARD):
            dh += _dot_nt(dz[:, s * IN_S:(s + 1) * IN_S], w_ref[s])
        gv = g_ref[...]
        xhat, r, _ = _rms_fwd(x_ref[...], gv)
        dxn, dg = _rms_bwd(dh, xhat, r, gv)
        dx_ref[...] = dxr_ref[...] + dxn

        @pl.when(pl.program_id(0) == 0)
        def _():
            dg_ref[...] = jnp.zeros_like(dg_ref)

        dg_ref[...] += dg

    pad_blocks = ATT_PAD // TM
    return pl.pallas_call(
        body, name="in_dgrad", grid=(T // TM,),
        in_specs=[_row_spec(TM, D_MODEL), _row_spec(TM, D_MODEL), _const_spec((1, D_MODEL)), _const_spec(w_in.shape),
                  _row_spec(TM, D_ATT), _row_spec(TM, D_ATT, pad_blocks), _row_spec(TM, D_ATT, pad_blocks),
                  _row_spec(TM, 2 * D_SGU), _row_spec(TM, 2 * D_MODEL)],
        out_specs=[_row_spec(TM, D_MODEL), _row_spec(TM, D_IN), _acc_spec((1, D_MODEL))],
        out_shape=[jax.ShapeDtypeStruct((T, D_MODEL), F32), jax.ShapeDtypeStruct((T, D_IN), BF16),
                   jax.ShapeDtypeStruct((1, D_MODEL), F32)],
        compiler_params=_params(("arbitrary",)),
    )(dx_res, x, g, w_in, dq, dk, dv, dzs, dgl)


def _in_wgrad(h, dz):
    T = h.shape[0]

    def body(h_ref, dz_ref, gw_ref):
        @pl.when(pl.program_id(1) == 0)
        def _():
            gw_ref[...] = jnp.zeros_like(gw_ref)

        gw_ref[0] += _dot_tn(h_ref[...], dz_ref[...])

    return pl.pallas_call(
        body, name="in_wgrad", grid=(N_SHARD, T // min(TW, T)),
        in_specs=[pl.BlockSpec((min(TW, T), D_MODEL), lambda s, i: (i, 0)),
                  pl.BlockSpec((min(TW, T), IN_S), lambda s, i: (i, s))],
        out_specs=pl.BlockSpec((1, D_MODEL, IN_S), lambda s, i: (s, 0, 0)),
        out_shape=jax.ShapeDtypeStruct((N_SHARD, D_MODEL, IN_S), F32),
        compiler_params=_params(("arbitrary", "arbitrary")),
    )(h, dz)


def _rel_onehot():
    r = lax.broadcasted_iota(jnp.int32, (REL_PAD, REL_PAD), 0)
    n = lax.broadcasted_iota(jnp.int32, (REL_PAD, REL_PAD), 1)
    idx = jnp.clip(BAND - 1 - n, -REL_CLIP, REL_CLIP) + REL_CLIP
    return jnp.where(r == idx, 1.0, 0.0).astype(BF16)


def _split3(v):
    p1 = v.astype(BF16)
    r1 = v - p1.astype(F32)
    p2 = r1.astype(BF16)
    p3 = (r1 - p2.astype(F32)).astype(BF16)
    return p1, p2, p3


def _relbias_fwd(tab_pad):
    def body(t_ref, o_ref):
        oh = _rel_onehot()
        acc = jnp.zeros((HEADS, REL_PAD), F32)
        for p in _split3(t_ref[...]):
            acc += _dot(p, oh)
        o_ref[...] = acc

    return pl.pallas_call(body, name="relbias_fwd", out_shape=jax.ShapeDtypeStruct((HEADS, REL_PAD), F32))(tab_pad)


def _relbias_bwd(z):
    def body(z_ref, o_ref):
        oh = _rel_onehot()
        dt2 = jnp.sum(z_ref[...], axis=1)
        acc = jnp.zeros((HEADS, REL_PAD), F32)
        for p in _split3(dt2):
            acc += _dot_nt(p, oh)
        o_ref[...] = acc

    return pl.pallas_call(body, name="relbias_bwd", out_shape=jax.ShapeDtypeStruct((HEADS, REL_PAD), F32))(z)


def _bias_blocks(t2):
    flat = jnp.tile(t2, (1, CHUNK))
    skew = flat[:, :CHUNK * (REL_PAD - 1)].reshape(HEADS, CHUNK, REL_PAD - 1)
    bias = skew[:, :, CHUNK - 1:CHUNK - 1 + BAND]
    slabs = [jnp.pad(bias, ((0, 0), (0, 0), (CHUNK * c, ATT_KEYS - BAND - CHUNK * c)), constant_values=NEG_INF)
             for c in range(2)]
    return jnp.concatenate(slabs, axis=1)


def _unskew(db2):
    out = []
    for c in range(2):
        slab = db2[:, CHUNK * c:CHUNK * (c + 1), CHUNK * c:CHUNK * c + BAND]
        y = jnp.pad(slab, ((0, 0), (0, 0), (CHUNK - 1, REL_PAD - BAND - CHUNK + 1)))
        yf = jnp.pad(y.reshape(HEADS, CHUNK * REL_PAD), ((0, 0), (0, CHUNK)))
        out.append(yf.reshape(HEADS, CHUNK, REL_PAD + 1)[:, :, :REL_PAD])
    return jnp.concatenate(out, axis=1)


def _att_load(qkv_hbm, q_s, k_s, v_s, sem, T):
    copies = [pltpu.make_async_copy(qkv_hbm.at[:, 0:D_ATT], q_s, sem.at[0]),
              pltpu.make_async_copy(qkv_hbm.at[:, D_ATT:2 * D_ATT], k_s.at[pl.ds(ATT_PAD, T), :], sem.at[1]),
              pltpu.make_async_copy(qkv_hbm.at[:, 2 * D_ATT:3 * D_ATT], v_s.at[pl.ds(ATT_PAD, T), :], sem.at[2])]
    for cp in copies:
        cp.start()
    k_s[0:ATT_PAD, :] = jnp.zeros((ATT_PAD, D_ATT), BF16)
    v_s[0:ATT_PAD, :] = jnp.zeros((ATT_PAD, D_ATT), BF16)
    for cp in copies:
        cp.wait()


def _head(v, h):
    return v[:, h * HEAD_DIM:(h + 1) * HEAD_DIM]


def _rows(v, h):
    return v[h * ATT_ROWS:(h + 1) * ATT_ROWS]


def _att_exp(qs, kw, bias_ref, valid):
    s = jnp.concatenate([_dot_nt(_head(qs, h), _head(kw, h)) + bias_ref[h] for h in range(HEADS)], axis=0)
    if valid is not None:
        s = jnp.where(valid, s, NEG_INF)
    e = jnp.exp(s - jnp.max(s, axis=-1, keepdims=True))
    return e, 1.0 / jnp.sum(e, axis=-1, keepdims=True)


def _att_blocks(T, block, keys_on_rows=False, middle=None):
    n_edge = min(ATT_PAD // ATT_ROWS, T // ATT_ROWS)
    shape, axis = ((ATT_KEYS, 1), 0) if keys_on_rows else ((1, ATT_KEYS), 1)

    def edge(i, carry):
        r0 = i * ATT_ROWS
        block(i, (lax.broadcasted_iota(jnp.int32, shape, axis) + (r0 - ATT_PAD)) >= 0)
        return carry

    def inner(i, carry):
        block(i, None)
        return carry

    n_blocks = T // ATT_ROWS
    lax.fori_loop(0, n_edge, edge, 0)
    if middle is None:
        lax.fori_loop(n_edge, n_blocks, inner, 0)
        return
    n_late = max(n_blocks - n_blocks // 4, n_edge)
    lax.fori_loop(n_edge, n_late, inner, 0)
    middle()
    lax.fori_loop(n_late, n_blocks, inner, 0)


def _att_fwd(qkv, bias2, payload=None):
    T = qkv.shape[0]

    def body(qkv_hbm, bias_ref, y_ref, q_s, k_s, v_s, sem, middle=None):
        _att_load(qkv_hbm, q_s, k_s, v_s, sem, T)

        def block(i, valid):
            r0 = pl.multiple_of(i * ATT_ROWS, ATT_ROWS)
            qs = q_s[pl.ds(r0, ATT_ROWS), :] * (HEAD_DIM ** -0.5)
            kw = k_s[pl.ds(r0, ATT_KEYS), :]
            vw = v_s[pl.ds(r0, ATT_KEYS), :]
            e, rinv = _att_exp(qs, kw, bias_ref, valid)
            eb = e.astype(BF16)
            outs = [_dot(_rows(eb, h), _head(vw, h)) * _rows(rinv, h) for h in range(HEADS)]
            y_ref[pl.ds(r0, ATT_ROWS), :] = jnp.concatenate(outs, axis=1).astype(BF16)

        _att_blocks(T, block, middle=middle)

    return _call(
        body, payload, name="att_fwd", grid=None, takes_middle=True,
        in_specs=[pl.BlockSpec(memory_space=pl.ANY), pl.BlockSpec(memory_space=pltpu.VMEM)],
        out_specs=[pl.BlockSpec(memory_space=pltpu.VMEM)],
        out_shape=[jax.ShapeDtypeStruct((T, D_ATT), BF16)],
        scratch_shapes=[pltpu.VMEM((T, D_ATT), BF16), pltpu.VMEM((T + ATT_PAD, D_ATT), BF16),
                        pltpu.VMEM((T + ATT_PAD, D_ATT), BF16), pltpu.SemaphoreType.DMA((3,))],
        operands=(qkv, bias2))


def _lanes(v, h):
    return v[:, h * ATT_ROWS:(h + 1) * ATT_ROWS]


def _att_bwd(qkv, dy, bias2t, payload=None):
    T = qkv.shape[0]

    def body(qkv_hbm, dy_ref, bias_ref, dq_ref, dk_ref, dv_ref, db_ref, q_s, k_s, v_s, sem):
        _att_load(qkv_hbm, q_s, k_s, v_s, sem, T)
        dk_ref[...] = jnp.zeros_like(dk_ref)
        dv_ref[...] = jnp.zeros_like(dv_ref)
        db_ref[...] = jnp.zeros_like(db_ref)

        def block(i, valid):
            r0 = pl.multiple_of(i * ATT_ROWS, ATT_ROWS)
            qs = q_s[pl.ds(r0, ATT_ROWS), :] * (HEAD_DIM ** -0.5)
            kw = k_s[pl.ds(r0, ATT_KEYS), :]
            vw = v_s[pl.ds(r0, ATT_KEYS), :]
            dyb = dy_ref[pl.ds(r0, ATT_ROWS), :]
            s = jnp.concatenate([_dot_nt(_head(kw, h), _head(qs, h)) + bias_ref[h] for h in range(HEADS)], axis=1)
            if valid is not None:
                s = jnp.where(valid, s, NEG_INF)
            e = jnp.exp(s - jnp.max(s, axis=0, keepdims=True))
            p = e * (1.0 / jnp.sum(e, axis=0, keepdims=True))
            dp = jnp.concatenate([_dot_nt(_head(vw, h), _head(dyb, h)) for h in range(HEADS)], axis=1)
            ds = p * (dp - jnp.sum(p * dp, axis=0, keepdims=True))
            for h in range(HEADS):
                db_ref[h] += _lanes(ds, h)
            dsb = ds.astype(BF16)
            pb = p.astype(BF16)
            dq = [_dot_tn(_lanes(dsb, h), _head(kw, h)) for h in range(HEADS)]
            dk = [_dot(_lanes(dsb, h), _head(qs, h)) for h in range(HEADS)]
            dv = [_dot(_lanes(pb, h), _head(dyb, h)) for h in range(HEADS)]
            dq_ref[pl.ds(r0, ATT_ROWS), :] = (jnp.concatenate(dq, axis=1) * (HEAD_DIM ** -0.5)).astype(BF16)
            dk_ref[pl.ds(r0, ATT_KEYS), :] += jnp.concatenate(dk, axis=1)
            dv_ref[pl.ds(r0, ATT_KEYS), :] += jnp.concatenate(dv, axis=1)

        _att_blocks(T, block, keys_on_rows=True)

    vmem = pl.BlockSpec(memory_space=pltpu.VMEM)
    return _call(
        body, payload, name="att_bwd", grid=None,
        in_specs=[pl.BlockSpec(memory_space=pl.ANY), vmem, vmem],
        out_specs=[vmem, vmem, vmem, vmem],
        out_shape=[jax.ShapeDtypeStruct((T, D_ATT), BF16), jax.ShapeDtypeStruct((T + ATT_PAD, D_ATT), F32),
                   jax.ShapeDtypeStruct((T + ATT_PAD, D_ATT), F32), jax.ShapeDtypeStruct((HEADS, ATT_KEYS, ATT_ROWS), F32)],
        scratch_shapes=[pltpu.VMEM((T, D_ATT), BF16), pltpu.VMEM((T + ATT_PAD, D_ATT), BF16),
                        pltpu.VMEM((T + ATT_PAD, D_ATT), BF16), pltpu.SemaphoreType.DMA((3,))],
        operands=(qkv, dy, bias2t))


_GELU_C = 0.7978845608028654
_GELU_A = 0.044715


def _gelu(x):
    t = jnp.tanh(_GELU_C * (x + _GELU_A * x * x * x))
    return 0.5 * x * (1.0 + t), t


def _gelu_grad(x, t):
    return 0.5 * (1.0 + t) + 0.5 * x * (1.0 - t * t) * _GELU_C * (1.0 + 3.0 * _GELU_A * x * x)


def _group_masks():
    col = lax.broadcasted_iota(jnp.int32, (SGU_GROUPS, D_SGU), 1) // SGU_GDIM
    grp = lax.broadcasted_iota(jnp.int32, (SGU_GROUPS, D_SGU), 0)
    return jnp.where(col == grp, 1.0, 0.0).astype(F32)


def _causal_mask(transposed=False):
    i = lax.broadcasted_iota(jnp.int32, (SGU_BLOCK, SGU_BLOCK), 0) // CHUNK
    j = lax.broadcasted_iota(jnp.int32, (SGU_BLOCK, SGU_BLOCK), 1) // CHUNK
    return (j >= i) if transposed else (i >= j)


def _sgu_norm(zs, lng, lnb):
    gz, t = _gelu(zs)
    u = gz[:, 0:D_SGU]
    vs = gz[:, D_SGU:2 * D_SGU]
    xc = vs - jnp.mean(vs, axis=-1, keepdims=True)
    rstd = lax.rsqrt(jnp.mean(xc * xc, axis=-1, keepdims=True) + EPS)
    xhat = xc * rstd
    return t, u, xhat, rstd, xhat * lng + lnb


def _sgu_mix(vn_blk, w_ref, bst, gm):
    mask = _causal_mask()
    s = jnp.zeros((SGU_BLOCK, D_SGU), F32)
    for g in range(SGU_GROUPS):
        wm = jnp.where(mask, w_ref[g], 0.0).astype(BF16)
        s += _dot(wm, (vn_blk * gm[g:g + 1, :]).astype(BF16))
        s += bst[:, g:g + 1] * gm[g:g + 1, :]
    return s


def _sgu_fwd(zs, lng, lnb, w_s, bst):
    T = zs.shape[0]
    nblk = TM // SGU_BLOCK

    def body(zs_ref, lng_ref, lnb_ref, w_ref, bst_ref, y_ref):
        _, u, _, _, vn = _sgu_norm(zs_ref[...], lng_ref[...], lnb_ref[...])
        gm = _group_masks()
        bst_v = bst_ref[...]
        for n in range(nblk):
            rows = slice(n * SGU_BLOCK, (n + 1) * SGU_BLOCK)
            s = _sgu_mix(vn[rows], w_ref, bst_v, gm)
            y_ref[rows, :] = (u[rows] * s).astype(BF16)

    return pl.pallas_call(
        body, name="sgu_fwd", grid=(T // TM,),
        in_specs=[_row_spec(TM, 2 * D_SGU), _const_spec((1, D_SGU)), _const_spec((1, D_SGU)),
                  _const_spec(w_s.shape), _const_spec(bst.shape)],
        out_specs=_row_spec(TM, D_SGU),
        out_shape=jax.ShapeDtypeStruct((T, D_SGU), BF16),
        compiler_params=_params(("arbitrary",)),
    )(zs, lng, lnb, w_s, bst)


def _sgu_bwd(zs, dy, lng, lnb, w_s, w_st, bst):
    T = zs.shape[0]
    nblk = TM // SGU_BLOCK

    def body(zs_ref, dy_ref, lng_ref, lnb_ref, w_ref, wt_ref, bst_ref, dzs_ref, dw_ref, dbt_ref, dlg_ref, dlb_ref):
        @pl.when(pl.program_id(0) == 0)
        def _():
            dw_ref[...] = jnp.zeros_like(dw_ref)
            dbt_ref[...] = jnp.zeros_like(dbt_ref)
            dlg_ref[...] = jnp.zeros_like(dlg_ref)
            dlb_ref[...] = jnp.zeros_like(dlb_ref)

        zs_v = zs_ref[...]
        lng_v = lng_ref[...]
        t, u, xhat, rstd, vn = _sgu_norm(zs_v, lng_v, lnb_ref[...])
        gm = _group_masks()
        bst_v = bst_ref[...]
        mask = _causal_mask()
        mask_t = _causal_mask(transposed=True)
        dyv = dy_ref[...].astype(F32)
        lane8 = lax.broadcasted_iota(jnp.int32, (1, SGU_GROUPS), 1)
        du_rows, dvn_rows = [], []
        for n in range(nblk):
            rows = slice(n * SGU_BLOCK, (n + 1) * SGU_BLOCK)
            vn_b = vn[rows]
            s = _sgu_mix(vn_b, w_ref, bst_v, gm)
            du_rows.append(dyv[rows] * s)
            dsb = dyv[rows] * u[rows]
            vnb16 = vn_b.astype(BF16)
            dvn = jnp.zeros((SGU_BLOCK, D_SGU), F32)
            dbt = jnp.zeros((SGU_BLOCK, SGU_GROUPS), F32)
            for g in range(SGU_GROUPS):
                dsg = dsb * gm[g:g + 1, :]
                dsg16 = dsg.astype(BF16)
                wmt = jnp.where(mask_t, wt_ref[g], 0.0).astype(BF16)
                dvn += _dot(wmt, dsg16)
                dw_ref[g] += jnp.where(mask, _dot_nt(dsg16, vnb16), 0.0)
                dbt += jnp.sum(dsg, axis=-1, keepdims=True) * jnp.where(lane8 == g, 1.0, 0.0)
            dbt_ref[...] += dbt
            dvn_rows.append(dvn)
        du = jnp.concatenate(du_rows, axis=0)
        dvn = jnp.concatenate(dvn_rows, axis=0)
        dlg_ref[...] += jnp.sum(dvn * xhat, axis=0, keepdims=True)
        dlb_ref[...] += jnp.sum(dvn, axis=0, keepdims=True)
        dxhat = dvn * lng_v
        dvs = rstd * (dxhat - jnp.mean(dxhat, axis=-1, keepdims=True)
                      - xhat * jnp.mean(dxhat * xhat, axis=-1, keepdims=True))
        dgz = jnp.concatenate([du, dvs], axis=1)
        dzs_ref[...] = (dgz * _gelu_grad(zs_v, t)).astype(BF16)

    return pl.pallas_call(
        body, name="sgu_bwd", grid=(T // TM,),
        in_specs=[_row_spec(TM, 2 * D_SGU), _row_spec(TM, D_SGU), _const_spec((1, D_SGU)), _const_spec((1, D_SGU)),
                  _const_spec(w_s.shape), _const_spec(w_st.shape), _const_spec(bst.shape)],
        out_specs=[_row_spec(TM, 2 * D_SGU), _acc_spec(w_s.shape), _acc_spec(bst.shape), _acc_spec((1, D_SGU)),
                   _acc_spec((1, D_SGU))],
        out_shape=[jax.ShapeDtypeStruct((T, 2 * D_SGU), BF16), jax.ShapeDtypeStruct(w_s.shape, F32),
                   jax.ShapeDtypeStruct(bst.shape, F32), jax.ShapeDtypeStruct((1, D_SGU), F32),
                   jax.ShapeDtypeStruct((1, D_SGU), F32)],
        compiler_params=_params(("arbitrary",)),
    )(zs, dy, lng, lnb, w_s, w_st, bst)


def _cols(v, s):
    return v[:, s * BR_S:(s + 1) * BR_S]


def _merge_fwd(x, y_att, y_sgu, gl, b_gate, wba, wbs, wo, payload=None):
    T = x.shape[0]

    def body(x_ref, ya_ref, ys_ref, gl_ref, bg_ref, wba_ref, wbs_ref, wo_ref, xo_ref, m_ref, pa_ref, ps_ref):
        ya = ya_ref[...]
        ys = ys_ref[...]
        pa = jnp.concatenate([_dot(ya, wba_ref[s]) for s in range(N_SHARD)], axis=1)
        ps = jnp.concatenate([_dot(ys, wbs_ref[s]) for s in range(N_SHARD)], axis=1)
        g = _sigmoid(gl_ref[...] + bg_ref[...])
        mb = (g[:, 0:D_MODEL] * pa + g[:, D_MODEL:2 * D_MODEL] * ps).astype(BF16)
        m_ref[...] = mb
        pa_ref[...] = pa.astype(BF16)
        ps_ref[...] = ps.astype(BF16)
        acc = jnp.zeros((TM, D_MODEL), F32)
        for s in range(N_SHARD):
            acc += _dot(_cols(mb, s), wo_ref[s])
        xo_ref[...] = x_ref[...] + acc

    tokd = jax.ShapeDtypeStruct((T, D_MODEL), BF16)
    return _call(
        body, payload, name="merge_fwd", grid=(T // TM,), when=_edges(T // TM), sem=("arbitrary",),
        in_specs=[_row_spec(TM, D_MODEL), _row_spec(TM, D_ATT), _row_spec(TM, D_SGU), _row_spec(TM, 2 * D_MODEL),
                  _const_spec((1, 2 * D_MODEL)), _const_spec(wba.shape), _const_spec(wbs.shape), _const_spec(wo.shape)],
        out_specs=[_row_spec(TM, D_MODEL)] * 4,
        out_shape=[jax.ShapeDtypeStruct((T, D_MODEL), F32), tokd, tokd, tokd],
        operands=(x, y_att, y_sgu, gl, b_gate, wba, wbs, wo))


def _merge_bwd(dx, y_att, y_sgu, gl, merged, pa, ps, b_gate, wba, wbs, wo, payload=None):
    T = dx.shape[0]

    def body(dx_ref, ya_ref, ys_ref, gl_ref, m_ref, pa_ref, ps_ref, bg_ref, wba_ref, wbs_ref, wo_ref,
             dya_ref, dys_ref, dgl_ref, dbg_ref, gwba_ref, gwbs_ref, gwo_ref):
        @pl.when(pl.program_id(0) == 0)
        def _():
            dbg_ref[...] = jnp.zeros_like(dbg_ref)
            gwba_ref[...] = jnp.zeros_like(gwba_ref)
            gwbs_ref[...] = jnp.zeros_like(gwbs_ref)
            gwo_ref[...] = jnp.zeros_like(gwo_ref)

        dxb = dx_ref[...].astype(BF16)
        dm = jnp.concatenate([_dot_nt(dxb, wo_ref[s]) for s in range(N_SHARD)], axis=1)
        g = _sigmoid(gl_ref[...] + bg_ref[...])
        ga = g[:, 0:D_MODEL]
        gs = g[:, D_MODEL:2 * D_MODEL]
        dpa = (dm * ga).astype(BF16)
        dps = (dm * gs).astype(BF16)
        dgl = jnp.concatenate([dm * pa_ref[...].astype(F32) * ga * (1.0 - ga),
                               dm * ps_ref[...].astype(F32) * gs * (1.0 - gs)], axis=1)
        dgl_ref[...] = dgl.astype(BF16)
        dbg_ref[...] += jnp.sum(dgl, axis=0, keepdims=True)
        ya = ya_ref[...]
        ys = ys_ref[...]
        mb = m_ref[...]
        dya = jnp.zeros((TM, D_ATT), F32)
        dys = jnp.zeros((TM, D_SGU), F32)
        for s in range(N_SHARD):
            dya += _dot_nt(_cols(dpa, s), wba_ref[s])
            dys += _dot_nt(_cols(dps, s), wbs_ref[s])
            gwo_ref[s] += _dot_tn(_cols(mb, s), dxb)
            gwba_ref[s] += _dot_tn(ya, _cols(dpa, s))
            gwbs_ref[s] += _dot_tn(ys, _cols(dps, s))
        dya_ref[...] = dya.astype(BF16)
        dys_ref[...] = dys.astype(BF16)

    return _call(
        body, payload, name="merge_bwd", grid=(T // TM,), when=_edges(T // TM), sem=("arbitrary",),
        operands=(dx, y_att, y_sgu, gl, merged, pa, ps, b_gate, wba, wbs, wo),
        in_specs=[_row_spec(TM, D_MODEL), _row_spec(TM, D_ATT), _row_spec(TM, D_SGU), _row_spec(TM, 2 * D_MODEL),
                  _row_spec(TM, D_MODEL), _row_spec(TM, D_MODEL), _row_spec(TM, D_MODEL),
                  _const_spec((1, 2 * D_MODEL)), _const_spec(wba.shape), _const_spec(wbs.shape), _const_spec(wo.shape)],
        out_specs=[_row_spec(TM, D_ATT), _row_spec(TM, D_SGU), _row_spec(TM, 2 * D_MODEL), _acc_spec((1, 2 * D_MODEL)),
                   _acc_spec(wba.shape), _acc_spec(wbs.shape), _acc_spec(wo.shape)],
        out_shape=[jax.ShapeDtypeStruct((T, D_ATT), BF16), jax.ShapeDtypeStruct((T, D_SGU), BF16),
                   jax.ShapeDtypeStruct((T, 2 * D_MODEL), BF16), jax.ShapeDtypeStruct((1, 2 * D_MODEL), F32),
                   jax.ShapeDtypeStruct(wba.shape, F32), jax.ShapeDtypeStruct(wbs.shape, F32),
                   jax.ShapeDtypeStruct(wo.shape, F32)])


BIG = ("ffn1_w_gate", "ffn1_w_up", "ffn1_w_down", "w_in", "w_branch_att", "w_branch_sgu", "w_out",
       "ffn2_w_gate", "ffn2_w_up", "ffn2_w_down")
SMALL = ("norm_ffn1", "norm_mix", "b_gate", "rel_bias", "sgu_ln_g", "sgu_ln_b", "sgu_w_s", "sgu_b_s", "norm_ffn2",
         "norm_final")


G_FFN1 = ("ffn1_w_gate", "ffn1_w_up", "ffn1_w_down")
G_MIX = ("w_in", "w_branch_att", "w_branch_sgu", "w_out")
G_FFN2 = ("ffn2_w_gate", "ffn2_w_up", "ffn2_w_down")


def _local_step(x, target, wb, ws, dist=None):
    def gather_on(names):
        return _ag_payload([wb[n] for n in names]) if dist else None

    t2 = _relbias_fwd(ws["rel_bias"])
    bias2 = _bias_blocks(t2)
    bst = ws["sgu_b_s"].T
    w_st = jnp.swapaxes(ws["sgu_w_s"], 1, 2)

    if dist:
        wb.update(zip(G_FFN1, _call(lambda: None, gather_on(G_FFN1), name="allgather_ffn1", grid=None, in_specs=[],
                                    out_specs=[], out_shape=[])))
    x1, h1, a1, b1, *got = _ffn_fwd(x, ws["norm_ffn1"], wb["ffn1_w_gate"], wb["ffn1_w_up"], wb["ffn1_w_down"],
                                    "ffn1_fwd", gather_on(G_MIX))
    wb.update(zip(G_MIX, got))
    h2, qkv, zs, gl, *got = _in_fwd(x1, ws["norm_mix"], wb["w_in"], gather_on(G_FFN2[0:1]))
    wb.update(zip(G_FFN2[0:1], got))
    y_att, *got = _att_fwd(qkv, bias2, gather_on(G_FFN2[1:2]))
    wb.update(zip(G_FFN2[1:2], got))
    y_sgu = _sgu_fwd(zs, ws["sgu_ln_g"], ws["sgu_ln_b"], ws["sgu_w_s"], bst)
    x2, merged, pa, ps, *got = _merge_fwd(x1, y_att, y_sgu, gl, ws["b_gate"], wb["w_branch_att"], wb["w_branch_sgu"],
                                          wb["w_out"], gather_on(G_FFN2[2:3]))
    wb.update(zip(G_FFN2[2:3], got))
    dx3, h3, a3, b3, loss, g_final = _ffn_fwd(x2, ws["norm_ffn2"], wb["ffn2_w_gate"], wb["ffn2_w_up"],
                                              wb["ffn2_w_down"], "ffn2_fwd", head=(target, ws["norm_final"]))

    gb, gs, sums = {}, {"norm_final": g_final}, {}

    def pair_on(names, small=None):
        return _px_payload([gb[n] for n in names], small) if dist else None

    def pair_add(names, halves):
        for n, rv in zip(names, halves):
            sums[n] = _pair_add(gb[n], rv, dist[0], dist[1], "pair_add_" + n)

    def chips_on(names):
        return _cx_payload([sums[n][1] for n in names], [sums[n][2] for n in names]) if dist else None

    dx2, da3, db3, gs["norm_ffn2"] = _ffn_dgrad(dx3, x2, a3, b3, ws["norm_ffn2"], wb["ffn2_w_gate"], wb["ffn2_w_up"],
                                                wb["ffn2_w_down"], "ffn2_dgrad")
    gb["ffn2_w_gate"], gb["ffn2_w_up"], gb["ffn2_w_down"] = _ffn_wgrad(h3, dx3, a3, b3, da3, db3, "ffn2_wgrad")
    dy_att, dy_sgu, dgl, gs["b_gate"], gb["w_branch_att"], gb["w_branch_sgu"], gb["w_out"], *got = _merge_bwd(
        dx2, y_att, y_sgu, gl, merged, pa, ps, ws["b_gate"], wb["w_branch_att"], wb["w_branch_sgu"], wb["w_out"],
        pair_on(G_FFN2))
    pair_add(G_FFN2, got)
    dq, dk, dv, db2t, *lands2 = _att_bwd(qkv, dy_att, jnp.swapaxes(bias2, 1, 2), chips_on(G_FFN2))
    gs["rel_bias"] = _relbias_bwd(_unskew(jnp.swapaxes(db2t, 1, 2)))
    dzs, gs["sgu_w_s"], dbt, gs["sgu_ln_g"], gs["sgu_ln_b"] = _sgu_bwd(zs, dy_sgu, ws["sgu_ln_g"], ws["sgu_ln_b"],
                                                                      ws["sgu_w_s"], w_st, bst)
    gs["sgu_b_s"] = dbt.T
    dx1, dz, gs["norm_mix"] = _in_dgrad(dx2, x1, ws["norm_mix"], wb["w_in"], dq, dk, dv, dzs, dgl)
    gb["w_in"] = _in_wgrad(h2, dz)
    gx, da1, db1, gs["norm_ffn1"], *got = _ffn_dgrad(dx1, x, a1, b1, ws["norm_ffn1"], wb["ffn1_w_gate"],
                                                    wb["ffn1_w_up"], wb["ffn1_w_down"], "ffn1_dgrad", pair_on(G_MIX))
    pair_add(G_MIX, got)
    gb["ffn1_w_gate"], gb["ffn1_w_up"], gb["ffn1_w_down"], *lands_mix = _ffn_wgrad(h1, dx1, a1, b1, da1, db1,
                                                                                   "ffn1_wgrad", chips_on(G_MIX))
    if not dist:
        return loss, gx, gb, gs

    def final_sums(names, lands):
        return [_final_sum(sums[n][0], land, dist[1], dist[0], "final_sum_" + n) for n, land in zip(names, lands)]

    early = G_FFN2 + G_MIX
    k = len(G_FFN1)
    tail = _tail_reduce([gb[n] for n in G_FFN1], _pack_small(gs, loss),
                        _ss_payload(final_sums(early, list(lands2) + list(lands_mix))))
    for i, n in enumerate(G_FFN1):
        sums[n] = (tail[i],)
    small_sums, shared = tail[2 * k], dict(zip(early, tail[2 * k + 1:]))
    shared.update(zip(G_FFN1, _sibling_share(final_sums(G_FFN1, tail[k:2 * k]), "sibling_share")))
    return loss, gx, shared, small_sums


_ANY = pl.BlockSpec(memory_space=pl.ANY)
_VMEM = pl.BlockSpec(memory_space=pltpu.VMEM)


def _mesh_pos():
    return lax.axis_index("x"), lax.axis_index("y"), lax.axis_index("c")


def _cast_slots(shards, chip, name):
    n = len(shards)
    r, ncol = shards[0].shape
    tr = r // 2

    def body(me_ref, *refs):
        for i_ref, o_ref in zip(refs[:n], refs[n:]):
            o_ref[0] = i_ref[...].astype(BF16)

    grid_spec = pltpu.PrefetchScalarGridSpec(
        num_scalar_prefetch=1, grid=(r // tr,),
        in_specs=[pl.BlockSpec((tr, ncol), lambda i, me: (i, 0))] * n,
        out_specs=[pl.BlockSpec((1, tr, ncol), lambda i, me: (me[0], i, 0))] * n)
    return pl.pallas_call(
        body, name=name, grid_spec=grid_spec,
        out_shape=[jax.ShapeDtypeStruct((N_SHARD, r, ncol), BF16)] * n,
        compiler_params=_params(("arbitrary",)),
    )(chip, *shards)


class _Payload:
    def __init__(self, arrays, out_shapes, aliases, scratch, phases):
        self.arrays = list(arrays)
        self.out_shapes = list(out_shapes)
        self.aliases = dict(aliases)
        self.scratch = list(scratch)
        self.phases = phases


def _remote(src, dst, ssem, rsem, dev):
    return pltpu.make_async_remote_copy(src_ref=src, dst_ref=dst, send_sem=ssem, recv_sem=rsem, device_id=dev,
                                        device_id_type=MESH)


def _call(body, payload, *, name, grid, in_specs, out_specs, out_shape, scratch_shapes=(), sem=None, when=None,
          operands=(), takes_middle=False):
    in_specs, out_specs, out_shape = list(in_specs), list(out_specs), list(out_shape)
    scratch_shapes = list(scratch_shapes)
    n_in, n_out, n_scr = len(in_specs), len(out_specs), len(scratch_shapes)
    kwargs = {}
    kernel = body
    if payload is not None:
        k_in, k_out = len(payload.arrays), len(payload.out_shapes)
        rank = len(grid) if grid else 0

        def kernel(*refs):
            a, b = n_in, n_in + k_in
            c, d = b + n_out, b + n_out + k_out
            e = d + n_scr
            phases = payload.phases(refs[a:b], refs[c:d], refs[e:])

            def run():
                body(*refs[:a], *refs[b:c], *refs[d:e])

            if not grid:
                phases[0]()
                if len(phases) == 3 and takes_middle:
                    body(*refs[:a], *refs[b:c], *refs[d:e], middle=phases[1])
                    phases[2]()
                    return
                run()
                for ph in phases[1:]:
                    ph()
                return
            step = pl.program_id(0)
            if rank == 2:
                step = step * grid[1] + pl.program_id(1)
            marks = list(when)
            if len(phases) == 3:
                marks = [when[0], (max(when[1][0] - 3, 0), False), when[1]]
            for ph, (at, before) in zip(phases, marks):
                if before:
                    pl.when(step == at)(ph)
            run()
            for ph, (at, before) in zip(phases, marks):
                if not before:
                    pl.when(step == at)(ph)

        in_specs += [_ANY] * k_in
        out_specs += [_ANY] * k_out
        out_shape += payload.out_shapes
        scratch_shapes += payload.scratch
        kwargs["input_output_aliases"] = {n_in + i: n_out + j for i, j in payload.aliases.items()}
        operands = tuple(operands) + tuple(payload.arrays)
    if grid:
        kwargs["grid"] = grid
    return pl.pallas_call(kernel, name=name, in_specs=in_specs, out_specs=out_specs, out_shape=out_shape,
                          scratch_shapes=scratch_shapes, compiler_params=_params(sem), **kwargs)(*operands)


def _ag_payload(slots):
    n = len(slots)

    def phases(_, refs, sems):
        send_i, recv_i, send_d, recv_d = sems
        x, y, c = _mesh_pos()
        me = 2 * x + y

        def half(w, core):
            rh = slots[w].shape[1] // 2
            return pl.ds(core * rh, rh)

        def ici(w, j):
            t = (me + 1 + j) % N_SHARD
            mine = refs[w].at[me, half(w, c), :]
            return _remote(mine, mine, send_i.at[3 * w + j], recv_i.at[3 * w + j], (t // 2, t % 2, c))

        def d2d(w, j, core):
            s = (me + 3 - j) % N_SHARD
            land = refs[w].at[s, half(w, core), :]
            return _remote(land, land, send_d.at[3 * w + j], recv_d.at[3 * w + j], (x, y, 1 - c))

        def start():
            for w in range(n):
                for j in range(3):
                    ici(w, j).start()

        def forward():
            for w in range(n):
                for j in range(3):
                    s = (me + 3 - j) % N_SHARD
                    land = refs[w].at[s, half(w, c), :]
                    _remote(land, land, send_i.at[3 * w + j], recv_i.at[3 * w + j], (x, y, c)).wait_recv()
                    d2d(w, j, c).start()

        def finish():
            for w in range(n):
                for j in range(3):
                    d2d(w, j, 1 - c).wait_recv()
            for w in range(n):
                for j in range(3):
                    ici(w, j).wait_send()
                    d2d(w, j, c).wait_send()

        return [start, forward, finish]

    return _Payload(slots, [jax.ShapeDtypeStruct(s.shape, s.dtype) for s in slots], {i: i for i in range(n)},
                    [pltpu.SemaphoreType.DMA((3 * n,)) for _ in range(4)], phases)


def _px_payload(grads, small=None):
    arrays = list(grads) + ([small] if small is not None else [])
    n = len(arrays)

    def phases(ins, outs, sems):
        send, recv = sems
        x, y, c = _mesh_pos()

        def copy(w):
            if w < len(grads):
                rh = grads[w].shape[1] // 2
                src = ins[w].at[:, pl.ds((1 - c) * rh, rh), :]
            else:
                src = ins[w]
            return _remote(src, outs[w], send.at[w], recv.at[w], (x, y, 1 - c))

        def start():
            for w in range(n):
                copy(w).start()

        def finish():
            for w in range(n):
                copy(w).wait()

        return [start, finish]

    out_shapes = [jax.ShapeDtypeStruct((N_SHARD, g.shape[1] // 2, g.shape[2]), F32) for g in grads]
    if small is not None:
        out_shapes.append(jax.ShapeDtypeStruct(small.shape, F32))
    return _Payload(arrays, out_shapes, {}, [pltpu.SemaphoreType.DMA((n,)), pltpu.SemaphoreType.DMA((n,))], phases)


def _cx_payload(pbs, lands):
    n = len(pbs)

    def phases(ins, outs, sems):
        send, recv = sems
        x, y, c = _mesh_pos()
        me = 2 * x + y

        def copy(w, j):
            t = (me + 1 + j) % N_SHARD
            return _remote(ins[w].at[t], outs[w].at[me], send.at[3 * w + j], recv.at[3 * w + j], (t // 2, t % 2, c))

        def start():
            for w in range(n):
                for j in range(3):
                    copy(w, j).start()

        def finish():
            for w in range(n):
                for j in range(3):
                    copy(w, j).wait()

        return [start, finish]

    return _Payload(list(pbs) + list(lands), [jax.ShapeDtypeStruct(p.shape, BF16) for p in lands],
                    {n + i: i for i in range(n)},
                    [pltpu.SemaphoreType.DMA((3 * n,)), pltpu.SemaphoreType.DMA((3 * n,))], phases)


def _pair_add(g, rv, core, chip, name):
    _, r, ncol = g.shape
    rh = r // 2

    def body(c_ref, me_ref, g_ref, rv_ref, pf_ref, pb_ref, land_ref):
        s = g_ref[0] + rv_ref[0]
        sb = s.astype(BF16)
        pb_ref[0] = sb

        @pl.when(pl.program_id(0) == me_ref[0])
        def _():
            pf_ref[...] = s
            land_ref[0] = sb

    slot = pl.BlockSpec((1, rh, ncol), lambda s, c, me: (s, 0, 0))
    grid_spec = pltpu.PrefetchScalarGridSpec(
        num_scalar_prefetch=2, grid=(N_SHARD,),
        in_specs=[pl.BlockSpec((1, rh, ncol), lambda s, c, me: (s, c[0], 0)), slot],
        out_specs=[pl.BlockSpec((rh, ncol), lambda s, c, me: (0, 0)), slot,
                   pl.BlockSpec((1, rh, ncol), lambda s, c, me: (me[0], 0, 0))])
    return pl.pallas_call(
        body, name=name, grid_spec=grid_spec,
        out_shape=[jax.ShapeDtypeStruct((rh, ncol), F32), jax.ShapeDtypeStruct((N_SHARD, rh, ncol), BF16),
                   jax.ShapeDtypeStruct((N_SHARD, rh, ncol), BF16)],
        compiler_params=_params(("arbitrary",)),
    )(core, chip, g, rv)


def _tail_reduce(grads, small, payload=None):
    n = len(grads)
    _, r, ncol = grads[0].shape
    rh = r // 2

    def body(*refs, middle=None):
        g_hbm, sm = refs[:n], refs[n]
        pf, land, sm_out = refs[n + 1:2 * n + 1], refs[2 * n + 1:3 * n + 1], refs[3 * n + 1]
        scr = refs[3 * n + 2:]
        rv, mine, sendb = scr[:n], scr[n:2 * n], scr[2 * n:3 * n]
        sm_rv, sm_sum, d_send, d_recv, load, i_send, i_recv, store = scr[3 * n:]
        x, y, c = _mesh_pos()
        me = 2 * x + y
        sib = (x, y, 1 - c)

        def pair(w):
            src = g_hbm[w].at[:, pl.ds((1 - c) * rh, rh), :] if w < n else sm
            return _remote(src, rv[w] if w < n else sm_rv, d_send.at[w], d_recv.at[w], sib)

        def chips(w, j):
            t = (me + 1 + j) % N_SHARD
            src = sendb[w].at[t] if w < n else sm_sum
            dst = land[w].at[me] if w < n else sm_out.at[me]
            return _remote(src, dst, i_send.at[3 * w + j], i_recv.at[3 * w + j], (t // 2, t % 2, c))

        loads = [pltpu.make_async_copy(g_hbm[w].at[:, pl.ds(c * rh, rh), :], mine[w], load.at[w]) for w in range(n)]
        for w in range(n + 1):
            pair(w).start()
        for cp in loads:
            cp.start()
        stores = []
        for w in range(n):
            loads[w].wait()
            pair(w).wait_recv()
            for k in range(N_SHARD):
                s = mine[w][k] + rv[w][k]
                mine[w][k] = s
                sendb[w][k] = s.astype(BF16)
            stores += [pltpu.make_async_copy(mine[w].at[me], pf[w], store.at[2 * w]),
                       pltpu.make_async_copy(sendb[w].at[me], land[w].at[me], store.at[2 * w + 1])]
            for cp in stores[-2:]:
                cp.start()
            for j in range(3):
                chips(w, j).start()
        pair(n).wait_recv()
        if middle is not None:
            middle()
        sm_sum[...] = sm[...] + sm_rv[...]
        stores.append(pltpu.make_async_copy(sm_sum, sm_out.at[me], store.at[2 * n]))
        stores[-1].start()
        for j in range(3):
            chips(n, j).start()
        for w in range(n + 1):
            pair(w).wait_send()
            for j in range(3):
                chips(w, j).wait()
        for cp in stores:
            cp.wait()

    half = (N_SHARD, rh, ncol)
    return _call(
        body, payload, name="tail_reduce", grid=None, takes_middle=True,
        in_specs=[_ANY] * n + [_VMEM], out_specs=[_ANY] * (2 * n + 1),
        out_shape=([jax.ShapeDtypeStruct((rh, ncol), F32)] * n + [jax.ShapeDtypeStruct(half, BF16)] * n
                   + [jax.ShapeDtypeStruct((N_SHARD,) + small.shape, F32)]),
        scratch_shapes=([pltpu.VMEM(half, F32)] * (2 * n) + [pltpu.VMEM(half, BF16)] * n
                        + [pltpu.VMEM(small.shape, F32), pltpu.VMEM(small.shape, F32),
                           pltpu.SemaphoreType.DMA((n + 1,)), pltpu.SemaphoreType.DMA((n + 1,)),
                           pltpu.SemaphoreType.DMA((n,)), pltpu.SemaphoreType.DMA((3 * n + 3,)),
                           pltpu.SemaphoreType.DMA((3 * n + 3,)), pltpu.SemaphoreType.DMA((2 * n + 1,))]),
        operands=(*grads, small))


def _final_sum(pf, land, chip, core, name):
    _, rh, ncol = land.shape

    def body(me_ref, c_ref, pf_ref, land_ref, o_ref):
        me = me_ref[0]
        acc = jnp.zeros((rh, ncol), F32)
        for k in range(N_SHARD):
            acc = acc + jnp.where(me == k, pf_ref[...], land_ref[k].astype(F32))
        o_ref[...] = acc

    grid_spec = pltpu.PrefetchScalarGridSpec(
        num_scalar_prefetch=2, grid=(1,),
        in_specs=[pl.BlockSpec((rh, ncol), lambda i, me, c: (0, 0)),
                  pl.BlockSpec((N_SHARD, rh, ncol), lambda i, me, c: (0, 0, 0))],
        out_specs=pl.BlockSpec((rh, ncol), lambda i, me, c: (c[0], 0)))
    return pl.pallas_call(
        body, name=name, grid_spec=grid_spec, out_shape=jax.ShapeDtypeStruct((2 * rh, ncol), F32),
        compiler_params=_params(("arbitrary",)),
    )(chip, core, pf, land)


def _ss_payload(fulls):
    n = len(fulls)

    def phases(_, outs, sems):
        send, recv = sems
        x, y, c = _mesh_pos()

        def copy(w):
            rh = fulls[w].shape[0] // 2
            mine = outs[w].at[pl.ds(c * rh, rh), :]
            return _remote(mine, mine, send.at[w], recv.at[w], (x, y, 1 - c))

        def start():
            for w in range(n):
                copy(w).start()

        def finish():
            for w in range(n):
                copy(w).wait()

        return [lambda: None, start, finish]

    return _Payload(fulls, [jax.ShapeDtypeStruct(f.shape, F32) for f in fulls], {i: i for i in range(n)},
                    [pltpu.SemaphoreType.DMA((n,)), pltpu.SemaphoreType.DMA((n,))], phases)


def _sibling_share(fulls, name):
    return _call(lambda: None, _ss_payload(fulls), name=name, grid=None, in_specs=[], out_specs=[], out_shape=[])


_ROW = {"rel_bias": 128, "sgu_b_s": 136, "norm_ffn1": 144, "norm_mix": 145, "norm_ffn2": 146, "norm_final": 147,
        "b_gate": 148, "sgu_ln_g": 150, "sgu_ln_b": 151}


def _pack_small(gs, loss):
    def body(ws, rel, bs, n1, nm, n2, nf, bg, lg, lb, loss_ref, o_ref):
        o_ref[...] = jnp.zeros_like(o_ref)
        o_ref[LOSS_ROW:LOSS_ROW + 1, 0:128] = loss_ref[...]
        for g in range(SGU_GROUPS):
            o_ref[0:SGU_BLOCK, g * SGU_BLOCK:(g + 1) * SGU_BLOCK] = ws[g]
        o_ref[128:136, 0:REL_PAD] = rel[...]
        o_ref[136:144, 0:SGU_BLOCK] = bs[...]
        o_ref[144:145, :] = n1[...]
        o_ref[145:146, :] = nm[...]
        o_ref[146:147, :] = n2[...]
        o_ref[147:148, :] = nf[...]
        o_ref[148:149, :] = bg[:, 0:D_MODEL]
        o_ref[149:150, :] = bg[:, D_MODEL:2 * D_MODEL]
        o_ref[150:151, 0:D_SGU] = lg[...]
        o_ref[151:152, 0:D_SGU] = lb[...]

    order = ("sgu_w_s", "rel_bias", "sgu_b_s", "norm_ffn1", "norm_mix", "norm_ffn2", "norm_final", "b_gate", "sgu_ln_g",
             "sgu_ln_b")
    return pl.pallas_call(body, name="pack_small", out_shape=jax.ShapeDtypeStruct((SMALL_ROWS, D_MODEL), F32))(
        *[gs[k] for k in order], loss)


def _adam(w, g, m, v):
    m2 = ADAM_B1 * m + (1.0 - ADAM_B1) * g
    v2 = ADAM_B2 * v + (1.0 - ADAM_B2) * (g * g)
    m_hat = m2 / (1.0 - ADAM_B1 ** ADAM_STEP)
    v_hat = v2 / (1.0 - ADAM_B2 ** ADAM_STEP)
    delta = -ADAM_LR * (m_hat / (jnp.sqrt(v_hat) + ADAM_EPS) + ADAM_WD * w)
    return delta, m2, v2


def _adam_small(sin, w, m, v):
    names = SMALL
    k = len(names)

    def body(*refs):
        sin_ref = refs[0]
        w_r, m_r, v_r = refs[1:1 + k], refs[1 + k:1 + 2 * k], refs[1 + 2 * k:1 + 3 * k]
        outs = refs[1 + 3 * k:]
        tot = sin_ref[0] + sin_ref[1] + sin_ref[2] + sin_ref[3]
        outs[4 * k][...] = tot[LOSS_ROW:LOSS_ROW + 1, 0:128]
        for i, name in enumerate(names):
            o = outs[4 * i:4 * i + 4]
            if name == "sgu_w_s":
                for gi in range(SGU_GROUPS):
                    g = tot[0:SGU_BLOCK, gi * SGU_BLOCK:(gi + 1) * SGU_BLOCK]
                    res = (g,) + _adam(w_r[i][gi], g, m_r[i][gi], v_r[i][gi])
                    for ref, val in zip(o, res):
                        ref[gi] = val
                continue
            r0 = _ROW[name]
            if name == "rel_bias":
                g = tot[r0:r0 + HEADS, 0:REL_PAD]
            elif name == "sgu_b_s":
                g = tot[r0:r0 + SGU_GROUPS, 0:SGU_BLOCK]
            elif name == "b_gate":
                g = jnp.concatenate([tot[r0:r0 + 1, :], tot[r0 + 1:r0 + 2, :]], axis=1)
            elif name in ("sgu_ln_g", "sgu_ln_b"):
                g = tot[r0:r0 + 1, 0:D_SGU]
            else:
                g = tot[r0:r0 + 1, :]
            res = (g,) + _adam(w_r[i][...], g, m_r[i][...], v_r[i][...])
            for ref, val in zip(o, res):
                ref[...] = val

    out_shape = []
    for name in names:
        out_shape += [jax.ShapeDtypeStruct(w[name].shape, F32)] * 4
    out_shape.append(jax.ShapeDtypeStruct((1, 128), F32))
    flat = pl.pallas_call(body, name="adam_small", out_shape=out_shape, compiler_params=_params())(
        sin, *[w[n] for n in names], *[m[n] for n in names], *[v[n] for n in names])
    return {name: tuple(flat[4 * i:4 * i + 4]) for i, name in enumerate(names)}, flat[4 * k]


def _adam_big(w, g, m, v, name):
    r, ncol = w.shape
    tr = 256 if r % 256 == 0 else r // 2

    def body(w_ref, g_ref, m_ref, v_ref, g2_ref, d_ref, m2_ref, v2_ref):
        gv = g_ref[...]
        g2_ref[...] = gv
        d_ref[...], m2_ref[...], v2_ref[...] = _adam(w_ref[...], gv, m_ref[...], v_ref[...])

    spec = pl.BlockSpec((tr, ncol), lambda i: (i, 0))
    return pl.pallas_call(
        body, name=name, grid=(r // tr,), in_specs=[spec] * 4, out_specs=[spec] * 4,
        out_shape=[jax.ShapeDtypeStruct(w.shape, F32)] * 4, compiler_params=_params(("arbitrary",)),
    )(w, g, m, v)


WEIGHTS = ("norm_ffn1", "ffn1_w_gate", "ffn1_w_up", "ffn1_w_down", "norm_mix", "w_in", "b_gate", "rel_bias", "sgu_ln_g",
           "sgu_ln_b", "sgu_w_s", "sgu_b_s", "w_branch_att", "w_branch_sgu", "w_out", "norm_ffn2", "ffn2_w_gate",
           "ffn2_w_up", "ffn2_w_down", "norm_final")


GATE_UP = ("ffn1_w_gate", "ffn1_w_up", "ffn2_w_gate", "ffn2_w_up")
_FFN = ("ffn1_w_gate", "ffn1_w_up", "ffn1_w_down", "ffn2_w_gate", "ffn2_w_up", "ffn2_w_down")
_CAST_GROUPS = ((_FFN, "cast_ffn"), (("w_in",), "cast_w_in"), (("w_branch_att", "w_branch_sgu"), "cast_branch"),
                (("w_out",), "cast_w_out"))


def _big_form(name, a):
    return jnp.swapaxes(a, 1, 2)[0] if name in GATE_UP else a[0]


def _big_back(name, a):
    return jnp.swapaxes(a[None], 1, 2) if name in GATE_UP else a[None]


def _small_form(name, a):
    if name == "norm_final":
        return a.reshape(1, D_MODEL)
    if name == "rel_bias":
        return jnp.pad(a[0], ((0, 0), (0, REL_PAD - N_REL)))
    if name in ("sgu_w_s", "sgu_b_s"):
        return a[0]
    return a


def _small_back(name, a, like):
    if name == "rel_bias":
        a = a[:, :N_REL]
    return a.reshape(like.shape)


def kernel(x, norm_ffn1, ffn1_w_gate, ffn1_w_up, ffn1_w_down, norm_mix, w_in, b_gate, rel_bias, sgu_ln_g, sgu_ln_b, sgu_w_s, sgu_b_s, w_branch_att, w_branch_sgu, w_out, norm_ffn2, ffn2_w_gate, ffn2_w_up, ffn2_w_down, norm_final, loss_target, m_norm_ffn1, m_ffn1_w_gate, m_ffn1_w_up, m_ffn1_w_down, m_norm_mix, m_w_in, m_b_gate, m_rel_bias, m_sgu_ln_g, m_sgu_ln_b, m_sgu_w_s, m_sgu_b_s, m_w_branch_att, m_w_branch_sgu, m_w_out, m_norm_ffn2, m_ffn2_w_gate, m_ffn2_w_up, m_ffn2_w_down, m_norm_final, v_norm_ffn1, v_ffn1_w_gate, v_ffn1_w_up, v_ffn1_w_down, v_norm_mix, v_w_in, v_b_gate, v_rel_bias, v_sgu_ln_g, v_sgu_ln_b, v_sgu_w_s, v_sgu_b_s, v_w_branch_att, v_w_branch_sgu, v_w_out, v_norm_ffn2, v_ffn2_w_gate, v_ffn2_w_up, v_ffn2_w_down, v_norm_final):
    w = dict(norm_ffn1=norm_ffn1, ffn1_w_gate=ffn1_w_gate, ffn1_w_up=ffn1_w_up, ffn1_w_down=ffn1_w_down, norm_mix=norm_mix,
             w_in=w_in, b_gate=b_gate, rel_bias=rel_bias, sgu_ln_g=sgu_ln_g, sgu_ln_b=sgu_ln_b, sgu_w_s=sgu_w_s,
             sgu_b_s=sgu_b_s, w_branch_att=w_branch_att, w_branch_sgu=w_branch_sgu, w_out=w_out, norm_ffn2=norm_ffn2,
             ffn2_w_gate=ffn2_w_gate, ffn2_w_up=ffn2_w_up, ffn2_w_down=ffn2_w_down, norm_final=norm_final)
    m = dict(norm_ffn1=m_norm_ffn1, ffn1_w_gate=m_ffn1_w_gate, ffn1_w_up=m_ffn1_w_up, ffn1_w_down=m_ffn1_w_down,
             norm_mix=m_norm_mix, w_in=m_w_in, b_gate=m_b_gate, rel_bias=m_rel_bias, sgu_ln_g=m_sgu_ln_g,
             sgu_ln_b=m_sgu_ln_b, sgu_w_s=m_sgu_w_s, sgu_b_s=m_sgu_b_s, w_branch_att=m_w_branch_att,
             w_branch_sgu=m_w_branch_sgu, w_out=m_w_out, norm_ffn2=m_norm_ffn2, ffn2_w_gate=m_ffn2_w_gate,
             ffn2_w_up=m_ffn2_w_up, ffn2_w_down=m_ffn2_w_down, norm_final=m_norm_final)
    v = dict(norm_ffn1=v_norm_ffn1, ffn1_w_gate=v_ffn1_w_gate, ffn1_w_up=v_ffn1_w_up, ffn1_w_down=v_ffn1_w_down,
             norm_mix=v_norm_mix, w_in=v_w_in, b_gate=v_b_gate, rel_bias=v_rel_bias, sgu_ln_g=v_sgu_ln_g,
             sgu_ln_b=v_sgu_ln_b, sgu_w_s=v_sgu_w_s, sgu_b_s=v_sgu_b_s, w_branch_att=v_w_branch_att,
             w_branch_sgu=v_w_branch_sgu, w_out=v_w_out, norm_ffn2=v_norm_ffn2, ffn2_w_gate=v_ffn2_w_gate,
             ffn2_w_up=v_ffn2_w_up, ffn2_w_down=v_ffn2_w_down, norm_final=v_norm_final)

    core = lax.axis_index("c").astype(jnp.int32).reshape(1)
    chip = (2 * lax.axis_index("x") + lax.axis_index("y")).astype(jnp.int32).reshape(1)

    wk = {n: _big_form(n, w[n]) for n in BIG}
    slots = {}
    for names, call in _CAST_GROUPS:
        slots.update(zip(names, _cast_slots([wk[n] for n in names], chip, call)))
    ws = {n: _small_form(n, w[n]) for n in SMALL}
    _, gx, shard_grads, small_sums = _local_step(x[0], loss_target[0], slots, ws, (core, chip))

    small, loss = _adam_small(small_sums, ws, {n: _small_form(n, m[n]) for n in SMALL},
                              {n: _small_form(n, v[n]) for n in SMALL})
    grad, delta, new_m, new_v = {}, {}, {}, {}
    for n in SMALL:
        grad[n], delta[n], new_m[n], new_v[n] = (_small_back(n, a, w[n]) for a in small[n])
    for n in BIG:
        g2, d2, m2, v2 = _adam_big(wk[n], shard_grads[n], _big_form(n, m[n]), _big_form(n, v[n]), "adam_" + n)
        grad[n], delta[n], new_m[n], new_v[n] = (_big_back(n, a) for a in (g2, d2, m2, v2))

    return (loss[0, 0], gx.reshape(x.shape), *[grad[n] for n in WEIGHTS], *[delta[n] for n in WEIGHTS],
            *[new_m[n] for n in WEIGHTS], *[new_v[n] for n in WEIGHTS])
```

```python
import functools

import jax
import jax.numpy as jnp
from jax import lax
from jax.experimental import pallas as pl
from jax.experimental.pallas import tpu as pltpu

F32 = jnp.float32
BF16 = jnp.bfloat16

D_MODEL = 1024
N_SHARD = 4
D_FF = 2816
FF_S = D_FF // N_SHARD
D_ATT = 512
D_SGU = 512
D_IN = 3 * D_ATT + 2 * D_SGU + 2 * D_MODEL
IN_S = D_IN // N_SHARD
BR_S = D_MODEL // N_SHARD
HEADS = 8
HEAD_DIM = 64
CHUNK = 64
N_LEFT = 8
BAND = (N_LEFT + 1) * CHUNK
REL_CLIP = 256
N_REL = 2 * REL_CLIP + 1
REL_PAD = 640
SGU_BLOCK = 128
SGU_GROUPS = 8
SGU_GDIM = 64
EPS = 1e-6
NEG_INF = -1e30

ATT_ROWS = 2 * CHUNK
ATT_KEYS = BAND + CHUNK
ATT_PAD = N_LEFT * CHUNK

ADAM_LR = 0.001
ADAM_B1 = 0.9
ADAM_B2 = 0.999
ADAM_EPS = 1e-08
ADAM_WD = 0.01
ADAM_STEP = 10

TM = 256
TW = 1024
DGRAD_ROWS = 64
VMEM_LIMIT = 56 * 1024 * 1024

SMALL_ROWS = 160
LOSS_ROW = 152
MESH = pl.DeviceIdType.MESH

_NT = (((1,), (1,)), ((), ()))
_TN = (((0,), (0,)), ((), ()))


def _params(sem=None):
    return pltpu.CompilerParams(dimension_semantics=sem, vmem_limit_bytes=VMEM_LIMIT)


def _const_spec(shape):
    nd = len(shape)
    return pl.BlockSpec(shape, lambda *_: (0,) * nd, pipeline_mode=pl.Buffered(1))


def _acc_spec(shape):
    nd = len(shape)
    return pl.BlockSpec(shape, lambda *_: (0,) * nd)


def _row_spec(tm, ncols, off=0):
    return pl.BlockSpec((tm, ncols), lambda i: (i + off, 0))


def _row3_spec(tm, ncols):
    return pl.BlockSpec((N_SHARD, tm, ncols), lambda i: (0, i, 0))


def _dot(a, b):
    return jnp.dot(a, b, preferred_element_type=F32)


def _dot_nt(a, b):
    return lax.dot_general(a, b, _NT, preferred_element_type=F32)


def _dot_tn(a, b):
    return lax.dot_general(a, b, _TN, preferred_element_type=F32)


def _rms_fwd(x, g):
    r = lax.rsqrt(jnp.mean(x * x, axis=-1, keepdims=True) + EPS)
    xhat = x * r
    return xhat, r, xhat * g


def _rms_bwd(dh, xhat, r, g):
    dxhat = dh * g
    dx = r * (dxhat - xhat * jnp.mean(dxhat * xhat, axis=-1, keepdims=True))
    dg = jnp.sum(dh * xhat, axis=0, keepdims=True)
    return dx, dg


def _sigmoid(x):
    return 1.0 / (1.0 + jnp.exp(-x))


def _edges(n_steps):
    return [(0, True), (n_steps - 1, False)]


def _ffn_fwd(x, g, wg, wu, wd, name, payload=None, head=None):
    T = x.shape[0]

    def body(x_ref, g_ref, wg_ref, wu_ref, wd_ref, *rest):
        if head:
            t_ref, gf_ref, xo_ref, h_ref, a_ref, b_ref, loss_ref, dgf_ref = rest
        else:
            xo_ref, h_ref, a_ref, b_ref = rest
        xv = x_ref[...]
        hb = _rms_fwd(xv, g_ref[...])[2].astype(BF16)
        h_ref[...] = hb
        acc = jnp.zeros((TM, D_MODEL), F32)
        for s in range(N_SHARD):
            a = _dot_nt(hb, wg_ref[s])
            b = _dot_nt(hb, wu_ref[s])
            a_ref[s] = a.astype(BF16)
            b_ref[s] = b.astype(BF16)
            sv = a * _sigmoid(a) * b
            acc += _dot(sv.astype(BF16), wd_ref[s])
        xo = xv + 0.5 * acc
        if not head:
            xo_ref[...] = xo
            return

        @pl.when(pl.program_id(0) == 0)
        def _():
            loss_ref[...] = jnp.zeros_like(loss_ref)
            dgf_ref[...] = jnp.zeros_like(dgf_ref)

        gf = gf_ref[...]
        xhat, r, y = _rms_fwd(xo, gf)
        err = y - t_ref[...]
        loss_ref[...] += 0.5 * jnp.sum(jnp.mean(err * err, axis=-1, keepdims=True), axis=0, keepdims=True)
        dxn, dgf = _rms_bwd(err * (1.0 / D_MODEL), xhat, r, gf)
        xo_ref[...] = dxn
        dgf_ref[...] += dgf

    tok = jax.ShapeDtypeStruct((T, D_MODEL), F32)
    act = jax.ShapeDtypeStruct((N_SHARD, T, FF_S), BF16)
    return _call(
        body, payload, name=name, grid=(T // TM,), when=_edges(T // TM), sem=("arbitrary",),
        in_specs=[_row_spec(TM, D_MODEL), _const_spec((1, D_MODEL)), _const_spec(wg.shape), _const_spec(wu.shape),
                  _const_spec(wd.shape)] + ([_row_spec(TM, D_MODEL), _const_spec((1, D_MODEL))] if head else []),
        out_specs=[_row_spec(TM, D_MODEL), _row_spec(TM, D_MODEL), _row3_spec(TM, FF_S), _row3_spec(TM, FF_S)]
        + ([_acc_spec((1, 128)), _acc_spec((1, D_MODEL))] if head else []),
        out_shape=[tok, jax.ShapeDtypeStruct((T, D_MODEL), BF16), act, act]
        + ([jax.ShapeDtypeStruct((1, 128), F32), jax.ShapeDtypeStruct((1, D_MODEL), F32)] if head else []),
        operands=(x, g, wg, wu, wd) + (tuple(head) if head else ()))


def _ffn_dgrad(dout, x, a, b, g, wg, wu, wd, name, payload=None):
    T = x.shape[0]

    def body(do_ref, x_ref, a_ref, b_ref, g_ref, wg_ref, wu_ref, wd_ref, dx_ref, da_ref, db_ref, dg_ref):
        do = do_ref[...]
        dob = (0.5 * do).astype(BF16)
        dh = jnp.zeros((TM, D_MODEL), F32)
        ds_next = _dot_nt(dob, wd_ref[0])
        for s in range(N_SHARD):
            ds = ds_next
            if s + 1 < N_SHARD:
                ds_next = _dot_nt(dob, wd_ref[s + 1])
            for r0 in range(0, TM, DGRAD_ROWS):
                rows = slice(r0, r0 + DGRAD_ROWS)
                av = a_ref[s, rows, :].astype(F32)
                bv = b_ref[s, rows, :].astype(F32)
                sig = _sigmoid(av)
                dsr = ds[rows]
                da_ref[s, rows, :] = (dsr * bv * (sig * (1.0 + av * (1.0 - sig)))).astype(BF16)
                db_ref[s, rows, :] = (dsr * (av * sig)).astype(BF16)
            dh += _dot(da_ref[s], wg_ref[s]) + _dot(db_ref[s], wu_ref[s])
        gv = g_ref[...]
        xhat, r, _ = _rms_fwd(x_ref[...], gv)
        dxn, dg = _rms_bwd(dh, xhat, r, gv)
        dx_ref[...] = do + dxn

        @pl.when(pl.program_id(0) == 0)
        def _():
            dg_ref[...] = jnp.zeros_like(dg_ref)

        dg_ref[...] += dg

    return _call(
        body, payload, name=name, grid=(T // TM,), when=_edges(T // TM), sem=("arbitrary",),
        in_specs=[_row_spec(TM, D_MODEL), _row_spec(TM, D_MODEL), _row3_spec(TM, FF_S), _row3_spec(TM, FF_S),
                  _const_spec((1, D_MODEL)), _const_spec(wg.shape), _const_spec(wu.shape), _const_spec(wd.shape)],
        out_specs=[_row_spec(TM, D_MODEL), _row3_spec(TM, FF_S), _row3_spec(TM, FF_S), _acc_spec((1, D_MODEL))],
        out_shape=[jax.ShapeDtypeStruct((T, D_MODEL), F32), jax.ShapeDtypeStruct((N_SHARD, T, FF_S), BF16),
                   jax.ShapeDtypeStruct((N_SHARD, T, FF_S), BF16), jax.ShapeDtypeStruct((1, D_MODEL), F32)],
        operands=(dout, x, a, b, g, wg, wu, wd))


def _ffn_wgrad(h, dout, a, b, da, db, name, payload=None):
    T = h.shape[0]

    def body(h_ref, do_ref, a_ref, b_ref, da_ref, db_ref, gwg_ref, gwu_ref, gwd_ref):
        @pl.when(pl.program_id(1) == 0)
        def _():
            gwg_ref[...] = jnp.zeros_like(gwg_ref)
            gwu_ref[...] = jnp.zeros_like(gwu_ref)
            gwd_ref[...] = jnp.zeros_like(gwd_ref)

        hv = h_ref[...]
        gwg_ref[0] += _dot_tn(da_ref[0], hv)
        gwu_ref[0] += _dot_tn(db_ref[0], hv)
        dob = do_ref[...].astype(BF16)
        av = a_ref[0].astype(F32)
        sv = (0.5 * av * _sigmoid(av) * b_ref[0].astype(F32)).astype(BF16)
        gwd_ref[0] += _dot_tn(sv, dob)

    tw = min(TW, T)
    tok = pl.BlockSpec((tw, D_MODEL), lambda s, i: (i, 0))
    act = pl.BlockSpec((1, tw, FF_S), lambda s, i: (s, i, 0))
    return _call(
        body, payload, name=name, grid=(N_SHARD, T // tw), when=_edges(N_SHARD * (T // tw)),
        sem=("arbitrary", "arbitrary"),
        in_specs=[tok, tok, act, act, act, act],
        out_specs=[pl.BlockSpec((1, FF_S, D_MODEL), lambda s, i: (s, 0, 0))] * 3,
        out_shape=[jax.ShapeDtypeStruct((N_SHARD, FF_S, D_MODEL), F32)] * 3,
        operands=(h, dout, a, b, da, db))


def _in_fwd(x, g, w_in, lng, lnb, w_s, bst, payload=None):
    T = x.shape[0]

    def body(x_ref, g_ref, w_ref, lng_ref, lnb_ref, ws_ref, bst_ref, h_ref, qkv_ref, zs_ref, gl_ref, y_ref):
        hb = _rms_fwd(x_ref[...], g_ref[...])[2].astype(BF16)
        h_ref[...] = hb
        z0 = _dot(hb, w_ref[0])
        qkv_ref[:, 0:IN_S] = z0.astype(BF16)
        z1 = _dot(hb, w_ref[1])
        qkv_ref[:, IN_S:3 * D_ATT] = z1[:, 0:384].astype(BF16)
        z2 = _dot(hb, w_ref[2])
        zs = jnp.concatenate([z1[:, 384:IN_S], z2[:, 0:256]], axis=1)
        zs_ref[...] = zs
        gl_ref[:, 0:896] = z2[:, 256:IN_S]
        gl_ref[:, 896:2048] = _dot(hb, w_ref[3])
        _, u, _, _, vn = _sgu_norm(zs, lng_ref[...], lnb_ref[...])
        gm = _group_masks()
        bst_v = bst_ref[...]
        for n in range(TM // SGU_BLOCK):
            rows = slice(n * SGU_BLOCK, (n + 1) * SGU_BLOCK)
            y_ref[rows, :] = (u[rows] * _sgu_mix(vn[rows], ws_ref, bst_v, gm)).astype(BF16)

    return _call(
        body, payload, name="in_fwd", grid=(T // TM,), when=_edges(T // TM), sem=("arbitrary",),
        in_specs=[_row_spec(TM, D_MODEL), _const_spec((1, D_MODEL)), _const_spec(w_in.shape), _const_spec((1, D_SGU)),
                  _const_spec((1, D_SGU)), _const_spec(w_s.shape), _const_spec(bst.shape)],
        out_specs=[_row_spec(TM, D_MODEL), _row_spec(TM, 3 * D_ATT), _row_spec(TM, 2 * D_SGU), _row_spec(TM, 2 * D_MODEL),
                   _row_spec(TM, D_SGU)],
        out_shape=[jax.ShapeDtypeStruct((T, D_MODEL), BF16), jax.ShapeDtypeStruct((T, 3 * D_ATT), BF16),
                   jax.ShapeDtypeStruct((T, 2 * D_SGU), F32), jax.ShapeDtypeStruct((T, 2 * D_MODEL), F32),
                   jax.ShapeDtypeStruct((T, D_SGU), BF16)],
        operands=(x, g, w_in, lng, lnb, w_s, bst))


def _in_dgrad(dx_res, x, g, w_in, dq, dk, dv, dzs, dgl):
    T = x.shape[0]

    def body(dxr_ref, x_ref, g_ref, w_ref, dq_ref, dk_ref, dv_ref, dzs_ref, dgl_ref, dx_ref, dz_ref, dg_ref):
        dz = jnp.concatenate([dq_ref[...], dk_ref[...].astype(BF16), dv_ref[...].astype(BF16), dzs_ref[...], dgl_ref[...]],
                             axis=1)
        dz_ref[...] = dz
        dh = jnp.zeros((TM, D_MODEL), F32)
        for s in range(N_SHARD):
            dh += _dot_nt(dz[:, s * IN_S:(s + 1) * IN_S], w_ref[s])
        gv = g_ref[...]
        xhat, r, _ = _rms_fwd(x_ref[...], gv)
        dxn, dg = _rms_bwd(dh, xhat, r, gv)
        dx_ref[...] = dxr_ref[...] + dxn

        @pl.when(pl.program_id(0) == 0)
        def _():
            dg_ref[...] = jnp.zeros_like(dg_ref)

        dg_ref[...] += dg

    pad_blocks = ATT_PAD // TM
    return pl.pallas_call(
        body, name="in_dgrad", grid=(T // TM,),
        in_specs=[_row_spec(TM, D_MODEL), _row_spec(TM, D_MODEL), _const_spec((1, D_MODEL)), _const_spec(w_in.shape),
                  _row_spec(TM, D_ATT), _row_spec(TM, D_ATT, pad_blocks), _row_spec(TM, D_ATT, pad_blocks),
                  _row_spec(TM, 2 * D_SGU), _row_spec(TM, 2 * D_MODEL)],
        out_specs=[_row_spec(TM, D_MODEL), _row_spec(TM, D_IN), _acc_spec((1, D_MODEL))],
        out_shape=[jax.ShapeDtypeStruct((T, D_MODEL), F32), jax.ShapeDtypeStruct((T, D_IN), BF16),
                   jax.ShapeDtypeStruct((1, D_MODEL), F32)],
        compiler_params=_params(("arbitrary",)),
    )(dx_res, x, g, w_in, dq, dk, dv, dzs, dgl)


def _in_wgrad(h, dz):
    T = h.shape[0]

    def body(h_ref, dz_ref, gw_ref):
        @pl.when(pl.program_id(1) == 0)
        def _():
            gw_ref[...] = jnp.zeros_like(gw_ref)

        gw_ref[0] += _dot_tn(h_ref[...], dz_ref[...])

    return pl.pallas_call(
        body, name="in_wgrad", grid=(N_SHARD, T // min(TW, T)),
        in_specs=[pl.BlockSpec((min(TW, T), D_MODEL), lambda s, i: (i, 0)),
                  pl.BlockSpec((min(TW, T), IN_S), lambda s, i: (i, s))],
        out_specs=pl.BlockSpec((1, D_MODEL, IN_S), lambda s, i: (s, 0, 0)),
        out_shape=jax.ShapeDtypeStruct((N_SHARD, D_MODEL, IN_S), F32),
        compiler_params=_params(("arbitrary", "arbitrary")),
    )(h, dz)


def _rel_onehot():
    r = lax.broadcasted_iota(jnp.int32, (REL_PAD, REL_PAD), 0)
    n = lax.broadcasted_iota(jnp.int32, (REL_PAD, REL_PAD), 1)
    idx = jnp.clip(BAND - 1 - n, -REL_CLIP, REL_CLIP) + REL_CLIP
    return jnp.where(r == idx, 1.0, 0.0).astype(BF16)


def _split3(v):
    p1 = v.astype(BF16)
    r1 = v - p1.astype(F32)
    p2 = r1.astype(BF16)
    p3 = (r1 - p2.astype(F32)).astype(BF16)
    return p1, p2, p3


def _relbias_fwd(tab_pad):
    def body(t_ref, o_ref):
        oh = _rel_onehot()
        acc = jnp.zeros((HEADS, REL_PAD), F32)
        for p in _split3(t_ref[...]):
            acc += _dot(p, oh)
        o_ref[...] = acc

    return pl.pallas_call(body, name="relbias_fwd", out_shape=jax.ShapeDtypeStruct((HEADS, REL_PAD), F32))(tab_pad)


def _relbias_bwd(z):
    def body(z_ref, o_ref):
        oh = _rel_onehot()
        dt2 = jnp.sum(z_ref[...], axis=1)
        acc = jnp.zeros((HEADS, REL_PAD), F32)
        for p in _split3(dt2):
            acc += _dot_nt(p, oh)
        o_ref[...] = acc

    return pl.pallas_call(body, name="relbias_bwd", out_shape=jax.ShapeDtypeStruct((HEADS, REL_PAD), F32))(z)


def _bias_blocks(t2):
    flat = jnp.tile(t2, (1, CHUNK))
    skew = flat[:, :CHUNK * (REL_PAD - 1)].reshape(HEADS, CHUNK, REL_PAD - 1)
    bias = skew[:, :, CHUNK - 1:CHUNK - 1 + BAND]
    slabs = [jnp.pad(bias, ((0, 0), (0, 0), (CHUNK * c, ATT_KEYS - BAND - CHUNK * c)), constant_values=NEG_INF)
             for c in range(2)]
    return jnp.concatenate(slabs, axis=1)


def _unskew(db2):
    out = []
    for c in range(2):
        slab = db2[:, CHUNK * c:CHUNK * (c + 1), CHUNK * c:CHUNK * c + BAND]
        y = jnp.pad(slab, ((0, 0), (0, 0), (CHUNK - 1, REL_PAD - BAND - CHUNK + 1)))
        yf = jnp.pad(y.reshape(HEADS, CHUNK * REL_PAD), ((0, 0), (0, CHUNK)))
        out.append(yf.reshape(HEADS, CHUNK, REL_PAD + 1)[:, :, :REL_PAD])
    return jnp.concatenate(out, axis=1)


def _att_load(qkv_hbm, q_s, k_s, v_s, sem, T):
    copies = [pltpu.make_async_copy(qkv_hbm.at[:, 0:D_ATT], q_s, sem.at[0]),
              pltpu.make_async_copy(qkv_hbm.at[:, D_ATT:2 * D_ATT], k_s.at[pl.ds(ATT_PAD, T), :], sem.at[1]),
              pltpu.make_async_copy(qkv_hbm.at[:, 2 * D_ATT:3 * D_ATT], v_s.at[pl.ds(ATT_PAD, T), :], sem.at[2])]
    for cp in copies:
        cp.start()
    k_s[0:ATT_PAD, :] = jnp.zeros((ATT_PAD, D_ATT), BF16)
    v_s[0:ATT_PAD, :] = jnp.zeros((ATT_PAD, D_ATT), BF16)
    for cp in copies:
        cp.wait()


def _head(v, h):
    return v[:, h * HEAD_DIM:(h + 1) * HEAD_DIM]


def _rows(v, h):
    return v[h * ATT_ROWS:(h + 1) * ATT_ROWS]


def _att_exp(qs, kw, bias_ref, valid):
    s = jnp.concatenate([_dot_nt(_head(qs, h), _head(kw, h)) + bias_ref[h] for h in range(HEADS)], axis=0)
    if valid is not None:
        s = jnp.where(valid, s, NEG_INF)
    e = jnp.exp(s - jnp.max(s, axis=-1, keepdims=True))
    return e, 1.0 / jnp.sum(e, axis=-1, keepdims=True)


def _att_blocks(T, block, keys_on_rows=False, middle=None):
    n_edge = min(ATT_PAD // ATT_ROWS, T // ATT_ROWS)
    shape, axis = ((ATT_KEYS, 1), 0) if keys_on_rows else ((1, ATT_KEYS), 1)

    def edge(i, carry):
        r0 = i * ATT_ROWS
        block(i, (lax.broadcasted_iota(jnp.int32, shape, axis) + (r0 - ATT_PAD)) >= 0)
        return carry

    def inner(i, carry):
        block(i, None)
        return carry

    n_blocks = T // ATT_ROWS
    lax.fori_loop(0, n_edge, edge, 0)
    if middle is None:
        lax.fori_loop(n_edge, n_blocks, inner, 0)
        return
    n_late = max(n_blocks - n_blocks // 4, n_edge)
    lax.fori_loop(n_edge, n_late, inner, 0)
    middle()
    lax.fori_loop(n_late, n_blocks, inner, 0)


def _att_fwd(qkv, bias2, payload=None):
    T = qkv.shape[0]

    def body(qkv_hbm, bias_ref, y_ref, q_s, k_s, v_s, sem, middle=None):
        _att_load(qkv_hbm, q_s, k_s, v_s, sem, T)

        def block(i, valid):
            r0 = pl.multiple_of(i * ATT_ROWS, ATT_ROWS)
            qs = q_s[pl.ds(r0, ATT_ROWS), :] * (HEAD_DIM ** -0.5)
            kw = k_s[pl.ds(r0, ATT_KEYS), :]
            vw = v_s[pl.ds(r0, ATT_KEYS), :]
            e, rinv = _att_exp(qs, kw, bias_ref, valid)
            eb = e.astype(BF16)
            outs = [_dot(_rows(eb, h), _head(vw, h)) * _rows(rinv, h) for h in range(HEADS)]
            y_ref[pl.ds(r0, ATT_ROWS), :] = jnp.concatenate(outs, axis=1).astype(BF16)

        _att_blocks(T, block, middle=middle)

    return _call(
        body, payload, name="att_fwd", grid=None, takes_middle=True,
        in_specs=[pl.BlockSpec(memory_space=pl.ANY), pl.BlockSpec(memory_space=pltpu.VMEM)],
        out_specs=[pl.BlockSpec(memory_space=pltpu.VMEM)],
        out_shape=[jax.ShapeDtypeStruct((T, D_ATT), BF16)],
        scratch_shapes=[pltpu.VMEM((T, D_ATT), BF16), pltpu.VMEM((T + ATT_PAD, D_ATT), BF16),
                        pltpu.VMEM((T + ATT_PAD, D_ATT), BF16), pltpu.SemaphoreType.DMA((3,))],
        operands=(qkv, bias2))


def _lanes(v, h):
    return v[:, h * ATT_ROWS:(h + 1) * ATT_ROWS]


def _att_bwd(qkv, dy, bias2t, payload=None):
    T = qkv.shape[0]

    def body(qkv_hbm, dy_ref, bias_ref, dq_ref, dk_ref, dv_ref, db_ref, q_s, k_s, v_s, sem):
        _att_load(qkv_hbm, q_s, k_s, v_s, sem, T)
        dk_ref[...] = jnp.zeros_like(dk_ref)
        dv_ref[...] = jnp.zeros_like(dv_ref)
        db_ref[...] = jnp.zeros_like(db_ref)

        def block(i, valid):
            r0 = pl.multiple_of(i * ATT_ROWS, ATT_ROWS)
            qs = q_s[pl.ds(r0, ATT_ROWS), :] * (HEAD_DIM ** -0.5)
            kw = k_s[pl.ds(r0, ATT_KEYS), :]
            vw = v_s[pl.ds(r0, ATT_KEYS), :]
            dyb = dy_ref[pl.ds(r0, ATT_ROWS), :]
            s = jnp.concatenate([_dot_nt(_head(kw, h), _head(qs, h)) + bias_ref[h] for h in range(HEADS)], axis=1)
            if valid is not None:
                s = jnp.where(valid, s, NEG_INF)
            e = jnp.exp(s - jnp.max(s, axis=0, keepdims=True))
            p = e * (1.0 / jnp.sum(e, axis=0, keepdims=True))
            dp = jnp.concatenate([_dot_nt(_head(vw, h), _head(dyb, h)) for h in range(HEADS)], axis=1)
            ds = p * (dp - jnp.sum(p * dp, axis=0, keepdims=True))
            for h in range(HEADS):
                db_ref[h] += _lanes(ds, h)
            dsb = ds.astype(BF16)
            pb = p.astype(BF16)
            dq = [_dot_tn(_lanes(dsb, h), _head(kw, h)) for h in range(HEADS)]
            dk = [_dot(_lanes(dsb, h), _head(qs, h)) for h in range(HEADS)]
            dv = [_dot(_lanes(pb, h), _head(dyb, h)) for h in range(HEADS)]
            dq_ref[pl.ds(r0, ATT_ROWS), :] = (jnp.concatenate(dq, axis=1) * (HEAD_DIM ** -0.5)).astype(BF16)
            dk_ref[pl.ds(r0, ATT_KEYS), :] += jnp.concatenate(dk, axis=1)
            dv_ref[pl.ds(r0, ATT_KEYS), :] += jnp.concatenate(dv, axis=1)

        _att_blocks(T, block, keys_on_rows=True)

    vmem = pl.BlockSpec(memory_space=pltpu.VMEM)
    return _call(
        body, payload, name="att_bwd", grid=None,
        in_specs=[pl.BlockSpec(memory_space=pl.ANY), vmem, vmem],
        out_specs=[vmem, vmem, vmem, vmem],
        out_shape=[jax.ShapeDtypeStruct((T, D_ATT), BF16), jax.ShapeDtypeStruct((T + ATT_PAD, D_ATT), F32),
                   jax.ShapeDtypeStruct((T + ATT_PAD, D_ATT), F32), jax.ShapeDtypeStruct((HEADS, ATT_KEYS, ATT_ROWS), F32)],
        scratch_shapes=[pltpu.VMEM((T, D_ATT), BF16), pltpu.VMEM((T + ATT_PAD, D_ATT), BF16),
                        pltpu.VMEM((T + ATT_PAD, D_ATT), BF16), pltpu.SemaphoreType.DMA((3,))],
        operands=(qkv, dy, bias2t))


_GELU_C = 0.7978845608028654
_GELU_A = 0.044715


def _gelu(x):
    t = jnp.tanh(_GELU_C * (x + _GELU_A * x * x * x))
    return 0.5 * x * (1.0 + t), t


def _gelu_grad(x, t):
    return 0.5 * (1.0 + t) + 0.5 * x * (1.0 - t * t) * _GELU_C * (1.0 + 3.0 * _GELU_A * x * x)


def _group_masks():
    col = lax.broadcasted_iota(jnp.int32, (SGU_GROUPS, D_SGU), 1) // SGU_GDIM
    grp = lax.broadcasted_iota(jnp.int32, (SGU_GROUPS, D_SGU), 0)
    return jnp.where(col == grp, 1.0, 0.0).astype(F32)


def _causal_mask(transposed=False):
    i = lax.broadcasted_iota(jnp.int32, (SGU_BLOCK, SGU_BLOCK), 0) // CHUNK
    j = lax.broadcasted_iota(jnp.int32, (SGU_BLOCK, SGU_BLOCK), 1) // CHUNK
    return (j >= i) if transposed else (i >= j)


def _sgu_norm(zs, lng, lnb):
    gz, t = _gelu(zs)
    u = gz[:, 0:D_SGU]
    vs = gz[:, D_SGU:2 * D_SGU]
    xc = vs - jnp.mean(vs, axis=-1, keepdims=True)
    rstd = lax.rsqrt(jnp.mean(xc * xc, axis=-1, keepdims=True) + EPS)
    xhat = xc * rstd
    return t, u, xhat, rstd, xhat * lng + lnb


def _sgu_mix(vn_blk, w_ref, bst, gm):
    mask = _causal_mask()
    s = jnp.zeros((SGU_BLOCK, D_SGU), F32)
    for g in range(SGU_GROUPS):
        wm = jnp.where(mask, w_ref[g], 0.0).astype(BF16)
        s += _dot(wm, (vn_blk * gm[g:g + 1, :]).astype(BF16))
        s += bst[:, g:g + 1] * gm[g:g + 1, :]
    return s


def _sgu_bwd(zs, dy, lng, lnb, w_s, w_st, bst):
    T = zs.shape[0]
    nblk = TM // SGU_BLOCK

    def body(zs_ref, dy_ref, lng_ref, lnb_ref, w_ref, wt_ref, bst_ref, dzs_ref, dw_ref, dbt_ref, dlg_ref, dlb_ref):
        @pl.when(pl.program_id(0) == 0)
        def _():
            dw_ref[...] = jnp.zeros_like(dw_ref)
            dbt_ref[...] = jnp.zeros_like(dbt_ref)
            dlg_ref[...] = jnp.zeros_like(dlg_ref)
            dlb_ref[...] = jnp.zeros_like(dlb_ref)

        zs_v = zs_ref[...]
        lng_v = lng_ref[...]
        t, u, xhat, rstd, vn = _sgu_norm(zs_v, lng_v, lnb_ref[...])
        gm = _group_masks()
        bst_v = bst_ref[...]
        mask = _causal_mask()
        mask_t = _causal_mask(transposed=True)
        dyv = dy_ref[...].astype(F32)
        lane8 = lax.broadcasted_iota(jnp.int32, (1, SGU_GROUPS), 1)
        du_rows, dvn_rows = [], []
        for n in range(nblk):
            rows = slice(n * SGU_BLOCK, (n + 1) * SGU_BLOCK)
            vn_b = vn[rows]
            s = _sgu_mix(vn_b, w_ref, bst_v, gm)
            du_rows.append(dyv[rows] * s)
            dsb = dyv[rows] * u[rows]
            vnb16 = vn_b.astype(BF16)
            dvn = jnp.zeros((SGU_BLOCK, D_SGU), F32)
            dbt = jnp.zeros((SGU_BLOCK, SGU_GROUPS), F32)
            for g in range(SGU_GROUPS):
                dsg = dsb * gm[g:g + 1, :]
                dsg16 = dsg.astype(BF16)
                wmt = jnp.where(mask_t, wt_ref[g], 0.0).astype(BF16)
                dvn += _dot(wmt, dsg16)
                dw_ref[g] += jnp.where(mask, _dot_nt(dsg16, vnb16), 0.0)
                dbt += jnp.sum(dsg, axis=-1, keepdims=True) * jnp.where(lane8 == g, 1.0, 0.0)
            dbt_ref[...] += dbt
            dvn_rows.append(dvn)
        du = jnp.concatenate(du_rows, axis=0)
        dvn = jnp.concatenate(dvn_rows, axis=0)
        dlg_ref[...] += jnp.sum(dvn * xhat, axis=0, keepdims=True)
        dlb_ref[...] += jnp.sum(dvn, axis=0, keepdims=True)
        dxhat = dvn * lng_v
        dvs = rstd * (dxhat - jnp.mean(dxhat, axis=-1, keepdims=True)
                      - xhat * jnp.mean(dxhat * xhat, axis=-1, keepdims=True))
        dgz = jnp.concatenate([du, dvs], axis=1)
        dzs_ref[...] = (dgz * _gelu_grad(zs_v, t)).astype(BF16)

    return pl.pallas_call(
        body, name="sgu_bwd", grid=(T // TM,),
        in_specs=[_row_spec(TM, 2 * D_SGU), _row_spec(TM, D_SGU), _const_spec((1, D_SGU)), _const_spec((1, D_SGU)),
                  _const_spec(w_s.shape), _const_spec(w_st.shape), _const_spec(bst.shape)],
        out_specs=[_row_spec(TM, 2 * D_SGU), _acc_spec(w_s.shape), _acc_spec(bst.shape), _acc_spec((1, D_SGU)),
                   _acc_spec((1, D_SGU))],
        out_shape=[jax.ShapeDtypeStruct((T, 2 * D_SGU), BF16), jax.ShapeDtypeStruct(w_s.shape, F32),
                   jax.ShapeDtypeStruct(bst.shape, F32), jax.ShapeDtypeStruct((1, D_SGU), F32),
                   jax.ShapeDtypeStruct((1, D_SGU), F32)],
        compiler_params=_params(("arbitrary",)),
    )(zs, dy, lng, lnb, w_s, w_st, bst)


def _cols(v, s):
    return v[:, s * BR_S:(s + 1) * BR_S]


def _merge_fwd(x, y_att, y_sgu, gl, b_gate, wba, wbs, wo, payload=None):
    T = x.shape[0]

    def body(x_ref, ya_ref, ys_ref, gl_ref, bg_ref, wba_ref, wbs_ref, wo_ref, xo_ref, m_ref, pa_ref, ps_ref):
        ya = ya_ref[...]
        ys = ys_ref[...]
        pa = jnp.concatenate([_dot(ya, wba_ref[s]) for s in range(N_SHARD)], axis=1)
        ps = jnp.concatenate([_dot(ys, wbs_ref[s]) for s in range(N_SHARD)], axis=1)
        g = _sigmoid(gl_ref[...] + bg_ref[...])
        mb = (g[:, 0:D_MODEL] * pa + g[:, D_MODEL:2 * D_MODEL] * ps).astype(BF16)
        m_ref[...] = mb
        pa_ref[...] = pa.astype(BF16)
        ps_ref[...] = ps.astype(BF16)
        acc = jnp.zeros((TM, D_MODEL), F32)
        for s in range(N_SHARD):
            acc += _dot(_cols(mb, s), wo_ref[s])
        xo_ref[...] = x_ref[...] + acc

    tokd = jax.ShapeDtypeStruct((T, D_MODEL), BF16)
    return _call(
        body, payload, name="merge_fwd", grid=(T // TM,), when=_edges(T // TM), sem=("arbitrary",),
        in_specs=[_row_spec(TM, D_MODEL), _row_spec(TM, D_ATT), _row_spec(TM, D_SGU), _row_spec(TM, 2 * D_MODEL),
                  _const_spec((1, 2 * D_MODEL)), _const_spec(wba.shape), _const_spec(wbs.shape), _const_spec(wo.shape)],
        out_specs=[_row_spec(TM, D_MODEL)] * 4,
        out_shape=[jax.ShapeDtypeStruct((T, D_MODEL), F32), tokd, tokd, tokd],
        operands=(x, y_att, y_sgu, gl, b_gate, wba, wbs, wo))


def _merge_bwd(dx, y_att, y_sgu, gl, merged, pa, ps, b_gate, wba, wbs, wo, payload=None):
    T = dx.shape[0]

    def body(dx_ref, ya_ref, ys_ref, gl_ref, m_ref, pa_ref, ps_ref, bg_ref, wba_ref, wbs_ref, wo_ref,
             dya_ref, dys_ref, dgl_ref, dbg_ref, gwba_ref, gwbs_ref, gwo_ref):
        @pl.when(pl.program_id(0) == 0)
        def _():
            dbg_ref[...] = jnp.zeros_like(dbg_ref)
            gwba_ref[...] = jnp.zeros_like(gwba_ref)
            gwbs_ref[...] = jnp.zeros_like(gwbs_ref)
            gwo_ref[...] = jnp.zeros_like(gwo_ref)

        dxb = dx_ref[...].astype(BF16)
        dm = jnp.concatenate([_dot_nt(dxb, wo_ref[s]) for s in range(N_SHARD)], axis=1)
        g = _sigmoid(gl_ref[...] + bg_ref[...])
        ga = g[:, 0:D_MODEL]
        gs = g[:, D_MODEL:2 * D_MODEL]
        dpa = (dm * ga).astype(BF16)
        dps = (dm * gs).astype(BF16)
        dgl = jnp.concatenate([dm * pa_ref[...].astype(F32) * ga * (1.0 - ga),
                               dm * ps_ref[...].astype(F32) * gs * (1.0 - gs)], axis=1)
        dgl_ref[...] = dgl.astype(BF16)
        dbg_ref[...] += jnp.sum(dgl, axis=0, keepdims=True)
        ya = ya_ref[...]
        ys = ys_ref[...]
        mb = m_ref[...]
        dya = jnp.zeros((TM, D_ATT), F32)
        dys = jnp.zeros((TM, D_SGU), F32)
        for s in range(N_SHARD):
            dya += _dot_nt(_cols(dpa, s), wba_ref[s])
            dys += _dot_nt(_cols(dps, s), wbs_ref[s])
            gwo_ref[s] += _dot_tn(_cols(mb, s), dxb)
            gwba_ref[s] += _dot_tn(ya, _cols(dpa, s))
            gwbs_ref[s] += _dot_tn(ys, _cols(dps, s))
        dya_ref[...] = dya.astype(BF16)
        dys_ref[...] = dys.astype(BF16)

    return _call(
        body, payload, name="merge_bwd", grid=(T // TM,), when=_edges(T // TM), sem=("arbitrary",),
        operands=(dx, y_att, y_sgu, gl, merged, pa, ps, b_gate, wba, wbs, wo),
        in_specs=[_row_spec(TM, D_MODEL), _row_spec(TM, D_ATT), _row_spec(TM, D_SGU), _row_spec(TM, 2 * D_MODEL),
                  _row_spec(TM, D_MODEL), _row_spec(TM, D_MODEL), _row_spec(TM, D_MODEL),
                  _const_spec((1, 2 * D_MODEL)), _const_spec(wba.shape), _const_spec(wbs.shape), _const_spec(wo.shape)],
        out_specs=[_row_spec(TM, D_ATT), _row_spec(TM, D_SGU), _row_spec(TM, 2 * D_MODEL), _acc_spec((1, 2 * D_MODEL)),
                   _acc_spec(wba.shape), _acc_spec(wbs.shape), _acc_spec(wo.shape)],
        out_shape=[jax.ShapeDtypeStruct((T, D_ATT), BF16), jax.ShapeDtypeStruct((T, D_SGU), BF16),
                   jax.ShapeDtypeStruct((T, 2 * D_MODEL), BF16), jax.ShapeDtypeStruct((1, 2 * D_MODEL), F32),
                   jax.ShapeDtypeStruct(wba.shape, F32), jax.ShapeDtypeStruct(wbs.shape, F32),
                   jax.ShapeDtypeStruct(wo.shape, F32)])


BIG = ("ffn1_w_gate", "ffn1_w_up", "ffn1_w_down", "w_in", "w_branch_att", "w_branch_sgu", "w_out",
       "ffn2_w_gate", "ffn2_w_up", "ffn2_w_down")
SMALL = ("norm_ffn1", "norm_mix", "b_gate", "rel_bias", "sgu_ln_g", "sgu_ln_b", "sgu_w_s", "sgu_b_s", "norm_ffn2",
         "norm_final")


G_FFN1 = ("ffn1_w_gate", "ffn1_w_up", "ffn1_w_down")
G_MIX = ("w_in", "w_branch_att", "w_branch_sgu", "w_out")
G_FFN2 = ("ffn2_w_gate", "ffn2_w_up", "ffn2_w_down")


def _local_step(x, target, wb, ws, dist=None):
    def gather_on(names):
        return _ag_payload([wb[n] for n in names]) if dist else None

    t2 = _relbias_fwd(ws["rel_bias"])
    bias2 = _bias_blocks(t2)
    bst = ws["sgu_b_s"].T
    w_st = jnp.swapaxes(ws["sgu_w_s"], 1, 2)

    if dist:
        wb.update(zip(G_FFN1, _call(lambda: None, gather_on(G_FFN1), name="allgather_ffn1", grid=None, in_specs=[],
                                    out_specs=[], out_shape=[])))
    x1, h1, a1, b1, *got = _ffn_fwd(x, ws["norm_ffn1"], wb["ffn1_w_gate"], wb["ffn1_w_up"], wb["ffn1_w_down"],
                                    "ffn1_fwd", gather_on(G_MIX))
    wb.update(zip(G_MIX, got))
    h2, qkv, zs, gl, y_sgu, *got = _in_fwd(x1, ws["norm_mix"], wb["w_in"], ws["sgu_ln_g"], ws["sgu_ln_b"],
                                           ws["sgu_w_s"], bst, gather_on(G_FFN2[0:1]))
    wb.update(zip(G_FFN2[0:1], got))
    y_att, *got = _att_fwd(qkv, bias2, gather_on(G_FFN2[1:2]))
    wb.update(zip(G_FFN2[1:2], got))
    x2, merged, pa, ps, *got = _merge_fwd(x1, y_att, y_sgu, gl, ws["b_gate"], wb["w_branch_att"], wb["w_branch_sgu"],
                                          wb["w_out"], gather_on(G_FFN2[2:3]))
    wb.update(zip(G_FFN2[2:3], got))
    dx3, h3, a3, b3, loss, g_final = _ffn_fwd(x2, ws["norm_ffn2"], wb["ffn2_w_gate"], wb["ffn2_w_up"],
                                              wb["ffn2_w_down"], "ffn2_fwd", head=(target, ws["norm_final"]))

    gb, gs, sums = {}, {"norm_final": g_final}, {}

    def pair_on(names, small=None):
        return _px_payload([gb[n] for n in names], small) if dist else None

    def pair_add(names, halves):
        for n, rv in zip(names, halves):
            sums[n] = _pair_add(gb[n], rv, dist[0], dist[1], "pair_add_" + n)

    def chips_on(names):
        return _cx_payload([sums[n][1] for n in names], [sums[n][2] for n in names]) if dist else None

    dx2, da3, db3, gs["norm_ffn2"] = _ffn_dgrad(dx3, x2, a3, b3, ws["norm_ffn2"], wb["ffn2_w_gate"], wb["ffn2_w_up"],
                                                wb["ffn2_w_down"], "ffn2_dgrad")
    gb["ffn2_w_gate"], gb["ffn2_w_up"], gb["ffn2_w_down"] = _ffn_wgrad(h3, dx3, a3, b3, da3, db3, "ffn2_wgrad")
    dy_att, dy_sgu, dgl, gs["b_gate"], gb["w_branch_att"], gb["w_branch_sgu"], gb["w_out"], *got = _merge_bwd(
        dx2, y_att, y_sgu, gl, merged, pa, ps, ws["b_gate"], wb["w_branch_att"], wb["w_branch_sgu"], wb["w_out"],
        pair_on(G_FFN2))
    pair_add(G_FFN2, got)
    dq, dk, dv, db2t, *lands2 = _att_bwd(qkv, dy_att, jnp.swapaxes(bias2, 1, 2), chips_on(G_FFN2))
    gs["rel_bias"] = _relbias_bwd(_unskew(jnp.swapaxes(db2t, 1, 2)))
    dzs, gs["sgu_w_s"], dbt, gs["sgu_ln_g"], gs["sgu_ln_b"] = _sgu_bwd(zs, dy_sgu, ws["sgu_ln_g"], ws["sgu_ln_b"],
                                                                      ws["sgu_w_s"], w_st, bst)
    gs["sgu_b_s"] = dbt.T
    dx1, dz, gs["norm_mix"] = _in_dgrad(dx2, x1, ws["norm_mix"], wb["w_in"], dq, dk, dv, dzs, dgl)
    gb["w_in"] = _in_wgrad(h2, dz)
    gx, da1, db1, gs["norm_ffn1"], *got = _ffn_dgrad(dx1, x, a1, b1, ws["norm_ffn1"], wb["ffn1_w_gate"],
                                                    wb["ffn1_w_up"], wb["ffn1_w_down"], "ffn1_dgrad", pair_on(G_MIX))
    pair_add(G_MIX, got)
    gb["ffn1_w_gate"], gb["ffn1_w_up"], gb["ffn1_w_down"], *lands_mix = _ffn_wgrad(h1, dx1, a1, b1, da1, db1,
                                                                                   "ffn1_wgrad", chips_on(G_MIX))
    if not dist:
        return loss, gx, gb, gs

    def final_sums(names, lands):
        return [_final_sum(sums[n][0], land, dist[1], dist[0], "final_sum_" + n) for n, land in zip(names, lands)]

    early = G_FFN2 + G_MIX
    k = len(G_FFN1)
    tail = _tail_reduce([gb[n] for n in G_FFN1], _pack_small(gs, loss),
                        _ss_payload(final_sums(early, list(lands2) + list(lands_mix))))
    for i, n in enumerate(G_FFN1):
        sums[n] = (tail[i],)
    small_sums, shared = tail[2 * k], dict(zip(early, tail[2 * k + 1:]))
    shared.update(zip(G_FFN1, _sibling_share(final_sums(G_FFN1, tail[k:2 * k]), "sibling_share")))
    return loss, gx, shared, small_sums


_ANY = pl.BlockSpec(memory_space=pl.ANY)
_VMEM = pl.BlockSpec(memory_space=pltpu.VMEM)


def _mesh_pos():
    return lax.axis_index("x"), lax.axis_index("y"), lax.axis_index("c")


def _cast_slots(shards, chip, name):
    n = len(shards)
    r, ncol = shards[0].shape
    tr = r // 2

    def body(me_ref, *refs):
        for i_ref, o_ref in zip(refs[:n], refs[n:]):
            o_ref[0] = i_ref[...].astype(BF16)

    grid_spec = pltpu.PrefetchScalarGridSpec(
        num_scalar_prefetch=1, grid=(r // tr,),
        in_specs=[pl.BlockSpec((tr, ncol), lambda i, me: (i, 0))] * n,
        out_specs=[pl.BlockSpec((1, tr, ncol), lambda i, me: (me[0], i, 0))] * n)
    return pl.pallas_call(
        body, name=name, grid_spec=grid_spec,
        out_shape=[jax.ShapeDtypeStruct((N_SHARD, r, ncol), BF16)] * n,
        compiler_params=_params(("arbitrary",)),
    )(chip, *shards)


class _Payload:
    def __init__(self, arrays, out_shapes, aliases, scratch, phases):
        self.arrays = list(arrays)
        self.out_shapes = list(out_shapes)
        self.aliases = dict(aliases)
        self.scratch = list(scratch)
        self.phases = phases


def _remote(src, dst, ssem, rsem, dev):
    return pltpu.make_async_remote_copy(src_ref=src, dst_ref=dst, send_sem=ssem, recv_sem=rsem, device_id=dev,
                                        device_id_type=MESH)


def _call(body, payload, *, name, grid, in_specs, out_specs, out_shape, scratch_shapes=(), sem=None, when=None,
          operands=(), takes_middle=False):
    in_specs, out_specs, out_shape = list(in_specs), list(out_specs), list(out_shape)
    scratch_shapes = list(scratch_shapes)
    n_in, n_out, n_scr = len(in_specs), len(out_specs), len(scratch_shapes)
    kwargs = {}
    kernel = body
    if payload is not None:
        k_in, k_out = len(payload.arrays), len(payload.out_shapes)
        rank = len(grid) if grid else 0

        def kernel(*refs):
            a, b = n_in, n_in + k_in
            c, d = b + n_out, b + n_out + k_out
            e = d + n_scr
            phases = payload.phases(refs[a:b], refs[c:d], refs[e:])

            def run():
                body(*refs[:a], *refs[b:c], *refs[d:e])

            if not grid:
                phases[0]()
                if len(phases) == 3 and takes_middle:
                    body(*refs[:a], *refs[b:c], *refs[d:e], middle=phases[1])
                    phases[2]()
                    return
                run()
                for ph in phases[1:]:
                    ph()
                return
            step = pl.program_id(0)
            if rank == 2:
                step = step * grid[1] + pl.program_id(1)
            marks = list(when)
            if len(phases) == 3:
                marks = [when[0], (max(when[1][0] - 3, 0), False), when[1]]
            for ph, (at, before) in zip(phases, marks):
                if before:
                    pl.when(step == at)(ph)
            run()
            for ph, (at, before) in zip(phases, marks):
                if not before:
                    pl.when(step == at)(ph)

        in_specs += [_ANY] * k_in
        out_specs += [_ANY] * k_out
        out_shape += payload.out_shapes
        scratch_shapes += payload.scratch
        kwargs["input_output_aliases"] = {n_in + i: n_out + j for i, j in payload.aliases.items()}
        operands = tuple(operands) + tuple(payload.arrays)
    if grid:
        kwargs["grid"] = grid
    return pl.pallas_call(kernel, name=name, in_specs=in_specs, out_specs=out_specs, out_shape=out_shape,
                          scratch_shapes=scratch_shapes, compiler_params=_params(sem), **kwargs)(*operands)


def _ag_payload(slots):
    n = len(slots)

    def phases(_, refs, sems):
        send_i, recv_i, send_d, recv_d = sems
        x, y, c = _mesh_pos()
        me = 2 * x + y

        def half(w, core):
            rh = slots[w].shape[1] // 2
            return pl.ds(core * rh, rh)

        def ici(w, j):
            t = (me + 1 + j) % N_SHARD
            mine = refs[w].at[me, half(w, c), :]
            return _remote(mine, mine, send_i.at[3 * w + j], recv_i.at[3 * w + j], (t // 2, t % 2, c))

        def d2d(w, j, core):
            s = (me + 3 - j) % N_SHARD
            land = refs[w].at[s, half(w, core), :]
            return _remote(land, land, send_d.at[3 * w + j], recv_d.at[3 * w + j], (x, y, 1 - c))

        def start():
            for w in range(n):
                for j in range(3):
                    ici(w, j).start()

        def forward():
            for w in range(n):
                for j in range(3):
                    s = (me + 3 - j) % N_SHARD
                    land = refs[w].at[s, half(w, c), :]
                    _remote(land, land, send_i.at[3 * w + j], recv_i.at[3 * w + j], (x, y, c)).wait_recv()
                    d2d(w, j, c).start()

        def finish():
            for w in range(n):
                for j in range(3):
                    d2d(w, j, 1 - c).wait_recv()
            for w in range(n):
                for j in range(3):
                    ici(w, j).wait_send()
                    d2d(w, j, c).wait_send()

        return [start, forward, finish]

    return _Payload(slots, [jax.ShapeDtypeStruct(s.shape, s.dtype) for s in slots], {i: i for i in range(n)},
                    [pltpu.SemaphoreType.DMA((3 * n,)) for _ in range(4)], phases)


def _px_payload(grads, small=None):
    arrays = list(grads) + ([small] if small is not None else [])
    n = len(arrays)

    def phases(ins, outs, sems):
        send, recv = sems
        x, y, c = _mesh_pos()

        def copy(w):
            if w < len(grads):
                rh = grads[w].shape[1] // 2
                src = ins[w].at[:, pl.ds((1 - c) * rh, rh), :]
            else:
                src = ins[w]
            return _remote(src, outs[w], send.at[w], recv.at[w], (x, y, 1 - c))

        def start():
            for w in range(n):
                copy(w).start()

        def finish():
            for w in range(n):
                copy(w).wait()

        return [start, finish]

    out_shapes = [jax.ShapeDtypeStruct((N_SHARD, g.shape[1] // 2, g.shape[2]), F32) for g in grads]
    if small is not None:
        out_shapes.append(jax.ShapeDtypeStruct(small.shape, F32))
    return _Payload(arrays, out_shapes, {}, [pltpu.SemaphoreType.DMA((n,)), pltpu.SemaphoreType.DMA((n,))], phases)


def _cx_payload(pbs, lands):
    n = len(pbs)

    def phases(ins, outs, sems):
        send, recv = sems
        x, y, c = _mesh_pos()
        me = 2 * x + y

        def copy(w, j):
            t = (me + 1 + j) % N_SHARD
            return _remote(ins[w].at[t], outs[w].at[me], send.at[3 * w + j], recv.at[3 * w + j], (t // 2, t % 2, c))

        def start():
            for w in range(n):
                for j in range(3):
                    copy(w, j).start()

        def finish():
            for w in range(n):
                for j in range(3):
                    copy(w, j).wait()

        return [start, finish]

    return _Payload(list(pbs) + list(lands), [jax.ShapeDtypeStruct(p.shape, BF16) for p in lands],
                    {n + i: i for i in range(n)},
                    [pltpu.SemaphoreType.DMA((3 * n,)), pltpu.SemaphoreType.DMA((3 * n,))], phases)


def _pair_add(g, rv, core, chip, name):
    _, r, ncol = g.shape
    rh = r // 2

    def body(c_ref, me_ref, g_ref, rv_ref, pf_ref, pb_ref, land_ref):
        s = g_ref[0] + rv_ref[0]
        sb = s.astype(BF16)
        pb_ref[0] = sb

        @pl.when(pl.program_id(0) == me_ref[0])
        def _():
            pf_ref[...] = s
            land_ref[0] = sb

    slot = pl.BlockSpec((1, rh, ncol), lambda s, c, me: (s, 0, 0))
    grid_spec = pltpu.PrefetchScalarGridSpec(
        num_scalar_prefetch=2, grid=(N_SHARD,),
        in_specs=[pl.BlockSpec((1, rh, ncol), lambda s, c, me: (s, c[0], 0)), slot],
        out_specs=[pl.BlockSpec((rh, ncol), lambda s, c, me: (0, 0)), slot,
                   pl.BlockSpec((1, rh, ncol), lambda s, c, me: (me[0], 0, 0))])
    return pl.pallas_call(
        body, name=name, grid_spec=grid_spec,
        out_shape=[jax.ShapeDtypeStruct((rh, ncol), F32), jax.ShapeDtypeStruct((N_SHARD, rh, ncol), BF16),
                   jax.ShapeDtypeStruct((N_SHARD, rh, ncol), BF16)],
        compiler_params=_params(("arbitrary",)),
    )(core, chip, g, rv)


def _tail_reduce(grads, small, payload=None):
    n = len(grads)
    _, r, ncol = grads[0].shape
    rh = r // 2

    def body(*refs, middle=None):
        g_hbm, sm = refs[:n], refs[n]
        pf, land, sm_out = refs[n + 1:2 * n + 1], refs[2 * n + 1:3 * n + 1], refs[3 * n + 1]
        scr = refs[3 * n + 2:]
        rv, mine, sendb = scr[:n], scr[n:2 * n], scr[2 * n:3 * n]
        sm_rv, sm_sum, d_send, d_recv, load, i_send, i_recv, store = scr[3 * n:]
        x, y, c = _mesh_pos()
        me = 2 * x + y
        sib = (x, y, 1 - c)

        def pair(w):
            src = g_hbm[w].at[:, pl.ds((1 - c) * rh, rh), :] if w < n else sm
            return _remote(src, rv[w] if w < n else sm_rv, d_send.at[w], d_recv.at[w], sib)

        def chips(w, j):
            t = (me + 1 + j) % N_SHARD
            src = sendb[w].at[t] if w < n else sm_sum
            dst = land[w].at[me] if w < n else sm_out.at[me]
            return _remote(src, dst, i_send.at[3 * w + j], i_recv.at[3 * w + j], (t // 2, t % 2, c))

        loads = [pltpu.make_async_copy(g_hbm[w].at[:, pl.ds(c * rh, rh), :], mine[w], load.at[w]) for w in range(n)]
        for w in range(n + 1):
            pair(w).start()
        for cp in loads:
            cp.start()
        stores = []
        for w in range(n):
            loads[w].wait()
            pair(w).wait_recv()
            for k in range(N_SHARD):
                s = mine[w][k] + rv[w][k]
                mine[w][k] = s
                sendb[w][k] = s.astype(BF16)
            stores += [pltpu.make_async_copy(mine[w].at[me], pf[w], store.at[2 * w]),
                       pltpu.make_async_copy(sendb[w].at[me], land[w].at[me], store.at[2 * w + 1])]
            for cp in stores[-2:]:
                cp.start()
            for j in range(3):
                chips(w, j).start()
        pair(n).wait_recv()
        if middle is not None:
            middle()
        sm_sum[...] = sm[...] + sm_rv[...]
        stores.append(pltpu.make_async_copy(sm_sum, sm_out.at[me], store.at[2 * n]))
        stores[-1].start()
        for j in range(3):
            chips(n, j).start()
        for w in range(n + 1):
            pair(w).wait_send()
            for j in range(3):
                chips(w, j).wait()
        for cp in stores:
            cp.wait()

    half = (N_SHARD, rh, ncol)
    return _call(
        body, payload, name="tail_reduce", grid=None, takes_middle=True,
        in_specs=[_ANY] * n + [_VMEM], out_specs=[_ANY] * (2 * n + 1),
        out_shape=([jax.ShapeDtypeStruct((rh, ncol), F32)] * n + [jax.ShapeDtypeStruct(half, BF16)] * n
                   + [jax.ShapeDtypeStruct((N_SHARD,) + small.shape, F32)]),
        scratch_shapes=([pltpu.VMEM(half, F32)] * (2 * n) + [pltpu.VMEM(half, BF16)] * n
                        + [pltpu.VMEM(small.shape, F32), pltpu.VMEM(small.shape, F32),
                           pltpu.SemaphoreType.DMA((n + 1,)), pltpu.SemaphoreType.DMA((n + 1,)),
                           pltpu.SemaphoreType.DMA((n,)), pltpu.SemaphoreType.DMA((3 * n + 3,)),
                           pltpu.SemaphoreType.DMA((3 * n + 3,)), pltpu.SemaphoreType.DMA((2 * n + 1,))]),
        operands=(*grads, small))


def _final_sum(pf, land, chip, core, name):
    _, rh, ncol = land.shape

    def body(me_ref, c_ref, pf_ref, land_ref, o_ref):
        me = me_ref[0]
        acc = jnp.zeros((rh, ncol), F32)
        for k in range(N_SHARD):
            acc = acc + jnp.where(me == k, pf_ref[...], land_ref[k].astype(F32))
        o_ref[...] = acc

    grid_spec = pltpu.PrefetchScalarGridSpec(
        num_scalar_prefetch=2, grid=(1,),
        in_specs=[pl.BlockSpec((rh, ncol), lambda i, me, c: (0, 0)),
                  pl.BlockSpec((N_SHARD, rh, ncol), lambda i, me, c: (0, 0, 0))],
        out_specs=pl.BlockSpec((rh, ncol), lambda i, me, c: (c[0], 0)))
    return pl.pallas_call(
        body, name=name, grid_spec=grid_spec, out_shape=jax.ShapeDtypeStruct((2 * rh, ncol), F32),
        compiler_params=_params(("arbitrary",)),
    )(chip, core, pf, land)


def _ss_payload(fulls):
    n = len(fulls)

    def phases(_, outs, sems):
        send, recv = sems
        x, y, c = _mesh_pos()

        def copy(w):
            rh = fulls[w].shape[0] // 2
            mine = outs[w].at[pl.ds(c * rh, rh), :]
            return _remote(mine, mine, send.at[w], recv.at[w], (x, y, 1 - c))

        def start():
            for w in range(n):
                copy(w).start()

        def finish():
            for w in range(n):
                copy(w).wait()

        return [lambda: None, start, finish]

    return _Payload(fulls, [jax.ShapeDtypeStruct(f.shape, F32) for f in fulls], {i: i for i in range(n)},
                    [pltpu.SemaphoreType.DMA((n,)), pltpu.SemaphoreType.DMA((n,))], phases)


def _sibling_share(fulls, name):
    return _call(lambda: None, _ss_payload(fulls), name=name, grid=None, in_specs=[], out_specs=[], out_shape=[])


_ROW = {"rel_bias": 128, "sgu_b_s": 136, "norm_ffn1": 144, "norm_mix": 145, "norm_ffn2": 146, "norm_final": 147,
        "b_gate": 148, "sgu_ln_g": 150, "sgu_ln_b": 151}


def _pack_small(gs, loss):
    def body(ws, rel, bs, n1, nm, n2, nf, bg, lg, lb, loss_ref, o_ref):
        o_ref[...] = jnp.zeros_like(o_ref)
        o_ref[LOSS_ROW:LOSS_ROW + 1, 0:128] = loss_ref[...]
        for g in range(SGU_GROUPS):
            o_ref[0:SGU_BLOCK, g * SGU_BLOCK:(g + 1) * SGU_BLOCK] = ws[g]
        o_ref[128:136, 0:REL_PAD] = rel[...]
        o_ref[136:144, 0:SGU_BLOCK] = bs[...]
        o_ref[144:145, :] = n1[...]
        o_ref[145:146, :] = nm[...]
        o_ref[146:147, :] = n2[...]
        o_ref[147:148, :] = nf[...]
        o_ref[148:149, :] = bg[:, 0:D_MODEL]
        o_ref[149:150, :] = bg[:, D_MODEL:2 * D_MODEL]
        o_ref[150:151, 0:D_SGU] = lg[...]
        o_ref[151:152, 0:D_SGU] = lb[...]

    order = ("sgu_w_s", "rel_bias", "sgu_b_s", "norm_ffn1", "norm_mix", "norm_ffn2", "norm_final", "b_gate", "sgu_ln_g",
             "sgu_ln_b")
    return pl.pallas_call(body, name="pack_small", out_shape=jax.ShapeDtypeStruct((SMALL_ROWS, D_MODEL), F32))(
        *[gs[k] for k in order], loss)


def _adam(w, g, m, v):
    m2 = ADAM_B1 * m + (1.0 - ADAM_B1) * g
    v2 = ADAM_B2 * v + (1.0 - ADAM_B2) * (g * g)
    m_hat = m2 / (1.0 - ADAM_B1 ** ADAM_STEP)
    v_hat = v2 / (1.0 - ADAM_B2 ** ADAM_STEP)
    delta = -ADAM_LR * (m_hat / (jnp.sqrt(v_hat) + ADAM_EPS) + ADAM_WD * w)
    return delta, m2, v2


def _adam_small(sin, w, m, v):
    names = SMALL
    k = len(names)

    def body(*refs):
        sin_ref = refs[0]
        w_r, m_r, v_r = refs[1:1 + k], refs[1 + k:1 + 2 * k], refs[1 + 2 * k:1 + 3 * k]
        outs = refs[1 + 3 * k:]
        tot = sin_ref[0] + sin_ref[1] + sin_ref[2] + sin_ref[3]
        outs[4 * k][...] = tot[LOSS_ROW:LOSS_ROW + 1, 0:128]
        for i, name in enumerate(names):
            o = outs[4 * i:4 * i + 4]
            if name == "sgu_w_s":
                for gi in range(SGU_GROUPS):
                    g = tot[0:SGU_BLOCK, gi * SGU_BLOCK:(gi + 1) * SGU_BLOCK]
                    res = (g,) + _adam(w_r[i][gi], g, m_r[i][gi], v_r[i][gi])
                    for ref, val in zip(o, res):
                        ref[gi] = val
                continue
            r0 = _ROW[name]
            if name == "rel_bias":
                g = tot[r0:r0 + HEADS, 0:REL_PAD]
            elif name == "sgu_b_s":
                g = tot[r0:r0 + SGU_GROUPS, 0:SGU_BLOCK]
            elif name == "b_gate":
                g = jnp.concatenate([tot[r0:r0 + 1, :], tot[r0 + 1:r0 + 2, :]], axis=1)
            elif name in ("sgu_ln_g", "sgu_ln_b"):
                g = tot[r0:r0 + 1, 0:D_SGU]
            else:
                g = tot[r0:r0 + 1, :]
            res = (g,) + _adam(w_r[i][...], g, m_r[i][...], v_r[i][...])
            for ref, val in zip(o, res):
                ref[...] = val

    out_shape = []
    for name in names:
        out_shape += [jax.ShapeDtypeStruct(w[name].shape, F32)] * 4
    out_shape.append(jax.ShapeDtypeStruct((1, 128), F32))
    flat = pl.pallas_call(body, name="adam_small", out_shape=out_shape, compiler_params=_params())(
        sin, *[w[n] for n in names], *[m[n] for n in names], *[v[n] for n in names])
    return {name: tuple(flat[4 * i:4 * i + 4]) for i, name in enumerate(names)}, flat[4 * k]


def _adam_big(w, g, m, v, name):
    r, ncol = w.shape
    tr = 256 if r % 256 == 0 else r // 2

    def body(w_ref, g_ref, m_ref, v_ref, g2_ref, d_ref, m2_ref, v2_ref):
        gv = g_ref[...]
        g2_ref[...] = gv
        d_ref[...], m2_ref[...], v2_ref[...] = _adam(w_ref[...], gv, m_ref[...], v_ref[...])

    spec = pl.BlockSpec((tr, ncol), lambda i: (i, 0))
    return pl.pallas_call(
        body, name=name, grid=(r // tr,), in_specs=[spec] * 4, out_specs=[spec] * 4,
        out_shape=[jax.ShapeDtypeStruct(w.shape, F32)] * 4, compiler_params=_params(("arbitrary",)),
    )(w, g, m, v)


WEIGHTS = ("norm_ffn1", "ffn1_w_gate", "ffn1_w_up", "ffn1_w_down", "norm_mix", "w_in", "b_gate", "rel_bias", "sgu_ln_g",
           "sgu_ln_b", "sgu_w_s", "sgu_b_s", "w_branch_att", "w_branch_sgu", "w_out", "norm_ffn2", "ffn2_w_gate",
           "ffn2_w_up", "ffn2_w_down", "norm_final")


GATE_UP = ("ffn1_w_gate", "ffn1_w_up", "ffn2_w_gate", "ffn2_w_up")
_FFN = ("ffn1_w_gate", "ffn1_w_up", "ffn1_w_down", "ffn2_w_gate", "ffn2_w_up", "ffn2_w_down")
_CAST_GROUPS = ((_FFN, "cast_ffn"), (("w_in",), "cast_w_in"), (("w_branch_att", "w_branch_sgu"), "cast_branch"),
                (("w_out",), "cast_w_out"))


def _big_form(name, a):
    return jnp.swapaxes(a, 1, 2)[0] if name in GATE_UP else a[0]


def _big_back(name, a):
    return jnp.swapaxes(a[None], 1, 2) if name in GATE_UP else a[None]


def _small_form(name, a):
    if name == "norm_final":
        return a.reshape(1, D_MODEL)
    if name == "rel_bias":
        return jnp.pad(a[0], ((0, 0), (0, REL_PAD - N_REL)))
    if name in ("sgu_w_s", "sgu_b_s"):
        return a[0]
    return a


def _small_back(name, a, like):
    if name == "rel_bias":
        a = a[:, :N_REL]
    return a.reshape(like.shape)


def kernel(x, norm_ffn1, ffn1_w_gate, ffn1_w_up, ffn1_w_down, norm_mix, w_in, b_gate, rel_bias, sgu_ln_g, sgu_ln_b, sgu_w_s, sgu_b_s, w_branch_att, w_branch_sgu, w_out, norm_ffn2, ffn2_w_gate, ffn2_w_up, ffn2_w_down, norm_final, loss_target, m_norm_ffn1, m_ffn1_w_gate, m_ffn1_w_up, m_ffn1_w_down, m_norm_mix, m_w_in, m_b_gate, m_rel_bias, m_sgu_ln_g, m_sgu_ln_b, m_sgu_w_s, m_sgu_b_s, m_w_branch_att, m_w_branch_sgu, m_w_out, m_norm_ffn2, m_ffn2_w_gate, m_ffn2_w_up, m_ffn2_w_down, m_norm_final, v_norm_ffn1, v_ffn1_w_gate, v_ffn1_w_up, v_ffn1_w_down, v_norm_mix, v_w_in, v_b_gate, v_rel_bias, v_sgu_ln_g, v_sgu_ln_b, v_sgu_w_s, v_sgu_b_s, v_w_branch_att, v_w_branch_sgu, v_w_out, v_norm_ffn2, v_ffn2_w_gate, v_ffn2_w_up, v_ffn2_w_down, v_norm_final):
    w = dict(norm_ffn1=norm_ffn1, ffn1_w_gate=ffn1_w_gate, ffn1_w_up=ffn1_w_up, ffn1_w_down=ffn1_w_down, norm_mix=norm_mix,
             w_in=w_in, b_gate=b_gate, rel_bias=rel_bias, sgu_ln_g=sgu_ln_g, sgu_ln_b=sgu_ln_b, sgu_w_s=sgu_w_s,
             sgu_b_s=sgu_b_s, w_branch_att=w_branch_att, w_branch_sgu=w_branch_sgu, w_out=w_out, norm_ffn2=norm_ffn2,
             ffn2_w_gate=ffn2_w_gate, ffn2_w_up=ffn2_w_up, ffn2_w_down=ffn2_w_down, norm_final=norm_final)
    m = dict(norm_ffn1=m_norm_ffn1, ffn1_w_gate=m_ffn1_w_gate, ffn1_w_up=m_ffn1_w_up, ffn1_w_down=m_ffn1_w_down,
             norm_mix=m_norm_mix, w_in=m_w_in, b_gate=m_b_gate, rel_bias=m_rel_bias, sgu_ln_g=m_sgu_ln_g,
             sgu_ln_b=m_sgu_ln_b, sgu_w_s=m_sgu_w_s, sgu_b_s=m_sgu_b_s, w_branch_att=m_w_branch_att,
             w_branch_sgu=m_w_branch_sgu, w_out=m_w_out, norm_ffn2=m_norm_ffn2, ffn2_w_gate=m_ffn2_w_gate,
             ffn2_w_up=m_ffn2_w_up, ffn2_w_down=m_ffn2_w_down, norm_final=m_norm_final)
    v = dict(norm_ffn1=v_norm_ffn1, ffn1_w_gate=v_ffn1_w_gate, ffn1_w_up=v_ffn1_w_up, ffn1_w_down=v_ffn1_w_down,
             norm_mix=v_norm_mix, w_in=v_w_in, b_gate=v_b_gate, rel_bias=v_rel_bias, sgu_ln_g=v_sgu_ln_g,
             sgu_ln_b=v_sgu_ln_b, sgu_w_s=v_sgu_w_s, sgu_b_s=v_sgu_b_s, w_branch_att=v_w_branch_att,
             w_branch_sgu=v_w_branch_sgu, w_out=v_w_out, norm_ffn2=v_norm_ffn2, ffn2_w_gate=v_ffn2_w_gate,
             ffn2_w_up=v_ffn2_w_up, ffn2_w_down=v_ffn2_w_down, norm_final=v_norm_final)

    core = lax.axis_index("c").astype(jnp.int32).reshape(1)
    chip = (2 * lax.axis_index("x") + lax.axis_index("y")).astype(jnp.int32).reshape(1)

    wk = {n: _big_form(n, w[n]) for n in BIG}
    slots = {}
    for names, call in _CAST_GROUPS:
        slots.update(zip(names, _cast_slots([wk[n] for n in names], chip, call)))
    ws = {n: _small_form(n, w[n]) for n in SMALL}
    _, gx, shard_grads, small_sums = _local_step(x[0], loss_target[0], slots, ws, (core, chip))

    small, loss = _adam_small(small_sums, ws, {n: _small_form(n, m[n]) for n in SMALL},
                              {n: _small_form(n, v[n]) for n in SMALL})
    grad, delta, new_m, new_v = {}, {}, {}, {}
    for n in SMALL:
        grad[n], delta[n], new_m[n], new_v[n] = (_small_back(n, a, w[n]) for a in small[n])
    for n in BIG:
        g2, d2, m2, v2 = _adam_big(wk[n], shard_grads[n], _big_form(n, m[n]), _big_form(n, v[n]), "adam_" + n)
        grad[n], delta[n], new_m[n], new_v[n] = (_big_back(n, a) for a in (g2, d2, m2, v2))

    return (loss[0, 0], gx.reshape(x.shape), *[grad[n] for n in WEIGHTS], *[delta[n] for n in WEIGHTS],
            *[new_m[n] for n in WEIGHTS], *[new_v[n] for n in WEIGHTS])
```

```python
import functools

import jax
import jax.numpy as jnp
from jax import lax
from jax.experimental import pallas as pl
from jax.experimental.pallas import tpu as pltpu

F32 = jnp.float32
BF16 = jnp.bfloat16

D_MODEL = 1024
N_SHARD = 4
D_FF = 2816
FF_S = D_FF // N_SHARD
D_ATT = 512
D_SGU = 512
D_IN = 3 * D_ATT + 2 * D_SGU + 2 * D_MODEL
IN_S = D_IN // N_SHARD
BR_S = D_MODEL // N_SHARD
HEADS = 8
HEAD_DIM = 64
CHUNK = 64
N_LEFT = 8
BAND = (N_LEFT + 1) * CHUNK
REL_CLIP = 256
N_REL = 2 * REL_CLIP + 1
REL_PAD = 640
SGU_BLOCK = 128
SGU_GROUPS = 8
SGU_GDIM = 64
EPS = 1e-6
NEG_INF = -1e30

ATT_ROWS = 2 * CHUNK
ATT_KEYS = BAND + CHUNK
ATT_PAD = N_LEFT * CHUNK

ADAM_LR = 0.001
ADAM_B1 = 0.9
ADAM_B2 = 0.999
ADAM_EPS = 1e-08
ADAM_WD = 0.01
ADAM_STEP = 10

TM = 256
TW = 1024
DGRAD_ROWS = 64
VMEM_LIMIT = 56 * 1024 * 1024

SMALL_ROWS = 160
LOSS_ROW = 152
MESH = pl.DeviceIdType.MESH

_NT = (((1,), (1,)), ((), ()))
_TN = (((0,), (0,)), ((), ()))


def _params(sem=None):
    return pltpu.CompilerParams(dimension_semantics=sem, vmem_limit_bytes=VMEM_LIMIT)


def _const_spec(shape):
    nd = len(shape)
    return pl.BlockSpec(shape, lambda *_: (0,) * nd, pipeline_mode=pl.Buffered(1))


def _acc_spec(shape):
    nd = len(shape)
    return pl.BlockSpec(shape, lambda *_: (0,) * nd)


def _row_spec(tm, ncols, off=0):
    return pl.BlockSpec((tm, ncols), lambda i: (i + off, 0))


def _row3_spec(tm, ncols):
    return pl.BlockSpec((N_SHARD, tm, ncols), lambda i: (0, i, 0))


def _dot(a, b):
    return jnp.dot(a, b, preferred_element_type=F32)


def _dot_nt(a, b):
    return lax.dot_general(a, b, _NT, preferred_element_type=F32)


def _dot_tn(a, b):
    return lax.dot_general(a, b, _TN, preferred_element_type=F32)


def _rms_fwd(x, g):
    r = lax.rsqrt(jnp.mean(x * x, axis=-1, keepdims=True) + EPS)
    xhat = x * r
    return xhat, r, xhat * g


def _rms_bwd(dh, xhat, r, g):
    dxhat = dh * g
    dx = r * (dxhat - xhat * jnp.mean(dxhat * xhat, axis=-1, keepdims=True))
    dg = jnp.sum(dh * xhat, axis=0, keepdims=True)
    return dx, dg


def _sigmoid(x):
    return 1.0 / (1.0 + jnp.exp(-x))


def _edges(n_steps):
    return [(0, True), (n_steps - 1, False)]


def _ffn_fwd(x, g, wg, wu, wd, name, payload=None, head=None):
    T = x.shape[0]

    def body(x_ref, g_ref, wg_ref, wu_ref, wd_ref, *rest):
        if head:
            t_ref, gf_ref, xo_ref, h_ref, a_ref, b_ref, loss_ref, dgf_ref = rest
        else:
            xo_ref, h_ref, a_ref, b_ref = rest
        xv = x_ref[...]
        hb = _rms_fwd(xv, g_ref[...])[2].astype(BF16)
        h_ref[...] = hb
        acc = jnp.zeros((TM, D_MODEL), F32)
        for s in range(N_SHARD):
            a = _dot_nt(hb, wg_ref[s])
            b = _dot_nt(hb, wu_ref[s])
            a_ref[s] = a.astype(BF16)
            b_ref[s] = b.astype(BF16)
            sv = a * _sigmoid(a) * b
            acc += _dot(sv.astype(BF16), wd_ref[s])
        xo = xv + 0.5 * acc
        if not head:
            xo_ref[...] = xo
            return

        @pl.when(pl.program_id(0) == 0)
        def _():
            loss_ref[...] = jnp.zeros_like(loss_ref)
            dgf_ref[...] = jnp.zeros_like(dgf_ref)

        gf = gf_ref[...]
        xhat, r, y = _rms_fwd(xo, gf)
        err = y - t_ref[...]
        loss_ref[...] += 0.5 * jnp.sum(jnp.mean(err * err, axis=-1, keepdims=True), axis=0, keepdims=True)
        dxn, dgf = _rms_bwd(err * (1.0 / D_MODEL), xhat, r, gf)
        xo_ref[...] = dxn
        dgf_ref[...] += dgf

    tok = jax.ShapeDtypeStruct((T, D_MODEL), F32)
    act = jax.ShapeDtypeStruct((N_SHARD, T, FF_S), BF16)
    return _call(
        body, payload, name=name, grid=(T // TM,), when=_edges(T // TM), sem=("arbitrary",),
        in_specs=[_row_spec(TM, D_MODEL), _const_spec((1, D_MODEL)), _const_spec(wg.shape), _const_spec(wu.shape),
                  _const_spec(wd.shape)] + ([_row_spec(TM, D_MODEL), _const_spec((1, D_MODEL))] if head else []),
        out_specs=[_row_spec(TM, D_MODEL), _row_spec(TM, D_MODEL), _row3_spec(TM, FF_S), _row3_spec(TM, FF_S)]
        + ([_acc_spec((1, 128)), _acc_spec((1, D_MODEL))] if head else []),
        out_shape=[tok, jax.ShapeDtypeStruct((T, D_MODEL), BF16), act, act]
        + ([jax.ShapeDtypeStruct((1, 128), F32), jax.ShapeDtypeStruct((1, D_MODEL), F32)] if head else []),
        operands=(x, g, wg, wu, wd) + (tuple(head) if head else ()))


def _ffn_dgrad(dout, x, a, b, g, wg, wu, wd, name, payload=None):
    T = x.shape[0]

    def body(do_ref, x_ref, a_ref, b_ref, g_ref, wg_ref, wu_ref, wd_ref, dx_ref, da_ref, db_ref, dg_ref):
        do = do_ref[...]
        dob = (0.5 * do).astype(BF16)
        dh = jnp.zeros((TM, D_MODEL), F32)
        ds_next = _dot_nt(dob, wd_ref[0])
        for s in range(N_SHARD):
            ds = ds_next
            if s + 1 < N_SHARD:
                ds_next = _dot_nt(dob, wd_ref[s + 1])
            for r0 in range(0, TM, DGRAD_ROWS):
                rows = slice(r0, r0 + DGRAD_ROWS)
                av = a_ref[s, rows, :].astype(F32)
                bv = b_ref[s, rows, :].astype(F32)
                sig = _sigmoid(av)
                dsr = ds[rows]
                da_ref[s, rows, :] = (dsr * bv * (sig * (1.0 + av * (1.0 - sig)))).astype(BF16)
                db_ref[s, rows, :] = (dsr * (av * sig)).astype(BF16)
            dh += _dot(da_ref[s], wg_ref[s]) + _dot(db_ref[s], wu_ref[s])
        gv = g_ref[...]
        xhat, r, _ = _rms_fwd(x_ref[...], gv)
        dxn, dg = _rms_bwd(dh, xhat, r, gv)
        dx_ref[...] = do + dxn

        @pl.when(pl.program_id(0) == 0)
        def _():
            dg_ref[...] = jnp.zeros_like(dg_ref)

        dg_ref[...] += dg

    return _call(
        body, payload, name=name, grid=(T // TM,), when=_edges(T // TM), sem=("arbitrary",),
        in_specs=[_row_spec(TM, D_MODEL), _row_spec(TM, D_MODEL), _row3_spec(TM, FF_S), _row3_spec(TM, FF_S),
                  _const_spec((1, D_MODEL)), _const_spec(wg.shape), _const_spec(wu.shape), _const_spec(wd.shape)],
        out_specs=[_row_spec(TM, D_MODEL), _row3_spec(TM, FF_S), _row3_spec(TM, FF_S), _acc_spec((1, D_MODEL))],
        out_shape=[jax.ShapeDtypeStruct((T, D_MODEL), F32), jax.ShapeDtypeStruct((N_SHARD, T, FF_S), BF16),
                   jax.ShapeDtypeStruct((N_SHARD, T, FF_S), BF16), jax.ShapeDtypeStruct((1, D_MODEL), F32)],
        operands=(dout, x, a, b, g, wg, wu, wd))


def _ffn_wgrad(h, dout, a, b, da, db, name, payload=None):
    T = h.shape[0]

    def body(h_ref, do_ref, a_ref, b_ref, da_ref, db_ref, gwg_ref, gwu_ref, gwd_ref):
        @pl.when(pl.program_id(1) == 0)
        def _():
            gwg_ref[...] = jnp.zeros_like(gwg_ref)
            gwu_ref[...] = jnp.zeros_like(gwu_ref)
            gwd_ref[...] = jnp.zeros_like(gwd_ref)

        hv = h_ref[...]
        gwg_ref[0] += _dot_tn(da_ref[0], hv)
        gwu_ref[0] += _dot_tn(db_ref[0], hv)
        dob = do_ref[...].astype(BF16)
        av = a_ref[0].astype(F32)
        sv = (0.5 * av * _sigmoid(av) * b_ref[0].astype(F32)).astype(BF16)
        gwd_ref[0] += _dot_tn(sv, dob)

    tw = min(TW, T)
    tok = pl.BlockSpec((tw, D_MODEL), lambda s, i: (i, 0))
    act = pl.BlockSpec((1, tw, FF_S), lambda s, i: (s, i, 0))
    return _call(
        body, payload, name=name, grid=(N_SHARD, T // tw), when=_edges(N_SHARD * (T // tw)),
        sem=("arbitrary", "arbitrary"),
        in_specs=[tok, tok, act, act, act, act],
        out_specs=[pl.BlockSpec((1, FF_S, D_MODEL), lambda s, i: (s, 0, 0))] * 3,
        out_shape=[jax.ShapeDtypeStruct((N_SHARD, FF_S, D_MODEL), F32)] * 3,
        operands=(h, dout, a, b, da, db))


def _in_fwd(x, g, w_in, lng, lnb, w_s, bst, payload=None):
    T = x.shape[0]

    def body(x_ref, g_ref, w_ref, lng_ref, lnb_ref, ws_ref, bst_ref, h_ref, qkv_ref, zs_ref, gl_ref, y_ref):
        hb = _rms_fwd(x_ref[...], g_ref[...])[2].astype(BF16)
        h_ref[...] = hb
        z0 = _dot(hb, w_ref[0])
        qkv_ref[:, 0:IN_S] = z0.astype(BF16)
        z1 = _dot(hb, w_ref[1])
        qkv_ref[:, IN_S:3 * D_ATT] = z1[:, 0:384].astype(BF16)
        z2 = _dot(hb, w_ref[2])
        zs = jnp.concatenate([z1[:, 384:IN_S], z2[:, 0:256]], axis=1)
        zs_ref[...] = zs
        gl_ref[:, 0:896] = z2[:, 256:IN_S]
        gl_ref[:, 896:2048] = _dot(hb, w_ref[3])
        _, u, _, _, vn = _sgu_norm(zs, lng_ref[...], lnb_ref[...])
        gm = _group_masks()
        bst_v = bst_ref[...]
        for n in range(TM // SGU_BLOCK):
            rows = slice(n * SGU_BLOCK, (n + 1) * SGU_BLOCK)
            y_ref[rows, :] = (u[rows] * _sgu_mix(vn[rows], ws_ref, bst_v, gm)).astype(BF16)

    return _call(
        body, payload, name="in_fwd", grid=(T // TM,), when=_edges(T // TM), sem=("arbitrary",),
        in_specs=[_row_spec(TM, D_MODEL), _const_spec((1, D_MODEL)), _const_spec(w_in.shape), _const_spec((1, D_SGU)),
                  _const_spec((1, D_SGU)), _const_spec(w_s.shape), _const_spec(bst.shape)],
        out_specs=[_row_spec(TM, D_MODEL), _row_spec(TM, 3 * D_ATT), _row_spec(TM, 2 * D_SGU), _row_spec(TM, 2 * D_MODEL),
                   _row_spec(TM, D_SGU)],
        out_shape=[jax.ShapeDtypeStruct((T, D_MODEL), BF16), jax.ShapeDtypeStruct((T, 3 * D_ATT), BF16),
                   jax.ShapeDtypeStruct((T, 2 * D_SGU), F32), jax.ShapeDtypeStruct((T, 2 * D_MODEL), F32),
                   jax.ShapeDtypeStruct((T, D_SGU), BF16)],
        operands=(x, g, w_in, lng, lnb, w_s, bst))


def _in_dgrad(dx_res, x, g, w_in, dq, dk, dv, dgl, zs, dy_sgu, lng, lnb, w_s, w_st, bst):
    T = x.shape[0]

    def body(dxr_ref, x_ref, g_ref, w_ref, dq_ref, dk_ref, dv_ref, dgl_ref, zs_ref, dys_ref, lng_ref, lnb_ref, ws_ref,
             wst_ref, bst_ref, dx_ref, dz_ref, dg_ref, dw_ref, dbt_ref, dlg_ref, dlb_ref):
        @pl.when(pl.program_id(0) == 0)
        def _():
            for ref in (dg_ref, dw_ref, dbt_ref, dlg_ref, dlb_ref):
                ref[...] = jnp.zeros_like(ref)

        dzs = _sgu_bwd_tile(zs_ref[...], dys_ref[...].astype(F32), lng_ref[...], lnb_ref[...], ws_ref, wst_ref,
                            bst_ref[...], dw_ref, dbt_ref, dlg_ref, dlb_ref)
        dz = jnp.concatenate([dq_ref[...], dk_ref[...].astype(BF16), dv_ref[...].astype(BF16), dzs, dgl_ref[...]], axis=1)
        dz_ref[...] = dz
        dh = jnp.zeros((TM, D_MODEL), F32)
        for s in range(N_SHARD):
            dh += _dot_nt(dz[:, s * IN_S:(s + 1) * IN_S], w_ref[s])
        gv = g_ref[...]
        xhat, r, _ = _rms_fwd(x_ref[...], gv)
        dxn, dg = _rms_bwd(dh, xhat, r, gv)
        dx_ref[...] = dxr_ref[...] + dxn
        dg_ref[...] += dg

    pad_blocks = ATT_PAD // TM
    return pl.pallas_call(
        body, name="in_dgrad", grid=(T // TM,),
        in_specs=[_row_spec(TM, D_MODEL), _row_spec(TM, D_MODEL), _const_spec((1, D_MODEL)), _const_spec(w_in.shape),
                  _row_spec(TM, D_ATT), _row_spec(TM, D_ATT, pad_blocks), _row_spec(TM, D_ATT, pad_blocks),
                  _row_spec(TM, 2 * D_MODEL), _row_spec(TM, 2 * D_SGU), _row_spec(TM, D_SGU), _const_spec((1, D_SGU)),
                  _const_spec((1, D_SGU)), _const_spec(w_s.shape), _const_spec(w_st.shape), _const_spec(bst.shape)],
        out_specs=[_row_spec(TM, D_MODEL), _row_spec(TM, D_IN), _acc_spec((1, D_MODEL)), _acc_spec(w_s.shape),
                   _acc_spec(bst.shape), _acc_spec((1, D_SGU)), _acc_spec((1, D_SGU))],
        out_shape=[jax.ShapeDtypeStruct((T, D_MODEL), F32), jax.ShapeDtypeStruct((T, D_IN), BF16),
                   jax.ShapeDtypeStruct((1, D_MODEL), F32), jax.ShapeDtypeStruct(w_s.shape, F32),
                   jax.ShapeDtypeStruct(bst.shape, F32), jax.ShapeDtypeStruct((1, D_SGU), F32),
                   jax.ShapeDtypeStruct((1, D_SGU), F32)],
        compiler_params=_params(("arbitrary",)),
    )(dx_res, x, g, w_in, dq, dk, dv, dgl, zs, dy_sgu, lng, lnb, w_s, w_st, bst)


def _in_wgrad(h, dz):
    T = h.shape[0]

    def body(h_ref, dz_ref, gw_ref):
        @pl.when(pl.program_id(1) == 0)
        def _():
            gw_ref[...] = jnp.zeros_like(gw_ref)

        gw_ref[0] += _dot_tn(h_ref[...], dz_ref[...])

    return pl.pallas_call(
        body, name="in_wgrad", grid=(N_SHARD, T // min(TW, T)),
        in_specs=[pl.BlockSpec((min(TW, T), D_MODEL), lambda s, i: (i, 0)),
                  pl.BlockSpec((min(TW, T), IN_S), lambda s, i: (i, s))],
        out_specs=pl.BlockSpec((1, D_MODEL, IN_S), lambda s, i: (s, 0, 0)),
        out_shape=jax.ShapeDtypeStruct((N_SHARD, D_MODEL, IN_S), F32),
        compiler_params=_params(("arbitrary", "arbitrary")),
    )(h, dz)


def _rel_onehot():
    r = lax.broadcasted_iota(jnp.int32, (REL_PAD, REL_PAD), 0)
    n = lax.broadcasted_iota(jnp.int32, (REL_PAD, REL_PAD), 1)
    idx = jnp.clip(BAND - 1 - n, -REL_CLIP, REL_CLIP) + REL_CLIP
    return jnp.where(r == idx, 1.0, 0.0).astype(BF16)


def _split3(v):
    p1 = v.astype(BF16)
    r1 = v - p1.astype(F32)
    p2 = r1.astype(BF16)
    p3 = (r1 - p2.astype(F32)).astype(BF16)
    return p1, p2, p3


def _relbias_fwd(tab_pad):
    def body(t_ref, o_ref):
        oh = _rel_onehot()
        acc = jnp.zeros((HEADS, REL_PAD), F32)
        for p in _split3(t_ref[...]):
            acc += _dot(p, oh)
        o_ref[...] = acc

    return pl.pallas_call(body, name="relbias_fwd", out_shape=jax.ShapeDtypeStruct((HEADS, REL_PAD), F32))(tab_pad)


def _relbias_bwd(z):
    def body(z_ref, o_ref):
        oh = _rel_onehot()
        dt2 = jnp.sum(z_ref[...], axis=1)
        acc = jnp.zeros((HEADS, REL_PAD), F32)
        for p in _split3(dt2):
            acc += _dot_nt(p, oh)
        o_ref[...] = acc

    return pl.pallas_call(body, name="relbias_bwd", out_shape=jax.ShapeDtypeStruct((HEADS, REL_PAD), F32))(z)


def _bias_blocks(t2):
    flat = jnp.tile(t2, (1, CHUNK))
    skew = flat[:, :CHUNK * (REL_PAD - 1)].reshape(HEADS, CHUNK, REL_PAD - 1)
    bias = skew[:, :, CHUNK - 1:CHUNK - 1 + BAND]
    slabs = [jnp.pad(bias, ((0, 0), (0, 0), (CHUNK * c, ATT_KEYS - BAND - CHUNK * c)), constant_values=NEG_INF)
             for c in range(2)]
    return jnp.concatenate(slabs, axis=1)


def _unskew(db2):
    out = []
    for c in range(2):
        slab = db2[:, CHUNK * c:CHUNK * (c + 1), CHUNK * c:CHUNK * c + BAND]
        y = jnp.pad(slab, ((0, 0), (0, 0), (CHUNK - 1, REL_PAD - BAND - CHUNK + 1)))
        yf = jnp.pad(y.reshape(HEADS, CHUNK * REL_PAD), ((0, 0), (0, CHUNK)))
        out.append(yf.reshape(HEADS, CHUNK, REL_PAD + 1)[:, :, :REL_PAD])
    return jnp.concatenate(out, axis=1)


def _att_load(qkv_hbm, q_s, k_s, v_s, sem, T):
    copies = [pltpu.make_async_copy(qkv_hbm.at[:, 0:D_ATT], q_s, sem.at[0]),
              pltpu.make_async_copy(qkv_hbm.at[:, D_ATT:2 * D_ATT], k_s.at[pl.ds(ATT_PAD, T), :], sem.at[1]),
              pltpu.make_async_copy(qkv_hbm.at[:, 2 * D_ATT:3 * D_ATT], v_s.at[pl.ds(ATT_PAD, T), :], sem.at[2])]
    for cp in copies:
        cp.start()
    k_s[0:ATT_PAD, :] = jnp.zeros((ATT_PAD, D_ATT), BF16)
    v_s[0:ATT_PAD, :] = jnp.zeros((ATT_PAD, D_ATT), BF16)
    for cp in copies:
        cp.wait()


def _head(v, h):
    return v[:, h * HEAD_DIM:(h + 1) * HEAD_DIM]


def _rows(v, h):
    return v[h * ATT_ROWS:(h + 1) * ATT_ROWS]


def _att_exp(qs, kw, bias_ref, valid):
    s = jnp.concatenate([_dot_nt(_head(qs, h), _head(kw, h)) + bias_ref[h] for h in range(HEADS)], axis=0)
    if valid is not None:
        s = jnp.where(valid, s, NEG_INF)
    e = jnp.exp(s - jnp.max(s, axis=-1, keepdims=True))
    return e, 1.0 / jnp.sum(e, axis=-1, keepdims=True)


def _att_blocks(T, block, keys_on_rows=False, middle=None):
    n_edge = min(ATT_PAD // ATT_ROWS, T // ATT_ROWS)
    shape, axis = ((ATT_KEYS, 1), 0) if keys_on_rows else ((1, ATT_KEYS), 1)

    def edge(i, carry):
        r0 = i * ATT_ROWS
        block(i, (lax.broadcasted_iota(jnp.int32, shape, axis) + (r0 - ATT_PAD)) >= 0)
        return carry

    def inner(i, carry):
        block(i, None)
        return carry

    n_blocks = T // ATT_ROWS
    lax.fori_loop(0, n_edge, edge, 0)
    if middle is None:
        lax.fori_loop(n_edge, n_blocks, inner, 0)
        return
    n_late = max(n_blocks - n_blocks // 4, n_edge)
    lax.fori_loop(n_edge, n_late, inner, 0)
    middle()
    lax.fori_loop(n_late, n_blocks, inner, 0)


def _att_fwd(qkv, bias2, payload=None):
    T = qkv.shape[0]

    def body(qkv_hbm, bias_ref, y_ref, q_s, k_s, v_s, sem, middle=None):
        _att_load(qkv_hbm, q_s, k_s, v_s, sem, T)

        def block(i, valid):
            r0 = pl.multiple_of(i * ATT_ROWS, ATT_ROWS)
            qs = q_s[pl.ds(r0, ATT_ROWS), :] * (HEAD_DIM ** -0.5)
            kw = k_s[pl.ds(r0, ATT_KEYS), :]
            vw = v_s[pl.ds(r0, ATT_KEYS), :]
            e, rinv = _att_exp(qs, kw, bias_ref, valid)
            eb = e.astype(BF16)
            outs = [_dot(_rows(eb, h), _head(vw, h)) * _rows(rinv, h) for h in range(HEADS)]
            y_ref[pl.ds(r0, ATT_ROWS), :] = jnp.concatenate(outs, axis=1).astype(BF16)

        _att_blocks(T, block, middle=middle)

    return _call(
        body, payload, name="att_fwd", grid=None, takes_middle=True,
        in_specs=[pl.BlockSpec(memory_space=pl.ANY), pl.BlockSpec(memory_space=pltpu.VMEM)],
        out_specs=[pl.BlockSpec(memory_space=pltpu.VMEM)],
        out_shape=[jax.ShapeDtypeStruct((T, D_ATT), BF16)],
        scratch_shapes=[pltpu.VMEM((T, D_ATT), BF16), pltpu.VMEM((T + ATT_PAD, D_ATT), BF16),
                        pltpu.VMEM((T + ATT_PAD, D_ATT), BF16), pltpu.SemaphoreType.DMA((3,))],
        operands=(qkv, bias2))


def _lanes(v, h):
    return v[:, h * ATT_ROWS:(h + 1) * ATT_ROWS]


def _att_bwd(qkv, dy, bias2t, payload=None):
    T = qkv.shape[0]

    def body(qkv_hbm, dy_ref, bias_ref, dq_ref, dk_ref, dv_ref, db_ref, q_s, k_s, v_s, sem):
        _att_load(qkv_hbm, q_s, k_s, v_s, sem, T)
        dk_ref[...] = jnp.zeros_like(dk_ref)
        dv_ref[...] = jnp.zeros_like(dv_ref)
        db_ref[...] = jnp.zeros_like(db_ref)

        def block(i, valid):
            r0 = pl.multiple_of(i * ATT_ROWS, ATT_ROWS)
            qs = q_s[pl.ds(r0, ATT_ROWS), :] * (HEAD_DIM ** -0.5)
            kw = k_s[pl.ds(r0, ATT_KEYS), :]
            vw = v_s[pl.ds(r0, ATT_KEYS), :]
            dyb = dy_ref[pl.ds(r0, ATT_ROWS), :]
            s = jnp.concatenate([_dot_nt(_head(kw, h), _head(qs, h)) + bias_ref[h] for h in range(HEADS)], axis=1)
            if valid is not None:
                s = jnp.where(valid, s, NEG_INF)
            e = jnp.exp(s - jnp.max(s, axis=0, keepdims=True))
            p = e * (1.0 / jnp.sum(e, axis=0, keepdims=True))
            dp = jnp.concatenate([_dot_nt(_head(vw, h), _head(dyb, h)) for h in range(HEADS)], axis=1)
            ds = p * (dp - jnp.sum(p * dp, axis=0, keepdims=True))
            for h in range(HEADS):
                db_ref[h] += _lanes(ds, h)
            dsb = ds.astype(BF16)
            pb = p.astype(BF16)
            dq = [_dot_tn(_lanes(dsb, h), _head(kw, h)) for h in range(HEADS)]
            dk = [_dot(_lanes(dsb, h), _head(qs, h)) for h in range(HEADS)]
            dv = [_dot(_lanes(pb, h), _head(dyb, h)) for h in range(HEADS)]
            dq_ref[pl.ds(r0, ATT_ROWS), :] = (jnp.concatenate(dq, axis=1) * (HEAD_DIM ** -0.5)).astype(BF16)
            dk_ref[pl.ds(r0, ATT_KEYS), :] += jnp.concatenate(dk, axis=1)
            dv_ref[pl.ds(r0, ATT_KEYS), :] += jnp.concatenate(dv, axis=1)

        _att_blocks(T, block, keys_on_rows=True)

    vmem = pl.BlockSpec(memory_space=pltpu.VMEM)
    return _call(
        body, payload, name="att_bwd", grid=None,
        in_specs=[pl.BlockSpec(memory_space=pl.ANY), vmem, vmem],
        out_specs=[vmem, vmem, vmem, vmem],
        out_shape=[jax.ShapeDtypeStruct((T, D_ATT), BF16), jax.ShapeDtypeStruct((T + ATT_PAD, D_ATT), F32),
                   jax.ShapeDtypeStruct((T + ATT_PAD, D_ATT), F32), jax.ShapeDtypeStruct((HEADS, ATT_KEYS, ATT_ROWS), F32)],
        scratch_shapes=[pltpu.VMEM((T, D_ATT), BF16), pltpu.VMEM((T + ATT_PAD, D_ATT), BF16),
                        pltpu.VMEM((T + ATT_PAD, D_ATT), BF16), pltpu.SemaphoreType.DMA((3,))],
        operands=(qkv, dy, bias2t))


_GELU_C = 0.7978845608028654
_GELU_A = 0.044715


def _gelu(x):
    t = jnp.tanh(_GELU_C * (x + _GELU_A * x * x * x))
    return 0.5 * x * (1.0 + t), t


def _gelu_grad(x, t):
    return 0.5 * (1.0 + t) + 0.5 * x * (1.0 - t * t) * _GELU_C * (1.0 + 3.0 * _GELU_A * x * x)


def _group_masks():
    col = lax.broadcasted_iota(jnp.int32, (SGU_GROUPS, D_SGU), 1) // SGU_GDIM
    grp = lax.broadcasted_iota(jnp.int32, (SGU_GROUPS, D_SGU), 0)
    return jnp.where(col == grp, 1.0, 0.0).astype(F32)


def _causal_mask(transposed=False):
    i = lax.broadcasted_iota(jnp.int32, (SGU_BLOCK, SGU_BLOCK), 0) // CHUNK
    j = lax.broadcasted_iota(jnp.int32, (SGU_BLOCK, SGU_BLOCK), 1) // CHUNK
    return (j >= i) if transposed else (i >= j)


def _sgu_norm(zs, lng, lnb):
    gz, t = _gelu(zs)
    u = gz[:, 0:D_SGU]
    vs = gz[:, D_SGU:2 * D_SGU]
    xc = vs - jnp.mean(vs, axis=-1, keepdims=True)
    rstd = lax.rsqrt(jnp.mean(xc * xc, axis=-1, keepdims=True) + EPS)
    xhat = xc * rstd
    return t, u, xhat, rstd, xhat * lng + lnb


def _sgu_mix(vn_blk, w_ref, bst, gm):
    mask = _causal_mask()
    s = jnp.zeros((SGU_BLOCK, D_SGU), F32)
    for g in range(SGU_GROUPS):
        wm = jnp.where(mask, w_ref[g], 0.0).astype(BF16)
        s += _dot(wm, (vn_blk * gm[g:g + 1, :]).astype(BF16))
        s += bst[:, g:g + 1] * gm[g:g + 1, :]
    return s


def _sgu_bwd_tile(zs_v, dyv, lng_v, lnb_v, w_ref, wt_ref, bst_v, dw_ref, dbt_ref, dlg_ref, dlb_ref):
    t, u, xhat, rstd, vn = _sgu_norm(zs_v, lng_v, lnb_v)
    gm = _group_masks()
    mask = _causal_mask()
    mask_t = _causal_mask(transposed=True)
    lane8 = lax.broadcasted_iota(jnp.int32, (1, SGU_GROUPS), 1)
    du_rows, dvn_rows = [], []
    for n in range(TM // SGU_BLOCK):
        rows = slice(n * SGU_BLOCK, (n + 1) * SGU_BLOCK)
        vn_b = vn[rows]
        s = _sgu_mix(vn_b, w_ref, bst_v, gm)
        du_rows.append(dyv[rows] * s)
        dsb = dyv[rows] * u[rows]
        vnb16 = vn_b.astype(BF16)
        dvn = jnp.zeros((SGU_BLOCK, D_SGU), F32)
        dbt = jnp.zeros((SGU_BLOCK, SGU_GROUPS), F32)
        for g in range(SGU_GROUPS):
            dsg = dsb * gm[g:g + 1, :]
            dsg16 = dsg.astype(BF16)
            wmt = jnp.where(mask_t, wt_ref[g], 0.0).astype(BF16)
            dvn += _dot(wmt, dsg16)
            dw_ref[g] += jnp.where(mask, _dot_nt(dsg16, vnb16), 0.0)
            dbt += jnp.sum(dsg, axis=-1, keepdims=True) * jnp.where(lane8 == g, 1.0, 0.0)
        dbt_ref[...] += dbt
        dvn_rows.append(dvn)
    du = jnp.concatenate(du_rows, axis=0)
    dvn = jnp.concatenate(dvn_rows, axis=0)
    dlg_ref[...] += jnp.sum(dvn * xhat, axis=0, keepdims=True)
    dlb_ref[...] += jnp.sum(dvn, axis=0, keepdims=True)
    dxhat = dvn * lng_v
    dvs = rstd * (dxhat - jnp.mean(dxhat, axis=-1, keepdims=True)
                  - xhat * jnp.mean(dxhat * xhat, axis=-1, keepdims=True))
    dgz = jnp.concatenate([du, dvs], axis=1)
    return (dgz * _gelu_grad(zs_v, t)).astype(BF16)


def _cols(v, s):
    return v[:, s * BR_S:(s + 1) * BR_S]


def _merge_fwd(x, y_att, y_sgu, gl, b_gate, wba, wbs, wo, payload=None):
    T = x.shape[0]

    def body(x_ref, ya_ref, ys_ref, gl_ref, bg_ref, wba_ref, wbs_ref, wo_ref, xo_ref, m_ref, pa_ref, ps_ref):
        ya = ya_ref[...]
        ys = ys_ref[...]
        pa = jnp.concatenate([_dot(ya, wba_ref[s]) for s in range(N_SHARD)], axis=1)
        ps = jnp.concatenate([_dot(ys, wbs_ref[s]) for s in range(N_SHARD)], axis=1)
        g = _sigmoid(gl_ref[...] + bg_ref[...])
        mb = (g[:, 0:D_MODEL] * pa + g[:, D_MODEL:2 * D_MODEL] * ps).astype(BF16)
        m_ref[...] = mb
        pa_ref[...] = pa.astype(BF16)
        ps_ref[...] = ps.astype(BF16)
        acc = jnp.zeros((TM, D_MODEL), F32)
        for s in range(N_SHARD):
            acc += _dot(_cols(mb, s), wo_ref[s])
        xo_ref[...] = x_ref[...] + acc

    tokd = jax.ShapeDtypeStruct((T, D_MODEL), BF16)
    return _call(
        body, payload, name="merge_fwd", grid=(T // TM,), when=_edges(T // TM), sem=("arbitrary",),
        in_specs=[_row_spec(TM, D_MODEL), _row_spec(TM, D_ATT), _row_spec(TM, D_SGU), _row_spec(TM, 2 * D_MODEL),
                  _const_spec((1, 2 * D_MODEL)), _const_spec(wba.shape), _const_spec(wbs.shape), _const_spec(wo.shape)],
        out_specs=[_row_spec(TM, D_MODEL)] * 4,
        out_shape=[jax.ShapeDtypeStruct((T, D_MODEL), F32), tokd, tokd, tokd],
        operands=(x, y_att, y_sgu, gl, b_gate, wba, wbs, wo))


def _merge_bwd(dx, y_att, y_sgu, gl, merged, pa, ps, b_gate, wba, wbs, wo, payload=None):
    T = dx.shape[0]

    def body(dx_ref, ya_ref, ys_ref, gl_ref, m_ref, pa_ref, ps_ref, bg_ref, wba_ref, wbs_ref, wo_ref,
             dya_ref, dys_ref, dgl_ref, dbg_ref, gwba_ref, gwbs_ref, gwo_ref):
        @pl.when(pl.program_id(0) == 0)
        def _():
            dbg_ref[...] = jnp.zeros_like(dbg_ref)
            gwba_ref[...] = jnp.zeros_like(gwba_ref)
            gwbs_ref[...] = jnp.zeros_like(gwbs_ref)
            gwo_ref[...] = jnp.zeros_like(gwo_ref)

        dxb = dx_ref[...].astype(BF16)
        dm = jnp.concatenate([_dot_nt(dxb, wo_ref[s]) for s in range(N_SHARD)], axis=1)
        g = _sigmoid(gl_ref[...] + bg_ref[...])
        ga = g[:, 0:D_MODEL]
        gs = g[:, D_MODEL:2 * D_MODEL]
        dpa = (dm * ga).astype(BF16)
        dps = (dm * gs).astype(BF16)
        dgl = jnp.concatenate([dm * pa_ref[...].astype(F32) * ga * (1.0 - ga),
                               dm * ps_ref[...].astype(F32) * gs * (1.0 - gs)], axis=1)
        dgl_ref[...] = dgl.astype(BF16)
        dbg_ref[...] += jnp.sum(dgl, axis=0, keepdims=True)
        ya = ya_ref[...]
        ys = ys_ref[...]
        mb = m_ref[...]
        dya = jnp.zeros((TM, D_ATT), F32)
        dys = jnp.zeros((TM, D_SGU), F32)
        for s in range(N_SHARD):
            dya += _dot_nt(_cols(dpa, s), wba_ref[s])
            dys += _dot_nt(_cols(dps, s), wbs_ref[s])
            gwo_ref[s] += _dot_tn(_cols(mb, s), dxb)
            gwba_ref[s] += _dot_tn(ya, _cols(dpa, s))
            gwbs_ref[s] += _dot_tn(ys, _cols(dps, s))
        dya_ref[...] = dya.astype(BF16)
        dys_ref[...] = dys.astype(BF16)

    return _call(
        body, payload, name="merge_bwd", grid=(T // TM,), when=_edges(T // TM), sem=("arbitrary",),
        operands=(dx, y_att, y_sgu, gl, merged, pa, ps, b_gate, wba, wbs, wo),
        in_specs=[_row_spec(TM, D_MODEL), _row_spec(TM, D_ATT), _row_spec(TM, D_SGU), _row_spec(TM, 2 * D_MODEL),
                  _row_spec(TM, D_MODEL), _row_spec(TM, D_MODEL), _row_spec(TM, D_MODEL),
                  _const_spec((1, 2 * D_MODEL)), _const_spec(wba.shape), _const_spec(wbs.shape), _const_spec(wo.shape)],
        out_specs=[_row_spec(TM, D_ATT), _row_spec(TM, D_SGU), _row_spec(TM, 2 * D_MODEL), _acc_spec((1, 2 * D_MODEL)),
                   _acc_spec(wba.shape), _acc_spec(wbs.shape), _acc_spec(wo.shape)],
        out_shape=[jax.ShapeDtypeStruct((T, D_ATT), BF16), jax.ShapeDtypeStruct((T, D_SGU), BF16),
                   jax.ShapeDtypeStruct((T, 2 * D_MODEL), BF16), jax.ShapeDtypeStruct((1, 2 * D_MODEL), F32),
                   jax.ShapeDtypeStruct(wba.shape, F32), jax.ShapeDtypeStruct(wbs.shape, F32),
                   jax.ShapeDtypeStruct(wo.shape, F32)])


BIG = ("ffn1_w_gate", "ffn1_w_up", "ffn1_w_down", "w_in", "w_branch_att", "w_branch_sgu", "w_out",
       "ffn2_w_gate", "ffn2_w_up", "ffn2_w_down")
SMALL = ("norm_ffn1", "norm_mix", "b_gate", "rel_bias", "sgu_ln_g", "sgu_ln_b", "sgu_w_s", "sgu_b_s", "norm_ffn2",
         "norm_final")


G_FFN1 = ("ffn1_w_gate", "ffn1_w_up", "ffn1_w_down")
G_MIX = ("w_in", "w_branch_att", "w_branch_sgu", "w_out")
G_FFN2 = ("ffn2_w_gate", "ffn2_w_up", "ffn2_w_down")


def _local_step(x, target, wb, ws, dist=None):
    def gather_on(names):
        return _ag_payload([wb[n] for n in names]) if dist else None

    t2 = _relbias_fwd(ws["rel_bias"])
    bias2 = _bias_blocks(t2)
    bst = ws["sgu_b_s"].T
    w_st = jnp.swapaxes(ws["sgu_w_s"], 1, 2)

    if dist:
        wb.update(zip(G_FFN1, _call(lambda: None, gather_on(G_FFN1), name="allgather_ffn1", grid=None, in_specs=[],
                                    out_specs=[], out_shape=[])))
    x1, h1, a1, b1, *got = _ffn_fwd(x, ws["norm_ffn1"], wb["ffn1_w_gate"], wb["ffn1_w_up"], wb["ffn1_w_down"],
                                    "ffn1_fwd", gather_on(G_MIX))
    wb.update(zip(G_MIX, got))
    h2, qkv, zs, gl, y_sgu, *got = _in_fwd(x1, ws["norm_mix"], wb["w_in"], ws["sgu_ln_g"], ws["sgu_ln_b"],
                                           ws["sgu_w_s"], bst, gather_on(G_FFN2[0:1]))
    wb.update(zip(G_FFN2[0:1], got))
    y_att, *got = _att_fwd(qkv, bias2, gather_on(G_FFN2[1:2]))
    wb.update(zip(G_FFN2[1:2], got))
    x2, merged, pa, ps, *got = _merge_fwd(x1, y_att, y_sgu, gl, ws["b_gate"], wb["w_branch_att"], wb["w_branch_sgu"],
                                          wb["w_out"], gather_on(G_FFN2[2:3]))
    wb.update(zip(G_FFN2[2:3], got))
    dx3, h3, a3, b3, loss, g_final = _ffn_fwd(x2, ws["norm_ffn2"], wb["ffn2_w_gate"], wb["ffn2_w_up"],
                                              wb["ffn2_w_down"], "ffn2_fwd", head=(target, ws["norm_final"]))

    gb, gs, sums = {}, {"norm_final": g_final}, {}

    def pair_on(names, small=None):
        return _px_payload([gb[n] for n in names], small) if dist else None

    def pair_add(names, halves):
        for n, rv in zip(names, halves):
            sums[n] = _pair_add(gb[n], rv, dist[0], dist[1], "pair_add_" + n)

    def chips_on(names):
        return _cx_payload([sums[n][1] for n in names], [sums[n][2] for n in names]) if dist else None

    dx2, da3, db3, gs["norm_ffn2"] = _ffn_dgrad(dx3, x2, a3, b3, ws["norm_ffn2"], wb["ffn2_w_gate"], wb["ffn2_w_up"],
                                                wb["ffn2_w_down"], "ffn2_dgrad")
    gb["ffn2_w_gate"], gb["ffn2_w_up"], gb["ffn2_w_down"] = _ffn_wgrad(h3, dx3, a3, b3, da3, db3, "ffn2_wgrad")
    dy_att, dy_sgu, dgl, gs["b_gate"], gb["w_branch_att"], gb["w_branch_sgu"], gb["w_out"], *got = _merge_bwd(
        dx2, y_att, y_sgu, gl, merged, pa, ps, ws["b_gate"], wb["w_branch_att"], wb["w_branch_sgu"], wb["w_out"],
        pair_on(G_FFN2))
    pair_add(G_FFN2, got)
    dq, dk, dv, db2t, *lands2 = _att_bwd(qkv, dy_att, jnp.swapaxes(bias2, 1, 2), chips_on(G_FFN2))
    gs["rel_bias"] = _relbias_bwd(_unskew(jnp.swapaxes(db2t, 1, 2)))
    dx1, dz, gs["norm_mix"], gs["sgu_w_s"], dbt, gs["sgu_ln_g"], gs["sgu_ln_b"] = _in_dgrad(
        dx2, x1, ws["norm_mix"], wb["w_in"], dq, dk, dv, dgl, zs, dy_sgu, ws["sgu_ln_g"], ws["sgu_ln_b"], ws["sgu_w_s"],
        w_st, bst)
    gs["sgu_b_s"] = dbt.T
    gb["w_in"] = _in_wgrad(h2, dz)
    gx, da1, db1, gs["norm_ffn1"], *got = _ffn_dgrad(dx1, x, a1, b1, ws["norm_ffn1"], wb["ffn1_w_gate"],
                                                    wb["ffn1_w_up"], wb["ffn1_w_down"], "ffn1_dgrad", pair_on(G_MIX))
    pair_add(G_MIX, got)
    gb["ffn1_w_gate"], gb["ffn1_w_up"], gb["ffn1_w_down"], *lands_mix = _ffn_wgrad(h1, dx1, a1, b1, da1, db1,
                                                                                   "ffn1_wgrad", chips_on(G_MIX))
    if not dist:
        return loss, gx, gb, gs

    def final_sums(names, lands):
        return [_final_sum(sums[n][0], land, dist[1], dist[0], "final_sum_" + n) for n, land in zip(names, lands)]

    early = G_FFN2 + G_MIX
    k = len(G_FFN1)
    tail = _tail_reduce([gb[n] for n in G_FFN1], _pack_small(gs, loss),
                        _ss_payload(final_sums(early, list(lands2) + list(lands_mix))))
    for i, n in enumerate(G_FFN1):
        sums[n] = (tail[i],)
    small_sums, shared = tail[2 * k], dict(zip(early, tail[2 * k + 1:]))
    shared.update(zip(G_FFN1, _sibling_share(final_sums(G_FFN1, tail[k:2 * k]), "sibling_share")))
    return loss, gx, shared, small_sums


_ANY = pl.BlockSpec(memory_space=pl.ANY)
_VMEM = pl.BlockSpec(memory_space=pltpu.VMEM)


def _mesh_pos():
    return lax.axis_index("x"), lax.axis_index("y"), lax.axis_index("c")


def _cast_slots(shards, chip, name):
    n = len(shards)
    r, ncol = shards[0].shape
    tr = r // 2

    def body(me_ref, *refs):
        for i_ref, o_ref in zip(refs[:n], refs[n:]):
            o_ref[0] = i_ref[...].astype(BF16)

    grid_spec = pltpu.PrefetchScalarGridSpec(
        num_scalar_prefetch=1, grid=(r // tr,),
        in_specs=[pl.BlockSpec((tr, ncol), lambda i, me: (i, 0))] * n,
        out_specs=[pl.BlockSpec((1, tr, ncol), lambda i, me: (me[0], i, 0))] * n)
    return pl.pallas_call(
        body, name=name, grid_spec=grid_spec,
        out_shape=[jax.ShapeDtypeStruct((N_SHARD, r, ncol), BF16)] * n,
        compiler_params=_params(("arbitrary",)),
    )(chip, *shards)


class _Payload:
    def __init__(self, arrays, out_shapes, aliases, scratch, phases):
        self.arrays = list(arrays)
        self.out_shapes = list(out_shapes)
        self.aliases = dict(aliases)
        self.scratch = list(scratch)
        self.phases = phases


def _remote(src, dst, ssem, rsem, dev):
    return pltpu.make_async_remote_copy(src_ref=src, dst_ref=dst, send_sem=ssem, recv_sem=rsem, device_id=dev,
                                        device_id_type=MESH)


def _call(body, payload, *, name, grid, in_specs, out_specs, out_shape, scratch_shapes=(), sem=None, when=None,
          operands=(), takes_middle=False):
    in_specs, out_specs, out_shape = list(in_specs), list(out_specs), list(out_shape)
    scratch_shapes = list(scratch_shapes)
    n_in, n_out, n_scr = len(in_specs), len(out_specs), len(scratch_shapes)
    kwargs = {}
    kernel = body
    if payload is not None:
        k_in, k_out = len(payload.arrays), len(payload.out_shapes)
        rank = len(grid) if grid else 0

        def kernel(*refs):
            a, b = n_in, n_in + k_in
            c, d = b + n_out, b + n_out + k_out
            e = d + n_scr
            phases = payload.phases(refs[a:b], refs[c:d], refs[e:])

            def run():
                body(*refs[:a], *refs[b:c], *refs[d:e])

            if not grid:
                phases[0]()
                if len(phases) == 3 and takes_middle:
                    body(*refs[:a], *refs[b:c], *refs[d:e], middle=phases[1])
                    phases[2]()
                    return
                run()
                for ph in phases[1:]:
                    ph()
                return
            step = pl.program_id(0)
            if rank == 2:
                step = step * grid[1] + pl.program_id(1)
            marks = list(when)
            if len(phases) == 3:
                marks = [when[0], (max(when[1][0] - 3, 0), False), when[1]]
            for ph, (at, before) in zip(phases, marks):
                if before:
                    pl.when(step == at)(ph)
            run()
            for ph, (at, before) in zip(phases, marks):
                if not before:
                    pl.when(step == at)(ph)

        in_specs += [_ANY] * k_in
        out_specs += [_ANY] * k_out
        out_shape += payload.out_shapes
        scratch_shapes += payload.scratch
        kwargs["input_output_aliases"] = {n_in + i: n_out + j for i, j in payload.aliases.items()}
        operands = tuple(operands) + tuple(payload.arrays)
    if grid:
        kwargs["grid"] = grid
    return pl.pallas_call(kernel, name=name, in_specs=in_specs, out_specs=out_specs, out_shape=out_shape,
                          scratch_shapes=scratch_shapes, compiler_params=_params(sem), **kwargs)(*operands)


def _ag_payload(slots):
    n = len(slots)

    def phases(_, refs, sems):
        send_i, recv_i, send_d, recv_d = sems
        x, y, c = _mesh_pos()
        me = 2 * x + y

        def half(w, core):
            rh = slots[w].shape[1] // 2
            return pl.ds(core * rh, rh)

        def ici(w, j):
            t = (me + 1 + j) % N_SHARD
            mine = refs[w].at[me, half(w, c), :]
            return _remote(mine, mine, send_i.at[3 * w + j], recv_i.at[3 * w + j], (t // 2, t % 2, c))

        def d2d(w, j, core):
            s = (me + 3 - j) % N_SHARD
            land = refs[w].at[s, half(w, core), :]
            return _remote(land, land, send_d.at[3 * w + j], recv_d.at[3 * w + j], (x, y, 1 - c))

        def start():
            for w in range(n):
                for j in range(3):
                    ici(w, j).start()

        def forward():
            for w in range(n):
                for j in range(3):
                    s = (me + 3 - j) % N_SHARD
                    land = refs[w].at[s, half(w, c), :]
                    _remote(land, land, send_i.at[3 * w + j], recv_i.at[3 * w + j], (x, y, c)).wait_recv()
                    d2d(w, j, c).start()

        def finish():
            for w in range(n):
                for j in range(3):
                    d2d(w, j, 1 - c).wait_recv()
            for w in range(n):
                for j in range(3):
                    ici(w, j).wait_send()
                    d2d(w, j, c).wait_send()

        return [start, forward, finish]

    return _Payload(slots, [jax.ShapeDtypeStruct(s.shape, s.dtype) for s in slots], {i: i for i in range(n)},
                    [pltpu.SemaphoreType.DMA((3 * n,)) for _ in range(4)], phases)


def _px_payload(grads, small=None):
    arrays = list(grads) + ([small] if small is not None else [])
    n = len(arrays)

    def phases(ins, outs, sems):
        send, recv = sems
        x, y, c = _mesh_pos()

        def copy(w):
            if w < len(grads):
                rh = grads[w].shape[1] // 2
                src = ins[w].at[:, pl.ds((1 - c) * rh, rh), :]
            else:
                src = ins[w]
            return _remote(src, outs[w], send.at[w], recv.at[w], (x, y, 1 - c))

        def start():
            for w in range(n):
                copy(w).start()

        def finish():
            for w in range(n):
                copy(w).wait()

        return [start, finish]

    out_shapes = [jax.ShapeDtypeStruct((N_SHARD, g.shape[1] // 2, g.shape[2]), F32) for g in grads]
    if small is not None:
        out_shapes.append(jax.ShapeDtypeStruct(small.shape, F32))
    return _Payload(arrays, out_shapes, {}, [pltpu.SemaphoreType.DMA((n,)), pltpu.SemaphoreType.DMA((n,))], phases)


def _cx_payload(pbs, lands):
    n = len(pbs)

    def phases(ins, outs, sems):
        send, recv = sems
        x, y, c = _mesh_pos()
        me = 2 * x + y

        def copy(w, j):
            t = (me + 1 + j) % N_SHARD
            return _remote(ins[w].at[t], outs[w].at[me], send.at[3 * w + j], recv.at[3 * w + j], (t // 2, t % 2, c))

        def start():
            for w in range(n):
                for j in range(3):
                    copy(w, j).start()

        def finish():
            for w in range(n):
                for j in range(3):
                    copy(w, j).wait()

        return [start, finish]

    return _Payload(list(pbs) + list(lands), [jax.ShapeDtypeStruct(p.shape, BF16) for p in lands],
                    {n + i: i for i in range(n)},
                    [pltpu.SemaphoreType.DMA((3 * n,)), pltpu.SemaphoreType.DMA((3 * n,))], phases)


def _pair_add(g, rv, core, chip, name):
    _, r, ncol = g.shape
    rh = r // 2

    def body(c_ref, me_ref, g_ref, rv_ref, pf_ref, pb_ref, land_ref):
        s = g_ref[0] + rv_ref[0]
        sb = s.astype(BF16)
        pb_ref[0] = sb

        @pl.when(pl.program_id(0) == me_ref[0])
        def _():
            pf_ref[...] = s
            land_ref[0] = sb

    slot = pl.BlockSpec((1, rh, ncol), lambda s, c, me: (s, 0, 0))
    grid_spec = pltpu.PrefetchScalarGridSpec(
        num_scalar_prefetch=2, grid=(N_SHARD,),
        in_specs=[pl.BlockSpec((1, rh, ncol), lambda s, c, me: (s, c[0], 0)), slot],
        out_specs=[pl.BlockSpec((rh, ncol), lambda s, c, me: (0, 0)), slot,
                   pl.BlockSpec((1, rh, ncol), lambda s, c, me: (me[0], 0, 0))])
    return pl.pallas_call(
        body, name=name, grid_spec=grid_spec,
        out_shape=[jax.ShapeDtypeStruct((rh, ncol), F32), jax.ShapeDtypeStruct((N_SHARD, rh, ncol), BF16),
                   jax.ShapeDtypeStruct((N_SHARD, rh, ncol), BF16)],
        compiler_params=_params(("arbitrary",)),
    )(core, chip, g, rv)


def _tail_reduce(grads, small, payload=None):
    n = len(grads)
    _, r, ncol = grads[0].shape
    rh = r // 2

    def body(*refs, middle=None):
        g_hbm, sm = refs[:n], refs[n]
        pf, land, sm_out = refs[n + 1:2 * n + 1], refs[2 * n + 1:3 * n + 1], refs[3 * n + 1]
        scr = refs[3 * n + 2:]
        rv, mine, sendb = scr[:n], scr[n:2 * n], scr[2 * n:3 * n]
        sm_rv, sm_sum, d_send, d_recv, load, i_send, i_recv, store = scr[3 * n:]
        x, y, c = _mesh_pos()
        me = 2 * x + y
        sib = (x, y, 1 - c)

        def pair(w):
            src = g_hbm[w].at[:, pl.ds((1 - c) * rh, rh), :] if w < n else sm
            return _remote(src, rv[w] if w < n else sm_rv, d_send.at[w], d_recv.at[w], sib)

        def chips(w, j):
            t = (me + 1 + j) % N_SHARD
            src = sendb[w].at[t] if w < n else sm_sum
            dst = land[w].at[me] if w < n else sm_out.at[me]
            return _remote(src, dst, i_send.at[3 * w + j], i_recv.at[3 * w + j], (t // 2, t % 2, c))

        loads = [pltpu.make_async_copy(g_hbm[w].at[:, pl.ds(c * rh, rh), :], mine[w], load.at[w]) for w in range(n)]
        for w in range(n + 1):
            pair(w).start()
        for cp in loads:
            cp.start()
        stores = []
        for w in range(n):
            loads[w].wait()
            pair(w).wait_recv()
            for k in range(N_SHARD):
                s = mine[w][k] + rv[w][k]
                mine[w][k] = s
                sendb[w][k] = s.astype(BF16)
            stores += [pltpu.make_async_copy(mine[w].at[me], pf[w], store.at[2 * w]),
                       pltpu.make_async_copy(sendb[w].at[me], land[w].at[me], store.at[2 * w + 1])]
            for cp in stores[-2:]:
                cp.start()
            for j in range(3):
                chips(w, j).start()
        pair(n).wait_recv()
        if middle is not None:
            middle()
        sm_sum[...] = sm[...] + sm_rv[...]
        stores.append(pltpu.make_async_copy(sm_sum, sm_out.at[me], store.at[2 * n]))
        stores[-1].start()
        for j in range(3):
            chips(n, j).start()
        for w in range(n + 1):
            pair(w).wait_send()
            for j in range(3):
                chips(w, j).wait()
        for cp in stores:
            cp.wait()

    half = (N_SHARD, rh, ncol)
    return _call(
        body, payload, name="tail_reduce", grid=None, takes_middle=True,
        in_specs=[_ANY] * n + [_VMEM], out_specs=[_ANY] * (2 * n + 1),
        out_shape=([jax.ShapeDtypeStruct((rh, ncol), F32)] * n + [jax.ShapeDtypeStruct(half, BF16)] * n
                   + [jax.ShapeDtypeStruct((N_SHARD,) + small.shape, F32)]),
        scratch_shapes=([pltpu.VMEM(half, F32)] * (2 * n) + [pltpu.VMEM(half, BF16)] * n
                        + [pltpu.VMEM(small.shape, F32), pltpu.VMEM(small.shape, F32),
                           pltpu.SemaphoreType.DMA((n + 1,)), pltpu.SemaphoreType.DMA((n + 1,)),
                           pltpu.SemaphoreType.DMA((n,)), pltpu.SemaphoreType.DMA((3 * n + 3,)),
                           pltpu.SemaphoreType.DMA((3 * n + 3,)), pltpu.SemaphoreType.DMA((2 * n + 1,))]),
        operands=(*grads, small))


def _final_sum(pf, land, chip, core, name):
    _, rh, ncol = land.shape

    def body(me_ref, c_ref, pf_ref, land_ref, o_ref):
        me = me_ref[0]
        acc = jnp.zeros((rh, ncol), F32)
        for k in range(N_SHARD):
            acc = acc + jnp.where(me == k, pf_ref[...], land_ref[k].astype(F32))
        o_ref[...] = acc

    grid_spec = pltpu.PrefetchScalarGridSpec(
        num_scalar_prefetch=2, grid=(1,),
        in_specs=[pl.BlockSpec((rh, ncol), lambda i, me, c: (0, 0)),
                  pl.BlockSpec((N_SHARD, rh, ncol), lambda i, me, c: (0, 0, 0))],
        out_specs=pl.BlockSpec((rh, ncol), lambda i, me, c: (c[0], 0)))
    return pl.pallas_call(
        body, name=name, grid_spec=grid_spec, out_shape=jax.ShapeDtypeStruct((2 * rh, ncol), F32),
        compiler_params=_params(("arbitrary",)),
    )(chip, core, pf, land)


def _ss_payload(fulls):
    n = len(fulls)

    def phases(_, outs, sems):
        send, recv = sems
        x, y, c = _mesh_pos()

        def copy(w):
            rh = fulls[w].shape[0] // 2
            mine = outs[w].at[pl.ds(c * rh, rh), :]
            return _remote(mine, mine, send.at[w], recv.at[w], (x, y, 1 - c))

        def start():
            for w in range(n):
                copy(w).start()

        def finish():
            for w in range(n):
                copy(w).wait()

        return [lambda: None, start, finish]

    return _Payload(fulls, [jax.ShapeDtypeStruct(f.shape, F32) for f in fulls], {i: i for i in range(n)},
                    [pltpu.SemaphoreType.DMA((n,)), pltpu.SemaphoreType.DMA((n,))], phases)


def _sibling_share(fulls, name):
    return _call(lambda: None, _ss_payload(fulls), name=name, grid=None, in_specs=[], out_specs=[], out_shape=[])


_ROW = {"rel_bias": 128, "sgu_b_s": 136, "norm_ffn1": 144, "norm_mix": 145, "norm_ffn2": 146, "norm_final": 147,
        "b_gate": 148, "sgu_ln_g": 150, "sgu_ln_b": 151}


def _pack_small(gs, loss):
    def body(ws, rel, bs, n1, nm, n2, nf, bg, lg, lb, loss_ref, o_ref):
        o_ref[...] = jnp.zeros_like(o_ref)
        o_ref[LOSS_ROW:LOSS_ROW + 1, 0:128] = loss_ref[...]
        for g in range(SGU_GROUPS):
            o_ref[0:SGU_BLOCK, g * SGU_BLOCK:(g + 1) * SGU_BLOCK] = ws[g]
        o_ref[128:136, 0:REL_PAD] = rel[...]
        o_ref[136:144, 0:SGU_BLOCK] = bs[...]
        o_ref[144:145, :] = n1[...]
        o_ref[145:146, :] = nm[...]
        o_ref[146:147, :] = n2[...]
        o_ref[147:148, :] = nf[...]
        o_ref[148:149, :] = bg[:, 0:D_MODEL]
        o_ref[149:150, :] = bg[:, D_MODEL:2 * D_MODEL]
        o_ref[150:151, 0:D_SGU] = lg[...]
        o_ref[151:152, 0:D_SGU] = lb[...]

    order = ("sgu_w_s", "rel_bias", "sgu_b_s", "norm_ffn1", "norm_mix", "norm_ffn2", "norm_final", "b_gate", "sgu_ln_g",
             "sgu_ln_b")
    return pl.pallas_call(body, name="pack_small", out_shape=jax.ShapeDtypeStruct((SMALL_ROWS, D_MODEL), F32))(
        *[gs[k] for k in order], loss)


def _adam(w, g, m, v):
    m2 = ADAM_B1 * m + (1.0 - ADAM_B1) * g
    v2 = ADAM_B2 * v + (1.0 - ADAM_B2) * (g * g)
    m_hat = m2 / (1.0 - ADAM_B1 ** ADAM_STEP)
    v_hat = v2 / (1.0 - ADAM_B2 ** ADAM_STEP)
    delta = -ADAM_LR * (m_hat / (jnp.sqrt(v_hat) + ADAM_EPS) + ADAM_WD * w)
    return delta, m2, v2


def _adam_small(sin, w, m, v):
    names = SMALL
    k = len(names)

    def body(*refs):
        sin_ref = refs[0]
        w_r, m_r, v_r = refs[1:1 + k], refs[1 + k:1 + 2 * k], refs[1 + 2 * k:1 + 3 * k]
        outs = refs[1 + 3 * k:]
        tot = sin_ref[0] + sin_ref[1] + sin_ref[2] + sin_ref[3]
        outs[4 * k][...] = tot[LOSS_ROW:LOSS_ROW + 1, 0:128]
        for i, name in enumerate(names):
            o = outs[4 * i:4 * i + 4]
            if name == "sgu_w_s":
                for gi in range(SGU_GROUPS):
                    g = tot[0:SGU_BLOCK, gi * SGU_BLOCK:(gi + 1) * SGU_BLOCK]
                    res = (g,) + _adam(w_r[i][gi], g, m_r[i][gi], v_r[i][gi])
                    for ref, val in zip(o, res):
                        ref[gi] = val
                continue
            r0 = _ROW[name]
            if name == "rel_bias":
                g = tot[r0:r0 + HEADS, 0:REL_PAD]
            elif name == "sgu_b_s":
                g = tot[r0:r0 + SGU_GROUPS, 0:SGU_BLOCK]
            elif name == "b_gate":
                g = jnp.concatenate([tot[r0:r0 + 1, :], tot[r0 + 1:r0 + 2, :]], axis=1)
            elif name in ("sgu_ln_g", "sgu_ln_b"):
                g = tot[r0:r0 + 1, 0:D_SGU]
            else:
                g = tot[r0:r0 + 1, :]
            res = (g,) + _adam(w_r[i][...], g, m_r[i][...], v_r[i][...])
            for ref, val in zip(o, res):
                ref[...] = val

    out_shape = []
    for name in names:
        out_shape += [jax.ShapeDtypeStruct(w[name].shape, F32)] * 4
    out_shape.append(jax.ShapeDtypeStruct((1, 128), F32))
    flat = pl.pallas_call(body, name="adam_small", out_shape=out_shape, compiler_params=_params())(
        sin, *[w[n] for n in names], *[m[n] for n in names], *[v[n] for n in names])
    return {name: tuple(flat[4 * i:4 * i + 4]) for i, name in enumerate(names)}, flat[4 * k]


def _adam_big(w, g, m, v, name):
    r, ncol = w.shape
    tr = 256 if r % 256 == 0 else r // 2

    def body(w_ref, g_ref, m_ref, v_ref, g2_ref, d_ref, m2_ref, v2_ref):
        gv = g_ref[...]
        g2_ref[...] = gv
        d_ref[...], m2_ref[...], v2_ref[...] = _adam(w_ref[...], gv, m_ref[...], v_ref[...])

    spec = pl.BlockSpec((tr, ncol), lambda i: (i, 0))
    return pl.pallas_call(
        body, name=name, grid=(r // tr,), in_specs=[spec] * 4, out_specs=[spec] * 4,
        out_shape=[jax.ShapeDtypeStruct(w.shape, F32)] * 4, compiler_params=_params(("arbitrary",)),
    )(w, g, m, v)


WEIGHTS = ("norm_ffn1", "ffn1_w_gate", "ffn1_w_up", "ffn1_w_down", "norm_mix", "w_in", "b_gate", "rel_bias", "sgu_ln_g",
           "sgu_ln_b", "sgu_w_s", "sgu_b_s", "w_branch_att", "w_branch_sgu", "w_out", "norm_ffn2", "ffn2_w_gate",
           "ffn2_w_up", "ffn2_w_down", "norm_final")


GATE_UP = ("ffn1_w_gate", "ffn1_w_up", "ffn2_w_gate", "ffn2_w_up")
_FFN = ("ffn1_w_gate", "ffn1_w_up", "ffn1_w_down", "ffn2_w_gate", "ffn2_w_up", "ffn2_w_down")
_CAST_GROUPS = ((_FFN, "cast_ffn"), (("w_in",), "cast_w_in"), (("w_branch_att", "w_branch_sgu"), "cast_branch"),
                (("w_out",), "cast_w_out"))


def _big_form(name, a):
    return jnp.swapaxes(a, 1, 2)[0] if name in GATE_UP else a[0]


def _big_back(name, a):
    return jnp.swapaxes(a[None], 1, 2) if name in GATE_UP else a[None]


def _small_form(name, a):
    if name == "norm_final":
        return a.reshape(1, D_MODEL)
    if name == "rel_bias":
        return jnp.pad(a[0], ((0, 0), (0, REL_PAD - N_REL)))
    if name in ("sgu_w_s", "sgu_b_s"):
        return a[0]
    return a


def _small_back(name, a, like):
    if name == "rel_bias":
        a = a[:, :N_REL]
    return a.reshape(like.shape)


def kernel(x, norm_ffn1, ffn1_w_gate, ffn1_w_up, ffn1_w_down, norm_mix, w_in, b_gate, rel_bias, sgu_ln_g, sgu_ln_b, sgu_w_s, sgu_b_s, w_branch_att, w_branch_sgu, w_out, norm_ffn2, ffn2_w_gate, ffn2_w_up, ffn2_w_down, norm_final, loss_target, m_norm_ffn1, m_ffn1_w_gate, m_ffn1_w_up, m_ffn1_w_down, m_norm_mix, m_w_in, m_b_gate, m_rel_bias, m_sgu_ln_g, m_sgu_ln_b, m_sgu_w_s, m_sgu_b_s, m_w_branch_att, m_w_branch_sgu, m_w_out, m_norm_ffn2, m_ffn2_w_gate, m_ffn2_w_up, m_ffn2_w_down, m_norm_final, v_norm_ffn1, v_ffn1_w_gate, v_ffn1_w_up, v_ffn1_w_down, v_norm_mix, v_w_in, v_b_gate, v_rel_bias, v_sgu_ln_g, v_sgu_ln_b, v_sgu_w_s, v_sgu_b_s, v_w_branch_att, v_w_branch_sgu, v_w_out, v_norm_ffn2, v_ffn2_w_gate, v_ffn2_w_up, v_ffn2_w_down, v_norm_final):
    w = dict(norm_ffn1=norm_ffn1, ffn1_w_gate=ffn1_w_gate, ffn1_w_up=ffn1_w_up, ffn1_w_down=ffn1_w_down, norm_mix=norm_mix,
             w_in=w_in, b_gate=b_gate, rel_bias=rel_bias, sgu_ln_g=sgu_ln_g, sgu_ln_b=sgu_ln_b, sgu_w_s=sgu_w_s,
             sgu_b_s=sgu_b_s, w_branch_att=w_branch_att, w_branch_sgu=w_branch_sgu, w_out=w_out, norm_ffn2=norm_ffn2,
             ffn2_w_gate=ffn2_w_gate, ffn2_w_up=ffn2_w_up, ffn2_w_down=ffn2_w_down, norm_final=norm_final)
    m = dict(norm_ffn1=m_norm_ffn1, ffn1_w_gate=m_ffn1_w_gate, ffn1_w_up=m_ffn1_w_up, ffn1_w_down=m_ffn1_w_down,
             norm_mix=m_norm_mix, w_in=m_w_in, b_gate=m_b_gate, rel_bias=m_rel_bias, sgu_ln_g=m_sgu_ln_g,
             sgu_ln_b=m_sgu_ln_b, sgu_w_s=m_sgu_w_s, sgu_b_s=m_sgu_b_s, w_branch_att=m_w_branch_att,
             w_branch_sgu=m_w_branch_sgu, w_out=m_w_out, norm_ffn2=m_norm_ffn2, ffn2_w_gate=m_ffn2_w_gate,
             ffn2_w_up=m_ffn2_w_up, ffn2_w_down=m_ffn2_w_down, norm_final=m_norm_final)
    v = dict(norm_ffn1=v_norm_ffn1, ffn1_w_gate=v_ffn1_w_gate, ffn1_w_up=v_ffn1_w_up, ffn1_w_down=v_ffn1_w_down,
             norm_mix=v_norm_mix, w_in=v_w_in, b_gate=v_b_gate, rel_bias=v_rel_bias, sgu_ln_g=v_sgu_ln_g,
             sgu_ln_b=v_sgu_ln_b, sgu_w_s=v_sgu_w_s, sgu_b_s=v_sgu_b_s, w_branch_att=v_w_branch_att,
             w_branch_sgu=v_w_branch_sgu, w_out=v_w_out, norm_ffn2=v_norm_ffn2, ffn2_w_gate=v_ffn2_w_gate,
             ffn2_w_up=v_ffn2_w_up, ffn2_w_down=v_ffn2_w_down, norm_final=v_norm_final)

    core = lax.axis_index("c").astype(jnp.int32).reshape(1)
    chip = (2 * lax.axis_index("x") + lax.axis_index("y")).astype(jnp.int32).reshape(1)

    wk = {n: _big_form(n, w[n]) for n in BIG}
    slots = {}
    for names, call in _CAST_GROUPS:
        slots.update(zip(names, _cast_slots([wk[n] for n in names], chip, call)))
    ws = {n: _small_form(n, w[n]) for n in SMALL}
    _, gx, shard_grads, small_sums = _local_step(x[0], loss_target[0], slots, ws, (core, chip))

    small, loss = _adam_small(small_sums, ws, {n: _small_form(n, m[n]) for n in SMALL},
                              {n: _small_form(n, v[n]) for n in SMALL})
    grad, delta, new_m, new_v = {}, {}, {}, {}
    for n in SMALL:
        grad[n], delta[n], new_m[n], new_v[n] = (_small_back(n, a, w[n]) for a in small[n])
    for n in BIG:
        g2, d2, m2, v2 = _adam_big(wk[n], shard_grads[n], _big_form(n, m[n]), _big_form(n, v[n]), "adam_" + n)
        grad[n], delta[n], new_m[n], new_v[n] = (_big_back(n, a) for a in (g2, d2, m2, v2))

    return (loss[0, 0], gx.reshape(x.shape), *[grad[n] for n in WEIGHTS], *[delta[n] for n in WEIGHTS],
            *[new_m[n] for n in WEIGHTS], *[new_v[n] for n in WEIGHTS])
```

```python
import functools

import jax
import jax.numpy as jnp
from jax import lax
from jax.experimental import pallas as pl
from jax.experimental.pallas import tpu as pltpu

F32 = jnp.float32
BF16 = jnp.bfloat16

D_MODEL = 1024
N_SHARD = 4
D_FF = 2816
FF_S = D_FF // N_SHARD
D_ATT = 512
D_SGU = 512
D_IN = 3 * D_ATT + 2 * D_SGU + 2 * D_MODEL
IN_S = D_IN // N_SHARD
BR_S = D_MODEL // N_SHARD
HEADS = 8
HEAD_DIM = 64
CHUNK = 64
N_LEFT = 8
BAND = (N_LEFT + 1) * CHUNK
REL_CLIP = 256
N_REL = 2 * REL_CLIP + 1
REL_PAD = 640
SGU_BLOCK = 128
SGU_GROUPS = 8
SGU_GDIM = 64
EPS = 1e-6
NEG_INF = -1e30

ATT_ROWS = 2 * CHUNK
ATT_KEYS = BAND + CHUNK
ATT_PAD = N_LEFT * CHUNK

ADAM_LR = 0.001
ADAM_B1 = 0.9
ADAM_B2 = 0.999
ADAM_EPS = 1e-08
ADAM_WD = 0.01
ADAM_STEP = 10

TM = 256
TW = 1024
DGRAD_ROWS = 64
VMEM_LIMIT = 56 * 1024 * 1024

SMALL_ROWS = 160
LOSS_ROW = 152
MESH = pl.DeviceIdType.MESH

_NT = (((1,), (1,)), ((), ()))
_TN = (((0,), (0,)), ((), ()))


def _params(sem=None):
    return pltpu.CompilerParams(dimension_semantics=sem, vmem_limit_bytes=VMEM_LIMIT)


def _const_spec(shape):
    nd = len(shape)
    return pl.BlockSpec(shape, lambda *_: (0,) * nd, pipeline_mode=pl.Buffered(1))


def _acc_spec(shape):
    nd = len(shape)
    return pl.BlockSpec(shape, lambda *_: (0,) * nd)


def _row_spec(tm, ncols, off=0):
    return pl.BlockSpec((tm, ncols), lambda i: (i + off, 0))


def _row3_spec(tm, ncols):
    return pl.BlockSpec((N_SHARD, tm, ncols), lambda i: (0, i, 0))


def _dot(a, b):
    return jnp.dot(a, b, preferred_element_type=F32)


def _dot_nt(a, b):
    return lax.dot_general(a, b, _NT, preferred_element_type=F32)


def _dot_tn(a, b):
    return lax.dot_general(a, b, _TN, preferred_element_type=F32)


def _rms_fwd(x, g):
    r = lax.rsqrt(jnp.mean(x * x, axis=-1, keepdims=True) + EPS)
    xhat = x * r
    return xhat, r, xhat * g


def _rms_bwd(dh, xhat, r, g):
    dxhat = dh * g
    dx = r * (dxhat - xhat * jnp.mean(dxhat * xhat, axis=-1, keepdims=True))
    dg = jnp.sum(dh * xhat, axis=0, keepdims=True)
    return dx, dg


def _sigmoid(x):
    return 1.0 / (1.0 + jnp.exp(-x))


def _edges(n_steps):
    return [(0, True), (n_steps - 1, False)]


def _ffn_fwd(x, g, wg, wu, wd, name, payload=None, head=None):
    T = x.shape[0]

    def body(x_ref, g_ref, wg_ref, wu_ref, wd_ref, *rest):
        if head:
            t_ref, gf_ref, xo_ref, h_ref, a_ref, b_ref, loss_ref, dgf_ref = rest
        else:
            xo_ref, h_ref, a_ref, b_ref = rest
        xv = x_ref[...]
        hb = _rms_fwd(xv, g_ref[...])[2].astype(BF16)
        h_ref[...] = hb
        acc = jnp.zeros((TM, D_MODEL), F32)
        for s in range(N_SHARD):
            a = _dot_nt(hb, wg_ref[s])
            b = _dot_nt(hb, wu_ref[s])
            a_ref[s] = a.astype(BF16)
            b_ref[s] = b.astype(BF16)
            sv = a * _sigmoid(a) * b
            acc += _dot(sv.astype(BF16), wd_ref[s])
        xo = xv + 0.5 * acc
        if not head:
            xo_ref[...] = xo
            return

        @pl.when(pl.program_id(0) == 0)
        def _():
            loss_ref[...] = jnp.zeros_like(loss_ref)
            dgf_ref[...] = jnp.zeros_like(dgf_ref)

        gf = gf_ref[...]
        xhat, r, y = _rms_fwd(xo, gf)
        err = y - t_ref[...]
        loss_ref[...] += 0.5 * jnp.sum(jnp.mean(err * err, axis=-1, keepdims=True), axis=0, keepdims=True)
        dxn, dgf = _rms_bwd(err * (1.0 / D_MODEL), xhat, r, gf)
        xo_ref[...] = dxn
        dgf_ref[...] += dgf

    tok = jax.ShapeDtypeStruct((T, D_MODEL), F32)
    act = jax.ShapeDtypeStruct((N_SHARD, T, FF_S), BF16)
    return _call(
        body, payload, name=name, grid=(T // TM,), when=_edges(T // TM), sem=("arbitrary",),
        in_specs=[_row_spec(TM, D_MODEL), _const_spec((1, D_MODEL)), _const_spec(wg.shape), _const_spec(wu.shape),
                  _const_spec(wd.shape)] + ([_row_spec(TM, D_MODEL), _const_spec((1, D_MODEL))] if head else []),
        out_specs=[_row_spec(TM, D_MODEL), _row_spec(TM, D_MODEL), _row3_spec(TM, FF_S), _row3_spec(TM, FF_S)]
        + ([_acc_spec((1, 128)), _acc_spec((1, D_MODEL))] if head else []),
        out_shape=[tok, jax.ShapeDtypeStruct((T, D_MODEL), BF16), act, act]
        + ([jax.ShapeDtypeStruct((1, 128), F32), jax.ShapeDtypeStruct((1, D_MODEL), F32)] if head else []),
        operands=(x, g, wg, wu, wd) + (tuple(head) if head else ()))


def _ffn_dgrad(dout, x, a, b, g, wg, wu, wd, name, payload=None):
    T = x.shape[0]

    def body(do_ref, x_ref, a_ref, b_ref, g_ref, wg_ref, wu_ref, wd_ref, dx_ref, da_ref, db_ref, dg_ref):
        do = do_ref[...]
        dob = (0.5 * do).astype(BF16)
        dh = jnp.zeros((TM, D_MODEL), F32)
        ds_next = _dot_nt(dob, wd_ref[0])
        for s in range(N_SHARD):
            ds = ds_next
            if s + 1 < N_SHARD:
                ds_next = _dot_nt(dob, wd_ref[s + 1])
            for r0 in range(0, TM, DGRAD_ROWS):
                rows = slice(r0, r0 + DGRAD_ROWS)
                av = a_ref[s, rows, :].astype(F32)
                bv = b_ref[s, rows, :].astype(F32)
                sig = _sigmoid(av)
                dsr = ds[rows]
                da_ref[s, rows, :] = (dsr * bv * (sig * (1.0 + av * (1.0 - sig)))).astype(BF16)
                db_ref[s, rows, :] = (dsr * (av * sig)).astype(BF16)
            dh += _dot(da_ref[s], wg_ref[s]) + _dot(db_ref[s], wu_ref[s])
        gv = g_ref[...]
        xhat, r, _ = _rms_fwd(x_ref[...], gv)
        dxn, dg = _rms_bwd(dh, xhat, r, gv)
        dx_ref[...] = do + dxn

        @pl.when(pl.program_id(0) == 0)
        def _():
            dg_ref[...] = jnp.zeros_like(dg_ref)

        dg_ref[...] += dg

    return _call(
        body, payload, name=name, grid=(T // TM,), when=_edges(T // TM), sem=("arbitrary",),
        in_specs=[_row_spec(TM, D_MODEL), _row_spec(TM, D_MODEL), _row3_spec(TM, FF_S), _row3_spec(TM, FF_S),
                  _const_spec((1, D_MODEL)), _const_spec(wg.shape), _const_spec(wu.shape), _const_spec(wd.shape)],
        out_specs=[_row_spec(TM, D_MODEL), _row3_spec(TM, FF_S), _row3_spec(TM, FF_S), _acc_spec((1, D_MODEL))],
        out_shape=[jax.ShapeDtypeStruct((T, D_MODEL), F32), jax.ShapeDtypeStruct((N_SHARD, T, FF_S), BF16),
                   jax.ShapeDtypeStruct((N_SHARD, T, FF_S), BF16), jax.ShapeDtypeStruct((1, D_MODEL), F32)],
        operands=(dout, x, a, b, g, wg, wu, wd))


def _ffn_wgrad(h, dout, a, b, da, db, name, payload=None):
    T = h.shape[0]

    def body(h_ref, do_ref, a_ref, b_ref, da_ref, db_ref, gwg_ref, gwu_ref, gwd_ref):
        @pl.when(pl.program_id(1) == 0)
        def _():
            gwg_ref[...] = jnp.zeros_like(gwg_ref)
            gwu_ref[...] = jnp.zeros_like(gwu_ref)
            gwd_ref[...] = jnp.zeros_like(gwd_ref)

        hv = h_ref[...]
        gwg_ref[0] += _dot_tn(da_ref[0], hv)
        gwu_ref[0] += _dot_tn(db_ref[0], hv)
        dob = do_ref[...].astype(BF16)
        av = a_ref[0].astype(F32)
        sv = (0.5 * av * _sigmoid(av) * b_ref[0].astype(F32)).astype(BF16)
        gwd_ref[0] += _dot_tn(sv, dob)

    tw = min(TW, T)
    tok = pl.BlockSpec((tw, D_MODEL), lambda s, i: (i, 0))
    act = pl.BlockSpec((1, tw, FF_S), lambda s, i: (s, i, 0))
    return _call(
        body, payload, name=name, grid=(N_SHARD, T // tw), when=_edges(N_SHARD * (T // tw)),
        sem=("arbitrary", "arbitrary"),
        in_specs=[tok, tok, act, act, act, act],
        out_specs=[pl.BlockSpec((1, FF_S, D_MODEL), lambda s, i: (s, 0, 0))] * 3,
        out_shape=[jax.ShapeDtypeStruct((N_SHARD, FF_S, D_MODEL), F32)] * 3,
        operands=(h, dout, a, b, da, db))


def _in_fwd(x, g, w_in, lng, lnb, w_s, bst, payload=None):
    T = x.shape[0]

    def body(x_ref, g_ref, w_ref, lng_ref, lnb_ref, ws_ref, bst_ref, h_ref, qkv_ref, zs_ref, gl_ref, y_ref):
        hb = _rms_fwd(x_ref[...], g_ref[...])[2].astype(BF16)
        h_ref[...] = hb
        z0 = _dot(hb, w_ref[0])
        qkv_ref[:, 0:IN_S] = z0.astype(BF16)
        z1 = _dot(hb, w_ref[1])
        qkv_ref[:, IN_S:3 * D_ATT] = z1[:, 0:384].astype(BF16)
        z2 = _dot(hb, w_ref[2])
        zs = jnp.concatenate([z1[:, 384:IN_S], z2[:, 0:256]], axis=1)
        zs_ref[...] = zs
        gl_ref[:, 0:896] = z2[:, 256:IN_S]
        gl_ref[:, 896:2048] = _dot(hb, w_ref[3])
        _, u, _, _, vn = _sgu_norm(zs, lng_ref[...], lnb_ref[...])
        gm = _group_masks()
        bst_v = bst_ref[...]
        for n in range(TM // SGU_BLOCK):
            rows = slice(n * SGU_BLOCK, (n + 1) * SGU_BLOCK)
            y_ref[rows, :] = (u[rows] * _sgu_mix(vn[rows], ws_ref, bst_v, gm)).astype(BF16)

    return _call(
        body, payload, name="in_fwd", grid=(T // TM,), when=_edges(T // TM), sem=("arbitrary",),
        in_specs=[_row_spec(TM, D_MODEL), _const_spec((1, D_MODEL)), _const_spec(w_in.shape), _const_spec((1, D_SGU)),
                  _const_spec((1, D_SGU)), _const_spec(w_s.shape), _const_spec(bst.shape)],
        out_specs=[_row_spec(TM, D_MODEL), _row_spec(TM, 3 * D_ATT), _row_spec(TM, 2 * D_SGU), _row_spec(TM, 2 * D_MODEL),
                   _row_spec(TM, D_SGU)],
        out_shape=[jax.ShapeDtypeStruct((T, D_MODEL), BF16), jax.ShapeDtypeStruct((T, 3 * D_ATT), BF16),
                   jax.ShapeDtypeStruct((T, 2 * D_SGU), F32), jax.ShapeDtypeStruct((T, 2 * D_MODEL), F32),
                   jax.ShapeDtypeStruct((T, D_SGU), BF16)],
        operands=(x, g, w_in, lng, lnb, w_s, bst))


def _in_dgrad(dx_res, x, g, w_in, dq, dk, dv, dgl, zs, dy_sgu, lng, lnb, w_s, w_st, bst):
    T = x.shape[0]

    def body(dxr_ref, x_ref, g_ref, w_ref, dq_ref, dk_ref, dv_ref, dgl_ref, zs_ref, dys_ref, lng_ref, lnb_ref, ws_ref,
             wst_ref, bst_ref, dx_ref, dz_ref, dg_ref, dw_ref, dbt_ref, dlg_ref, dlb_ref):
        @pl.when(pl.program_id(0) == 0)
        def _():
            for ref in (dg_ref, dw_ref, dbt_ref, dlg_ref, dlb_ref):
                ref[...] = jnp.zeros_like(ref)

        dzs = _sgu_bwd_tile(zs_ref[...], dys_ref[...].astype(F32), lng_ref[...], lnb_ref[...], ws_ref, wst_ref,
                            bst_ref[...], dw_ref, dbt_ref, dlg_ref, dlb_ref)
        dz = jnp.concatenate([dq_ref[...], dk_ref[...].astype(BF16), dv_ref[...].astype(BF16), dzs, dgl_ref[...]], axis=1)
        dz_ref[...] = dz
        dh = jnp.zeros((TM, D_MODEL), F32)
        for s in range(N_SHARD):
            dh += _dot_nt(dz[:, s * IN_S:(s + 1) * IN_S], w_ref[s])
        gv = g_ref[...]
        xhat, r, _ = _rms_fwd(x_ref[...], gv)
        dxn, dg = _rms_bwd(dh, xhat, r, gv)
        dx_ref[...] = dxr_ref[...] + dxn
        dg_ref[...] += dg

    pad_blocks = ATT_PAD // TM
    return pl.pallas_call(
        body, name="in_dgrad", grid=(T // TM,),
        in_specs=[_row_spec(TM, D_MODEL), _row_spec(TM, D_MODEL), _const_spec((1, D_MODEL)), _const_spec(w_in.shape),
                  _row_spec(TM, D_ATT), _row_spec(TM, D_ATT, pad_blocks), _row_spec(TM, D_ATT, pad_blocks),
                  _row_spec(TM, 2 * D_MODEL), _row_spec(TM, 2 * D_SGU), _row_spec(TM, D_SGU), _const_spec((1, D_SGU)),
                  _const_spec((1, D_SGU)), _const_spec(w_s.shape), _const_spec(w_st.shape), _const_spec(bst.shape)],
        out_specs=[_row_spec(TM, D_MODEL), _row_spec(TM, D_IN), _acc_spec((1, D_MODEL)), _acc_spec(w_s.shape),
                   _acc_spec(bst.shape), _acc_spec((1, D_SGU)), _acc_spec((1, D_SGU))],
        out_shape=[jax.ShapeDtypeStruct((T, D_MODEL), F32), jax.ShapeDtypeStruct((T, D_IN), BF16),
                   jax.ShapeDtypeStruct((1, D_MODEL), F32), jax.ShapeDtypeStruct(w_s.shape, F32),
                   jax.ShapeDtypeStruct(bst.shape, F32), jax.ShapeDtypeStruct((1, D_SGU), F32),
                   jax.ShapeDtypeStruct((1, D_SGU), F32)],
        compiler_params=_params(("arbitrary",)),
    )(dx_res, x, g, w_in, dq, dk, dv, dgl, zs, dy_sgu, lng, lnb, w_s, w_st, bst)


def _in_wgrad(h, dz):
    T = h.shape[0]

    def body(h_ref, dz_ref, gw_ref):
        @pl.when(pl.program_id(1) == 0)
        def _():
            gw_ref[...] = jnp.zeros_like(gw_ref)

        gw_ref[0] += _dot_tn(h_ref[...], dz_ref[...])

    return pl.pallas_call(
        body, name="in_wgrad", grid=(N_SHARD, T // min(TW, T)),
        in_specs=[pl.BlockSpec((min(TW, T), D_MODEL), lambda s, i: (i, 0)),
                  pl.BlockSpec((min(TW, T), IN_S), lambda s, i: (i, s))],
        out_specs=pl.BlockSpec((1, D_MODEL, IN_S), lambda s, i: (s, 0, 0)),
        out_shape=jax.ShapeDtypeStruct((N_SHARD, D_MODEL, IN_S), F32),
        compiler_params=_params(("arbitrary", "arbitrary")),
    )(h, dz)


def _rel_onehot():
    r = lax.broadcasted_iota(jnp.int32, (REL_PAD, REL_PAD), 0)
    n = lax.broadcasted_iota(jnp.int32, (REL_PAD, REL_PAD), 1)
    idx = jnp.clip(BAND - 1 - n, -REL_CLIP, REL_CLIP) + REL_CLIP
    return jnp.where(r == idx, 1.0, 0.0).astype(BF16)


def _split3(v):
    p1 = v.astype(BF16)
    r1 = v - p1.astype(F32)
    p2 = r1.astype(BF16)
    p3 = (r1 - p2.astype(F32)).astype(BF16)
    return p1, p2, p3


def _relbias_fwd(tab_pad):
    def body(t_ref, o_ref):
        oh = _rel_onehot()
        acc = jnp.zeros((HEADS, REL_PAD), F32)
        for p in _split3(t_ref[...]):
            acc += _dot(p, oh)
        o_ref[...] = acc

    return pl.pallas_call(body, name="relbias_fwd", out_shape=jax.ShapeDtypeStruct((HEADS, REL_PAD), F32))(tab_pad)


def _relbias_bwd(z):
    def body(z_ref, o_ref):
        oh = _rel_onehot()
        dt2 = jnp.sum(z_ref[...], axis=1)
        acc = jnp.zeros((HEADS, REL_PAD), F32)
        for p in _split3(dt2):
            acc += _dot_nt(p, oh)
        o_ref[...] = acc

    return pl.pallas_call(body, name="relbias_bwd", out_shape=jax.ShapeDtypeStruct((HEADS, REL_PAD), F32))(z)


def _bias_blocks(t2):
    flat = jnp.tile(t2, (1, CHUNK))
    skew = flat[:, :CHUNK * (REL_PAD - 1)].reshape(HEADS, CHUNK, REL_PAD - 1)
    bias = skew[:, :, CHUNK - 1:CHUNK - 1 + BAND]
    slabs = [jnp.pad(bias, ((0, 0), (0, 0), (CHUNK * c, ATT_KEYS - BAND - CHUNK * c)), constant_values=NEG_INF)
             for c in range(2)]
    return jnp.concatenate(slabs, axis=1)


def _unskew(db2):
    out = []
    for c in range(2):
        slab = db2[:, CHUNK * c:CHUNK * (c + 1), CHUNK * c:CHUNK * c + BAND]
        y = jnp.pad(slab, ((0, 0), (0, 0), (CHUNK - 1, REL_PAD - BAND - CHUNK + 1)))
        yf = jnp.pad(y.reshape(HEADS, CHUNK * REL_PAD), ((0, 0), (0, CHUNK)))
        out.append(yf.reshape(HEADS, CHUNK, REL_PAD + 1)[:, :, :REL_PAD])
    return jnp.concatenate(out, axis=1)


def _att_load(qkv_hbm, q_s, k_s, v_s, sem, T):
    copies = [pltpu.make_async_copy(qkv_hbm.at[:, 0:D_ATT], q_s, sem.at[0]),
              pltpu.make_async_copy(qkv_hbm.at[:, D_ATT:2 * D_ATT], k_s.at[pl.ds(ATT_PAD, T), :], sem.at[1]),
              pltpu.make_async_copy(qkv_hbm.at[:, 2 * D_ATT:3 * D_ATT], v_s.at[pl.ds(ATT_PAD, T), :], sem.at[2])]
    for cp in copies:
        cp.start()
    k_s[0:ATT_PAD, :] = jnp.zeros((ATT_PAD, D_ATT), BF16)
    v_s[0:ATT_PAD, :] = jnp.zeros((ATT_PAD, D_ATT), BF16)
    for cp in copies:
        cp.wait()


def _head(v, h):
    return v[:, h * HEAD_DIM:(h + 1) * HEAD_DIM]


def _rows(v, h):
    return v[h * ATT_ROWS:(h + 1) * ATT_ROWS]


def _att_exp(qs, kw, bias_ref, valid):
    s = jnp.concatenate([_dot_nt(_head(qs, h), _head(kw, h)) + bias_ref[h] for h in range(HEADS)], axis=0)
    if valid is not None:
        s = jnp.where(valid, s, NEG_INF)
    e = jnp.exp(s - jnp.max(s, axis=-1, keepdims=True))
    return e, 1.0 / jnp.sum(e, axis=-1, keepdims=True)


def _att_blocks(T, block, keys_on_rows=False, middle=None):
    n_edge = min(ATT_PAD // ATT_ROWS, T // ATT_ROWS)
    shape, axis = ((ATT_KEYS, 1), 0) if keys_on_rows else ((1, ATT_KEYS), 1)

    def edge(i, carry):
        r0 = i * ATT_ROWS
        block(i, (lax.broadcasted_iota(jnp.int32, shape, axis) + (r0 - ATT_PAD)) >= 0)
        return carry

    def inner(i, carry):
        block(i, None)
        return carry

    n_blocks = T // ATT_ROWS
    lax.fori_loop(0, n_edge, edge, 0)
    if middle is None:
        lax.fori_loop(n_edge, n_blocks, inner, 0)
        return
    n_late = max(n_blocks - n_blocks // 4, n_edge)
    lax.fori_loop(n_edge, n_late, inner, 0)
    middle()
    lax.fori_loop(n_late, n_blocks, inner, 0)


def _att_fwd(qkv, bias2, payload=None):
    T = qkv.shape[0]

    def body(qkv_hbm, bias_ref, y_ref, q_s, k_s, v_s, sem, middle=None):
        _att_load(qkv_hbm, q_s, k_s, v_s, sem, T)

        def block(i, valid):
            r0 = pl.multiple_of(i * ATT_ROWS, ATT_ROWS)
            qs = q_s[pl.ds(r0, ATT_ROWS), :] * (HEAD_DIM ** -0.5)
            kw = k_s[pl.ds(r0, ATT_KEYS), :]
            vw = v_s[pl.ds(r0, ATT_KEYS), :]
            e, rinv = _att_exp(qs, kw, bias_ref, valid)
            eb = e.astype(BF16)
            outs = [_dot(_rows(eb, h), _head(vw, h)) * _rows(rinv, h) for h in range(HEADS)]
            y_ref[pl.ds(r0, ATT_ROWS), :] = jnp.concatenate(outs, axis=1).astype(BF16)

        _att_blocks(T, block, middle=middle)

    return _call(
        body, payload, name="att_fwd", grid=None, takes_middle=True,
        in_specs=[pl.BlockSpec(memory_space=pl.ANY), pl.BlockSpec(memory_space=pltpu.VMEM)],
        out_specs=[pl.BlockSpec(memory_space=pltpu.VMEM)],
        out_shape=[jax.ShapeDtypeStruct((T, D_ATT), BF16)],
        scratch_shapes=[pltpu.VMEM((T, D_ATT), BF16), pltpu.VMEM((T + ATT_PAD, D_ATT), BF16),
                        pltpu.VMEM((T + ATT_PAD, D_ATT), BF16), pltpu.SemaphoreType.DMA((3,))],
        operands=(qkv, bias2))


def _lanes(v, h):
    return v[:, h * ATT_ROWS:(h + 1) * ATT_ROWS]


def _att_bwd(qkv, dy, bias2t, payload=None):
    T = qkv.shape[0]

    def body(qkv_hbm, dy_ref, bias_ref, dq_ref, dk_ref, dv_ref, db_ref, q_s, k_s, v_s, sem):
        _att_load(qkv_hbm, q_s, k_s, v_s, sem, T)
        dk_ref[...] = jnp.zeros_like(dk_ref)
        dv_ref[...] = jnp.zeros_like(dv_ref)
        db_ref[...] = jnp.zeros_like(db_ref)

        def block(i, valid):
            r0 = pl.multiple_of(i * ATT_ROWS, ATT_ROWS)
            qs = q_s[pl.ds(r0, ATT_ROWS), :] * (HEAD_DIM ** -0.5)
            kw = k_s[pl.ds(r0, ATT_KEYS), :]
            vw = v_s[pl.ds(r0, ATT_KEYS), :]
            dyb = dy_ref[pl.ds(r0, ATT_ROWS), :]
            s = jnp.concatenate([_dot_nt(_head(kw, h), _head(qs, h)) + bias_ref[h] for h in range(HEADS)], axis=1)
            if valid is not None:
                s = jnp.where(valid, s, NEG_INF)
            e = jnp.exp(s - jnp.max(s, axis=0, keepdims=True))
            p = e * (1.0 / jnp.sum(e, axis=0, keepdims=True))
            dp = jnp.concatenate([_dot_nt(_head(vw, h), _head(dyb, h)) for h in range(HEADS)], axis=1)
            ds = p * (dp - jnp.sum(p * dp, axis=0, keepdims=True))
            for h in range(HEADS):
                db_ref[h] += _lanes(ds, h)
            dsb = ds.astype(BF16)
            pb = p.astype(BF16)
            dq = [_dot_tn(_lanes(dsb, h), _head(kw, h)) for h in range(HEADS)]
            dk = [_dot(_lanes(dsb, h), _head(qs, h)) for h in range(HEADS)]
            dv = [_dot(_lanes(pb, h), _head(dyb, h)) for h in range(HEADS)]
            dq_ref[pl.ds(r0, ATT_ROWS), :] = (jnp.concatenate(dq, axis=1) * (HEAD_DIM ** -0.5)).astype(BF16)
            dk_ref[pl.ds(r0, ATT_KEYS), :] += jnp.concatenate(dk, axis=1)
            dv_ref[pl.ds(r0, ATT_KEYS), :] += jnp.concatenate(dv, axis=1)

        _att_blocks(T, block, keys_on_rows=True)

    vmem = pl.BlockSpec(memory_space=pltpu.VMEM)
    return _call(
        body, payload, name="att_bwd", grid=None,
        in_specs=[pl.BlockSpec(memory_space=pl.ANY), vmem, vmem],
        out_specs=[vmem, vmem, vmem, vmem],
        out_shape=[jax.ShapeDtypeStruct((T, D_ATT), BF16), jax.ShapeDtypeStruct((T + ATT_PAD, D_ATT), F32),
                   jax.ShapeDtypeStruct((T + ATT_PAD, D_ATT), F32), jax.ShapeDtypeStruct((HEADS, ATT_KEYS, ATT_ROWS), F32)],
        scratch_shapes=[pltpu.VMEM((T, D_ATT), BF16), pltpu.VMEM((T + ATT_PAD, D_ATT), BF16),
                        pltpu.VMEM((T + ATT_PAD, D_ATT), BF16), pltpu.SemaphoreType.DMA((3,))],
        operands=(qkv, dy, bias2t))


_GELU_C = 0.7978845608028654
_GELU_A = 0.044715


def _gelu(x):
    t = jnp.tanh(_GELU_C * (x + _GELU_A * x * x * x))
    return 0.5 * x * (1.0 + t), t


def _gelu_grad(x, t):
    return 0.5 * (1.0 + t) + 0.5 * x * (1.0 - t * t) * _GELU_C * (1.0 + 3.0 * _GELU_A * x * x)


def _group_masks():
    col = lax.broadcasted_iota(jnp.int32, (SGU_GROUPS, D_SGU), 1) // SGU_GDIM
    grp = lax.broadcasted_iota(jnp.int32, (SGU_GROUPS, D_SGU), 0)
    return jnp.where(col == grp, 1.0, 0.0).astype(F32)


def _causal_mask(transposed=False):
    i = lax.broadcasted_iota(jnp.int32, (SGU_BLOCK, SGU_BLOCK), 0) // CHUNK
    j = lax.broadcasted_iota(jnp.int32, (SGU_BLOCK, SGU_BLOCK), 1) // CHUNK
    return (j >= i) if transposed else (i >= j)


def _sgu_norm(zs, lng, lnb):
    gz, t = _gelu(zs)
    u = gz[:, 0:D_SGU]
    vs = gz[:, D_SGU:2 * D_SGU]
    xc = vs - jnp.mean(vs, axis=-1, keepdims=True)
    rstd = lax.rsqrt(jnp.mean(xc * xc, axis=-1, keepdims=True) + EPS)
    xhat = xc * rstd
    return t, u, xhat, rstd, xhat * lng + lnb


def _sgu_mix(vn_blk, w_ref, bst, gm):
    mask = _causal_mask()
    s = jnp.zeros((SGU_BLOCK, D_SGU), F32)
    for g in range(SGU_GROUPS):
        wm = jnp.where(mask, w_ref[g], 0.0).astype(BF16)
        s += _dot(wm, (vn_blk * gm[g:g + 1, :]).astype(BF16))
        s += bst[:, g:g + 1] * gm[g:g + 1, :]
    return s


def _sgu_bwd_tile(zs_v, dyv, lng_v, lnb_v, w_ref, wt_ref, bst_v, dw_ref, dbt_ref, dlg_ref, dlb_ref):
    t, u, xhat, rstd, vn = _sgu_norm(zs_v, lng_v, lnb_v)
    gm = _group_masks()
    mask = _causal_mask()
    mask_t = _causal_mask(transposed=True)
    lane8 = lax.broadcasted_iota(jnp.int32, (1, SGU_GROUPS), 1)
    du_rows, dvn_rows = [], []
    for n in range(TM // SGU_BLOCK):
        rows = slice(n * SGU_BLOCK, (n + 1) * SGU_BLOCK)
        vn_b = vn[rows]
        s = _sgu_mix(vn_b, w_ref, bst_v, gm)
        du_rows.append(dyv[rows] * s)
        dsb = dyv[rows] * u[rows]
        vnb16 = vn_b.astype(BF16)
        dvn = jnp.zeros((SGU_BLOCK, D_SGU), F32)
        dbt = jnp.zeros((SGU_BLOCK, SGU_GROUPS), F32)
        for g in range(SGU_GROUPS):
            dsg = dsb * gm[g:g + 1, :]
            dsg16 = dsg.astype(BF16)
            wmt = jnp.where(mask_t, wt_ref[g], 0.0).astype(BF16)
            dvn += _dot(wmt, dsg16)
            dw_ref[g] += jnp.where(mask, _dot_nt(dsg16, vnb16), 0.0)
            dbt += jnp.sum(dsg, axis=-1, keepdims=True) * jnp.where(lane8 == g, 1.0, 0.0)
        dbt_ref[...] += dbt
        dvn_rows.append(dvn)
    du = jnp.concatenate(du_rows, axis=0)
    dvn = jnp.concatenate(dvn_rows, axis=0)
    dlg_ref[...] += jnp.sum(dvn * xhat, axis=0, keepdims=True)
    dlb_ref[...] += jnp.sum(dvn, axis=0, keepdims=True)
    dxhat = dvn * lng_v
    dvs = rstd * (dxhat - jnp.mean(dxhat, axis=-1, keepdims=True)
                  - xhat * jnp.mean(dxhat * xhat, axis=-1, keepdims=True))
    dgz = jnp.concatenate([du, dvs], axis=1)
    return (dgz * _gelu_grad(zs_v, t)).astype(BF16)


def _cols(v, s):
    return v[:, s * BR_S:(s + 1) * BR_S]


def _merge_fwd(x, y_att, y_sgu, gl, b_gate, wba, wbs, wo, payload=None):
    T = x.shape[0]

    def body(x_ref, ya_ref, ys_ref, gl_ref, bg_ref, wba_ref, wbs_ref, wo_ref, xo_ref, m_ref, pa_ref, ps_ref):
        ya = ya_ref[...]
        ys = ys_ref[...]
        pa = jnp.concatenate([_dot(ya, wba_ref[s]) for s in range(N_SHARD)], axis=1)
        ps = jnp.concatenate([_dot(ys, wbs_ref[s]) for s in range(N_SHARD)], axis=1)
        g = _sigmoid(gl_ref[...] + bg_ref[...])
        mb = (g[:, 0:D_MODEL] * pa + g[:, D_MODEL:2 * D_MODEL] * ps).astype(BF16)
        m_ref[...] = mb
        pa_ref[...] = pa.astype(BF16)
        ps_ref[...] = ps.astype(BF16)
        acc = jnp.zeros((TM, D_MODEL), F32)
        for s in range(N_SHARD):
            acc += _dot(_cols(mb, s), wo_ref[s])
        xo_ref[...] = x_ref[...] + acc

    tokd = jax.ShapeDtypeStruct((T, D_MODEL), BF16)
    return _call(
        body, payload, name="merge_fwd", grid=(T // TM,), when=_edges(T // TM), sem=("arbitrary",),
        in_specs=[_row_spec(TM, D_MODEL), _row_spec(TM, D_ATT), _row_spec(TM, D_SGU), _row_spec(TM, 2 * D_MODEL),
                  _const_spec((1, 2 * D_MODEL)), _const_spec(wba.shape), _const_spec(wbs.shape), _const_spec(wo.shape)],
        out_specs=[_row_spec(TM, D_MODEL)] * 4,
        out_shape=[jax.ShapeDtypeStruct((T, D_MODEL), F32), tokd, tokd, tokd],
        operands=(x, y_att, y_sgu, gl, b_gate, wba, wbs, wo))


def _merge_bwd(dx, y_att, y_sgu, gl, merged, pa, ps, b_gate, wba, wbs, wo, payload=None):
    T = dx.shape[0]

    def body(dx_ref, ya_ref, ys_ref, gl_ref, m_ref, pa_ref, ps_ref, bg_ref, wba_ref, wbs_ref, wo_ref,
             dya_ref, dys_ref, dgl_ref, dbg_ref, gwba_ref, gwbs_ref, gwo_ref):
        @pl.when(pl.program_id(0) == 0)
        def _():
            dbg_ref[...] = jnp.zeros_like(dbg_ref)
            gwba_ref[...] = jnp.zeros_like(gwba_ref)
            gwbs_ref[...] = jnp.zeros_like(gwbs_ref)
            gwo_ref[...] = jnp.zeros_like(gwo_ref)

        dxb = dx_ref[...].astype(BF16)
        dm = jnp.concatenate([_dot_nt(dxb, wo_ref[s]) for s in range(N_SHARD)], axis=1)
        g = _sigmoid(gl_ref[...] + bg_ref[...])
        ga = g[:, 0:D_MODEL]
        gs = g[:, D_MODEL:2 * D_MODEL]
        dpa = (dm * ga).astype(BF16)
        dps = (dm * gs).astype(BF16)
        dgl = jnp.concatenate([dm * pa_ref[...].astype(F32) * ga * (1.0 - ga),
                               dm * ps_ref[...].astype(F32) * gs * (1.0 - gs)], axis=1)
        dgl_ref[...] = dgl.astype(BF16)
        dbg_ref[...] += jnp.sum(dgl, axis=0, keepdims=True)
        ya = ya_ref[...]
        ys = ys_ref[...]
        mb = m_ref[...]
        dya = jnp.zeros((TM, D_ATT), F32)
        dys = jnp.zeros((TM, D_SGU), F32)
        for s in range(N_SHARD):
            dya += _dot_nt(_cols(dpa, s), wba_ref[s])
            dys += _dot_nt(_cols(dps, s), wbs_ref[s])
            gwo_ref[s] += _dot_tn(_cols(mb, s), dxb)
            gwba_ref[s] += _dot_tn(ya, _cols(dpa, s))
            gwbs_ref[s] += _dot_tn(ys, _cols(dps, s))
        dya_ref[...] = dya.astype(BF16)
        dys_ref[...] = dys.astype(BF16)

    return _call(
        body, payload, name="merge_bwd", grid=(T // TM,), when=_edges(T // TM), sem=("arbitrary",),
        operands=(dx, y_att, y_sgu, gl, merged, pa, ps, b_gate, wba, wbs, wo),
        in_specs=[_row_spec(TM, D_MODEL), _row_spec(TM, D_ATT), _row_spec(TM, D_SGU), _row_spec(TM, 2 * D_MODEL),
                  _row_spec(TM, D_MODEL), _row_spec(TM, D_MODEL), _row_spec(TM, D_MODEL),
                  _const_spec((1, 2 * D_MODEL)), _const_spec(wba.shape), _const_spec(wbs.shape), _const_spec(wo.shape)],
        out_specs=[_row_spec(TM, D_ATT), _row_spec(TM, D_SGU), _row_spec(TM, 2 * D_MODEL), _acc_spec((1, 2 * D_MODEL)),
                   _acc_spec(wba.shape), _acc_spec(wbs.shape), _acc_spec(wo.shape)],
        out_shape=[jax.ShapeDtypeStruct((T, D_ATT), BF16), jax.ShapeDtypeStruct((T, D_SGU), BF16),
                   jax.ShapeDtypeStruct((T, 2 * D_MODEL), BF16), jax.ShapeDtypeStruct((1, 2 * D_MODEL), F32),
                   jax.ShapeDtypeStruct(wba.shape, F32), jax.ShapeDtypeStruct(wbs.shape, F32),
                   jax.ShapeDtypeStruct(wo.shape, F32)])


BIG = ("ffn1_w_gate", "ffn1_w_up", "ffn1_w_down", "w_in", "w_branch_att", "w_branch_sgu", "w_out",
       "ffn2_w_gate", "ffn2_w_up", "ffn2_w_down")
SMALL = ("norm_ffn1", "norm_mix", "b_gate", "rel_bias", "sgu_ln_g", "sgu_ln_b", "sgu_w_s", "sgu_b_s", "norm_ffn2",
         "norm_final")


G_FFN1 = ("ffn1_w_gate", "ffn1_w_up", "ffn1_w_down")
G_MIX = ("w_in", "w_branch_att", "w_branch_sgu", "w_out")
G_FFN2 = ("ffn2_w_gate", "ffn2_w_up", "ffn2_w_down")


def _local_step(x, target, wb, ws, dist=None):
    def gather_on(names):
        return _ag_payload([wb[n] for n in names]) if dist else None

    t2 = _relbias_fwd(ws["rel_bias"])
    bias2 = _bias_blocks(t2)
    bst = ws["sgu_b_s"].T
    w_st = jnp.swapaxes(ws["sgu_w_s"], 1, 2)

    if dist:
        wb.update(zip(G_FFN1, _call(lambda: None, gather_on(G_FFN1), name="allgather_ffn1", grid=None, in_specs=[],
                                    out_specs=[], out_shape=[])))
    x1, h1, a1, b1, *got = _ffn_fwd(x, ws["norm_ffn1"], wb["ffn1_w_gate"], wb["ffn1_w_up"], wb["ffn1_w_down"],
                                    "ffn1_fwd", gather_on(G_MIX))
    wb.update(zip(G_MIX, got))
    h2, qkv, zs, gl, y_sgu, *got = _in_fwd(x1, ws["norm_mix"], wb["w_in"], ws["sgu_ln_g"], ws["sgu_ln_b"],
                                           ws["sgu_w_s"], bst, gather_on(G_FFN2[0:2]))
    wb.update(zip(G_FFN2[0:2], got))
    (y_att,) = _att_fwd(qkv, bias2)
    x2, merged, pa, ps, *got = _merge_fwd(x1, y_att, y_sgu, gl, ws["b_gate"], wb["w_branch_att"], wb["w_branch_sgu"],
                                          wb["w_out"], gather_on(G_FFN2[2:3]))
    wb.update(zip(G_FFN2[2:3], got))
    dx3, h3, a3, b3, loss, g_final = _ffn_fwd(x2, ws["norm_ffn2"], wb["ffn2_w_gate"], wb["ffn2_w_up"],
                                              wb["ffn2_w_down"], "ffn2_fwd", head=(target, ws["norm_final"]))

    gb, gs, sums = {}, {"norm_final": g_final}, {}

    def pair_on(names, small=None):
        return _px_payload([gb[n] for n in names], small) if dist else None

    def pair_add(names, halves):
        for n, rv in zip(names, halves):
            sums[n] = _pair_add(gb[n], rv, dist[0], dist[1], "pair_add_" + n)

    def chips_on(names):
        return _cx_payload([sums[n][1] for n in names], [sums[n][2] for n in names]) if dist else None

    dx2, da3, db3, gs["norm_ffn2"] = _ffn_dgrad(dx3, x2, a3, b3, ws["norm_ffn2"], wb["ffn2_w_gate"], wb["ffn2_w_up"],
                                                wb["ffn2_w_down"], "ffn2_dgrad")
    gb["ffn2_w_gate"], gb["ffn2_w_up"], gb["ffn2_w_down"] = _ffn_wgrad(h3, dx3, a3, b3, da3, db3, "ffn2_wgrad")
    dy_att, dy_sgu, dgl, gs["b_gate"], gb["w_branch_att"], gb["w_branch_sgu"], gb["w_out"], *got = _merge_bwd(
        dx2, y_att, y_sgu, gl, merged, pa, ps, ws["b_gate"], wb["w_branch_att"], wb["w_branch_sgu"], wb["w_out"],
        pair_on(G_FFN2))
    pair_add(G_FFN2, got)
    dq, dk, dv, db2t, *lands2 = _att_bwd(qkv, dy_att, jnp.swapaxes(bias2, 1, 2), chips_on(G_FFN2))
    gs["rel_bias"] = _relbias_bwd(_unskew(jnp.swapaxes(db2t, 1, 2)))
    dx1, dz, gs["norm_mix"], gs["sgu_w_s"], dbt, gs["sgu_ln_g"], gs["sgu_ln_b"] = _in_dgrad(
        dx2, x1, ws["norm_mix"], wb["w_in"], dq, dk, dv, dgl, zs, dy_sgu, ws["sgu_ln_g"], ws["sgu_ln_b"], ws["sgu_w_s"],
        w_st, bst)
    gs["sgu_b_s"] = dbt.T
    gb["w_in"] = _in_wgrad(h2, dz)
    gx, da1, db1, gs["norm_ffn1"], *got = _ffn_dgrad(dx1, x, a1, b1, ws["norm_ffn1"], wb["ffn1_w_gate"],
                                                    wb["ffn1_w_up"], wb["ffn1_w_down"], "ffn1_dgrad", pair_on(G_MIX))
    pair_add(G_MIX, got)
    gb["ffn1_w_gate"], gb["ffn1_w_up"], gb["ffn1_w_down"], *lands_mix = _ffn_wgrad(h1, dx1, a1, b1, da1, db1,
                                                                                   "ffn1_wgrad", chips_on(G_MIX))
    if not dist:
        return loss, gx, gb, gs

    def final_sums(names, lands):
        return [_final_sum(sums[n][0], land, dist[1], dist[0], "final_sum_" + n) for n, land in zip(names, lands)]

    early = G_FFN2 + G_MIX
    k = len(G_FFN1)
    tail = _tail_reduce([gb[n] for n in G_FFN1], _pack_small(gs, loss),
                        _ss_payload(final_sums(early, list(lands2) + list(lands_mix))))
    for i, n in enumerate(G_FFN1):
        sums[n] = (tail[i],)
    small_sums, shared = tail[2 * k], dict(zip(early, tail[2 * k + 1:]))
    shared.update(zip(G_FFN1, _sibling_share(final_sums(G_FFN1, tail[k:2 * k]), "sibling_share")))
    return loss, gx, shared, small_sums


_ANY = pl.BlockSpec(memory_space=pl.ANY)
_VMEM = pl.BlockSpec(memory_space=pltpu.VMEM)


def _mesh_pos():
    return lax.axis_index("x"), lax.axis_index("y"), lax.axis_index("c")


def _cast_slots(shards, chip, name):
    n = len(shards)
    r, ncol = shards[0].shape
    tr = r // 2

    def body(me_ref, *refs):
        for i_ref, o_ref in zip(refs[:n], refs[n:]):
            o_ref[0] = i_ref[...].astype(BF16)

    grid_spec = pltpu.PrefetchScalarGridSpec(
        num_scalar_prefetch=1, grid=(r // tr,),
        in_specs=[pl.BlockSpec((tr, ncol), lambda i, me: (i, 0))] * n,
        out_specs=[pl.BlockSpec((1, tr, ncol), lambda i, me: (me[0], i, 0))] * n)
    return pl.pallas_call(
        body, name=name, grid_spec=grid_spec,
        out_shape=[jax.ShapeDtypeStruct((N_SHARD, r, ncol), BF16)] * n,
        compiler_params=_params(("arbitrary",)),
    )(chip, *shards)


class _Payload:
    def __init__(self, arrays, out_shapes, aliases, scratch, phases):
        self.arrays = list(arrays)
        self.out_shapes = list(out_shapes)
        self.aliases = dict(aliases)
        self.scratch = list(scratch)
        self.phases = phases


def _remote(src, dst, ssem, rsem, dev):
    return pltpu.make_async_remote_copy(src_ref=src, dst_ref=dst, send_sem=ssem, recv_sem=rsem, device_id=dev,
                                        device_id_type=MESH)


def _call(body, payload, *, name, grid, in_specs, out_specs, out_shape, scratch_shapes=(), sem=None, when=None,
          operands=(), takes_middle=False):
    in_specs, out_specs, out_shape = list(in_specs), list(out_specs), list(out_shape)
    scratch_shapes = list(scratch_shapes)
    n_in, n_out, n_scr = len(in_specs), len(out_specs), len(scratch_shapes)
    kwargs = {}
    kernel = body
    if payload is not None:
        k_in, k_out = len(payload.arrays), len(payload.out_shapes)
        rank = len(grid) if grid else 0

        def kernel(*refs):
            a, b = n_in, n_in + k_in
            c, d = b + n_out, b + n_out + k_out
            e = d + n_scr
            phases = payload.phases(refs[a:b], refs[c:d], refs[e:])

            def run():
                body(*refs[:a], *refs[b:c], *refs[d:e])

            if not grid:
                phases[0]()
                if len(phases) == 3 and takes_middle:
                    body(*refs[:a], *refs[b:c], *refs[d:e], middle=phases[1])
                    phases[2]()
                    return
                run()
                for ph in phases[1:]:
                    ph()
                return
            step = pl.program_id(0)
            if rank == 2:
                step = step * grid[1] + pl.program_id(1)
            marks = list(when)
            if len(phases) == 3:
                marks = [when[0], (max(when[1][0] - 3, 0), False), when[1]]
            for ph, (at, before) in zip(phases, marks):
                if before:
                    pl.when(step == at)(ph)
            run()
            for ph, (at, before) in zip(phases, marks):
                if not before:
                    pl.when(step == at)(ph)

        in_specs += [_ANY] * k_in
        out_specs += [_ANY] * k_out
        out_shape += payload.out_shapes
        scratch_shapes += payload.scratch
        kwargs["input_output_aliases"] = {n_in + i: n_out + j for i, j in payload.aliases.items()}
        operands = tuple(operands) + tuple(payload.arrays)
    if grid:
        kwargs["grid"] = grid
    return pl.pallas_call(kernel, name=name, in_specs=in_specs, out_specs=out_specs, out_shape=out_shape,
                          scratch_shapes=scratch_shapes, compiler_params=_params(sem), **kwargs)(*operands)


def _ag_payload(slots):
    n = len(slots)

    def phases(_, refs, sems):
        send_i, recv_i, send_d, recv_d = sems
        x, y, c = _mesh_pos()
        me = 2 * x + y

        def half(w, core):
            rh = slots[w].shape[1] // 2
            return pl.ds(core * rh, rh)

        def ici(w, j):
            t = (me + 1 + j) % N_SHARD
            mine = refs[w].at[me, half(w, c), :]
            return _remote(mine, mine, send_i.at[3 * w + j], recv_i.at[3 * w + j], (t // 2, t % 2, c))

        def d2d(w, j, core):
            s = (me + 3 - j) % N_SHARD
            land = refs[w].at[s, half(w, core), :]
            return _remote(land, land, send_d.at[3 * w + j], recv_d.at[3 * w + j], (x, y, 1 - c))

        def start():
            for w in range(n):
                for j in range(3):
                    ici(w, j).start()

        def forward():
            for w in range(n):
                for j in range(3):
                    s = (me + 3 - j) % N_SHARD
                    land = refs[w].at[s, half(w, c), :]
                    _remote(land, land, send_i.at[3 * w + j], recv_i.at[3 * w + j], (x, y, c)).wait_recv()
                    d2d(w, j, c).start()

        def finish():
            for w in range(n):
                for j in range(3):
                    d2d(w, j, 1 - c).wait_recv()
            for w in range(n):
                for j in range(3):
                    ici(w, j).wait_send()
                    d2d(w, j, c).wait_send()

        return [start, forward, finish]

    return _Payload(slots, [jax.ShapeDtypeStruct(s.shape, s.dtype) for s in slots], {i: i for i in range(n)},
                    [pltpu.SemaphoreType.DMA((3 * n,)) for _ in range(4)], phases)


def _px_payload(grads, small=None):
    arrays = list(grads) + ([small] if small is not None else [])
    n = len(arrays)

    def phases(ins, outs, sems):
        send, recv = sems
        x, y, c = _mesh_pos()

        def copy(w):
            if w < len(grads):
                rh = grads[w].shape[1] // 2
                src = ins[w].at[:, pl.ds((1 - c) * rh, rh), :]
            else:
                src = ins[w]
            return _remote(src, outs[w], send.at[w], recv.at[w], (x, y, 1 - c))

        def start():
            for w in range(n):
                copy(w).start()

        def finish():
            for w in range(n):
                copy(w).wait()

        return [start, finish]

    out_shapes = [jax.ShapeDtypeStruct((N_SHARD, g.shape[1] // 2, g.shape[2]), F32) for g in grads]
    if small is not None:
        out_shapes.append(jax.ShapeDtypeStruct(small.shape, F32))
    return _Payload(arrays, out_shapes, {}, [pltpu.SemaphoreType.DMA((n,)), pltpu.SemaphoreType.DMA((n,))], phases)


def _cx_payload(pbs, lands):
    n = len(pbs)

    def phases(ins, outs, sems):
        send, recv = sems
        x, y, c = _mesh_pos()
        me = 2 * x + y

        def copy(w, j):
            t = (me + 1 + j) % N_SHARD
            return _remote(ins[w].at[t], outs[w].at[me], send.at[3 * w + j], recv.at[3 * w + j], (t // 2, t % 2, c))

        def start():
            for w in range(n):
                for j in range(3):
                    copy(w, j).start()

        def finish():
            for w in range(n):
                for j in range(3):
                    copy(w, j).wait()

        return [start, finish]

    return _Payload(list(pbs) + list(lands), [jax.ShapeDtypeStruct(p.shape, BF16) for p in lands],
                    {n + i: i for i in range(n)},
                    [pltpu.SemaphoreType.DMA((3 * n,)), pltpu.SemaphoreType.DMA((3 * n,))], phases)


def _pair_add(g, rv, core, chip, name):
    _, r, ncol = g.shape
    rh = r // 2

    def body(c_ref, me_ref, g_ref, rv_ref, pf_ref, pb_ref, land_ref):
        s = g_ref[0] + rv_ref[0]
        sb = s.astype(BF16)
        pb_ref[0] = sb

        @pl.when(pl.program_id(0) == me_ref[0])
        def _():
            pf_ref[...] = s
            land_ref[0] = sb

    slot = pl.BlockSpec((1, rh, ncol), lambda s, c, me: (s, 0, 0))
    grid_spec = pltpu.PrefetchScalarGridSpec(
        num_scalar_prefetch=2, grid=(N_SHARD,),
        in_specs=[pl.BlockSpec((1, rh, ncol), lambda s, c, me: (s, c[0], 0)), slot],
        out_specs=[pl.BlockSpec((rh, ncol), lambda s, c, me: (0, 0)), slot,
                   pl.BlockSpec((1, rh, ncol), lambda s, c, me: (me[0], 0, 0))])
    return pl.pallas_call(
        body, name=name, grid_spec=grid_spec,
        out_shape=[jax.ShapeDtypeStruct((rh, ncol), F32), jax.ShapeDtypeStruct((N_SHARD, rh, ncol), BF16),
                   jax.ShapeDtypeStruct((N_SHARD, rh, ncol), BF16)],
        compiler_params=_params(("arbitrary",)),
    )(core, chip, g, rv)


def _tail_reduce(grads, small, payload=None):
    n = len(grads)
    _, r, ncol = grads[0].shape
    rh = r // 2

    def body(*refs, middle=None):
        g_hbm, sm = refs[:n], refs[n]
        pf, land, sm_out = refs[n + 1:2 * n + 1], refs[2 * n + 1:3 * n + 1], refs[3 * n + 1]
        scr = refs[3 * n + 2:]
        rv, mine, sendb = scr[:n], scr[n:2 * n], scr[2 * n:3 * n]
        sm_rv, sm_sum, d_send, d_recv, load, i_send, i_recv, store = scr[3 * n:]
        x, y, c = _mesh_pos()
        me = 2 * x + y
        sib = (x, y, 1 - c)

        def pair(w):
            src = g_hbm[w].at[:, pl.ds((1 - c) * rh, rh), :] if w < n else sm
            return _remote(src, rv[w] if w < n else sm_rv, d_send.at[w], d_recv.at[w], sib)

        def chips(w, j):
            t = (me + 1 + j) % N_SHARD
            src = sendb[w].at[t] if w < n else sm_sum
            dst = land[w].at[me] if w < n else sm_out.at[me]
            return _remote(src, dst, i_send.at[3 * w + j], i_recv.at[3 * w + j], (t // 2, t % 2, c))

        loads = [pltpu.make_async_copy(g_hbm[w].at[:, pl.ds(c * rh, rh), :], mine[w], load.at[w]) for w in range(n)]
        for w in range(n + 1):
            pair(w).start()
        for cp in loads:
            cp.start()
        stores = []
        for w in range(n):
            loads[w].wait()
            pair(w).wait_recv()
            for k in range(N_SHARD):
                s = mine[w][k] + rv[w][k]
                mine[w][k] = s
                sendb[w][k] = s.astype(BF16)
            stores += [pltpu.make_async_copy(mine[w].at[me], pf[w], store.at[2 * w]),
                       pltpu.make_async_copy(sendb[w].at[me], land[w].at[me], store.at[2 * w + 1])]
            for cp in stores[-2:]:
                cp.start()
            for j in range(3):
                chips(w, j).start()
        pair(n).wait_recv()
        if middle is not None:
            middle()
        sm_sum[...] = sm[...] + sm_rv[...]
        stores.append(pltpu.make_async_copy(sm_sum, sm_out.at[me], store.at[2 * n]))
        stores[-1].start()
        for j in range(3):
            chips(n, j).start()
        for w in range(n + 1):
            pair(w).wait_send()
            for j in range(3):
                chips(w, j).wait()
        for cp in stores:
            cp.wait()

    half = (N_SHARD, rh, ncol)
    return _call(
        body, payload, name="tail_reduce", grid=None, takes_middle=True,
        in_specs=[_ANY] * n + [_VMEM], out_specs=[_ANY] * (2 * n + 1),
        out_shape=([jax.ShapeDtypeStruct((rh, ncol), F32)] * n + [jax.ShapeDtypeStruct(half, BF16)] * n
                   + [jax.ShapeDtypeStruct((N_SHARD,) + small.shape, F32)]),
        scratch_shapes=([pltpu.VMEM(half, F32)] * (2 * n) + [pltpu.VMEM(half, BF16)] * n
                        + [pltpu.VMEM(small.shape, F32), pltpu.VMEM(small.shape, F32),
                           pltpu.SemaphoreType.DMA((n + 1,)), pltpu.SemaphoreType.DMA((n + 1,)),
                           pltpu.SemaphoreType.DMA((n,)), pltpu.SemaphoreType.DMA((3 * n + 3,)),
                           pltpu.SemaphoreType.DMA((3 * n + 3,)), pltpu.SemaphoreType.DMA((2 * n + 1,))]),
        operands=(*grads, small))


def _final_sum(pf, land, chip, core, name):
    _, rh, ncol = land.shape

    def body(me_ref, c_ref, pf_ref, land_ref, o_ref):
        me = me_ref[0]
        acc = jnp.zeros((rh, ncol), F32)
        for k in range(N_SHARD):
            acc = acc + jnp.where(me == k, pf_ref[...], land_ref[k].astype(F32))
        o_ref[...] = acc

    grid_spec = pltpu.PrefetchScalarGridSpec(
        num_scalar_prefetch=2, grid=(1,),
        in_specs=[pl.BlockSpec((rh, ncol), lambda i, me, c: (0, 0)),
                  pl.BlockSpec((N_SHARD, rh, ncol), lambda i, me, c: (0, 0, 0))],
        out_specs=pl.BlockSpec((rh, ncol), lambda i, me, c: (c[0], 0)))
    return pl.pallas_call(
        body, name=name, grid_spec=grid_spec, out_shape=jax.ShapeDtypeStruct((2 * rh, ncol), F32),
        compiler_params=_params(("arbitrary",)),
    )(chip, core, pf, land)


def _ss_payload(fulls):
    n = len(fulls)

    def phases(_, outs, sems):
        send, recv = sems
        x, y, c = _mesh_pos()

        def copy(w):
            rh = fulls[w].shape[0] // 2
            mine = outs[w].at[pl.ds(c * rh, rh), :]
            return _remote(mine, mine, send.at[w], recv.at[w], (x, y, 1 - c))

        def start():
            for w in range(n):
                copy(w).start()

        def finish():
            for w in range(n):
                copy(w).wait()

        return [lambda: None, start, finish]

    return _Payload(fulls, [jax.ShapeDtypeStruct(f.shape, F32) for f in fulls], {i: i for i in range(n)},
                    [pltpu.SemaphoreType.DMA((n,)), pltpu.SemaphoreType.DMA((n,))], phases)


def _sibling_share(fulls, name):
    return _call(lambda: None, _ss_payload(fulls), name=name, grid=None, in_specs=[], out_specs=[], out_shape=[])


_ROW = {"rel_bias": 128, "sgu_b_s": 136, "norm_ffn1": 144, "norm_mix": 145, "norm_ffn2": 146, "norm_final": 147,
        "b_gate": 148, "sgu_ln_g": 150, "sgu_ln_b": 151}


def _pack_small(gs, loss):
    def body(ws, rel, bs, n1, nm, n2, nf, bg, lg, lb, loss_ref, o_ref):
        o_ref[...] = jnp.zeros_like(o_ref)
        o_ref[LOSS_ROW:LOSS_ROW + 1, 0:128] = loss_ref[...]
        for g in range(SGU_GROUPS):
            o_ref[0:SGU_BLOCK, g * SGU_BLOCK:(g + 1) * SGU_BLOCK] = ws[g]
        o_ref[128:136, 0:REL_PAD] = rel[...]
        o_ref[136:144, 0:SGU_BLOCK] = bs[...]
        o_ref[144:145, :] = n1[...]
        o_ref[145:146, :] = nm[...]
        o_ref[146:147, :] = n2[...]
        o_ref[147:148, :] = nf[...]
        o_ref[148:149, :] = bg[:, 0:D_MODEL]
        o_ref[149:150, :] = bg[:, D_MODEL:2 * D_MODEL]
        o_ref[150:151, 0:D_SGU] = lg[...]
        o_ref[151:152, 0:D_SGU] = lb[...]

    order = ("sgu_w_s", "rel_bias", "sgu_b_s", "norm_ffn1", "norm_mix", "norm_ffn2", "norm_final", "b_gate", "sgu_ln_g",
             "sgu_ln_b")
    return pl.pallas_call(body, name="pack_small", out_shape=jax.ShapeDtypeStruct((SMALL_ROWS, D_MODEL), F32))(
        *[gs[k] for k in order], loss)


def _adam(w, g, m, v):
    m2 = ADAM_B1 * m + (1.0 - ADAM_B1) * g
    v2 = ADAM_B2 * v + (1.0 - ADAM_B2) * (g * g)
    m_hat = m2 / (1.0 - ADAM_B1 ** ADAM_STEP)
    v_hat = v2 / (1.0 - ADAM_B2 ** ADAM_STEP)
    delta = -ADAM_LR * (m_hat / (jnp.sqrt(v_hat) + ADAM_EPS) + ADAM_WD * w)
    return delta, m2, v2


def _adam_small(sin, w, m, v):
    names = SMALL
    k = len(names)

    def body(*refs):
        sin_ref = refs[0]
        w_r, m_r, v_r = refs[1:1 + k], refs[1 + k:1 + 2 * k], refs[1 + 2 * k:1 + 3 * k]
        outs = refs[1 + 3 * k:]
        tot = sin_ref[0] + sin_ref[1] + sin_ref[2] + sin_ref[3]
        outs[4 * k][...] = tot[LOSS_ROW:LOSS_ROW + 1, 0:128]
        for i, name in enumerate(names):
            o = outs[4 * i:4 * i + 4]
            if name == "sgu_w_s":
                for gi in range(SGU_GROUPS):
                    g = tot[0:SGU_BLOCK, gi * SGU_BLOCK:(gi + 1) * SGU_BLOCK]
                    res = (g,) + _adam(w_r[i][gi], g, m_r[i][gi], v_r[i][gi])
                    for ref, val in zip(o, res):
                        ref[gi] = val
                continue
            r0 = _ROW[name]
            if name == "rel_bias":
                g = tot[r0:r0 + HEADS, 0:REL_PAD]
            elif name == "sgu_b_s":
                g = tot[r0:r0 + SGU_GROUPS, 0:SGU_BLOCK]
            elif name == "b_gate":
                g = jnp.concatenate([tot[r0:r0 + 1, :], tot[r0 + 1:r0 + 2, :]], axis=1)
            elif name in ("sgu_ln_g", "sgu_ln_b"):
                g = tot[r0:r0 + 1, 0:D_SGU]
            else:
                g = tot[r0:r0 + 1, :]
            res = (g,) + _adam(w_r[i][...], g, m_r[i][...], v_r[i][...])
            for ref, val in zip(o, res):
                ref[...] = val

    out_shape = []
    for name in names:
        out_shape += [jax.ShapeDtypeStruct(w[name].shape, F32)] * 4
    out_shape.append(jax.ShapeDtypeStruct((1, 128), F32))
    flat = pl.pallas_call(body, name="adam_small", out_shape=out_shape, compiler_params=_params())(
        sin, *[w[n] for n in names], *[m[n] for n in names], *[v[n] for n in names])
    return {name: tuple(flat[4 * i:4 * i + 4]) for i, name in enumerate(names)}, flat[4 * k]


def _adam_big(w, g, m, v, name):
    r, ncol = w.shape
    tr = 256 if r % 256 == 0 else r // 2

    def body(w_ref, g_ref, m_ref, v_ref, g2_ref, d_ref, m2_ref, v2_ref):
        gv = g_ref[...]
        g2_ref[...] = gv
        d_ref[...], m2_ref[...], v2_ref[...] = _adam(w_ref[...], gv, m_ref[...], v_ref[...])

    spec = pl.BlockSpec((tr, ncol), lambda i: (i, 0))
    return pl.pallas_call(
        body, name=name, grid=(r // tr,), in_specs=[spec] * 4, out_specs=[spec] * 4,
        out_shape=[jax.ShapeDtypeStruct(w.shape, F32)] * 4, compiler_params=_params(("arbitrary",)),
    )(w, g, m, v)


WEIGHTS = ("norm_ffn1", "ffn1_w_gate", "ffn1_w_up", "ffn1_w_down", "norm_mix", "w_in", "b_gate", "rel_bias", "sgu_ln_g",
           "sgu_ln_b", "sgu_w_s", "sgu_b_s", "w_branch_att", "w_branch_sgu", "w_out", "norm_ffn2", "ffn2_w_gate",
           "ffn2_w_up", "ffn2_w_down", "norm_final")


GATE_UP = ("ffn1_w_gate", "ffn1_w_up", "ffn2_w_gate", "ffn2_w_up")
_FFN = ("ffn1_w_gate", "ffn1_w_up", "ffn1_w_down", "ffn2_w_gate", "ffn2_w_up", "ffn2_w_down")
_CAST_GROUPS = ((_FFN, "cast_ffn"), (("w_in",), "cast_w_in"), (("w_branch_att", "w_branch_sgu"), "cast_branch"),
                (("w_out",), "cast_w_out"))


def _big_form(name, a):
    return jnp.swapaxes(a, 1, 2)[0] if name in GATE_UP else a[0]


def _big_back(name, a):
    return jnp.swapaxes(a[None], 1, 2) if name in GATE_UP else a[None]


def _small_form(name, a):
    if name == "norm_final":
        return a.reshape(1, D_MODEL)
    if name == "rel_bias":
        return jnp.pad(a[0], ((0, 0), (0, REL_PAD - N_REL)))
    if name in ("sgu_w_s", "sgu_b_s"):
        return a[0]
    return a


def _small_back(name, a, like):
    if name == "rel_bias":
        a = a[:, :N_REL]
    return a.reshape(like.shape)


def kernel(x, norm_ffn1, ffn1_w_gate, ffn1_w_up, ffn1_w_down, norm_mix, w_in, b_gate, rel_bias, sgu_ln_g, sgu_ln_b, sgu_w_s, sgu_b_s, w_branch_att, w_branch_sgu, w_out, norm_ffn2, ffn2_w_gate, ffn2_w_up, ffn2_w_down, norm_final, loss_target, m_norm_ffn1, m_ffn1_w_gate, m_ffn1_w_up, m_ffn1_w_down, m_norm_mix, m_w_in, m_b_gate, m_rel_bias, m_sgu_ln_g, m_sgu_ln_b, m_sgu_w_s, m_sgu_b_s, m_w_branch_att, m_w_branch_sgu, m_w_out, m_norm_ffn2, m_ffn2_w_gate, m_ffn2_w_up, m_ffn2_w_down, m_norm_final, v_norm_ffn1, v_ffn1_w_gate, v_ffn1_w_up, v_ffn1_w_down, v_norm_mix, v_w_in, v_b_gate, v_rel_bias, v_sgu_ln_g, v_sgu_ln_b, v_sgu_w_s, v_sgu_b_s, v_w_branch_att, v_w_branch_sgu, v_w_out, v_norm_ffn2, v_ffn2_w_gate, v_ffn2_w_up, v_ffn2_w_down, v_norm_final):
    w = dict(norm_ffn1=norm_ffn1, ffn1_w_gate=ffn1_w_gate, ffn1_w_up=ffn1_w_up, ffn1_w_down=ffn1_w_down, norm_mix=norm_mix,
             w_in=w_in, b_gate=b_gate, rel_bias=rel_bias, sgu_ln_g=sgu_ln_g, sgu_ln_b=sgu_ln_b, sgu_w_s=sgu_w_s,
             sgu_b_s=sgu_b_s, w_branch_att=w_branch_att, w_branch_sgu=w_branch_sgu, w_out=w_out, norm_ffn2=norm_ffn2,
             ffn2_w_gate=ffn2_w_gate, ffn2_w_up=ffn2_w_up, ffn2_w_down=ffn2_w_down, norm_final=norm_final)
    m = dict(norm_ffn1=m_norm_ffn1, ffn1_w_gate=m_ffn1_w_gate, ffn1_w_up=m_ffn1_w_up, ffn1_w_down=m_ffn1_w_down,
             norm_mix=m_norm_mix, w_in=m_w_in, b_gate=m_b_gate, rel_bias=m_rel_bias, sgu_ln_g=m_sgu_ln_g,
             sgu_ln_b=m_sgu_ln_b, sgu_w_s=m_sgu_w_s, sgu_b_s=m_sgu_b_s, w_branch_att=m_w_branch_att,
             w_branch_sgu=m_w_branch_sgu, w_out=m_w_out, norm_ffn2=m_norm_ffn2, ffn2_w_gate=m_ffn2_w_gate,
             ffn2_w_up=m_ffn2_w_up, ffn2_w_down=m_ffn2_w_down, norm_final=m_norm_final)
    v = dict(norm_ffn1=v_norm_ffn1, ffn1_w_gate=v_ffn1_w_gate, ffn1_w_up=v_ffn1_w_up, ffn1_w_down=v_ffn1_w_down,
             norm_mix=v_norm_mix, w_in=v_w_in, b_gate=v_b_gate, rel_bias=v_rel_bias, sgu_ln_g=v_sgu_ln_g,
             sgu_ln_b=v_sgu_ln_b, sgu_w_s=v_sgu_w_s, sgu_b_s=v_sgu_b_s, w_branch_att=v_w_branch_att,
             w_branch_sgu=v_w_branch_sgu, w_out=v_w_out, norm_ffn2=v_norm_ffn2, ffn2_w_gate=v_ffn2_w_gate,
             ffn2_w_up=v_ffn2_w_up, ffn2_w_down=v_ffn2_w_down, norm_final=v_norm_final)

    core = lax.axis_index("c").astype(jnp.int32).reshape(1)
    chip = (2 * lax.axis_index("x") + lax.axis_index("y")).astype(jnp.int32).reshape(1)

    wk = {n: _big_form(n, w[n]) for n in BIG}
    slots = {}
    for names, call in _CAST_GROUPS:
        slots.update(zip(names, _cast_slots([wk[n] for n in names], chip, call)))
    ws = {n: _small_form(n, w[n]) for n in SMALL}
    _, gx, shard_grads, small_sums = _local_step(x[0], loss_target[0], slots, ws, (core, chip))

    small, loss = _adam_small(small_sums, ws, {n: _small_form(n, m[n]) for n in SMALL},
                              {n: _small_form(n, v[n]) for n in SMALL})
    grad, delta, new_m, new_v = {}, {}, {}, {}
    for n in SMALL:
        grad[n], delta[n], new_m[n], new_v[n] = (_small_back(n, a, w[n]) for a in small[n])
    for n in BIG:
        g2, d2, m2, v2 = _adam_big(wk[n], shard_grads[n], _big_form(n, m[n]), _big_form(n, v[n]), "adam_" + n)
        grad[n], delta[n], new_m[n], new_v[n] = (_big_back(n, a) for a in (g2, d2, m2, v2))

    return (loss[0, 0], gx.reshape(x.shape), *[grad[n] for n in WEIGHTS], *[delta[n] for n in WEIGHTS],
            *[new_m[n] for n in WEIGHTS], *[new_v[n] for n in WEIGHTS])
```

```python
import functools

import jax
import jax.numpy as jnp
from jax import lax
from jax.experimental import pallas as pl
from jax.experimental.pallas import tpu as pltpu

F32 = jnp.float32
BF16 = jnp.bfloat16

D_MODEL = 1024
N_SHARD = 4
D_FF = 2816
FF_S = D_FF // N_SHARD
D_ATT = 512
D_SGU = 512
D_IN = 3 * D_ATT + 2 * D_SGU + 2 * D_MODEL
IN_S = D_IN // N_SHARD
BR_S = D_MODEL // N_SHARD
HEADS = 8
HEAD_DIM = 64
CHUNK = 64
N_LEFT = 8
BAND = (N_LEFT + 1) * CHUNK
REL_CLIP = 256
N_REL = 2 * REL_CLIP + 1
REL_PAD = 640
SGU_BLOCK = 128
SGU_GROUPS = 8
SGU_GDIM = 64
EPS = 1e-6
NEG_INF = -1e30

ATT_ROWS = 2 * CHUNK
ATT_KEYS = BAND + CHUNK
ATT_PAD = N_LEFT * CHUNK

ADAM_LR = 0.001
ADAM_B1 = 0.9
ADAM_B2 = 0.999
ADAM_EPS = 1e-08
ADAM_WD = 0.01
ADAM_STEP = 10

TM = 256
TMF = 512
TW = 1024
DGRAD_ROWS = 64
VMEM_LIMIT = 56 * 1024 * 1024

SMALL_ROWS = 160
LOSS_ROW = 152
MESH = pl.DeviceIdType.MESH

_NT = (((1,), (1,)), ((), ()))
_TN = (((0,), (0,)), ((), ()))


def _params(sem=None):
    return pltpu.CompilerParams(dimension_semantics=sem, vmem_limit_bytes=VMEM_LIMIT)


def _const_spec(shape):
    nd = len(shape)
    return pl.BlockSpec(shape, lambda *_: (0,) * nd, pipeline_mode=pl.Buffered(1))


def _acc_spec(shape):
    nd = len(shape)
    return pl.BlockSpec(shape, lambda *_: (0,) * nd)


def _row_spec(tm, ncols, off=0):
    return pl.BlockSpec((tm, ncols), lambda i: (i + off, 0))


def _row3_spec(tm, ncols):
    return pl.BlockSpec((N_SHARD, tm, ncols), lambda i: (0, i, 0))


def _dot(a, b):
    return jnp.dot(a, b, preferred_element_type=F32)


def _dot_nt(a, b):
    return lax.dot_general(a, b, _NT, preferred_element_type=F32)


def _dot_tn(a, b):
    return lax.dot_general(a, b, _TN, preferred_element_type=F32)


def _rms_fwd(x, g):
    r = lax.rsqrt(jnp.mean(x * x, axis=-1, keepdims=True) + EPS)
    xhat = x * r
    return xhat, r, xhat * g


def _rms_bwd(dh, xhat, r, g):
    dxhat = dh * g
    dx = r * (dxhat - xhat * jnp.mean(dxhat * xhat, axis=-1, keepdims=True))
    dg = jnp.sum(dh * xhat, axis=0, keepdims=True)
    return dx, dg


def _sigmoid(x):
    return 1.0 / (1.0 + jnp.exp(-x))


def _edges(n_steps):
    return [(0, True), (n_steps - 1, False)]


def _ffn_fwd(x, g, wg, wu, wd, name, payload=None, head=None):
    T = x.shape[0]

    def body(x_ref, g_ref, wg_ref, wu_ref, wd_ref, *rest):
        if head:
            t_ref, gf_ref, xo_ref, h_ref, a_ref, b_ref, loss_ref, dgf_ref = rest
        else:
            xo_ref, h_ref, a_ref, b_ref = rest
        xv = x_ref[...]
        hb = _rms_fwd(xv, g_ref[...])[2].astype(BF16)
        h_ref[...] = hb
        acc = jnp.zeros((TM, D_MODEL), F32)
        for s in range(N_SHARD):
            a = _dot_nt(hb, wg_ref[s])
            b = _dot_nt(hb, wu_ref[s])
            a_ref[s] = a.astype(BF16)
            b_ref[s] = b.astype(BF16)
            sv = a * _sigmoid(a) * b
            acc += _dot(sv.astype(BF16), wd_ref[s])
        xo = xv + 0.5 * acc
        if not head:
            xo_ref[...] = xo
            return

        @pl.when(pl.program_id(0) == 0)
        def _():
            loss_ref[...] = jnp.zeros_like(loss_ref)
            dgf_ref[...] = jnp.zeros_like(dgf_ref)

        gf = gf_ref[...]
        xhat, r, y = _rms_fwd(xo, gf)
        err = y - t_ref[...]
        loss_ref[...] += 0.5 * jnp.sum(jnp.mean(err * err, axis=-1, keepdims=True), axis=0, keepdims=True)
        dxn, dgf = _rms_bwd(err * (1.0 / D_MODEL), xhat, r, gf)
        xo_ref[...] = dxn
        dgf_ref[...] += dgf

    tok = jax.ShapeDtypeStruct((T, D_MODEL), F32)
    act = jax.ShapeDtypeStruct((N_SHARD, T, FF_S), BF16)
    return _call(
        body, payload, name=name, grid=(T // TM,), when=_edges(T // TM), sem=("arbitrary",),
        in_specs=[_row_spec(TM, D_MODEL), _const_spec((1, D_MODEL)), _const_spec(wg.shape), _const_spec(wu.shape),
                  _const_spec(wd.shape)] + ([_row_spec(TM, D_MODEL), _const_spec((1, D_MODEL))] if head else []),
        out_specs=[_row_spec(TM, D_MODEL), _row_spec(TM, D_MODEL), _row3_spec(TM, FF_S), _row3_spec(TM, FF_S)]
        + ([_acc_spec((1, 128)), _acc_spec((1, D_MODEL))] if head else []),
        out_shape=[tok, jax.ShapeDtypeStruct((T, D_MODEL), BF16), act, act]
        + ([jax.ShapeDtypeStruct((1, 128), F32), jax.ShapeDtypeStruct((1, D_MODEL), F32)] if head else []),
        operands=(x, g, wg, wu, wd) + (tuple(head) if head else ()))


def _ffn_dgrad(dout, x, a, b, g, wg, wu, wd, name, payload=None):
    T = x.shape[0]

    def body(do_ref, x_ref, a_ref, b_ref, g_ref, wg_ref, wu_ref, wd_ref, dx_ref, da_ref, db_ref, dg_ref):
        do = do_ref[...]
        dob = (0.5 * do).astype(BF16)
        dh = jnp.zeros((TM, D_MODEL), F32)
        ds_next = _dot_nt(dob, wd_ref[0])
        for s in range(N_SHARD):
            ds = ds_next
            if s + 1 < N_SHARD:
                ds_next = _dot_nt(dob, wd_ref[s + 1])
            for r0 in range(0, TM, DGRAD_ROWS):
                rows = slice(r0, r0 + DGRAD_ROWS)
                av = a_ref[s, rows, :].astype(F32)
                bv = b_ref[s, rows, :].astype(F32)
                sig = _sigmoid(av)
                dsr = ds[rows]
                da_ref[s, rows, :] = (dsr * bv * (sig * (1.0 + av * (1.0 - sig)))).astype(BF16)
                db_ref[s, rows, :] = (dsr * (av * sig)).astype(BF16)
            dh += _dot(da_ref[s], wg_ref[s]) + _dot(db_ref[s], wu_ref[s])
        gv = g_ref[...]
        xhat, r, _ = _rms_fwd(x_ref[...], gv)
        dxn, dg = _rms_bwd(dh, xhat, r, gv)
        dx_ref[...] = do + dxn

        @pl.when(pl.program_id(0) == 0)
        def _():
            dg_ref[...] = jnp.zeros_like(dg_ref)

        dg_ref[...] += dg

    return _call(
        body, payload, name=name, grid=(T // TM,), when=_edges(T // TM), sem=("arbitrary",),
        in_specs=[_row_spec(TM, D_MODEL), _row_spec(TM, D_MODEL), _row3_spec(TM, FF_S), _row3_spec(TM, FF_S),
                  _const_spec((1, D_MODEL)), _const_spec(wg.shape), _const_spec(wu.shape), _const_spec(wd.shape)],
        out_specs=[_row_spec(TM, D_MODEL), _row3_spec(TM, FF_S), _row3_spec(TM, FF_S), _acc_spec((1, D_MODEL))],
        out_shape=[jax.ShapeDtypeStruct((T, D_MODEL), F32), jax.ShapeDtypeStruct((N_SHARD, T, FF_S), BF16),
                   jax.ShapeDtypeStruct((N_SHARD, T, FF_S), BF16), jax.ShapeDtypeStruct((1, D_MODEL), F32)],
        operands=(dout, x, a, b, g, wg, wu, wd))


def _ffn_wgrad(h, dout, a, b, da, db, name, payload=None):
    T = h.shape[0]

    def body(h_ref, do_ref, a_ref, b_ref, da_ref, db_ref, gwg_ref, gwu_ref, gwd_ref):
        @pl.when(pl.program_id(1) == 0)
        def _():
            gwg_ref[...] = jnp.zeros_like(gwg_ref)
            gwu_ref[...] = jnp.zeros_like(gwu_ref)
            gwd_ref[...] = jnp.zeros_like(gwd_ref)

        hv = h_ref[...]
        gwg_ref[0] += _dot_tn(da_ref[0], hv)
        gwu_ref[0] += _dot_tn(db_ref[0], hv)
        dob = do_ref[...].astype(BF16)
        av = a_ref[0].astype(F32)
        sv = (0.5 * av * _sigmoid(av) * b_ref[0].astype(F32)).astype(BF16)
        gwd_ref[0] += _dot_tn(sv, dob)

    tw = min(TW, T)
    tok = pl.BlockSpec((tw, D_MODEL), lambda s, i: (i, 0))
    act = pl.BlockSpec((1, tw, FF_S), lambda s, i: (s, i, 0))
    return _call(
        body, payload, name=name, grid=(N_SHARD, T // tw), when=_edges(N_SHARD * (T // tw)),
        sem=("arbitrary", "arbitrary"),
        in_specs=[tok, tok, act, act, act, act],
        out_specs=[pl.BlockSpec((1, FF_S, D_MODEL), lambda s, i: (s, 0, 0))] * 3,
        out_shape=[jax.ShapeDtypeStruct((N_SHARD, FF_S, D_MODEL), F32)] * 3,
        operands=(h, dout, a, b, da, db))


def _in_fwd(x, g, w_in, lng, lnb, w_s, bst, payload=None):
    T = x.shape[0]

    def body(x_ref, g_ref, w_ref, lng_ref, lnb_ref, ws_ref, bst_ref, h_ref, qkv_ref, zs_ref, gl_ref, y_ref):
        hb = _rms_fwd(x_ref[...], g_ref[...])[2].astype(BF16)
        h_ref[...] = hb
        z0 = _dot(hb, w_ref[0])
        qkv_ref[:, 0:IN_S] = z0.astype(BF16)
        z1 = _dot(hb, w_ref[1])
        qkv_ref[:, IN_S:3 * D_ATT] = z1[:, 0:384].astype(BF16)
        z2 = _dot(hb, w_ref[2])
        zs = jnp.concatenate([z1[:, 384:IN_S], z2[:, 0:256]], axis=1)
        zs_ref[...] = zs
        gl_ref[:, 0:896] = z2[:, 256:IN_S]
        gl_ref[:, 896:2048] = _dot(hb, w_ref[3])
        _, u, _, _, vn = _sgu_norm(zs, lng_ref[...], lnb_ref[...])
        gm = _group_masks()
        bst_v = bst_ref[...]
        for n in range(TM // SGU_BLOCK):
            rows = slice(n * SGU_BLOCK, (n + 1) * SGU_BLOCK)
            y_ref[rows, :] = (u[rows] * _sgu_mix(vn[rows], ws_ref, bst_v, gm)).astype(BF16)

    return _call(
        body, payload, name="in_fwd", grid=(T // TM,), when=_edges(T // TM), sem=("arbitrary",),
        in_specs=[_row_spec(TM, D_MODEL), _const_spec((1, D_MODEL)), _const_spec(w_in.shape), _const_spec((1, D_SGU)),
                  _const_spec((1, D_SGU)), _const_spec(w_s.shape), _const_spec(bst.shape)],
        out_specs=[_row_spec(TM, D_MODEL), _row_spec(TM, 3 * D_ATT), _row_spec(TM, 2 * D_SGU), _row_spec(TM, 2 * D_MODEL),
                   _row_spec(TM, D_SGU)],
        out_shape=[jax.ShapeDtypeStruct((T, D_MODEL), BF16), jax.ShapeDtypeStruct((T, 3 * D_ATT), BF16),
                   jax.ShapeDtypeStruct((T, 2 * D_SGU), F32), jax.ShapeDtypeStruct((T, 2 * D_MODEL), F32),
                   jax.ShapeDtypeStruct((T, D_SGU), BF16)],
        operands=(x, g, w_in, lng, lnb, w_s, bst))


def _in_dgrad(dx_res, x, g, w_in, dq, dk, dv, dgl, zs, dy_sgu, lng, lnb, w_s, w_st, bst):
    T = x.shape[0]

    def body(dxr_ref, x_ref, g_ref, w_ref, dq_ref, dk_ref, dv_ref, dgl_ref, zs_ref, dys_ref, lng_ref, lnb_ref, ws_ref,
             wst_ref, bst_ref, dx_ref, dz_ref, dg_ref, dw_ref, dbt_ref, dlg_ref, dlb_ref):
        @pl.when(pl.program_id(0) == 0)
        def _():
            for ref in (dg_ref, dw_ref, dbt_ref, dlg_ref, dlb_ref):
                ref[...] = jnp.zeros_like(ref)

        dzs = _sgu_bwd_tile(zs_ref[...], dys_ref[...].astype(F32), lng_ref[...], lnb_ref[...], ws_ref, wst_ref,
                            bst_ref[...], dw_ref, dbt_ref, dlg_ref, dlb_ref)
        dz = jnp.concatenate([dq_ref[...], dk_ref[...].astype(BF16), dv_ref[...].astype(BF16), dzs, dgl_ref[...]], axis=1)
        dz_ref[...] = dz
        dh = jnp.zeros((TM, D_MODEL), F32)
        for s in range(N_SHARD):
            dh += _dot_nt(dz[:, s * IN_S:(s + 1) * IN_S], w_ref[s])
        gv = g_ref[...]
        xhat, r, _ = _rms_fwd(x_ref[...], gv)
        dxn, dg = _rms_bwd(dh, xhat, r, gv)
        dx_ref[...] = dxr_ref[...] + dxn
        dg_ref[...] += dg

    pad_blocks = ATT_PAD // TM
    return pl.pallas_call(
        body, name="in_dgrad", grid=(T // TM,),
        in_specs=[_row_spec(TM, D_MODEL), _row_spec(TM, D_MODEL), _const_spec((1, D_MODEL)), _const_spec(w_in.shape),
                  _row_spec(TM, D_ATT), _row_spec(TM, D_ATT, pad_blocks), _row_spec(TM, D_ATT, pad_blocks),
                  _row_spec(TM, 2 * D_MODEL), _row_spec(TM, 2 * D_SGU), _row_spec(TM, D_SGU), _const_spec((1, D_SGU)),
                  _const_spec((1, D_SGU)), _const_spec(w_s.shape), _const_spec(w_st.shape), _const_spec(bst.shape)],
        out_specs=[_row_spec(TM, D_MODEL), _row_spec(TM, D_IN), _acc_spec((1, D_MODEL)), _acc_spec(w_s.shape),
                   _acc_spec(bst.shape), _acc_spec((1, D_SGU)), _acc_spec((1, D_SGU))],
        out_shape=[jax.ShapeDtypeStruct((T, D_MODEL), F32), jax.ShapeDtypeStruct((T, D_IN), BF16),
                   jax.ShapeDtypeStruct((1, D_MODEL), F32), jax.ShapeDtypeStruct(w_s.shape, F32),
                   jax.ShapeDtypeStruct(bst.shape, F32), jax.ShapeDtypeStruct((1, D_SGU), F32),
                   jax.ShapeDtypeStruct((1, D_SGU), F32)],
        compiler_params=_params(("arbitrary",)),
    )(dx_res, x, g, w_in, dq, dk, dv, dgl, zs, dy_sgu, lng, lnb, w_s, w_st, bst)


def _in_wgrad(h, dz):
    T = h.shape[0]

    def body(h_ref, dz_ref, gw_ref):
        @pl.when(pl.program_id(1) == 0)
        def _():
            gw_ref[...] = jnp.zeros_like(gw_ref)

        gw_ref[0] += _dot_tn(h_ref[...], dz_ref[...])

    return pl.pallas_call(
        body, name="in_wgrad", grid=(N_SHARD, T // min(TW, T)),
        in_specs=[pl.BlockSpec((min(TW, T), D_MODEL), lambda s, i: (i, 0)),
                  pl.BlockSpec((min(TW, T), IN_S), lambda s, i: (i, s))],
        out_specs=pl.BlockSpec((1, D_MODEL, IN_S), lambda s, i: (s, 0, 0)),
        out_shape=jax.ShapeDtypeStruct((N_SHARD, D_MODEL, IN_S), F32),
        compiler_params=_params(("arbitrary", "arbitrary")),
    )(h, dz)


def _rel_onehot():
    r = lax.broadcasted_iota(jnp.int32, (REL_PAD, REL_PAD), 0)
    n = lax.broadcasted_iota(jnp.int32, (REL_PAD, REL_PAD), 1)
    idx = jnp.clip(BAND - 1 - n, -REL_CLIP, REL_CLIP) + REL_CLIP
    return jnp.where(r == idx, 1.0, 0.0).astype(BF16)


def _split3(v):
    p1 = v.astype(BF16)
    r1 = v - p1.astype(F32)
    p2 = r1.astype(BF16)
    p3 = (r1 - p2.astype(F32)).astype(BF16)
    return p1, p2, p3


def _relbias_fwd(tab_pad):
    def body(t_ref, o_ref):
        oh = _rel_onehot()
        acc = jnp.zeros((HEADS, REL_PAD), F32)
        for p in _split3(t_ref[...]):
            acc += _dot(p, oh)
        o_ref[...] = acc

    return pl.pallas_call(body, name="relbias_fwd", out_shape=jax.ShapeDtypeStruct((HEADS, REL_PAD), F32))(tab_pad)


def _relbias_bwd(z):
    def body(z_ref, o_ref):
        oh = _rel_onehot()
        dt2 = jnp.sum(z_ref[...], axis=1)
        acc = jnp.zeros((HEADS, REL_PAD), F32)
        for p in _split3(dt2):
            acc += _dot_nt(p, oh)
        o_ref[...] = acc

    return pl.pallas_call(body, name="relbias_bwd", out_shape=jax.ShapeDtypeStruct((HEADS, REL_PAD), F32))(z)


def _bias_blocks(t2):
    flat = jnp.tile(t2, (1, CHUNK))
    skew = flat[:, :CHUNK * (REL_PAD - 1)].reshape(HEADS, CHUNK, REL_PAD - 1)
    bias = skew[:, :, CHUNK - 1:CHUNK - 1 + BAND]
    slabs = [jnp.pad(bias, ((0, 0), (0, 0), (CHUNK * c, ATT_KEYS - BAND - CHUNK * c)), constant_values=NEG_INF)
             for c in range(2)]
    return jnp.concatenate(slabs, axis=1)


def _unskew(db2):
    out = []
    for c in range(2):
        slab = db2[:, CHUNK * c:CHUNK * (c + 1), CHUNK * c:CHUNK * c + BAND]
        y = jnp.pad(slab, ((0, 0), (0, 0), (CHUNK - 1, REL_PAD - BAND - CHUNK + 1)))
        yf = jnp.pad(y.reshape(HEADS, CHUNK * REL_PAD), ((0, 0), (0, CHUNK)))
        out.append(yf.reshape(HEADS, CHUNK, REL_PAD + 1)[:, :, :REL_PAD])
    return jnp.concatenate(out, axis=1)


def _att_load(qkv_hbm, q_s, k_s, v_s, sem, T):
    copies = [pltpu.make_async_copy(qkv_hbm.at[:, 0:D_ATT], q_s, sem.at[0]),
              pltpu.make_async_copy(qkv_hbm.at[:, D_ATT:2 * D_ATT], k_s.at[pl.ds(ATT_PAD, T), :], sem.at[1]),
              pltpu.make_async_copy(qkv_hbm.at[:, 2 * D_ATT:3 * D_ATT], v_s.at[pl.ds(ATT_PAD, T), :], sem.at[2])]
    for cp in copies:
        cp.start()
    k_s[0:ATT_PAD, :] = jnp.zeros((ATT_PAD, D_ATT), BF16)
    v_s[0:ATT_PAD, :] = jnp.zeros((ATT_PAD, D_ATT), BF16)
    for cp in copies:
        cp.wait()


def _head(v, h):
    return v[:, h * HEAD_DIM:(h + 1) * HEAD_DIM]


def _rows(v, h):
    return v[h * ATT_ROWS:(h + 1) * ATT_ROWS]


def _att_exp(qs, kw, bias_ref, valid):
    s = jnp.concatenate([_dot_nt(_head(qs, h), _head(kw, h)) + bias_ref[h] for h in range(HEADS)], axis=0)
    if valid is not None:
        s = jnp.where(valid, s, NEG_INF)
    e = jnp.exp(s - jnp.max(s, axis=-1, keepdims=True))
    return e, 1.0 / jnp.sum(e, axis=-1, keepdims=True)


def _att_blocks(T, block, keys_on_rows=False, middle=None):
    n_edge = min(ATT_PAD // ATT_ROWS, T // ATT_ROWS)
    shape, axis = ((ATT_KEYS, 1), 0) if keys_on_rows else ((1, ATT_KEYS), 1)

    def edge(i, carry):
        r0 = i * ATT_ROWS
        block(i, (lax.broadcasted_iota(jnp.int32, shape, axis) + (r0 - ATT_PAD)) >= 0)
        return carry

    def inner(i, carry):
        block(i, None)
        return carry

    n_blocks = T // ATT_ROWS
    lax.fori_loop(0, n_edge, edge, 0)
    if middle is None:
        lax.fori_loop(n_edge, n_blocks, inner, 0)
        return
    n_late = max(n_blocks - n_blocks // 4, n_edge)
    lax.fori_loop(n_edge, n_late, inner, 0)
    middle()
    lax.fori_loop(n_late, n_blocks, inner, 0)


def _att_fwd(qkv, bias2, payload=None):
    T = qkv.shape[0]

    def body(qkv_hbm, bias_ref, y_ref, q_s, k_s, v_s, sem, middle=None):
        _att_load(qkv_hbm, q_s, k_s, v_s, sem, T)

        def block(i, valid):
            r0 = pl.multiple_of(i * ATT_ROWS, ATT_ROWS)
            qs = q_s[pl.ds(r0, ATT_ROWS), :] * (HEAD_DIM ** -0.5)
            kw = k_s[pl.ds(r0, ATT_KEYS), :]
            vw = v_s[pl.ds(r0, ATT_KEYS), :]
            e, rinv = _att_exp(qs, kw, bias_ref, valid)
            eb = e.astype(BF16)
            outs = [_dot(_rows(eb, h), _head(vw, h)) * _rows(rinv, h) for h in range(HEADS)]
            y_ref[pl.ds(r0, ATT_ROWS), :] = jnp.concatenate(outs, axis=1).astype(BF16)

        _att_blocks(T, block, middle=middle)

    return _call(
        body, payload, name="att_fwd", grid=None, takes_middle=True,
        in_specs=[pl.BlockSpec(memory_space=pl.ANY), pl.BlockSpec(memory_space=pltpu.VMEM)],
        out_specs=[pl.BlockSpec(memory_space=pltpu.VMEM)],
        out_shape=[jax.ShapeDtypeStruct((T, D_ATT), BF16)],
        scratch_shapes=[pltpu.VMEM((T, D_ATT), BF16), pltpu.VMEM((T + ATT_PAD, D_ATT), BF16),
                        pltpu.VMEM((T + ATT_PAD, D_ATT), BF16), pltpu.SemaphoreType.DMA((3,))],
        operands=(qkv, bias2))


def _lanes(v, h):
    return v[:, h * ATT_ROWS:(h + 1) * ATT_ROWS]


def _att_bwd(qkv, dy, bias2t, payload=None):
    T = qkv.shape[0]

    def body(qkv_hbm, dy_ref, bias_ref, dq_ref, dk_ref, dv_ref, db_ref, q_s, k_s, v_s, sem):
        _att_load(qkv_hbm, q_s, k_s, v_s, sem, T)
        dk_ref[...] = jnp.zeros_like(dk_ref)
        dv_ref[...] = jnp.zeros_like(dv_ref)
        db_ref[...] = jnp.zeros_like(db_ref)

        def block(i, valid):
            r0 = pl.multiple_of(i * ATT_ROWS, ATT_ROWS)
            qs = q_s[pl.ds(r0, ATT_ROWS), :] * (HEAD_DIM ** -0.5)
            kw = k_s[pl.ds(r0, ATT_KEYS), :]
            vw = v_s[pl.ds(r0, ATT_KEYS), :]
            dyb = dy_ref[pl.ds(r0, ATT_ROWS), :]
            s = jnp.concatenate([_dot_nt(_head(kw, h), _head(qs, h)) + bias_ref[h] for h in range(HEADS)], axis=1)
            if valid is not None:
                s = jnp.where(valid, s, NEG_INF)
            e = jnp.exp(s - jnp.max(s, axis=0, keepdims=True))
            p = e * (1.0 / jnp.sum(e, axis=0, keepdims=True))
            dp = jnp.concatenate([_dot_nt(_head(vw, h), _head(dyb, h)) for h in range(HEADS)], axis=1)
            ds = p * (dp - jnp.sum(p * dp, axis=0, keepdims=True))
            for h in range(HEADS):
                db_ref[h] += _lanes(ds, h)
            dsb = ds.astype(BF16)
            pb = p.astype(BF16)
            dq = [_dot_tn(_lanes(dsb, h), _head(kw, h)) for h in range(HEADS)]
            dk = [_dot(_lanes(dsb, h), _head(qs, h)) for h in range(HEADS)]
            dv = [_dot(_lanes(pb, h), _head(dyb, h)) for h in range(HEADS)]
            dq_ref[pl.ds(r0, ATT_ROWS), :] = (jnp.concatenate(dq, axis=1) * (HEAD_DIM ** -0.5)).astype(BF16)
            dk_ref[pl.ds(r0, ATT_KEYS), :] += jnp.concatenate(dk, axis=1)
            dv_ref[pl.ds(r0, ATT_KEYS), :] += jnp.concatenate(dv, axis=1)

        _att_blocks(T, block, keys_on_rows=True)

    vmem = pl.BlockSpec(memory_space=pltpu.VMEM)
    return _call(
        body, payload, name="att_bwd", grid=None,
        in_specs=[pl.BlockSpec(memory_space=pl.ANY), vmem, vmem],
        out_specs=[vmem, vmem, vmem, vmem],
        out_shape=[jax.ShapeDtypeStruct((T, D_ATT), BF16), jax.ShapeDtypeStruct((T + ATT_PAD, D_ATT), F32),
                   jax.ShapeDtypeStruct((T + ATT_PAD, D_ATT), F32), jax.ShapeDtypeStruct((HEADS, ATT_KEYS, ATT_ROWS), F32)],
        scratch_shapes=[pltpu.VMEM((T, D_ATT), BF16), pltpu.VMEM((T + ATT_PAD, D_ATT), BF16),
                        pltpu.VMEM((T + ATT_PAD, D_ATT), BF16), pltpu.SemaphoreType.DMA((3,))],
        operands=(qkv, dy, bias2t))


_GELU_C = 0.7978845608028654
_GELU_A = 0.044715


def _gelu(x):
    t = jnp.tanh(_GELU_C * (x + _GELU_A * x * x * x))
    return 0.5 * x * (1.0 + t), t


def _gelu_grad(x, t):
    return 0.5 * (1.0 + t) + 0.5 * x * (1.0 - t * t) * _GELU_C * (1.0 + 3.0 * _GELU_A * x * x)


def _group_masks():
    col = lax.broadcasted_iota(jnp.int32, (SGU_GROUPS, D_SGU), 1) // SGU_GDIM
    grp = lax.broadcasted_iota(jnp.int32, (SGU_GROUPS, D_SGU), 0)
    return jnp.where(col == grp, 1.0, 0.0).astype(F32)


def _causal_mask(transposed=False):
    i = lax.broadcasted_iota(jnp.int32, (SGU_BLOCK, SGU_BLOCK), 0) // CHUNK
    j = lax.broadcasted_iota(jnp.int32, (SGU_BLOCK, SGU_BLOCK), 1) // CHUNK
    return (j >= i) if transposed else (i >= j)


def _sgu_norm(zs, lng, lnb):
    gz, t = _gelu(zs)
    u = gz[:, 0:D_SGU]
    vs = gz[:, D_SGU:2 * D_SGU]
    xc = vs - jnp.mean(vs, axis=-1, keepdims=True)
    rstd = lax.rsqrt(jnp.mean(xc * xc, axis=-1, keepdims=True) + EPS)
    xhat = xc * rstd
    return t, u, xhat, rstd, xhat * lng + lnb


def _sgu_mix(vn_blk, w_ref, bst, gm):
    mask = _causal_mask()
    s = jnp.zeros((SGU_BLOCK, D_SGU), F32)
    for g in range(SGU_GROUPS):
        wm = jnp.where(mask, w_ref[g], 0.0).astype(BF16)
        s += _dot(wm, (vn_blk * gm[g:g + 1, :]).astype(BF16))
        s += bst[:, g:g + 1] * gm[g:g + 1, :]
    return s


def _sgu_bwd_tile(zs_v, dyv, lng_v, lnb_v, w_ref, wt_ref, bst_v, dw_ref, dbt_ref, dlg_ref, dlb_ref):
    t, u, xhat, rstd, vn = _sgu_norm(zs_v, lng_v, lnb_v)
    gm = _group_masks()
    mask = _causal_mask()
    mask_t = _causal_mask(transposed=True)
    lane8 = lax.broadcasted_iota(jnp.int32, (1, SGU_GROUPS), 1)
    du_rows, dvn_rows = [], []
    for n in range(TM // SGU_BLOCK):
        rows = slice(n * SGU_BLOCK, (n + 1) * SGU_BLOCK)
        vn_b = vn[rows]
        s = _sgu_mix(vn_b, w_ref, bst_v, gm)
        du_rows.append(dyv[rows] * s)
        dsb = dyv[rows] * u[rows]
        vnb16 = vn_b.astype(BF16)
        dvn = jnp.zeros((SGU_BLOCK, D_SGU), F32)
        dbt = jnp.zeros((SGU_BLOCK, SGU_GROUPS), F32)
        for g in range(SGU_GROUPS):
            dsg = dsb * gm[g:g + 1, :]
            dsg16 = dsg.astype(BF16)
            wmt = jnp.where(mask_t, wt_ref[g], 0.0).astype(BF16)
            dvn += _dot(wmt, dsg16)
            dw_ref[g] += jnp.where(mask, _dot_nt(dsg16, vnb16), 0.0)
            dbt += jnp.sum(dsg, axis=-1, keepdims=True) * jnp.where(lane8 == g, 1.0, 0.0)
        dbt_ref[...] += dbt
        dvn_rows.append(dvn)
    du = jnp.concatenate(du_rows, axis=0)
    dvn = jnp.concatenate(dvn_rows, axis=0)
    dlg_ref[...] += jnp.sum(dvn * xhat, axis=0, keepdims=True)
    dlb_ref[...] += jnp.sum(dvn, axis=0, keepdims=True)
    dxhat = dvn * lng_v
    dvs = rstd * (dxhat - jnp.mean(dxhat, axis=-1, keepdims=True)
                  - xhat * jnp.mean(dxhat * xhat, axis=-1, keepdims=True))
    dgz = jnp.concatenate([du, dvs], axis=1)
    return (dgz * _gelu_grad(zs_v, t)).astype(BF16)


def _cols(v, s):
    return v[:, s * BR_S:(s + 1) * BR_S]


def _merge_fwd(x, y_att, y_sgu, gl, b_gate, wba, wbs, wo, payload=None):
    T = x.shape[0]

    def body(x_ref, ya_ref, ys_ref, gl_ref, bg_ref, wba_ref, wbs_ref, wo_ref, xo_ref, m_ref, pa_ref, ps_ref):
        ya = ya_ref[...]
        ys = ys_ref[...]
        pa = jnp.concatenate([_dot(ya, wba_ref[s]) for s in range(N_SHARD)], axis=1)
        ps = jnp.concatenate([_dot(ys, wbs_ref[s]) for s in range(N_SHARD)], axis=1)
        g = _sigmoid(gl_ref[...] + bg_ref[...])
        mb = (g[:, 0:D_MODEL] * pa + g[:, D_MODEL:2 * D_MODEL] * ps).astype(BF16)
        m_ref[...] = mb
        pa_ref[...] = pa.astype(BF16)
        ps_ref[...] = ps.astype(BF16)
        acc = jnp.zeros((tm, D_MODEL), F32)
        for s in range(N_SHARD):
            acc += _dot(_cols(mb, s), wo_ref[s])
        xo_ref[...] = x_ref[...] + acc

    tm = min(TMF, T)
    tokd = jax.ShapeDtypeStruct((T, D_MODEL), BF16)
    return _call(
        body, payload, name="merge_fwd", grid=(T // tm,), when=_edges(T // tm), sem=("arbitrary",),
        in_specs=[_row_spec(tm, D_MODEL), _row_spec(tm, D_ATT), _row_spec(tm, D_SGU), _row_spec(tm, 2 * D_MODEL),
                  _const_spec((1, 2 * D_MODEL)), _const_spec(wba.shape), _const_spec(wbs.shape), _const_spec(wo.shape)],
        out_specs=[_row_spec(tm, D_MODEL)] * 4,
        out_shape=[jax.ShapeDtypeStruct((T, D_MODEL), F32), tokd, tokd, tokd],
        operands=(x, y_att, y_sgu, gl, b_gate, wba, wbs, wo))


def _merge_bwd(dx, y_att, y_sgu, gl, merged, pa, ps, b_gate, wba, wbs, wo, payload=None):
    T = dx.shape[0]

    def body(dx_ref, ya_ref, ys_ref, gl_ref, m_ref, pa_ref, ps_ref, bg_ref, wba_ref, wbs_ref, wo_ref,
             dya_ref, dys_ref, dgl_ref, dbg_ref, gwba_ref, gwbs_ref, gwo_ref):
        @pl.when(pl.program_id(0) == 0)
        def _():
            dbg_ref[...] = jnp.zeros_like(dbg_ref)
            gwba_ref[...] = jnp.zeros_like(gwba_ref)
            gwbs_ref[...] = jnp.zeros_like(gwbs_ref)
            gwo_ref[...] = jnp.zeros_like(gwo_ref)

        dxb = dx_ref[...].astype(BF16)
        dm = jnp.concatenate([_dot_nt(dxb, wo_ref[s]) for s in range(N_SHARD)], axis=1)
        g = _sigmoid(gl_ref[...] + bg_ref[...])
        ga = g[:, 0:D_MODEL]
        gs = g[:, D_MODEL:2 * D_MODEL]
        dpa = (dm * ga).astype(BF16)
        dps = (dm * gs).astype(BF16)
        dgl = jnp.concatenate([dm * pa_ref[...].astype(F32) * ga * (1.0 - ga),
                               dm * ps_ref[...].astype(F32) * gs * (1.0 - gs)], axis=1)
        dgl_ref[...] = dgl.astype(BF16)
        dbg_ref[...] += jnp.sum(dgl, axis=0, keepdims=True)
        ya = ya_ref[...]
        ys = ys_ref[...]
        mb = m_ref[...]
        dya = jnp.zeros((TM, D_ATT), F32)
        dys = jnp.zeros((TM, D_SGU), F32)
        for s in range(N_SHARD):
            dya += _dot_nt(_cols(dpa, s), wba_ref[s])
            dys += _dot_nt(_cols(dps, s), wbs_ref[s])
            gwo_ref[s] += _dot_tn(_cols(mb, s), dxb)
            gwba_ref[s] += _dot_tn(ya, _cols(dpa, s))
            gwbs_ref[s] += _dot_tn(ys, _cols(dps, s))
        dya_ref[...] = dya.astype(BF16)
        dys_ref[...] = dys.astype(BF16)

    return _call(
        body, payload, name="merge_bwd", grid=(T // TM,), when=_edges(T // TM), sem=("arbitrary",),
        operands=(dx, y_att, y_sgu, gl, merged, pa, ps, b_gate, wba, wbs, wo),
        in_specs=[_row_spec(TM, D_MODEL), _row_spec(TM, D_ATT), _row_spec(TM, D_SGU), _row_spec(TM, 2 * D_MODEL),
                  _row_spec(TM, D_MODEL), _row_spec(TM, D_MODEL), _row_spec(TM, D_MODEL),
                  _const_spec((1, 2 * D_MODEL)), _const_spec(wba.shape), _const_spec(wbs.shape), _const_spec(wo.shape)],
        out_specs=[_row_spec(TM, D_ATT), _row_spec(TM, D_SGU), _row_spec(TM, 2 * D_MODEL), _acc_spec((1, 2 * D_MODEL)),
                   _acc_spec(wba.shape), _acc_spec(wbs.shape), _acc_spec(wo.shape)],
        out_shape=[jax.ShapeDtypeStruct((T, D_ATT), BF16), jax.ShapeDtypeStruct((T, D_SGU), BF16),
                   jax.ShapeDtypeStruct((T, 2 * D_MODEL), BF16), jax.ShapeDtypeStruct((1, 2 * D_MODEL), F32),
                   jax.ShapeDtypeStruct(wba.shape, F32), jax.ShapeDtypeStruct(wbs.shape, F32),
                   jax.ShapeDtypeStruct(wo.shape, F32)])


BIG = ("ffn1_w_gate", "ffn1_w_up", "ffn1_w_down", "w_in", "w_branch_att", "w_branch_sgu", "w_out",
       "ffn2_w_gate", "ffn2_w_up", "ffn2_w_down")
SMALL = ("norm_ffn1", "norm_mix", "b_gate", "rel_bias", "sgu_ln_g", "sgu_ln_b", "sgu_w_s", "sgu_b_s", "norm_ffn2",
         "norm_final")


G_FFN1 = ("ffn1_w_gate", "ffn1_w_up", "ffn1_w_down")
G_MIX = ("w_in", "w_branch_att", "w_branch_sgu", "w_out")
G_FFN2 = ("ffn2_w_gate", "ffn2_w_up", "ffn2_w_down")


def _local_step(x, target, wb, ws, dist=None):
    def gather_on(names):
        return _ag_payload([wb[n] for n in names]) if dist else None

    t2 = _relbias_fwd(ws["rel_bias"])
    bias2 = _bias_blocks(t2)
    bst = ws["sgu_b_s"].T
    w_st = jnp.swapaxes(ws["sgu_w_s"], 1, 2)

    if dist:
        wb.update(zip(G_FFN1, _call(lambda: None, gather_on(G_FFN1), name="allgather_ffn1", grid=None, in_specs=[],
                                    out_specs=[], out_shape=[])))
    x1, h1, a1, b1, *got = _ffn_fwd(x, ws["norm_ffn1"], wb["ffn1_w_gate"], wb["ffn1_w_up"], wb["ffn1_w_down"],
                                    "ffn1_fwd", gather_on(G_MIX))
    wb.update(zip(G_MIX, got))
    h2, qkv, zs, gl, y_sgu, *got = _in_fwd(x1, ws["norm_mix"], wb["w_in"], ws["sgu_ln_g"], ws["sgu_ln_b"],
                                           ws["sgu_w_s"], bst, gather_on(G_FFN2[0:1]))
    wb.update(zip(G_FFN2[0:1], got))
    y_att, *got = _att_fwd(qkv, bias2, gather_on(G_FFN2[1:2]))
    wb.update(zip(G_FFN2[1:2], got))
    x2, merged, pa, ps, *got = _merge_fwd(x1, y_att, y_sgu, gl, ws["b_gate"], wb["w_branch_att"], wb["w_branch_sgu"],
                                          wb["w_out"], gather_on(G_FFN2[2:3]))
    wb.update(zip(G_FFN2[2:3], got))
    dx3, h3, a3, b3, loss, g_final = _ffn_fwd(x2, ws["norm_ffn2"], wb["ffn2_w_gate"], wb["ffn2_w_up"],
                                              wb["ffn2_w_down"], "ffn2_fwd", head=(target, ws["norm_final"]))

    gb, gs, sums = {}, {"norm_final": g_final}, {}

    def pair_on(names, small=None):
        return _px_payload([gb[n] for n in names], small) if dist else None

    def pair_add(names, halves):
        for n, rv in zip(names, halves):
            sums[n] = _pair_add(gb[n], rv, dist[0], dist[1], "pair_add_" + n)

    def chips_on(names):
        return _cx_payload([sums[n][1] for n in names], [sums[n][2] for n in names]) if dist else None

    dx2, da3, db3, gs["norm_ffn2"] = _ffn_dgrad(dx3, x2, a3, b3, ws["norm_ffn2"], wb["ffn2_w_gate"], wb["ffn2_w_up"],
                                                wb["ffn2_w_down"], "ffn2_dgrad")
    gb["ffn2_w_gate"], gb["ffn2_w_up"], gb["ffn2_w_down"] = _ffn_wgrad(h3, dx3, a3, b3, da3, db3, "ffn2_wgrad")
    dy_att, dy_sgu, dgl, gs["b_gate"], gb["w_branch_att"], gb["w_branch_sgu"], gb["w_out"], *got = _merge_bwd(
        dx2, y_att, y_sgu, gl, merged, pa, ps, ws["b_gate"], wb["w_branch_att"], wb["w_branch_sgu"], wb["w_out"],
        pair_on(G_FFN2))
    pair_add(G_FFN2, got)
    dq, dk, dv, db2t, *lands2 = _att_bwd(qkv, dy_att, jnp.swapaxes(bias2, 1, 2), chips_on(G_FFN2))
    gs["rel_bias"] = _relbias_bwd(_unskew(jnp.swapaxes(db2t, 1, 2)))
    dx1, dz, gs["norm_mix"], gs["sgu_w_s"], dbt, gs["sgu_ln_g"], gs["sgu_ln_b"] = _in_dgrad(
        dx2, x1, ws["norm_mix"], wb["w_in"], dq, dk, dv, dgl, zs, dy_sgu, ws["sgu_ln_g"], ws["sgu_ln_b"], ws["sgu_w_s"],
        w_st, bst)
    gs["sgu_b_s"] = dbt.T
    gb["w_in"] = _in_wgrad(h2, dz)
    gx, da1, db1, gs["norm_ffn1"], *got = _ffn_dgrad(dx1, x, a1, b1, ws["norm_ffn1"], wb["ffn1_w_gate"],
                                                    wb["ffn1_w_up"], wb["ffn1_w_down"], "ffn1_dgrad", pair_on(G_MIX))
    pair_add(G_MIX, got)
    gb["ffn1_w_gate"], gb["ffn1_w_up"], gb["ffn1_w_down"], *lands_mix = _ffn_wgrad(h1, dx1, a1, b1, da1, db1,
                                                                                   "ffn1_wgrad", chips_on(G_MIX))
    if not dist:
        return loss, gx, gb, gs

    def final_sums(names, lands):
        return [_final_sum(sums[n][0], land, dist[1], dist[0], "final_sum_" + n) for n, land in zip(names, lands)]

    early = G_FFN2 + G_MIX
    k = len(G_FFN1)
    tail = _tail_reduce([gb[n] for n in G_FFN1], _pack_small(gs, loss),
                        _ss_payload(final_sums(early, list(lands2) + list(lands_mix))))
    for i, n in enumerate(G_FFN1):
        sums[n] = (tail[i],)
    small_sums, shared = tail[2 * k], dict(zip(early, tail[2 * k + 1:]))
    shared.update(zip(G_FFN1, _sibling_share(final_sums(G_FFN1, tail[k:2 * k]), "sibling_share")))
    return loss, gx, shared, small_sums


_ANY = pl.BlockSpec(memory_space=pl.ANY)
_VMEM = pl.BlockSpec(memory_space=pltpu.VMEM)


def _mesh_pos():
    return lax.axis_index("x"), lax.axis_index("y"), lax.axis_index("c")


def _cast_slots(shards, chip, name):
    n = len(shards)
    r, ncol = shards[0].shape
    tr = r // 2

    def body(me_ref, *refs):
        for i_ref, o_ref in zip(refs[:n], refs[n:]):
            o_ref[0] = i_ref[...].astype(BF16)

    grid_spec = pltpu.PrefetchScalarGridSpec(
        num_scalar_prefetch=1, grid=(r // tr,),
        in_specs=[pl.BlockSpec((tr, ncol), lambda i, me: (i, 0))] * n,
        out_specs=[pl.BlockSpec((1, tr, ncol), lambda i, me: (me[0], i, 0))] * n)
    return pl.pallas_call(
        body, name=name, grid_spec=grid_spec,
        out_shape=[jax.ShapeDtypeStruct((N_SHARD, r, ncol), BF16)] * n,
        compiler_params=_params(("arbitrary",)),
    )(chip, *shards)


class _Payload:
    def __init__(self, arrays, out_shapes, aliases, scratch, phases):
        self.arrays = list(arrays)
        self.out_shapes = list(out_shapes)
        self.aliases = dict(aliases)
        self.scratch = list(scratch)
        self.phases = phases


def _remote(src, dst, ssem, rsem, dev):
    return pltpu.make_async_remote_copy(src_ref=src, dst_ref=dst, send_sem=ssem, recv_sem=rsem, device_id=dev,
                                        device_id_type=MESH)


def _call(body, payload, *, name, grid, in_specs, out_specs, out_shape, scratch_shapes=(), sem=None, when=None,
          operands=(), takes_middle=False):
    in_specs, out_specs, out_shape = list(in_specs), list(out_specs), list(out_shape)
    scratch_shapes = list(scratch_shapes)
    n_in, n_out, n_scr = len(in_specs), len(out_specs), len(scratch_shapes)
    kwargs = {}
    kernel = body
    if payload is not None:
        k_in, k_out = len(payload.arrays), len(payload.out_shapes)
        rank = len(grid) if grid else 0

        def kernel(*refs):
            a, b = n_in, n_in + k_in
            c, d = b + n_out, b + n_out + k_out
            e = d + n_scr
            phases = payload.phases(refs[a:b], refs[c:d], refs[e:])

            def run():
                body(*refs[:a], *refs[b:c], *refs[d:e])

            if not grid:
                phases[0]()
                if len(phases) == 3 and takes_middle:
                    body(*refs[:a], *refs[b:c], *refs[d:e], middle=phases[1])
                    phases[2]()
                    return
                run()
                for ph in phases[1:]:
                    ph()
                return
            step = pl.program_id(0)
            if rank == 2:
                step = step * grid[1] + pl.program_id(1)
            marks = list(when)
            if len(phases) == 3:
                marks = [when[0], (max(when[1][0] - 3, 0), False), when[1]]
            for ph, (at, before) in zip(phases, marks):
                if before:
                    pl.when(step == at)(ph)
            run()
            for ph, (at, before) in zip(phases, marks):
                if not before:
                    pl.when(step == at)(ph)

        in_specs += [_ANY] * k_in
        out_specs += [_ANY] * k_out
        out_shape += payload.out_shapes
        scratch_shapes += payload.scratch
        kwargs["input_output_aliases"] = {n_in + i: n_out + j for i, j in payload.aliases.items()}
        operands = tuple(operands) + tuple(payload.arrays)
    if grid:
        kwargs["grid"] = grid
    return pl.pallas_call(kernel, name=name, in_specs=in_specs, out_specs=out_specs, out_shape=out_shape,
                          scratch_shapes=scratch_shapes, compiler_params=_params(sem), **kwargs)(*operands)


def _ag_payload(slots):
    n = len(slots)

    def phases(_, refs, sems):
        send_i, recv_i, send_d, recv_d = sems
        x, y, c = _mesh_pos()
        me = 2 * x + y

        def half(w, core):
            rh = slots[w].shape[1] // 2
            return pl.ds(core * rh, rh)

        def ici(w, j):
            t = (me + 1 + j) % N_SHARD
            mine = refs[w].at[me, half(w, c), :]
            return _remote(mine, mine, send_i.at[3 * w + j], recv_i.at[3 * w + j], (t // 2, t % 2, c))

        def d2d(w, j, core):
            s = (me + 3 - j) % N_SHARD
            land = refs[w].at[s, half(w, core), :]
            return _remote(land, land, send_d.at[3 * w + j], recv_d.at[3 * w + j], (x, y, 1 - c))

        def start():
            for w in range(n):
                for j in range(3):
                    ici(w, j).start()

        def forward():
            for w in range(n):
                for j in range(3):
                    s = (me + 3 - j) % N_SHARD
                    land = refs[w].at[s, half(w, c), :]
                    _remote(land, land, send_i.at[3 * w + j], recv_i.at[3 * w + j], (x, y, c)).wait_recv()
                    d2d(w, j, c).start()

        def finish():
            for w in range(n):
                for j in range(3):
                    d2d(w, j, 1 - c).wait_recv()
            for w in range(n):
                for j in range(3):
                    ici(w, j).wait_send()
                    d2d(w, j, c).wait_send()

        return [start, forward, finish]

    return _Payload(slots, [jax.ShapeDtypeStruct(s.shape, s.dtype) for s in slots], {i: i for i in range(n)},
                    [pltpu.SemaphoreType.DMA((3 * n,)) for _ in range(4)], phases)


def _px_payload(grads, small=None):
    arrays = list(grads) + ([small] if small is not None else [])
    n = len(arrays)

    def phases(ins, outs, sems):
        send, recv = sems
        x, y, c = _mesh_pos()

        def copy(w):
            if w < len(grads):
                rh = grads[w].shape[1] // 2
                src = ins[w].at[:, pl.ds((1 - c) * rh, rh), :]
            else:
                src = ins[w]
            return _remote(src, outs[w], send.at[w], recv.at[w], (x, y, 1 - c))

        def start():
            for w in range(n):
                copy(w).start()

        def finish():
            for w in range(n):
                copy(w).wait()

        return [start, finish]

    out_shapes = [jax.ShapeDtypeStruct((N_SHARD, g.shape[1] // 2, g.shape[2]), F32) for g in grads]
    if small is not None:
        out_shapes.append(jax.ShapeDtypeStruct(small.shape, F32))
    return _Payload(arrays, out_shapes, {}, [pltpu.SemaphoreType.DMA((n,)), pltpu.SemaphoreType.DMA((n,))], phases)


def _cx_payload(pbs, lands):
    n = len(pbs)

    def phases(ins, outs, sems):
        send, recv = sems
        x, y, c = _mesh_pos()
        me = 2 * x + y

        def copy(w, j):
            t = (me + 1 + j) % N_SHARD
            return _remote(ins[w].at[t], outs[w].at[me], send.at[3 * w + j], recv.at[3 * w + j], (t // 2, t % 2, c))

        def start():
            for w in range(n):
                for j in range(3):
                    copy(w, j).start()

        def finish():
            for w in range(n):
                for j in range(3):
                    copy(w, j).wait()

        return [start, finish]

    return _Payload(list(pbs) + list(lands), [jax.ShapeDtypeStruct(p.shape, BF16) for p in lands],
                    {n + i: i for i in range(n)},
                    [pltpu.SemaphoreType.DMA((3 * n,)), pltpu.SemaphoreType.DMA((3 * n,))], phases)


def _pair_add(g, rv, core, chip, name):
    _, r, ncol = g.shape
    rh = r // 2

    def body(c_ref, me_ref, g_ref, rv_ref, pf_ref, pb_ref, land_ref):
        s = g_ref[0] + rv_ref[0]
        sb = s.astype(BF16)
        pb_ref[0] = sb

        @pl.when(pl.program_id(0) == me_ref[0])
        def _():
            pf_ref[...] = s
            land_ref[0] = sb

    slot = pl.BlockSpec((1, rh, ncol), lambda s, c, me: (s, 0, 0))
    grid_spec = pltpu.PrefetchScalarGridSpec(
        num_scalar_prefetch=2, grid=(N_SHARD,),
        in_specs=[pl.BlockSpec((1, rh, ncol), lambda s, c, me: (s, c[0], 0)), slot],
        out_specs=[pl.BlockSpec((rh, ncol), lambda s, c, me: (0, 0)), slot,
                   pl.BlockSpec((1, rh, ncol), lambda s, c, me: (me[0], 0, 0))])
    return pl.pallas_call(
        body, name=name, grid_spec=grid_spec,
        out_shape=[jax.ShapeDtypeStruct((rh, ncol), F32), jax.ShapeDtypeStruct((N_SHARD, rh, ncol), BF16),
                   jax.ShapeDtypeStruct((N_SHARD, rh, ncol), BF16)],
        compiler_params=_params(("arbitrary",)),
    )(core, chip, g, rv)


def _tail_reduce(grads, small, payload=None):
    n = len(grads)
    _, r, ncol = grads[0].shape
    rh = r // 2

    def body(*refs, middle=None):
        g_hbm, sm = refs[:n], refs[n]
        pf, land, sm_out = refs[n + 1:2 * n + 1], refs[2 * n + 1:3 * n + 1], refs[3 * n + 1]
        scr = refs[3 * n + 2:]
        rv, mine, sendb = scr[:n], scr[n:2 * n], scr[2 * n:3 * n]
        sm_rv, sm_sum, d_send, d_recv, load, i_send, i_recv, store = scr[3 * n:]
        x, y, c = _mesh_pos()
        me = 2 * x + y
        sib = (x, y, 1 - c)

        def pair(w):
            src = g_hbm[w].at[:, pl.ds((1 - c) * rh, rh), :] if w < n else sm
            return _remote(src, rv[w] if w < n else sm_rv, d_send.at[w], d_recv.at[w], sib)

        def chips(w, j):
            t = (me + 1 + j) % N_SHARD
            src = sendb[w].at[t] if w < n else sm_sum
            dst = land[w].at[me] if w < n else sm_out.at[me]
            return _remote(src, dst, i_send.at[3 * w + j], i_recv.at[3 * w + j], (t // 2, t % 2, c))

        loads = [pltpu.make_async_copy(g_hbm[w].at[:, pl.ds(c * rh, rh), :], mine[w], load.at[w]) for w in range(n)]
        for w in range(n + 1):
            pair(w).start()
        for cp in loads:
            cp.start()
        stores = []
        for w in range(n):
            loads[w].wait()
            pair(w).wait_recv()
            for k in range(N_SHARD):
                s = mine[w][k] + rv[w][k]
                mine[w][k] = s
                sendb[w][k] = s.astype(BF16)
            stores += [pltpu.make_async_copy(mine[w].at[me], pf[w], store.at[2 * w]),
                       pltpu.make_async_copy(sendb[w].at[me], land[w].at[me], store.at[2 * w + 1])]
            for cp in stores[-2:]:
                cp.start()
            for j in range(3):
                chips(w, j).start()
        pair(n).wait_recv()
        if middle is not None:
            middle()
        sm_sum[...] = sm[...] + sm_rv[...]
        stores.append(pltpu.make_async_copy(sm_sum, sm_out.at[me], store.at[2 * n]))
        stores[-1].start()
        for j in range(3):
            chips(n, j).start()
        for w in range(n + 1):
            pair(w).wait_send()
            for j in range(3):
                chips(w, j).wait()
        for cp in stores:
            cp.wait()

    half = (N_SHARD, rh, ncol)
    return _call(
        body, payload, name="tail_reduce", grid=None, takes_middle=True,
        in_specs=[_ANY] * n + [_VMEM], out_specs=[_ANY] * (2 * n + 1),
        out_shape=([jax.ShapeDtypeStruct((rh, ncol), F32)] * n + [jax.ShapeDtypeStruct(half, BF16)] * n
                   + [jax.ShapeDtypeStruct((N_SHARD,) + small.shape, F32)]),
        scratch_shapes=([pltpu.VMEM(half, F32)] * (2 * n) + [pltpu.VMEM(half, BF16)] * n
                        + [pltpu.VMEM(small.shape, F32), pltpu.VMEM(small.shape, F32),
                           pltpu.SemaphoreType.DMA((n + 1,)), pltpu.SemaphoreType.DMA((n + 1,)),
                           pltpu.SemaphoreType.DMA((n,)), pltpu.SemaphoreType.DMA((3 * n + 3,)),
                           pltpu.SemaphoreType.DMA((3 * n + 3,)), pltpu.SemaphoreType.DMA((2 * n + 1,))]),
        operands=(*grads, small))


def _final_sum(pf, land, chip, core, name):
    _, rh, ncol = land.shape

    def body(me_ref, c_ref, pf_ref, land_ref, o_ref):
        me = me_ref[0]
        acc = jnp.zeros((rh, ncol), F32)
        for k in range(N_SHARD):
            acc = acc + jnp.where(me == k, pf_ref[...], land_ref[k].astype(F32))
        o_ref[...] = acc

    grid_spec = pltpu.PrefetchScalarGridSpec(
        num_scalar_prefetch=2, grid=(1,),
        in_specs=[pl.BlockSpec((rh, ncol), lambda i, me, c: (0, 0)),
                  pl.BlockSpec((N_SHARD, rh, ncol), lambda i, me, c: (0, 0, 0))],
        out_specs=pl.BlockSpec((rh, ncol), lambda i, me, c: (c[0], 0)))
    return pl.pallas_call(
        body, name=name, grid_spec=grid_spec, out_shape=jax.ShapeDtypeStruct((2 * rh, ncol), F32),
        compiler_params=_params(("arbitrary",)),
    )(chip, core, pf, land)


def _ss_payload(fulls):
    n = len(fulls)

    def phases(_, outs, sems):
        send, recv = sems
        x, y, c = _mesh_pos()

        def copy(w):
            rh = fulls[w].shape[0] // 2
            mine = outs[w].at[pl.ds(c * rh, rh), :]
            return _remote(mine, mine, send.at[w], recv.at[w], (x, y, 1 - c))

        def start():
            for w in range(n):
                copy(w).start()

        def finish():
            for w in range(n):
                copy(w).wait()

        return [lambda: None, start, finish]

    return _Payload(fulls, [jax.ShapeDtypeStruct(f.shape, F32) for f in fulls], {i: i for i in range(n)},
                    [pltpu.SemaphoreType.DMA((n,)), pltpu.SemaphoreType.DMA((n,))], phases)


def _sibling_share(fulls, name):
    return _call(lambda: None, _ss_payload(fulls), name=name, grid=None, in_specs=[], out_specs=[], out_shape=[])


_ROW = {"rel_bias": 128, "sgu_b_s": 136, "norm_ffn1": 144, "norm_mix": 145, "norm_ffn2": 146, "norm_final": 147,
        "b_gate": 148, "sgu_ln_g": 150, "sgu_ln_b": 151}


def _pack_small(gs, loss):
    def body(ws, rel, bs, n1, nm, n2, nf, bg, lg, lb, loss_ref, o_ref):
        o_ref[...] = jnp.zeros_like(o_ref)
        o_ref[LOSS_ROW:LOSS_ROW + 1, 0:128] = loss_ref[...]
        for g in range(SGU_GROUPS):
            o_ref[0:SGU_BLOCK, g * SGU_BLOCK:(g + 1) * SGU_BLOCK] = ws[g]
        o_ref[128:136, 0:REL_PAD] = rel[...]
        o_ref[136:144, 0:SGU_BLOCK] = bs[...]
        o_ref[144:145, :] = n1[...]
        o_ref[145:146, :] = nm[...]
        o_ref[146:147, :] = n2[...]
        o_ref[147:148, :] = nf[...]
        o_ref[148:149, :] = bg[:, 0:D_MODEL]
        o_ref[149:150, :] = bg[:, D_MODEL:2 * D_MODEL]
        o_ref[150:151, 0:D_SGU] = lg[...]
        o_ref[151:152, 0:D_SGU] = lb[...]

    order = ("sgu_w_s", "rel_bias", "sgu_b_s", "norm_ffn1", "norm_mix", "norm_ffn2", "norm_final", "b_gate", "sgu_ln_g",
             "sgu_ln_b")
    return pl.pallas_call(body, name="pack_small", out_shape=jax.ShapeDtypeStruct((SMALL_ROWS, D_MODEL), F32))(
        *[gs[k] for k in order], loss)


def _adam(w, g, m, v):
    m2 = ADAM_B1 * m + (1.0 - ADAM_B1) * g
    v2 = ADAM_B2 * v + (1.0 - ADAM_B2) * (g * g)
    m_hat = m2 / (1.0 - ADAM_B1 ** ADAM_STEP)
    v_hat = v2 / (1.0 - ADAM_B2 ** ADAM_STEP)
    delta = -ADAM_LR * (m_hat / (jnp.sqrt(v_hat) + ADAM_EPS) + ADAM_WD * w)
    return delta, m2, v2


def _adam_small(sin, w, m, v):
    names = SMALL
    k = len(names)

    def body(*refs):
        sin_ref = refs[0]
        w_r, m_r, v_r = refs[1:1 + k], refs[1 + k:1 + 2 * k], refs[1 + 2 * k:1 + 3 * k]
        outs = refs[1 + 3 * k:]
        tot = sin_ref[0] + sin_ref[1] + sin_ref[2] + sin_ref[3]
        outs[4 * k][...] = tot[LOSS_ROW:LOSS_ROW + 1, 0:128]
        for i, name in enumerate(names):
            o = outs[4 * i:4 * i + 4]
            if name == "sgu_w_s":
                for gi in range(SGU_GROUPS):
                    g = tot[0:SGU_BLOCK, gi * SGU_BLOCK:(gi + 1) * SGU_BLOCK]
                    res = (g,) + _adam(w_r[i][gi], g, m_r[i][gi], v_r[i][gi])
                    for ref, val in zip(o, res):
                        ref[gi] = val
                continue
            r0 = _ROW[name]
            if name == "rel_bias":
                g = tot[r0:r0 + HEADS, 0:REL_PAD]
            elif name == "sgu_b_s":
                g = tot[r0:r0 + SGU_GROUPS, 0:SGU_BLOCK]
            elif name == "b_gate":
                g = jnp.concatenate([tot[r0:r0 + 1, :], tot[r0 + 1:r0 + 2, :]], axis=1)
            elif name in ("sgu_ln_g", "sgu_ln_b"):
                g = tot[r0:r0 + 1, 0:D_SGU]
            else:
                g = tot[r0:r0 + 1, :]
            res = (g,) + _adam(w_r[i][...], g, m_r[i][...], v_r[i][...])
            for ref, val in zip(o, res):
                ref[...] = val

    out_shape = []
    for name in names:
        out_shape += [jax.ShapeDtypeStruct(w[name].shape, F32)] * 4
    out_shape.append(jax.ShapeDtypeStruct((1, 128), F32))
    flat = pl.pallas_call(body, name="adam_small", out_shape=out_shape, compiler_params=_params())(
        sin, *[w[n] for n in names], *[m[n] for n in names], *[v[n] for n in names])
    return {name: tuple(flat[4 * i:4 * i + 4]) for i, name in enumerate(names)}, flat[4 * k]


def _adam_big(w, g, m, v, name):
    r, ncol = w.shape
    tr = 256 if r % 256 == 0 else r // 2

    def body(w_ref, g_ref, m_ref, v_ref, g2_ref, d_ref, m2_ref, v2_ref):
        gv = g_ref[...]
        g2_ref[...] = gv
        d_ref[...], m2_ref[...], v2_ref[...] = _adam(w_ref[...], gv, m_ref[...], v_ref[...])

    spec = pl.BlockSpec((tr, ncol), lambda i: (i, 0))
    return pl.pallas_call(
        body, name=name, grid=(r // tr,), in_specs=[spec] * 4, out_specs=[spec] * 4,
        out_shape=[jax.ShapeDtypeStruct(w.shape, F32)] * 4, compiler_params=_params(("arbitrary",)),
    )(w, g, m, v)


WEIGHTS = ("norm_ffn1", "ffn1_w_gate", "ffn1_w_up", "ffn1_w_down", "norm_mix", "w_in", "b_gate", "rel_bias", "sgu_ln_g",
           "sgu_ln_b", "sgu_w_s", "sgu_b_s", "w_branch_att", "w_branch_sgu", "w_out", "norm_ffn2", "ffn2_w_gate",
           "ffn2_w_up", "ffn2_w_down", "norm_final")


GATE_UP = ("ffn1_w_gate", "ffn1_w_up", "ffn2_w_gate", "ffn2_w_up")
_FFN = ("ffn1_w_gate", "ffn1_w_up", "ffn1_w_down", "ffn2_w_gate", "ffn2_w_up", "ffn2_w_down")
_CAST_GROUPS = ((_FFN, "cast_ffn"), (("w_in",), "cast_w_in"), (("w_branch_att", "w_branch_sgu"), "cast_branch"),
                (("w_out",), "cast_w_out"))


def _big_form(name, a):
    return jnp.swapaxes(a, 1, 2)[0] if name in GATE_UP else a[0]


def _big_back(name, a):
    return jnp.swapaxes(a[None], 1, 2) if name in GATE_UP else a[None]


def _small_form(name, a):
    if name == "norm_final":
        return a.reshape(1, D_MODEL)
    if name == "rel_bias":
        return jnp.pad(a[0], ((0, 0), (0, REL_PAD - N_REL)))
    if name in ("sgu_w_s", "sgu_b_s"):
        return a[0]
    return a


def _small_back(name, a, like):
    if name == "rel_bias":
        a = a[:, :N_REL]
    return a.reshape(like.shape)


def kernel(x, norm_ffn1, ffn1_w_gate, ffn1_w_up, ffn1_w_down, norm_mix, w_in, b_gate, rel_bias, sgu_ln_g, sgu_ln_b, sgu_w_s, sgu_b_s, w_branch_att, w_branch_sgu, w_out, norm_ffn2, ffn2_w_gate, ffn2_w_up, ffn2_w_down, norm_final, loss_target, m_norm_ffn1, m_ffn1_w_gate, m_ffn1_w_up, m_ffn1_w_down, m_norm_mix, m_w_in, m_b_gate, m_rel_bias, m_sgu_ln_g, m_sgu_ln_b, m_sgu_w_s, m_sgu_b_s, m_w_branch_att, m_w_branch_sgu, m_w_out, m_norm_ffn2, m_ffn2_w_gate, m_ffn2_w_up, m_ffn2_w_down, m_norm_final, v_norm_ffn1, v_ffn1_w_gate, v_ffn1_w_up, v_ffn1_w_down, v_norm_mix, v_w_in, v_b_gate, v_rel_bias, v_sgu_ln_g, v_sgu_ln_b, v_sgu_w_s, v_sgu_b_s, v_w_branch_att, v_w_branch_sgu, v_w_out, v_norm_ffn2, v_ffn2_w_gate, v_ffn2_w_up, v_ffn2_w_down, v_norm_final):
    w = dict(norm_ffn1=norm_ffn1, ffn1_w_gate=ffn1_w_gate, ffn1_w_up=ffn1_w_up, ffn1_w_down=ffn1_w_down, norm_mix=norm_mix,
             w_in=w_in, b_gate=b_gate, rel_bias=rel_bias, sgu_ln_g=sgu_ln_g, sgu_ln_b=sgu_ln_b, sgu_w_s=sgu_w_s,
             sgu_b_s=sgu_b_s, w_branch_att=w_branch_att, w_branch_sgu=w_branch_sgu, w_out=w_out, norm_ffn2=norm_ffn2,
             ffn2_w_gate=ffn2_w_gate, ffn2_w_up=ffn2_w_up, ffn2_w_down=ffn2_w_down, norm_final=norm_final)
    m = dict(norm_ffn1=m_norm_ffn1, ffn1_w_gate=m_ffn1_w_gate, ffn1_w_up=m_ffn1_w_up, ffn1_w_down=m_ffn1_w_down,
             norm_mix=m_norm_mix, w_in=m_w_in, b_gate=m_b_gate, rel_bias=m_rel_bias, sgu_ln_g=m_sgu_ln_g,
             sgu_ln_b=m_sgu_ln_b, sgu_w_s=m_sgu_w_s, sgu_b_s=m_sgu_b_s, w_branch_att=m_w_branch_att,
             w_branch_sgu=m_w_branch_sgu, w_out=m_w_out, norm_ffn2=m_norm_ffn2, ffn2_w_gate=m_ffn2_w_gate,
             ffn2_w_up=m_ffn2_w_up, ffn2_w_down=m_ffn2_w_down, norm_final=m_norm_final)
    v = dict(norm_ffn1=v_norm_ffn1, ffn1_w_gate=v_ffn1_w_gate, ffn1_w_up=v_ffn1_w_up, ffn1_w_down=v_ffn1_w_down,
             norm_mix=v_norm_mix, w_in=v_w_in, b_gate=v_b_gate, rel_bias=v_rel_bias, sgu_ln_g=v_sgu_ln_g,
             sgu_ln_b=v_sgu_ln_b, sgu_w_s=v_sgu_w_s, sgu_b_s=v_sgu_b_s, w_branch_att=v_w_branch_att,
             w_branch_sgu=v_w_branch_sgu, w_out=v_w_out, norm_ffn2=v_norm_ffn2, ffn2_w_gate=v_ffn2_w_gate,
             ffn2_w_up=v_ffn2_w_up, ffn2_w_down=v_ffn2_w_down, norm_final=v_norm_final)

    core = lax.axis_index("c").astype(jnp.int32).reshape(1)
    chip = (2 * lax.axis_index("x") + lax.axis_index("y")).astype(jnp.int32).reshape(1)

    wk = {n: _big_form(n, w[n]) for n in BIG}
    slots = {}
    for names, call in _CAST_GROUPS:
        slots.update(zip(names, _cast_slots([wk[n] for n in names], chip, call)))
    ws = {n: _small_form(n, w[n]) for n in SMALL}
    _, gx, shard_grads, small_sums = _local_step(x[0], loss_target[0], slots, ws, (core, chip))

    small, loss = _adam_small(small_sums, ws, {n: _small_form(n, m[n]) for n in SMALL},
                              {n: _small_form(n, v[n]) for n in SMALL})
    grad, delta, new_m, new_v = {}, {}, {}, {}
    for n in SMALL:
        grad[n], delta[n], new_m[n], new_v[n] = (_small_back(n, a, w[n]) for a in small[n])
    for n in BIG:
        g2, d2, m2, v2 = _adam_big(wk[n], shard_grads[n], _big_form(n, m[n]), _big_form(n, v[n]), "adam_" + n)
        grad[n], delta[n], new_m[n], new_v[n] = (_big_back(n, a) for a in (g2, d2, m2, v2))

    return (loss[0, 0], gx.reshape(x.shape), *[grad[n] for n in WEIGHTS], *[delta[n] for n in WEIGHTS],
            *[new_m[n] for n in WEIGHTS], *[new_v[n] for n in WEIGHTS])
```

```python
import functools

import jax
import jax.numpy as jnp
from jax import lax
from jax.experimental import pallas as pl
from jax.experimental.pallas import tpu as pltpu

F32 = jnp.float32
BF16 = jnp.bfloat16

D_MODEL = 1024
N_SHARD = 4
D_FF = 2816
FF_S = D_FF // N_SHARD
D_ATT = 512
D_SGU = 512
D_IN = 3 * D_ATT + 2 * D_SGU + 2 * D_MODEL
IN_S = D_IN // N_SHARD
BR_S = D_MODEL // N_SHARD
HEADS = 8
HEAD_DIM = 64
CHUNK = 64
N_LEFT = 8
BAND = (N_LEFT + 1) * CHUNK
REL_CLIP = 256
N_REL = 2 * REL_CLIP + 1
REL_PAD = 640
SGU_BLOCK = 128
SGU_GROUPS = 8
SGU_GDIM = 64
EPS = 1e-6
NEG_INF = -1e30

ATT_ROWS = 2 * CHUNK
ATT_KEYS = BAND + CHUNK
ATT_PAD = N_LEFT * CHUNK

ADAM_LR = 0.001
ADAM_B1 = 0.9
ADAM_B2 = 0.999
ADAM_EPS = 1e-08
ADAM_WD = 0.01
ADAM_STEP = 10

TM = 256
TW = 1024
DGRAD_ROWS = 64
VMEM_LIMIT = 56 * 1024 * 1024

SMALL_ROWS = 160
LOSS_ROW = 152
MESH = pl.DeviceIdType.MESH

_NT = (((1,), (1,)), ((), ()))
_TN = (((0,), (0,)), ((), ()))


def _params(sem=None):
    return pltpu.CompilerParams(dimension_semantics=sem, vmem_limit_bytes=VMEM_LIMIT)


def _const_spec(shape):
    nd = len(shape)
    return pl.BlockSpec(shape, lambda *_: (0,) * nd, pipeline_mode=pl.Buffered(1))


def _acc_spec(shape):
    nd = len(shape)
    return pl.BlockSpec(shape, lambda *_: (0,) * nd)


def _row_spec(tm, ncols, off=0):
    return pl.BlockSpec((tm, ncols), lambda i: (i + off, 0))


def _row3_spec(tm, ncols):
    return pl.BlockSpec((N_SHARD, tm, ncols), lambda i: (0, i, 0))


def _dot(a, b):
    return jnp.dot(a, b, preferred_element_type=F32)


def _dot_nt(a, b):
    return lax.dot_general(a, b, _NT, preferred_element_type=F32)


def _dot_tn(a, b):
    return lax.dot_general(a, b, _TN, preferred_element_type=F32)


def _rms_fwd(x, g):
    r = lax.rsqrt(jnp.mean(x * x, axis=-1, keepdims=True) + EPS)
    xhat = x * r
    return xhat, r, xhat * g


def _rms_bwd(dh, xhat, r, g):
    dxhat = dh * g
    dx = r * (dxhat - xhat * jnp.mean(dxhat * xhat, axis=-1, keepdims=True))
    dg = jnp.sum(dh * xhat, axis=0, keepdims=True)
    return dx, dg


def _sigmoid(x):
    return 1.0 / (1.0 + jnp.exp(-x))


def _edges(n_steps):
    return [(0, True), (n_steps - 1, False)]


def _ffn_fwd(x, g, wg, wu, wd, name, payload=None, head=None):
    T = x.shape[0]

    def body(x_ref, g_ref, wg_ref, wu_ref, wd_ref, *rest):
        if head:
            t_ref, gf_ref, xo_ref, h_ref, a_ref, b_ref, loss_ref, dgf_ref = rest
        else:
            xo_ref, h_ref, a_ref, b_ref = rest
        xv = x_ref[...]
        hb = _rms_fwd(xv, g_ref[...])[2].astype(BF16)
        h_ref[...] = hb
        acc = jnp.zeros((TM, D_MODEL), F32)
        for s in range(N_SHARD):
            a = _dot_nt(hb, wg_ref[s])
            b = _dot_nt(hb, wu_ref[s])
            a_ref[s] = a.astype(BF16)
            b_ref[s] = b.astype(BF16)
            sv = a * _sigmoid(a) * b
            acc += _dot(sv.astype(BF16), wd_ref[s])
        xo = xv + 0.5 * acc
        if not head:
            xo_ref[...] = xo
            return

        @pl.when(pl.program_id(0) == 0)
        def _():
            loss_ref[...] = jnp.zeros_like(loss_ref)
            dgf_ref[...] = jnp.zeros_like(dgf_ref)

        gf = gf_ref[...]
        xhat, r, y = _rms_fwd(xo, gf)
        err = y - t_ref[...]
        loss_ref[...] += 0.5 * jnp.sum(jnp.mean(err * err, axis=-1, keepdims=True), axis=0, keepdims=True)
        dxn, dgf = _rms_bwd(err * (1.0 / D_MODEL), xhat, r, gf)
        xo_ref[...] = dxn
        dgf_ref[...] += dgf

    tok = jax.ShapeDtypeStruct((T, D_MODEL), F32)
    act = jax.ShapeDtypeStruct((N_SHARD, T, FF_S), BF16)
    return _call(
        body, payload, name=name, grid=(T // TM,), when=_edges(T // TM), sem=("arbitrary",),
        in_specs=[_row_spec(TM, D_MODEL), _const_spec((1, D_MODEL)), _const_spec(wg.shape), _const_spec(wu.shape),
                  _const_spec(wd.shape)] + ([_row_spec(TM, D_MODEL), _const_spec((1, D_MODEL))] if head else []),
        out_specs=[_row_spec(TM, D_MODEL), _row_spec(TM, D_MODEL), _row3_spec(TM, FF_S), _row3_spec(TM, FF_S)]
        + ([_acc_spec((1, 128)), _acc_spec((1, D_MODEL))] if head else []),
        out_shape=[tok, jax.ShapeDtypeStruct((T, D_MODEL), BF16), act, act]
        + ([jax.ShapeDtypeStruct((1, 128), F32), jax.ShapeDtypeStruct((1, D_MODEL), F32)] if head else []),
        operands=(x, g, wg, wu, wd) + (tuple(head) if head else ()))


def _ffn_dgrad(dout, x, a, b, g, wg, wu, wd, name, payload=None):
    T = x.shape[0]

    def body(do_ref, x_ref, a_ref, b_ref, g_ref, wg_ref, wu_ref, wd_ref, dx_ref, da_ref, db_ref, dg_ref):
        do = do_ref[...]
        dob = (0.5 * do).astype(BF16)
        dh = jnp.zeros((TM, D_MODEL), F32)
        ds_next = _dot_nt(dob, wd_ref[0])
        for s in range(N_SHARD):
            ds = ds_next
            if s + 1 < N_SHARD:
                ds_next = _dot_nt(dob, wd_ref[s + 1])
            for r0 in range(0, TM, DGRAD_ROWS):
                rows = slice(r0, r0 + DGRAD_ROWS)
                av = a_ref[s, rows, :].astype(F32)
                bv = b_ref[s, rows, :].astype(F32)
                sig = _sigmoid(av)
                dsr = ds[rows]
                da_ref[s, rows, :] = (dsr * bv * (sig * (1.0 + av * (1.0 - sig)))).astype(BF16)
                db_ref[s, rows, :] = (dsr * (av * sig)).astype(BF16)
            dh += _dot(da_ref[s], wg_ref[s]) + _dot(db_ref[s], wu_ref[s])
        gv = g_ref[...]
        xhat, r, _ = _rms_fwd(x_ref[...], gv)
        dxn, dg = _rms_bwd(dh, xhat, r, gv)
        dx_ref[...] = do + dxn

        @pl.when(pl.program_id(0) == 0)
        def _():
            dg_ref[...] = jnp.zeros_like(dg_ref)

        dg_ref[...] += dg

    return _call(
        body, payload, name=name, grid=(T // TM,), when=_edges(T // TM), sem=("arbitrary",),
        in_specs=[_row_spec(TM, D_MODEL), _row_spec(TM, D_MODEL), _row3_spec(TM, FF_S), _row3_spec(TM, FF_S),
                  _const_spec((1, D_MODEL)), _const_spec(wg.shape), _const_spec(wu.shape), _const_spec(wd.shape)],
        out_specs=[_row_spec(TM, D_MODEL), _row3_spec(TM, FF_S), _row3_spec(TM, FF_S), _acc_spec((1, D_MODEL))],
        out_shape=[jax.ShapeDtypeStruct((T, D_MODEL), F32), jax.ShapeDtypeStruct((N_SHARD, T, FF_S), BF16),
                   jax.ShapeDtypeStruct((N_SHARD, T, FF_S), BF16), jax.ShapeDtypeStruct((1, D_MODEL), F32)],
        operands=(dout, x, a, b, g, wg, wu, wd))


def _ffn_wgrad(h, dout, a, b, da, db, name, payload=None):
    T = h.shape[0]

    def body(h_ref, do_ref, a_ref, b_ref, da_ref, db_ref, gwg_ref, gwu_ref, gwd_ref):
        @pl.when(pl.program_id(1) == 0)
        def _():
            gwg_ref[...] = jnp.zeros_like(gwg_ref)
            gwu_ref[...] = jnp.zeros_like(gwu_ref)
            gwd_ref[...] = jnp.zeros_like(gwd_ref)

        hv = h_ref[...]
        gwg_ref[0] += _dot_tn(da_ref[0], hv)
        gwu_ref[0] += _dot_tn(db_ref[0], hv)
        dob = do_ref[...].astype(BF16)
        av = a_ref[0].astype(F32)
        sv = (0.5 * av * _sigmoid(av) * b_ref[0].astype(F32)).astype(BF16)
        gwd_ref[0] += _dot_tn(sv, dob)

    tw = min(TW, T)
    tok = pl.BlockSpec((tw, D_MODEL), lambda s, i: (i, 0))
    act = pl.BlockSpec((1, tw, FF_S), lambda s, i: (s, i, 0))
    return _call(
        body, payload, name=name, grid=(N_SHARD, T // tw), when=_edges(N_SHARD * (T // tw)),
        sem=("arbitrary", "arbitrary"),
        in_specs=[tok, tok, act, act, act, act],
        out_specs=[pl.BlockSpec((1, FF_S, D_MODEL), lambda s, i: (s, 0, 0))] * 3,
        out_shape=[jax.ShapeDtypeStruct((N_SHARD, FF_S, D_MODEL), F32)] * 3,
        operands=(h, dout, a, b, da, db))


def _in_fwd(x, g, w_in, lng, lnb, w_s, bst, payload=None):
    T = x.shape[0]

    def body(x_ref, g_ref, w_ref, lng_ref, lnb_ref, ws_ref, bst_ref, h_ref, qkv_ref, zs_ref, gl_ref, y_ref):
        hb = _rms_fwd(x_ref[...], g_ref[...])[2].astype(BF16)
        h_ref[...] = hb
        z0 = _dot(hb, w_ref[0])
        qkv_ref[:, 0:IN_S] = z0.astype(BF16)
        z1 = _dot(hb, w_ref[1])
        qkv_ref[:, IN_S:3 * D_ATT] = z1[:, 0:384].astype(BF16)
        z2 = _dot(hb, w_ref[2])
        zs = jnp.concatenate([z1[:, 384:IN_S], z2[:, 0:256]], axis=1)
        zs_ref[...] = zs
        gl_ref[:, 0:896] = z2[:, 256:IN_S]
        gl_ref[:, 896:2048] = _dot(hb, w_ref[3])
        _, u, _, _, vn = _sgu_norm(zs, lng_ref[...], lnb_ref[...])
        gm = _group_masks()
        bst_v = bst_ref[...]
        for n in range(TM // SGU_BLOCK):
            rows = slice(n * SGU_BLOCK, (n + 1) * SGU_BLOCK)
            y_ref[rows, :] = (u[rows] * _sgu_mix(vn[rows], ws_ref, bst_v, gm)).astype(BF16)

    return _call(
        body, payload, name="in_fwd", grid=(T // TM,), when=_edges(T // TM), sem=("arbitrary",),
        in_specs=[_row_spec(TM, D_MODEL), _const_spec((1, D_MODEL)), _const_spec(w_in.shape), _const_spec((1, D_SGU)),
                  _const_spec((1, D_SGU)), _const_spec(w_s.shape), _const_spec(bst.shape)],
        out_specs=[_row_spec(TM, D_MODEL), _row_spec(TM, 3 * D_ATT), _row_spec(TM, 2 * D_SGU), _row_spec(TM, 2 * D_MODEL),
                   _row_spec(TM, D_SGU)],
        out_shape=[jax.ShapeDtypeStruct((T, D_MODEL), BF16), jax.ShapeDtypeStruct((T, 3 * D_ATT), BF16),
                   jax.ShapeDtypeStruct((T, 2 * D_SGU), F32), jax.ShapeDtypeStruct((T, 2 * D_MODEL), F32),
                   jax.ShapeDtypeStruct((T, D_SGU), BF16)],
        operands=(x, g, w_in, lng, lnb, w_s, bst))


def _in_dgrad(dx_res, x, g, w_in, dq, dk, dv, dgl, zs, dy_sgu, lng, lnb, w_s, w_st, bst):
    T = x.shape[0]

    def body(dxr_ref, x_ref, g_ref, w_ref, dq_ref, dk_ref, dv_ref, dgl_ref, zs_ref, dys_ref, lng_ref, lnb_ref, ws_ref,
             wst_ref, bst_ref, dx_ref, dz_ref, dg_ref, dw_ref, dbt_ref, dlg_ref, dlb_ref):
        @pl.when(pl.program_id(0) == 0)
        def _():
            for ref in (dg_ref, dw_ref, dbt_ref, dlg_ref, dlb_ref):
                ref[...] = jnp.zeros_like(ref)

        dv = dv_ref[...].astype(BF16)
        dgl = dgl_ref[...]
        dz0 = jnp.concatenate([dq_ref[...], dk_ref[...].astype(BF16), dv[:, 0:128]], axis=1)
        dz3 = dgl[:, 896:2048]
        dh = _dot_nt(dz0, w_ref[0]) + _dot_nt(dz3, w_ref[3])
        dzs = _sgu_bwd_tile(zs_ref[...], dys_ref[...].astype(F32), lng_ref[...], lnb_ref[...], ws_ref, wst_ref,
                            bst_ref[...], dw_ref, dbt_ref, dlg_ref, dlb_ref)
        dz1 = jnp.concatenate([dv[:, 128:D_ATT], dzs[:, 0:768]], axis=1)
        dz2 = jnp.concatenate([dzs[:, 768:1024], dgl[:, 0:896]], axis=1)
        dh += _dot_nt(dz1, w_ref[1]) + _dot_nt(dz2, w_ref[2])
        for s, piece in enumerate((dz0, dz1, dz2, dz3)):
            dz_ref[:, s * IN_S:(s + 1) * IN_S] = piece
        gv = g_ref[...]
        xhat, r, _ = _rms_fwd(x_ref[...], gv)
        dxn, dg = _rms_bwd(dh, xhat, r, gv)
        dx_ref[...] = dxr_ref[...] + dxn
        dg_ref[...] += dg

    pad_blocks = ATT_PAD // TM
    return pl.pallas_call(
        body, name="in_dgrad", grid=(T // TM,),
        in_specs=[_row_spec(TM, D_MODEL), _row_spec(TM, D_MODEL), _const_spec((1, D_MODEL)), _const_spec(w_in.shape),
                  _row_spec(TM, D_ATT), _row_spec(TM, D_ATT, pad_blocks), _row_spec(TM, D_ATT, pad_blocks),
                  _row_spec(TM, 2 * D_MODEL), _row_spec(TM, 2 * D_SGU), _row_spec(TM, D_SGU), _const_spec((1, D_SGU)),
                  _const_spec((1, D_SGU)), _const_spec(w_s.shape), _const_spec(w_st.shape), _const_spec(bst.shape)],
        out_specs=[_row_spec(TM, D_MODEL), _row_spec(TM, D_IN), _acc_spec((1, D_MODEL)), _acc_spec(w_s.shape),
                   _acc_spec(bst.shape), _acc_spec((1, D_SGU)), _acc_spec((1, D_SGU))],
        out_shape=[jax.ShapeDtypeStruct((T, D_MODEL), F32), jax.ShapeDtypeStruct((T, D_IN), BF16),
                   jax.ShapeDtypeStruct((1, D_MODEL), F32), jax.ShapeDtypeStruct(w_s.shape, F32),
                   jax.ShapeDtypeStruct(bst.shape, F32), jax.ShapeDtypeStruct((1, D_SGU), F32),
                   jax.ShapeDtypeStruct((1, D_SGU), F32)],
        compiler_params=_params(("arbitrary",)),
    )(dx_res, x, g, w_in, dq, dk, dv, dgl, zs, dy_sgu, lng, lnb, w_s, w_st, bst)


def _in_wgrad(h, dz):
    T = h.shape[0]

    def body(h_ref, dz_ref, gw_ref):
        @pl.when(pl.program_id(1) == 0)
        def _():
            gw_ref[...] = jnp.zeros_like(gw_ref)

        gw_ref[0] += _dot_tn(h_ref[...], dz_ref[...])

    return pl.pallas_call(
        body, name="in_wgrad", grid=(N_SHARD, T // min(TW, T)),
        in_specs=[pl.BlockSpec((min(TW, T), D_MODEL), lambda s, i: (i, 0)),
                  pl.BlockSpec((min(TW, T), IN_S), lambda s, i: (i, s))],
        out_specs=pl.BlockSpec((1, D_MODEL, IN_S), lambda s, i: (s, 0, 0)),
        out_shape=jax.ShapeDtypeStruct((N_SHARD, D_MODEL, IN_S), F32),
        compiler_params=_params(("arbitrary", "arbitrary")),
    )(h, dz)


def _rel_onehot():
    r = lax.broadcasted_iota(jnp.int32, (REL_PAD, REL_PAD), 0)
    n = lax.broadcasted_iota(jnp.int32, (REL_PAD, REL_PAD), 1)
    idx = jnp.clip(BAND - 1 - n, -REL_CLIP, REL_CLIP) + REL_CLIP
    return jnp.where(r == idx, 1.0, 0.0).astype(BF16)


def _split3(v):
    p1 = v.astype(BF16)
    r1 = v - p1.astype(F32)
    p2 = r1.astype(BF16)
    p3 = (r1 - p2.astype(F32)).astype(BF16)
    return p1, p2, p3


def _relbias_fwd(tab_pad):
    def body(t_ref, o_ref):
        oh = _rel_onehot()
        acc = jnp.zeros((HEADS, REL_PAD), F32)
        for p in _split3(t_ref[...]):
            acc += _dot(p, oh)
        o_ref[...] = acc

    return pl.pallas_call(body, name="relbias_fwd", out_shape=jax.ShapeDtypeStruct((HEADS, REL_PAD), F32))(tab_pad)


def _relbias_bwd(z):
    def body(z_ref, o_ref):
        oh = _rel_onehot()
        dt2 = jnp.sum(z_ref[...], axis=1)
        acc = jnp.zeros((HEADS, REL_PAD), F32)
        for p in _split3(dt2):
            acc += _dot_nt(p, oh)
        o_ref[...] = acc

    return pl.pallas_call(body, name="relbias_bwd", out_shape=jax.ShapeDtypeStruct((HEADS, REL_PAD), F32))(z)


def _bias_blocks(t2):
    flat = jnp.tile(t2, (1, CHUNK))
    skew = flat[:, :CHUNK * (REL_PAD - 1)].reshape(HEADS, CHUNK, REL_PAD - 1)
    bias = skew[:, :, CHUNK - 1:CHUNK - 1 + BAND]
    slabs = [jnp.pad(bias, ((0, 0), (0, 0), (CHUNK * c, ATT_KEYS - BAND - CHUNK * c)), constant_values=NEG_INF)
             for c in range(2)]
    return jnp.concatenate(slabs, axis=1)


def _unskew(db2):
    out = []
    for c in range(2):
        slab = db2[:, CHUNK * c:CHUNK * (c + 1), CHUNK * c:CHUNK * c + BAND]
        y = jnp.pad(slab, ((0, 0), (0, 0), (CHUNK - 1, REL_PAD - BAND - CHUNK + 1)))
        yf = jnp.pad(y.reshape(HEADS, CHUNK * REL_PAD), ((0, 0), (0, CHUNK)))
        out.append(yf.reshape(HEADS, CHUNK, REL_PAD + 1)[:, :, :REL_PAD])
    return jnp.concatenate(out, axis=1)


def _att_load(qkv_hbm, q_s, k_s, v_s, sem, T):
    copies = [pltpu.make_async_copy(qkv_hbm.at[:, 0:D_ATT], q_s, sem.at[0]),
              pltpu.make_async_copy(qkv_hbm.at[:, D_ATT:2 * D_ATT], k_s.at[pl.ds(ATT_PAD, T), :], sem.at[1]),
              pltpu.make_async_copy(qkv_hbm.at[:, 2 * D_ATT:3 * D_ATT], v_s.at[pl.ds(ATT_PAD, T), :], sem.at[2])]
    for cp in copies:
        cp.start()
    k_s[0:ATT_PAD, :] = jnp.zeros((ATT_PAD, D_ATT), BF16)
    v_s[0:ATT_PAD, :] = jnp.zeros((ATT_PAD, D_ATT), BF16)
    for cp in copies:
        cp.wait()


def _head(v, h):
    return v[:, h * HEAD_DIM:(h + 1) * HEAD_DIM]


def _rows(v, h):
    return v[h * ATT_ROWS:(h + 1) * ATT_ROWS]


def _att_exp(qs, kw, bias_ref, valid):
    s = jnp.concatenate([_dot_nt(_head(qs, h), _head(kw, h)) + bias_ref[h] for h in range(HEADS)], axis=0)
    if valid is not None:
        s = jnp.where(valid, s, NEG_INF)
    e = jnp.exp(s - jnp.max(s, axis=-1, keepdims=True))
    return e, 1.0 / jnp.sum(e, axis=-1, keepdims=True)


def _att_blocks(T, block, keys_on_rows=False, middle=None):
    n_edge = min(ATT_PAD // ATT_ROWS, T // ATT_ROWS)
    shape, axis = ((ATT_KEYS, 1), 0) if keys_on_rows else ((1, ATT_KEYS), 1)

    def edge(i, carry):
        r0 = i * ATT_ROWS
        block(i, (lax.broadcasted_iota(jnp.int32, shape, axis) + (r0 - ATT_PAD)) >= 0)
        return carry

    def inner(i, carry):
        block(i, None)
        return carry

    n_blocks = T // ATT_ROWS
    lax.fori_loop(0, n_edge, edge, 0)
    if middle is None:
        lax.fori_loop(n_edge, n_blocks, inner, 0)
        return
    n_late = max(n_blocks - n_blocks // 4, n_edge)
    lax.fori_loop(n_edge, n_late, inner, 0)
    middle()
    lax.fori_loop(n_late, n_blocks, inner, 0)


def _att_fwd(qkv, bias2, payload=None):
    T = qkv.shape[0]

    def body(qkv_hbm, bias_ref, y_ref, q_s, k_s, v_s, sem, middle=None):
        _att_load(qkv_hbm, q_s, k_s, v_s, sem, T)

        def block(i, valid):
            r0 = pl.multiple_of(i * ATT_ROWS, ATT_ROWS)
            qs = q_s[pl.ds(r0, ATT_ROWS), :] * (HEAD_DIM ** -0.5)
            kw = k_s[pl.ds(r0, ATT_KEYS), :]
            vw = v_s[pl.ds(r0, ATT_KEYS), :]
            e, rinv = _att_exp(qs, kw, bias_ref, valid)
            eb = e.astype(BF16)
            outs = [_dot(_rows(eb, h), _head(vw, h)) * _rows(rinv, h) for h in range(HEADS)]
            y_ref[pl.ds(r0, ATT_ROWS), :] = jnp.concatenate(outs, axis=1).astype(BF16)

        _att_blocks(T, block, middle=middle)

    return _call(
        body, payload, name="att_fwd", grid=None, takes_middle=True,
        in_specs=[pl.BlockSpec(memory_space=pl.ANY), pl.BlockSpec(memory_space=pltpu.VMEM)],
        out_specs=[pl.BlockSpec(memory_space=pltpu.VMEM)],
        out_shape=[jax.ShapeDtypeStruct((T, D_ATT), BF16)],
        scratch_shapes=[pltpu.VMEM((T, D_ATT), BF16), pltpu.VMEM((T + ATT_PAD, D_ATT), BF16),
                        pltpu.VMEM((T + ATT_PAD, D_ATT), BF16), pltpu.SemaphoreType.DMA((3,))],
        operands=(qkv, bias2))


def _lanes(v, h):
    return v[:, h * ATT_ROWS:(h + 1) * ATT_ROWS]


def _att_bwd(qkv, dy, bias2t, payload=None):
    T = qkv.shape[0]

    def body(qkv_hbm, dy_ref, bias_ref, dq_ref, dk_ref, dv_ref, db_ref, q_s, k_s, v_s, sem):
        _att_load(qkv_hbm, q_s, k_s, v_s, sem, T)
        dk_ref[...] = jnp.zeros_like(dk_ref)
        dv_ref[...] = jnp.zeros_like(dv_ref)
        db_ref[...] = jnp.zeros_like(db_ref)

        def block(i, valid):
            r0 = pl.multiple_of(i * ATT_ROWS, ATT_ROWS)
            qs = q_s[pl.ds(r0, ATT_ROWS), :] * (HEAD_DIM ** -0.5)
            kw = k_s[pl.ds(r0, ATT_KEYS), :]
            vw = v_s[pl.ds(r0, ATT_KEYS), :]
            dyb = dy_ref[pl.ds(r0, ATT_ROWS), :]
            s = jnp.concatenate([_dot_nt(_head(kw, h), _head(qs, h)) + bias_ref[h] for h in range(HEADS)], axis=1)
            if valid is not None:
                s = jnp.where(valid, s, NEG_INF)
            e = jnp.exp(s - jnp.max(s, axis=0, keepdims=True))
            p = e * (1.0 / jnp.sum(e, axis=0, keepdims=True))
            dp = jnp.concatenate([_dot_nt(_head(vw, h), _head(dyb, h)) for h in range(HEADS)], axis=1)
            ds = p * (dp - jnp.sum(p * dp, axis=0, keepdims=True))
            for h in range(HEADS):
                db_ref[h] += _lanes(ds, h)
            dsb = ds.astype(BF16)
            pb = p.astype(BF16)
            dq = [_dot_tn(_lanes(dsb, h), _head(kw, h)) for h in range(HEADS)]
            dk = [_dot(_lanes(dsb, h), _head(qs, h)) for h in range(HEADS)]
            dv = [_dot(_lanes(pb, h), _head(dyb, h)) for h in range(HEADS)]
            dq_ref[pl.ds(r0, ATT_ROWS), :] = (jnp.concatenate(dq, axis=1) * (HEAD_DIM ** -0.5)).astype(BF16)
            dk_ref[pl.ds(r0, ATT_KEYS), :] += jnp.concatenate(dk, axis=1)
            dv_ref[pl.ds(r0, ATT_KEYS), :] += jnp.concatenate(dv, axis=1)

        _att_blocks(T, block, keys_on_rows=True)

    vmem = pl.BlockSpec(memory_space=pltpu.VMEM)
    return _call(
        body, payload, name="att_bwd", grid=None,
        in_specs=[pl.BlockSpec(memory_space=pl.ANY), vmem, vmem],
        out_specs=[vmem, vmem, vmem, vmem],
        out_shape=[jax.ShapeDtypeStruct((T, D_ATT), BF16), jax.ShapeDtypeStruct((T + ATT_PAD, D_ATT), F32),
                   jax.ShapeDtypeStruct((T + ATT_PAD, D_ATT), F32), jax.ShapeDtypeStruct((HEADS, ATT_KEYS, ATT_ROWS), F32)],
        scratch_shapes=[pltpu.VMEM((T, D_ATT), BF16), pltpu.VMEM((T + ATT_PAD, D_ATT), BF16),
                        pltpu.VMEM((T + ATT_PAD, D_ATT), BF16), pltpu.SemaphoreType.DMA((3,))],
        operands=(qkv, dy, bias2t))


_GELU_C = 0.7978845608028654
_GELU_A = 0.044715


def _gelu(x):
    t = jnp.tanh(_GELU_C * (x + _GELU_A * x * x * x))
    return 0.5 * x * (1.0 + t), t


def _gelu_grad(x, t):
    return 0.5 * (1.0 + t) + 0.5 * x * (1.0 - t * t) * _GELU_C * (1.0 + 3.0 * _GELU_A * x * x)


def _group_masks():
    col = lax.broadcasted_iota(jnp.int32, (SGU_GROUPS, D_SGU), 1) // SGU_GDIM
    grp = lax.broadcasted_iota(jnp.int32, (SGU_GROUPS, D_SGU), 0)
    return jnp.where(col == grp, 1.0, 0.0).astype(F32)


def _causal_mask(transposed=False):
    i = lax.broadcasted_iota(jnp.int32, (SGU_BLOCK, SGU_BLOCK), 0) // CHUNK
    j = lax.broadcasted_iota(jnp.int32, (SGU_BLOCK, SGU_BLOCK), 1) // CHUNK
    return (j >= i) if transposed else (i >= j)


def _sgu_norm(zs, lng, lnb):
    gz, t = _gelu(zs)
    u = gz[:, 0:D_SGU]
    vs = gz[:, D_SGU:2 * D_SGU]
    xc = vs - jnp.mean(vs, axis=-1, keepdims=True)
    rstd = lax.rsqrt(jnp.mean(xc * xc, axis=-1, keepdims=True) + EPS)
    xhat = xc * rstd
    return t, u, xhat, rstd, xhat * lng + lnb


def _sgu_mix(vn_blk, w_ref, bst, gm):
    mask = _causal_mask()
    s = jnp.zeros((SGU_BLOCK, D_SGU), F32)
    for g in range(SGU_GROUPS):
        wm = jnp.where(mask, w_ref[g], 0.0).astype(BF16)
        s += _dot(wm, (vn_blk * gm[g:g + 1, :]).astype(BF16))
        s += bst[:, g:g + 1] * gm[g:g + 1, :]
    return s


def _sgu_bwd_tile(zs_v, dyv, lng_v, lnb_v, w_ref, wt_ref, bst_v, dw_ref, dbt_ref, dlg_ref, dlb_ref):
    t, u, xhat, rstd, vn = _sgu_norm(zs_v, lng_v, lnb_v)
    gm = _group_masks()
    mask = _causal_mask()
    mask_t = _causal_mask(transposed=True)
    lane8 = lax.broadcasted_iota(jnp.int32, (1, SGU_GROUPS), 1)
    du_rows, dvn_rows = [], []
    for n in range(TM // SGU_BLOCK):
        rows = slice(n * SGU_BLOCK, (n + 1) * SGU_BLOCK)
        vn_b = vn[rows]
        s = _sgu_mix(vn_b, w_ref, bst_v, gm)
        du_rows.append(dyv[rows] * s)
        dsb = dyv[rows] * u[rows]
        vnb16 = vn_b.astype(BF16)
        dvn = jnp.zeros((SGU_BLOCK, D_SGU), F32)
        dbt = jnp.zeros((SGU_BLOCK, SGU_GROUPS), F32)
        for g in range(SGU_GROUPS):
            dsg = dsb * gm[g:g + 1, :]
            dsg16 = dsg.astype(BF16)
            wmt = jnp.where(mask_t, wt_ref[g], 0.0).astype(BF16)
            dvn += _dot(wmt, dsg16)
            dw_ref[g] += jnp.where(mask, _dot_nt(dsg16, vnb16), 0.0)
            dbt += jnp.sum(dsg, axis=-1, keepdims=True) * jnp.where(lane8 == g, 1.0, 0.0)
        dbt_ref[...] += dbt
        dvn_rows.append(dvn)
    du = jnp.concatenate(du_rows, axis=0)
    dvn = jnp.concatenate(dvn_rows, axis=0)
    dlg_ref[...] += jnp.sum(dvn * xhat, axis=0, keepdims=True)
    dlb_ref[...] += jnp.sum(dvn, axis=0, keepdims=True)
    dxhat = dvn * lng_v
    dvs = rstd * (dxhat - jnp.mean(dxhat, axis=-1, keepdims=True)
                  - xhat * jnp.mean(dxhat * xhat, axis=-1, keepdims=True))
    dgz = jnp.concatenate([du, dvs], axis=1)
    return (dgz * _gelu_grad(zs_v, t)).astype(BF16)


def _cols(v, s):
    return v[:, s * BR_S:(s + 1) * BR_S]


def _merge_fwd(x, y_att, y_sgu, gl, b_gate, wba, wbs, wo, payload=None):
    T = x.shape[0]

    def body(x_ref, ya_ref, ys_ref, gl_ref, bg_ref, wba_ref, wbs_ref, wo_ref, xo_ref, m_ref, pa_ref, ps_ref):
        ya = ya_ref[...]
        ys = ys_ref[...]
        pa = jnp.concatenate([_dot(ya, wba_ref[s]) for s in range(N_SHARD)], axis=1)
        ps = jnp.concatenate([_dot(ys, wbs_ref[s]) for s in range(N_SHARD)], axis=1)
        g = _sigmoid(gl_ref[...] + bg_ref[...])
        mb = (g[:, 0:D_MODEL] * pa + g[:, D_MODEL:2 * D_MODEL] * ps).astype(BF16)
        m_ref[...] = mb
        pa_ref[...] = pa.astype(BF16)
        ps_ref[...] = ps.astype(BF16)
        acc = jnp.zeros((TM, D_MODEL), F32)
        for s in range(N_SHARD):
            acc += _dot(_cols(mb, s), wo_ref[s])
        xo_ref[...] = x_ref[...] + acc

    tokd = jax.ShapeDtypeStruct((T, D_MODEL), BF16)
    return _call(
        body, payload, name="merge_fwd", grid=(T // TM,), when=_edges(T // TM), sem=("arbitrary",),
        in_specs=[_row_spec(TM, D_MODEL), _row_spec(TM, D_ATT), _row_spec(TM, D_SGU), _row_spec(TM, 2 * D_MODEL),
                  _const_spec((1, 2 * D_MODEL)), _const_spec(wba.shape), _const_spec(wbs.shape), _const_spec(wo.shape)],
        out_specs=[_row_spec(TM, D_MODEL)] * 4,
        out_shape=[jax.ShapeDtypeStruct((T, D_MODEL), F32), tokd, tokd, tokd],
        operands=(x, y_att, y_sgu, gl, b_gate, wba, wbs, wo))


def _merge_bwd(dx, y_att, y_sgu, gl, merged, pa, ps, b_gate, wba, wbs, wo, payload=None):
    T = dx.shape[0]

    def body(dx_ref, ya_ref, ys_ref, gl_ref, m_ref, pa_ref, ps_ref, bg_ref, wba_ref, wbs_ref, wo_ref,
             dya_ref, dys_ref, dgl_ref, dbg_ref, gwba_ref, gwbs_ref, gwo_ref):
        @pl.when(pl.program_id(0) == 0)
        def _():
            dbg_ref[...] = jnp.zeros_like(dbg_ref)
            gwba_ref[...] = jnp.zeros_like(gwba_ref)
            gwbs_ref[...] = jnp.zeros_like(gwbs_ref)
            gwo_ref[...] = jnp.zeros_like(gwo_ref)

        dxb = dx_ref[...].astype(BF16)
        dm = jnp.concatenate([_dot_nt(dxb, wo_ref[s]) for s in range(N_SHARD)], axis=1)
        g = _sigmoid(gl_ref[...] + bg_ref[...])
        ga = g[:, 0:D_MODEL]
        gs = g[:, D_MODEL:2 * D_MODEL]
        dpa = (dm * ga).astype(BF16)
        dps = (dm * gs).astype(BF16)
        dgl = jnp.concatenate([dm * pa_ref[...].astype(F32) * ga * (1.0 - ga),
                               dm * ps_ref[...].astype(F32) * gs * (1.0 - gs)], axis=1)
        dgl_ref[...] = dgl.astype(BF16)
        dbg_ref[...] += jnp.sum(dgl, axis=0, keepdims=True)
        ya = ya_ref[...]
        ys = ys_ref[...]
        mb = m_ref[...]
        dya = jnp.zeros((TM, D_ATT), F32)
        dys = jnp.zeros((TM, D_SGU), F32)
        for s in range(N_SHARD):
            dya += _dot_nt(_cols(dpa, s), wba_ref[s])
            dys += _dot_nt(_cols(dps, s), wbs_ref[s])
            gwo_ref[s] += _dot_tn(_cols(mb, s), dxb)
            gwba_ref[s] += _dot_tn(ya, _cols(dpa, s))
            gwbs_ref[s] += _dot_tn(ys, _cols(dps, s))
        dya_ref[...] = dya.astype(BF16)
        dys_ref[...] = dys.astype(BF16)

    return _call(
        body, payload, name="merge_bwd", grid=(T // TM,), when=_edges(T // TM), sem=("arbitrary",),
        operands=(dx, y_att, y_sgu, gl, merged, pa, ps, b_gate, wba, wbs, wo),
        in_specs=[_row_spec(TM, D_MODEL), _row_spec(TM, D_ATT), _row_spec(TM, D_SGU), _row_spec(TM, 2 * D_MODEL),
                  _row_spec(TM, D_MODEL), _row_spec(TM, D_MODEL), _row_spec(TM, D_MODEL),
                  _const_spec((1, 2 * D_MODEL)), _const_spec(wba.shape), _const_spec(wbs.shape), _const_spec(wo.shape)],
        out_specs=[_row_spec(TM, D_ATT), _row_spec(TM, D_SGU), _row_spec(TM, 2 * D_MODEL), _acc_spec((1, 2 * D_MODEL)),
                   _acc_spec(wba.shape), _acc_spec(wbs.shape), _acc_spec(wo.shape)],
        out_shape=[jax.ShapeDtypeStruct((T, D_ATT), BF16), jax.ShapeDtypeStruct((T, D_SGU), BF16),
                   jax.ShapeDtypeStruct((T, 2 * D_MODEL), BF16), jax.ShapeDtypeStruct((1, 2 * D_MODEL), F32),
                   jax.ShapeDtypeStruct(wba.shape, F32), jax.ShapeDtypeStruct(wbs.shape, F32),
                   jax.ShapeDtypeStruct(wo.shape, F32)])


BIG = ("ffn1_w_gate", "ffn1_w_up", "ffn1_w_down", "w_in", "w_branch_att", "w_branch_sgu", "w_out",
       "ffn2_w_gate", "ffn2_w_up", "ffn2_w_down")
SMALL = ("norm_ffn1", "norm_mix", "b_gate", "rel_bias", "sgu_ln_g", "sgu_ln_b", "sgu_w_s", "sgu_b_s", "norm_ffn2",
         "norm_final")


G_FFN1 = ("ffn1_w_gate", "ffn1_w_up", "ffn1_w_down")
G_MIX = ("w_in", "w_branch_att", "w_branch_sgu", "w_out")
G_FFN2 = ("ffn2_w_gate", "ffn2_w_up", "ffn2_w_down")


def _local_step(x, target, wb, ws, dist=None):
    def gather_on(names):
        return _ag_payload([wb[n] for n in names]) if dist else None

    t2 = _relbias_fwd(ws["rel_bias"])
    bias2 = _bias_blocks(t2)
    bst = ws["sgu_b_s"].T
    w_st = jnp.swapaxes(ws["sgu_w_s"], 1, 2)

    if dist:
        wb.update(zip(G_FFN1, _call(lambda: None, gather_on(G_FFN1), name="allgather_ffn1", grid=None, in_specs=[],
                                    out_specs=[], out_shape=[])))
    x1, h1, a1, b1, *got = _ffn_fwd(x, ws["norm_ffn1"], wb["ffn1_w_gate"], wb["ffn1_w_up"], wb["ffn1_w_down"],
                                    "ffn1_fwd", gather_on(G_MIX))
    wb.update(zip(G_MIX, got))
    h2, qkv, zs, gl, y_sgu, *got = _in_fwd(x1, ws["norm_mix"], wb["w_in"], ws["sgu_ln_g"], ws["sgu_ln_b"],
                                           ws["sgu_w_s"], bst, gather_on(G_FFN2[0:1]))
    wb.update(zip(G_FFN2[0:1], got))
    y_att, *got = _att_fwd(qkv, bias2, gather_on(G_FFN2[1:2]))
    wb.update(zip(G_FFN2[1:2], got))
    x2, merged, pa, ps, *got = _merge_fwd(x1, y_att, y_sgu, gl, ws["b_gate"], wb["w_branch_att"], wb["w_branch_sgu"],
                                          wb["w_out"], gather_on(G_FFN2[2:3]))
    wb.update(zip(G_FFN2[2:3], got))
    dx3, h3, a3, b3, loss, g_final = _ffn_fwd(x2, ws["norm_ffn2"], wb["ffn2_w_gate"], wb["ffn2_w_up"],
                                              wb["ffn2_w_down"], "ffn2_fwd", head=(target, ws["norm_final"]))

    gb, gs, sums = {}, {"norm_final": g_final}, {}

    def pair_on(names, small=None):
        return _px_payload([gb[n] for n in names], small) if dist else None

    def pair_add(names, halves):
        for n, rv in zip(names, halves):
            sums[n] = _pair_add(gb[n], rv, dist[0], dist[1], "pair_add_" + n)

    def chips_on(names):
        return _cx_payload([sums[n][1] for n in names], [sums[n][2] for n in names]) if dist else None

    dx2, da3, db3, gs["norm_ffn2"] = _ffn_dgrad(dx3, x2, a3, b3, ws["norm_ffn2"], wb["ffn2_w_gate"], wb["ffn2_w_up"],
                                                wb["ffn2_w_down"], "ffn2_dgrad")
    gb["ffn2_w_gate"], gb["ffn2_w_up"], gb["ffn2_w_down"] = _ffn_wgrad(h3, dx3, a3, b3, da3, db3, "ffn2_wgrad")
    dy_att, dy_sgu, dgl, gs["b_gate"], gb["w_branch_att"], gb["w_branch_sgu"], gb["w_out"], *got = _merge_bwd(
        dx2, y_att, y_sgu, gl, merged, pa, ps, ws["b_gate"], wb["w_branch_att"], wb["w_branch_sgu"], wb["w_out"],
        pair_on(G_FFN2))
    pair_add(G_FFN2, got)
    dq, dk, dv, db2t, *lands2 = _att_bwd(qkv, dy_att, jnp.swapaxes(bias2, 1, 2), chips_on(G_FFN2))
    gs["rel_bias"] = _relbias_bwd(_unskew(jnp.swapaxes(db2t, 1, 2)))
    dx1, dz, gs["norm_mix"], gs["sgu_w_s"], dbt, gs["sgu_ln_g"], gs["sgu_ln_b"] = _in_dgrad(
        dx2, x1, ws["norm_mix"], wb["w_in"], dq, dk, dv, dgl, zs, dy_sgu, ws["sgu_ln_g"], ws["sgu_ln_b"], ws["sgu_w_s"],
        w_st, bst)
    gs["sgu_b_s"] = dbt.T
    gb["w_in"] = _in_wgrad(h2, dz)
    gx, da1, db1, gs["norm_ffn1"], *got = _ffn_dgrad(dx1, x, a1, b1, ws["norm_ffn1"], wb["ffn1_w_gate"],
                                                    wb["ffn1_w_up"], wb["ffn1_w_down"], "ffn1_dgrad", pair_on(G_MIX))
    pair_add(G_MIX, got)
    gb["ffn1_w_gate"], gb["ffn1_w_up"], gb["ffn1_w_down"], *lands_mix = _ffn_wgrad(h1, dx1, a1, b1, da1, db1,
                                                                                   "ffn1_wgrad", chips_on(G_MIX))
    if not dist:
        return loss, gx, gb, gs

    def final_sums(names, lands):
        return [_final_sum(sums[n][0], land, dist[1], dist[0], "final_sum_" + n) for n, land in zip(names, lands)]

    early = G_FFN2 + G_MIX
    k = len(G_FFN1)
    tail = _tail_reduce([gb[n] for n in G_FFN1], _pack_small(gs, loss),
                        _ss_payload(final_sums(early, list(lands2) + list(lands_mix))))
    for i, n in enumerate(G_FFN1):
        sums[n] = (tail[i],)
    small_sums, shared = tail[2 * k], dict(zip(early, tail[2 * k + 1:]))
    shared.update(zip(G_FFN1, _sibling_share(final_sums(G_FFN1, tail[k:2 * k]), "sibling_share")))
    return loss, gx, shared, small_sums


_ANY = pl.BlockSpec(memory_space=pl.ANY)
_VMEM = pl.BlockSpec(memory_space=pltpu.VMEM)


def _mesh_pos():
    return lax.axis_index("x"), lax.axis_index("y"), lax.axis_index("c")


def _cast_slots(shards, chip, name):
    n = len(shards)
    r, ncol = shards[0].shape
    tr = r // 2

    def body(me_ref, *refs):
        for i_ref, o_ref in zip(refs[:n], refs[n:]):
            o_ref[0] = i_ref[...].astype(BF16)

    grid_spec = pltpu.PrefetchScalarGridSpec(
        num_scalar_prefetch=1, grid=(r // tr,),
        in_specs=[pl.BlockSpec((tr, ncol), lambda i, me: (i, 0))] * n,
        out_specs=[pl.BlockSpec((1, tr, ncol), lambda i, me: (me[0], i, 0))] * n)
    return pl.pallas_call(
        body, name=name, grid_spec=grid_spec,
        out_shape=[jax.ShapeDtypeStruct((N_SHARD, r, ncol), BF16)] * n,
        compiler_params=_params(("arbitrary",)),
    )(chip, *shards)


class _Payload:
    def __init__(self, arrays, out_shapes, aliases, scratch, phases):
        self.arrays = list(arrays)
        self.out_shapes = list(out_shapes)
        self.aliases = dict(aliases)
        self.scratch = list(scratch)
        self.phases = phases


def _remote(src, dst, ssem, rsem, dev):
    return pltpu.make_async_remote_copy(src_ref=src, dst_ref=dst, send_sem=ssem, recv_sem=rsem, device_id=dev,
                                        device_id_type=MESH)


def _call(body, payload, *, name, grid, in_specs, out_specs, out_shape, scratch_shapes=(), sem=None, when=None,
          operands=(), takes_middle=False):
    in_specs, out_specs, out_shape = list(in_specs), list(out_specs), list(out_shape)
    scratch_shapes = list(scratch_shapes)
    n_in, n_out, n_scr = len(in_specs), len(out_specs), len(scratch_shapes)
    kwargs = {}
    kernel = body
    if payload is not None:
        k_in, k_out = len(payload.arrays), len(payload.out_shapes)
        rank = len(grid) if grid else 0

        def kernel(*refs):
            a, b = n_in, n_in + k_in
            c, d = b + n_out, b + n_out + k_out
            e = d + n_scr
            phases = payload.phases(refs[a:b], refs[c:d], refs[e:])

            def run():
                body(*refs[:a], *refs[b:c], *refs[d:e])

            if not grid:
                phases[0]()
                if len(phases) == 3 and takes_middle:
                    body(*refs[:a], *refs[b:c], *refs[d:e], middle=phases[1])
                    phases[2]()
                    return
                run()
                for ph in phases[1:]:
                    ph()
                return
            step = pl.program_id(0)
            if rank == 2:
                step = step * grid[1] + pl.program_id(1)
            marks = list(when)
            if len(phases) == 3:
                marks = [when[0], (max(when[1][0] - 3, 0), False), when[1]]
            for ph, (at, before) in zip(phases, marks):
                if before:
                    pl.when(step == at)(ph)
            run()
            for ph, (at, before) in zip(phases, marks):
                if not before:
                    pl.when(step == at)(ph)

        in_specs += [_ANY] * k_in
        out_specs += [_ANY] * k_out
        out_shape += payload.out_shapes
        scratch_shapes += payload.scratch
        kwargs["input_output_aliases"] = {n_in + i: n_out + j for i, j in payload.aliases.items()}
        operands = tuple(operands) + tuple(payload.arrays)
    if grid:
        kwargs["grid"] = grid
    return pl.pallas_call(kernel, name=name, in_specs=in_specs, out_specs=out_specs, out_shape=out_shape,
                          scratch_shapes=scratch_shapes, compiler_params=_params(sem), **kwargs)(*operands)


def _ag_payload(slots):
    n = len(slots)

    def phases(_, refs, sems):
        send_i, recv_i, send_d, recv_d = sems
        x, y, c = _mesh_pos()
        me = 2 * x + y

        def half(w, core):
            rh = slots[w].shape[1] // 2
            return pl.ds(core * rh, rh)

        def ici(w, j):
            t = (me + 1 + j) % N_SHARD
            mine = refs[w].at[me, half(w, c), :]
            return _remote(mine, mine, send_i.at[3 * w + j], recv_i.at[3 * w + j], (t // 2, t % 2, c))

        def d2d(w, j, core):
            s = (me + 3 - j) % N_SHARD
            land = refs[w].at[s, half(w, core), :]
            return _remote(land, land, send_d.at[3 * w + j], recv_d.at[3 * w + j], (x, y, 1 - c))

        def start():
            for w in range(n):
                for j in range(3):
                    ici(w, j).start()

        def forward():
            for w in range(n):
                for j in range(3):
                    s = (me + 3 - j) % N_SHARD
                    land = refs[w].at[s, half(w, c), :]
                    _remote(land, land, send_i.at[3 * w + j], recv_i.at[3 * w + j], (x, y, c)).wait_recv()
                    d2d(w, j, c).start()

        def finish():
            for w in range(n):
                for j in range(3):
                    d2d(w, j, 1 - c).wait_recv()
            for w in range(n):
                for j in range(3):
                    ici(w, j).wait_send()
                    d2d(w, j, c).wait_send()

        return [start, forward, finish]

    return _Payload(slots, [jax.ShapeDtypeStruct(s.shape, s.dtype) for s in slots], {i: i for i in range(n)},
                    [pltpu.SemaphoreType.DMA((3 * n,)) for _ in range(4)], phases)


def _px_payload(grads, small=None):
    arrays = list(grads) + ([small] if small is not None else [])
    n = len(arrays)

    def phases(ins, outs, sems):
        send, recv = sems
        x, y, c = _mesh_pos()

        def copy(w):
            if w < len(grads):
                rh = grads[w].shape[1] // 2
                src = ins[w].at[:, pl.ds((1 - c) * rh, rh), :]
            else:
                src = ins[w]
            return _remote(src, outs[w], send.at[w], recv.at[w], (x, y, 1 - c))

        def start():
            for w in range(n):
                copy(w).start()

        def finish():
            for w in range(n):
                copy(w).wait()

        return [start, finish]

    out_shapes = [jax.ShapeDtypeStruct((N_SHARD, g.shape[1] // 2, g.shape[2]), F32) for g in grads]
    if small is not None:
        out_shapes.append(jax.ShapeDtypeStruct(small.shape, F32))
    return _Payload(arrays, out_shapes, {}, [pltpu.SemaphoreType.DMA((n,)), pltpu.SemaphoreType.DMA((n,))], phases)


def _cx_payload(pbs, lands):
    n = len(pbs)

    def phases(ins, outs, sems):
        send, recv = sems
        x, y, c = _mesh_pos()
        me = 2 * x + y

        def copy(w, j):
            t = (me + 1 + j) % N_SHARD
            return _remote(ins[w].at[t], outs[w].at[me], send.at[3 * w + j], recv.at[3 * w + j], (t // 2, t % 2, c))

        def start():
            for w in range(n):
                for j in range(3):
                    copy(w, j).start()

        def finish():
            for w in range(n):
                for j in range(3):
                    copy(w, j).wait()

        return [start, finish]

    return _Payload(list(pbs) + list(lands), [jax.ShapeDtypeStruct(p.shape, BF16) for p in lands],
                    {n + i: i for i in range(n)},
                    [pltpu.SemaphoreType.DMA((3 * n,)), pltpu.SemaphoreType.DMA((3 * n,))], phases)


def _pair_add(g, rv, core, chip, name):
    _, r, ncol = g.shape
    rh = r // 2

    def body(c_ref, me_ref, g_ref, rv_ref, pf_ref, pb_ref, land_ref):
        s = g_ref[0] + rv_ref[0]
        sb = s.astype(BF16)
        pb_ref[0] = sb

        @pl.when(pl.program_id(0) == me_ref[0])
        def _():
            pf_ref[...] = s
            land_ref[0] = sb

    slot = pl.BlockSpec((1, rh, ncol), lambda s, c, me: (s, 0, 0))
    grid_spec = pltpu.PrefetchScalarGridSpec(
        num_scalar_prefetch=2, grid=(N_SHARD,),
        in_specs=[pl.BlockSpec((1, rh, ncol), lambda s, c, me: (s, c[0], 0)), slot],
        out_specs=[pl.BlockSpec((rh, ncol), lambda s, c, me: (0, 0)), slot,
                   pl.BlockSpec((1, rh, ncol), lambda s, c, me: (me[0], 0, 0))])
    return pl.pallas_call(
        body, name=name, grid_spec=grid_spec,
        out_shape=[jax.ShapeDtypeStruct((rh, ncol), F32), jax.ShapeDtypeStruct((N_SHARD, rh, ncol), BF16),
                   jax.ShapeDtypeStruct((N_SHARD, rh, ncol), BF16)],
        compiler_params=_params(("arbitrary",)),
    )(core, chip, g, rv)


def _tail_reduce(grads, small, payload=None):
    n = len(grads)
    _, r, ncol = grads[0].shape
    rh = r // 2

    def body(*refs, middle=None):
        g_hbm, sm = refs[:n], refs[n]
        pf, land, sm_out = refs[n + 1:2 * n + 1], refs[2 * n + 1:3 * n + 1], refs[3 * n + 1]
        scr = refs[3 * n + 2:]
        rv, mine, sendb = scr[:n], scr[n:2 * n], scr[2 * n:3 * n]
        sm_rv, sm_sum, d_send, d_recv, load, i_send, i_recv, store = scr[3 * n:]
        x, y, c = _mesh_pos()
        me = 2 * x + y
        sib = (x, y, 1 - c)

        def pair(w):
            src = g_hbm[w].at[:, pl.ds((1 - c) * rh, rh), :] if w < n else sm
            return _remote(src, rv[w] if w < n else sm_rv, d_send.at[w], d_recv.at[w], sib)

        def chips(w, j):
            t = (me + 1 + j) % N_SHARD
            src = sendb[w].at[t] if w < n else sm_sum
            dst = land[w].at[me] if w < n else sm_out.at[me]
            return _remote(src, dst, i_send.at[3 * w + j], i_recv.at[3 * w + j], (t // 2, t % 2, c))

        loads = [pltpu.make_async_copy(g_hbm[w].at[:, pl.ds(c * rh, rh), :], mine[w], load.at[w]) for w in range(n)]
        for w in range(n + 1):
            pair(w).start()
        for cp in loads:
            cp.start()
        stores = []
        for w in range(n):
            loads[w].wait()
            pair(w).wait_recv()
            for k in range(N_SHARD):
                s = mine[w][k] + rv[w][k]
                mine[w][k] = s
                sendb[w][k] = s.astype(BF16)
            stores += [pltpu.make_async_copy(mine[w].at[me], pf[w], store.at[2 * w]),
                       pltpu.make_async_copy(sendb[w].at[me], land[w].at[me], store.at[2 * w + 1])]
            for cp in stores[-2:]:
                cp.start()
            for j in range(3):
                chips(w, j).start()
        pair(n).wait_recv()
        if middle is not None:
            middle()
        sm_sum[...] = sm[...] + sm_rv[...]
        stores.append(pltpu.make_async_copy(sm_sum, sm_out.at[me], store.at[2 * n]))
        stores[-1].start()
        for j in range(3):
            chips(n, j).start()
        for w in range(n + 1):
            pair(w).wait_send()
            for j in range(3):
                chips(w, j).wait()
        for cp in stores:
            cp.wait()

    half = (N_SHARD, rh, ncol)
    return _call(
        body, payload, name="tail_reduce", grid=None, takes_middle=True,
        in_specs=[_ANY] * n + [_VMEM], out_specs=[_ANY] * (2 * n + 1),
        out_shape=([jax.ShapeDtypeStruct((rh, ncol), F32)] * n + [jax.ShapeDtypeStruct(half, BF16)] * n
                   + [jax.ShapeDtypeStruct((N_SHARD,) + small.shape, F32)]),
        scratch_shapes=([pltpu.VMEM(half, F32)] * (2 * n) + [pltpu.VMEM(half, BF16)] * n
                        + [pltpu.VMEM(small.shape, F32), pltpu.VMEM(small.shape, F32),
                           pltpu.SemaphoreType.DMA((n + 1,)), pltpu.SemaphoreType.DMA((n + 1,)),
                           pltpu.SemaphoreType.DMA((n,)), pltpu.SemaphoreType.DMA((3 * n + 3,)),
                           pltpu.SemaphoreType.DMA((3 * n + 3,)), pltpu.SemaphoreType.DMA((2 * n + 1,))]),
        operands=(*grads, small))


def _final_sum(pf, land, chip, core, name):
    _, rh, ncol = land.shape

    def body(me_ref, c_ref, pf_ref, land_ref, o_ref):
        me = me_ref[0]
        acc = jnp.zeros((rh, ncol), F32)
        for k in range(N_SHARD):
            acc = acc + jnp.where(me == k, pf_ref[...], land_ref[k].astype(F32))
        o_ref[...] = acc

    grid_spec = pltpu.PrefetchScalarGridSpec(
        num_scalar_prefetch=2, grid=(1,),
        in_specs=[pl.BlockSpec((rh, ncol), lambda i, me, c: (0, 0)),
                  pl.BlockSpec((N_SHARD, rh, ncol), lambda i, me, c: (0, 0, 0))],
        out_specs=pl.BlockSpec((rh, ncol), lambda i, me, c: (c[0], 0)))
    return pl.pallas_call(
        body, name=name, grid_spec=grid_spec, out_shape=jax.ShapeDtypeStruct((2 * rh, ncol), F32),
        compiler_params=_params(("arbitrary",)),
    )(chip, core, pf, land)


def _ss_payload(fulls):
    n = len(fulls)

    def phases(_, outs, sems):
        send, recv = sems
        x, y, c = _mesh_pos()

        def copy(w):
            rh = fulls[w].shape[0] // 2
            mine = outs[w].at[pl.ds(c * rh, rh), :]
            return _remote(mine, mine, send.at[w], recv.at[w], (x, y, 1 - c))

        def start():
            for w in range(n):
                copy(w).start()

        def finish():
            for w in range(n):
                copy(w).wait()

        return [lambda: None, start, finish]

    return _Payload(fulls, [jax.ShapeDtypeStruct(f.shape, F32) for f in fulls], {i: i for i in range(n)},
                    [pltpu.SemaphoreType.DMA((n,)), pltpu.SemaphoreType.DMA((n,))], phases)


def _sibling_share(fulls, name):
    return _call(lambda: None, _ss_payload(fulls), name=name, grid=None, in_specs=[], out_specs=[], out_shape=[])


_ROW = {"rel_bias": 128, "sgu_b_s": 136, "norm_ffn1": 144, "norm_mix": 145, "norm_ffn2": 146, "norm_final": 147,
        "b_gate": 148, "sgu_ln_g": 150, "sgu_ln_b": 151}


def _pack_small(gs, loss):
    def body(ws, rel, bs, n1, nm, n2, nf, bg, lg, lb, loss_ref, o_ref):
        o_ref[...] = jnp.zeros_like(o_ref)
        o_ref[LOSS_ROW:LOSS_ROW + 1, 0:128] = loss_ref[...]
        for g in range(SGU_GROUPS):
            o_ref[0:SGU_BLOCK, g * SGU_BLOCK:(g + 1) * SGU_BLOCK] = ws[g]
        o_ref[128:136, 0:REL_PAD] = rel[...]
        o_ref[136:144, 0:SGU_BLOCK] = bs[...]
        o_ref[144:145, :] = n1[...]
        o_ref[145:146, :] = nm[...]
        o_ref[146:147, :] = n2[...]
        o_ref[147:148, :] = nf[...]
        o_ref[148:149, :] = bg[:, 0:D_MODEL]
        o_ref[149:150, :] = bg[:, D_MODEL:2 * D_MODEL]
        o_ref[150:151, 0:D_SGU] = lg[...]
        o_ref[151:152, 0:D_SGU] = lb[...]

    order = ("sgu_w_s", "rel_bias", "sgu_b_s", "norm_ffn1", "norm_mix", "norm_ffn2", "norm_final", "b_gate", "sgu_ln_g",
             "sgu_ln_b")
    return pl.pallas_call(body, name="pack_small", out_shape=jax.ShapeDtypeStruct((SMALL_ROWS, D_MODEL), F32))(
        *[gs[k] for k in order], loss)


def _adam(w, g, m, v):
    m2 = ADAM_B1 * m + (1.0 - ADAM_B1) * g
    v2 = ADAM_B2 * v + (1.0 - ADAM_B2) * (g * g)
    m_hat = m2 / (1.0 - ADAM_B1 ** ADAM_STEP)
    v_hat = v2 / (1.0 - ADAM_B2 ** ADAM_STEP)
    delta = -ADAM_LR * (m_hat / (jnp.sqrt(v_hat) + ADAM_EPS) + ADAM_WD * w)
    return delta, m2, v2


def _adam_small(sin, w, m, v):
    names = SMALL
    k = len(names)

    def body(*refs):
        sin_ref = refs[0]
        w_r, m_r, v_r = refs[1:1 + k], refs[1 + k:1 + 2 * k], refs[1 + 2 * k:1 + 3 * k]
        outs = refs[1 + 3 * k:]
        tot = sin_ref[0] + sin_ref[1] + sin_ref[2] + sin_ref[3]
        outs[4 * k][...] = tot[LOSS_ROW:LOSS_ROW + 1, 0:128]
        for i, name in enumerate(names):
            o = outs[4 * i:4 * i + 4]
            if name == "sgu_w_s":
                for gi in range(SGU_GROUPS):
                    g = tot[0:SGU_BLOCK, gi * SGU_BLOCK:(gi + 1) * SGU_BLOCK]
                    res = (g,) + _adam(w_r[i][gi], g, m_r[i][gi], v_r[i][gi])
                    for ref, val in zip(o, res):
                        ref[gi] = val
                continue
            r0 = _ROW[name]
            if name == "rel_bias":
                g = tot[r0:r0 + HEADS, 0:REL_PAD]
            elif name == "sgu_b_s":
                g = tot[r0:r0 + SGU_GROUPS, 0:SGU_BLOCK]
            elif name == "b_gate":
                g = jnp.concatenate([tot[r0:r0 + 1, :], tot[r0 + 1:r0 + 2, :]], axis=1)
            elif name in ("sgu_ln_g", "sgu_ln_b"):
                g = tot[r0:r0 + 1, 0:D_SGU]
            else:
                g = tot[r0:r0 + 1, :]
            res = (g,) + _adam(w_r[i][...], g, m_r[i][...], v_r[i][...])
            for ref, val in zip(o, res):
                ref[...] = val

    out_shape = []
    for name in names:
        out_shape += [jax.ShapeDtypeStruct(w[name].shape, F32)] * 4
    out_shape.append(jax.ShapeDtypeStruct((1, 128), F32))
    flat = pl.pallas_call(body, name="adam_small", out_shape=out_shape, compiler_params=_params())(
        sin, *[w[n] for n in names], *[m[n] for n in names], *[v[n] for n in names])
    return {name: tuple(flat[4 * i:4 * i + 4]) for i, name in enumerate(names)}, flat[4 * k]


def _adam_big(w, g, m, v, name):
    r, ncol = w.shape
    tr = 256 if r % 256 == 0 else r // 2

    def body(w_ref, g_ref, m_ref, v_ref, g2_ref, d_ref, m2_ref, v2_ref):
        gv = g_ref[...]
        g2_ref[...] = gv
        d_ref[...], m2_ref[...], v2_ref[...] = _adam(w_ref[...], gv, m_ref[...], v_ref[...])

    spec = pl.BlockSpec((tr, ncol), lambda i: (i, 0))
    return pl.pallas_call(
        body, name=name, grid=(r // tr,), in_specs=[spec] * 4, out_specs=[spec] * 4,
        out_shape=[jax.ShapeDtypeStruct(w.shape, F32)] * 4, compiler_params=_params(("arbitrary",)),
    )(w, g, m, v)


WEIGHTS = ("norm_ffn1", "ffn1_w_gate", "ffn1_w_up", "ffn1_w_down", "norm_mix", "w_in", "b_gate", "rel_bias", "sgu_ln_g",
           "sgu_ln_b", "sgu_w_s", "sgu_b_s", "w_branch_att", "w_branch_sgu", "w_out", "norm_ffn2", "ffn2_w_gate",
           "ffn2_w_up", "ffn2_w_down", "norm_final")


GATE_UP = ("ffn1_w_gate", "ffn1_w_up", "ffn2_w_gate", "ffn2_w_up")
_FFN = ("ffn1_w_gate", "ffn1_w_up", "ffn1_w_down", "ffn2_w_gate", "ffn2_w_up", "ffn2_w_down")
_CAST_GROUPS = ((_FFN, "cast_ffn"), (("w_in",), "cast_w_in"), (("w_branch_att", "w_branch_sgu"), "cast_branch"),
                (("w_out",), "cast_w_out"))


def _big_form(name, a):
    return jnp.swapaxes(a, 1, 2)[0] if name in GATE_UP else a[0]


def _big_back(name, a):
    return jnp.swapaxes(a[None], 1, 2) if name in GATE_UP else a[None]


def _small_form(name, a):
    if name == "norm_final":
        return a.reshape(1, D_MODEL)
    if name == "rel_bias":
        return jnp.pad(a[0], ((0, 0), (0, REL_PAD - N_REL)))
    if name in ("sgu_w_s", "sgu_b_s"):
        return a[0]
    return a


def _small_back(name, a, like):
    if name == "rel_bias":
        a = a[:, :N_REL]
    return a.reshape(like.shape)


def kernel(x, norm_ffn1, ffn1_w_gate, ffn1_w_up, ffn1_w_down, norm_mix, w_in, b_gate, rel_bias, sgu_ln_g, sgu_ln_b, sgu_w_s, sgu_b_s, w_branch_att, w_branch_sgu, w_out, norm_ffn2, ffn2_w_gate, ffn2_w_up, ffn2_w_down, norm_final, loss_target, m_norm_ffn1, m_ffn1_w_gate, m_ffn1_w_up, m_ffn1_w_down, m_norm_mix, m_w_in, m_b_gate, m_rel_bias, m_sgu_ln_g, m_sgu_ln_b, m_sgu_w_s, m_sgu_b_s, m_w_branch_att, m_w_branch_sgu, m_w_out, m_norm_ffn2, m_ffn2_w_gate, m_ffn2_w_up, m_ffn2_w_down, m_norm_final, v_norm_ffn1, v_ffn1_w_gate, v_ffn1_w_up, v_ffn1_w_down, v_norm_mix, v_w_in, v_b_gate, v_rel_bias, v_sgu_ln_g, v_sgu_ln_b, v_sgu_w_s, v_sgu_b_s, v_w_branch_att, v_w_branch_sgu, v_w_out, v_norm_ffn2, v_ffn2_w_gate, v_ffn2_w_up, v_ffn2_w_down, v_norm_final):
    w = dict(norm_ffn1=norm_ffn1, ffn1_w_gate=ffn1_w_gate, ffn1_w_up=ffn1_w_up, ffn1_w_down=ffn1_w_down, norm_mix=norm_mix,
             w_in=w_in, b_gate=b_gate, rel_bias=rel_bias, sgu_ln_g=sgu_ln_g, sgu_ln_b=sgu_ln_b, sgu_w_s=sgu_w_s,
             sgu_b_s=sgu_b_s, w_branch_att=w_branch_att, w_branch_sgu=w_branch_sgu, w_out=w_out, norm_ffn2=norm_ffn2,
             ffn2_w_gate=ffn2_w_gate, ffn2_w_up=ffn2_w_up, ffn2_w_down=ffn2_w_down, norm_final=norm_final)
    m = dict(norm_ffn1=m_norm_ffn1, ffn1_w_gate=m_ffn1_w_gate, ffn1_w_up=m_ffn1_w_up, ffn1_w_down=m_ffn1_w_down,
             norm_mix=m_norm_mix, w_in=m_w_in, b_gate=m_b_gate, rel_bias=m_rel_bias, sgu_ln_g=m_sgu_ln_g,
             sgu_ln_b=m_sgu_ln_b, sgu_w_s=m_sgu_w_s, sgu_b_s=m_sgu_b_s, w_branch_att=m_w_branch_att,
             w_branch_sgu=m_w_branch_sgu, w_out=m_w_out, norm_ffn2=m_norm_ffn2, ffn2_w_gate=m_ffn2_w_gate,
             ffn2_w_up=m_ffn2_w_up, ffn2_w_down=m_ffn2_w_down, norm_final=m_norm_final)
    v = dict(norm_ffn1=v_norm_ffn1, ffn1_w_gate=v_ffn1_w_gate, ffn1_w_up=v_ffn1_w_up, ffn1_w_down=v_ffn1_w_down,
             norm_mix=v_norm_mix, w_in=v_w_in, b_gate=v_b_gate, rel_bias=v_rel_bias, sgu_ln_g=v_sgu_ln_g,
             sgu_ln_b=v_sgu_ln_b, sgu_w_s=v_sgu_w_s, sgu_b_s=v_sgu_b_s, w_branch_att=v_w_branch_att,
             w_branch_sgu=v_w_branch_sgu, w_out=v_w_out, norm_ffn2=v_norm_ffn2, ffn2_w_gate=v_ffn2_w_gate,
             ffn2_w_up=v_ffn2_w_up, ffn2_w_down=v_ffn2_w_down, norm_final=v_norm_final)

    core = lax.axis_index("c").astype(jnp.int32).reshape(1)
    chip = (2 * lax.axis_index("x") + lax.axis_index("y")).astype(jnp.int32).reshape(1)

    wk = {n: _big_form(n, w[n]) for n in BIG}
    slots = {}
    for names, call in _CAST_GROUPS:
        slots.update(zip(names, _cast_slots([wk[n] for n in names], chip, call)))
    ws = {n: _small_form(n, w[n]) for n in SMALL}
    _, gx, shard_grads, small_sums = _local_step(x[0], loss_target[0], slots, ws, (core, chip))

    small, loss = _adam_small(small_sums, ws, {n: _small_form(n, m[n]) for n in SMALL},
                              {n: _small_form(n, v[n]) for n in SMALL})
    grad, delta, new_m, new_v = {}, {}, {}, {}
    for n in SMALL:
        grad[n], delta[n], new_m[n], new_v[n] = (_small_back(n, a, w[n]) for a in small[n])
    for n in BIG:
        g2, d2, m2, v2 = _adam_big(wk[n], shard_grads[n], _big_form(n, m[n]), _big_form(n, v[n]), "adam_" + n)
        grad[n], delta[n], new_m[n], new_v[n] = (_big_back(n, a) for a in (g2, d2, m2, v2))

    return (loss[0, 0], gx.reshape(x.shape), *[grad[n] for n in WEIGHTS], *[delta[n] for n in WEIGHTS],
            *[new_m[n] for n in WEIGHTS], *[new_v[n] for n in WEIGHTS])
```
